```python
import math
import jax, jax.numpy as jnp
from jax import lax
import numpy as np

D_MODEL = 1024
BATCH = 32
SEQ = 2048
DEPTH = 1

META_TOKENS = 16
MIX_WIDTH = D_MODEL
CONV_WIDTH = MIX_WIDTH // 2
CONV_K = 3
GDN_HEADS = 4
GDN_HEAD_DIM = 128
GDN_WIDTH = GDN_HEADS * GDN_HEAD_DIM
GDN_CONV_K = 4
CHUNK = 64
IN_COLS = 3 * CONV_WIDTH + 4 * GDN_WIDTH + 2 * GDN_HEADS
N_EXPERTS = 256
TOP_K = 8
N_GROUPS = 8
TOPK_GROUPS = 4
E_PER_GROUP = N_EXPERTS // N_GROUPS
EXPERT_FF = 256
SHARED_FF = 256
ROUTED_SCALE = 2.5
MOE_BLOCK = 256
DN_ALPHA = (2 * DEPTH) ** 0.25
DN_BETA = (8 * DEPTH) ** -0.25
NORM_EPS = 1e-5

kernel_name = 'hymba_conv_gdn_moe_deepnorm'


def layer_norm(x, g, b):
    xf = x.astype(jnp.float32)
    mu = jnp.mean(xf, -1, keepdims=True)
    var = jnp.mean(jnp.square(xf - mu), -1, keepdims=True)
    y = (xf - mu) * lax.rsqrt(var + NORM_EPS) * g.astype(jnp.float32) + b.astype(jnp.float32)
    return y.astype(x.dtype)


def rms_norm(x, g):
    xf = x.astype(jnp.float32)
    y = xf * lax.rsqrt(jnp.mean(jnp.square(xf), -1, keepdims=True) + NORM_EPS) * g.astype(jnp.float32)
    return y.astype(x.dtype)


def l2_normalize(x):
    xf = x.astype(jnp.float32)
    return xf * lax.rsqrt(jnp.sum(jnp.square(xf), -1, keepdims=True) + 1e-6)


def causal_depthwise_conv(u, w):
    K = w.shape[0]
    L = u.shape[1]
    up = jnp.pad(u, ((0, 0), (K - 1, 0), (0, 0)))
    y = up[:, 0:L] * w[0]
    for j in range(1, K):
        y = y + up[:, j:j + L] * w[j]
    return y


def chunked_gated_delta_rule(q, k, v, g, beta):
    bsz, L, H, dk = q.shape
    dv = v.shape[-1]
    pad = (-L) % CHUNK
    n = (L + pad) // CHUNK

    def chunks(t):
        t = jnp.pad(t.astype(jnp.float32), ((0, 0), (pad, 0)) + ((0, 0),) * (t.ndim - 2))
        t = t.reshape((bsz, n, CHUNK) + t.shape[2:])
        return jnp.moveaxis(t, (1, 3), (0, 2))

    q, k, v, g, beta = chunks(q), chunks(k), chunks(v), chunks(g), chunks(beta)
    gc = jnp.cumsum(g, axis=-1)
    causal = jnp.tril(jnp.ones((CHUNK, CHUNK), dtype=bool))
    decay = jnp.exp(jnp.where(causal, gc[..., :, None] - gc[..., None, :], -jnp.inf))
    k_beta = k * beta[..., None]
    lower = jnp.tril(jnp.einsum('nbhid,nbhjd->nbhij', k_beta, k) * decay, -1)
    a_mat = jnp.eye(CHUNK, dtype=jnp.float32) + lower
    rhs = jnp.concatenate([v * beta[..., None], k_beta * jnp.exp(gc)[..., None]], axis=-1)
    sol = lax.linalg.triangular_solve(a_mat, rhs, left_side=True, lower=True, unit_diagonal=True)
    u, w = sol[..., :dv], sol[..., dv:]
    attn = jnp.einsum('nbhid,nbhjd->nbhij', q, k) * decay
    q_dec = q * jnp.exp(gc)[..., None]
    g_last = gc[..., -1]
    k_dec = k * jnp.exp(g_last[..., None] - gc)[..., None]

    def step(S, xs):
        q_n, k_n, u_n, w_n, a_n, gl_n = xs
        v_new = u_n - jnp.einsum('bhck,bhkv->bhcv', w_n, S)
        o_n = jnp.einsum('bhck,bhkv->bhcv', q_n, S) + jnp.einsum('bhij,bhjv->bhiv', a_n, v_new)
        S = S * jnp.exp(gl_n)[..., None, None] + jnp.einsum('bhck,bhcv->bhkv', k_n, v_new)
        return S, o_n

    S0 = jnp.zeros((bsz, H, dk, dv), jnp.float32)
    _, o = lax.scan(step, S0, (q_dec, k_dec, u, w, attn, g_last))
    o = jnp.moveaxis(o, (0, 2), (1, 3)).reshape(bsz, n * CHUNK, H, dv)
    return o[:, pad:]


def hybrid_mixer(h, w_in, conv_w, conv_norm_w, gdn_conv_w, a_log, dt_bias, gdn_norm_w, w_out):
    bsz, L, _ = h.shape
    c, gw = CONV_WIDTH, GDN_WIDTH
    cuts = [c, 2 * c, 3 * c, 3 * c + 3 * gw, 3 * c + 4 * gw, 3 * c + 4 * gw + GDN_HEADS]
    gate_b, gate_c, conv_in, qkv, z, beta_in, decay_in = jnp.split(h @ w_in, cuts, axis=-1)
    y_conv = gate_b * causal_depthwise_conv(gate_c * conv_in, conv_w)
    y_conv = rms_norm(y_conv, conv_norm_w)
    qkv = jax.nn.silu(causal_depthwise_conv(qkv, gdn_conv_w))
    qkv = qkv.reshape(bsz, L, 3, GDN_HEADS, GDN_HEAD_DIM)
    q = l2_normalize(qkv[:, :, 0]) * (GDN_HEAD_DIM ** -0.5)
    k = l2_normalize(qkv[:, :, 1])
    v = qkv[:, :, 2]
    beta = jax.nn.sigmoid(beta_in.astype(jnp.float32))
    g = -jnp.exp(a_log.astype(jnp.float32)) * jax.nn.softplus(decay_in.astype(jnp.float32) + dt_bias.astype(jnp.float32))
    o = chunked_gated_delta_rule(q, k, v, g, beta)
    o = rms_norm(o, gdn_norm_w) * jax.nn.silu(z.reshape(bsz, L, GDN_HEADS, GDN_HEAD_DIM).astype(jnp.float32))
    y_gdn = o.reshape(bsz, L, GDN_WIDTH).astype(h.dtype)
    return jnp.concatenate([y_conv, y_gdn], axis=-1) @ w_out


def moe_ffn(h, w_router, b_router, w_gate, w_up, w_down, ws_gate, ws_up, ws_down):
    bsz, L, D = h.shape
    x2d = h.reshape(-1, D)
    T = x2d.shape[0]
    scores = jax.nn.sigmoid((x2d @ w_router).astype(jnp.float32))
    sel = scores + b_router.astype(jnp.float32)
    grp_score = jnp.sum(lax.top_k(sel.reshape(T, N_GROUPS, E_PER_GROUP), 2)[0], axis=-1)
    _, top_g = lax.top_k(grp_score, TOPK_GROUPS)
    gmask = jnp.sum(jax.nn.one_hot(top_g, N_GROUPS, dtype=jnp.float32), axis=1) > 0
    sel = jnp.where(jnp.repeat(gmask, E_PER_GROUP, axis=1), sel, -jnp.inf)
    _, idx = lax.top_k(sel, TOP_K)
    gate = jnp.take_along_axis(scores, idx, axis=1)
    gate = gate / jnp.sum(gate, -1, keepdims=True) * ROUTED_SCALE
    TK = T * TOP_K
    flat_e = idx.reshape(TK)
    flat_t = jnp.repeat(jnp.arange(T, dtype=jnp.int32), TOP_K)
    flat_g = gate.reshape(TK)
    order = jnp.argsort(flat_e)
    e_sorted = flat_e[order]
    counts = jnp.bincount(flat_e, length=N_EXPERTS)
    pcounts = (counts + MOE_BLOCK - 1) // MOE_BLOCK * MOE_BLOCK
    pends = jnp.cumsum(pcounts)
    pstarts = pends - pcounts
    starts = jnp.cumsum(counts) - counts
    dest = pstarts[e_sorted] + jnp.arange(TK, dtype=jnp.int32) - starts[e_sorted]
    n_blocks = -(-TK // MOE_BLOCK) + N_EXPERTS
    slot_tok = jnp.zeros((n_blocks * MOE_BLOCK,), jnp.int32).at[dest].set(flat_t[order])
    slot_gate = jnp.zeros((n_blocks * MOE_BLOCK,), jnp.float32).at[dest].set(flat_g[order])
    block_e = jnp.minimum(jnp.searchsorted(pends, jnp.arange(n_blocks, dtype=jnp.int32) * MOE_BLOCK, side='right'), N_EXPERTS - 1)

    def body(out, blk):
        tok, gw, e = blk
        xb = x2d[tok]
        hb = jax.nn.silu(xb @ w_gate[e]) * (xb @ w_up[e])
        yb = (hb @ w_down[e]) * gw[:, None].astype(x2d.dtype)
        return out.at[tok].add(yb), None

    routed, _ = lax.scan(body, jnp.zeros_like(x2d),
                         (slot_tok.reshape(n_blocks, MOE_BLOCK), slot_gate.reshape(n_blocks, MOE_BLOCK), block_e))
    shared = (jax.nn.silu(x2d @ ws_gate) * (x2d @ ws_up)) @ ws_down
    return (routed + shared).reshape(bsz, L, D)


def setup_inputs(seed: int = 0) -> dict:
    key = jax.random.key(seed)
    ks = jax.random.split(key, 24)

    def nrm(k, shape, scale):
        return jax.random.normal(k, shape, jnp.float32) * scale

    x = nrm(ks[0], (BATCH, SEQ, D_MODEL), 1.0)
    meta_tokens = nrm(ks[1], (META_TOKENS, D_MODEL), 1.0)
    col_scale = jnp.concatenate([
        jnp.ones((2 * CONV_WIDTH,), jnp.float32), jnp.full((CONV_WIDTH,), DN_BETA, jnp.float32),
        jnp.ones((2 * GDN_WIDTH,), jnp.float32), jnp.full((GDN_WIDTH,), DN_BETA, jnp.float32),
        jnp.ones((GDN_WIDTH + 2 * GDN_HEADS,), jnp.float32)])
    w_in = nrm(ks[2], (DEPTH, D_MODEL, IN_COLS), D_MODEL ** -0.5) * col_scale
    conv_w = nrm(ks[3], (DEPTH, CONV_K, CONV_WIDTH), CONV_K ** -0.5)
    conv_norm_w = 1.0 + nrm(ks[4], (DEPTH, CONV_WIDTH), 0.01)
    gdn_conv_w = nrm(ks[5], (DEPTH, GDN_CONV_K, 3 * GDN_WIDTH), GDN_CONV_K ** -0.5)
    a_log = jnp.log(jax.random.uniform(ks[6], (DEPTH, GDN_HEADS), jnp.float32, 1.0, 16.0))
    dt = jnp.exp(jax.random.uniform(ks[7], (DEPTH, GDN_HEADS), jnp.float32, math.log(1e-3), math.log(1e-1)))
    dt_bias = dt + jnp.log(-jnp.expm1(-dt))
    gdn_norm_w = 1.0 + nrm(ks[8], (DEPTH, GDN_HEAD_DIM), 0.01)
    w_out = nrm(ks[9], (DEPTH, MIX_WIDTH, D_MODEL), MIX_WIDTH ** -0.5 * DN_BETA)
    ln1_g = 1.0 + nrm(ks[10], (DEPTH, D_MODEL), 0.01)
    ln1_b = nrm(ks[11], (DEPTH, D_MODEL), 0.01)
    w_router = nrm(ks[12], (DEPTH, D_MODEL, N_EXPERTS), D_MODEL ** -0.5)
    b_router = nrm(ks[13], (DEPTH, N_EXPERTS), 0.01)
    w_gate = nrm(ks[14], (DEPTH, N_EXPERTS, D_MODEL, EXPERT_FF), D_MODEL ** -0.5)
    w_up = nrm(ks[15], (DEPTH, N_EXPERTS, D_MODEL, EXPERT_FF), D_MODEL ** -0.5)
    w_down = nrm(ks[16], (DEPTH, N_EXPERTS, EXPERT_FF, D_MODEL), EXPERT_FF ** -0.5 * DN_BETA)
    ws_gate = nrm(ks[17], (DEPTH, D_MODEL, SHARED_FF), D_MODEL ** -0.5)
    ws_up = nrm(ks[18], (DEPTH, D_MODEL, SHARED_FF), D_MODEL ** -0.5)
    ws_down = nrm(ks[19], (DEPTH, SHARED_FF, D_MODEL), SHARED_FF ** -0.5 * DN_BETA)
    ln2_g = 1.0 + nrm(ks[20], (DEPTH, D_MODEL), 0.01)
    ln2_b = nrm(ks[21], (DEPTH, D_MODEL), 0.01)
    return {'x': x, 'meta_tokens': meta_tokens, 'w_in': w_in, 'conv_w': conv_w, 'conv_norm_w': conv_norm_w,
            'gdn_conv_w': gdn_conv_w, 'a_log': a_log, 'dt_bias': dt_bias, 'gdn_norm_w': gdn_norm_w,
            'w_out': w_out, 'ln1_g': ln1_g, 'ln1_b': ln1_b, 'w_router': w_router, 'b_router': b_router,
            'w_gate': w_gate, 'w_up': w_up, 'w_down': w_down, 'ws_gate': ws_gate, 'ws_up': ws_up,
            'ws_down': ws_down, 'ln2_g': ln2_g, 'ln2_b': ln2_b}


def reference(x, meta_tokens, w_in, conv_w, conv_norm_w, gdn_conv_w, a_log, dt_bias, gdn_norm_w,
              w_out, ln1_g, ln1_b, w_router, b_router, w_gate, w_up, w_down, ws_gate, ws_up,
              ws_down, ln2_g, ln2_b):
    bsz = x.shape[0]
    meta = jnp.broadcast_to(meta_tokens[None].astype(x.dtype), (bsz, META_TOKENS, x.shape[-1]))
    h = jnp.concatenate([meta, x], axis=1)
    for l in range(DEPTH):
        mix = hybrid_mixer(h, w_in[l], conv_w[l], conv_norm_w[l], gdn_conv_w[l], a_log[l], dt_bias[l],
                           gdn_norm_w[l], w_out[l])
        h = layer_norm(DN_ALPHA * h + mix, ln1_g[l], ln1_b[l])
        ffn = moe_ffn(h, w_router[l], b_router[l], w_gate[l], w_up[l], w_down[l], ws_gate[l], ws_up[l], ws_down[l])
        h = layer_norm(DN_ALPHA * h + ffn, ln2_g[l], ln2_b[l])
    return h[:, META_TOKENS:]
```

```python
import functools

import jax
import jax.numpy as jnp
from jax import lax
from jax.experimental import pallas as pl
from jax.experimental.pallas import tpu as pltpu

_F32 = jnp.float32
_BF16 = jnp.bfloat16
_I32 = jnp.int32

D_MODEL = 1024
N_META = 16
CONV_WIDTH = 512
CONV_K = 3
GDN_HEADS = 4
GDN_HEAD_DIM = 128
GDN_WIDTH = GDN_HEADS * GDN_HEAD_DIM
GDN_CONV_K = 4
CHUNK = 64
N_EXPERTS = 256
TOP_K = 8
N_GROUPS = 8
TOPK_GROUPS = 4
E_PER_GROUP = N_EXPERTS // N_GROUPS
EXPERT_FF = 256
ROUTED_SCALE = 2.5
ROW_BLOCK = 256
DN_ALPHA = 2.0 ** 0.25
NORM_EPS = 1e-5
HALF = D_MODEL // 2
STACK = GDN_HEADS * CHUNK
HIST = 8

V7X_VMEM_BYTES = 64 * 1024 * 1024
VMEM_LIMIT = V7X_VMEM_BYTES - 8 * 1024 * 1024


def _cparams(*sem):
    return pltpu.CompilerParams(dimension_semantics=sem, vmem_limit_bytes=VMEM_LIMIT)


def _mm(a, b):
    return jnp.dot(a.astype(_BF16), b.astype(_BF16), preferred_element_type=_F32)


def _mm_nt(a, b):
    return lax.dot_general(a.astype(_BF16), b.astype(_BF16), (((1,), (1,)), ((), ())),
                           preferred_element_type=_F32)


def _mm_tn(a, b):
    return lax.dot_general(a.astype(_BF16), b.astype(_BF16), (((0,), (0,)), ((), ())),
                           preferred_element_type=_F32)


def _sigmoid(x):
    return 1.0 / (1.0 + jnp.exp(-x))


def _silu(x):
    return x * _sigmoid(x)


def _softplus(x):
    return jnp.maximum(x, 0.0) + jnp.log1p(jnp.exp(-jnp.abs(x)))


def _pack_halves(y):
    return pltpu.pack_elementwise([y[:, :HALF], y[:, HALF:]], packed_dtype=_BF16)


def _unpack_halves(p):
    lo = pltpu.unpack_elementwise(p, index=0, packed_dtype=_BF16, unpacked_dtype=_F32)
    hi = pltpu.unpack_elementwise(p, index=1, packed_dtype=_BF16, unpacked_dtype=_F32)
    return lo, hi


def _layer_norm(h, g, b):
    mu = jnp.mean(h, axis=-1, keepdims=True)
    d = h - mu
    var = jnp.mean(d * d, axis=-1, keepdims=True)
    return d * lax.rsqrt(var + NORM_EPS) * g + b


def _premix_body(x_ref, tails_ref, wa_ref, wq_ref, wz_ref, wbd_ref, wbdt_ref, cw_ref, cnw_ref,
                 gcw_ref, prow_ref, pcol_ref,
                 yc_ref, q_ref, k_ref, v_ref, z_ref, bgc_ref, bgr_ref, tout_ref, ext_ref, *, lt):
    cw_ = CONV_WIDTH

    @pl.when(pl.program_id(1) == 0)
    def _():
        ext_ref[0:HIST, :] = tails_ref[...]

    xb = x_ref[0].astype(_BF16)
    pa = jnp.dot(xb, wa_ref[...], preferred_element_type=_F32)
    gate_b = pa[:, 0:cw_]
    u = pa[:, cw_:2 * cw_] * pa[:, 2 * cw_:3 * cw_]
    ext_ref[HIST:HIST + lt, 0:cw_] = u
    pq = jnp.dot(xb, wq_ref[...], preferred_element_type=_F32)
    ext_ref[HIST:HIST + lt, cw_:] = pq

    cw = cw_ref[...]
    ca = u * cw[CONV_K - 1:CONV_K, :]
    for j in range(CONV_K - 1):
        ca = ca + ext_ref[pl.ds(HIST - (CONV_K - 1) + j, lt), 0:cw_] * cw[j:j + 1, :]
    yc = gate_b * ca
    ms = jnp.mean(yc * yc, axis=-1, keepdims=True)
    yc_ref[0] = (yc * lax.rsqrt(ms + NORM_EPS) * cnw_ref[...]).astype(_BF16)

    gcw = gcw_ref[...]
    cq = pq * gcw[GDN_CONV_K - 1:GDN_CONV_K, :]
    for j in range(GDN_CONV_K - 1):
        cq = cq + ext_ref[pl.ds(HIST - (GDN_CONV_K - 1) + j, lt), cw_:] * gcw[j:j + 1, :]
    s = _silu(cq)
    for h in range(GDN_HEADS):
        lo, hi = h * GDN_HEAD_DIM, (h + 1) * GDN_HEAD_DIM
        qh = s[:, lo:hi]
        kh = s[:, GDN_WIDTH + lo:GDN_WIDTH + hi]
        qn = qh * lax.rsqrt(jnp.sum(qh * qh, axis=-1, keepdims=True) + 1e-6)
        kn = kh * lax.rsqrt(jnp.sum(kh * kh, axis=-1, keepdims=True) + 1e-6)
        q_ref[0, :, lo:hi] = (qn * (GDN_HEAD_DIM ** -0.5)).astype(_BF16)
        k_ref[0, :, lo:hi] = kn.astype(_BF16)
    v_ref[0] = s[:, 2 * GDN_WIDTH:].astype(_BF16)
    z_ref[0] = jnp.dot(xb, wz_ref[...], preferred_element_type=_F32).astype(_BF16)

    bdc = jnp.dot(xb, wbd_ref[...], preferred_element_type=_F32)
    prow = prow_ref[...]
    g_c = -jnp.exp(prow[0:1, :]) * _softplus(bdc + prow[1:2, :])
    lane = lax.broadcasted_iota(_I32, bdc.shape, 1)
    bgc_ref[0] = jnp.where(lane < GDN_HEADS, _sigmoid(bdc), g_c)
    bdr = _mm_nt(wbdt_ref[...], xb)
    pcol = pcol_ref[...]
    g_r = -jnp.exp(pcol[:, 0:1]) * _softplus(bdr + pcol[:, 1:2])
    row = lax.broadcasted_iota(_I32, bdr.shape, 0)
    bgr_ref[0] = jnp.where(row < GDN_HEADS, _sigmoid(bdr), g_r)

    tail = ext_ref[lt:lt + HIST, :]
    ext_ref[0:HIST, :] = tail
    tout_ref[0] = tail


def _premix(x, tails, wts, *, lt):
    bsz, seq, d = x.shape
    assert seq % lt == 0
    grid = (bsz, seq // lt)
    full = lambda a: pl.BlockSpec(a.shape, lambda b, j: (0,) * a.ndim)
    tok = lambda w: pl.BlockSpec((1, lt, w), lambda b, j: (b, j, 0))
    (wa, wq, wz, wbd, wbdt, cw, cnw, gcw, prow, pcol) = wts
    ext_w = CONV_WIDTH + 3 * GDN_WIDTH
    out_shape = (
        jax.ShapeDtypeStruct((bsz, seq, CONV_WIDTH), _BF16),
        jax.ShapeDtypeStruct((bsz, seq, GDN_WIDTH), _BF16),
        jax.ShapeDtypeStruct((bsz, seq, GDN_WIDTH), _BF16),
        jax.ShapeDtypeStruct((bsz, seq, GDN_WIDTH), _BF16),
        jax.ShapeDtypeStruct((bsz, seq, GDN_WIDTH), _BF16),
        jax.ShapeDtypeStruct((bsz, seq, 128), _F32),
        jax.ShapeDtypeStruct((bsz, 8, seq), _F32),
        jax.ShapeDtypeStruct((bsz, HIST, ext_w), _F32),
    )
    out_specs = (tok(CONV_WIDTH), tok(GDN_WIDTH), tok(GDN_WIDTH), tok(GDN_WIDTH), tok(GDN_WIDTH),
                 tok(128), pl.BlockSpec((1, 8, lt), lambda b, j: (b, 0, j)),
                 pl.BlockSpec((1, HIST, ext_w), lambda b, j: (b, 0, 0)))
    return pl.pallas_call(
        functools.partial(_premix_body, lt=lt),
        grid=grid,
        in_specs=[tok(d), full(tails)] + [full(w) for w in wts],
        out_specs=out_specs,
        out_shape=out_shape,
        scratch_shapes=[pltpu.VMEM((HIST + lt, ext_w), _F32)],
        compiler_params=_cparams("arbitrary", "arbitrary"),
        name="premix",
    )(x, tails, *wts)


def _cumsum_rows(x):
    row = lax.broadcasted_iota(_I32, x.shape, 0)
    s = 1
    while s < x.shape[0]:
        x = x + jnp.where(row >= s, pltpu.roll(x, s, 0), 0.0)
        s *= 2
    return x


def _cumsum_lanes_seg(x):
    lane = lax.broadcasted_iota(_I32, x.shape, 1) & (CHUNK - 1)
    s = 1
    while s < CHUNK:
        x = x + jnp.where(lane >= s, pltpu.roll(x, s, 1), 0.0)
        s *= 2
    return x


def _stack_heads(a):
    return jnp.concatenate([a[:, h * GDN_HEAD_DIM:(h + 1) * GDN_HEAD_DIM] for h in range(GDN_HEADS)], axis=0)


def _gdn_body(q_ref, k_ref, v_ref, z_ref, bgc_ref, grow_ref, s0_ref, gnw_ref,
              y_ref, sout_ref, s_ref, *, nc):
    @pl.when(pl.program_id(1) == 0)
    def _():
        s_ref[...] = s0_ref[...]

    ri = lax.broadcasted_iota(_I32, (STACK, STACK), 0)
    ci = lax.broadcasted_iota(_I32, (STACK, STACK), 1)
    same64 = (ri >> 6) == (ci >> 6)
    same32 = (ri >> 5) == (ci >> 5)
    same16 = (ri >> 4) == (ci >> 4)
    low_incl = same64 & (ri >= ci)
    low_strict = same64 & (ri > ci)
    gnw = gnw_ref[...]

    def chunk(c, carry):
        off = pl.multiple_of(c * CHUNK, CHUNK)
        q_all = _stack_heads(q_ref[0, pl.ds(off, CHUNK), :].astype(_F32))
        k_all = _stack_heads(k_ref[0, pl.ds(off, CHUNK), :].astype(_F32))
        v_all = _stack_heads(v_ref[0, pl.ds(off, CHUNK), :].astype(_F32))
        bgc = bgc_ref[0, pl.ds(off, CHUNK), :]
        gcs = _cumsum_rows(bgc)
        hd = (CHUNK, GDN_HEAD_DIM)
        beta_b = jnp.concatenate(
            [jnp.broadcast_to(bgc[:, h:h + 1], hd) for h in range(GDN_HEADS)], axis=0)
        gc_b = jnp.concatenate(
            [jnp.broadcast_to(gcs[:, GDN_HEADS + h:GDN_HEADS + h + 1], hd) for h in range(GDN_HEADS)], axis=0)
        gl = [gcs[CHUNK - 1:CHUNK, GDN_HEADS + h:GDN_HEADS + h + 1] for h in range(GDN_HEADS)]
        gl_b = jnp.concatenate([jnp.broadcast_to(g1, hd) for g1 in gl], axis=0)
        gcr = _cumsum_lanes_seg(jnp.broadcast_to(grow_ref[0, c], (8, STACK)))[0:1, :]

        diff = jnp.concatenate([gc_b, gc_b], axis=1) - gcr
        decay = jnp.exp(jnp.where(low_incl, diff, -1e30))
        kb = k_all * beta_b
        a1 = _mm_nt(jnp.concatenate([kb, q_all], axis=0), k_all)
        m = jnp.where(low_strict, a1[:STACK] * decay, 0.0)
        attn = a1[STACK:] * decay

        l16 = jnp.where(same16, m, 0.0)
        c1 = jnp.where(same32 & jnp.logical_not(same16), m, 0.0)
        c2 = jnp.where(same32, 0.0, m)
        p2 = _mm(l16, l16)
        p4 = _mm(p2, p2)
        p8 = _mm(p4, p4)
        na = p2 - l16 - _mm(l16, p2)
        nb = na + p4 + _mm(na, p4)
        ncm = nb + p8 + _mm(nb, p8)
        y1 = c1 + _mm(c1, ncm)
        n1 = ncm - y1 - _mm(ncm, y1)
        y2 = c2 + _mm(c2, n1)
        nt = n1 - y2 - _mm(n1, y2)

        egc = jnp.exp(gc_b)
        rhs = jnp.concatenate([v_all * beta_b, kb * egc], axis=1)
        uw = rhs + _mm(nt, rhs)
        u_all = uw[:, :GDN_HEAD_DIM]
        w_all = uw[:, GDN_HEAD_DIM:]
        qd = q_all * egc
        kd = k_all * jnp.exp(gl_b - gc_b)

        vn, qs = [], []
        for h in range(GDN_HEADS):
            r0, r1 = h * CHUNK, (h + 1) * CHUNK
            b = _mm(jnp.concatenate([w_all[r0:r1], qd[r0:r1]], axis=0), s_ref[h])
            vn.append(u_all[r0:r1] - b[:CHUNK])
            qs.append(b[CHUNK:])
        vn_all = jnp.concatenate(vn, axis=0)
        o_all = jnp.concatenate(qs, axis=0) + _mm(attn, vn_all)
        for h in range(GDN_HEADS):
            r0, r1 = h * CHUNK, (h + 1) * CHUNK
            s_ref[h] = s_ref[h] * jnp.exp(gl[h]) + _mm_tn(kd[r0:r1], vn[h])
            o = o_all[r0:r1]
            zz = z_ref[0, pl.ds(off, CHUNK), h * GDN_HEAD_DIM:(h + 1) * GDN_HEAD_DIM].astype(_F32)
            on = o * lax.rsqrt(jnp.mean(o * o, axis=-1, keepdims=True) + NORM_EPS) * gnw
            y_ref[0, pl.ds(off, CHUNK), h * GDN_HEAD_DIM:(h + 1) * GDN_HEAD_DIM] = (on * _silu(zz)).astype(_BF16)
        return carry

    lax.fori_loop(0, nc, chunk, 0)
    sout_ref[0] = s_ref[...]


def _gdn(q, k, v, z, bgc, grow, s0, gnw, *, lg):
    bsz, seq, _ = q.shape
    assert seq % lg == 0 and lg % CHUNK == 0
    nc = lg // CHUNK
    tok = lambda w: pl.BlockSpec((1, lg, w), lambda b, j: (b, j, 0))
    full = lambda a: pl.BlockSpec(a.shape, lambda b, j: (0,) * a.ndim)
    return pl.pallas_call(
        functools.partial(_gdn_body, nc=nc),
        grid=(bsz, seq // lg),
        in_specs=[tok(GDN_WIDTH)] * 4 + [tok(128), pl.BlockSpec((1, nc, 1, STACK), lambda b, j: (b, j, 0, 0)),
                                           full(s0), full(gnw)],
        out_specs=(tok(GDN_WIDTH), pl.BlockSpec((1, GDN_HEADS, GDN_HEAD_DIM, GDN_HEAD_DIM), lambda b, j: (b, 0, 0, 0))),
        out_shape=(jax.ShapeDtypeStruct((bsz, seq, GDN_WIDTH), _BF16),
                   jax.ShapeDtypeStruct((bsz, GDN_HEADS, GDN_HEAD_DIM, GDN_HEAD_DIM), _F32)),
        scratch_shapes=[pltpu.VMEM((GDN_HEADS, GDN_HEAD_DIM, GDN_HEAD_DIM), _F32)],
        compiler_params=_cparams("arbitrary", "arbitrary"),
        name="gdn",
    )(q, k, v, z, bgc, grow, s0, gnw)


def _outproj_body(yc_ref, yg_ref, x_ref, wo_ref, g_ref, b_ref, h1_ref, h1p_ref):
    mix = (jnp.dot(yc_ref[...], wo_ref[0:CONV_WIDTH, :], preferred_element_type=_F32)
           + jnp.dot(yg_ref[...], wo_ref[CONV_WIDTH:, :], preferred_element_type=_F32))
    h1 = _layer_norm(DN_ALPHA * x_ref[...] + mix, g_ref[...], b_ref[...])
    h1_ref[...] = h1
    h1p_ref[...] = _pack_halves(h1)


def _outproj(yc, yg, x2d, wo, g, b, *, tm):
    t = x2d.shape[0]
    assert t % tm == 0
    row = lambda w: pl.BlockSpec((tm, w), lambda i: (i, 0))
    full = lambda a: pl.BlockSpec(a.shape, lambda i: (0,) * a.ndim)
    return pl.pallas_call(
        _outproj_body,
        grid=(t // tm,),
        in_specs=[row(CONV_WIDTH), row(GDN_WIDTH), row(D_MODEL), full(wo), full(g), full(b)],
        out_specs=(row(D_MODEL), row(HALF)),
        out_shape=(jax.ShapeDtypeStruct((t, D_MODEL), _F32), jax.ShapeDtypeStruct((t, HALF), jnp.uint32)),
        compiler_params=_cparams("arbitrary"),
        name="outproj",
    )(yc, yg, x2d, wo, g, b)


def _router_body(h1_ref, wh_ref, wl_ref, br_ref, idx_ref, gate_ref, rank_ref, cnt_ref, carry_ref, *, tt):
    @pl.when(pl.program_id(0) == 0)
    def _():
        carry_ref[...] = jnp.zeros_like(carry_ref)

    x = h1_ref[...]
    xh = x.astype(_BF16)
    xl = (x - xh.astype(_F32)).astype(_BF16)
    wh = wh_ref[...]
    logits = _mm_nt(wh, xh) + _mm_nt(wh, xl) + _mm_nt(wl_ref[...], xh)
    scores = _sigmoid(logits)
    sel = scores + br_ref[...]
    ninf = -jnp.inf

    r32 = lax.broadcasted_iota(_I32, (E_PER_GROUP, tt), 0)
    gsc = []
    for g in range(N_GROUPS):
        xg = sel[g * E_PER_GROUP:(g + 1) * E_PER_GROUP]
        m1 = jnp.max(xg, axis=0, keepdims=True)
        i1 = jnp.min(jnp.where(xg == m1, r32, E_PER_GROUP), axis=0, keepdims=True)
        m2 = jnp.max(jnp.where(r32 == i1, ninf, xg), axis=0, keepdims=True)
        gsc.append(m1 + m2)
    work = jnp.concatenate(gsc, axis=0)
    r8 = lax.broadcasted_iota(_I32, (N_GROUPS, tt), 0)
    gkeep = jnp.zeros((N_GROUPS, tt), _F32)
    for _ in range(TOPK_GROUPS):
        m = jnp.max(work, axis=0, keepdims=True)
        gi = jnp.min(jnp.where(work == m, r8, N_GROUPS), axis=0, keepdims=True)
        pick = r8 == gi
        gkeep = jnp.where(pick, 1.0, gkeep)
        work = jnp.where(pick, ninf, work)
    selm = jnp.concatenate(
        [jnp.where(gkeep[g:g + 1] > 0.5, sel[g * E_PER_GROUP:(g + 1) * E_PER_GROUP], ninf)
         for g in range(N_GROUPS)], axis=0)

    re = lax.broadcasted_iota(_I32, (N_EXPERTS, tt), 0)
    msel = jnp.zeros((N_EXPERTS, tt), _F32)
    idxs, gates = [], []
    for _ in range(TOP_K):
        m = jnp.max(selm, axis=0, keepdims=True)
        ii = jnp.min(jnp.where(selm == m, re, N_EXPERTS), axis=0, keepdims=True)
        hit = re == ii
        idxs.append(ii)
        gates.append(jnp.sum(jnp.where(hit, scores, 0.0), axis=0, keepdims=True))
        selm = jnp.where(hit, ninf, selm)
        msel = jnp.where(hit, 1.0, msel)
    gate = jnp.concatenate(gates, axis=0)
    gate_ref[...] = gate / jnp.sum(gate, axis=0, keepdims=True) * ROUTED_SCALE
    idx_ref[...] = jnp.concatenate(idxs, axis=0)

    ta = lax.broadcasted_iota(_I32, (tt, tt), 0)
    tb = lax.broadcasted_iota(_I32, (tt, tt), 1)
    earlier = jnp.where(ta < tb, 1.0, 0.0)
    carry = carry_ref[...]
    rank_all = _mm(msel, earlier) + carry[:, 0:1]
    rank_ref[...] = jnp.concatenate(
        [jnp.sum(jnp.where(re == ii, rank_all, 0.0), axis=0, keepdims=True) for ii in idxs],
        axis=0).astype(_I32)
    carry = carry + jnp.sum(msel, axis=1, keepdims=True)
    carry_ref[...] = carry
    cnt_ref[...] = carry


def _router(h1, wh, wl, br, *, tt):
    t = h1.shape[0]
    assert t % tt == 0
    full = lambda a: pl.BlockSpec(a.shape, lambda i: (0,) * a.ndim)
    kt = pl.BlockSpec((TOP_K, tt), lambda i: (0, i))
    return pl.pallas_call(
        functools.partial(_router_body, tt=tt),
        grid=(t // tt,),
        in_specs=[pl.BlockSpec((tt, D_MODEL), lambda i: (i, 0)), full(wh), full(wl), full(br)],
        out_specs=(kt, kt, kt, pl.BlockSpec((N_EXPERTS, 128), lambda i: (0, 0))),
        out_shape=(jax.ShapeDtypeStruct((TOP_K, t), _I32), jax.ShapeDtypeStruct((TOP_K, t), _F32),
                   jax.ShapeDtypeStruct((TOP_K, t), _I32), jax.ShapeDtypeStruct((N_EXPERTS, 128), _F32)),
        scratch_shapes=[pltpu.VMEM((N_EXPERTS, 128), _F32)],
        compiler_params=_cparams("arbitrary"),
        name="router",
    )(h1, wh, wl, br)


def _position_body(idx_ref, rank_ref, pstart_ref, pos_ref, *, tt):
    re = lax.broadcasted_iota(_I32, (N_EXPERTS, tt), 0)
    ps = pstart_ref[...]
    idx = idx_ref[...]
    rows = [jnp.sum(jnp.where(re == idx[k:k + 1], ps, 0), axis=0, keepdims=True) for k in range(TOP_K)]
    pos_ref[0] = jnp.concatenate(rows, axis=0) + rank_ref[...]


def _position(idx, rank, pstart, *, tt):
    t = idx.shape[1]
    kt = pl.BlockSpec((TOP_K, tt), lambda i: (0, i))
    return pl.pallas_call(
        functools.partial(_position_body, tt=tt),
        grid=(t // tt,),
        in_specs=[kt, kt, pl.BlockSpec(pstart.shape, lambda i: (0, 0))],
        out_specs=pl.BlockSpec((1, TOP_K, tt), lambda i: (i, 0, 0)),
        out_shape=jax.ShapeDtypeStruct((t // tt, TOP_K, tt), _I32),
        compiler_params=_cparams("arbitrary"),
        name="position",
    )(idx, rank, pstart)


def _dispatch_body(pos_hbm, h1p_ref, xs_in, xs_out, pos_smem, psem, sem, *, tt):
    del xs_in
    i = pl.program_id(0)
    cp = pltpu.make_async_copy(pos_hbm.at[i], pos_smem, psem)
    cp.start()
    cp.wait()

    def row_copy(t, k):
        return pltpu.make_async_copy(h1p_ref.at[pl.ds(t, 1)], xs_out.at[pl.ds(pos_smem[k * tt + t], 1)], sem)

    def issue(t, c):
        for k in range(TOP_K):
            row_copy(t, k).start()
        return c

    lax.fori_loop(0, tt, issue, 0)

    def drain(t, c):
        for k in range(TOP_K):
            row_copy(t, k).wait()
        return c

    lax.fori_loop(0, tt, drain, 0)


def _dispatch(pos_tiles, h1p, xs_zero, *, tt):
    t = h1p.shape[0]
    return pl.pallas_call(
        functools.partial(_dispatch_body, tt=tt),
        grid=(t // tt,),
        in_specs=[pl.BlockSpec(memory_space=pl.ANY), pl.BlockSpec((tt, HALF), lambda i: (i, 0)),
                  pl.BlockSpec(memory_space=pl.ANY)],
        out_specs=pl.BlockSpec(memory_space=pl.ANY),
        out_shape=jax.ShapeDtypeStruct(xs_zero.shape, xs_zero.dtype),
        scratch_shapes=[pltpu.SMEM((TOP_K * tt,), _I32), pltpu.SemaphoreType.DMA, pltpu.SemaphoreType.DMA],
        input_output_aliases={2: 0},
        compiler_params=_cparams("arbitrary"),
        name="dispatch",
    )(pos_tiles, h1p, xs_zero)


def _ffn_body(be_ref, nu_ref, xs_ref, wg_ref, wu_ref, wd_ref, y_ref):
    del be_ref
    b = pl.program_id(0)

    @pl.when(b < nu_ref[0])
    def _():
        lo, hi = _unpack_halves(xs_ref[...])
        wg = wg_ref[0]
        wu = wu_ref[0]
        g = _mm(lo, wg[:HALF]) + _mm(hi, wg[HALF:])
        u = _mm(lo, wu[:HALF]) + _mm(hi, wu[HALF:])
        y_ref[...] = _pack_halves(_mm(_silu(g) * u, wd_ref[0]))

    @pl.when(b >= nu_ref[0])
    def _():
        y_ref[...] = jnp.zeros_like(y_ref)


def _ffn(block_e, n_used, xs, wg, wu, wd):
    nb = xs.shape[0] // ROW_BLOCK
    grid_spec = pltpu.PrefetchScalarGridSpec(
        num_scalar_prefetch=2,
        grid=(nb,),
        in_specs=[pl.BlockSpec((ROW_BLOCK, HALF), lambda b, be, nu: (b, 0)),
                  pl.BlockSpec((1, D_MODEL, EXPERT_FF), lambda b, be, nu: (be[b], 0, 0)),
                  pl.BlockSpec((1, D_MODEL, EXPERT_FF), lambda b, be, nu: (be[b], 0, 0)),
                  pl.BlockSpec((1, EXPERT_FF, D_MODEL), lambda b, be, nu: (be[b], 0, 0))],
        out_specs=pl.BlockSpec((ROW_BLOCK, HALF), lambda b, be, nu: (b, 0)),
    )
    return pl.pallas_call(
        _ffn_body,
        grid_spec=grid_spec,
        out_shape=jax.ShapeDtypeStruct(xs.shape, jnp.uint32),
        compiler_params=_cparams("arbitrary"),
        name="ffn",
    )(block_e, n_used, xs, wg, wu, wd)


def _combine_body(pos_hbm, gate_ref, h1_ref, ys_hbm, wsg_ref, wsu_ref, wsd_ref, g_ref, b_ref,
                  out_ref, pos_smem, psem, ybuf, sem, *, tt):
    i = pl.program_id(0)
    cp = pltpu.make_async_copy(pos_hbm.at[i], pos_smem, psem)
    cp.start()
    cp.wait()

    def row_copy(t, k):
        return pltpu.make_async_copy(ys_hbm.at[pl.ds(pos_smem[k * tt + t], 1)], ybuf.at[k, pl.ds(t, 1)], sem)

    def issue(t, c):
        for k in range(TOP_K):
            row_copy(t, k).start()
        return c

    lax.fori_loop(0, tt, issue, 0)

    x = h1_ref[...]
    xb = x.astype(_BF16)
    shared = _mm(_silu(_mm(xb, wsg_ref[...])) * _mm(xb, wsu_ref[...]), wsd_ref[...])

    def drain(t, c):
        for k in range(TOP_K):
            row_copy(t, k).wait()
        return c

    lax.fori_loop(0, tt, drain, 0)

    gcol = gate_ref[...].T
    acc_lo = jnp.zeros((tt, HALF), _F32)
    acc_hi = jnp.zeros((tt, HALF), _F32)
    for k in range(TOP_K):
        lo, hi = _unpack_halves(ybuf[k])
        acc_lo = acc_lo + gcol[:, k:k + 1] * lo
        acc_hi = acc_hi + gcol[:, k:k + 1] * hi
    routed = jnp.concatenate([acc_lo, acc_hi], axis=1)
    out_ref[...] = _layer_norm(DN_ALPHA * x + (routed + shared), g_ref[...], b_ref[...])


def _combine(pos_tiles, gate, h1, ys, wsg, wsu, wsd, g, b, *, tt):
    t = h1.shape[0]
    full = lambda a: pl.BlockSpec(a.shape, lambda i: (0,) * a.ndim)
    return pl.pallas_call(
        functools.partial(_combine_body, tt=tt),
        grid=(t // tt,),
        in_specs=[pl.BlockSpec(memory_space=pl.ANY), pl.BlockSpec((TOP_K, tt), lambda i: (0, i)),
                  pl.BlockSpec((tt, D_MODEL), lambda i: (i, 0)), pl.BlockSpec(memory_space=pl.ANY),
                  full(wsg), full(wsu), full(wsd), full(g), full(b)],
        out_specs=pl.BlockSpec((tt, D_MODEL), lambda i: (i, 0)),
        out_shape=jax.ShapeDtypeStruct((t, D_MODEL), _F32),
        scratch_shapes=[pltpu.SMEM((TOP_K * tt,), _I32), pltpu.SemaphoreType.DMA,
                        pltpu.VMEM((TOP_K, tt, HALF), jnp.uint32), pltpu.SemaphoreType.DMA],
        compiler_params=_cparams("arbitrary"),
        name="combine",
    )(pos_tiles, gate, h1, ys, wsg, wsu, wsd, g, b)


def _pick(n, pref):
    t = min(n, pref)
    while n % t:
        t -= CHUNK
    return t


def _mixer(x, tails, s0, wts, gnw, *, lt, lg):
    yc, q, k, v, z, bgc, bgr, tails_out = _premix(x, tails, wts, lt=lt)
    bsz, seq, _ = x.shape
    nch = seq // CHUNK
    grow = bgr[:, GDN_HEADS:2 * GDN_HEADS, :].reshape(bsz, GDN_HEADS, nch, CHUNK)
    grow = grow.transpose(0, 2, 1, 3).reshape(bsz, nch, 1, STACK)
    yg, s_out = _gdn(q, k, v, z, bgc, grow, s0, gnw, lg=lg)
    return yc, yg, tails_out, s_out


def kernel(x, meta_tokens, w_in, conv_w, conv_norm_w, gdn_conv_w, a_log, dt_bias, gdn_norm_w, w_out,
           ln1_g, ln1_b, w_router, b_router, w_gate, w_up, w_down, ws_gate, ws_up, ws_down, ln2_g, ln2_b):
    assert w_in.shape[0] == 1, "single-layer stack"
    bsz, seq, d = x.shape
    assert d == D_MODEL and seq % CHUNK == 0
    c, gw = CONV_WIDTH, GDN_WIDTH
    win = w_in[0].astype(_BF16)
    wbd = win[:, 3 * c + 4 * gw:]
    zpad = jnp.zeros((128 - 2 * GDN_HEADS,), _F32)
    zpad4 = jnp.zeros((GDN_HEADS,), _F32)
    prow = jnp.zeros((8, 128), _F32)
    prow = prow.at[0].set(jnp.concatenate([zpad4, a_log[0], zpad]))
    prow = prow.at[1].set(jnp.concatenate([zpad4, dt_bias[0], zpad]))
    wts = (win[:, :3 * c], win[:, 3 * c:3 * c + 3 * gw], win[:, 3 * c + 3 * gw:3 * c + 4 * gw],
           jnp.pad(wbd, ((0, 0), (0, 128 - 2 * GDN_HEADS))), wbd.T,
           conv_w[0], conv_norm_w, gdn_conv_w[0], prow, prow.T[:8])
    gnw = gdn_norm_w

    meta = jnp.concatenate([jnp.zeros((CHUNK - N_META, d), x.dtype), meta_tokens.astype(x.dtype)])[None]
    tails0 = jnp.zeros((HIST, c + 3 * gw), _F32)
    s00 = jnp.zeros((GDN_HEADS, GDN_HEAD_DIM, GDN_HEAD_DIM), _F32)
    _, _, tails_m, s_m = _mixer(meta, tails0, s00, wts, gnw, lt=CHUNK, lg=CHUNK)

    yc, yg, _, _ = _mixer(x, tails_m[0], s_m[0], wts, gnw, lt=_pick(seq, 512), lg=_pick(seq, 512))

    t = bsz * seq
    tm = _pick(t, 512)
    h1, h1p = _outproj(yc.reshape(t, c), yg.reshape(t, gw), x.reshape(t, d), w_out[0].astype(_BF16),
                       ln1_g, ln1_b, tm=tm)

    tt = _pick(t, 256)
    wr_t = w_router[0].T
    wr_hi = wr_t.astype(_BF16)
    wr_lo = (wr_t - wr_hi.astype(_F32)).astype(_BF16)
    idx, gate, rank, cnt = _router(h1, wr_hi, wr_lo, b_router[0][:, None], tt=tt)

    counts = cnt[:, 0].astype(_I32)
    pcounts = (counts + ROW_BLOCK - 1) // ROW_BLOCK * ROW_BLOCK
    pends = jnp.cumsum(pcounts)
    pstarts = pends - pcounts
    nb = t * TOP_K // ROW_BLOCK + N_EXPERTS
    block_row0 = jnp.arange(nb, dtype=_I32) * ROW_BLOCK
    block_e = jnp.minimum(jnp.sum((pends[None, :] <= block_row0[:, None]).astype(_I32), axis=1), N_EXPERTS - 1)
    n_used = (pends[-1:] // ROW_BLOCK).astype(_I32)

    pos = _position(idx, rank, pstarts[:, None].astype(_I32), tt=tt).reshape(t // tt, TOP_K * tt)
    xs = _dispatch(pos, h1p, jnp.zeros((nb * ROW_BLOCK, HALF), jnp.uint32), tt=tt)
    ys = _ffn(block_e, n_used, xs, w_gate[0], w_up[0], w_down[0])
    out = _combine(pos, gate, h1, ys, ws_gate[0].astype(_BF16), ws_up[0].astype(_BF16),
                   ws_down[0].astype(_BF16), ln2_g, ln2_b, tt=tt)
    return out.reshape(bsz, seq, d)
```

```python
import functools

import jax
import jax.numpy as jnp
from jax import lax
from jax.experimental import pallas as pl
from jax.experimental.pallas import tpu as pltpu

_F32 = jnp.float32
_BF16 = jnp.bfloat16
_I32 = jnp.int32

D_MODEL = 1024
N_META = 16
CONV_WIDTH = 512
CONV_K = 3
GDN_HEADS = 4
GDN_HEAD_DIM = 128
GDN_WIDTH = GDN_HEADS * GDN_HEAD_DIM
GDN_CONV_K = 4
CHUNK = 64
N_EXPERTS = 256
TOP_K = 8
N_GROUPS = 8
TOPK_GROUPS = 4
E_PER_GROUP = N_EXPERTS // N_GROUPS
EXPERT_FF = 256
ROUTED_SCALE = 2.5
ROW_BLOCK = 256
DN_ALPHA = 2.0 ** 0.25
NORM_EPS = 1e-5
HALF = D_MODEL // 2
STACK = GDN_HEADS * CHUNK
HIST = 8
GDN_ROWS = 4

V7X_VMEM_BYTES = 64 * 1024 * 1024
VMEM_LIMIT = V7X_VMEM_BYTES - 8 * 1024 * 1024


def _cparams(*sem):
    return pltpu.CompilerParams(dimension_semantics=sem, vmem_limit_bytes=VMEM_LIMIT)


def _mm(a, b):
    return jnp.dot(a.astype(_BF16), b.astype(_BF16), preferred_element_type=_F32)


def _mm_nt(a, b):
    return lax.dot_general(a.astype(_BF16), b.astype(_BF16), (((1,), (1,)), ((), ())),
                           preferred_element_type=_F32)


def _mm_tn(a, b):
    return lax.dot_general(a.astype(_BF16), b.astype(_BF16), (((0,), (0,)), ((), ())),
                           preferred_element_type=_F32)


def _sigmoid(x):
    return 1.0 / (1.0 + jnp.exp(-x))


def _silu(x):
    return x * _sigmoid(x)


def _softplus(x):
    return jnp.maximum(x, 0.0) + jnp.log1p(jnp.exp(-jnp.abs(x)))


def _pack_halves(y):
    return pltpu.pack_elementwise([y[:, :HALF], y[:, HALF:]], packed_dtype=_BF16)


def _unpack_halves(p):
    lo = pltpu.unpack_elementwise(p, index=0, packed_dtype=_BF16, unpacked_dtype=_F32)
    hi = pltpu.unpack_elementwise(p, index=1, packed_dtype=_BF16, unpacked_dtype=_F32)
    return lo, hi


def _layer_norm(h, g, b):
    mu = jnp.mean(h, axis=-1, keepdims=True)
    d = h - mu
    var = jnp.mean(d * d, axis=-1, keepdims=True)
    return d * lax.rsqrt(var + NORM_EPS) * g + b


def _premix_body(x_ref, tails_ref, wa_ref, wq_ref, wz_ref, wbd_ref, wbdt_ref, cw_ref, cnw_ref,
                 gcw_ref, prow_ref, pcol_ref,
                 yc_ref, q_ref, k_ref, v_ref, z_ref, bgc_ref, bgr_ref, tout_ref, ext_ref, *, lt):
    cw_ = CONV_WIDTH

    @pl.when(pl.program_id(1) == 0)
    def _():
        ext_ref[0:HIST, :] = tails_ref[...]

    xb = x_ref[0].astype(_BF16)
    pa = jnp.dot(xb, wa_ref[...], preferred_element_type=_F32)
    gate_b = pa[:, 0:cw_]
    u = pa[:, cw_:2 * cw_] * pa[:, 2 * cw_:3 * cw_]
    ext_ref[HIST:HIST + lt, 0:cw_] = u
    pq = jnp.dot(xb, wq_ref[...], preferred_element_type=_F32)
    ext_ref[HIST:HIST + lt, cw_:] = pq

    cw = cw_ref[...]
    ca = u * cw[CONV_K - 1:CONV_K, :]
    for j in range(CONV_K - 1):
        ca = ca + ext_ref[pl.ds(HIST - (CONV_K - 1) + j, lt), 0:cw_] * cw[j:j + 1, :]
    yc = gate_b * ca
    ms = jnp.mean(yc * yc, axis=-1, keepdims=True)
    yc_ref[0] = (yc * lax.rsqrt(ms + NORM_EPS) * cnw_ref[...]).astype(_BF16)

    gcw = gcw_ref[...]
    cq = pq * gcw[GDN_CONV_K - 1:GDN_CONV_K, :]
    for j in range(GDN_CONV_K - 1):
        cq = cq + ext_ref[pl.ds(HIST - (GDN_CONV_K - 1) + j, lt), cw_:] * gcw[j:j + 1, :]
    s = _silu(cq)
    for h in range(GDN_HEADS):
        lo, hi = h * GDN_HEAD_DIM, (h + 1) * GDN_HEAD_DIM
        qh = s[:, lo:hi]
        kh = s[:, GDN_WIDTH + lo:GDN_WIDTH + hi]
        qn = qh * lax.rsqrt(jnp.sum(qh * qh, axis=-1, keepdims=True) + 1e-6)
        kn = kh * lax.rsqrt(jnp.sum(kh * kh, axis=-1, keepdims=True) + 1e-6)
        q_ref[0, :, lo:hi] = (qn * (GDN_HEAD_DIM ** -0.5)).astype(_BF16)
        k_ref[0, :, lo:hi] = kn.astype(_BF16)
    v_ref[0] = s[:, 2 * GDN_WIDTH:].astype(_BF16)
    z_ref[0] = jnp.dot(xb, wz_ref[...], preferred_element_type=_F32).astype(_BF16)

    bdc = jnp.dot(xb, wbd_ref[...], preferred_element_type=_F32)
    prow = prow_ref[...]
    g_c = -jnp.exp(prow[0:1, :]) * _softplus(bdc + prow[1:2, :])
    lane = lax.broadcasted_iota(_I32, bdc.shape, 1)
    bgc_ref[0] = jnp.where(lane < GDN_HEADS, _sigmoid(bdc), g_c)
    bdr = _mm_nt(wbdt_ref[...], xb)
    pcol = pcol_ref[...]
    g_r = -jnp.exp(pcol[:, 0:1]) * _softplus(bdr + pcol[:, 1:2])
    row = lax.broadcasted_iota(_I32, bdr.shape, 0)
    bgr_ref[0] = jnp.where(row < GDN_HEADS, _sigmoid(bdr), g_r)

    tail = ext_ref[lt:lt + HIST, :]
    ext_ref[0:HIST, :] = tail
    tout_ref[0] = tail


def _premix(x, tails, wts, *, lt):
    bsz, seq, d = x.shape
    assert seq % lt == 0
    grid = (bsz, seq // lt)
    full = lambda a: pl.BlockSpec(a.shape, lambda b, j: (0,) * a.ndim)
    tok = lambda w: pl.BlockSpec((1, lt, w), lambda b, j: (b, j, 0))
    (wa, wq, wz, wbd, wbdt, cw, cnw, gcw, prow, pcol) = wts
    ext_w = CONV_WIDTH + 3 * GDN_WIDTH
    out_shape = (
        jax.ShapeDtypeStruct((bsz, seq, CONV_WIDTH), _BF16),
        jax.ShapeDtypeStruct((bsz, seq, GDN_WIDTH), _BF16),
        jax.ShapeDtypeStruct((bsz, seq, GDN_WIDTH), _BF16),
        jax.ShapeDtypeStruct((bsz, seq, GDN_WIDTH), _BF16),
        jax.ShapeDtypeStruct((bsz, seq, GDN_WIDTH), _BF16),
        jax.ShapeDtypeStruct((bsz, seq, 128), _F32),
        jax.ShapeDtypeStruct((bsz, 8, seq), _F32),
        jax.ShapeDtypeStruct((bsz, HIST, ext_w), _F32),
    )
    out_specs = (tok(CONV_WIDTH), tok(GDN_WIDTH), tok(GDN_WIDTH), tok(GDN_WIDTH), tok(GDN_WIDTH),
                 tok(128), pl.BlockSpec((1, 8, lt), lambda b, j: (b, 0, j)),
                 pl.BlockSpec((1, HIST, ext_w), lambda b, j: (b, 0, 0)))
    return pl.pallas_call(
        functools.partial(_premix_body, lt=lt),
        grid=grid,
        in_specs=[tok(d), full(tails)] + [full(w) for w in wts],
        out_specs=out_specs,
        out_shape=out_shape,
        scratch_shapes=[pltpu.VMEM((HIST + lt, ext_w), _F32)],
        compiler_params=_cparams("arbitrary", "arbitrary"),
        name="premix",
    )(x, tails, *wts)


def _cumsum_rows(x):
    row = lax.broadcasted_iota(_I32, x.shape, 0)
    s = 1
    while s < x.shape[0]:
        x = x + jnp.where(row >= s, pltpu.roll(x, s, 0), 0.0)
        s *= 2
    return x


def _cumsum_lanes_seg(x):
    lane = lax.broadcasted_iota(_I32, x.shape, 1) & (CHUNK - 1)
    s = 1
    while s < CHUNK:
        x = x + jnp.where(lane >= s, pltpu.roll(x, s, 1), 0.0)
        s *= 2
    return x


def _stack_heads(a):
    return jnp.concatenate([a[:, h * GDN_HEAD_DIM:(h + 1) * GDN_HEAD_DIM] for h in range(GDN_HEADS)], axis=0)


def _gdn_body(q_ref, k_ref, v_ref, z_ref, bgc_ref, grow_ref, s0_ref, gnw_ref,
              y_ref, sout_ref, s_ref, *, nc, nbb):
    @pl.when(pl.program_id(1) == 0)
    def _():
        for r in range(nbb):
            s_ref[r] = s0_ref[...]

    ri = lax.broadcasted_iota(_I32, (STACK, STACK), 0)
    ci = lax.broadcasted_iota(_I32, (STACK, STACK), 1)
    same64 = (ri >> 6) == (ci >> 6)
    same32 = (ri >> 5) == (ci >> 5)
    same16 = (ri >> 4) == (ci >> 4)
    low_incl = same64 & (ri >= ci)
    low_strict = same64 & (ri > ci)
    gnw = gnw_ref[...]

    def chunk_row(r, c):
        off = pl.multiple_of(c * CHUNK, CHUNK)
        q_all = _stack_heads(q_ref[r, pl.ds(off, CHUNK), :].astype(_F32))
        k_all = _stack_heads(k_ref[r, pl.ds(off, CHUNK), :].astype(_F32))
        v_all = _stack_heads(v_ref[r, pl.ds(off, CHUNK), :].astype(_F32))
        bgc = bgc_ref[r, pl.ds(off, CHUNK), :]
        gcs = _cumsum_rows(bgc)
        hd = (CHUNK, GDN_HEAD_DIM)
        beta_b = jnp.concatenate(
            [jnp.broadcast_to(bgc[:, h:h + 1], hd) for h in range(GDN_HEADS)], axis=0)
        gc_b = jnp.concatenate(
            [jnp.broadcast_to(gcs[:, GDN_HEADS + h:GDN_HEADS + h + 1], hd) for h in range(GDN_HEADS)], axis=0)
        gl = [gcs[CHUNK - 1:CHUNK, GDN_HEADS + h:GDN_HEADS + h + 1] for h in range(GDN_HEADS)]
        gl_b = jnp.concatenate([jnp.broadcast_to(g1, hd) for g1 in gl], axis=0)
        gcr = _cumsum_lanes_seg(jnp.broadcast_to(grow_ref[r, c], (8, STACK)))[0:1, :]

        diff = jnp.concatenate([gc_b, gc_b], axis=1) - gcr
        decay = jnp.exp(jnp.where(low_incl, diff, -1e30))
        kb = k_all * beta_b
        a1 = _mm_nt(jnp.concatenate([kb, q_all], axis=0), k_all)
        yield
        m = jnp.where(low_strict, a1[:STACK] * decay, 0.0)
        attn = a1[STACK:] * decay

        l16 = jnp.where(same16, m, 0.0)
        c1 = jnp.where(same32 & jnp.logical_not(same16), m, 0.0)
        c2 = jnp.where(same32, 0.0, m)
        p2 = _mm(l16, l16)
        yield
        p4 = _mm(p2, p2)
        t = _mm(l16, p2)
        yield
        na = p2 - l16 - t
        p8 = _mm(p4, p4)
        t = _mm(na, p4)
        yield
        nb = na + p4 + t
        t = _mm(nb, p8)
        yield
        ncm = nb + p8 + t
        t = _mm(c1, ncm)
        yield
        y1 = c1 + t
        t = _mm(ncm, y1)
        yield
        n1 = ncm - y1 - t
        t = _mm(c2, n1)
        yield
        y2 = c2 + t
        t = _mm(n1, y2)
        yield
        nt = n1 - y2 - t

        egc = jnp.exp(gc_b)
        rhs = jnp.concatenate([v_all * beta_b, kb * egc], axis=1)
        t = _mm(nt, rhs)
        yield
        uw = rhs + t
        u_all = uw[:, :GDN_HEAD_DIM]
        w_all = uw[:, GDN_HEAD_DIM:]
        qd = q_all * egc
        kd = k_all * jnp.exp(gl_b - gc_b)

        bs = []
        for h in range(GDN_HEADS):
            r0, r1 = h * CHUNK, (h + 1) * CHUNK
            bs.append(_mm(jnp.concatenate([w_all[r0:r1], qd[r0:r1]], axis=0), s_ref[r, h]))
        yield
        vn = [u_all[h * CHUNK:(h + 1) * CHUNK] - bs[h][:CHUNK] for h in range(GDN_HEADS)]
        vn_all = jnp.concatenate(vn, axis=0)
        t = _mm(attn, vn_all)
        ds = [_mm_tn(kd[h * CHUNK:(h + 1) * CHUNK], vn[h]) for h in range(GDN_HEADS)]
        yield
        o_all = jnp.concatenate([b[CHUNK:] for b in bs], axis=0) + t
        for h in range(GDN_HEADS):
            r0, r1 = h * CHUNK, (h + 1) * CHUNK
            s_ref[r, h] = s_ref[r, h] * jnp.exp(gl[h]) + ds[h]
            o = o_all[r0:r1]
            zz = z_ref[r, pl.ds(off, CHUNK), h * GDN_HEAD_DIM:(h + 1) * GDN_HEAD_DIM].astype(_F32)
            on = o * lax.rsqrt(jnp.mean(o * o, axis=-1, keepdims=True) + NORM_EPS) * gnw
            y_ref[r, pl.ds(off, CHUNK), h * GDN_HEAD_DIM:(h + 1) * GDN_HEAD_DIM] = (on * _silu(zz)).astype(_BF16)

    def chunk(c, carry):
        live = [chunk_row(r, c) for r in range(nbb)]
        while live:
            live = [g for g in live if next(g, live) is not live]
        return carry

    lax.fori_loop(0, nc, chunk, 0)
    sout_ref[...] = s_ref[...]


def _gdn(q, k, v, z, bgc, grow, s0, gnw, *, lg, nbb):
    bsz, seq, _ = q.shape
    assert seq % lg == 0 and lg % CHUNK == 0 and bsz % nbb == 0
    nc = lg // CHUNK
    tok = lambda w: pl.BlockSpec((nbb, lg, w), lambda b, j: (b, j, 0))
    full = lambda a: pl.BlockSpec(a.shape, lambda b, j: (0,) * a.ndim)
    st = (nbb, GDN_HEADS, GDN_HEAD_DIM, GDN_HEAD_DIM)
    return pl.pallas_call(
        functools.partial(_gdn_body, nc=nc, nbb=nbb),
        grid=(bsz // nbb, seq // lg),
        in_specs=[tok(GDN_WIDTH)] * 4 + [tok(128), pl.BlockSpec((nbb, nc, 1, STACK), lambda b, j: (b, j, 0, 0)),
                                           full(s0), full(gnw)],
        out_specs=(tok(GDN_WIDTH), pl.BlockSpec(st, lambda b, j: (b, 0, 0, 0))),
        out_shape=(jax.ShapeDtypeStruct((bsz, seq, GDN_WIDTH), _BF16),
                   jax.ShapeDtypeStruct((bsz, GDN_HEADS, GDN_HEAD_DIM, GDN_HEAD_DIM), _F32)),
        scratch_shapes=[pltpu.VMEM(st, _F32)],
        compiler_params=_cparams("arbitrary", "arbitrary"),
        name="gdn",
    )(q, k, v, z, bgc, grow, s0, gnw)


def _outproj_body(yc_ref, yg_ref, x_ref, wo_ref, g_ref, b_ref, h1_ref, h1p_ref):
    mix = (jnp.dot(yc_ref[...], wo_ref[0:CONV_WIDTH, :], preferred_element_type=_F32)
           + jnp.dot(yg_ref[...], wo_ref[CONV_WIDTH:, :], preferred_element_type=_F32))
    h1 = _layer_norm(DN_ALPHA * x_ref[...] + mix, g_ref[...], b_ref[...])
    h1_ref[...] = h1
    h1p_ref[...] = _pack_halves(h1)


def _outproj(yc, yg, x2d, wo, g, b, *, tm):
    t = x2d.shape[0]
    assert t % tm == 0
    row = lambda w: pl.BlockSpec((tm, w), lambda i: (i, 0))
    full = lambda a: pl.BlockSpec(a.shape, lambda i: (0,) * a.ndim)
    return pl.pallas_call(
        _outproj_body,
        grid=(t // tm,),
        in_specs=[row(CONV_WIDTH), row(GDN_WIDTH), row(D_MODEL), full(wo), full(g), full(b)],
        out_specs=(row(D_MODEL), row(HALF)),
        out_shape=(jax.ShapeDtypeStruct((t, D_MODEL), _F32), jax.ShapeDtypeStruct((t, HALF), jnp.uint32)),
        compiler_params=_cparams("arbitrary"),
        name="outproj",
    )(yc, yg, x2d, wo, g, b)


def _router_body(h1_ref, wh_ref, wl_ref, br_ref, idx_ref, gate_ref, rank_ref, cnt_ref, carry_ref, *, tt):
    @pl.when(pl.program_id(0) == 0)
    def _():
        carry_ref[...] = jnp.zeros_like(carry_ref)

    x = h1_ref[...]
    xh = x.astype(_BF16)
    xl = (x - xh.astype(_F32)).astype(_BF16)
    wh = wh_ref[...]
    logits = _mm_nt(wh, xh) + _mm_nt(wh, xl) + _mm_nt(wl_ref[...], xh)
    scores = _sigmoid(logits)
    sel = scores + br_ref[...]
    ninf = -jnp.inf

    r32 = lax.broadcasted_iota(_I32, (E_PER_GROUP, tt), 0)
    gsc = []
    for g in range(N_GROUPS):
        xg = sel[g * E_PER_GROUP:(g + 1) * E_PER_GROUP]
        m1 = jnp.max(xg, axis=0, keepdims=True)
        i1 = jnp.min(jnp.where(xg == m1, r32, E_PER_GROUP), axis=0, keepdims=True)
        m2 = jnp.max(jnp.where(r32 == i1, ninf, xg), axis=0, keepdims=True)
        gsc.append(m1 + m2)
    work = jnp.concatenate(gsc, axis=0)
    r8 = lax.broadcasted_iota(_I32, (N_GROUPS, tt), 0)
    gkeep = jnp.zeros((N_GROUPS, tt), _F32)
    for _ in range(TOPK_GROUPS):
        m = jnp.max(work, axis=0, keepdims=True)
        gi = jnp.min(jnp.where(work == m, r8, N_GROUPS), axis=0, keepdims=True)
        pick = r8 == gi
        gkeep = jnp.where(pick, 1.0, gkeep)
        work = jnp.where(pick, ninf, work)
    selm = jnp.concatenate(
        [jnp.where(gkeep[g:g + 1] > 0.5, sel[g * E_PER_GROUP:(g + 1) * E_PER_GROUP], ninf)
         for g in range(N_GROUPS)], axis=0)

    re = lax.broadcasted_iota(_I32, (N_EXPERTS, tt), 0)
    msel = jnp.zeros((N_EXPERTS, tt), _F32)
    idxs, gates = [], []
    for _ in range(TOP_K):
        m = jnp.max(selm, axis=0, keepdims=True)
        ii = jnp.min(jnp.where(selm == m, re, N_EXPERTS), axis=0, keepdims=True)
        hit = re == ii
        idxs.append(ii)
        gates.append(jnp.sum(jnp.where(hit, scores, 0.0), axis=0, keepdims=True))
        selm = jnp.where(hit, ninf, selm)
        msel = jnp.where(hit, 1.0, msel)
    gate = jnp.concatenate(gates, axis=0)
    gate_ref[...] = gate / jnp.sum(gate, axis=0, keepdims=True) * ROUTED_SCALE
    idx_ref[...] = jnp.concatenate(idxs, axis=0)

    ta = lax.broadcasted_iota(_I32, (tt, tt), 0)
    tb = lax.broadcasted_iota(_I32, (tt, tt), 1)
    earlier = jnp.where(ta < tb, 1.0, 0.0)
    carry = carry_ref[...]
    rank_all = _mm(msel, earlier) + carry[:, 0:1]
    rank_ref[...] = jnp.concatenate(
        [jnp.sum(jnp.where(re == ii, rank_all, 0.0), axis=0, keepdims=True) for ii in idxs],
        axis=0).astype(_I32)
    carry = carry + jnp.sum(msel, axis=1, keepdims=True)
    carry_ref[...] = carry
    cnt_ref[...] = carry


def _router(h1, wh, wl, br, *, tt):
    t = h1.shape[0]
    assert t % tt == 0
    full = lambda a: pl.BlockSpec(a.shape, lambda i: (0,) * a.ndim)
    kt = pl.BlockSpec((TOP_K, tt), lambda i: (0, i))
    return pl.pallas_call(
        functools.partial(_router_body, tt=tt),
        grid=(t // tt,),
        in_specs=[pl.BlockSpec((tt, D_MODEL), lambda i: (i, 0)), full(wh), full(wl), full(br)],
        out_specs=(kt, kt, kt, pl.BlockSpec((N_EXPERTS, 128), lambda i: (0, 0))),
        out_shape=(jax.ShapeDtypeStruct((TOP_K, t), _I32), jax.ShapeDtypeStruct((TOP_K, t), _F32),
                   jax.ShapeDtypeStruct((TOP_K, t), _I32), jax.ShapeDtypeStruct((N_EXPERTS, 128), _F32)),
        scratch_shapes=[pltpu.VMEM((N_EXPERTS, 128), _F32)],
        compiler_params=_cparams("arbitrary"),
        name="router",
    )(h1, wh, wl, br)


def _position_body(idx_ref, rank_ref, pstart_ref, pos_ref, *, tt):
    re = lax.broadcasted_iota(_I32, (N_EXPERTS, tt), 0)
    ps = pstart_ref[...]
    idx = idx_ref[...]
    rows = [jnp.sum(jnp.where(re == idx[k:k + 1], ps, 0), axis=0, keepdims=True) for k in range(TOP_K)]
    pos_ref[0] = jnp.concatenate(rows, axis=0) + rank_ref[...]


def _position(idx, rank, pstart, *, tt):
    t = idx.shape[1]
    kt = pl.BlockSpec((TOP_K, tt), lambda i: (0, i))
    return pl.pallas_call(
        functools.partial(_position_body, tt=tt),
        grid=(t // tt,),
        in_specs=[kt, kt, pl.BlockSpec(pstart.shape, lambda i: (0, 0))],
        out_specs=pl.BlockSpec((1, TOP_K, tt), lambda i: (i, 0, 0)),
        out_shape=jax.ShapeDtypeStruct((t // tt, TOP_K, tt), _I32),
        compiler_params=_cparams("arbitrary"),
        name="position",
    )(idx, rank, pstart)


def _dispatch_body(pos_hbm, h1p_ref, xs_in, xs_out, pos_smem, psem, sem, *, tt):
    del xs_in
    i = pl.program_id(0)
    cp = pltpu.make_async_copy(pos_hbm.at[i], pos_smem, psem)
    cp.start()
    cp.wait()

    def row_copy(t, k):
        return pltpu.make_async_copy(h1p_ref.at[pl.ds(t, 1)], xs_out.at[pl.ds(pos_smem[k * tt + t], 1)], sem)

    def issue(t, c):
        for k in range(TOP_K):
            row_copy(t, k).start(priority=k % 2)
        return c

    lax.fori_loop(0, tt, issue, 0)

    def drain(t, c):
        for k in range(TOP_K):
            row_copy(t, k).wait()
        return c

    lax.fori_loop(0, tt, drain, 0)


def _dispatch(pos_tiles, h1p, xs_zero, *, tt):
    t = h1p.shape[0]
    return pl.pallas_call(
        functools.partial(_dispatch_body, tt=tt),
        grid=(t // tt,),
        in_specs=[pl.BlockSpec(memory_space=pl.ANY), pl.BlockSpec((tt, HALF), lambda i: (i, 0)),
                  pl.BlockSpec(memory_space=pl.ANY)],
        out_specs=pl.BlockSpec(memory_space=pl.ANY),
        out_shape=jax.ShapeDtypeStruct(xs_zero.shape, xs_zero.dtype),
        scratch_shapes=[pltpu.SMEM((TOP_K * tt,), _I32), pltpu.SemaphoreType.DMA, pltpu.SemaphoreType.DMA],
        input_output_aliases={2: 0},
        compiler_params=_cparams("arbitrary"),
        name="dispatch",
    )(pos_tiles, h1p, xs_zero)


def _ffn_body(be_ref, nu_ref, xs_ref, wg_ref, wu_ref, wd_ref, y_ref, wgu_bf, wd_bf):
    b = pl.program_id(0)
    new_expert = jnp.logical_or(b == 0, be_ref[b] != be_ref[jnp.maximum(b - 1, 0)])

    @pl.when(jnp.logical_and(new_expert, b < nu_ref[0]))
    def _():
        wgu_bf[:, 0:EXPERT_FF] = wg_ref[0].astype(_BF16)
        wgu_bf[:, EXPERT_FF:] = wu_ref[0].astype(_BF16)
        wd_bf[...] = wd_ref[0].astype(_BF16)

    @pl.when(b < nu_ref[0])
    def _():
        lo, hi = _unpack_halves(xs_ref[...])
        gu = (jnp.dot(lo.astype(_BF16), wgu_bf[0:HALF, :], preferred_element_type=_F32)
              + jnp.dot(hi.astype(_BF16), wgu_bf[HALF:, :], preferred_element_type=_F32))
        h = (_silu(gu[:, :EXPERT_FF]) * gu[:, EXPERT_FF:]).astype(_BF16)
        y_ref[...] = _pack_halves(jnp.dot(h, wd_bf[...], preferred_element_type=_F32))

    @pl.when(b >= nu_ref[0])
    def _():
        y_ref[...] = jnp.zeros_like(y_ref)


def _ffn(block_e, n_used, xs, wg, wu, wd):
    nb = xs.shape[0] // ROW_BLOCK
    grid_spec = pltpu.PrefetchScalarGridSpec(
        num_scalar_prefetch=2,
        grid=(nb,),
        in_specs=[pl.BlockSpec((ROW_BLOCK, HALF), lambda b, be, nu: (b, 0)),
                  pl.BlockSpec((1, D_MODEL, EXPERT_FF), lambda b, be, nu: (be[b], 0, 0)),
                  pl.BlockSpec((1, D_MODEL, EXPERT_FF), lambda b, be, nu: (be[b], 0, 0)),
                  pl.BlockSpec((1, EXPERT_FF, D_MODEL), lambda b, be, nu: (be[b], 0, 0))],
        out_specs=pl.BlockSpec((ROW_BLOCK, HALF), lambda b, be, nu: (b, 0)),
        scratch_shapes=[pltpu.VMEM((D_MODEL, 2 * EXPERT_FF), _BF16), pltpu.VMEM((EXPERT_FF, D_MODEL), _BF16)],
    )
    return pl.pallas_call(
        _ffn_body,
        grid_spec=grid_spec,
        out_shape=jax.ShapeDtypeStruct(xs.shape, jnp.uint32),
        compiler_params=_cparams("arbitrary"),
        name="ffn",
    )(block_e, n_used, xs, wg, wu, wd)


def _combine_body(pos_hbm, gate_ref, h1_ref, ys_hbm, wsg_ref, wsu_ref, wsd_ref, g_ref, b_ref,
                  out_ref, pos_smem, psem, ybuf, sem, *, tt):
    i = pl.program_id(0)
    cp = pltpu.make_async_copy(pos_hbm.at[i], pos_smem, psem)
    cp.start()
    cp.wait()

    def row_copy(t, k):
        return pltpu.make_async_copy(ys_hbm.at[pl.ds(pos_smem[k * tt + t], 1)], ybuf.at[k, pl.ds(t, 1)], sem)

    def issue(t, c):
        for k in range(TOP_K):
            row_copy(t, k).start(priority=k % 2)
        return c

    lax.fori_loop(0, tt, issue, 0)

    x = h1_ref[...]
    xb = x.astype(_BF16)
    shared = _mm(_silu(_mm(xb, wsg_ref[...])) * _mm(xb, wsu_ref[...]), wsd_ref[...])

    def drain(t, c):
        for k in range(TOP_K):
            row_copy(t, k).wait()
        return c

    lax.fori_loop(0, tt, drain, 0)

    gcol = gate_ref[...].T
    acc_lo = jnp.zeros((tt, HALF), _F32)
    acc_hi = jnp.zeros((tt, HALF), _F32)
    for k in range(TOP_K):
        lo, hi = _unpack_halves(ybuf[k])
        acc_lo = acc_lo + gcol[:, k:k + 1] * lo
        acc_hi = acc_hi + gcol[:, k:k + 1] * hi
    routed = jnp.concatenate([acc_lo, acc_hi], axis=1)
    out_ref[...] = _layer_norm(DN_ALPHA * x + (routed + shared), g_ref[...], b_ref[...])


def _combine(pos_tiles, gate, h1, ys, wsg, wsu, wsd, g, b, *, tt):
    t = h1.shape[0]
    full = lambda a: pl.BlockSpec(a.shape, lambda i: (0,) * a.ndim)
    return pl.pallas_call(
        functools.partial(_combine_body, tt=tt),
        grid=(t // tt,),
        in_specs=[pl.BlockSpec(memory_space=pl.ANY), pl.BlockSpec((TOP_K, tt), lambda i: (0, i)),
                  pl.BlockSpec((tt, D_MODEL), lambda i: (i, 0)), pl.BlockSpec(memory_space=pl.ANY),
                  full(wsg), full(wsu), full(wsd), full(g), full(b)],
        out_specs=pl.BlockSpec((tt, D_MODEL), lambda i: (i, 0)),
        out_shape=jax.ShapeDtypeStruct((t, D_MODEL), _F32),
        scratch_shapes=[pltpu.SMEM((TOP_K * tt,), _I32), pltpu.SemaphoreType.DMA,
                        pltpu.VMEM((TOP_K, tt, HALF), jnp.uint32), pltpu.SemaphoreType.DMA],
        compiler_params=_cparams("arbitrary"),
        name="combine",
    )(pos_tiles, gate, h1, ys, wsg, wsu, wsd, g, b)


def _pick(n, pref):
    t = min(n, pref)
    while n % t:
        t -= CHUNK
    return t


def _mixer(x, tails, s0, wts, gnw, *, lt, lg, nbb):
    yc, q, k, v, z, bgc, bgr, tails_out = _premix(x, tails, wts, lt=lt)
    bsz, seq, _ = x.shape
    nch = seq // CHUNK
    grow = bgr[:, GDN_HEADS:2 * GDN_HEADS, :].reshape(bsz, GDN_HEADS, nch, CHUNK)
    grow = grow.transpose(0, 2, 1, 3).reshape(bsz, nch, 1, STACK)
    yg, s_out = _gdn(q, k, v, z, bgc, grow, s0, gnw, lg=lg, nbb=nbb)
    return yc, yg, tails_out, s_out


def kernel(x, meta_tokens, w_in, conv_w, conv_norm_w, gdn_conv_w, a_log, dt_bias, gdn_norm_w, w_out,
           ln1_g, ln1_b, w_router, b_router, w_gate, w_up, w_down, ws_gate, ws_up, ws_down, ln2_g, ln2_b):
    assert w_in.shape[0] == 1, "single-layer stack"
    bsz, seq, d = x.shape
    assert d == D_MODEL and seq % CHUNK == 0
    c, gw = CONV_WIDTH, GDN_WIDTH
    win = w_in[0].astype(_BF16)
    wbd = win[:, 3 * c + 4 * gw:]
    zpad = jnp.zeros((128 - 2 * GDN_HEADS,), _F32)
    zpad4 = jnp.zeros((GDN_HEADS,), _F32)
    prow = jnp.zeros((8, 128), _F32)
    prow = prow.at[0].set(jnp.concatenate([zpad4, a_log[0], zpad]))
    prow = prow.at[1].set(jnp.concatenate([zpad4, dt_bias[0], zpad]))
    wts = (win[:, :3 * c], win[:, 3 * c:3 * c + 3 * gw], win[:, 3 * c + 3 * gw:3 * c + 4 * gw],
           jnp.pad(wbd, ((0, 0), (0, 128 - 2 * GDN_HEADS))), wbd.T,
           conv_w[0], conv_norm_w, gdn_conv_w[0], prow, prow.T[:8])
    gnw = gdn_norm_w

    meta = jnp.concatenate([jnp.zeros((CHUNK - N_META, d), x.dtype), meta_tokens.astype(x.dtype)])[None]
    tails0 = jnp.zeros((HIST, c + 3 * gw), _F32)
    s00 = jnp.zeros((GDN_HEADS, GDN_HEAD_DIM, GDN_HEAD_DIM), _F32)
    _, _, tails_m, s_m = _mixer(meta, tails0, s00, wts, gnw, lt=CHUNK, lg=CHUNK, nbb=1)

    yc, yg, _, _ = _mixer(x, tails_m[0], s_m[0], wts, gnw, lt=_pick(seq, 512), lg=_pick(seq, 512),
                          nbb=GDN_ROWS if bsz % GDN_ROWS == 0 else 1)

    t = bsz * seq
    tm = _pick(t, 512)
    h1, h1p = _outproj(yc.reshape(t, c), yg.reshape(t, gw), x.reshape(t, d), w_out[0].astype(_BF16),
                       ln1_g, ln1_b, tm=tm)

    tt = _pick(t, 256)
    wr_t = w_router[0].T
    wr_hi = wr_t.astype(_BF16)
    wr_lo = (wr_t - wr_hi.astype(_F32)).astype(_BF16)
    idx, gate, rank, cnt = _router(h1, wr_hi, wr_lo, b_router[0][:, None], tt=tt)

    counts = cnt[:, 0].astype(_I32)
    pcounts = (counts + ROW_BLOCK - 1) // ROW_BLOCK * ROW_BLOCK
    pends = jnp.cumsum(pcounts)
    pstarts = pends - pcounts
    nb = t * TOP_K // ROW_BLOCK + N_EXPERTS
    block_row0 = jnp.arange(nb, dtype=_I32) * ROW_BLOCK
    block_e = jnp.minimum(jnp.sum((pends[None, :] <= block_row0[:, None]).astype(_I32), axis=1), N_EXPERTS - 1)
    n_used = (pends[-1:] // ROW_BLOCK).astype(_I32)

    pos = _position(idx, rank, pstarts[:, None].astype(_I32), tt=tt).reshape(t // tt, TOP_K * tt)
    xs = _dispatch(pos, h1p, jnp.zeros((nb * ROW_BLOCK, HALF), jnp.uint32), tt=tt)
    ys = _ffn(block_e, n_used, xs, w_gate[0], w_up[0], w_down[0])
    out = _combine(pos, gate, h1, ys, ws_gate[0].astype(_BF16), ws_up[0].astype(_BF16),
                   ws_down[0].astype(_BF16), ln2_g, ln2_b, tt=tt)
    return out.reshape(bsz, seq, d)
```

```python
import functools

import jax
import jax.numpy as jnp
from jax import lax
from jax.experimental import pallas as pl
from jax.experimental.pallas import tpu as pltpu

_F32 = jnp.float32
_BF16 = jnp.bfloat16
_I32 = jnp.int32

D_MODEL = 1024
N_META = 16
CONV_WIDTH = 512
CONV_K = 3
GDN_HEADS = 4
GDN_HEAD_DIM = 128
GDN_WIDTH = GDN_HEADS * GDN_HEAD_DIM
GDN_CONV_K = 4
CHUNK = 64
N_EXPERTS = 256
TOP_K = 8
N_GROUPS = 8
TOPK_GROUPS = 4
E_PER_GROUP = N_EXPERTS // N_GROUPS
EXPERT_FF = 256
ROUTED_SCALE = 2.5
ROW_BLOCK = 256
DN_ALPHA = 2.0 ** 0.25
NORM_EPS = 1e-5
HALF = D_MODEL // 2
STACK = GDN_HEADS * CHUNK
HIST = 8
GDN_ROWS = 4

V7X_VMEM_BYTES = 64 * 1024 * 1024
VMEM_LIMIT = V7X_VMEM_BYTES - 8 * 1024 * 1024


def _cparams(*sem):
    return pltpu.CompilerParams(dimension_semantics=sem, vmem_limit_bytes=VMEM_LIMIT)


def _mm(a, b):
    return jnp.dot(a.astype(_BF16), b.astype(_BF16), preferred_element_type=_F32)


def _mm_nt(a, b):
    return lax.dot_general(a.astype(_BF16), b.astype(_BF16), (((1,), (1,)), ((), ())),
                           preferred_element_type=_F32)


def _mm_tn(a, b):
    return lax.dot_general(a.astype(_BF16), b.astype(_BF16), (((0,), (0,)), ((), ())),
                           preferred_element_type=_F32)


def _sigmoid(x):
    return 1.0 / (1.0 + jnp.exp(-x))


def _silu(x):
    return x * _sigmoid(x)


def _softplus(x):
    return jnp.maximum(x, 0.0) + jnp.log1p(jnp.exp(-jnp.abs(x)))


def _pack_halves(y):
    return pltpu.pack_elementwise([y[:, :HALF], y[:, HALF:]], packed_dtype=_BF16)


def _unpack_halves(p):
    lo = pltpu.unpack_elementwise(p, index=0, packed_dtype=_BF16, unpacked_dtype=_F32)
    hi = pltpu.unpack_elementwise(p, index=1, packed_dtype=_BF16, unpacked_dtype=_F32)
    return lo, hi


def _layer_norm(h, g, b):
    mu = jnp.mean(h, axis=-1, keepdims=True)
    d = h - mu
    var = jnp.mean(d * d, axis=-1, keepdims=True)
    return d * lax.rsqrt(var + NORM_EPS) * g + b


def _premix_body(x_ref, tails_ref, wa_ref, wq_ref, wz_ref, wbd_ref, wbdt_ref, cw_ref, cnw_ref,
                 gcw_ref, prow_ref, pcol_ref,
                 yc_ref, q_ref, k_ref, v_ref, z_ref, bgc_ref, bgr_ref, tout_ref, ext_ref, *, lt):
    cw_ = CONV_WIDTH

    @pl.when(pl.program_id(1) == 0)
    def _():
        ext_ref[0:HIST, :] = tails_ref[...]

    xb = x_ref[0].astype(_BF16)
    pa = jnp.dot(xb, wa_ref[...], preferred_element_type=_F32)
    gate_b = pa[:, 0:cw_]
    u = pa[:, cw_:2 * cw_] * pa[:, 2 * cw_:3 * cw_]
    ext_ref[HIST:HIST + lt, 0:cw_] = u
    pq = jnp.dot(xb, wq_ref[...], preferred_element_type=_F32)
    ext_ref[HIST:HIST + lt, cw_:] = pq

    cw = cw_ref[...]
    ca = u * cw[CONV_K - 1:CONV_K, :]
    for j in range(CONV_K - 1):
        ca = ca + ext_ref[pl.ds(HIST - (CONV_K - 1) + j, lt), 0:cw_] * cw[j:j + 1, :]
    yc = gate_b * ca
    ms = jnp.mean(yc * yc, axis=-1, keepdims=True)
    yc_ref[0] = (yc * lax.rsqrt(ms + NORM_EPS) * cnw_ref[...]).astype(_BF16)

    gcw = gcw_ref[...]
    cq = pq * gcw[GDN_CONV_K - 1:GDN_CONV_K, :]
    for j in range(GDN_CONV_K - 1):
        cq = cq + ext_ref[pl.ds(HIST - (GDN_CONV_K - 1) + j, lt), cw_:] * gcw[j:j + 1, :]
    s = _silu(cq)
    for h in range(GDN_HEADS):
        lo, hi = h * GDN_HEAD_DIM, (h + 1) * GDN_HEAD_DIM
        qh = s[:, lo:hi]
        kh = s[:, GDN_WIDTH + lo:GDN_WIDTH + hi]
        qn = qh * lax.rsqrt(jnp.sum(qh * qh, axis=-1, keepdims=True) + 1e-6)
        kn = kh * lax.rsqrt(jnp.sum(kh * kh, axis=-1, keepdims=True) + 1e-6)
        q_ref[0, :, lo:hi] = (qn * (GDN_HEAD_DIM ** -0.5)).astype(_BF16)
        k_ref[0, :, lo:hi] = kn.astype(_BF16)
    v_ref[0] = s[:, 2 * GDN_WIDTH:].astype(_BF16)
    z_ref[0] = jnp.dot(xb, wz_ref[...], preferred_element_type=_F32).astype(_BF16)

    bdc = jnp.dot(xb, wbd_ref[...], preferred_element_type=_F32)
    prow = prow_ref[...]
    g_c = -jnp.exp(prow[0:1, :]) * _softplus(bdc + prow[1:2, :])
    lane = lax.broadcasted_iota(_I32, bdc.shape, 1)
    bgc_ref[0] = jnp.where(lane < GDN_HEADS, _sigmoid(bdc), g_c)
    bdr = _mm_nt(wbdt_ref[...], xb)
    pcol = pcol_ref[...]
    g_r = -jnp.exp(pcol[:, 0:1]) * _softplus(bdr + pcol[:, 1:2])
    row = lax.broadcasted_iota(_I32, bdr.shape, 0)
    bgr_ref[0] = jnp.where(row < GDN_HEADS, _sigmoid(bdr), g_r)

    tail = ext_ref[lt:lt + HIST, :]
    ext_ref[0:HIST, :] = tail
    tout_ref[0] = tail


def _premix(x, tails, wts, *, lt):
    bsz, seq, d = x.shape
    assert seq % lt == 0
    grid = (bsz, seq // lt)
    full = lambda a: pl.BlockSpec(a.shape, lambda b, j: (0,) * a.ndim)
    tok = lambda w: pl.BlockSpec((1, lt, w), lambda b, j: (b, j, 0))
    (wa, wq, wz, wbd, wbdt, cw, cnw, gcw, prow, pcol) = wts
    ext_w = CONV_WIDTH + 3 * GDN_WIDTH
    out_shape = (
        jax.ShapeDtypeStruct((bsz, seq, CONV_WIDTH), _BF16),
        jax.ShapeDtypeStruct((bsz, seq, GDN_WIDTH), _BF16),
        jax.ShapeDtypeStruct((bsz, seq, GDN_WIDTH), _BF16),
        jax.ShapeDtypeStruct((bsz, seq, GDN_WIDTH), _BF16),
        jax.ShapeDtypeStruct((bsz, seq, GDN_WIDTH), _BF16),
        jax.ShapeDtypeStruct((bsz, seq, 128), _F32),
        jax.ShapeDtypeStruct((bsz, 8, seq), _F32),
        jax.ShapeDtypeStruct((bsz, HIST, ext_w), _F32),
    )
    out_specs = (tok(CONV_WIDTH), tok(GDN_WIDTH), tok(GDN_WIDTH), tok(GDN_WIDTH), tok(GDN_WIDTH),
                 tok(128), pl.BlockSpec((1, 8, lt), lambda b, j: (b, 0, j)),
                 pl.BlockSpec((1, HIST, ext_w), lambda b, j: (b, 0, 0)))
    return pl.pallas_call(
        functools.partial(_premix_body, lt=lt),
        grid=grid,
        in_specs=[tok(d), full(tails)] + [full(w) for w in wts],
        out_specs=out_specs,
        out_shape=out_shape,
        scratch_shapes=[pltpu.VMEM((HIST + lt, ext_w), _F32)],
        compiler_params=_cparams("arbitrary", "arbitrary"),
        name="premix",
    )(x, tails, *wts)


def _cumsum_rows(x):
    row = lax.broadcasted_iota(_I32, x.shape, 0)
    s = 1
    while s < x.shape[0]:
        x = x + jnp.where(row >= s, pltpu.roll(x, s, 0), 0.0)
        s *= 2
    return x


def _cumsum_lanes_seg(x):
    lane = lax.broadcasted_iota(_I32, x.shape, 1) & (CHUNK - 1)
    s = 1
    while s < CHUNK:
        x = x + jnp.where(lane >= s, pltpu.roll(x, s, 1), 0.0)
        s *= 2
    return x


def _stack_heads(a):
    return jnp.concatenate([a[:, h * GDN_HEAD_DIM:(h + 1) * GDN_HEAD_DIM] for h in range(GDN_HEADS)], axis=0)


def _gdn_body(q_ref, k_ref, v_ref, z_ref, bgc_ref, grow_ref, s0_ref, gnw_ref,
              y_ref, sout_ref, s_ref, *, nc, nbb):
    @pl.when(pl.program_id(1) == 0)
    def _():
        for r in range(nbb):
            s_ref[r] = s0_ref[...]

    ri = lax.broadcasted_iota(_I32, (STACK, STACK), 0)
    ci = lax.broadcasted_iota(_I32, (STACK, STACK), 1)
    same64 = (ri >> 6) == (ci >> 6)
    same32 = (ri >> 5) == (ci >> 5)
    same16 = (ri >> 4) == (ci >> 4)
    low_incl = same64 & (ri >= ci)
    low_strict = same64 & (ri > ci)
    gnw = gnw_ref[...]

    def chunk_row(r, c):
        off = pl.multiple_of(c * CHUNK, CHUNK)
        q_all = _stack_heads(q_ref[r, pl.ds(off, CHUNK), :].astype(_F32))
        k_all = _stack_heads(k_ref[r, pl.ds(off, CHUNK), :].astype(_F32))
        v_all = _stack_heads(v_ref[r, pl.ds(off, CHUNK), :].astype(_F32))
        bgc = bgc_ref[r, pl.ds(off, CHUNK), :]
        gcs = _cumsum_rows(bgc)
        hd = (CHUNK, GDN_HEAD_DIM)
        beta_b = jnp.concatenate(
            [jnp.broadcast_to(bgc[:, h:h + 1], hd) for h in range(GDN_HEADS)], axis=0)
        gc_b = jnp.concatenate(
            [jnp.broadcast_to(gcs[:, GDN_HEADS + h:GDN_HEADS + h + 1], hd) for h in range(GDN_HEADS)], axis=0)
        gl = [gcs[CHUNK - 1:CHUNK, GDN_HEADS + h:GDN_HEADS + h + 1] for h in range(GDN_HEADS)]
        gl_b = jnp.concatenate([jnp.broadcast_to(g1, hd) for g1 in gl], axis=0)
        gcr = _cumsum_lanes_seg(jnp.broadcast_to(grow_ref[r, c], (8, STACK)))[0:1, :]

        diff = jnp.concatenate([gc_b, gc_b], axis=1) - gcr
        decay = jnp.exp(jnp.where(low_incl, diff, -1e30))
        kb = k_all * beta_b
        a1 = _mm_nt(jnp.concatenate([kb, q_all], axis=0), k_all)
        yield
        m = jnp.where(low_strict, a1[:STACK] * decay, 0.0)
        attn = a1[STACK:] * decay

        l16 = jnp.where(same16, m, 0.0)
        c1 = jnp.where(same32 & jnp.logical_not(same16), m, 0.0)
        c2 = jnp.where(same32, 0.0, m)
        p2 = _mm(l16, l16)
        yield
        p4 = _mm(p2, p2)
        t = _mm(l16, p2)
        yield
        na = p2 - l16 - t
        p8 = _mm(p4, p4)
        t = _mm(na, p4)
        yield
        nb = na + p4 + t
        t = _mm(nb, p8)
        yield
        ncm = nb + p8 + t
        t = _mm(c1, ncm)
        yield
        y1 = c1 + t
        t = _mm(ncm, y1)
        yield
        n1 = ncm - y1 - t
        t = _mm(c2, n1)
        yield
        y2 = c2 + t
        t = _mm(n1, y2)
        yield
        nt = n1 - y2 - t

        egc = jnp.exp(gc_b)
        rhs = jnp.concatenate([v_all * beta_b, kb * egc], axis=1)
        t = _mm(nt, rhs)
        yield
        uw = rhs + t
        u_all = uw[:, :GDN_HEAD_DIM]
        w_all = uw[:, GDN_HEAD_DIM:]
        qd = q_all * egc
        kd = k_all * jnp.exp(gl_b - gc_b)

        bs = []
        for h in range(GDN_HEADS):
            r0, r1 = h * CHUNK, (h + 1) * CHUNK
            bs.append(_mm(jnp.concatenate([w_all[r0:r1], qd[r0:r1]], axis=0), s_ref[r, h]))
        yield
        vn = [u_all[h * CHUNK:(h + 1) * CHUNK] - bs[h][:CHUNK] for h in range(GDN_HEADS)]
        vn_all = jnp.concatenate(vn, axis=0)
        t = _mm(attn, vn_all)
        ds = [_mm_tn(kd[h * CHUNK:(h + 1) * CHUNK], vn[h]) for h in range(GDN_HEADS)]
        yield
        o_all = jnp.concatenate([b[CHUNK:] for b in bs], axis=0) + t
        for h in range(GDN_HEADS):
            r0, r1 = h * CHUNK, (h + 1) * CHUNK
            s_ref[r, h] = s_ref[r, h] * jnp.exp(gl[h]) + ds[h]
            o = o_all[r0:r1]
            zz = z_ref[r, pl.ds(off, CHUNK), h * GDN_HEAD_DIM:(h + 1) * GDN_HEAD_DIM].astype(_F32)
            on = o * lax.rsqrt(jnp.mean(o * o, axis=-1, keepdims=True) + NORM_EPS) * gnw
            y_ref[r, pl.ds(off, CHUNK), h * GDN_HEAD_DIM:(h + 1) * GDN_HEAD_DIM] = (on * _silu(zz)).astype(_BF16)

    def chunk(c, carry):
        live = [chunk_row(r, c) for r in range(nbb)]
        while live:
            live = [g for g in live if next(g, live) is not live]
        return carry

    lax.fori_loop(0, nc, chunk, 0)
    sout_ref[...] = s_ref[...]


def _gdn(q, k, v, z, bgc, grow, s0, gnw, *, lg, nbb):
    bsz, seq, _ = q.shape
    assert seq % lg == 0 and lg % CHUNK == 0 and bsz % nbb == 0
    nc = lg // CHUNK
    tok = lambda w: pl.BlockSpec((nbb, lg, w), lambda b, j: (b, j, 0))
    full = lambda a: pl.BlockSpec(a.shape, lambda b, j: (0,) * a.ndim)
    st = (nbb, GDN_HEADS, GDN_HEAD_DIM, GDN_HEAD_DIM)
    return pl.pallas_call(
        functools.partial(_gdn_body, nc=nc, nbb=nbb),
        grid=(bsz // nbb, seq // lg),
        in_specs=[tok(GDN_WIDTH)] * 4 + [tok(128), pl.BlockSpec((nbb, nc, 1, STACK), lambda b, j: (b, j, 0, 0)),
                                           full(s0), full(gnw)],
        out_specs=(tok(GDN_WIDTH), pl.BlockSpec(st, lambda b, j: (b, 0, 0, 0))),
        out_shape=(jax.ShapeDtypeStruct((bsz, seq, GDN_WIDTH), _BF16),
                   jax.ShapeDtypeStruct((bsz, GDN_HEADS, GDN_HEAD_DIM, GDN_HEAD_DIM), _F32)),
        scratch_shapes=[pltpu.VMEM(st, _F32)],
        compiler_params=_cparams("arbitrary", "arbitrary"),
        name="gdn",
    )(q, k, v, z, bgc, grow, s0, gnw)


def _outproj_body(yc_ref, yg_ref, x_ref, wo_ref, g_ref, b_ref, h1_ref, h1p_ref):
    mix = (jnp.dot(yc_ref[...], wo_ref[0:CONV_WIDTH, :], preferred_element_type=_F32)
           + jnp.dot(yg_ref[...], wo_ref[CONV_WIDTH:, :], preferred_element_type=_F32))
    h1 = _layer_norm(DN_ALPHA * x_ref[...] + mix, g_ref[...], b_ref[...])
    h1_ref[...] = h1
    h1p_ref[...] = _pack_halves(h1)


def _outproj(yc, yg, x2d, wo, g, b, *, tm):
    t = x2d.shape[0]
    assert t % tm == 0
    row = lambda w: pl.BlockSpec((tm, w), lambda i: (i, 0))
    full = lambda a: pl.BlockSpec(a.shape, lambda i: (0,) * a.ndim)
    return pl.pallas_call(
        _outproj_body,
        grid=(t // tm,),
        in_specs=[row(CONV_WIDTH), row(GDN_WIDTH), row(D_MODEL), full(wo), full(g), full(b)],
        out_specs=(row(D_MODEL), row(HALF)),
        out_shape=(jax.ShapeDtypeStruct((t, D_MODEL), _F32), jax.ShapeDtypeStruct((t, HALF), jnp.uint32)),
        compiler_params=_cparams("arbitrary"),
        name="outproj",
    )(yc, yg, x2d, wo, g, b)


def _router_body(h1_ref, wh_ref, wl_ref, br_ref, idx_ref, gate_ref, rank_ref, cnt_ref, carry_ref, *, tt):
    @pl.when(pl.program_id(0) == 0)
    def _():
        carry_ref[...] = jnp.zeros_like(carry_ref)

    x = h1_ref[...]
    xh = x.astype(_BF16)
    xl = (x - xh.astype(_F32)).astype(_BF16)
    wh = wh_ref[...]
    logits = _mm_nt(wh, xh) + _mm_nt(wh, xl) + _mm_nt(wl_ref[...], xh)
    scores = _sigmoid(logits)
    sel = scores + br_ref[...]
    ninf = -jnp.inf

    r32 = lax.broadcasted_iota(_I32, (E_PER_GROUP, tt), 0)
    gsc = []
    for g in range(N_GROUPS):
        xg = sel[g * E_PER_GROUP:(g + 1) * E_PER_GROUP]
        m1 = jnp.max(xg, axis=0, keepdims=True)
        i1 = jnp.min(jnp.where(xg == m1, r32, E_PER_GROUP), axis=0, keepdims=True)
        m2 = jnp.max(jnp.where(r32 == i1, ninf, xg), axis=0, keepdims=True)
        gsc.append(m1 + m2)
    work = jnp.concatenate(gsc, axis=0)
    r8 = lax.broadcasted_iota(_I32, (N_GROUPS, tt), 0)
    gkeep = jnp.zeros((N_GROUPS, tt), _F32)
    for _ in range(TOPK_GROUPS):
        m = jnp.max(work, axis=0, keepdims=True)
        gi = jnp.min(jnp.where(work == m, r8, N_GROUPS), axis=0, keepdims=True)
        pick = r8 == gi
        gkeep = jnp.where(pick, 1.0, gkeep)
        work = jnp.where(pick, ninf, work)
    selm = jnp.concatenate(
        [jnp.where(gkeep[g:g + 1] > 0.5, sel[g * E_PER_GROUP:(g + 1) * E_PER_GROUP], ninf)
         for g in range(N_GROUPS)], axis=0)

    re = lax.broadcasted_iota(_I32, (N_EXPERTS, tt), 0)
    msel = jnp.zeros((N_EXPERTS, tt), _F32)
    idxs, gates = [], []
    for _ in range(TOP_K):
        m = jnp.max(selm, axis=0, keepdims=True)
        ii = jnp.min(jnp.where(selm == m, re, N_EXPERTS), axis=0, keepdims=True)
        hit = re == ii
        idxs.append(ii)
        gates.append(jnp.sum(jnp.where(hit, scores, 0.0), axis=0, keepdims=True))
        selm = jnp.where(hit, ninf, selm)
        msel = jnp.where(hit, 1.0, msel)
    gate = jnp.concatenate(gates, axis=0)
    gate_ref[...] = gate / jnp.sum(gate, axis=0, keepdims=True) * ROUTED_SCALE
    idx_ref[...] = jnp.concatenate(idxs, axis=0)

    ta = lax.broadcasted_iota(_I32, (tt, tt), 0)
    tb = lax.broadcasted_iota(_I32, (tt, tt), 1)
    earlier = jnp.where(ta < tb, 1.0, 0.0)
    carry = carry_ref[...]
    rank_all = _mm(msel, earlier) + carry[:, 0:1]
    rank_ref[...] = jnp.concatenate(
        [jnp.sum(jnp.where(re == ii, rank_all, 0.0), axis=0, keepdims=True) for ii in idxs],
        axis=0).astype(_I32)
    carry = carry + jnp.sum(msel, axis=1, keepdims=True)
    carry_ref[...] = carry
    cnt_ref[...] = carry


def _router(h1, wh, wl, br, *, tt):
    t = h1.shape[0]
    assert t % tt == 0
    full = lambda a: pl.BlockSpec(a.shape, lambda i: (0,) * a.ndim)
    kt = pl.BlockSpec((TOP_K, tt), lambda i: (0, i))
    return pl.pallas_call(
        functools.partial(_router_body, tt=tt),
        grid=(t // tt,),
        in_specs=[pl.BlockSpec((tt, D_MODEL), lambda i: (i, 0)), full(wh), full(wl), full(br)],
        out_specs=(kt, kt, kt, pl.BlockSpec((N_EXPERTS, 128), lambda i: (0, 0))),
        out_shape=(jax.ShapeDtypeStruct((TOP_K, t), _I32), jax.ShapeDtypeStruct((TOP_K, t), _F32),
                   jax.ShapeDtypeStruct((TOP_K, t), _I32), jax.ShapeDtypeStruct((N_EXPERTS, 128), _F32)),
        scratch_shapes=[pltpu.VMEM((N_EXPERTS, 128), _F32)],
        compiler_params=_cparams("arbitrary"),
        name="router",
    )(h1, wh, wl, br)


def _position_body(idx_ref, rank_ref, pstart_ref, pos_ref, *, tt):
    re = lax.broadcasted_iota(_I32, (N_EXPERTS, tt), 0)
    ps = pstart_ref[...]
    idx = idx_ref[...]
    rows = [jnp.sum(jnp.where(re == idx[k:k + 1], ps, 0), axis=0, keepdims=True) for k in range(TOP_K)]
    pos_ref[0] = jnp.concatenate(rows, axis=0) + rank_ref[...]


def _position(idx, rank, pstart, *, tt):
    t = idx.shape[1]
    kt = pl.BlockSpec((TOP_K, tt), lambda i: (0, i))
    return pl.pallas_call(
        functools.partial(_position_body, tt=tt),
        grid=(t // tt,),
        in_specs=[kt, kt, pl.BlockSpec(pstart.shape, lambda i: (0, 0))],
        out_specs=pl.BlockSpec((1, TOP_K, tt), lambda i: (i, 0, 0)),
        out_shape=jax.ShapeDtypeStruct((t // tt, TOP_K, tt), _I32),
        compiler_params=_cparams("arbitrary"),
        name="position",
    )(idx, rank, pstart)


def _dispatch_body(pos_hbm, h1p_ref, xs_in, xs_out, pos_smem, psem, sem, *, tt):
    del xs_in
    i = pl.program_id(0)
    cp = pltpu.make_async_copy(pos_hbm.at[i], pos_smem, psem)
    cp.start()
    cp.wait()

    def row_copy(t, k):
        return pltpu.make_async_copy(h1p_ref.at[pl.ds(t, 1)], xs_out.at[pl.ds(pos_smem[k * tt + t], 1)], sem)

    def issue(t, c):
        for k in range(TOP_K):
            row_copy(t, k).start(priority=k % 2)
        return c

    lax.fori_loop(0, tt, issue, 0)

    def drain(t, c):
        for k in range(TOP_K):
            row_copy(t, k).wait()
        return c

    lax.fori_loop(0, tt, drain, 0)


def _dispatch(pos_tiles, h1p, xs_zero, *, tt):
    t = h1p.shape[0]
    return pl.pallas_call(
        functools.partial(_dispatch_body, tt=tt),
        grid=(t // tt,),
        in_specs=[pl.BlockSpec(memory_space=pl.ANY), pl.BlockSpec((tt, HALF), lambda i: (i, 0)),
                  pl.BlockSpec(memory_space=pl.ANY)],
        out_specs=pl.BlockSpec(memory_space=pl.ANY),
        out_shape=jax.ShapeDtypeStruct(xs_zero.shape, xs_zero.dtype),
        scratch_shapes=[pltpu.SMEM((TOP_K * tt,), _I32), pltpu.SemaphoreType.DMA, pltpu.SemaphoreType.DMA],
        input_output_aliases={2: 0},
        compiler_params=_cparams("arbitrary"),
        name="dispatch",
    )(pos_tiles, h1p, xs_zero)


def _ffn_body(row0_ref, nblk_ref, xs_hbm, wg_ref, wu_ref, wd_ref, ys_hbm,
              xbuf, ybuf, sem_in, sem_out, wgu_bf, wd_bf):
    e = pl.program_id(0)
    nblk = nblk_ref[e]
    row0 = row0_ref[e]

    def rows(j):
        return pl.ds(pl.multiple_of(row0 + j * ROW_BLOCK, ROW_BLOCK), ROW_BLOCK)

    def in_copy(j, slot):
        return pltpu.make_async_copy(xs_hbm.at[rows(j)], xbuf.at[slot], sem_in.at[slot])

    def out_copy(j, slot):
        return pltpu.make_async_copy(ybuf.at[slot], ys_hbm.at[rows(j)], sem_out.at[slot])

    @pl.when(nblk > 0)
    def _():
        in_copy(0, 0).start()
        wgu_bf[:, 0:EXPERT_FF] = wg_ref[0].astype(_BF16)
        wgu_bf[:, EXPERT_FF:] = wu_ref[0].astype(_BF16)
        wd_bf[...] = wd_ref[0].astype(_BF16)

        def block(j, carry):
            slot = j & 1
            in_copy(j, slot).wait()

            @pl.when(j + 1 < nblk)
            def _():
                in_copy(j + 1, 1 - slot).start()

            @pl.when(j >= 2)
            def _():
                out_copy(j - 2, slot).wait()

            lo, hi = _unpack_halves(xbuf[slot])
            gu = (jnp.dot(lo.astype(_BF16), wgu_bf[0:HALF, :], preferred_element_type=_F32)
                  + jnp.dot(hi.astype(_BF16), wgu_bf[HALF:, :], preferred_element_type=_F32))
            h = (_silu(gu[:, :EXPERT_FF]) * gu[:, EXPERT_FF:]).astype(_BF16)
            ybuf[slot] = _pack_halves(jnp.dot(h, wd_bf[...], preferred_element_type=_F32))
            out_copy(j, slot).start()
            return carry

        lax.fori_loop(0, nblk, block, 0)

        @pl.when(nblk >= 2)
        def _():
            out_copy(nblk - 2, nblk & 1).wait()

        out_copy(nblk - 1, (nblk - 1) & 1).wait()


def _ffn(row0, nblk, xs, wg, wu, wd):
    grid_spec = pltpu.PrefetchScalarGridSpec(
        num_scalar_prefetch=2,
        grid=(N_EXPERTS,),
        in_specs=[pl.BlockSpec(memory_space=pl.ANY),
                  pl.BlockSpec((1, D_MODEL, EXPERT_FF), lambda e, r0, nb: (e, 0, 0)),
                  pl.BlockSpec((1, D_MODEL, EXPERT_FF), lambda e, r0, nb: (e, 0, 0)),
                  pl.BlockSpec((1, EXPERT_FF, D_MODEL), lambda e, r0, nb: (e, 0, 0))],
        out_specs=pl.BlockSpec(memory_space=pl.ANY),
        scratch_shapes=[pltpu.VMEM((2, ROW_BLOCK, HALF), jnp.uint32), pltpu.VMEM((2, ROW_BLOCK, HALF), jnp.uint32),
                        pltpu.SemaphoreType.DMA((2,)), pltpu.SemaphoreType.DMA((2,)),
                        pltpu.VMEM((D_MODEL, 2 * EXPERT_FF), _BF16), pltpu.VMEM((EXPERT_FF, D_MODEL), _BF16)],
    )
    return pl.pallas_call(
        _ffn_body,
        grid_spec=grid_spec,
        out_shape=jax.ShapeDtypeStruct(xs.shape, jnp.uint32),
        compiler_params=_cparams("arbitrary"),
        name="ffn",
    )(row0, nblk, xs, wg, wu, wd)


def _combine_body(pos_hbm, gate_ref, h1_ref, ys_hbm, wsg_ref, wsu_ref, wsd_ref, g_ref, b_ref,
                  out_ref, pos_smem, psem, ybuf, sem, *, tt):
    i = pl.program_id(0)
    cp = pltpu.make_async_copy(pos_hbm.at[i], pos_smem, psem)
    cp.start()
    cp.wait()

    def row_copy(t, k):
        return pltpu.make_async_copy(ys_hbm.at[pl.ds(pos_smem[k * tt + t], 1)], ybuf.at[k, pl.ds(t, 1)], sem)

    def issue(t, c):
        for k in range(TOP_K):
            row_copy(t, k).start(priority=k % 2)
        return c

    lax.fori_loop(0, tt, issue, 0)

    x = h1_ref[...]
    xb = x.astype(_BF16)
    shared = _mm(_silu(_mm(xb, wsg_ref[...])) * _mm(xb, wsu_ref[...]), wsd_ref[...])

    def drain(t, c):
        for k in range(TOP_K):
            row_copy(t, k).wait()
        return c

    lax.fori_loop(0, tt, drain, 0)

    gcol = gate_ref[...].T
    acc_lo = jnp.zeros((tt, HALF), _F32)
    acc_hi = jnp.zeros((tt, HALF), _F32)
    for k in range(TOP_K):
        lo, hi = _unpack_halves(ybuf[k])
        acc_lo = acc_lo + gcol[:, k:k + 1] * lo
        acc_hi = acc_hi + gcol[:, k:k + 1] * hi
    routed = jnp.concatenate([acc_lo, acc_hi], axis=1)
    out_ref[...] = _layer_norm(DN_ALPHA * x + (routed + shared), g_ref[...], b_ref[...])


def _combine(pos_tiles, gate, h1, ys, wsg, wsu, wsd, g, b, *, tt):
    t = h1.shape[0]
    full = lambda a: pl.BlockSpec(a.shape, lambda i: (0,) * a.ndim)
    return pl.pallas_call(
        functools.partial(_combine_body, tt=tt),
        grid=(t // tt,),
        in_specs=[pl.BlockSpec(memory_space=pl.ANY), pl.BlockSpec((TOP_K, tt), lambda i: (0, i)),
                  pl.BlockSpec((tt, D_MODEL), lambda i: (i, 0)), pl.BlockSpec(memory_space=pl.ANY),
                  full(wsg), full(wsu), full(wsd), full(g), full(b)],
        out_specs=pl.BlockSpec((tt, D_MODEL), lambda i: (i, 0)),
        out_shape=jax.ShapeDtypeStruct((t, D_MODEL), _F32),
        scratch_shapes=[pltpu.SMEM((TOP_K * tt,), _I32), pltpu.SemaphoreType.DMA,
                        pltpu.VMEM((TOP_K, tt, HALF), jnp.uint32), pltpu.SemaphoreType.DMA],
        compiler_params=_cparams("arbitrary"),
        name="combine",
    )(pos_tiles, gate, h1, ys, wsg, wsu, wsd, g, b)


def _pick(n, pref):
    t = min(n, pref)
    while n % t:
        t -= CHUNK
    return t


def _mixer(x, tails, s0, wts, gnw, *, lt, lg, nbb):
    yc, q, k, v, z, bgc, bgr, tails_out = _premix(x, tails, wts, lt=lt)
    bsz, seq, _ = x.shape
    nch = seq // CHUNK
    grow = bgr[:, GDN_HEADS:2 * GDN_HEADS, :].reshape(bsz, GDN_HEADS, nch, CHUNK)
    grow = grow.transpose(0, 2, 1, 3).reshape(bsz, nch, 1, STACK)
    yg, s_out = _gdn(q, k, v, z, bgc, grow, s0, gnw, lg=lg, nbb=nbb)
    return yc, yg, tails_out, s_out


def kernel(x, meta_tokens, w_in, conv_w, conv_norm_w, gdn_conv_w, a_log, dt_bias, gdn_norm_w, w_out,
           ln1_g, ln1_b, w_router, b_router, w_gate, w_up, w_down, ws_gate, ws_up, ws_down, ln2_g, ln2_b):
    assert w_in.shape[0] == 1, "single-layer stack"
    bsz, seq, d = x.shape
    assert d == D_MODEL and seq % CHUNK == 0
    c, gw = CONV_WIDTH, GDN_WIDTH
    win = w_in[0].astype(_BF16)
    wbd = win[:, 3 * c + 4 * gw:]
    zpad = jnp.zeros((128 - 2 * GDN_HEADS,), _F32)
    zpad4 = jnp.zeros((GDN_HEADS,), _F32)
    prow = jnp.zeros((8, 128), _F32)
    prow = prow.at[0].set(jnp.concatenate([zpad4, a_log[0], zpad]))
    prow = prow.at[1].set(jnp.concatenate([zpad4, dt_bias[0], zpad]))
    wts = (win[:, :3 * c], win[:, 3 * c:3 * c + 3 * gw], win[:, 3 * c + 3 * gw:3 * c + 4 * gw],
           jnp.pad(wbd, ((0, 0), (0, 128 - 2 * GDN_HEADS))), wbd.T,
           conv_w[0], conv_norm_w, gdn_conv_w[0], prow, prow.T[:8])
    gnw = gdn_norm_w

    meta = jnp.concatenate([jnp.zeros((CHUNK - N_META, d), x.dtype), meta_tokens.astype(x.dtype)])[None]
    tails0 = jnp.zeros((HIST, c + 3 * gw), _F32)
    s00 = jnp.zeros((GDN_HEADS, GDN_HEAD_DIM, GDN_HEAD_DIM), _F32)
    _, _, tails_m, s_m = _mixer(meta, tails0, s00, wts, gnw, lt=CHUNK, lg=CHUNK, nbb=1)

    yc, yg, _, _ = _mixer(x, tails_m[0], s_m[0], wts, gnw, lt=_pick(seq, 512), lg=_pick(seq, 512),
                          nbb=GDN_ROWS if bsz % GDN_ROWS == 0 else 1)

    t = bsz * seq
    tm = _pick(t, 512)
    h1, h1p = _outproj(yc.reshape(t, c), yg.reshape(t, gw), x.reshape(t, d), w_out[0].astype(_BF16),
                       ln1_g, ln1_b, tm=tm)

    tt = _pick(t, 256)
    wr_t = w_router[0].T
    wr_hi = wr_t.astype(_BF16)
    wr_lo = (wr_t - wr_hi.astype(_F32)).astype(_BF16)
    idx, gate, rank, cnt = _router(h1, wr_hi, wr_lo, b_router[0][:, None], tt=tt)

    counts = cnt[:, 0].astype(_I32)
    pcounts = (counts + ROW_BLOCK - 1) // ROW_BLOCK * ROW_BLOCK
    pends = jnp.cumsum(pcounts)
    pstarts = pends - pcounts
    nb = t * TOP_K // ROW_BLOCK + N_EXPERTS

    pos = _position(idx, rank, pstarts[:, None].astype(_I32), tt=tt).reshape(t // tt, TOP_K * tt)
    xs = _dispatch(pos, h1p, jnp.zeros((nb * ROW_BLOCK, HALF), jnp.uint32), tt=tt)
    ys = _ffn(pstarts.astype(_I32), (pcounts // ROW_BLOCK).astype(_I32), xs, w_gate[0], w_up[0], w_down[0])
    out = _combine(pos, gate, h1, ys, ws_gate[0].astype(_BF16), ws_up[0].astype(_BF16),
                   ws_down[0].astype(_BF16), ln2_g, ln2_b, tt=tt)
    return out.reshape(bsz, seq, d)
```

```python
import functools

import jax
import jax.numpy as jnp
from jax import lax
from jax.experimental import pallas as pl
from jax.experimental.pallas import tpu as pltpu

_F32 = jnp.float32
_BF16 = jnp.bfloat16
_I32 = jnp.int32

D_MODEL = 1024
N_META = 16
CONV_WIDTH = 512
CONV_K = 3
GDN_HEADS = 4
GDN_HEAD_DIM = 128
GDN_WIDTH = GDN_HEADS * GDN_HEAD_DIM
GDN_CONV_K = 4
CHUNK = 64
N_EXPERTS = 256
TOP_K = 8
N_GROUPS = 8
TOPK_GROUPS = 4
E_PER_GROUP = N_EXPERTS // N_GROUPS
EXPERT_FF = 256
ROUTED_SCALE = 2.5
ROW_BLOCK = 256
DN_ALPHA = 2.0 ** 0.25
NORM_EPS = 1e-5
HALF = D_MODEL // 2
STACK = GDN_HEADS * CHUNK
HIST = 8
GDN_ROWS = 4
DMA_SPLIT = 8
ISSUE_UNROLL = 8

V7X_VMEM_BYTES = 64 * 1024 * 1024
VMEM_LIMIT = V7X_VMEM_BYTES - 8 * 1024 * 1024


def _cparams(*sem):
    return pltpu.CompilerParams(dimension_semantics=sem, vmem_limit_bytes=VMEM_LIMIT)


def _mm(a, b):
    return jnp.dot(a.astype(_BF16), b.astype(_BF16), preferred_element_type=_F32)


def _mm_nt(a, b):
    return lax.dot_general(a.astype(_BF16), b.astype(_BF16), (((1,), (1,)), ((), ())),
                           preferred_element_type=_F32)


def _mm_tn(a, b):
    return lax.dot_general(a.astype(_BF16), b.astype(_BF16), (((0,), (0,)), ((), ())),
                           preferred_element_type=_F32)


def _sigmoid(x):
    return 1.0 / (1.0 + jnp.exp(-x))


def _silu(x):
    return x * _sigmoid(x)


def _softplus(x):
    return jnp.maximum(x, 0.0) + jnp.log1p(jnp.exp(-jnp.abs(x)))


def _pack_halves(y):
    return pltpu.pack_elementwise([y[:, :HALF], y[:, HALF:]], packed_dtype=_BF16)


def _unpack_halves(p):
    lo = pltpu.unpack_elementwise(p, index=0, packed_dtype=_BF16, unpacked_dtype=_F32)
    hi = pltpu.unpack_elementwise(p, index=1, packed_dtype=_BF16, unpacked_dtype=_F32)
    return lo, hi


def _layer_norm(h, g, b):
    mu = jnp.mean(h, axis=-1, keepdims=True)
    d = h - mu
    var = jnp.mean(d * d, axis=-1, keepdims=True)
    return d * lax.rsqrt(var + NORM_EPS) * g + b


def _premix_body(x_ref, tails_ref, wa_ref, wq_ref, wz_ref, wbd_ref, wbdt_ref, cw_ref, cnw_ref,
                 gcw_ref, prow_ref, pcol_ref,
                 yc_ref, q_ref, k_ref, v_ref, z_ref, bgc_ref, bgr_ref, tout_ref, ext_ref, *, lt):
    cw_ = CONV_WIDTH

    @pl.when(pl.program_id(1) == 0)
    def _():
        ext_ref[0:HIST, :] = tails_ref[...]

    xb = x_ref[0].astype(_BF16)
    pa = jnp.dot(xb, wa_ref[...], preferred_element_type=_F32)
    gate_b = pa[:, 0:cw_]
    u = pa[:, cw_:2 * cw_] * pa[:, 2 * cw_:3 * cw_]
    ext_ref[HIST:HIST + lt, 0:cw_] = u
    pq = jnp.dot(xb, wq_ref[...], preferred_element_type=_F32)
    ext_ref[HIST:HIST + lt, cw_:] = pq

    cw = cw_ref[...]
    ca = u * cw[CONV_K - 1:CONV_K, :]
    for j in range(CONV_K - 1):
        ca = ca + ext_ref[pl.ds(HIST - (CONV_K - 1) + j, lt), 0:cw_] * cw[j:j + 1, :]
    yc = gate_b * ca
    ms = jnp.mean(yc * yc, axis=-1, keepdims=True)
    yc_ref[0] = (yc * lax.rsqrt(ms + NORM_EPS) * cnw_ref[...]).astype(_BF16)

    gcw = gcw_ref[...]
    cq = pq * gcw[GDN_CONV_K - 1:GDN_CONV_K, :]
    for j in range(GDN_CONV_K - 1):
        cq = cq + ext_ref[pl.ds(HIST - (GDN_CONV_K - 1) + j, lt), cw_:] * gcw[j:j + 1, :]
    s = _silu(cq)
    for h in range(GDN_HEADS):
        lo, hi = h * GDN_HEAD_DIM, (h + 1) * GDN_HEAD_DIM
        qh = s[:, lo:hi]
        kh = s[:, GDN_WIDTH + lo:GDN_WIDTH + hi]
        qn = qh * lax.rsqrt(jnp.sum(qh * qh, axis=-1, keepdims=True) + 1e-6)
        kn = kh * lax.rsqrt(jnp.sum(kh * kh, axis=-1, keepdims=True) + 1e-6)
        q_ref[0, :, lo:hi] = (qn * (GDN_HEAD_DIM ** -0.5)).astype(_BF16)
        k_ref[0, :, lo:hi] = kn.astype(_BF16)
    v_ref[0] = s[:, 2 * GDN_WIDTH:].astype(_BF16)
    z_ref[0] = jnp.dot(xb, wz_ref[...], preferred_element_type=_F32).astype(_BF16)

    bdc = jnp.dot(xb, wbd_ref[...], preferred_element_type=_F32)
    prow = prow_ref[...]
    g_c = -jnp.exp(prow[0:1, :]) * _softplus(bdc + prow[1:2, :])
    lane = lax.broadcasted_iota(_I32, bdc.shape, 1)
    bgc_ref[0] = jnp.where(lane < GDN_HEADS, _sigmoid(bdc), g_c)
    bdr = _mm_nt(wbdt_ref[...], xb)
    pcol = pcol_ref[...]
    g_r = -jnp.exp(pcol[:, 0:1]) * _softplus(bdr + pcol[:, 1:2])
    row = lax.broadcasted_iota(_I32, bdr.shape, 0)
    bgr_ref[0] = jnp.where(row < GDN_HEADS, _sigmoid(bdr), g_r)

    tail = ext_ref[lt:lt + HIST, :]
    ext_ref[0:HIST, :] = tail
    tout_ref[0] = tail


def _premix(x, tails, wts, *, lt):
    bsz, seq, d = x.shape
    assert seq % lt == 0
    grid = (bsz, seq // lt)
    full = lambda a: pl.BlockSpec(a.shape, lambda b, j: (0,) * a.ndim)
    tok = lambda w: pl.BlockSpec((1, lt, w), lambda b, j: (b, j, 0))
    (wa, wq, wz, wbd, wbdt, cw, cnw, gcw, prow, pcol) = wts
    ext_w = CONV_WIDTH + 3 * GDN_WIDTH
    out_shape = (
        jax.ShapeDtypeStruct((bsz, seq, CONV_WIDTH), _BF16),
        jax.ShapeDtypeStruct((bsz, seq, GDN_WIDTH), _BF16),
        jax.ShapeDtypeStruct((bsz, seq, GDN_WIDTH), _BF16),
        jax.ShapeDtypeStruct((bsz, seq, GDN_WIDTH), _BF16),
        jax.ShapeDtypeStruct((bsz, seq, GDN_WIDTH), _BF16),
        jax.ShapeDtypeStruct((bsz, seq, 128), _F32),
        jax.ShapeDtypeStruct((bsz, 8, seq), _F32),
        jax.ShapeDtypeStruct((bsz, HIST, ext_w), _F32),
    )
    out_specs = (tok(CONV_WIDTH), tok(GDN_WIDTH), tok(GDN_WIDTH), tok(GDN_WIDTH), tok(GDN_WIDTH),
                 tok(128), pl.BlockSpec((1, 8, lt), lambda b, j: (b, 0, j)),
                 pl.BlockSpec((1, HIST, ext_w), lambda b, j: (b, 0, 0)))
    return pl.pallas_call(
        functools.partial(_premix_body, lt=lt),
        grid=grid,
        in_specs=[tok(d), full(tails)] + [full(w) for w in wts],
        out_specs=out_specs,
        out_shape=out_shape,
        scratch_shapes=[pltpu.VMEM((HIST + lt, ext_w), _F32)],
        compiler_params=_cparams("arbitrary", "arbitrary"),
        name="premix",
    )(x, tails, *wts)


def _cumsum_rows(x):
    row = lax.broadcasted_iota(_I32, x.shape, 0)
    s = 1
    while s < x.shape[0]:
        x = x + jnp.where(row >= s, pltpu.roll(x, s, 0), 0.0)
        s *= 2
    return x


def _cumsum_lanes_seg(x):
    lane = lax.broadcasted_iota(_I32, x.shape, 1) & (CHUNK - 1)
    s = 1
    while s < CHUNK:
        x = x + jnp.where(lane >= s, pltpu.roll(x, s, 1), 0.0)
        s *= 2
    return x


def _stack_heads(a):
    return jnp.concatenate([a[:, h * GDN_HEAD_DIM:(h + 1) * GDN_HEAD_DIM] for h in range(GDN_HEADS)], axis=0)


def _gdn_body(q_ref, k_ref, v_ref, z_ref, bgc_ref, grow_ref, s0_ref, gnw_ref,
              y_ref, sout_ref, s_ref, *, nc, nbb):
    @pl.when(pl.program_id(1) == 0)
    def _():
        for r in range(nbb):
            s_ref[r] = s0_ref[...]

    ri = lax.broadcasted_iota(_I32, (STACK, STACK), 0)
    ci = lax.broadcasted_iota(_I32, (STACK, STACK), 1)
    same64 = (ri >> 6) == (ci >> 6)
    same32 = (ri >> 5) == (ci >> 5)
    same16 = (ri >> 4) == (ci >> 4)
    low_incl = same64 & (ri >= ci)
    low_strict = same64 & (ri > ci)
    gnw = gnw_ref[...]

    def chunk_row(r, c):
        off = pl.multiple_of(c * CHUNK, CHUNK)
        q_all = _stack_heads(q_ref[r, pl.ds(off, CHUNK), :].astype(_F32))
        k_all = _stack_heads(k_ref[r, pl.ds(off, CHUNK), :].astype(_F32))
        v_all = _stack_heads(v_ref[r, pl.ds(off, CHUNK), :].astype(_F32))
        bgc = bgc_ref[r, pl.ds(off, CHUNK), :]
        gcs = _cumsum_rows(bgc)
        hd = (CHUNK, GDN_HEAD_DIM)
        beta_b = jnp.concatenate(
            [jnp.broadcast_to(bgc[:, h:h + 1], hd) for h in range(GDN_HEADS)], axis=0)
        gc_b = jnp.concatenate(
            [jnp.broadcast_to(gcs[:, GDN_HEADS + h:GDN_HEADS + h + 1], hd) for h in range(GDN_HEADS)], axis=0)
        gl = [gcs[CHUNK - 1:CHUNK, GDN_HEADS + h:GDN_HEADS + h + 1] for h in range(GDN_HEADS)]
        gl_b = jnp.concatenate([jnp.broadcast_to(g1, hd) for g1 in gl], axis=0)
        gcr = _cumsum_lanes_seg(jnp.broadcast_to(grow_ref[r, c], (8, STACK)))[0:1, :]

        diff = jnp.concatenate([gc_b, gc_b], axis=1) - gcr
        decay = jnp.exp(jnp.where(low_incl, diff, -1e30))
        kb = k_all * beta_b
        a1 = _mm_nt(jnp.concatenate([kb, q_all], axis=0), k_all)
        yield
        m = jnp.where(low_strict, a1[:STACK] * decay, 0.0)
        attn = a1[STACK:] * decay

        l16 = jnp.where(same16, m, 0.0)
        c1 = jnp.where(same32 & jnp.logical_not(same16), m, 0.0)
        c2 = jnp.where(same32, 0.0, m)
        p2 = _mm(l16, l16)
        yield
        p4 = _mm(p2, p2)
        t = _mm(l16, p2)
        yield
        na = p2 - l16 - t
        p8 = _mm(p4, p4)
        t = _mm(na, p4)
        yield
        nb = na + p4 + t
        t = _mm(nb, p8)
        yield
        ncm = nb + p8 + t
        t = _mm(c1, ncm)
        yield
        y1 = c1 + t
        t = _mm(ncm, y1)
        yield
        n1 = ncm - y1 - t
        t = _mm(c2, n1)
        yield
        y2 = c2 + t
        t = _mm(n1, y2)
        yield
        nt = n1 - y2 - t

        egc = jnp.exp(gc_b)
        rhs = jnp.concatenate([v_all * beta_b, kb * egc], axis=1)
        t = _mm(nt, rhs)
        yield
        uw = rhs + t
        u_all = uw[:, :GDN_HEAD_DIM]
        w_all = uw[:, GDN_HEAD_DIM:]
        qd = q_all * egc
        kd = k_all * jnp.exp(gl_b - gc_b)

        bs = []
        for h in range(GDN_HEADS):
            r0, r1 = h * CHUNK, (h + 1) * CHUNK
            bs.append(_mm(jnp.concatenate([w_all[r0:r1], qd[r0:r1]], axis=0), s_ref[r, h]))
        yield
        vn = [u_all[h * CHUNK:(h + 1) * CHUNK] - bs[h][:CHUNK] for h in range(GDN_HEADS)]
        vn_all = jnp.concatenate(vn, axis=0)
        t = _mm(attn, vn_all)
        ds = [_mm_tn(kd[h * CHUNK:(h + 1) * CHUNK], vn[h]) for h in range(GDN_HEADS)]
        yield
        o_all = jnp.concatenate([b[CHUNK:] for b in bs], axis=0) + t
        for h in range(GDN_HEADS):
            r0, r1 = h * CHUNK, (h + 1) * CHUNK
            s_ref[r, h] = s_ref[r, h] * jnp.exp(gl[h]) + ds[h]
            o = o_all[r0:r1]
            zz = z_ref[r, pl.ds(off, CHUNK), h * GDN_HEAD_DIM:(h + 1) * GDN_HEAD_DIM].astype(_F32)
            on = o * lax.rsqrt(jnp.mean(o * o, axis=-1, keepdims=True) + NORM_EPS) * gnw
            y_ref[r, pl.ds(off, CHUNK), h * GDN_HEAD_DIM:(h + 1) * GDN_HEAD_DIM] = (on * _silu(zz)).astype(_BF16)

    def chunk(c, carry):
        live = [chunk_row(r, c) for r in range(nbb)]
        while live:
            live = [g for g in live if next(g, live) is not live]
        return carry

    lax.fori_loop(0, nc, chunk, 0)
    sout_ref[...] = s_ref[...]


def _gdn(q, k, v, z, bgc, grow, s0, gnw, *, lg, nbb):
    bsz, seq, _ = q.shape
    assert seq % lg == 0 and lg % CHUNK == 0 and bsz % nbb == 0
    nc = lg // CHUNK
    tok = lambda w: pl.BlockSpec((nbb, lg, w), lambda b, j: (b, j, 0))
    full = lambda a: pl.BlockSpec(a.shape, lambda b, j: (0,) * a.ndim)
    st = (nbb, GDN_HEADS, GDN_HEAD_DIM, GDN_HEAD_DIM)
    return pl.pallas_call(
        functools.partial(_gdn_body, nc=nc, nbb=nbb),
        grid=(bsz // nbb, seq // lg),
        in_specs=[tok(GDN_WIDTH)] * 4 + [tok(128), pl.BlockSpec((nbb, nc, 1, STACK), lambda b, j: (b, j, 0, 0)),
                                           full(s0), full(gnw)],
        out_specs=(tok(GDN_WIDTH), pl.BlockSpec(st, lambda b, j: (b, 0, 0, 0))),
        out_shape=(jax.ShapeDtypeStruct((bsz, seq, GDN_WIDTH), _BF16),
                   jax.ShapeDtypeStruct((bsz, GDN_HEADS, GDN_HEAD_DIM, GDN_HEAD_DIM), _F32)),
        scratch_shapes=[pltpu.VMEM(st, _F32)],
        compiler_params=_cparams("arbitrary", "arbitrary"),
        name="gdn",
    )(q, k, v, z, bgc, grow, s0, gnw)


def _outproj_body(yc_ref, yg_ref, x_ref, wo_ref, g_ref, b_ref, h1_ref, h1p_ref):
    mix = (jnp.dot(yc_ref[...], wo_ref[0:CONV_WIDTH, :], preferred_element_type=_F32)
           + jnp.dot(yg_ref[...], wo_ref[CONV_WIDTH:, :], preferred_element_type=_F32))
    h1 = _layer_norm(DN_ALPHA * x_ref[...] + mix, g_ref[...], b_ref[...])
    h1_ref[...] = h1
    h1p_ref[...] = _pack_halves(h1)


def _outproj(yc, yg, x2d, wo, g, b, *, tm):
    t = x2d.shape[0]
    assert t % tm == 0
    row = lambda w: pl.BlockSpec((tm, w), lambda i: (i, 0))
    full = lambda a: pl.BlockSpec(a.shape, lambda i: (0,) * a.ndim)
    return pl.pallas_call(
        _outproj_body,
        grid=(t // tm,),
        in_specs=[row(CONV_WIDTH), row(GDN_WIDTH), row(D_MODEL), full(wo), full(g), full(b)],
        out_specs=(row(D_MODEL), row(HALF)),
        out_shape=(jax.ShapeDtypeStruct((t, D_MODEL), _F32), jax.ShapeDtypeStruct((t, HALF), jnp.uint32)),
        compiler_params=_cparams("arbitrary"),
        name="outproj",
    )(yc, yg, x2d, wo, g, b)


def _router_body(h1_ref, wh_ref, wl_ref, br_ref, idx_ref, gate_ref, rank_ref, cnt_ref, carry_ref, *, tt):
    @pl.when(pl.program_id(0) == 0)
    def _():
        carry_ref[...] = jnp.zeros_like(carry_ref)

    x = h1_ref[...]
    xh = x.astype(_BF16)
    xl = (x - xh.astype(_F32)).astype(_BF16)
    wh = wh_ref[...]
    logits = _mm_nt(wh, xh) + _mm_nt(wh, xl) + _mm_nt(wl_ref[...], xh)
    scores = _sigmoid(logits)
    sel = scores + br_ref[...]
    ninf = -jnp.inf

    r32 = lax.broadcasted_iota(_I32, (E_PER_GROUP, tt), 0)
    gsc = []
    for g in range(N_GROUPS):
        xg = sel[g * E_PER_GROUP:(g + 1) * E_PER_GROUP]
        m1 = jnp.max(xg, axis=0, keepdims=True)
        i1 = jnp.min(jnp.where(xg == m1, r32, E_PER_GROUP), axis=0, keepdims=True)
        m2 = jnp.max(jnp.where(r32 == i1, ninf, xg), axis=0, keepdims=True)
        gsc.append(m1 + m2)
    work = jnp.concatenate(gsc, axis=0)
    r8 = lax.broadcasted_iota(_I32, (N_GROUPS, tt), 0)
    gkeep = jnp.zeros((N_GROUPS, tt), _F32)
    for _ in range(TOPK_GROUPS):
        m = jnp.max(work, axis=0, keepdims=True)
        gi = jnp.min(jnp.where(work == m, r8, N_GROUPS), axis=0, keepdims=True)
        pick = r8 == gi
        gkeep = jnp.where(pick, 1.0, gkeep)
        work = jnp.where(pick, ninf, work)
    selm = jnp.concatenate(
        [jnp.where(gkeep[g:g + 1] > 0.5, sel[g * E_PER_GROUP:(g + 1) * E_PER_GROUP], ninf)
         for g in range(N_GROUPS)], axis=0)

    re = lax.broadcasted_iota(_I32, (N_EXPERTS, tt), 0)
    msel = jnp.zeros((N_EXPERTS, tt), _F32)
    idxs, gates = [], []
    for _ in range(TOP_K):
        m = jnp.max(selm, axis=0, keepdims=True)
        ii = jnp.min(jnp.where(selm == m, re, N_EXPERTS), axis=0, keepdims=True)
        hit = re == ii
        idxs.append(ii)
        gates.append(jnp.sum(jnp.where(hit, scores, 0.0), axis=0, keepdims=True))
        selm = jnp.where(hit, ninf, selm)
        msel = jnp.where(hit, 1.0, msel)
    gate = jnp.concatenate(gates, axis=0)
    gate_ref[...] = gate / jnp.sum(gate, axis=0, keepdims=True) * ROUTED_SCALE
    idx_ref[...] = jnp.concatenate(idxs, axis=0)

    ta = lax.broadcasted_iota(_I32, (tt, tt), 0)
    tb = lax.broadcasted_iota(_I32, (tt, tt), 1)
    earlier = jnp.where(ta < tb, 1.0, 0.0)
    carry = carry_ref[...]
    rank_all = _mm(msel, earlier) + carry[:, 0:1]
    rank_ref[...] = jnp.concatenate(
        [jnp.sum(jnp.where(re == ii, rank_all, 0.0), axis=0, keepdims=True) for ii in idxs],
        axis=0).astype(_I32)
    carry = carry + jnp.sum(msel, axis=1, keepdims=True)
    carry_ref[...] = carry
    cnt_ref[...] = carry


def _router(h1, wh, wl, br, *, tt):
    t = h1.shape[0]
    assert t % tt == 0
    full = lambda a: pl.BlockSpec(a.shape, lambda i: (0,) * a.ndim)
    kt = pl.BlockSpec((TOP_K, tt), lambda i: (0, i))
    return pl.pallas_call(
        functools.partial(_router_body, tt=tt),
        grid=(t // tt,),
        in_specs=[pl.BlockSpec((tt, D_MODEL), lambda i: (i, 0)), full(wh), full(wl), full(br)],
        out_specs=(kt, kt, kt, pl.BlockSpec((N_EXPERTS, 128), lambda i: (0, 0))),
        out_shape=(jax.ShapeDtypeStruct((TOP_K, t), _I32), jax.ShapeDtypeStruct((TOP_K, t), _F32),
                   jax.ShapeDtypeStruct((TOP_K, t), _I32), jax.ShapeDtypeStruct((N_EXPERTS, 128), _F32)),
        scratch_shapes=[pltpu.VMEM((N_EXPERTS, 128), _F32)],
        compiler_params=_cparams("arbitrary"),
        name="router",
    )(h1, wh, wl, br)


def _position_body(idx_ref, rank_ref, pstart_ref, pos_ref, *, tt):
    re = lax.broadcasted_iota(_I32, (N_EXPERTS, tt), 0)
    ps = pstart_ref[...]
    idx = idx_ref[...]
    rows = [jnp.sum(jnp.where(re == idx[k:k + 1], ps, 0), axis=0, keepdims=True) for k in range(TOP_K)]
    pos_ref[0] = jnp.concatenate(rows, axis=0) + rank_ref[...]


def _position(idx, rank, pstart, *, tt):
    t = idx.shape[1]
    kt = pl.BlockSpec((TOP_K, tt), lambda i: (0, i))
    return pl.pallas_call(
        functools.partial(_position_body, tt=tt),
        grid=(t // tt,),
        in_specs=[kt, kt, pl.BlockSpec(pstart.shape, lambda i: (0, 0))],
        out_specs=pl.BlockSpec((1, TOP_K, tt), lambda i: (i, 0, 0)),
        out_shape=jax.ShapeDtypeStruct((t // tt, TOP_K, tt), _I32),
        compiler_params=_cparams("arbitrary"),
        name="position",
    )(idx, rank, pstart)


def _dispatch_body(pos_hbm, h1p_ref, xs_in, xs_out, pos_smem, psem, sem, *, tt):
    del xs_in
    i = pl.program_id(0)
    cp = pltpu.make_async_copy(pos_hbm.at[i], pos_smem, psem)
    cp.start()
    cp.wait()

    def row_copy(t, k):
        return pltpu.make_async_copy(h1p_ref.at[pl.ds(t, 1)], xs_out.at[pl.ds(pos_smem[k * tt + t], 1)], sem)

    def issue(t, c):
        for k in range(TOP_K):
            row_copy(t, k).start(priority=k % 2)
        return c

    lax.fori_loop(0, tt, issue, 0, unroll=ISSUE_UNROLL)

    for k in range(TOP_K):
        pltpu.make_async_copy(h1p_ref, xs_out.at[pl.ds(0, tt)], sem).wait()


def _dispatch(pos_tiles, h1p, xs_zero, *, tt):
    t = h1p.shape[0]
    return pl.pallas_call(
        functools.partial(_dispatch_body, tt=tt),
        grid=(t // tt,),
        in_specs=[pl.BlockSpec(memory_space=pl.ANY), pl.BlockSpec((tt, HALF), lambda i: (i, 0)),
                  pl.BlockSpec(memory_space=pl.ANY)],
        out_specs=pl.BlockSpec(memory_space=pl.ANY),
        out_shape=jax.ShapeDtypeStruct(xs_zero.shape, xs_zero.dtype),
        scratch_shapes=[pltpu.SMEM((TOP_K * tt,), _I32), pltpu.SemaphoreType.DMA, pltpu.SemaphoreType.DMA],
        input_output_aliases={2: 0},
        compiler_params=_cparams("arbitrary"),
        name="dispatch",
    )(pos_tiles, h1p, xs_zero)


def _ffn_body(row0_ref, nblk_ref, xs_hbm, wg_ref, wu_ref, wd_ref, ys_hbm,
              xbuf, ybuf, sem_in, sem_out, wgu_bf, wd_bf):
    e = pl.program_id(0)
    nblk = nblk_ref[e]
    row0 = row0_ref[e]

    strip = ROW_BLOCK // DMA_SPLIT

    def rows(j, p):
        return pl.ds(pl.multiple_of(row0 + j * ROW_BLOCK + p * strip, strip), strip)

    def in_start(j, slot):
        for p in range(DMA_SPLIT):
            pltpu.make_async_copy(xs_hbm.at[rows(j, p)], xbuf.at[slot, pl.ds(p * strip, strip)],
                                  sem_in.at[slot]).start()

    def in_wait(slot):
        pltpu.make_async_copy(xs_hbm.at[pl.ds(0, ROW_BLOCK)], xbuf.at[slot], sem_in.at[slot]).wait()

    def out_start(j, slot):
        for p in range(DMA_SPLIT):
            pltpu.make_async_copy(ybuf.at[slot, pl.ds(p * strip, strip)], ys_hbm.at[rows(j, p)],
                                  sem_out.at[slot]).start()

    def out_wait(slot):
        pltpu.make_async_copy(ybuf.at[slot], ys_hbm.at[pl.ds(0, ROW_BLOCK)], sem_out.at[slot]).wait()

    @pl.when(nblk > 0)
    def _():
        in_start(0, 0)
        wgu_bf[:, 0:EXPERT_FF] = wg_ref[0].astype(_BF16)
        wgu_bf[:, EXPERT_FF:] = wu_ref[0].astype(_BF16)
        wd_bf[...] = wd_ref[0].astype(_BF16)

        def block(j, carry):
            slot = j & 1
            in_wait(slot)

            @pl.when(j + 1 < nblk)
            def _():
                in_start(j + 1, 1 - slot)

            @pl.when(j >= 2)
            def _():
                out_wait(slot)

            lo, hi = _unpack_halves(xbuf[slot])
            gu = (jnp.dot(lo.astype(_BF16), wgu_bf[0:HALF, :], preferred_element_type=_F32)
                  + jnp.dot(hi.astype(_BF16), wgu_bf[HALF:, :], preferred_element_type=_F32))
            h = (_silu(gu[:, :EXPERT_FF]) * gu[:, EXPERT_FF:]).astype(_BF16)
            ybuf[slot] = _pack_halves(jnp.dot(h, wd_bf[...], preferred_element_type=_F32))
            out_start(j, slot)
            return carry

        lax.fori_loop(0, nblk, block, 0)

        @pl.when(nblk >= 2)
        def _():
            out_wait(nblk & 1)

        out_wait((nblk - 1) & 1)


def _ffn(row0, nblk, xs, wg, wu, wd):
    grid_spec = pltpu.PrefetchScalarGridSpec(
        num_scalar_prefetch=2,
        grid=(N_EXPERTS,),
        in_specs=[pl.BlockSpec(memory_space=pl.ANY),
                  pl.BlockSpec((1, D_MODEL, EXPERT_FF), lambda e, r0, nb: (e, 0, 0)),
                  pl.BlockSpec((1, D_MODEL, EXPERT_FF), lambda e, r0, nb: (e, 0, 0)),
                  pl.BlockSpec((1, EXPERT_FF, D_MODEL), lambda e, r0, nb: (e, 0, 0))],
        out_specs=pl.BlockSpec(memory_space=pl.ANY),
        scratch_shapes=[pltpu.VMEM((2, ROW_BLOCK, HALF), jnp.uint32), pltpu.VMEM((2, ROW_BLOCK, HALF), jnp.uint32),
                        pltpu.SemaphoreType.DMA((2,)), pltpu.SemaphoreType.DMA((2,)),
                        pltpu.VMEM((D_MODEL, 2 * EXPERT_FF), _BF16), pltpu.VMEM((EXPERT_FF, D_MODEL), _BF16)],
    )
    return pl.pallas_call(
        _ffn_body,
        grid_spec=grid_spec,
        out_shape=jax.ShapeDtypeStruct(xs.shape, jnp.uint32),
        compiler_params=_cparams("arbitrary"),
        name="ffn",
    )(row0, nblk, xs, wg, wu, wd)


def _combine_body(pos_hbm, gate_ref, h1_ref, ys_hbm, wsg_ref, wsu_ref, wsd_ref, g_ref, b_ref,
                  out_ref, pos_smem, psem, ybuf, sem, *, tt):
    i = pl.program_id(0)
    cp = pltpu.make_async_copy(pos_hbm.at[i], pos_smem, psem)
    cp.start()
    cp.wait()

    def row_copy(t, k):
        return pltpu.make_async_copy(ys_hbm.at[pl.ds(pos_smem[k * tt + t], 1)], ybuf.at[k, pl.ds(t, 1)], sem)

    def issue(t, c):
        for k in range(TOP_K):
            row_copy(t, k).start(priority=k % 2)
        return c

    lax.fori_loop(0, tt, issue, 0, unroll=ISSUE_UNROLL)

    x = h1_ref[...]
    xb = x.astype(_BF16)
    shared = _mm(_silu(_mm(xb, wsg_ref[...])) * _mm(xb, wsu_ref[...]), wsd_ref[...])

    for k in range(TOP_K):
        pltpu.make_async_copy(ys_hbm.at[pl.ds(0, tt)], ybuf.at[k], sem).wait()

    gcol = gate_ref[...].T
    acc_lo = jnp.zeros((tt, HALF), _F32)
    acc_hi = jnp.zeros((tt, HALF), _F32)
    for k in range(TOP_K):
        lo, hi = _unpack_halves(ybuf[k])
        acc_lo = acc_lo + gcol[:, k:k + 1] * lo
        acc_hi = acc_hi + gcol[:, k:k + 1] * hi
    routed = jnp.concatenate([acc_lo, acc_hi], axis=1)
    out_ref[...] = _layer_norm(DN_ALPHA * x + (routed + shared), g_ref[...], b_ref[...])


def _combine(pos_tiles, gate, h1, ys, wsg, wsu, wsd, g, b, *, tt):
    t = h1.shape[0]
    full = lambda a: pl.BlockSpec(a.shape, lambda i: (0,) * a.ndim)
    return pl.pallas_call(
        functools.partial(_combine_body, tt=tt),
        grid=(t // tt,),
        in_specs=[pl.BlockSpec(memory_space=pl.ANY), pl.BlockSpec((TOP_K, tt), lambda i: (0, i)),
                  pl.BlockSpec((tt, D_MODEL), lambda i: (i, 0)), pl.BlockSpec(memory_space=pl.ANY),
                  full(wsg), full(wsu), full(wsd), full(g), full(b)],
        out_specs=pl.BlockSpec((tt, D_MODEL), lambda i: (i, 0)),
        out_shape=jax.ShapeDtypeStruct((t, D_MODEL), _F32),
        scratch_shapes=[pltpu.SMEM((TOP_K * tt,), _I32), pltpu.SemaphoreType.DMA,
                        pltpu.VMEM((TOP_K, tt, HALF), jnp.uint32), pltpu.SemaphoreType.DMA],
        compiler_params=_cparams("arbitrary"),
        name="combine",
    )(pos_tiles, gate, h1, ys, wsg, wsu, wsd, g, b)


def _pick(n, pref):
    t = min(n, pref)
    while n % t:
        t -= CHUNK
    return t


def _mixer(x, tails, s0, wts, gnw, *, lt, lg, nbb):
    yc, q, k, v, z, bgc, bgr, tails_out = _premix(x, tails, wts, lt=lt)
    bsz, seq, _ = x.shape
    nch = seq // CHUNK
    grow = bgr[:, GDN_HEADS:2 * GDN_HEADS, :].reshape(bsz, GDN_HEADS, nch, CHUNK)
    grow = grow.transpose(0, 2, 1, 3).reshape(bsz, nch, 1, STACK)
    yg, s_out = _gdn(q, k, v, z, bgc, grow, s0, gnw, lg=lg, nbb=nbb)
    return yc, yg, tails_out, s_out


def kernel(x, meta_tokens, w_in, conv_w, conv_norm_w, gdn_conv_w, a_log, dt_bias, gdn_norm_w, w_out,
           ln1_g, ln1_b, w_router, b_router, w_gate, w_up, w_down, ws_gate, ws_up, ws_down, ln2_g, ln2_b):
    assert w_in.shape[0] == 1, "single-layer stack"
    bsz, seq, d = x.shape
    assert d == D_MODEL and seq % CHUNK == 0
    c, gw = CONV_WIDTH, GDN_WIDTH
    win = w_in[0].astype(_BF16)
    wbd = win[:, 3 * c + 4 * gw:]
    zpad = jnp.zeros((128 - 2 * GDN_HEADS,), _F32)
    zpad4 = jnp.zeros((GDN_HEADS,), _F32)
    prow = jnp.zeros((8, 128), _F32)
    prow = prow.at[0].set(jnp.concatenate([zpad4, a_log[0], zpad]))
    prow = prow.at[1].set(jnp.concatenate([zpad4, dt_bias[0], zpad]))
    wts = (win[:, :3 * c], win[:, 3 * c:3 * c + 3 * gw], win[:, 3 * c + 3 * gw:3 * c + 4 * gw],
           jnp.pad(wbd, ((0, 0), (0, 128 - 2 * GDN_HEADS))), wbd.T,
           conv_w[0], conv_norm_w, gdn_conv_w[0], prow, prow.T[:8])
    gnw = gdn_norm_w

    meta = jnp.concatenate([jnp.zeros((CHUNK - N_META, d), x.dtype), meta_tokens.astype(x.dtype)])[None]
    tails0 = jnp.zeros((HIST, c + 3 * gw), _F32)
    s00 = jnp.zeros((GDN_HEADS, GDN_HEAD_DIM, GDN_HEAD_DIM), _F32)
    _, _, tails_m, s_m = _mixer(meta, tails0, s00, wts, gnw, lt=CHUNK, lg=CHUNK, nbb=1)

    yc, yg, _, _ = _mixer(x, tails_m[0], s_m[0], wts, gnw, lt=_pick(seq, 512), lg=_pick(seq, 512),
                          nbb=GDN_ROWS if bsz % GDN_ROWS == 0 else 1)

    t = bsz * seq
    tm = _pick(t, 512)
    h1, h1p = _outproj(yc.reshape(t, c), yg.reshape(t, gw), x.reshape(t, d), w_out[0].astype(_BF16),
                       ln1_g, ln1_b, tm=tm)

    tt = _pick(t, 256)
    wr_t = w_router[0].T
    wr_hi = wr_t.astype(_BF16)
    wr_lo = (wr_t - wr_hi.astype(_F32)).astype(_BF16)
    idx, gate, rank, cnt = _router(h1, wr_hi, wr_lo, b_router[0][:, None], tt=tt)

    counts = cnt[:, 0].astype(_I32)
    pcounts = (counts + ROW_BLOCK - 1) // ROW_BLOCK * ROW_BLOCK
    pends = jnp.cumsum(pcounts)
    pstarts = pends - pcounts
    nb = t * TOP_K // ROW_BLOCK + N_EXPERTS

    pos = _position(idx, rank, pstarts[:, None].astype(_I32), tt=tt).reshape(t // tt, TOP_K * tt)
    xs = _dispatch(pos, h1p, jnp.zeros((nb * ROW_BLOCK, HALF), jnp.uint32), tt=tt)
    ys = _ffn(pstarts.astype(_I32), (pcounts // ROW_BLOCK).astype(_I32), xs, w_gate[0], w_up[0], w_down[0])
    out = _combine(pos, gate, h1, ys, ws_gate[0].astype(_BF16), ws_up[0].astype(_BF16),
                   ws_down[0].astype(_BF16), ln2_g, ln2_b, tt=tt)
    return out.reshape(bsz, seq, d)
```

```python
import functools

import jax
import jax.numpy as jnp
from jax import lax
from jax.experimental import pallas as pl
from jax.experimental.pallas import tpu as pltpu

_F32 = jnp.float32
_BF16 = jnp.bfloat16
_I32 = jnp.int32

D_MODEL = 1024
N_META = 16
CONV_WIDTH = 512
CONV_K = 3
GDN_HEADS = 4
GDN_HEAD_DIM = 128
GDN_WIDTH = GDN_HEADS * GDN_HEAD_DIM
GDN_CONV_K = 4
CHUNK = 64
N_EXPERTS = 256
TOP_K = 8
N_GROUPS = 8
TOPK_GROUPS = 4
E_PER_GROUP = N_EXPERTS // N_GROUPS
EXPERT_FF = 256
ROUTED_SCALE = 2.5
ROW_BLOCK = 256
DN_ALPHA = 2.0 ** 0.25
NORM_EPS = 1e-5
HALF = D_MODEL // 2
STACK = GDN_HEADS * CHUNK
HIST = 8
GDN_ROWS = 4
DMA_SPLIT = 8
ISSUE_UNROLL = 8
RING = 4
IN_AHEAD = RING - 1

V7X_VMEM_BYTES = 64 * 1024 * 1024
VMEM_LIMIT = V7X_VMEM_BYTES - 8 * 1024 * 1024


def _cparams(*sem):
    return pltpu.CompilerParams(dimension_semantics=sem, vmem_limit_bytes=VMEM_LIMIT)


def _mm(a, b):
    return jnp.dot(a.astype(_BF16), b.astype(_BF16), preferred_element_type=_F32)


def _mm_nt(a, b):
    return lax.dot_general(a.astype(_BF16), b.astype(_BF16), (((1,), (1,)), ((), ())),
                           preferred_element_type=_F32)


def _mm_tn(a, b):
    return lax.dot_general(a.astype(_BF16), b.astype(_BF16), (((0,), (0,)), ((), ())),
                           preferred_element_type=_F32)


def _sigmoid(x):
    return 1.0 / (1.0 + jnp.exp(-x))


def _silu(x):
    return x * _sigmoid(x)


def _softplus(x):
    return jnp.maximum(x, 0.0) + jnp.log1p(jnp.exp(-jnp.abs(x)))


def _pack_halves(y):
    return pltpu.pack_elementwise([y[:, :HALF], y[:, HALF:]], packed_dtype=_BF16)


def _unpack_halves(p):
    lo = pltpu.unpack_elementwise(p, index=0, packed_dtype=_BF16, unpacked_dtype=_F32)
    hi = pltpu.unpack_elementwise(p, index=1, packed_dtype=_BF16, unpacked_dtype=_F32)
    return lo, hi


def _layer_norm(h, g, b):
    mu = jnp.mean(h, axis=-1, keepdims=True)
    d = h - mu
    var = jnp.mean(d * d, axis=-1, keepdims=True)
    return d * lax.rsqrt(var + NORM_EPS) * g + b


def _premix_body(x_ref, tails_ref, wa_ref, wq_ref, wz_ref, wbd_ref, wbdt_ref, cw_ref, cnw_ref,
                 gcw_ref, prow_ref, pcol_ref,
                 yc_ref, q_ref, k_ref, v_ref, z_ref, bgc_ref, bgr_ref, tout_ref, ext_ref, *, lt):
    cw_ = CONV_WIDTH

    @pl.when(pl.program_id(1) == 0)
    def _():
        ext_ref[0:HIST, :] = tails_ref[...]

    xb = x_ref[0].astype(_BF16)
    pa = jnp.dot(xb, wa_ref[...], preferred_element_type=_F32)
    gate_b = pa[:, 0:cw_]
    u = pa[:, cw_:2 * cw_] * pa[:, 2 * cw_:3 * cw_]
    ext_ref[HIST:HIST + lt, 0:cw_] = u
    pq = jnp.dot(xb, wq_ref[...], preferred_element_type=_F32)
    ext_ref[HIST:HIST + lt, cw_:] = pq

    cw = cw_ref[...]
    ca = u * cw[CONV_K - 1:CONV_K, :]
    for j in range(CONV_K - 1):
        ca = ca + ext_ref[pl.ds(HIST - (CONV_K - 1) + j, lt), 0:cw_] * cw[j:j + 1, :]
    yc = gate_b * ca
    ms = jnp.mean(yc * yc, axis=-1, keepdims=True)
    yc_ref[0] = (yc * lax.rsqrt(ms + NORM_EPS) * cnw_ref[...]).astype(_BF16)

    gcw = gcw_ref[...]
    cq = pq * gcw[GDN_CONV_K - 1:GDN_CONV_K, :]
    for j in range(GDN_CONV_K - 1):
        cq = cq + ext_ref[pl.ds(HIST - (GDN_CONV_K - 1) + j, lt), cw_:] * gcw[j:j + 1, :]
    s = _silu(cq)
    for h in range(GDN_HEADS):
        lo, hi = h * GDN_HEAD_DIM, (h + 1) * GDN_HEAD_DIM
        qh = s[:, lo:hi]
        kh = s[:, GDN_WIDTH + lo:GDN_WIDTH + hi]
        qn = qh * lax.rsqrt(jnp.sum(qh * qh, axis=-1, keepdims=True) + 1e-6)
        kn = kh * lax.rsqrt(jnp.sum(kh * kh, axis=-1, keepdims=True) + 1e-6)
        q_ref[0, :, lo:hi] = (qn * (GDN_HEAD_DIM ** -0.5)).astype(_BF16)
        k_ref[0, :, lo:hi] = kn.astype(_BF16)
    v_ref[0] = s[:, 2 * GDN_WIDTH:].astype(_BF16)
    z_ref[0] = jnp.dot(xb, wz_ref[...], preferred_element_type=_F32).astype(_BF16)

    bdc = jnp.dot(xb, wbd_ref[...], preferred_element_type=_F32)
    prow = prow_ref[...]
    g_c = -jnp.exp(prow[0:1, :]) * _softplus(bdc + prow[1:2, :])
    lane = lax.broadcasted_iota(_I32, bdc.shape, 1)
    bgc_ref[0] = jnp.where(lane < GDN_HEADS, _sigmoid(bdc), g_c)
    bdr = _mm_nt(wbdt_ref[...], xb)
    pcol = pcol_ref[...]
    g_r = -jnp.exp(pcol[:, 0:1]) * _softplus(bdr + pcol[:, 1:2])
    row = lax.broadcasted_iota(_I32, bdr.shape, 0)
    bgr_ref[0] = jnp.where(row < GDN_HEADS, _sigmoid(bdr), g_r)

    tail = ext_ref[lt:lt + HIST, :]
    ext_ref[0:HIST, :] = tail
    tout_ref[0] = tail


def _premix(x, tails, wts, *, lt):
    bsz, seq, d = x.shape
    assert seq % lt == 0
    grid = (bsz, seq // lt)
    full = lambda a: pl.BlockSpec(a.shape, lambda b, j: (0,) * a.ndim)
    tok = lambda w: pl.BlockSpec((1, lt, w), lambda b, j: (b, j, 0))
    (wa, wq, wz, wbd, wbdt, cw, cnw, gcw, prow, pcol) = wts
    ext_w = CONV_WIDTH + 3 * GDN_WIDTH
    out_shape = (
        jax.ShapeDtypeStruct((bsz, seq, CONV_WIDTH), _BF16),
        jax.ShapeDtypeStruct((bsz, seq, GDN_WIDTH), _BF16),
        jax.ShapeDtypeStruct((bsz, seq, GDN_WIDTH), _BF16),
        jax.ShapeDtypeStruct((bsz, seq, GDN_WIDTH), _BF16),
        jax.ShapeDtypeStruct((bsz, seq, GDN_WIDTH), _BF16),
        jax.ShapeDtypeStruct((bsz, seq, 128), _F32),
        jax.ShapeDtypeStruct((bsz, 8, seq), _F32),
        jax.ShapeDtypeStruct((bsz, HIST, ext_w), _F32),
    )
    out_specs = (tok(CONV_WIDTH), tok(GDN_WIDTH), tok(GDN_WIDTH), tok(GDN_WIDTH), tok(GDN_WIDTH),
                 tok(128), pl.BlockSpec((1, 8, lt), lambda b, j: (b, 0, j)),
                 pl.BlockSpec((1, HIST, ext_w), lambda b, j: (b, 0, 0)))
    return pl.pallas_call(
        functools.partial(_premix_body, lt=lt),
        grid=grid,
        in_specs=[tok(d), full(tails)] + [full(w) for w in wts],
        out_specs=out_specs,
        out_shape=out_shape,
        scratch_shapes=[pltpu.VMEM((HIST + lt, ext_w), _F32)],
        compiler_params=_cparams("arbitrary", "arbitrary"),
        name="premix",
    )(x, tails, *wts)


def _cumsum_rows(x):
    row = lax.broadcasted_iota(_I32, x.shape, 0)
    s = 1
    while s < x.shape[0]:
        x = x + jnp.where(row >= s, pltpu.roll(x, s, 0), 0.0)
        s *= 2
    return x


def _cumsum_lanes_seg(x):
    lane = lax.broadcasted_iota(_I32, x.shape, 1) & (CHUNK - 1)
    s = 1
    while s < CHUNK:
        x = x + jnp.where(lane >= s, pltpu.roll(x, s, 1), 0.0)
        s *= 2
    return x


def _stack_heads(a):
    return jnp.concatenate([a[:, h * GDN_HEAD_DIM:(h + 1) * GDN_HEAD_DIM] for h in range(GDN_HEADS)], axis=0)


def _gdn_body(q_ref, k_ref, v_ref, z_ref, bgc_ref, grow_ref, s0_ref, gnw_ref,
              y_ref, sout_ref, s_ref, *, nc, nbb):
    @pl.when(pl.program_id(1) == 0)
    def _():
        for r in range(nbb):
            s_ref[r] = s0_ref[...]

    ri = lax.broadcasted_iota(_I32, (STACK, STACK), 0)
    ci = lax.broadcasted_iota(_I32, (STACK, STACK), 1)
    same64 = (ri >> 6) == (ci >> 6)
    same32 = (ri >> 5) == (ci >> 5)
    same16 = (ri >> 4) == (ci >> 4)
    low_incl = same64 & (ri >= ci)
    low_strict = same64 & (ri > ci)
    gnw = gnw_ref[...]

    def chunk_row(r, c):
        off = pl.multiple_of(c * CHUNK, CHUNK)
        q_all = _stack_heads(q_ref[r, pl.ds(off, CHUNK), :].astype(_F32))
        k_all = _stack_heads(k_ref[r, pl.ds(off, CHUNK), :].astype(_F32))
        v_all = _stack_heads(v_ref[r, pl.ds(off, CHUNK), :].astype(_F32))
        bgc = bgc_ref[r, pl.ds(off, CHUNK), :]
        gcs = _cumsum_rows(bgc)
        hd = (CHUNK, GDN_HEAD_DIM)
        beta_b = jnp.concatenate(
            [jnp.broadcast_to(bgc[:, h:h + 1], hd) for h in range(GDN_HEADS)], axis=0)
        gc_b = jnp.concatenate(
            [jnp.broadcast_to(gcs[:, GDN_HEADS + h:GDN_HEADS + h + 1], hd) for h in range(GDN_HEADS)], axis=0)
        gl = [gcs[CHUNK - 1:CHUNK, GDN_HEADS + h:GDN_HEADS + h + 1] for h in range(GDN_HEADS)]
        gl_b = jnp.concatenate([jnp.broadcast_to(g1, hd) for g1 in gl], axis=0)
        gcr = _cumsum_lanes_seg(jnp.broadcast_to(grow_ref[r, c], (8, STACK)))[0:1, :]

        diff = jnp.concatenate([gc_b, gc_b], axis=1) - gcr
        decay = jnp.exp(jnp.where(low_incl, diff, -1e30))
        kb = k_all * beta_b
        a1 = _mm_nt(jnp.concatenate([kb, q_all], axis=0), k_all)
        yield
        m = jnp.where(low_strict, a1[:STACK] * decay, 0.0)
        attn = a1[STACK:] * decay

        l16 = jnp.where(same16, m, 0.0)
        c1 = jnp.where(same32 & jnp.logical_not(same16), m, 0.0)
        c2 = jnp.where(same32, 0.0, m)
        p2 = _mm(l16, l16)
        yield
        p4 = _mm(p2, p2)
        t = _mm(l16, p2)
        yield
        na = p2 - l16 - t
        p8 = _mm(p4, p4)
        t = _mm(na, p4)
        yield
        nb = na + p4 + t
        t = _mm(nb, p8)
        yield
        ncm = nb + p8 + t
        t = _mm(c1, ncm)
        yield
        y1 = c1 + t
        t = _mm(ncm, y1)
        yield
        n1 = ncm - y1 - t
        t = _mm(c2, n1)
        yield
        y2 = c2 + t
        t = _mm(n1, y2)
        yield
        nt = n1 - y2 - t

        egc = jnp.exp(gc_b)
        rhs = jnp.concatenate([v_all * beta_b, kb * egc], axis=1)
        t = _mm(nt, rhs)
        yield
        uw = rhs + t
        u_all = uw[:, :GDN_HEAD_DIM]
        w_all = uw[:, GDN_HEAD_DIM:]
        qd = q_all * egc
        kd = k_all * jnp.exp(gl_b - gc_b)

        bs = []
        for h in range(GDN_HEADS):
            r0, r1 = h * CHUNK, (h + 1) * CHUNK
            bs.append(_mm(jnp.concatenate([w_all[r0:r1], qd[r0:r1]], axis=0), s_ref[r, h]))
        yield
        vn = [u_all[h * CHUNK:(h + 1) * CHUNK] - bs[h][:CHUNK] for h in range(GDN_HEADS)]
        vn_all = jnp.concatenate(vn, axis=0)
        t = _mm(attn, vn_all)
        ds = [_mm_tn(kd[h * CHUNK:(h + 1) * CHUNK], vn[h]) for h in range(GDN_HEADS)]
        yield
        o_all = jnp.concatenate([b[CHUNK:] for b in bs], axis=0) + t
        for h in range(GDN_HEADS):
            r0, r1 = h * CHUNK, (h + 1) * CHUNK
            s_ref[r, h] = s_ref[r, h] * jnp.exp(gl[h]) + ds[h]
            o = o_all[r0:r1]
            zz = z_ref[r, pl.ds(off, CHUNK), h * GDN_HEAD_DIM:(h + 1) * GDN_HEAD_DIM].astype(_F32)
            on = o * lax.rsqrt(jnp.mean(o * o, axis=-1, keepdims=True) + NORM_EPS) * gnw
            y_ref[r, pl.ds(off, CHUNK), h * GDN_HEAD_DIM:(h + 1) * GDN_HEAD_DIM] = (on * _silu(zz)).astype(_BF16)

    def chunk(c, carry):
        live = [chunk_row(r, c) for r in range(nbb)]
        while live:
            live = [g for g in live if next(g, live) is not live]
        return carry

    lax.fori_loop(0, nc, chunk, 0)
    sout_ref[...] = s_ref[...]


def _gdn(q, k, v, z, bgc, grow, s0, gnw, *, lg, nbb):
    bsz, seq, _ = q.shape
    assert seq % lg == 0 and lg % CHUNK == 0 and bsz % nbb == 0
    nc = lg // CHUNK
    tok = lambda w: pl.BlockSpec((nbb, lg, w), lambda b, j: (b, j, 0))
    full = lambda a: pl.BlockSpec(a.shape, lambda b, j: (0,) * a.ndim)
    st = (nbb, GDN_HEADS, GDN_HEAD_DIM, GDN_HEAD_DIM)
    return pl.pallas_call(
        functools.partial(_gdn_body, nc=nc, nbb=nbb),
        grid=(bsz // nbb, seq // lg),
        in_specs=[tok(GDN_WIDTH)] * 4 + [tok(128), pl.BlockSpec((nbb, nc, 1, STACK), lambda b, j: (b, j, 0, 0)),
                                           full(s0), full(gnw)],
        out_specs=(tok(GDN_WIDTH), pl.BlockSpec(st, lambda b, j: (b, 0, 0, 0))),
        out_shape=(jax.ShapeDtypeStruct((bsz, seq, GDN_WIDTH), _BF16),
                   jax.ShapeDtypeStruct((bsz, GDN_HEADS, GDN_HEAD_DIM, GDN_HEAD_DIM), _F32)),
        scratch_shapes=[pltpu.VMEM(st, _F32)],
        compiler_params=_cparams("arbitrary", "arbitrary"),
        name="gdn",
    )(q, k, v, z, bgc, grow, s0, gnw)


def _outproj_body(yc_ref, yg_ref, x_ref, wo_ref, g_ref, b_ref, h1_ref, h1p_ref):
    mix = (jnp.dot(yc_ref[...], wo_ref[0:CONV_WIDTH, :], preferred_element_type=_F32)
           + jnp.dot(yg_ref[...], wo_ref[CONV_WIDTH:, :], preferred_element_type=_F32))
    h1 = _layer_norm(DN_ALPHA * x_ref[...] + mix, g_ref[...], b_ref[...])
    h1_ref[...] = h1
    h1p_ref[...] = _pack_halves(h1)


def _outproj(yc, yg, x2d, wo, g, b, *, tm):
    t = x2d.shape[0]
    assert t % tm == 0
    row = lambda w: pl.BlockSpec((tm, w), lambda i: (i, 0))
    full = lambda a: pl.BlockSpec(a.shape, lambda i: (0,) * a.ndim)
    return pl.pallas_call(
        _outproj_body,
        grid=(t // tm,),
        in_specs=[row(CONV_WIDTH), row(GDN_WIDTH), row(D_MODEL), full(wo), full(g), full(b)],
        out_specs=(row(D_MODEL), row(HALF)),
        out_shape=(jax.ShapeDtypeStruct((t, D_MODEL), _F32), jax.ShapeDtypeStruct((t, HALF), jnp.uint32)),
        compiler_params=_cparams("arbitrary"),
        name="outproj",
    )(yc, yg, x2d, wo, g, b)


def _router_body(h1_ref, wh_ref, wl_ref, br_ref, idx_ref, gate_ref, rank_ref, cnt_ref, carry_ref, *, tt):
    @pl.when(pl.program_id(0) == 0)
    def _():
        carry_ref[...] = jnp.zeros_like(carry_ref)

    x = h1_ref[...]
    xh = x.astype(_BF16)
    xl = (x - xh.astype(_F32)).astype(_BF16)
    wh = wh_ref[...]
    logits = _mm_nt(wh, xh) + _mm_nt(wh, xl) + _mm_nt(wl_ref[...], xh)
    scores = _sigmoid(logits)
    sel = scores + br_ref[...]
    ninf = -jnp.inf

    r32 = lax.broadcasted_iota(_I32, (E_PER_GROUP, tt), 0)
    gsc = []
    for g in range(N_GROUPS):
        xg = sel[g * E_PER_GROUP:(g + 1) * E_PER_GROUP]
        m1 = jnp.max(xg, axis=0, keepdims=True)
        i1 = jnp.min(jnp.where(xg == m1, r32, E_PER_GROUP), axis=0, keepdims=True)
        m2 = jnp.max(jnp.where(r32 == i1, ninf, xg), axis=0, keepdims=True)
        gsc.append(m1 + m2)
    work = jnp.concatenate(gsc, axis=0)
    r8 = lax.broadcasted_iota(_I32, (N_GROUPS, tt), 0)
    gkeep = jnp.zeros((N_GROUPS, tt), _F32)
    for _ in range(TOPK_GROUPS):
        m = jnp.max(work, axis=0, keepdims=True)
        gi = jnp.min(jnp.where(work == m, r8, N_GROUPS), axis=0, keepdims=True)
        pick = r8 == gi
        gkeep = jnp.where(pick, 1.0, gkeep)
        work = jnp.where(pick, ninf, work)
    selm = jnp.concatenate(
        [jnp.where(gkeep[g:g + 1] > 0.5, sel[g * E_PER_GROUP:(g + 1) * E_PER_GROUP], ninf)
         for g in range(N_GROUPS)], axis=0)

    re = lax.broadcasted_iota(_I32, (N_EXPERTS, tt), 0)
    msel = jnp.zeros((N_EXPERTS, tt), _F32)
    idxs, gates = [], []
    for _ in range(TOP_K):
        m = jnp.max(selm, axis=0, keepdims=True)
        ii = jnp.min(jnp.where(selm == m, re, N_EXPERTS), axis=0, keepdims=True)
        hit = re == ii
        idxs.append(ii)
        gates.append(jnp.sum(jnp.where(hit, scores, 0.0), axis=0, keepdims=True))
        selm = jnp.where(hit, ninf, selm)
        msel = jnp.where(hit, 1.0, msel)
    gate = jnp.concatenate(gates, axis=0)
    gate_ref[...] = gate / jnp.sum(gate, axis=0, keepdims=True) * ROUTED_SCALE
    idx_ref[...] = jnp.concatenate(idxs, axis=0)

    ta = lax.broadcasted_iota(_I32, (tt, tt), 0)
    tb = lax.broadcasted_iota(_I32, (tt, tt), 1)
    earlier = jnp.where(ta < tb, 1.0, 0.0)
    carry = carry_ref[...]
    rank_all = _mm(msel, earlier) + carry[:, 0:1]
    rank_ref[...] = jnp.concatenate(
        [jnp.sum(jnp.where(re == ii, rank_all, 0.0), axis=0, keepdims=True) for ii in idxs],
        axis=0).astype(_I32)
    carry = carry + jnp.sum(msel, axis=1, keepdims=True)
    carry_ref[...] = carry
    cnt_ref[...] = carry


def _router(h1, wh, wl, br, *, tt):
    t = h1.shape[0]
    assert t % tt == 0
    full = lambda a: pl.BlockSpec(a.shape, lambda i: (0,) * a.ndim)
    kt = pl.BlockSpec((TOP_K, tt), lambda i: (0, i))
    return pl.pallas_call(
        functools.partial(_router_body, tt=tt),
        grid=(t // tt,),
        in_specs=[pl.BlockSpec((tt, D_MODEL), lambda i: (i, 0)), full(wh), full(wl), full(br)],
        out_specs=(kt, kt, kt, pl.BlockSpec((N_EXPERTS, 128), lambda i: (0, 0))),
        out_shape=(jax.ShapeDtypeStruct((TOP_K, t), _I32), jax.ShapeDtypeStruct((TOP_K, t), _F32),
                   jax.ShapeDtypeStruct((TOP_K, t), _I32), jax.ShapeDtypeStruct((N_EXPERTS, 128), _F32)),
        scratch_shapes=[pltpu.VMEM((N_EXPERTS, 128), _F32)],
        compiler_params=_cparams("arbitrary"),
        name="router",
    )(h1, wh, wl, br)


def _position_body(idx_ref, rank_ref, pstart_ref, pos_ref, *, tt):
    re = lax.broadcasted_iota(_I32, (N_EXPERTS, tt), 0)
    ps = pstart_ref[...]
    idx = idx_ref[...]
    rows = [jnp.sum(jnp.where(re == idx[k:k + 1], ps, 0), axis=0, keepdims=True) for k in range(TOP_K)]
    pos_ref[0] = jnp.concatenate(rows, axis=0) + rank_ref[...]


def _position(idx, rank, pstart, *, tt):
    t = idx.shape[1]
    kt = pl.BlockSpec((TOP_K, tt), lambda i: (0, i))
    return pl.pallas_call(
        functools.partial(_position_body, tt=tt),
        grid=(t // tt,),
        in_specs=[kt, kt, pl.BlockSpec(pstart.shape, lambda i: (0, 0))],
        out_specs=pl.BlockSpec((1, TOP_K, tt), lambda i: (i, 0, 0)),
        out_shape=jax.ShapeDtypeStruct((t // tt, TOP_K, tt), _I32),
        compiler_params=_cparams("arbitrary"),
        name="position",
    )(idx, rank, pstart)


def _dispatch_body(pos_hbm, h1p_ref, xs_in, xs_out, pos_smem, psem, sem, *, tt):
    del xs_in
    i = pl.program_id(0)
    cp = pltpu.make_async_copy(pos_hbm.at[i], pos_smem, psem)
    cp.start()
    cp.wait()

    def row_copy(t, k):
        return pltpu.make_async_copy(h1p_ref.at[pl.ds(t, 1)], xs_out.at[pl.ds(pos_smem[k * tt + t], 1)], sem)

    def issue(t, c):
        for k in range(TOP_K):
            row_copy(t, k).start(priority=k % 2)
        return c

    lax.fori_loop(0, tt, issue, 0, unroll=ISSUE_UNROLL)

    for k in range(TOP_K):
        pltpu.make_async_copy(h1p_ref, xs_out.at[pl.ds(0, tt)], sem).wait()


def _dispatch(pos_tiles, h1p, xs_zero, *, tt):
    t = h1p.shape[0]
    return pl.pallas_call(
        functools.partial(_dispatch_body, tt=tt),
        grid=(t // tt,),
        in_specs=[pl.BlockSpec(memory_space=pl.ANY), pl.BlockSpec((tt, HALF), lambda i: (i, 0)),
                  pl.BlockSpec(memory_space=pl.ANY)],
        out_specs=pl.BlockSpec(memory_space=pl.ANY),
        out_shape=jax.ShapeDtypeStruct(xs_zero.shape, xs_zero.dtype),
        scratch_shapes=[pltpu.SMEM((TOP_K * tt,), _I32), pltpu.SemaphoreType.DMA, pltpu.SemaphoreType.DMA],
        input_output_aliases={2: 0},
        compiler_params=_cparams("arbitrary"),
        name="dispatch",
    )(pos_tiles, h1p, xs_zero)


def _ffn_body(blk0_ref, nblk_ref, ntot_ref, xs_hbm, wg_ref, wu_ref, wd_ref, ys_hbm,
              xbuf, ybuf, sem_in, sem_out, wgu_bf, wd_bf):
    e = pl.program_id(0)
    nblk = nblk_ref[e]
    blk0 = blk0_ref[e]
    ntot = ntot_ref[0]

    strip = ROW_BLOCK // DMA_SPLIT

    def rows(g, p):
        return pl.ds(pl.multiple_of(g * ROW_BLOCK + p * strip, strip), strip)

    def in_start(j, slot):
        for p in range(DMA_SPLIT):
            pltpu.make_async_copy(xs_hbm.at[rows(j, p)], xbuf.at[slot, pl.ds(p * strip, strip)],
                                  sem_in.at[slot]).start()

    def in_wait(slot):
        pltpu.make_async_copy(xs_hbm.at[pl.ds(0, ROW_BLOCK)], xbuf.at[slot], sem_in.at[slot]).wait()

    def out_start(j, slot):
        for p in range(DMA_SPLIT):
            pltpu.make_async_copy(ybuf.at[slot, pl.ds(p * strip, strip)], ys_hbm.at[rows(j, p)],
                                  sem_out.at[slot]).start()

    def out_wait(slot):
        pltpu.make_async_copy(ybuf.at[slot], ys_hbm.at[pl.ds(0, ROW_BLOCK)], sem_out.at[slot]).wait()

    @pl.when(e == 0)
    def _():
        for i in range(IN_AHEAD):
            @pl.when(i < ntot)
            def _():
                in_start(i, i)

    @pl.when(nblk > 0)
    def _():
        wgu_bf[:, 0:EXPERT_FF] = wg_ref[0].astype(_BF16)
        wgu_bf[:, EXPERT_FF:] = wu_ref[0].astype(_BF16)
        wd_bf[...] = wd_ref[0].astype(_BF16)

        def block(j, carry):
            g = blk0 + j
            slot = g & (RING - 1)
            in_wait(slot)

            @pl.when(g + IN_AHEAD < ntot)
            def _():
                in_start(g + IN_AHEAD, (g + IN_AHEAD) & (RING - 1))

            @pl.when(g >= RING)
            def _():
                out_wait(slot)

            lo, hi = _unpack_halves(xbuf[slot])
            gu = (jnp.dot(lo.astype(_BF16), wgu_bf[0:HALF, :], preferred_element_type=_F32)
                  + jnp.dot(hi.astype(_BF16), wgu_bf[HALF:, :], preferred_element_type=_F32))
            h = (_silu(gu[:, :EXPERT_FF]) * gu[:, EXPERT_FF:]).astype(_BF16)
            ybuf[slot] = _pack_halves(jnp.dot(h, wd_bf[...], preferred_element_type=_F32))
            out_start(g, slot)
            return carry

        lax.fori_loop(0, nblk, block, 0)

    @pl.when(e == N_EXPERTS - 1)
    def _():
        for i in range(RING):
            @pl.when(i < ntot)
            def _():
                out_wait((ntot - 1 - i) & (RING - 1))


def _ffn(blk0, nblk, ntot, xs, wg, wu, wd):
    grid_spec = pltpu.PrefetchScalarGridSpec(
        num_scalar_prefetch=3,
        grid=(N_EXPERTS,),
        in_specs=[pl.BlockSpec(memory_space=pl.ANY),
                  pl.BlockSpec((1, D_MODEL, EXPERT_FF), lambda e, *_: (e, 0, 0)),
                  pl.BlockSpec((1, D_MODEL, EXPERT_FF), lambda e, *_: (e, 0, 0)),
                  pl.BlockSpec((1, EXPERT_FF, D_MODEL), lambda e, *_: (e, 0, 0))],
        out_specs=pl.BlockSpec(memory_space=pl.ANY),
        scratch_shapes=[pltpu.VMEM((RING, ROW_BLOCK, HALF), jnp.uint32), pltpu.VMEM((RING, ROW_BLOCK, HALF), jnp.uint32),
                        pltpu.SemaphoreType.DMA((RING,)), pltpu.SemaphoreType.DMA((RING,)),
                        pltpu.VMEM((D_MODEL, 2 * EXPERT_FF), _BF16), pltpu.VMEM((EXPERT_FF, D_MODEL), _BF16)],
    )
    return pl.pallas_call(
        _ffn_body,
        grid_spec=grid_spec,
        out_shape=jax.ShapeDtypeStruct(xs.shape, jnp.uint32),
        compiler_params=_cparams("arbitrary"),
        name="ffn",
    )(blk0, nblk, ntot, xs, wg, wu, wd)


def _combine_body(pos_hbm, gate_ref, h1_ref, ys_hbm, wsg_ref, wsu_ref, wsd_ref, g_ref, b_ref,
                  out_ref, pos_smem, psem, ybuf, sem, *, tt):
    i = pl.program_id(0)
    cp = pltpu.make_async_copy(pos_hbm.at[i], pos_smem, psem)
    cp.start()
    cp.wait()

    def row_copy(t, k):
        return pltpu.make_async_copy(ys_hbm.at[pl.ds(pos_smem[k * tt + t], 1)], ybuf.at[k, pl.ds(t, 1)], sem)

    def issue(t, c):
        for k in range(TOP_K):
            row_copy(t, k).start(priority=k % 2)
        return c

    lax.fori_loop(0, tt, issue, 0, unroll=ISSUE_UNROLL)

    x = h1_ref[...]
    xb = x.astype(_BF16)
    shared = _mm(_silu(_mm(xb, wsg_ref[...])) * _mm(xb, wsu_ref[...]), wsd_ref[...])

    for k in range(TOP_K):
        pltpu.make_async_copy(ys_hbm.at[pl.ds(0, tt)], ybuf.at[k], sem).wait()

    gcol = gate_ref[...].T
    acc_lo = jnp.zeros((tt, HALF), _F32)
    acc_hi = jnp.zeros((tt, HALF), _F32)
    for k in range(TOP_K):
        lo, hi = _unpack_halves(ybuf[k])
        acc_lo = acc_lo + gcol[:, k:k + 1] * lo
        acc_hi = acc_hi + gcol[:, k:k + 1] * hi
    routed = jnp.concatenate([acc_lo, acc_hi], axis=1)
    out_ref[...] = _layer_norm(DN_ALPHA * x + (routed + shared), g_ref[...], b_ref[...])


def _combine(pos_tiles, gate, h1, ys, wsg, wsu, wsd, g, b, *, tt):
    t = h1.shape[0]
    full = lambda a: pl.BlockSpec(a.shape, lambda i: (0,) * a.ndim)
    return pl.pallas_call(
        functools.partial(_combine_body, tt=tt),
        grid=(t // tt,),
        in_specs=[pl.BlockSpec(memory_space=pl.ANY), pl.BlockSpec((TOP_K, tt), lambda i: (0, i)),
                  pl.BlockSpec((tt, D_MODEL), lambda i: (i, 0)), pl.BlockSpec(memory_space=pl.ANY),
                  full(wsg), full(wsu), full(wsd), full(g), full(b)],
        out_specs=pl.BlockSpec((tt, D_MODEL), lambda i: (i, 0)),
        out_shape=jax.ShapeDtypeStruct((t, D_MODEL), _F32),
        scratch_shapes=[pltpu.SMEM((TOP_K * tt,), _I32), pltpu.SemaphoreType.DMA,
                        pltpu.VMEM((TOP_K, tt, HALF), jnp.uint32), pltpu.SemaphoreType.DMA],
        compiler_params=_cparams("arbitrary"),
        name="combine",
    )(pos_tiles, gate, h1, ys, wsg, wsu, wsd, g, b)


def _pick(n, pref):
    t = min(n, pref)
    while n % t:
        t -= CHUNK
    return t


def _mixer(x, tails, s0, wts, gnw, *, lt, lg, nbb):
    yc, q, k, v, z, bgc, bgr, tails_out = _premix(x, tails, wts, lt=lt)
    bsz, seq, _ = x.shape
    nch = seq // CHUNK
    grow = bgr[:, GDN_HEADS:2 * GDN_HEADS, :].reshape(bsz, GDN_HEADS, nch, CHUNK)
    grow = grow.transpose(0, 2, 1, 3).reshape(bsz, nch, 1, STACK)
    yg, s_out = _gdn(q, k, v, z, bgc, grow, s0, gnw, lg=lg, nbb=nbb)
    return yc, yg, tails_out, s_out


def kernel(x, meta_tokens, w_in, conv_w, conv_norm_w, gdn_conv_w, a_log, dt_bias, gdn_norm_w, w_out,
           ln1_g, ln1_b, w_router, b_router, w_gate, w_up, w_down, ws_gate, ws_up, ws_down, ln2_g, ln2_b):
    assert w_in.shape[0] == 1, "single-layer stack"
    bsz, seq, d = x.shape
    assert d == D_MODEL and seq % CHUNK == 0
    c, gw = CONV_WIDTH, GDN_WIDTH
    win = w_in[0].astype(_BF16)
    wbd = win[:, 3 * c + 4 * gw:]
    zpad = jnp.zeros((128 - 2 * GDN_HEADS,), _F32)
    zpad4 = jnp.zeros((GDN_HEADS,), _F32)
    prow = jnp.zeros((8, 128), _F32)
    prow = prow.at[0].set(jnp.concatenate([zpad4, a_log[0], zpad]))
    prow = prow.at[1].set(jnp.concatenate([zpad4, dt_bias[0], zpad]))
    wts = (win[:, :3 * c], win[:, 3 * c:3 * c + 3 * gw], win[:, 3 * c + 3 * gw:3 * c + 4 * gw],
           jnp.pad(wbd, ((0, 0), (0, 128 - 2 * GDN_HEADS))), wbd.T,
           conv_w[0], conv_norm_w, gdn_conv_w[0], prow, prow.T[:8])
    gnw = gdn_norm_w

    meta = jnp.concatenate([jnp.zeros((CHUNK - N_META, d), x.dtype), meta_tokens.astype(x.dtype)])[None]
    tails0 = jnp.zeros((HIST, c + 3 * gw), _F32)
    s00 = jnp.zeros((GDN_HEADS, GDN_HEAD_DIM, GDN_HEAD_DIM), _F32)
    _, _, tails_m, s_m = _mixer(meta, tails0, s00, wts, gnw, lt=CHUNK, lg=CHUNK, nbb=1)

    yc, yg, _, _ = _mixer(x, tails_m[0], s_m[0], wts, gnw, lt=_pick(seq, 512), lg=_pick(seq, 512),
                          nbb=GDN_ROWS if bsz % GDN_ROWS == 0 else 1)

    t = bsz * seq
    tm = _pick(t, 512)
    h1, h1p = _outproj(yc.reshape(t, c), yg.reshape(t, gw), x.reshape(t, d), w_out[0].astype(_BF16),
                       ln1_g, ln1_b, tm=tm)

    tt = _pick(t, 256)
    wr_t = w_router[0].T
    wr_hi = wr_t.astype(_BF16)
    wr_lo = (wr_t - wr_hi.astype(_F32)).astype(_BF16)
    idx, gate, rank, cnt = _router(h1, wr_hi, wr_lo, b_router[0][:, None], tt=tt)

    counts = cnt[:, 0].astype(_I32)
    pcounts = (counts + ROW_BLOCK - 1) // ROW_BLOCK * ROW_BLOCK
    pends = jnp.cumsum(pcounts)
    pstarts = pends - pcounts
    nb = t * TOP_K // ROW_BLOCK + N_EXPERTS

    pos = _position(idx, rank, pstarts[:, None].astype(_I32), tt=tt).reshape(t // tt, TOP_K * tt)
    xs = _dispatch(pos, h1p, jnp.zeros((nb * ROW_BLOCK, HALF), jnp.uint32), tt=tt)
    ys = _ffn((pstarts // ROW_BLOCK).astype(_I32), (pcounts // ROW_BLOCK).astype(_I32),
              (pends[-1:] // ROW_BLOCK).astype(_I32), xs, w_gate[0], w_up[0], w_down[0])
    out = _combine(pos, gate, h1, ys, ws_gate[0].astype(_BF16), ws_up[0].astype(_BF16),
                   ws_down[0].astype(_BF16), ln2_g, ln2_b, tt=tt)
    return out.reshape(bsz, seq, d)
```

```python
import functools

import jax
import jax.numpy as jnp
from jax import lax
from jax.experimental import pallas as pl
from jax.experimental.pallas import tpu as pltpu

_F32 = jnp.float32
_BF16 = jnp.bfloat16
_I32 = jnp.int32

D_MODEL = 1024
N_META = 16
CONV_WIDTH = 512
CONV_K = 3
GDN_HEADS = 4
GDN_HEAD_DIM = 128
GDN_WIDTH = GDN_HEADS * GDN_HEAD_DIM
GDN_CONV_K = 4
CHUNK = 64
N_EXPERTS = 256
TOP_K = 8
N_GROUPS = 8
TOPK_GROUPS = 4
E_PER_GROUP = N_EXPERTS // N_GROUPS
EXPERT_FF = 256
ROUTED_SCALE = 2.5
ROW_BLOCK = 256
DN_ALPHA = 2.0 ** 0.25
NORM_EPS = 1e-5
HALF = D_MODEL // 2
QUAD = HALF // 128
STACK = GDN_HEADS * CHUNK
HIST = 8
GDN_ROWS = 4
DMA_SPLIT = 8
ISSUE_UNROLL = 8
RING = 4
IN_AHEAD = RING - 1

V7X_VMEM_BYTES = 64 * 1024 * 1024
VMEM_LIMIT = V7X_VMEM_BYTES - 8 * 1024 * 1024


def _cparams(*sem):
    return pltpu.CompilerParams(dimension_semantics=sem, vmem_limit_bytes=VMEM_LIMIT)


def _mm(a, b):
    return jnp.dot(a.astype(_BF16), b.astype(_BF16), preferred_element_type=_F32)


def _mm_nt(a, b):
    return lax.dot_general(a.astype(_BF16), b.astype(_BF16), (((1,), (1,)), ((), ())),
                           preferred_element_type=_F32)


def _mm_tn(a, b):
    return lax.dot_general(a.astype(_BF16), b.astype(_BF16), (((0,), (0,)), ((), ())),
                           preferred_element_type=_F32)


def _sigmoid(x):
    return 1.0 / (1.0 + jnp.exp(-x))


def _silu(x):
    return x * _sigmoid(x)


def _softplus(x):
    return jnp.maximum(x, 0.0) + jnp.log1p(jnp.exp(-jnp.abs(x)))


def _pack_halves(y):
    return pltpu.pack_elementwise([y[:, :HALF], y[:, HALF:]], packed_dtype=_BF16)


def _store_rows(ref, packed):
    r = packed.shape[0]
    for c in range(QUAD):
        ref[pl.ds(c, r, stride=QUAD), :] = packed[:, c * 128:(c + 1) * 128]


def _load_rows(ref, r):
    return jnp.concatenate([ref[pl.ds(c, r, stride=QUAD), :] for c in range(QUAD)], axis=1)


def _unpack_halves(p):
    lo = pltpu.unpack_elementwise(p, index=0, packed_dtype=_BF16, unpacked_dtype=_F32)
    hi = pltpu.unpack_elementwise(p, index=1, packed_dtype=_BF16, unpacked_dtype=_F32)
    return lo, hi


def _layer_norm(h, g, b):
    mu = jnp.mean(h, axis=-1, keepdims=True)
    d = h - mu
    var = jnp.mean(d * d, axis=-1, keepdims=True)
    return d * lax.rsqrt(var + NORM_EPS) * g + b


def _premix_body(x_ref, tails_ref, wa_ref, wq_ref, wz_ref, wbd_ref, wbdt_ref, cw_ref, cnw_ref,
                 gcw_ref, prow_ref, pcol_ref,
                 yc_ref, q_ref, k_ref, v_ref, z_ref, bgc_ref, bgr_ref, tout_ref, ext_ref, *, lt):
    cw_ = CONV_WIDTH

    @pl.when(pl.program_id(1) == 0)
    def _():
        ext_ref[0:HIST, :] = tails_ref[...]

    xb = x_ref[0].astype(_BF16)
    pa = jnp.dot(xb, wa_ref[...], preferred_element_type=_F32)
    gate_b = pa[:, 0:cw_]
    u = pa[:, cw_:2 * cw_] * pa[:, 2 * cw_:3 * cw_]
    ext_ref[HIST:HIST + lt, 0:cw_] = u
    pq = jnp.dot(xb, wq_ref[...], preferred_element_type=_F32)
    ext_ref[HIST:HIST + lt, cw_:] = pq

    cw = cw_ref[...]
    ca = u * cw[CONV_K - 1:CONV_K, :]
    for j in range(CONV_K - 1):
        ca = ca + ext_ref[pl.ds(HIST - (CONV_K - 1) + j, lt), 0:cw_] * cw[j:j + 1, :]
    yc = gate_b * ca
    ms = jnp.mean(yc * yc, axis=-1, keepdims=True)
    yc_ref[0] = (yc * lax.rsqrt(ms + NORM_EPS) * cnw_ref[...]).astype(_BF16)

    gcw = gcw_ref[...]
    cq = pq * gcw[GDN_CONV_K - 1:GDN_CONV_K, :]
    for j in range(GDN_CONV_K - 1):
        cq = cq + ext_ref[pl.ds(HIST - (GDN_CONV_K - 1) + j, lt), cw_:] * gcw[j:j + 1, :]
    s = _silu(cq)
    for h in range(GDN_HEADS):
        lo, hi = h * GDN_HEAD_DIM, (h + 1) * GDN_HEAD_DIM
        qh = s[:, lo:hi]
        kh = s[:, GDN_WIDTH + lo:GDN_WIDTH + hi]
        qn = qh * lax.rsqrt(jnp.sum(qh * qh, axis=-1, keepdims=True) + 1e-6)
        kn = kh * lax.rsqrt(jnp.sum(kh * kh, axis=-1, keepdims=True) + 1e-6)
        q_ref[0, :, lo:hi] = (qn * (GDN_HEAD_DIM ** -0.5)).astype(_BF16)
        k_ref[0, :, lo:hi] = kn.astype(_BF16)
    v_ref[0] = s[:, 2 * GDN_WIDTH:].astype(_BF16)
    z_ref[0] = jnp.dot(xb, wz_ref[...], preferred_element_type=_F32).astype(_BF16)

    bdc = jnp.dot(xb, wbd_ref[...], preferred_element_type=_F32)
    prow = prow_ref[...]
    g_c = -jnp.exp(prow[0:1, :]) * _softplus(bdc + prow[1:2, :])
    lane = lax.broadcasted_iota(_I32, bdc.shape, 1)
    bgc_ref[0] = jnp.where(lane < GDN_HEADS, _sigmoid(bdc), g_c)
    bdr = _mm_nt(wbdt_ref[...], xb)
    pcol = pcol_ref[...]
    g_r = -jnp.exp(pcol[:, 0:1]) * _softplus(bdr + pcol[:, 1:2])
    row = lax.broadcasted_iota(_I32, bdr.shape, 0)
    bgr_ref[0] = jnp.where(row < GDN_HEADS, _sigmoid(bdr), g_r)

    tail = ext_ref[lt:lt + HIST, :]
    ext_ref[0:HIST, :] = tail
    tout_ref[0] = tail


def _premix(x, tails, wts, *, lt):
    bsz, seq, d = x.shape
    assert seq % lt == 0
    grid = (bsz, seq // lt)
    full = lambda a: pl.BlockSpec(a.shape, lambda b, j: (0,) * a.ndim)
    tok = lambda w: pl.BlockSpec((1, lt, w), lambda b, j: (b, j, 0))
    (wa, wq, wz, wbd, wbdt, cw, cnw, gcw, prow, pcol) = wts
    ext_w = CONV_WIDTH + 3 * GDN_WIDTH
    out_shape = (
        jax.ShapeDtypeStruct((bsz, seq, CONV_WIDTH), _BF16),
        jax.ShapeDtypeStruct((bsz, seq, GDN_WIDTH), _BF16),
        jax.ShapeDtypeStruct((bsz, seq, GDN_WIDTH), _BF16),
        jax.ShapeDtypeStruct((bsz, seq, GDN_WIDTH), _BF16),
        jax.ShapeDtypeStruct((bsz, seq, GDN_WIDTH), _BF16),
        jax.ShapeDtypeStruct((bsz, seq, 128), _F32),
        jax.ShapeDtypeStruct((bsz, 8, seq), _F32),
        jax.ShapeDtypeStruct((bsz, HIST, ext_w), _F32),
    )
    out_specs = (tok(CONV_WIDTH), tok(GDN_WIDTH), tok(GDN_WIDTH), tok(GDN_WIDTH), tok(GDN_WIDTH),
                 tok(128), pl.BlockSpec((1, 8, lt), lambda b, j: (b, 0, j)),
                 pl.BlockSpec((1, HIST, ext_w), lambda b, j: (b, 0, 0)))
    return pl.pallas_call(
        functools.partial(_premix_body, lt=lt),
        grid=grid,
        in_specs=[tok(d), full(tails)] + [full(w) for w in wts],
        out_specs=out_specs,
        out_shape=out_shape,
        scratch_shapes=[pltpu.VMEM((HIST + lt, ext_w), _F32)],
        compiler_params=_cparams("arbitrary", "arbitrary"),
        name="premix",
    )(x, tails, *wts)


def _cumsum_rows(x):
    row = lax.broadcasted_iota(_I32, x.shape, 0)
    s = 1
    while s < x.shape[0]:
        x = x + jnp.where(row >= s, pltpu.roll(x, s, 0), 0.0)
        s *= 2
    return x


def _cumsum_lanes_seg(x):
    lane = lax.broadcasted_iota(_I32, x.shape, 1) & (CHUNK - 1)
    s = 1
    while s < CHUNK:
        x = x + jnp.where(lane >= s, pltpu.roll(x, s, 1), 0.0)
        s *= 2
    return x


def _stack_heads(a):
    return jnp.concatenate([a[:, h * GDN_HEAD_DIM:(h + 1) * GDN_HEAD_DIM] for h in range(GDN_HEADS)], axis=0)


def _gdn_body(q_ref, k_ref, v_ref, z_ref, bgc_ref, grow_ref, s0_ref, gnw_ref,
              y_ref, sout_ref, s_ref, *, nc, nbb):
    @pl.when(pl.program_id(1) == 0)
    def _():
        for r in range(nbb):
            s_ref[r] = s0_ref[...]

    ri = lax.broadcasted_iota(_I32, (STACK, STACK), 0)
    ci = lax.broadcasted_iota(_I32, (STACK, STACK), 1)
    same64 = (ri >> 6) == (ci >> 6)
    same32 = (ri >> 5) == (ci >> 5)
    same16 = (ri >> 4) == (ci >> 4)
    low_incl = same64 & (ri >= ci)
    low_strict = same64 & (ri > ci)
    gnw = gnw_ref[...]

    def chunk_row(r, c):
        off = pl.multiple_of(c * CHUNK, CHUNK)
        q_all = _stack_heads(q_ref[r, pl.ds(off, CHUNK), :].astype(_F32))
        k_all = _stack_heads(k_ref[r, pl.ds(off, CHUNK), :].astype(_F32))
        v_all = _stack_heads(v_ref[r, pl.ds(off, CHUNK), :].astype(_F32))
        bgc = bgc_ref[r, pl.ds(off, CHUNK), :]
        gcs = _cumsum_rows(bgc)
        hd = (CHUNK, GDN_HEAD_DIM)
        beta_b = jnp.concatenate(
            [jnp.broadcast_to(bgc[:, h:h + 1], hd) for h in range(GDN_HEADS)], axis=0)
        gc_b = jnp.concatenate(
            [jnp.broadcast_to(gcs[:, GDN_HEADS + h:GDN_HEADS + h + 1], hd) for h in range(GDN_HEADS)], axis=0)
        gl = [gcs[CHUNK - 1:CHUNK, GDN_HEADS + h:GDN_HEADS + h + 1] for h in range(GDN_HEADS)]
        gl_b = jnp.concatenate([jnp.broadcast_to(g1, hd) for g1 in gl], axis=0)
        gcr = _cumsum_lanes_seg(jnp.broadcast_to(grow_ref[r, c], (8, STACK)))[0:1, :]

        diff = jnp.concatenate([gc_b, gc_b], axis=1) - gcr
        decay = jnp.exp(jnp.where(low_incl, diff, -1e30))
        kb = k_all * beta_b
        a1 = _mm_nt(jnp.concatenate([kb, q_all], axis=0), k_all)
        yield
        m = jnp.where(low_strict, a1[:STACK] * decay, 0.0)
        attn = a1[STACK:] * decay

        l16 = jnp.where(same16, m, 0.0)
        c1 = jnp.where(same32 & jnp.logical_not(same16), m, 0.0)
        c2 = jnp.where(same32, 0.0, m)
        p2 = _mm(l16, l16)
        yield
        p4 = _mm(p2, p2)
        t = _mm(l16, p2)
        yield
        na = p2 - l16 - t
        p8 = _mm(p4, p4)
        t = _mm(na, p4)
        yield
        nb = na + p4 + t
        t = _mm(nb, p8)
        yield
        ncm = nb + p8 + t
        t = _mm(c1, ncm)
        yield
        y1 = c1 + t
        t = _mm(ncm, y1)
        yield
        n1 = ncm - y1 - t
        t = _mm(c2, n1)
        yield
        y2 = c2 + t
        t = _mm(n1, y2)
        yield
        nt = n1 - y2 - t

        egc = jnp.exp(gc_b)
        rhs = jnp.concatenate([v_all * beta_b, kb * egc], axis=1)
        t = _mm(nt, rhs)
        yield
        uw = rhs + t
        u_all = uw[:, :GDN_HEAD_DIM]
        w_all = uw[:, GDN_HEAD_DIM:]
        qd = q_all * egc
        kd = k_all * jnp.exp(gl_b - gc_b)

        bs = []
        for h in range(GDN_HEADS):
            r0, r1 = h * CHUNK, (h + 1) * CHUNK
            bs.append(_mm(jnp.concatenate([w_all[r0:r1], qd[r0:r1]], axis=0), s_ref[r, h]))
        yield
        vn = [u_all[h * CHUNK:(h + 1) * CHUNK] - bs[h][:CHUNK] for h in range(GDN_HEADS)]
        vn_all = jnp.concatenate(vn, axis=0)
        t = _mm(attn, vn_all)
        ds = [_mm_tn(kd[h * CHUNK:(h + 1) * CHUNK], vn[h]) for h in range(GDN_HEADS)]
        yield
        o_all = jnp.concatenate([b[CHUNK:] for b in bs], axis=0) + t
        for h in range(GDN_HEADS):
            r0, r1 = h * CHUNK, (h + 1) * CHUNK
            s_ref[r, h] = s_ref[r, h] * jnp.exp(gl[h]) + ds[h]
            o = o_all[r0:r1]
            zz = z_ref[r, pl.ds(off, CHUNK), h * GDN_HEAD_DIM:(h + 1) * GDN_HEAD_DIM].astype(_F32)
            on = o * lax.rsqrt(jnp.mean(o * o, axis=-1, keepdims=True) + NORM_EPS) * gnw
            y_ref[r, pl.ds(off, CHUNK), h * GDN_HEAD_DIM:(h + 1) * GDN_HEAD_DIM] = (on * _silu(zz)).astype(_BF16)

    def chunk(c, carry):
        live = [chunk_row(r, c) for r in range(nbb)]
        while live:
            live = [g for g in live if next(g, live) is not live]
        return carry

    lax.fori_loop(0, nc, chunk, 0)
    sout_ref[...] = s_ref[...]


def _gdn(q, k, v, z, bgc, grow, s0, gnw, *, lg, nbb):
    bsz, seq, _ = q.shape
    assert seq % lg == 0 and lg % CHUNK == 0 and bsz % nbb == 0
    nc = lg // CHUNK
    tok = lambda w: pl.BlockSpec((nbb, lg, w), lambda b, j: (b, j, 0))
    full = lambda a: pl.BlockSpec(a.shape, lambda b, j: (0,) * a.ndim)
    st = (nbb, GDN_HEADS, GDN_HEAD_DIM, GDN_HEAD_DIM)
    return pl.pallas_call(
        functools.partial(_gdn_body, nc=nc, nbb=nbb),
        grid=(bsz // nbb, seq // lg),
        in_specs=[tok(GDN_WIDTH)] * 4 + [tok(128), pl.BlockSpec((nbb, nc, 1, STACK), lambda b, j: (b, j, 0, 0)),
                                           full(s0), full(gnw)],
        out_specs=(tok(GDN_WIDTH), pl.BlockSpec(st, lambda b, j: (b, 0, 0, 0))),
        out_shape=(jax.ShapeDtypeStruct((bsz, seq, GDN_WIDTH), _BF16),
                   jax.ShapeDtypeStruct((bsz, GDN_HEADS, GDN_HEAD_DIM, GDN_HEAD_DIM), _F32)),
        scratch_shapes=[pltpu.VMEM(st, _F32)],
        compiler_params=_cparams("arbitrary", "arbitrary"),
        name="gdn",
    )(q, k, v, z, bgc, grow, s0, gnw)


def _outproj_body(yc_ref, yg_ref, x_ref, wo_ref, g_ref, b_ref, h1_ref, h1p_ref):
    mix = (jnp.dot(yc_ref[...], wo_ref[0:CONV_WIDTH, :], preferred_element_type=_F32)
           + jnp.dot(yg_ref[...], wo_ref[CONV_WIDTH:, :], preferred_element_type=_F32))
    h1 = _layer_norm(DN_ALPHA * x_ref[...] + mix, g_ref[...], b_ref[...])
    h1_ref[...] = h1
    _store_rows(h1p_ref, _pack_halves(h1))


def _outproj(yc, yg, x2d, wo, g, b, *, tm):
    t = x2d.shape[0]
    assert t % tm == 0
    row = lambda w: pl.BlockSpec((tm, w), lambda i: (i, 0))
    full = lambda a: pl.BlockSpec(a.shape, lambda i: (0,) * a.ndim)
    return pl.pallas_call(
        _outproj_body,
        grid=(t // tm,),
        in_specs=[row(CONV_WIDTH), row(GDN_WIDTH), row(D_MODEL), full(wo), full(g), full(b)],
        out_specs=(row(D_MODEL), pl.BlockSpec((tm * QUAD, 128), lambda i: (i, 0))),
        out_shape=(jax.ShapeDtypeStruct((t, D_MODEL), _F32), jax.ShapeDtypeStruct((t * QUAD, 128), jnp.uint32)),
        compiler_params=_cparams("arbitrary"),
        name="outproj",
    )(yc, yg, x2d, wo, g, b)


def _router_body(h1_ref, wh_ref, wl_ref, br_ref, idx_ref, gate_ref, rank_ref, cnt_ref, carry_ref, *, tt):
    @pl.when(pl.program_id(0) == 0)
    def _():
        carry_ref[...] = jnp.zeros_like(carry_ref)

    x = h1_ref[...]
    xh = x.astype(_BF16)
    xl = (x - xh.astype(_F32)).astype(_BF16)
    wh = wh_ref[...]
    logits = _mm_nt(wh, xh) + _mm_nt(wh, xl) + _mm_nt(wl_ref[...], xh)
    scores = _sigmoid(logits)
    sel = scores + br_ref[...]
    ninf = -jnp.inf

    r32 = lax.broadcasted_iota(_I32, (E_PER_GROUP, tt), 0)
    gsc = []
    for g in range(N_GROUPS):
        xg = sel[g * E_PER_GROUP:(g + 1) * E_PER_GROUP]
        m1 = jnp.max(xg, axis=0, keepdims=True)
        i1 = jnp.min(jnp.where(xg == m1, r32, E_PER_GROUP), axis=0, keepdims=True)
        m2 = jnp.max(jnp.where(r32 == i1, ninf, xg), axis=0, keepdims=True)
        gsc.append(m1 + m2)
    work = jnp.concatenate(gsc, axis=0)
    r8 = lax.broadcasted_iota(_I32, (N_GROUPS, tt), 0)
    gkeep = jnp.zeros((N_GROUPS, tt), _F32)
    for _ in range(TOPK_GROUPS):
        m = jnp.max(work, axis=0, keepdims=True)
        gi = jnp.min(jnp.where(work == m, r8, N_GROUPS), axis=0, keepdims=True)
        pick = r8 == gi
        gkeep = jnp.where(pick, 1.0, gkeep)
        work = jnp.where(pick, ninf, work)
    selm = jnp.concatenate(
        [jnp.where(gkeep[g:g + 1] > 0.5, sel[g * E_PER_GROUP:(g + 1) * E_PER_GROUP], ninf)
         for g in range(N_GROUPS)], axis=0)

    re = lax.broadcasted_iota(_I32, (N_EXPERTS, tt), 0)
    msel = jnp.zeros((N_EXPERTS, tt), _F32)
    idxs, gates = [], []
    for _ in range(TOP_K):
        m = jnp.max(selm, axis=0, keepdims=True)
        ii = jnp.min(jnp.where(selm == m, re, N_EXPERTS), axis=0, keepdims=True)
        hit = re == ii
        idxs.append(ii)
        gates.append(jnp.sum(jnp.where(hit, scores, 0.0), axis=0, keepdims=True))
        selm = jnp.where(hit, ninf, selm)
        msel = jnp.where(hit, 1.0, msel)
    gate = jnp.concatenate(gates, axis=0)
    gate_ref[...] = gate / jnp.sum(gate, axis=0, keepdims=True) * ROUTED_SCALE
    idx_ref[...] = jnp.concatenate(idxs, axis=0)

    ta = lax.broadcasted_iota(_I32, (tt, tt), 0)
    tb = lax.broadcasted_iota(_I32, (tt, tt), 1)
    earlier = jnp.where(ta < tb, 1.0, 0.0)
    carry = carry_ref[...]
    rank_all = _mm(msel, earlier) + carry[:, 0:1]
    rank_ref[...] = jnp.concatenate(
        [jnp.sum(jnp.where(re == ii, rank_all, 0.0), axis=0, keepdims=True) for ii in idxs],
        axis=0).astype(_I32)
    carry = carry + jnp.sum(msel, axis=1, keepdims=True)
    carry_ref[...] = carry
    cnt_ref[...] = carry


def _router(h1, wh, wl, br, *, tt):
    t = h1.shape[0]
    assert t % tt == 0
    full = lambda a: pl.BlockSpec(a.shape, lambda i: (0,) * a.ndim)
    kt = pl.BlockSpec((TOP_K, tt), lambda i: (0, i))
    return pl.pallas_call(
        functools.partial(_router_body, tt=tt),
        grid=(t // tt,),
        in_specs=[pl.BlockSpec((tt, D_MODEL), lambda i: (i, 0)), full(wh), full(wl), full(br)],
        out_specs=(kt, kt, kt, pl.BlockSpec((N_EXPERTS, 128), lambda i: (0, 0))),
        out_shape=(jax.ShapeDtypeStruct((TOP_K, t), _I32), jax.ShapeDtypeStruct((TOP_K, t), _F32),
                   jax.ShapeDtypeStruct((TOP_K, t), _I32), jax.ShapeDtypeStruct((N_EXPERTS, 128), _F32)),
        scratch_shapes=[pltpu.VMEM((N_EXPERTS, 128), _F32)],
        compiler_params=_cparams("arbitrary"),
        name="router",
    )(h1, wh, wl, br)


def _position_body(idx_ref, rank_ref, pstart_ref, pos_ref, *, tt):
    re = lax.broadcasted_iota(_I32, (N_EXPERTS, tt), 0)
    ps = pstart_ref[...]
    idx = idx_ref[...]
    rows = [jnp.sum(jnp.where(re == idx[k:k + 1], ps, 0), axis=0, keepdims=True) for k in range(TOP_K)]
    pos_ref[0] = jnp.concatenate(rows, axis=0) + rank_ref[...]


def _position(idx, rank, pstart, *, tt):
    t = idx.shape[1]
    kt = pl.BlockSpec((TOP_K, tt), lambda i: (0, i))
    return pl.pallas_call(
        functools.partial(_position_body, tt=tt),
        grid=(t // tt,),
        in_specs=[kt, kt, pl.BlockSpec(pstart.shape, lambda i: (0, 0))],
        out_specs=pl.BlockSpec((1, TOP_K, tt), lambda i: (i, 0, 0)),
        out_shape=jax.ShapeDtypeStruct((t // tt, TOP_K, tt), _I32),
        compiler_params=_cparams("arbitrary"),
        name="position",
    )(idx, rank, pstart)


def _dispatch_body(pos_hbm, h1p_ref, xs_in, xs_out, pos_smem, psem, sem, *, tt):
    del xs_in
    i = pl.program_id(0)
    cp = pltpu.make_async_copy(pos_hbm.at[i], pos_smem, psem)
    cp.start()
    cp.wait()

    def row_copy(t, k):
        return pltpu.make_async_copy(h1p_ref.at[pl.ds(QUAD * t, QUAD)],
                                     xs_out.at[pl.ds(QUAD * pos_smem[k * tt + t], QUAD)], sem)

    def issue(t, c):
        for k in range(TOP_K):
            row_copy(t, k).start(priority=k % 2)
        return c

    lax.fori_loop(0, tt, issue, 0, unroll=ISSUE_UNROLL)

    for k in range(TOP_K):
        pltpu.make_async_copy(h1p_ref, xs_out.at[pl.ds(0, QUAD * tt)], sem).wait()


def _dispatch(pos_tiles, h1p, xs_zero, *, tt):
    t = h1p.shape[0] // QUAD
    return pl.pallas_call(
        functools.partial(_dispatch_body, tt=tt),
        grid=(t // tt,),
        in_specs=[pl.BlockSpec(memory_space=pl.ANY), pl.BlockSpec((tt * QUAD, 128), lambda i: (i, 0)),
                  pl.BlockSpec(memory_space=pl.ANY)],
        out_specs=pl.BlockSpec(memory_space=pl.ANY),
        out_shape=jax.ShapeDtypeStruct(xs_zero.shape, xs_zero.dtype),
        scratch_shapes=[pltpu.SMEM((TOP_K * tt,), _I32), pltpu.SemaphoreType.DMA, pltpu.SemaphoreType.DMA],
        input_output_aliases={2: 0},
        compiler_params=_cparams("arbitrary"),
        name="dispatch",
    )(pos_tiles, h1p, xs_zero)


def _ffn_body(blk0_ref, nblk_ref, ntot_ref, xs_hbm, wg_ref, wu_ref, wd_ref, ys_hbm,
              xbuf, ybuf, sem_in, sem_out, wgu_bf, wd_bf):
    e = pl.program_id(0)
    nblk = nblk_ref[e]
    blk0 = blk0_ref[e]
    ntot = ntot_ref[0]

    blk_rows = ROW_BLOCK * QUAD

    def rows(g):
        return pl.ds(pl.multiple_of(g * blk_rows, blk_rows), blk_rows)

    def in_start(g, slot):
        pltpu.make_async_copy(xs_hbm.at[rows(g)], xbuf.at[slot], sem_in.at[slot]).start()

    def in_wait(slot):
        pltpu.make_async_copy(xs_hbm.at[rows(0)], xbuf.at[slot], sem_in.at[slot]).wait()

    def out_start(g, slot):
        pltpu.make_async_copy(ybuf.at[slot], ys_hbm.at[rows(g)], sem_out.at[slot]).start()

    def out_wait(slot):
        pltpu.make_async_copy(ybuf.at[slot], ys_hbm.at[rows(0)], sem_out.at[slot]).wait()

    @pl.when(e == 0)
    def _():
        for i in range(IN_AHEAD):
            @pl.when(i < ntot)
            def _():
                in_start(i, i)

    @pl.when(nblk > 0)
    def _():
        wgu_bf[:, 0:EXPERT_FF] = wg_ref[0].astype(_BF16)
        wgu_bf[:, EXPERT_FF:] = wu_ref[0].astype(_BF16)
        wd_bf[...] = wd_ref[0].astype(_BF16)

        def block(j, carry):
            g = blk0 + j
            slot = g & (RING - 1)
            in_wait(slot)

            @pl.when(g + IN_AHEAD < ntot)
            def _():
                in_start(g + IN_AHEAD, (g + IN_AHEAD) & (RING - 1))

            @pl.when(g >= RING)
            def _():
                out_wait(slot)

            lo, hi = _unpack_halves(_load_rows(xbuf.at[slot], ROW_BLOCK))
            gu = (jnp.dot(lo.astype(_BF16), wgu_bf[0:HALF, :], preferred_element_type=_F32)
                  + jnp.dot(hi.astype(_BF16), wgu_bf[HALF:, :], preferred_element_type=_F32))
            h = (_silu(gu[:, :EXPERT_FF]) * gu[:, EXPERT_FF:]).astype(_BF16)
            _store_rows(ybuf.at[slot], _pack_halves(jnp.dot(h, wd_bf[...], preferred_element_type=_F32)))
            out_start(g, slot)
            return carry

        lax.fori_loop(0, nblk, block, 0)

    @pl.when(e == N_EXPERTS - 1)
    def _():
        for i in range(RING):
            @pl.when(i < ntot)
            def _():
                out_wait((ntot - 1 - i) & (RING - 1))


def _ffn(blk0, nblk, ntot, xs, wg, wu, wd):
    grid_spec = pltpu.PrefetchScalarGridSpec(
        num_scalar_prefetch=3,
        grid=(N_EXPERTS,),
        in_specs=[pl.BlockSpec(memory_space=pl.ANY),
                  pl.BlockSpec((1, D_MODEL, EXPERT_FF), lambda e, *_: (e, 0, 0)),
                  pl.BlockSpec((1, D_MODEL, EXPERT_FF), lambda e, *_: (e, 0, 0)),
                  pl.BlockSpec((1, EXPERT_FF, D_MODEL), lambda e, *_: (e, 0, 0))],
        out_specs=pl.BlockSpec(memory_space=pl.ANY),
        scratch_shapes=[pltpu.VMEM((RING, ROW_BLOCK * QUAD, 128), jnp.uint32),
                        pltpu.VMEM((RING, ROW_BLOCK * QUAD, 128), jnp.uint32),
                        pltpu.SemaphoreType.DMA((RING,)), pltpu.SemaphoreType.DMA((RING,)),
                        pltpu.VMEM((D_MODEL, 2 * EXPERT_FF), _BF16), pltpu.VMEM((EXPERT_FF, D_MODEL), _BF16)],
    )
    return pl.pallas_call(
        _ffn_body,
        grid_spec=grid_spec,
        out_shape=jax.ShapeDtypeStruct(xs.shape, jnp.uint32),
        compiler_params=_cparams("arbitrary"),
        name="ffn",
    )(blk0, nblk, ntot, xs, wg, wu, wd)


def _combine_body(pos_hbm, gate_ref, h1_ref, ys_hbm, wsg_ref, wsu_ref, wsd_ref, g_ref, b_ref,
                  out_ref, pos_smem, psem, ybuf, sem, *, tt):
    i = pl.program_id(0)
    cp = pltpu.make_async_copy(pos_hbm.at[i], pos_smem, psem)
    cp.start()
    cp.wait()

    def row_copy(t, k):
        return pltpu.make_async_copy(ys_hbm.at[pl.ds(QUAD * pos_smem[k * tt + t], QUAD)],
                                     ybuf.at[k, pl.ds(QUAD * t, QUAD)], sem)

    def issue(t, c):
        for k in range(TOP_K):
            row_copy(t, k).start(priority=k % 2)
        return c

    lax.fori_loop(0, tt, issue, 0, unroll=ISSUE_UNROLL)

    x = h1_ref[...]
    xb = x.astype(_BF16)
    shared = _mm(_silu(_mm(xb, wsg_ref[...])) * _mm(xb, wsu_ref[...]), wsd_ref[...])

    for k in range(TOP_K):
        pltpu.make_async_copy(ys_hbm.at[pl.ds(0, QUAD * tt)], ybuf.at[k], sem).wait()

    gcol = gate_ref[...].T
    acc_lo = jnp.zeros((tt, HALF), _F32)
    acc_hi = jnp.zeros((tt, HALF), _F32)
    for k in range(TOP_K):
        lo, hi = _unpack_halves(_load_rows(ybuf.at[k], tt))
        acc_lo = acc_lo + gcol[:, k:k + 1] * lo
        acc_hi = acc_hi + gcol[:, k:k + 1] * hi
    routed = jnp.concatenate([acc_lo, acc_hi], axis=1)
    out_ref[...] = _layer_norm(DN_ALPHA * x + (routed + shared), g_ref[...], b_ref[...])


def _combine(pos_tiles, gate, h1, ys, wsg, wsu, wsd, g, b, *, tt):
    t = h1.shape[0]
    full = lambda a: pl.BlockSpec(a.shape, lambda i: (0,) * a.ndim)
    return pl.pallas_call(
        functools.partial(_combine_body, tt=tt),
        grid=(t // tt,),
        in_specs=[pl.BlockSpec(memory_space=pl.ANY), pl.BlockSpec((TOP_K, tt), lambda i: (0, i)),
                  pl.BlockSpec((tt, D_MODEL), lambda i: (i, 0)), pl.BlockSpec(memory_space=pl.ANY),
                  full(wsg), full(wsu), full(wsd), full(g), full(b)],
        out_specs=pl.BlockSpec((tt, D_MODEL), lambda i: (i, 0)),
        out_shape=jax.ShapeDtypeStruct((t, D_MODEL), _F32),
        scratch_shapes=[pltpu.SMEM((TOP_K * tt,), _I32), pltpu.SemaphoreType.DMA,
                        pltpu.VMEM((TOP_K, tt * QUAD, 128), jnp.uint32), pltpu.SemaphoreType.DMA],
        compiler_params=_cparams("arbitrary"),
        name="combine",
    )(pos_tiles, gate, h1, ys, wsg, wsu, wsd, g, b)


def _pick(n, pref):
    t = min(n, pref)
    while n % t:
        t -= CHUNK
    return t


def _mixer(x, tails, s0, wts, gnw, *, lt, lg, nbb):
    yc, q, k, v, z, bgc, bgr, tails_out = _premix(x, tails, wts, lt=lt)
    bsz, seq, _ = x.shape
    nch = seq // CHUNK
    grow = bgr[:, GDN_HEADS:2 * GDN_HEADS, :].reshape(bsz, GDN_HEADS, nch, CHUNK)
    grow = grow.transpose(0, 2, 1, 3).reshape(bsz, nch, 1, STACK)
    yg, s_out = _gdn(q, k, v, z, bgc, grow, s0, gnw, lg=lg, nbb=nbb)
    return yc, yg, tails_out, s_out


def kernel(x, meta_tokens, w_in, conv_w, conv_norm_w, gdn_conv_w, a_log, dt_bias, gdn_norm_w, w_out,
           ln1_g, ln1_b, w_router, b_router, w_gate, w_up, w_down, ws_gate, ws_up, ws_down, ln2_g, ln2_b):
    assert w_in.shape[0] == 1, "single-layer stack"
    bsz, seq, d = x.shape
    assert d == D_MODEL and seq % CHUNK == 0
    c, gw = CONV_WIDTH, GDN_WIDTH
    win = w_in[0].astype(_BF16)
    wbd = win[:, 3 * c + 4 * gw:]
    zpad = jnp.zeros((128 - 2 * GDN_HEADS,), _F32)
    zpad4 = jnp.zeros((GDN_HEADS,), _F32)
    prow = jnp.zeros((8, 128), _F32)
    prow = prow.at[0].set(jnp.concatenate([zpad4, a_log[0], zpad]))
    prow = prow.at[1].set(jnp.concatenate([zpad4, dt_bias[0], zpad]))
    wts = (win[:, :3 * c], win[:, 3 * c:3 * c + 3 * gw], win[:, 3 * c + 3 * gw:3 * c + 4 * gw],
           jnp.pad(wbd, ((0, 0), (0, 128 - 2 * GDN_HEADS))), wbd.T,
           conv_w[0], conv_norm_w, gdn_conv_w[0], prow, prow.T[:8])
    gnw = gdn_norm_w

    meta = jnp.concatenate([jnp.zeros((CHUNK - N_META, d), x.dtype), meta_tokens.astype(x.dtype)])[None]
    tails0 = jnp.zeros((HIST, c + 3 * gw), _F32)
    s00 = jnp.zeros((GDN_HEADS, GDN_HEAD_DIM, GDN_HEAD_DIM), _F32)
    _, _, tails_m, s_m = _mixer(meta, tails0, s00, wts, gnw, lt=CHUNK, lg=CHUNK, nbb=1)

    yc, yg, _, _ = _mixer(x, tails_m[0], s_m[0], wts, gnw, lt=_pick(seq, 512), lg=_pick(seq, 512),
                          nbb=GDN_ROWS if bsz % GDN_ROWS == 0 else 1)

    t = bsz * seq
    tm = _pick(t, 512)
    h1, h1p = _outproj(yc.reshape(t, c), yg.reshape(t, gw), x.reshape(t, d), w_out[0].astype(_BF16),
                       ln1_g, ln1_b, tm=tm)

    tt = _pick(t, 256)
    wr_t = w_router[0].T
    wr_hi = wr_t.astype(_BF16)
    wr_lo = (wr_t - wr_hi.astype(_F32)).astype(_BF16)
    idx, gate, rank, cnt = _router(h1, wr_hi, wr_lo, b_router[0][:, None], tt=tt)

    counts = cnt[:, 0].astype(_I32)
    pcounts = (counts + ROW_BLOCK - 1) // ROW_BLOCK * ROW_BLOCK
    pends = jnp.cumsum(pcounts)
    pstarts = pends - pcounts
    nb = t * TOP_K // ROW_BLOCK + N_EXPERTS

    pos = _position(idx, rank, pstarts[:, None].astype(_I32), tt=tt).reshape(t // tt, TOP_K * tt)
    xs = _dispatch(pos, h1p, jnp.zeros((nb * ROW_BLOCK * QUAD, 128), jnp.uint32), tt=tt)
    ys = _ffn((pstarts // ROW_BLOCK).astype(_I32), (pcounts // ROW_BLOCK).astype(_I32),
              (pends[-1:] // ROW_BLOCK).astype(_I32), xs, w_gate[0], w_up[0], w_down[0])
    out = _combine(pos, gate, h1, ys, ws_gate[0].astype(_BF16), ws_up[0].astype(_BF16),
                   ws_down[0].astype(_BF16), ln2_g, ln2_b, tt=tt)
    return out.reshape(bsz, seq, d)
```

```python
import functools

import jax
import jax.numpy as jnp
from jax import lax
from jax.experimental import pallas as pl
from jax.experimental.pallas import tpu as pltpu

_F32 = jnp.float32
_BF16 = jnp.bfloat16
_I32 = jnp.int32

D_MODEL = 1024
N_META = 16
CONV_WIDTH = 512
CONV_K = 3
GDN_HEADS = 4
GDN_HEAD_DIM = 128
GDN_WIDTH = GDN_HEADS * GDN_HEAD_DIM
GDN_CONV_K = 4
CHUNK = 64
N_EXPERTS = 256
TOP_K = 8
N_GROUPS = 8
TOPK_GROUPS = 4
E_PER_GROUP = N_EXPERTS // N_GROUPS
EXPERT_FF = 256
ROUTED_SCALE = 2.5
ROW_BLOCK = 256
DN_ALPHA = 2.0 ** 0.25
NORM_EPS = 1e-5
HALF = D_MODEL // 2
QUAD = HALF // 128
STACK = GDN_HEADS * CHUNK
HIST = 8
GDN_ROWS = 4
ISSUE_UNROLL = 8
RING = 4
IN_AHEAD = RING - 1

V7X_VMEM_BYTES = 64 * 1024 * 1024
VMEM_LIMIT = V7X_VMEM_BYTES - 8 * 1024 * 1024


def _cparams(*sem):
    return pltpu.CompilerParams(dimension_semantics=sem, vmem_limit_bytes=VMEM_LIMIT)


def _mm(a, b):
    return jnp.dot(a.astype(_BF16), b.astype(_BF16), preferred_element_type=_F32)


def _mm_nt(a, b):
    return lax.dot_general(a.astype(_BF16), b.astype(_BF16), (((1,), (1,)), ((), ())),
                           preferred_element_type=_F32)


def _mm_tn(a, b):
    return lax.dot_general(a.astype(_BF16), b.astype(_BF16), (((0,), (0,)), ((), ())),
                           preferred_element_type=_F32)


def _sigmoid(x):
    return 1.0 / (1.0 + jnp.exp(-x))


def _silu(x):
    return x * _sigmoid(x)


def _softplus(x):
    return jnp.maximum(x, 0.0) + jnp.log1p(jnp.exp(-jnp.abs(x)))


def _pack_halves(y):
    return pltpu.pack_elementwise([y[:, :HALF], y[:, HALF:]], packed_dtype=_BF16)


def _store_rows(ref, packed):
    r = packed.shape[0]
    for c in range(QUAD):
        ref[pl.ds(c, r, stride=QUAD), :] = packed[:, c * 128:(c + 1) * 128]


def _load_rows(ref, r):
    return jnp.concatenate([ref[pl.ds(c, r, stride=QUAD), :] for c in range(QUAD)], axis=1)


def _unpack_halves(p):
    lo = pltpu.unpack_elementwise(p, index=0, packed_dtype=_BF16, unpacked_dtype=_F32)
    hi = pltpu.unpack_elementwise(p, index=1, packed_dtype=_BF16, unpacked_dtype=_F32)
    return lo, hi


def _layer_norm(h, g, b):
    mu = jnp.mean(h, axis=-1, keepdims=True)
    d = h - mu
    var = jnp.mean(d * d, axis=-1, keepdims=True)
    return d * lax.rsqrt(var + NORM_EPS) * g + b


def _premix_body(x_ref, tails_ref, wa_ref, wq_ref, wz_ref, wbd_ref, wbdt_ref, cw_ref, cnw_ref,
                 gcw_ref, prow_ref, pcol_ref,
                 yc_ref, q_ref, k_ref, v_ref, z_ref, bgc_ref, bgr_ref, tout_ref, ext_ref, *, lt):
    cw_ = CONV_WIDTH

    @pl.when(pl.program_id(1) == 0)
    def _():
        ext_ref[0:HIST, :] = tails_ref[...]

    xb = x_ref[0].astype(_BF16)
    pa = jnp.dot(xb, wa_ref[...], preferred_element_type=_F32)
    gate_b = pa[:, 0:cw_]
    u = pa[:, cw_:2 * cw_] * pa[:, 2 * cw_:3 * cw_]
    ext_ref[HIST:HIST + lt, 0:cw_] = u
    pq = jnp.dot(xb, wq_ref[...], preferred_element_type=_F32)
    ext_ref[HIST:HIST + lt, cw_:] = pq

    cw = cw_ref[...]
    ca = u * cw[CONV_K - 1:CONV_K, :]
    for j in range(CONV_K - 1):
        ca = ca + ext_ref[pl.ds(HIST - (CONV_K - 1) + j, lt), 0:cw_] * cw[j:j + 1, :]
    yc = gate_b * ca
    ms = jnp.mean(yc * yc, axis=-1, keepdims=True)
    yc_ref[0] = (yc * lax.rsqrt(ms + NORM_EPS) * cnw_ref[...]).astype(_BF16)

    gcw = gcw_ref[...]
    cq = pq * gcw[GDN_CONV_K - 1:GDN_CONV_K, :]
    for j in range(GDN_CONV_K - 1):
        cq = cq + ext_ref[pl.ds(HIST - (GDN_CONV_K - 1) + j, lt), cw_:] * gcw[j:j + 1, :]
    s = _silu(cq)
    for h in range(GDN_HEADS):
        lo, hi = h * GDN_HEAD_DIM, (h + 1) * GDN_HEAD_DIM
        qh = s[:, lo:hi]
        kh = s[:, GDN_WIDTH + lo:GDN_WIDTH + hi]
        qn = qh * lax.rsqrt(jnp.sum(qh * qh, axis=-1, keepdims=True) + 1e-6)
        kn = kh * lax.rsqrt(jnp.sum(kh * kh, axis=-1, keepdims=True) + 1e-6)
        q_ref[0, :, lo:hi] = (qn * (GDN_HEAD_DIM ** -0.5)).astype(_BF16)
        k_ref[0, :, lo:hi] = kn.astype(_BF16)
    v_ref[0] = s[:, 2 * GDN_WIDTH:].astype(_BF16)
    z_ref[0] = jnp.dot(xb, wz_ref[...], preferred_element_type=_F32).astype(_BF16)

    bdc = jnp.dot(xb, wbd_ref[...], preferred_element_type=_F32)
    prow = prow_ref[...]
    g_c = -jnp.exp(prow[0:1, :]) * _softplus(bdc + prow[1:2, :])
    lane = lax.broadcasted_iota(_I32, bdc.shape, 1)
    bgc_ref[0] = jnp.where(lane < GDN_HEADS, _sigmoid(bdc), g_c)
    bdr = _mm_nt(wbdt_ref[...], xb)
    pcol = pcol_ref[...]
    g_r = -jnp.exp(pcol[:, 0:1]) * _softplus(bdr + pcol[:, 1:2])
    row = lax.broadcasted_iota(_I32, bdr.shape, 0)
    bgr_ref[0] = jnp.where(row < GDN_HEADS, _sigmoid(bdr), g_r)

    tail = ext_ref[lt:lt + HIST, :]
    ext_ref[0:HIST, :] = tail
    tout_ref[0] = tail


def _premix(x, tails, wts, *, lt):
    bsz, seq, d = x.shape
    assert seq % lt == 0
    grid = (bsz, seq // lt)
    full = lambda a: pl.BlockSpec(a.shape, lambda b, j: (0,) * a.ndim)
    tok = lambda w: pl.BlockSpec((1, lt, w), lambda b, j: (b, j, 0))
    (wa, wq, wz, wbd, wbdt, cw, cnw, gcw, prow, pcol) = wts
    ext_w = CONV_WIDTH + 3 * GDN_WIDTH
    out_shape = (
        jax.ShapeDtypeStruct((bsz, seq, CONV_WIDTH), _BF16),
        jax.ShapeDtypeStruct((bsz, seq, GDN_WIDTH), _BF16),
        jax.ShapeDtypeStruct((bsz, seq, GDN_WIDTH), _BF16),
        jax.ShapeDtypeStruct((bsz, seq, GDN_WIDTH), _BF16),
        jax.ShapeDtypeStruct((bsz, seq, GDN_WIDTH), _BF16),
        jax.ShapeDtypeStruct((bsz, seq, 128), _F32),
        jax.ShapeDtypeStruct((bsz, 8, seq), _F32),
        jax.ShapeDtypeStruct((bsz, HIST, ext_w), _F32),
    )
    out_specs = (tok(CONV_WIDTH), tok(GDN_WIDTH), tok(GDN_WIDTH), tok(GDN_WIDTH), tok(GDN_WIDTH),
                 tok(128), pl.BlockSpec((1, 8, lt), lambda b, j: (b, 0, j)),
                 pl.BlockSpec((1, HIST, ext_w), lambda b, j: (b, 0, 0)))
    return pl.pallas_call(
        functools.partial(_premix_body, lt=lt),
        grid=grid,
        in_specs=[tok(d), full(tails)] + [full(w) for w in wts],
        out_specs=out_specs,
        out_shape=out_shape,
        scratch_shapes=[pltpu.VMEM((HIST + lt, ext_w), _F32)],
        compiler_params=_cparams("arbitrary", "arbitrary"),
        name="premix",
    )(x, tails, *wts)


def _cumsum_rows(x):
    row = lax.broadcasted_iota(_I32, x.shape, 0)
    s = 1
    while s < x.shape[0]:
        x = x + jnp.where(row >= s, pltpu.roll(x, s, 0), 0.0)
        s *= 2
    return x


def _cumsum_lanes_seg(x):
    lane = lax.broadcasted_iota(_I32, x.shape, 1) & (CHUNK - 1)
    s = 1
    while s < CHUNK:
        x = x + jnp.where(lane >= s, pltpu.roll(x, s, 1), 0.0)
        s *= 2
    return x


def _stack_heads(a):
    return jnp.concatenate([a[:, h * GDN_HEAD_DIM:(h + 1) * GDN_HEAD_DIM] for h in range(GDN_HEADS)], axis=0)


def _gdn_body(q_ref, k_ref, v_ref, z_ref, bgc_ref, grow_ref, s0_ref, gnw_ref,
              y_ref, sout_ref, s_ref, *, nc, nbb):
    @pl.when(pl.program_id(1) == 0)
    def _():
        for r in range(nbb):
            s_ref[r] = s0_ref[...]

    ri = lax.broadcasted_iota(_I32, (STACK, STACK), 0)
    ci = lax.broadcasted_iota(_I32, (STACK, STACK), 1)
    same64 = (ri >> 6) == (ci >> 6)
    same32 = (ri >> 5) == (ci >> 5)
    same16 = (ri >> 4) == (ci >> 4)
    low_incl = same64 & (ri >= ci)
    low_strict = same64 & (ri > ci)
    gnw = gnw_ref[...]

    def chunk_row(r, c):
        off = pl.multiple_of(c * CHUNK, CHUNK)
        q_all = _stack_heads(q_ref[r, pl.ds(off, CHUNK), :].astype(_F32))
        k_all = _stack_heads(k_ref[r, pl.ds(off, CHUNK), :].astype(_F32))
        v_all = _stack_heads(v_ref[r, pl.ds(off, CHUNK), :].astype(_F32))
        bgc = bgc_ref[r, pl.ds(off, CHUNK), :]
        gcs = _cumsum_rows(bgc)
        hd = (CHUNK, GDN_HEAD_DIM)
        beta_b = jnp.concatenate(
            [jnp.broadcast_to(bgc[:, h:h + 1], hd) for h in range(GDN_HEADS)], axis=0)
        gc_b = jnp.concatenate(
            [jnp.broadcast_to(gcs[:, GDN_HEADS + h:GDN_HEADS + h + 1], hd) for h in range(GDN_HEADS)], axis=0)
        gl = [gcs[CHUNK - 1:CHUNK, GDN_HEADS + h:GDN_HEADS + h + 1] for h in range(GDN_HEADS)]
        gl_b = jnp.concatenate([jnp.broadcast_to(g1, hd) for g1 in gl], axis=0)
        gcr = _cumsum_lanes_seg(jnp.broadcast_to(grow_ref[r, c], (8, STACK)))[0:1, :]

        diff = jnp.concatenate([gc_b, gc_b], axis=1) - gcr
        decay = jnp.exp(jnp.where(low_incl, diff, -1e30))
        kb = k_all * beta_b
        a1 = _mm_nt(jnp.concatenate([kb, q_all], axis=0), k_all)
        yield
        m = jnp.where(low_strict, a1[:STACK] * decay, 0.0)
        attn = a1[STACK:] * decay

        l16 = jnp.where(same16, m, 0.0)
        c1 = jnp.where(same32 & jnp.logical_not(same16), m, 0.0)
        c2 = jnp.where(same32, 0.0, m)
        p2 = _mm(l16, l16)
        yield
        p4 = _mm(p2, p2)
        t = _mm(l16, p2)
        yield
        na = p2 - l16 - t
        p8 = _mm(p4, p4)
        t = _mm(na, p4)
        yield
        nb = na + p4 + t
        t = _mm(nb, p8)
        yield
        ncm = nb + p8 + t
        t = _mm(c1, ncm)
        yield
        y1 = c1 + t
        t = _mm(ncm, y1)
        yield
        n1 = ncm - y1 - t
        t = _mm(c2, n1)
        yield
        y2 = c2 + t
        t = _mm(n1, y2)
        yield
        nt = n1 - y2 - t

        egc = jnp.exp(gc_b)
        rhs = jnp.concatenate([v_all * beta_b, kb * egc], axis=1)
        t = _mm(nt, rhs)
        yield
        uw = rhs + t
        u_all = uw[:, :GDN_HEAD_DIM]
        w_all = uw[:, GDN_HEAD_DIM:]
        qd = q_all * egc
        kd = k_all * jnp.exp(gl_b - gc_b)

        bs = []
        for h in range(GDN_HEADS):
            r0, r1 = h * CHUNK, (h + 1) * CHUNK
            bs.append(_mm(jnp.concatenate([w_all[r0:r1], qd[r0:r1]], axis=0), s_ref[r, h]))
        yield
        vn = [u_all[h * CHUNK:(h + 1) * CHUNK] - bs[h][:CHUNK] for h in range(GDN_HEADS)]
        vn_all = jnp.concatenate(vn, axis=0)
        t = _mm(attn, vn_all)
        ds = [_mm_tn(kd[h * CHUNK:(h + 1) * CHUNK], vn[h]) for h in range(GDN_HEADS)]
        yield
        o_all = jnp.concatenate([b[CHUNK:] for b in bs], axis=0) + t
        for h in range(GDN_HEADS):
            r0, r1 = h * CHUNK, (h + 1) * CHUNK
            s_ref[r, h] = s_ref[r, h] * jnp.exp(gl[h]) + ds[h]
            o = o_all[r0:r1]
            zz = z_ref[r, pl.ds(off, CHUNK), h * GDN_HEAD_DIM:(h + 1) * GDN_HEAD_DIM].astype(_F32)
            on = o * lax.rsqrt(jnp.mean(o * o, axis=-1, keepdims=True) + NORM_EPS) * gnw
            y_ref[r, pl.ds(off, CHUNK), h * GDN_HEAD_DIM:(h + 1) * GDN_HEAD_DIM] = (on * _silu(zz)).astype(_BF16)

    def chunk(c, carry):
        live = [chunk_row(r, c) for r in range(nbb)]
        while live:
            live = [g for g in live if next(g, live) is not live]
        return carry

    lax.fori_loop(0, nc, chunk, 0)
    sout_ref[...] = s_ref[...]


def _gdn(q, k, v, z, bgc, grow, s0, gnw, *, lg, nbb):
    bsz, seq, _ = q.shape
    assert seq % lg == 0 and lg % CHUNK == 0 and bsz % nbb == 0
    nc = lg // CHUNK
    tok = lambda w: pl.BlockSpec((nbb, lg, w), lambda b, j: (b, j, 0))
    full = lambda a: pl.BlockSpec(a.shape, lambda b, j: (0,) * a.ndim)
    st = (nbb, GDN_HEADS, GDN_HEAD_DIM, GDN_HEAD_DIM)
    return pl.pallas_call(
        functools.partial(_gdn_body, nc=nc, nbb=nbb),
        grid=(bsz // nbb, seq // lg),
        in_specs=[tok(GDN_WIDTH)] * 4 + [tok(128), pl.BlockSpec((nbb, nc, 1, STACK), lambda b, j: (b, j, 0, 0)),
                                           full(s0), full(gnw)],
        out_specs=(tok(GDN_WIDTH), pl.BlockSpec(st, lambda b, j: (b, 0, 0, 0))),
        out_shape=(jax.ShapeDtypeStruct((bsz, seq, GDN_WIDTH), _BF16),
                   jax.ShapeDtypeStruct((bsz, GDN_HEADS, GDN_HEAD_DIM, GDN_HEAD_DIM), _F32)),
        scratch_shapes=[pltpu.VMEM(st, _F32)],
        compiler_params=_cparams("arbitrary", "arbitrary"),
        name="gdn",
    )(q, k, v, z, bgc, grow, s0, gnw)


def _outproj_body(yc_ref, yg_ref, x_ref, wo_ref, g_ref, b_ref, h1_ref, h1p_ref):
    mix = (jnp.dot(yc_ref[...], wo_ref[0:CONV_WIDTH, :], preferred_element_type=_F32)
           + jnp.dot(yg_ref[...], wo_ref[CONV_WIDTH:, :], preferred_element_type=_F32))
    h1 = _layer_norm(DN_ALPHA * x_ref[...] + mix, g_ref[...], b_ref[...])
    h1_ref[...] = h1
    _store_rows(h1p_ref, _pack_halves(h1))


def _outproj(yc, yg, x2d, wo, g, b, *, tm):
    t = x2d.shape[0]
    assert t % tm == 0
    row = lambda w: pl.BlockSpec((tm, w), lambda i: (i, 0))
    full = lambda a: pl.BlockSpec(a.shape, lambda i: (0,) * a.ndim)
    return pl.pallas_call(
        _outproj_body,
        grid=(t // tm,),
        in_specs=[row(CONV_WIDTH), row(GDN_WIDTH), row(D_MODEL), full(wo), full(g), full(b)],
        out_specs=(row(D_MODEL), pl.BlockSpec((tm * QUAD, 128), lambda i: (i, 0))),
        out_shape=(jax.ShapeDtypeStruct((t, D_MODEL), _F32), jax.ShapeDtypeStruct((t * QUAD, 128), jnp.uint32)),
        compiler_params=_cparams("arbitrary"),
        name="outproj",
    )(yc, yg, x2d, wo, g, b)


def _router_body(h1_ref, wh_ref, wl_ref, br_ref, idx_ref, gate_ref, rank_ref, cnt_ref, carry_ref, *, tt):
    @pl.when(pl.program_id(0) == 0)
    def _():
        carry_ref[...] = jnp.zeros_like(carry_ref)

    x = h1_ref[...]
    xh = x.astype(_BF16)
    xl = (x - xh.astype(_F32)).astype(_BF16)
    wh = wh_ref[...]
    logits = _mm_nt(wh, xh) + _mm_nt(wh, xl) + _mm_nt(wl_ref[...], xh)
    scores = _sigmoid(logits)
    sel = scores + br_ref[...]
    ninf = -jnp.inf

    r32 = lax.broadcasted_iota(_I32, (E_PER_GROUP, tt), 0)
    gsc = []
    for g in range(N_GROUPS):
        xg = sel[g * E_PER_GROUP:(g + 1) * E_PER_GROUP]
        m1 = jnp.max(xg, axis=0, keepdims=True)
        i1 = jnp.min(jnp.where(xg == m1, r32, E_PER_GROUP), axis=0, keepdims=True)
        m2 = jnp.max(jnp.where(r32 == i1, ninf, xg), axis=0, keepdims=True)
        gsc.append(m1 + m2)
    work = jnp.concatenate(gsc, axis=0)
    r8 = lax.broadcasted_iota(_I32, (N_GROUPS, tt), 0)
    gkeep = jnp.zeros((N_GROUPS, tt), _F32)
    for _ in range(TOPK_GROUPS):
        m = jnp.max(work, axis=0, keepdims=True)
        gi = jnp.min(jnp.where(work == m, r8, N_GROUPS), axis=0, keepdims=True)
        pick = r8 == gi
        gkeep = jnp.where(pick, 1.0, gkeep)
        work = jnp.where(pick, ninf, work)
    selm = jnp.concatenate(
        [jnp.where(gkeep[g:g + 1] > 0.5, sel[g * E_PER_GROUP:(g + 1) * E_PER_GROUP], ninf)
         for g in range(N_GROUPS)], axis=0)

    re = lax.broadcasted_iota(_I32, (N_EXPERTS, tt), 0)
    msel = jnp.zeros((N_EXPERTS, tt), _F32)
    idxs, gates = [], []
    for _ in range(TOP_K):
        m = jnp.max(selm, axis=0, keepdims=True)
        ii = jnp.min(jnp.where(selm == m, re, N_EXPERTS), axis=0, keepdims=True)
        hit = re == ii
        idxs.append(ii)
        gates.append(jnp.sum(jnp.where(hit, scores, 0.0), axis=0, keepdims=True))
        selm = jnp.where(hit, ninf, selm)
        msel = jnp.where(hit, 1.0, msel)
    gate = jnp.concatenate(gates, axis=0)
    gate_ref[...] = gate / jnp.sum(gate, axis=0, keepdims=True) * ROUTED_SCALE
    idx_ref[...] = jnp.concatenate(idxs, axis=0)

    ta = lax.broadcasted_iota(_I32, (tt, tt), 0)
    tb = lax.broadcasted_iota(_I32, (tt, tt), 1)
    earlier = jnp.where(ta < tb, 1.0, 0.0)
    carry = carry_ref[...]
    rank_all = _mm(msel, earlier) + carry[:, 0:1]
    rank_ref[...] = jnp.concatenate(
        [jnp.sum(jnp.where(re == ii, rank_all, 0.0), axis=0, keepdims=True) for ii in idxs],
        axis=0).astype(_I32)
    carry = carry + jnp.sum(msel, axis=1, keepdims=True)
    carry_ref[...] = carry
    cnt_ref[...] = carry


def _router(h1, wh, wl, br, *, tt):
    t = h1.shape[0]
    assert t % tt == 0
    full = lambda a: pl.BlockSpec(a.shape, lambda i: (0,) * a.ndim)
    kt = pl.BlockSpec((TOP_K, tt), lambda i: (0, i))
    return pl.pallas_call(
        functools.partial(_router_body, tt=tt),
        grid=(t // tt,),
        in_specs=[pl.BlockSpec((tt, D_MODEL), lambda i: (i, 0)), full(wh), full(wl), full(br)],
        out_specs=(kt, kt, kt, pl.BlockSpec((N_EXPERTS, 128), lambda i: (0, 0))),
        out_shape=(jax.ShapeDtypeStruct((TOP_K, t), _I32), jax.ShapeDtypeStruct((TOP_K, t), _F32),
                   jax.ShapeDtypeStruct((TOP_K, t), _I32), jax.ShapeDtypeStruct((N_EXPERTS, 128), _F32)),
        scratch_shapes=[pltpu.VMEM((N_EXPERTS, 128), _F32)],
        compiler_params=_cparams("arbitrary"),
        name="router",
    )(h1, wh, wl, br)


def _position_body(idx_ref, rank_ref, pstart_ref, pos_ref, *, tt):
    re = lax.broadcasted_iota(_I32, (N_EXPERTS, tt), 0)
    ps = pstart_ref[...]
    idx = idx_ref[...]
    rows = [jnp.sum(jnp.where(re == idx[k:k + 1], ps, 0), axis=0, keepdims=True) for k in range(TOP_K)]
    pos_ref[0] = jnp.concatenate(rows, axis=0) + rank_ref[...]


def _position(idx, rank, pstart, *, tt):
    t = idx.shape[1]
    kt = pl.BlockSpec((TOP_K, tt), lambda i: (0, i))
    return pl.pallas_call(
        functools.partial(_position_body, tt=tt),
        grid=(t // tt,),
        in_specs=[kt, kt, pl.BlockSpec(pstart.shape, lambda i: (0, 0))],
        out_specs=pl.BlockSpec((1, TOP_K, tt), lambda i: (i, 0, 0)),
        out_shape=jax.ShapeDtypeStruct((t // tt, TOP_K, tt), _I32),
        compiler_params=_cparams("arbitrary"),
        name="position",
    )(idx, rank, pstart)


def _dispatch_body(cnt_ref, pst_ref, pcn_ref, pos_hbm, h1p_ref, xs_out, pos_smem, psem, sem, zbuf, zsem, *, tt):
    i = pl.program_id(0)
    cp = pltpu.make_async_copy(pos_hbm.at[i], pos_smem, psem)
    cp.start()

    @pl.when(i == 0)
    def _():
        zbuf[...] = jnp.zeros_like(zbuf)

        def pad_runs(e, act):
            pad = pcn_ref[e] - cnt_ref[e]
            base = pst_ref[e] + cnt_ref[e]
            for b in range(ROW_BLOCK.bit_length() - 1):
                n = 1 << b

                @pl.when(((pad >> b) & 1) == 1)
                def _():
                    off = base + (pad & (n - 1))
                    act(pltpu.make_async_copy(zbuf.at[pl.ds(0, QUAD * n)],
                                              xs_out.at[pl.ds(QUAD * off, QUAD * n)], zsem))

        def start_all(e, c):
            pad_runs(e, lambda d: d.start())
            return c

        def wait_all(e, c):
            pad_runs(e, lambda d: d.wait())
            return c

        lax.fori_loop(0, N_EXPERTS, start_all, 0)
        lax.fori_loop(0, N_EXPERTS, wait_all, 0)

    cp.wait()

    def row_copy(t, k):
        return pltpu.make_async_copy(h1p_ref.at[pl.ds(QUAD * t, QUAD)],
                                     xs_out.at[pl.ds(QUAD * pos_smem[k * tt + t], QUAD)], sem)

    def issue(t, c):
        for k in range(TOP_K):
            row_copy(t, k).start(priority=k % 2)
        return c

    lax.fori_loop(0, tt, issue, 0, unroll=ISSUE_UNROLL)

    for k in range(TOP_K):
        pltpu.make_async_copy(h1p_ref, xs_out.at[pl.ds(0, QUAD * tt)], sem).wait()


def _dispatch(counts, pstarts, pcounts, pos_tiles, h1p, n_rows, *, tt):
    t = h1p.shape[0] // QUAD
    grid_spec = pltpu.PrefetchScalarGridSpec(
        num_scalar_prefetch=3,
        grid=(t // tt,),
        in_specs=[pl.BlockSpec(memory_space=pl.ANY), pl.BlockSpec((tt * QUAD, 128), lambda i, *_: (i, 0))],
        out_specs=pl.BlockSpec(memory_space=pl.ANY),
        scratch_shapes=[pltpu.SMEM((TOP_K * tt,), _I32), pltpu.SemaphoreType.DMA, pltpu.SemaphoreType.DMA,
                        pltpu.VMEM((QUAD * ROW_BLOCK // 2, 128), jnp.uint32), pltpu.SemaphoreType.DMA],
    )
    return pl.pallas_call(
        functools.partial(_dispatch_body, tt=tt),
        grid_spec=grid_spec,
        out_shape=jax.ShapeDtypeStruct((n_rows * QUAD, 128), jnp.uint32),
        compiler_params=_cparams("arbitrary"),
        name="dispatch",
    )(counts, pstarts, pcounts, pos_tiles, h1p)


def _ffn_body(blk0_ref, nblk_ref, ntot_ref, xs_hbm, wg_ref, wu_ref, wd_ref, ys_hbm,
              xbuf, ybuf, sem_in, sem_out, wgu_bf, wd_bf):
    e = pl.program_id(0)
    nblk = nblk_ref[e]
    blk0 = blk0_ref[e]
    ntot = ntot_ref[0]

    blk_rows = ROW_BLOCK * QUAD

    def rows(g):
        return pl.ds(pl.multiple_of(g * blk_rows, blk_rows), blk_rows)

    def in_start(g, slot):
        pltpu.make_async_copy(xs_hbm.at[rows(g)], xbuf.at[slot], sem_in.at[slot]).start()

    def in_wait(slot):
        pltpu.make_async_copy(xs_hbm.at[rows(0)], xbuf.at[slot], sem_in.at[slot]).wait()

    def out_start(g, slot):
        pltpu.make_async_copy(ybuf.at[slot], ys_hbm.at[rows(g)], sem_out.at[slot]).start()

    def out_wait(slot):
        pltpu.make_async_copy(ybuf.at[slot], ys_hbm.at[rows(0)], sem_out.at[slot]).wait()

    @pl.when(e == 0)
    def _():
        for i in range(IN_AHEAD):
            @pl.when(i < ntot)
            def _():
                in_start(i, i)

    @pl.when(nblk > 0)
    def _():
        wgu_bf[:, 0:EXPERT_FF] = wg_ref[0].astype(_BF16)
        wgu_bf[:, EXPERT_FF:] = wu_ref[0].astype(_BF16)
        wd_bf[...] = wd_ref[0].astype(_BF16)

        def block(j, carry):
            g = blk0 + j
            slot = g & (RING - 1)
            in_wait(slot)

            @pl.when(g + IN_AHEAD < ntot)
            def _():
                in_start(g + IN_AHEAD, (g + IN_AHEAD) & (RING - 1))

            @pl.when(g >= RING)
            def _():
                out_wait(slot)

            lo, hi = _unpack_halves(_load_rows(xbuf.at[slot], ROW_BLOCK))
            gu = (jnp.dot(lo.astype(_BF16), wgu_bf[0:HALF, :], preferred_element_type=_F32)
                  + jnp.dot(hi.astype(_BF16), wgu_bf[HALF:, :], preferred_element_type=_F32))
            h = (_silu(gu[:, :EXPERT_FF]) * gu[:, EXPERT_FF:]).astype(_BF16)
            _store_rows(ybuf.at[slot], _pack_halves(jnp.dot(h, wd_bf[...], preferred_element_type=_F32)))
            out_start(g, slot)
            return carry

        lax.fori_loop(0, nblk, block, 0)

    @pl.when(e == N_EXPERTS - 1)
    def _():
        for i in range(RING):
            @pl.when(i < ntot)
            def _():
                out_wait((ntot - 1 - i) & (RING - 1))


def _ffn(blk0, nblk, ntot, xs, wg, wu, wd):
    grid_spec = pltpu.PrefetchScalarGridSpec(
        num_scalar_prefetch=3,
        grid=(N_EXPERTS,),
        in_specs=[pl.BlockSpec(memory_space=pl.ANY),
                  pl.BlockSpec((1, D_MODEL, EXPERT_FF), lambda e, *_: (e, 0, 0)),
                  pl.BlockSpec((1, D_MODEL, EXPERT_FF), lambda e, *_: (e, 0, 0)),
                  pl.BlockSpec((1, EXPERT_FF, D_MODEL), lambda e, *_: (e, 0, 0))],
        out_specs=pl.BlockSpec(memory_space=pl.ANY),
        scratch_shapes=[pltpu.VMEM((RING, ROW_BLOCK * QUAD, 128), jnp.uint32),
                        pltpu.VMEM((RING, ROW_BLOCK * QUAD, 128), jnp.uint32),
                        pltpu.SemaphoreType.DMA((RING,)), pltpu.SemaphoreType.DMA((RING,)),
                        pltpu.VMEM((D_MODEL, 2 * EXPERT_FF), _BF16), pltpu.VMEM((EXPERT_FF, D_MODEL), _BF16)],
    )
    return pl.pallas_call(
        _ffn_body,
        grid_spec=grid_spec,
        out_shape=jax.ShapeDtypeStruct(xs.shape, jnp.uint32),
        compiler_params=_cparams("arbitrary"),
        name="ffn",
    )(blk0, nblk, ntot, xs, wg, wu, wd)


def _combine_body(pos_hbm, gate_ref, h1_ref, ys_hbm, wsg_ref, wsu_ref, wsd_ref, g_ref, b_ref,
                  out_ref, pos_smem, psem, ybuf, sem, *, tt):
    i = pl.program_id(0)
    cp = pltpu.make_async_copy(pos_hbm.at[i], pos_smem, psem)
    cp.start()
    cp.wait()

    def row_copy(t, k):
        return pltpu.make_async_copy(ys_hbm.at[pl.ds(QUAD * pos_smem[k * tt + t], QUAD)],
                                     ybuf.at[k, pl.ds(QUAD * t, QUAD)], sem)

    def issue(t, c):
        for k in range(TOP_K):
            row_copy(t, k).start(priority=k % 2)
        return c

    lax.fori_loop(0, tt, issue, 0, unroll=ISSUE_UNROLL)

    x = h1_ref[...]
    xb = x.astype(_BF16)
    shared = _mm(_silu(_mm(xb, wsg_ref[...])) * _mm(xb, wsu_ref[...]), wsd_ref[...])

    for k in range(TOP_K):
        pltpu.make_async_copy(ys_hbm.at[pl.ds(0, QUAD * tt)], ybuf.at[k], sem).wait()

    gcol = gate_ref[...].T
    acc_lo = jnp.zeros((tt, HALF), _F32)
    acc_hi = jnp.zeros((tt, HALF), _F32)
    for k in range(TOP_K):
        lo, hi = _unpack_halves(_load_rows(ybuf.at[k], tt))
        acc_lo = acc_lo + gcol[:, k:k + 1] * lo
        acc_hi = acc_hi + gcol[:, k:k + 1] * hi
    routed = jnp.concatenate([acc_lo, acc_hi], axis=1)
    out_ref[...] = _layer_norm(DN_ALPHA * x + (routed + shared), g_ref[...], b_ref[...])


def _combine(pos_tiles, gate, h1, ys, wsg, wsu, wsd, g, b, *, tt):
    t = h1.shape[0]
    full = lambda a: pl.BlockSpec(a.shape, lambda i: (0,) * a.ndim)
    return pl.pallas_call(
        functools.partial(_combine_body, tt=tt),
        grid=(t // tt,),
        in_specs=[pl.BlockSpec(memory_space=pl.ANY), pl.BlockSpec((TOP_K, tt), lambda i: (0, i)),
                  pl.BlockSpec((tt, D_MODEL), lambda i: (i, 0)), pl.BlockSpec(memory_space=pl.ANY),
                  full(wsg), full(wsu), full(wsd), full(g), full(b)],
        out_specs=pl.BlockSpec((tt, D_MODEL), lambda i: (i, 0)),
        out_shape=jax.ShapeDtypeStruct((t, D_MODEL), _F32),
        scratch_shapes=[pltpu.SMEM((TOP_K * tt,), _I32), pltpu.SemaphoreType.DMA,
                        pltpu.VMEM((TOP_K, tt * QUAD, 128), jnp.uint32), pltpu.SemaphoreType.DMA],
        compiler_params=_cparams("arbitrary"),
        name="combine",
    )(pos_tiles, gate, h1, ys, wsg, wsu, wsd, g, b)


def _pick(n, pref):
    t = min(n, pref)
    while n % t:
        t -= CHUNK
    return t


def _mixer(x, tails, s0, wts, gnw, *, lt, lg, nbb):
    yc, q, k, v, z, bgc, bgr, tails_out = _premix(x, tails, wts, lt=lt)
    bsz, seq, _ = x.shape
    nch = seq // CHUNK
    grow = bgr[:, GDN_HEADS:2 * GDN_HEADS, :].reshape(bsz, GDN_HEADS, nch, CHUNK)
    grow = grow.transpose(0, 2, 1, 3).reshape(bsz, nch, 1, STACK)
    yg, s_out = _gdn(q, k, v, z, bgc, grow, s0, gnw, lg=lg, nbb=nbb)
    return yc, yg, tails_out, s_out


def kernel(x, meta_tokens, w_in, conv_w, conv_norm_w, gdn_conv_w, a_log, dt_bias, gdn_norm_w, w_out,
           ln1_g, ln1_b, w_router, b_router, w_gate, w_up, w_down, ws_gate, ws_up, ws_down, ln2_g, ln2_b):
    assert w_in.shape[0] == 1, "single-layer stack"
    bsz, seq, d = x.shape
    assert d == D_MODEL and seq % CHUNK == 0
    c, gw = CONV_WIDTH, GDN_WIDTH
    win = w_in[0].astype(_BF16)
    wbd = win[:, 3 * c + 4 * gw:]
    zpad = jnp.zeros((128 - 2 * GDN_HEADS,), _F32)
    zpad4 = jnp.zeros((GDN_HEADS,), _F32)
    prow = jnp.zeros((8, 128), _F32)
    prow = prow.at[0].set(jnp.concatenate([zpad4, a_log[0], zpad]))
    prow = prow.at[1].set(jnp.concatenate([zpad4, dt_bias[0], zpad]))
    wts = (win[:, :3 * c], win[:, 3 * c:3 * c + 3 * gw], win[:, 3 * c + 3 * gw:3 * c + 4 * gw],
           jnp.pad(wbd, ((0, 0), (0, 128 - 2 * GDN_HEADS))), wbd.T,
           conv_w[0], conv_norm_w, gdn_conv_w[0], prow, prow.T[:8])
    gnw = gdn_norm_w

    meta = jnp.concatenate([jnp.zeros((CHUNK - N_META, d), x.dtype), meta_tokens.astype(x.dtype)])[None]
    tails0 = jnp.zeros((HIST, c + 3 * gw), _F32)
    s00 = jnp.zeros((GDN_HEADS, GDN_HEAD_DIM, GDN_HEAD_DIM), _F32)
    _, _, tails_m, s_m = _mixer(meta, tails0, s00, wts, gnw, lt=CHUNK, lg=CHUNK, nbb=1)

    yc, yg, _, _ = _mixer(x, tails_m[0], s_m[0], wts, gnw, lt=_pick(seq, 512), lg=_pick(seq, 512),
                          nbb=GDN_ROWS if bsz % GDN_ROWS == 0 else 1)

    t = bsz * seq
    tm = _pick(t, 512)
    h1, h1p = _outproj(yc.reshape(t, c), yg.reshape(t, gw), x.reshape(t, d), w_out[0].astype(_BF16),
                       ln1_g, ln1_b, tm=tm)

    tt = _pick(t, 256)
    wr_t = w_router[0].T
    wr_hi = wr_t.astype(_BF16)
    wr_lo = (wr_t - wr_hi.astype(_F32)).astype(_BF16)
    idx, gate, rank, cnt = _router(h1, wr_hi, wr_lo, b_router[0][:, None], tt=tt)

    counts = cnt[:, 0].astype(_I32)
    pcounts = (counts + ROW_BLOCK - 1) // ROW_BLOCK * ROW_BLOCK
    pends = jnp.cumsum(pcounts)
    pstarts = pends - pcounts
    nb = t * TOP_K // ROW_BLOCK + N_EXPERTS

    pos = _position(idx, rank, pstarts[:, None].astype(_I32), tt=tt).reshape(t // tt, TOP_K * tt)
    xs = _dispatch(counts, pstarts.astype(_I32), pcounts.astype(_I32), pos, h1p, nb * ROW_BLOCK, tt=tt)
    ys = _ffn((pstarts // ROW_BLOCK).astype(_I32), (pcounts // ROW_BLOCK).astype(_I32),
              (pends[-1:] // ROW_BLOCK).astype(_I32), xs, w_gate[0], w_up[0], w_down[0])
    out = _combine(pos, gate, h1, ys, ws_gate[0].astype(_BF16), ws_up[0].astype(_BF16),
                   ws_down[0].astype(_BF16), ln2_g, ln2_b, tt=tt)
    return out.reshape(bsz, seq, d)
```

```python
import functools

import jax
import jax.numpy as jnp
from jax import lax
from jax.experimental import pallas as pl
from jax.experimental.pallas import tpu as pltpu
from jax.experimental.pallas import tpu_sc as plsc

_F32 = jnp.float32
_BF16 = jnp.bfloat16
_I32 = jnp.int32

D_MODEL = 1024
N_META = 16
CONV_WIDTH = 512
CONV_K = 3
GDN_HEADS = 4
GDN_HEAD_DIM = 128
GDN_WIDTH = GDN_HEADS * GDN_HEAD_DIM
GDN_CONV_K = 4
CHUNK = 64
N_EXPERTS = 256
TOP_K = 8
N_GROUPS = 8
TOPK_GROUPS = 4
E_PER_GROUP = N_EXPERTS // N_GROUPS
EXPERT_FF = 256
ROUTED_SCALE = 2.5
ROW_BLOCK = 256
DN_ALPHA = 2.0 ** 0.25
NORM_EPS = 1e-5
HALF = D_MODEL // 2
QUAD = HALF // 128
STACK = GDN_HEADS * CHUNK
HIST = 8
GDN_ROWS = 4
ISSUE_UNROLL = 8
SC_CORES = 2
SC_SUBCORES = 16
SC_WORKERS = SC_CORES * SC_SUBCORES
SC_CHUNK = 64
RING = 4
IN_AHEAD = RING - 1

V7X_VMEM_BYTES = 64 * 1024 * 1024
VMEM_LIMIT = V7X_VMEM_BYTES - 8 * 1024 * 1024


def _cparams(*sem):
    return pltpu.CompilerParams(dimension_semantics=sem, vmem_limit_bytes=VMEM_LIMIT)


def _mm(a, b):
    return jnp.dot(a.astype(_BF16), b.astype(_BF16), preferred_element_type=_F32)


def _mm_nt(a, b):
    return lax.dot_general(a.astype(_BF16), b.astype(_BF16), (((1,), (1,)), ((), ())),
                           preferred_element_type=_F32)


def _mm_tn(a, b):
    return lax.dot_general(a.astype(_BF16), b.astype(_BF16), (((0,), (0,)), ((), ())),
                           preferred_element_type=_F32)


def _sigmoid(x):
    return 1.0 / (1.0 + jnp.exp(-x))


def _silu(x):
    return x * _sigmoid(x)


def _softplus(x):
    return jnp.maximum(x, 0.0) + jnp.log1p(jnp.exp(-jnp.abs(x)))


def _pack_halves(y):
    return pltpu.pack_elementwise([y[:, :HALF], y[:, HALF:]], packed_dtype=_BF16)


def _store_rows(ref, packed):
    r = packed.shape[0]
    for c in range(QUAD):
        ref[pl.ds(c, r, stride=QUAD), :] = packed[:, c * 128:(c + 1) * 128]


def _load_rows(ref, r):
    return jnp.concatenate([ref[pl.ds(c, r, stride=QUAD), :] for c in range(QUAD)], axis=1)


def _unpack_halves(p):
    lo = pltpu.unpack_elementwise(p, index=0, packed_dtype=_BF16, unpacked_dtype=_F32)
    hi = pltpu.unpack_elementwise(p, index=1, packed_dtype=_BF16, unpacked_dtype=_F32)
    return lo, hi


def _layer_norm(h, g, b):
    mu = jnp.mean(h, axis=-1, keepdims=True)
    d = h - mu
    var = jnp.mean(d * d, axis=-1, keepdims=True)
    return d * lax.rsqrt(var + NORM_EPS) * g + b


def _premix_body(x_ref, tails_ref, wa_ref, wq_ref, wz_ref, wbd_ref, wbdt_ref, cw_ref, cnw_ref,
                 gcw_ref, prow_ref, pcol_ref,
                 yc_ref, q_ref, k_ref, v_ref, z_ref, bgc_ref, bgr_ref, tout_ref, ext_ref, *, lt):
    cw_ = CONV_WIDTH

    @pl.when(pl.program_id(1) == 0)
    def _():
        ext_ref[0:HIST, :] = tails_ref[...]

    xb = x_ref[0].astype(_BF16)
    pa = jnp.dot(xb, wa_ref[...], preferred_element_type=_F32)
    gate_b = pa[:, 0:cw_]
    u = pa[:, cw_:2 * cw_] * pa[:, 2 * cw_:3 * cw_]
    ext_ref[HIST:HIST + lt, 0:cw_] = u
    pq = jnp.dot(xb, wq_ref[...], preferred_element_type=_F32)
    ext_ref[HIST:HIST + lt, cw_:] = pq

    cw = cw_ref[...]
    ca = u * cw[CONV_K - 1:CONV_K, :]
    for j in range(CONV_K - 1):
        ca = ca + ext_ref[pl.ds(HIST - (CONV_K - 1) + j, lt), 0:cw_] * cw[j:j + 1, :]
    yc = gate_b * ca
    ms = jnp.mean(yc * yc, axis=-1, keepdims=True)
    yc_ref[0] = (yc * lax.rsqrt(ms + NORM_EPS) * cnw_ref[...]).astype(_BF16)

    gcw = gcw_ref[...]
    cq = pq * gcw[GDN_CONV_K - 1:GDN_CONV_K, :]
    for j in range(GDN_CONV_K - 1):
        cq = cq + ext_ref[pl.ds(HIST - (GDN_CONV_K - 1) + j, lt), cw_:] * gcw[j:j + 1, :]
    s = _silu(cq)
    for h in range(GDN_HEADS):
        lo, hi = h * GDN_HEAD_DIM, (h + 1) * GDN_HEAD_DIM
        qh = s[:, lo:hi]
        kh = s[:, GDN_WIDTH + lo:GDN_WIDTH + hi]
        qn = qh * lax.rsqrt(jnp.sum(qh * qh, axis=-1, keepdims=True) + 1e-6)
        kn = kh * lax.rsqrt(jnp.sum(kh * kh, axis=-1, keepdims=True) + 1e-6)
        q_ref[0, :, lo:hi] = (qn * (GDN_HEAD_DIM ** -0.5)).astype(_BF16)
        k_ref[0, :, lo:hi] = kn.astype(_BF16)
    v_ref[0] = s[:, 2 * GDN_WIDTH:].astype(_BF16)
    z_ref[0] = jnp.dot(xb, wz_ref[...], preferred_element_type=_F32).astype(_BF16)

    bdc = jnp.dot(xb, wbd_ref[...], preferred_element_type=_F32)
    prow = prow_ref[...]
    g_c = -jnp.exp(prow[0:1, :]) * _softplus(bdc + prow[1:2, :])
    lane = lax.broadcasted_iota(_I32, bdc.shape, 1)
    bgc_ref[0] = jnp.where(lane < GDN_HEADS, _sigmoid(bdc), g_c)
    bdr = _mm_nt(wbdt_ref[...], xb)
    pcol = pcol_ref[...]
    g_r = -jnp.exp(pcol[:, 0:1]) * _softplus(bdr + pcol[:, 1:2])
    row = lax.broadcasted_iota(_I32, bdr.shape, 0)
    bgr_ref[0] = jnp.where(row < GDN_HEADS, _sigmoid(bdr), g_r)

    tail = ext_ref[lt:lt + HIST, :]
    ext_ref[0:HIST, :] = tail
    tout_ref[0] = tail


def _premix(x, tails, wts, *, lt):
    bsz, seq, d = x.shape
    assert seq % lt == 0
    grid = (bsz, seq // lt)
    full = lambda a: pl.BlockSpec(a.shape, lambda b, j: (0,) * a.ndim)
    tok = lambda w: pl.BlockSpec((1, lt, w), lambda b, j: (b, j, 0))
    (wa, wq, wz, wbd, wbdt, cw, cnw, gcw, prow, pcol) = wts
    ext_w = CONV_WIDTH + 3 * GDN_WIDTH
    out_shape = (
        jax.ShapeDtypeStruct((bsz, seq, CONV_WIDTH), _BF16),
        jax.ShapeDtypeStruct((bsz, seq, GDN_WIDTH), _BF16),
        jax.ShapeDtypeStruct((bsz, seq, GDN_WIDTH), _BF16),
        jax.ShapeDtypeStruct((bsz, seq, GDN_WIDTH), _BF16),
        jax.ShapeDtypeStruct((bsz, seq, GDN_WIDTH), _BF16),
        jax.ShapeDtypeStruct((bsz, seq, 128), _F32),
        jax.ShapeDtypeStruct((bsz, 8, seq), _F32),
        jax.ShapeDtypeStruct((bsz, HIST, ext_w), _F32),
    )
    out_specs = (tok(CONV_WIDTH), tok(GDN_WIDTH), tok(GDN_WIDTH), tok(GDN_WIDTH), tok(GDN_WIDTH),
                 tok(128), pl.BlockSpec((1, 8, lt), lambda b, j: (b, 0, j)),
                 pl.BlockSpec((1, HIST, ext_w), lambda b, j: (b, 0, 0)))
    return pl.pallas_call(
        functools.partial(_premix_body, lt=lt),
        grid=grid,
        in_specs=[tok(d), full(tails)] + [full(w) for w in wts],
        out_specs=out_specs,
        out_shape=out_shape,
        scratch_shapes=[pltpu.VMEM((HIST + lt, ext_w), _F32)],
        compiler_params=_cparams("arbitrary", "arbitrary"),
        name="premix",
    )(x, tails, *wts)


def _cumsum_rows(x):
    row = lax.broadcasted_iota(_I32, x.shape, 0)
    s = 1
    while s < x.shape[0]:
        x = x + jnp.where(row >= s, pltpu.roll(x, s, 0), 0.0)
        s *= 2
    return x


def _cumsum_lanes_seg(x):
    lane = lax.broadcasted_iota(_I32, x.shape, 1) & (CHUNK - 1)
    s = 1
    while s < CHUNK:
        x = x + jnp.where(lane >= s, pltpu.roll(x, s, 1), 0.0)
        s *= 2
    return x


def _stack_heads(a):
    return jnp.concatenate([a[:, h * GDN_HEAD_DIM:(h + 1) * GDN_HEAD_DIM] for h in range(GDN_HEADS)], axis=0)


def _gdn_body(q_ref, k_ref, v_ref, z_ref, bgc_ref, grow_ref, s0_ref, gnw_ref,
              y_ref, sout_ref, s_ref, *, nc, nbb):
    @pl.when(pl.program_id(1) == 0)
    def _():
        for r in range(nbb):
            s_ref[r] = s0_ref[...]

    ri = lax.broadcasted_iota(_I32, (STACK, STACK), 0)
    ci = lax.broadcasted_iota(_I32, (STACK, STACK), 1)
    same64 = (ri >> 6) == (ci >> 6)
    same32 = (ri >> 5) == (ci >> 5)
    same16 = (ri >> 4) == (ci >> 4)
    low_incl = same64 & (ri >= ci)
    low_strict = same64 & (ri > ci)
    gnw = gnw_ref[...]

    def chunk_row(r, c):
        off = pl.multiple_of(c * CHUNK, CHUNK)
        q_all = _stack_heads(q_ref[r, pl.ds(off, CHUNK), :].astype(_F32))
        k_all = _stack_heads(k_ref[r, pl.ds(off, CHUNK), :].astype(_F32))
        v_all = _stack_heads(v_ref[r, pl.ds(off, CHUNK), :].astype(_F32))
        bgc = bgc_ref[r, pl.ds(off, CHUNK), :]
        gcs = _cumsum_rows(bgc)
        hd = (CHUNK, GDN_HEAD_DIM)
        beta_b = jnp.concatenate(
            [jnp.broadcast_to(bgc[:, h:h + 1], hd) for h in range(GDN_HEADS)], axis=0)
        gc_b = jnp.concatenate(
            [jnp.broadcast_to(gcs[:, GDN_HEADS + h:GDN_HEADS + h + 1], hd) for h in range(GDN_HEADS)], axis=0)
        gl = [gcs[CHUNK - 1:CHUNK, GDN_HEADS + h:GDN_HEADS + h + 1] for h in range(GDN_HEADS)]
        gl_b = jnp.concatenate([jnp.broadcast_to(g1, hd) for g1 in gl], axis=0)
        gcr = _cumsum_lanes_seg(jnp.broadcast_to(grow_ref[r, c], (8, STACK)))[0:1, :]

        diff = jnp.concatenate([gc_b, gc_b], axis=1) - gcr
        decay = jnp.exp(jnp.where(low_incl, diff, -1e30))
        kb = k_all * beta_b
        a1 = _mm_nt(jnp.concatenate([kb, q_all], axis=0), k_all)
        yield
        m = jnp.where(low_strict, a1[:STACK] * decay, 0.0)
        attn = a1[STACK:] * decay

        l16 = jnp.where(same16, m, 0.0)
        c1 = jnp.where(same32 & jnp.logical_not(same16), m, 0.0)
        c2 = jnp.where(same32, 0.0, m)
        p2 = _mm(l16, l16)
        yield
        p4 = _mm(p2, p2)
        t = _mm(l16, p2)
        yield
        na = p2 - l16 - t
        p8 = _mm(p4, p4)
        t = _mm(na, p4)
        yield
        nb = na + p4 + t
        t = _mm(nb, p8)
        yield
        ncm = nb + p8 + t
        t = _mm(c1, ncm)
        yield
        y1 = c1 + t
        t = _mm(ncm, y1)
        yield
        n1 = ncm - y1 - t
        t = _mm(c2, n1)
        yield
        y2 = c2 + t
        t = _mm(n1, y2)
        yield
        nt = n1 - y2 - t

        egc = jnp.exp(gc_b)
        rhs = jnp.concatenate([v_all * beta_b, kb * egc], axis=1)
        t = _mm(nt, rhs)
        yield
        uw = rhs + t
        u_all = uw[:, :GDN_HEAD_DIM]
        w_all = uw[:, GDN_HEAD_DIM:]
        qd = q_all * egc
        kd = k_all * jnp.exp(gl_b - gc_b)

        bs = []
        for h in range(GDN_HEADS):
            r0, r1 = h * CHUNK, (h + 1) * CHUNK
            bs.append(_mm(jnp.concatenate([w_all[r0:r1], qd[r0:r1]], axis=0), s_ref[r, h]))
        yield
        vn = [u_all[h * CHUNK:(h + 1) * CHUNK] - bs[h][:CHUNK] for h in range(GDN_HEADS)]
        vn_all = jnp.concatenate(vn, axis=0)
        t = _mm(attn, vn_all)
        ds = [_mm_tn(kd[h * CHUNK:(h + 1) * CHUNK], vn[h]) for h in range(GDN_HEADS)]
        yield
        o_all = jnp.concatenate([b[CHUNK:] for b in bs], axis=0) + t
        for h in range(GDN_HEADS):
            r0, r1 = h * CHUNK, (h + 1) * CHUNK
            s_ref[r, h] = s_ref[r, h] * jnp.exp(gl[h]) + ds[h]
            o = o_all[r0:r1]
            zz = z_ref[r, pl.ds(off, CHUNK), h * GDN_HEAD_DIM:(h + 1) * GDN_HEAD_DIM].astype(_F32)
            on = o * lax.rsqrt(jnp.mean(o * o, axis=-1, keepdims=True) + NORM_EPS) * gnw
            y_ref[r, pl.ds(off, CHUNK), h * GDN_HEAD_DIM:(h + 1) * GDN_HEAD_DIM] = (on * _silu(zz)).astype(_BF16)

    def chunk(c, carry):
        live = [chunk_row(r, c) for r in range(nbb)]
        while live:
            live = [g for g in live if next(g, live) is not live]
        return carry

    lax.fori_loop(0, nc, chunk, 0)
    sout_ref[...] = s_ref[...]


def _gdn(q, k, v, z, bgc, grow, s0, gnw, *, lg, nbb):
    bsz, seq, _ = q.shape
    assert seq % lg == 0 and lg % CHUNK == 0 and bsz % nbb == 0
    nc = lg // CHUNK
    tok = lambda w: pl.BlockSpec((nbb, lg, w), lambda b, j: (b, j, 0))
    full = lambda a: pl.BlockSpec(a.shape, lambda b, j: (0,) * a.ndim)
    st = (nbb, GDN_HEADS, GDN_HEAD_DIM, GDN_HEAD_DIM)
    return pl.pallas_call(
        functools.partial(_gdn_body, nc=nc, nbb=nbb),
        grid=(bsz // nbb, seq // lg),
        in_specs=[tok(GDN_WIDTH)] * 4 + [tok(128), pl.BlockSpec((nbb, nc, 1, STACK), lambda b, j: (b, j, 0, 0)),
                                           full(s0), full(gnw)],
        out_specs=(tok(GDN_WIDTH), pl.BlockSpec(st, lambda b, j: (b, 0, 0, 0))),
        out_shape=(jax.ShapeDtypeStruct((bsz, seq, GDN_WIDTH), _BF16),
                   jax.ShapeDtypeStruct((bsz, GDN_HEADS, GDN_HEAD_DIM, GDN_HEAD_DIM), _F32)),
        scratch_shapes=[pltpu.VMEM(st, _F32)],
        compiler_params=_cparams("arbitrary", "arbitrary"),
        name="gdn",
    )(q, k, v, z, bgc, grow, s0, gnw)


def _outproj_body(yc_ref, yg_ref, x_ref, wo_ref, g_ref, b_ref, h1_ref, h1p_ref):
    mix = (jnp.dot(yc_ref[...], wo_ref[0:CONV_WIDTH, :], preferred_element_type=_F32)
           + jnp.dot(yg_ref[...], wo_ref[CONV_WIDTH:, :], preferred_element_type=_F32))
    h1 = _layer_norm(DN_ALPHA * x_ref[...] + mix, g_ref[...], b_ref[...])
    h1_ref[...] = h1
    _store_rows(h1p_ref, _pack_halves(h1))


def _outproj(yc, yg, x2d, wo, g, b, *, tm):
    t = x2d.shape[0]
    assert t % tm == 0
    row = lambda w: pl.BlockSpec((tm, w), lambda i: (i, 0))
    full = lambda a: pl.BlockSpec(a.shape, lambda i: (0,) * a.ndim)
    return pl.pallas_call(
        _outproj_body,
        grid=(t // tm,),
        in_specs=[row(CONV_WIDTH), row(GDN_WIDTH), row(D_MODEL), full(wo), full(g), full(b)],
        out_specs=(row(D_MODEL), pl.BlockSpec((tm * QUAD, 128), lambda i: (i, 0))),
        out_shape=(jax.ShapeDtypeStruct((t, D_MODEL), _F32), jax.ShapeDtypeStruct((t * QUAD, 128), jnp.uint32)),
        compiler_params=_cparams("arbitrary"),
        name="outproj",
    )(yc, yg, x2d, wo, g, b)


def _router_body(h1_ref, wh_ref, wl_ref, br_ref, idx_ref, gate_ref, rank_ref, cnt_ref, carry_ref, *, tt):
    @pl.when(pl.program_id(0) == 0)
    def _():
        carry_ref[...] = jnp.zeros_like(carry_ref)

    x = h1_ref[...]
    xh = x.astype(_BF16)
    xl = (x - xh.astype(_F32)).astype(_BF16)
    wh = wh_ref[...]
    logits = _mm_nt(wh, xh) + _mm_nt(wh, xl) + _mm_nt(wl_ref[...], xh)
    scores = _sigmoid(logits)
    sel = scores + br_ref[...]
    ninf = -jnp.inf

    r32 = lax.broadcasted_iota(_I32, (E_PER_GROUP, tt), 0)
    gsc = []
    for g in range(N_GROUPS):
        xg = sel[g * E_PER_GROUP:(g + 1) * E_PER_GROUP]
        m1 = jnp.max(xg, axis=0, keepdims=True)
        i1 = jnp.min(jnp.where(xg == m1, r32, E_PER_GROUP), axis=0, keepdims=True)
        m2 = jnp.max(jnp.where(r32 == i1, ninf, xg), axis=0, keepdims=True)
        gsc.append(m1 + m2)
    work = jnp.concatenate(gsc, axis=0)
    r8 = lax.broadcasted_iota(_I32, (N_GROUPS, tt), 0)
    gkeep = jnp.zeros((N_GROUPS, tt), _F32)
    for _ in range(TOPK_GROUPS):
        m = jnp.max(work, axis=0, keepdims=True)
        gi = jnp.min(jnp.where(work == m, r8, N_GROUPS), axis=0, keepdims=True)
        pick = r8 == gi
        gkeep = jnp.where(pick, 1.0, gkeep)
        work = jnp.where(pick, ninf, work)
    selm = jnp.concatenate(
        [jnp.where(gkeep[g:g + 1] > 0.5, sel[g * E_PER_GROUP:(g + 1) * E_PER_GROUP], ninf)
         for g in range(N_GROUPS)], axis=0)

    re = lax.broadcasted_iota(_I32, (N_EXPERTS, tt), 0)
    msel = jnp.zeros((N_EXPERTS, tt), _F32)
    idxs, gates = [], []
    for _ in range(TOP_K):
        m = jnp.max(selm, axis=0, keepdims=True)
        ii = jnp.min(jnp.where(selm == m, re, N_EXPERTS), axis=0, keepdims=True)
        hit = re == ii
        idxs.append(ii)
        gates.append(jnp.sum(jnp.where(hit, scores, 0.0), axis=0, keepdims=True))
        selm = jnp.where(hit, ninf, selm)
        msel = jnp.where(hit, 1.0, msel)
    gate = jnp.concatenate(gates, axis=0)
    gate_ref[...] = gate / jnp.sum(gate, axis=0, keepdims=True) * ROUTED_SCALE
    idx_ref[...] = jnp.concatenate(idxs, axis=0)

    ta = lax.broadcasted_iota(_I32, (tt, tt), 0)
    tb = lax.broadcasted_iota(_I32, (tt, tt), 1)
    earlier = jnp.where(ta < tb, 1.0, 0.0)
    carry = carry_ref[...]
    rank_all = _mm(msel, earlier) + carry[:, 0:1]
    rank_ref[...] = jnp.concatenate(
        [jnp.sum(jnp.where(re == ii, rank_all, 0.0), axis=0, keepdims=True) for ii in idxs],
        axis=0).astype(_I32)
    carry = carry + jnp.sum(msel, axis=1, keepdims=True)
    carry_ref[...] = carry
    cnt_ref[...] = carry


def _router(h1, wh, wl, br, *, tt):
    t = h1.shape[0]
    assert t % tt == 0
    full = lambda a: pl.BlockSpec(a.shape, lambda i: (0,) * a.ndim)
    kt = pl.BlockSpec((TOP_K, tt), lambda i: (0, i))
    return pl.pallas_call(
        functools.partial(_router_body, tt=tt),
        grid=(t // tt,),
        in_specs=[pl.BlockSpec((tt, D_MODEL), lambda i: (i, 0)), full(wh), full(wl), full(br)],
        out_specs=(kt, kt, kt, pl.BlockSpec((N_EXPERTS, 128), lambda i: (0, 0))),
        out_shape=(jax.ShapeDtypeStruct((TOP_K, t), _I32), jax.ShapeDtypeStruct((TOP_K, t), _F32),
                   jax.ShapeDtypeStruct((TOP_K, t), _I32), jax.ShapeDtypeStruct((N_EXPERTS, 128), _F32)),
        scratch_shapes=[pltpu.VMEM((N_EXPERTS, 128), _F32)],
        compiler_params=_cparams("arbitrary"),
        name="router",
    )(h1, wh, wl, br)


def _position_body(idx_ref, rank_ref, pstart_ref, pos_ref, *, tt):
    re = lax.broadcasted_iota(_I32, (N_EXPERTS, tt), 0)
    ps = pstart_ref[...]
    idx = idx_ref[...]
    rows = [jnp.sum(jnp.where(re == idx[k:k + 1], ps, 0), axis=0, keepdims=True) for k in range(TOP_K)]
    pos_ref[0] = jnp.concatenate(rows, axis=0) + rank_ref[...]


def _position(idx, rank, pstart, *, tt):
    t = idx.shape[1]
    kt = pl.BlockSpec((TOP_K, tt), lambda i: (0, i))
    return pl.pallas_call(
        functools.partial(_position_body, tt=tt),
        grid=(t // tt,),
        in_specs=[kt, kt, pl.BlockSpec(pstart.shape, lambda i: (0, 0))],
        out_specs=pl.BlockSpec((1, TOP_K, tt), lambda i: (i, 0, 0)),
        out_shape=jax.ShapeDtypeStruct((t // tt, TOP_K, tt), _I32),
        compiler_params=_cparams("arbitrary"),
        name="position",
    )(idx, rank, pstart)


def _dispatch_body(cnt_ref, pst_ref, pcn_ref, pos_hbm, h1p_ref, xs_out, pos_smem, psem, sem, zbuf, zsem, *, tt):
    i = pl.program_id(0)
    cp = pltpu.make_async_copy(pos_hbm.at[i], pos_smem, psem)
    cp.start()

    @pl.when(i == 0)
    def _():
        zbuf[...] = jnp.zeros_like(zbuf)

        def pad_runs(e, act):
            pad = pcn_ref[e] - cnt_ref[e]
            base = pst_ref[e] + cnt_ref[e]
            for b in range(ROW_BLOCK.bit_length() - 1):
                n = 1 << b

                @pl.when(((pad >> b) & 1) == 1)
                def _():
                    off = base + (pad & (n - 1))
                    act(pltpu.make_async_copy(zbuf.at[pl.ds(0, QUAD * n)],
                                              xs_out.at[pl.ds(QUAD * off, QUAD * n)], zsem))

        def start_all(e, c):
            pad_runs(e, lambda d: d.start())
            return c

        def wait_all(e, c):
            pad_runs(e, lambda d: d.wait())
            return c

        lax.fori_loop(0, N_EXPERTS, start_all, 0)
        lax.fori_loop(0, N_EXPERTS, wait_all, 0)

    cp.wait()

    def row_copy(t, k):
        return pltpu.make_async_copy(h1p_ref.at[pl.ds(QUAD * t, QUAD)],
                                     xs_out.at[pl.ds(QUAD * pos_smem[k * tt + t], QUAD)], sem)

    def issue(t, c):
        for k in range(TOP_K):
            row_copy(t, k).start(priority=k % 2)
        return c

    lax.fori_loop(0, tt, issue, 0, unroll=ISSUE_UNROLL)

    for k in range(TOP_K):
        pltpu.make_async_copy(h1p_ref, xs_out.at[pl.ds(0, QUAD * tt)], sem).wait()


def _dispatch(counts, pstarts, pcounts, pos_tiles, h1p, n_rows, *, tt):
    t = h1p.shape[0] // QUAD
    grid_spec = pltpu.PrefetchScalarGridSpec(
        num_scalar_prefetch=3,
        grid=(t // tt,),
        in_specs=[pl.BlockSpec(memory_space=pl.ANY), pl.BlockSpec((tt * QUAD, 128), lambda i, *_: (i, 0))],
        out_specs=pl.BlockSpec(memory_space=pl.ANY),
        scratch_shapes=[pltpu.SMEM((TOP_K * tt,), _I32), pltpu.SemaphoreType.DMA, pltpu.SemaphoreType.DMA,
                        pltpu.VMEM((QUAD * ROW_BLOCK // 2, 128), jnp.uint32), pltpu.SemaphoreType.DMA],
    )
    return pl.pallas_call(
        functools.partial(_dispatch_body, tt=tt),
        grid_spec=grid_spec,
        out_shape=jax.ShapeDtypeStruct((n_rows * QUAD, 128), jnp.uint32),
        compiler_params=_cparams("arbitrary"),
        name="dispatch",
    )(counts, pstarts, pcounts, pos_tiles, h1p)


def _ffn_body(blk0_ref, nblk_ref, ntot_ref, xs_hbm, wg_ref, wu_ref, wd_ref, ys_hbm,
              xbuf, ybuf, sem_in, sem_out, wgu_bf, wd_bf):
    e = pl.program_id(0)
    nblk = nblk_ref[e]
    blk0 = blk0_ref[e]
    ntot = ntot_ref[0]

    blk_rows = ROW_BLOCK * QUAD

    def rows(g):
        return pl.ds(pl.multiple_of(g * blk_rows, blk_rows), blk_rows)

    def in_start(g, slot):
        pltpu.make_async_copy(xs_hbm.at[rows(g)], xbuf.at[slot], sem_in.at[slot]).start()

    def in_wait(slot):
        pltpu.make_async_copy(xs_hbm.at[rows(0)], xbuf.at[slot], sem_in.at[slot]).wait()

    def out_start(g, slot):
        pltpu.make_async_copy(ybuf.at[slot], ys_hbm.at[rows(g)], sem_out.at[slot]).start()

    def out_wait(slot):
        pltpu.make_async_copy(ybuf.at[slot], ys_hbm.at[rows(0)], sem_out.at[slot]).wait()

    @pl.when(e == 0)
    def _():
        for i in range(IN_AHEAD):
            @pl.when(i < ntot)
            def _():
                in_start(i, i)

    @pl.when(nblk > 0)
    def _():
        wgu_bf[:, 0:EXPERT_FF] = wg_ref[0].astype(_BF16)
        wgu_bf[:, EXPERT_FF:] = wu_ref[0].astype(_BF16)
        wd_bf[...] = wd_ref[0].astype(_BF16)

        def block(j, carry):
            g = blk0 + j
            slot = g & (RING - 1)
            in_wait(slot)

            @pl.when(g + IN_AHEAD < ntot)
            def _():
                in_start(g + IN_AHEAD, (g + IN_AHEAD) & (RING - 1))

            @pl.when(g >= RING)
            def _():
                out_wait(slot)

            lo, hi = _unpack_halves(_load_rows(xbuf.at[slot], ROW_BLOCK))
            gu = (jnp.dot(lo.astype(_BF16), wgu_bf[0:HALF, :], preferred_element_type=_F32)
                  + jnp.dot(hi.astype(_BF16), wgu_bf[HALF:, :], preferred_element_type=_F32))
            h = (_silu(gu[:, :EXPERT_FF]) * gu[:, EXPERT_FF:]).astype(_BF16)
            _store_rows(ybuf.at[slot], _pack_halves(jnp.dot(h, wd_bf[...], preferred_element_type=_F32)))
            out_start(g, slot)
            return carry

        lax.fori_loop(0, nblk, block, 0)

    @pl.when(e == N_EXPERTS - 1)
    def _():
        for i in range(RING):
            @pl.when(i < ntot)
            def _():
                out_wait((ntot - 1 - i) & (RING - 1))


def _ffn(blk0, nblk, ntot, xs, wg, wu, wd):
    grid_spec = pltpu.PrefetchScalarGridSpec(
        num_scalar_prefetch=3,
        grid=(N_EXPERTS,),
        in_specs=[pl.BlockSpec(memory_space=pl.ANY),
                  pl.BlockSpec((1, D_MODEL, EXPERT_FF), lambda e, *_: (e, 0, 0)),
                  pl.BlockSpec((1, D_MODEL, EXPERT_FF), lambda e, *_: (e, 0, 0)),
                  pl.BlockSpec((1, EXPERT_FF, D_MODEL), lambda e, *_: (e, 0, 0))],
        out_specs=pl.BlockSpec(memory_space=pl.ANY),
        scratch_shapes=[pltpu.VMEM((RING, ROW_BLOCK * QUAD, 128), jnp.uint32),
                        pltpu.VMEM((RING, ROW_BLOCK * QUAD, 128), jnp.uint32),
                        pltpu.SemaphoreType.DMA((RING,)), pltpu.SemaphoreType.DMA((RING,)),
                        pltpu.VMEM((D_MODEL, 2 * EXPERT_FF), _BF16), pltpu.VMEM((EXPERT_FF, D_MODEL), _BF16)],
    )
    return pl.pallas_call(
        _ffn_body,
        grid_spec=grid_spec,
        out_shape=jax.ShapeDtypeStruct(xs.shape, jnp.uint32),
        compiler_params=_cparams("arbitrary"),
        name="ffn",
    )(blk0, nblk, ntot, xs, wg, wu, wd)


def _combine_body(pos_hbm, gate_ref, h1_ref, ys_hbm, wsg_ref, wsu_ref, wsd_ref, g_ref, b_ref,
                  out_ref, pos_smem, psem, ybuf, sem, *, tt):
    i = pl.program_id(0)
    cp = pltpu.make_async_copy(pos_hbm.at[i], pos_smem, psem)
    cp.start()
    cp.wait()

    def row_copy(t, k):
        return pltpu.make_async_copy(ys_hbm.at[pl.ds(QUAD * pos_smem[k * tt + t], QUAD)],
                                     ybuf.at[k, pl.ds(QUAD * t, QUAD)], sem)

    def issue(t, c):
        for k in range(TOP_K):
            row_copy(t, k).start(priority=k % 2)
        return c

    lax.fori_loop(0, tt, issue, 0, unroll=ISSUE_UNROLL)

    x = h1_ref[...]
    xb = x.astype(_BF16)
    shared = _mm(_silu(_mm(xb, wsg_ref[...])) * _mm(xb, wsu_ref[...]), wsd_ref[...])

    for k in range(TOP_K):
        pltpu.make_async_copy(ys_hbm.at[pl.ds(0, QUAD * tt)], ybuf.at[k], sem).wait()

    gcol = gate_ref[...].T
    acc_lo = jnp.zeros((tt, HALF), _F32)
    acc_hi = jnp.zeros((tt, HALF), _F32)
    for k in range(TOP_K):
        lo, hi = _unpack_halves(_load_rows(ybuf.at[k], tt))
        acc_lo = acc_lo + gcol[:, k:k + 1] * lo
        acc_hi = acc_hi + gcol[:, k:k + 1] * hi
    routed = jnp.concatenate([acc_lo, acc_hi], axis=1)
    out_ref[...] = _layer_norm(DN_ALPHA * x + (routed + shared), g_ref[...], b_ref[...])


def _combine(pos_tiles, gate, h1, ys, wsg, wsu, wsd, g, b, *, tt):
    t = h1.shape[0]
    full = lambda a: pl.BlockSpec(a.shape, lambda i: (0,) * a.ndim)
    return pl.pallas_call(
        functools.partial(_combine_body, tt=tt),
        grid=(t // tt,),
        in_specs=[pl.BlockSpec(memory_space=pl.ANY), pl.BlockSpec((TOP_K, tt), lambda i: (0, i)),
                  pl.BlockSpec((tt, D_MODEL), lambda i: (i, 0)), pl.BlockSpec(memory_space=pl.ANY),
                  full(wsg), full(wsu), full(wsd), full(g), full(b)],
        out_specs=pl.BlockSpec((tt, D_MODEL), lambda i: (i, 0)),
        out_shape=jax.ShapeDtypeStruct((t, D_MODEL), _F32),
        scratch_shapes=[pltpu.SMEM((TOP_K * tt,), _I32), pltpu.SemaphoreType.DMA,
                        pltpu.VMEM((TOP_K, tt * QUAD, 128), jnp.uint32), pltpu.SemaphoreType.DMA],
        compiler_params=_cparams("arbitrary"),
        name="combine",
    )(pos_tiles, gate, h1, ys, wsg, wsu, wsd, g, b)


def _sc_gather(table, idx):
    b = idx.shape[0]
    per_w = b // SC_WORKERS
    assert per_w * SC_WORKERS == b and per_w % SC_CHUNK == 0
    row = table.shape[1:]
    mesh = plsc.VectorSubcoreMesh(core_axis_name="c", subcore_axis_name="s",
                                  num_cores=SC_CORES, num_subcores=SC_SUBCORES)

    @functools.partial(
        pl.kernel, mesh=mesh,
        out_type=jax.ShapeDtypeStruct((b,) + row, table.dtype),
        scratch_types=[pltpu.VMEM((SC_CHUNK,), _I32), pltpu.VMEM((SC_CHUNK,) + row, table.dtype),
                       pltpu.SemaphoreType.DMA],
        name="sc_gather",
    )
    def gather(table_hbm, idx_hbm, out_hbm, idx_v, rows_v, sem):
        wid = lax.axis_index("s") * SC_CORES + lax.axis_index("c")
        base = wid * per_w

        @pl.loop(0, per_w // SC_CHUNK)
        def _(i):
            off = pl.multiple_of(base + i * SC_CHUNK, SC_CHUNK)
            pltpu.sync_copy(idx_hbm.at[pl.ds(off, SC_CHUNK)], idx_v)
            pltpu.async_copy(table_hbm.at[idx_v], rows_v, sem).wait()
            pltpu.sync_copy(rows_v, out_hbm.at[pl.ds(off, SC_CHUNK)])

    return gather(table, idx)


def _combine_stream_body(gate_ref, h1_ref, yg_ref, wsg_ref, wsu_ref, wsd_ref, g_ref, b_ref, out_ref, *, tt):
    x = h1_ref[...]
    xb = x.astype(_BF16)
    shared = _mm(_silu(_mm(xb, wsg_ref[...])) * _mm(xb, wsu_ref[...]), wsd_ref[...])
    gcol = gate_ref[...].T
    acc_lo = jnp.zeros((tt, HALF), _F32)
    acc_hi = jnp.zeros((tt, HALF), _F32)
    for k in range(TOP_K):
        lo, hi = _unpack_halves(_load_rows(yg_ref.at[0, k], tt))
        acc_lo = acc_lo + gcol[:, k:k + 1] * lo
        acc_hi = acc_hi + gcol[:, k:k + 1] * hi
    routed = jnp.concatenate([acc_lo, acc_hi], axis=1)
    out_ref[...] = _layer_norm(DN_ALPHA * x + (routed + shared), g_ref[...], b_ref[...])


def _combine_stream(gate, h1, yg, wsg, wsu, wsd, g, b, *, tt):
    t = h1.shape[0]
    full = lambda a: pl.BlockSpec(a.shape, lambda i: (0,) * a.ndim)
    return pl.pallas_call(
        functools.partial(_combine_stream_body, tt=tt),
        grid=(t // tt,),
        in_specs=[pl.BlockSpec((TOP_K, tt), lambda i: (0, i)), pl.BlockSpec((tt, D_MODEL), lambda i: (i, 0)),
                  pl.BlockSpec((1, TOP_K, tt * QUAD, 128), lambda i: (i, 0, 0, 0)),
                  full(wsg), full(wsu), full(wsd), full(g), full(b)],
        out_specs=pl.BlockSpec((tt, D_MODEL), lambda i: (i, 0)),
        out_shape=jax.ShapeDtypeStruct((t, D_MODEL), _F32),
        compiler_params=_cparams("arbitrary"),
        name="combine",
    )(gate, h1, yg, wsg, wsu, wsd, g, b)


def _pick(n, pref):
    t = min(n, pref)
    while n % t:
        t -= CHUNK
    return t


def _mixer(x, tails, s0, wts, gnw, *, lt, lg, nbb):
    yc, q, k, v, z, bgc, bgr, tails_out = _premix(x, tails, wts, lt=lt)
    bsz, seq, _ = x.shape
    nch = seq // CHUNK
    grow = bgr[:, GDN_HEADS:2 * GDN_HEADS, :].reshape(bsz, GDN_HEADS, nch, CHUNK)
    grow = grow.transpose(0, 2, 1, 3).reshape(bsz, nch, 1, STACK)
    yg, s_out = _gdn(q, k, v, z, bgc, grow, s0, gnw, lg=lg, nbb=nbb)
    return yc, yg, tails_out, s_out


def kernel(x, meta_tokens, w_in, conv_w, conv_norm_w, gdn_conv_w, a_log, dt_bias, gdn_norm_w, w_out,
           ln1_g, ln1_b, w_router, b_router, w_gate, w_up, w_down, ws_gate, ws_up, ws_down, ln2_g, ln2_b):
    assert w_in.shape[0] == 1, "single-layer stack"
    bsz, seq, d = x.shape
    assert d == D_MODEL and seq % CHUNK == 0
    c, gw = CONV_WIDTH, GDN_WIDTH
    win = w_in[0].astype(_BF16)
    wbd = win[:, 3 * c + 4 * gw:]
    zpad = jnp.zeros((128 - 2 * GDN_HEADS,), _F32)
    zpad4 = jnp.zeros((GDN_HEADS,), _F32)
    prow = jnp.zeros((8, 128), _F32)
    prow = prow.at[0].set(jnp.concatenate([zpad4, a_log[0], zpad]))
    prow = prow.at[1].set(jnp.concatenate([zpad4, dt_bias[0], zpad]))
    wts = (win[:, :3 * c], win[:, 3 * c:3 * c + 3 * gw], win[:, 3 * c + 3 * gw:3 * c + 4 * gw],
           jnp.pad(wbd, ((0, 0), (0, 128 - 2 * GDN_HEADS))), wbd.T,
           conv_w[0], conv_norm_w, gdn_conv_w[0], prow, prow.T[:8])
    gnw = gdn_norm_w

    meta = jnp.concatenate([jnp.zeros((CHUNK - N_META, d), x.dtype), meta_tokens.astype(x.dtype)])[None]
    tails0 = jnp.zeros((HIST, c + 3 * gw), _F32)
    s00 = jnp.zeros((GDN_HEADS, GDN_HEAD_DIM, GDN_HEAD_DIM), _F32)
    _, _, tails_m, s_m = _mixer(meta, tails0, s00, wts, gnw, lt=CHUNK, lg=CHUNK, nbb=1)

    yc, yg, _, _ = _mixer(x, tails_m[0], s_m[0], wts, gnw, lt=_pick(seq, 512), lg=_pick(seq, 512),
                          nbb=GDN_ROWS if bsz % GDN_ROWS == 0 else 1)

    t = bsz * seq
    tm = _pick(t, 512)
    h1, h1p = _outproj(yc.reshape(t, c), yg.reshape(t, gw), x.reshape(t, d), w_out[0].astype(_BF16),
                       ln1_g, ln1_b, tm=tm)

    tt = _pick(t, 256)
    wr_t = w_router[0].T
    wr_hi = wr_t.astype(_BF16)
    wr_lo = (wr_t - wr_hi.astype(_F32)).astype(_BF16)
    idx, gate, rank, cnt = _router(h1, wr_hi, wr_lo, b_router[0][:, None], tt=tt)

    counts = cnt[:, 0].astype(_I32)
    pcounts = (counts + ROW_BLOCK - 1) // ROW_BLOCK * ROW_BLOCK
    pends = jnp.cumsum(pcounts)
    pstarts = pends - pcounts
    nb = t * TOP_K // ROW_BLOCK + N_EXPERTS

    pos = _position(idx, rank, pstarts[:, None].astype(_I32), tt=tt).reshape(t // tt, TOP_K * tt)
    xs = _dispatch(counts, pstarts.astype(_I32), pcounts.astype(_I32), pos, h1p, nb * ROW_BLOCK, tt=tt)
    ys = _ffn((pstarts // ROW_BLOCK).astype(_I32), (pcounts // ROW_BLOCK).astype(_I32),
              (pends[-1:] // ROW_BLOCK).astype(_I32), xs, w_gate[0], w_up[0], w_down[0])
    yg = _sc_gather(ys.reshape(nb * ROW_BLOCK, QUAD, 128), pos.reshape(t * TOP_K))
    yg = yg.reshape(t // tt, TOP_K, tt * QUAD, 128)
    out = _combine_stream(gate, h1, yg, ws_gate[0].astype(_BF16), ws_up[0].astype(_BF16),
                          ws_down[0].astype(_BF16), ln2_g, ln2_b, tt=tt)
    return out.reshape(bsz, seq, d)
```

```python
import functools

import jax
import jax.numpy as jnp
from jax import lax
from jax.experimental import pallas as pl
from jax.experimental.pallas import tpu as pltpu
from jax.experimental.pallas import tpu_sc as plsc

_F32 = jnp.float32
_BF16 = jnp.bfloat16
_I32 = jnp.int32

D_MODEL = 1024
N_META = 16
CONV_WIDTH = 512
CONV_K = 3
GDN_HEADS = 4
GDN_HEAD_DIM = 128
GDN_WIDTH = GDN_HEADS * GDN_HEAD_DIM
GDN_CONV_K = 4
CHUNK = 64
N_EXPERTS = 256
TOP_K = 8
N_GROUPS = 8
TOPK_GROUPS = 4
E_PER_GROUP = N_EXPERTS // N_GROUPS
EXPERT_FF = 256
ROUTED_SCALE = 2.5
ROW_BLOCK = 256
DN_ALPHA = 2.0 ** 0.25
NORM_EPS = 1e-5
HALF = D_MODEL // 2
QUAD = HALF // 128
STACK = GDN_HEADS * CHUNK
HIST = 8
GDN_ROWS = 4
ISSUE_UNROLL = 8
SC_CORES = 2
SC_SUBCORES = 16
SC_WORKERS = SC_CORES * SC_SUBCORES
SC_CHUNK = 64
SC_WINDOW = 128
RING = 4
IN_AHEAD = RING - 1

V7X_VMEM_BYTES = 64 * 1024 * 1024
VMEM_LIMIT = V7X_VMEM_BYTES - 8 * 1024 * 1024


def _cparams(*sem):
    return pltpu.CompilerParams(dimension_semantics=sem, vmem_limit_bytes=VMEM_LIMIT)


def _mm(a, b):
    return jnp.dot(a.astype(_BF16), b.astype(_BF16), preferred_element_type=_F32)


def _mm_nt(a, b):
    return lax.dot_general(a.astype(_BF16), b.astype(_BF16), (((1,), (1,)), ((), ())),
                           preferred_element_type=_F32)


def _mm_tn(a, b):
    return lax.dot_general(a.astype(_BF16), b.astype(_BF16), (((0,), (0,)), ((), ())),
                           preferred_element_type=_F32)


def _sigmoid(x):
    return 1.0 / (1.0 + jnp.exp(-x))


def _silu(x):
    return x * _sigmoid(x)


def _softplus(x):
    return jnp.maximum(x, 0.0) + jnp.log1p(jnp.exp(-jnp.abs(x)))


def _pack_halves(y):
    return pltpu.pack_elementwise([y[:, :HALF], y[:, HALF:]], packed_dtype=_BF16)


def _store_rows(ref, packed):
    r = packed.shape[0]
    for c in range(QUAD):
        ref[pl.ds(c, r, stride=QUAD), :] = packed[:, c * 128:(c + 1) * 128]


def _load_rows(ref, r):
    return jnp.concatenate([ref[pl.ds(c, r, stride=QUAD), :] for c in range(QUAD)], axis=1)


def _unpack_halves(p):
    lo = pltpu.unpack_elementwise(p, index=0, packed_dtype=_BF16, unpacked_dtype=_F32)
    hi = pltpu.unpack_elementwise(p, index=1, packed_dtype=_BF16, unpacked_dtype=_F32)
    return lo, hi


def _layer_norm(h, g, b):
    mu = jnp.mean(h, axis=-1, keepdims=True)
    d = h - mu
    var = jnp.mean(d * d, axis=-1, keepdims=True)
    return d * lax.rsqrt(var + NORM_EPS) * g + b


def _premix_body(x_ref, tails_ref, wa_ref, wq_ref, wz_ref, wbd_ref, wbdt_ref, cw_ref, cnw_ref,
                 gcw_ref, prow_ref, pcol_ref,
                 yc_ref, q_ref, k_ref, v_ref, z_ref, bgc_ref, bgr_ref, tout_ref, ext_ref, *, lt):
    cw_ = CONV_WIDTH

    @pl.when(pl.program_id(1) == 0)
    def _():
        ext_ref[0:HIST, :] = tails_ref[...]

    xb = x_ref[0].astype(_BF16)
    pa = jnp.dot(xb, wa_ref[...], preferred_element_type=_F32)
    gate_b = pa[:, 0:cw_]
    u = pa[:, cw_:2 * cw_] * pa[:, 2 * cw_:3 * cw_]
    ext_ref[HIST:HIST + lt, 0:cw_] = u
    pq = jnp.dot(xb, wq_ref[...], preferred_element_type=_F32)
    ext_ref[HIST:HIST + lt, cw_:] = pq

    cw = cw_ref[...]
    ca = u * cw[CONV_K - 1:CONV_K, :]
    for j in range(CONV_K - 1):
        ca = ca + ext_ref[pl.ds(HIST - (CONV_K - 1) + j, lt), 0:cw_] * cw[j:j + 1, :]
    yc = gate_b * ca
    ms = jnp.mean(yc * yc, axis=-1, keepdims=True)
    yc_ref[0] = (yc * lax.rsqrt(ms + NORM_EPS) * cnw_ref[...]).astype(_BF16)

    gcw = gcw_ref[...]
    cq = pq * gcw[GDN_CONV_K - 1:GDN_CONV_K, :]
    for j in range(GDN_CONV_K - 1):
        cq = cq + ext_ref[pl.ds(HIST - (GDN_CONV_K - 1) + j, lt), cw_:] * gcw[j:j + 1, :]
    s = _silu(cq)
    for h in range(GDN_HEADS):
        lo, hi = h * GDN_HEAD_DIM, (h + 1) * GDN_HEAD_DIM
        qh = s[:, lo:hi]
        kh = s[:, GDN_WIDTH + lo:GDN_WIDTH + hi]
        qn = qh * lax.rsqrt(jnp.sum(qh * qh, axis=-1, keepdims=True) + 1e-6)
        kn = kh * lax.rsqrt(jnp.sum(kh * kh, axis=-1, keepdims=True) + 1e-6)
        q_ref[0, :, lo:hi] = (qn * (GDN_HEAD_DIM ** -0.5)).astype(_BF16)
        k_ref[0, :, lo:hi] = kn.astype(_BF16)
    v_ref[0] = s[:, 2 * GDN_WIDTH:].astype(_BF16)
    z_ref[0] = jnp.dot(xb, wz_ref[...], preferred_element_type=_F32).astype(_BF16)

    bdc = jnp.dot(xb, wbd_ref[...], preferred_element_type=_F32)
    prow = prow_ref[...]
    g_c = -jnp.exp(prow[0:1, :]) * _softplus(bdc + prow[1:2, :])
    lane = lax.broadcasted_iota(_I32, bdc.shape, 1)
    bgc_ref[0] = jnp.where(lane < GDN_HEADS, _sigmoid(bdc), g_c)
    bdr = _mm_nt(wbdt_ref[...], xb)
    pcol = pcol_ref[...]
    g_r = -jnp.exp(pcol[:, 0:1]) * _softplus(bdr + pcol[:, 1:2])
    row = lax.broadcasted_iota(_I32, bdr.shape, 0)
    bgr_ref[0] = jnp.where(row < GDN_HEADS, _sigmoid(bdr), g_r)

    tail = ext_ref[lt:lt + HIST, :]
    ext_ref[0:HIST, :] = tail
    tout_ref[0] = tail


def _premix(x, tails, wts, *, lt):
    bsz, seq, d = x.shape
    assert seq % lt == 0
    grid = (bsz, seq // lt)
    full = lambda a: pl.BlockSpec(a.shape, lambda b, j: (0,) * a.ndim)
    tok = lambda w: pl.BlockSpec((1, lt, w), lambda b, j: (b, j, 0))
    (wa, wq, wz, wbd, wbdt, cw, cnw, gcw, prow, pcol) = wts
    ext_w = CONV_WIDTH + 3 * GDN_WIDTH
    out_shape = (
        jax.ShapeDtypeStruct((bsz, seq, CONV_WIDTH), _BF16),
        jax.ShapeDtypeStruct((bsz, seq, GDN_WIDTH), _BF16),
        jax.ShapeDtypeStruct((bsz, seq, GDN_WIDTH), _BF16),
        jax.ShapeDtypeStruct((bsz, seq, GDN_WIDTH), _BF16),
        jax.ShapeDtypeStruct((bsz, seq, GDN_WIDTH), _BF16),
        jax.ShapeDtypeStruct((bsz, seq, 128), _F32),
        jax.ShapeDtypeStruct((bsz, 8, seq), _F32),
        jax.ShapeDtypeStruct((bsz, HIST, ext_w), _F32),
    )
    out_specs = (tok(CONV_WIDTH), tok(GDN_WIDTH), tok(GDN_WIDTH), tok(GDN_WIDTH), tok(GDN_WIDTH),
                 tok(128), pl.BlockSpec((1, 8, lt), lambda b, j: (b, 0, j)),
                 pl.BlockSpec((1, HIST, ext_w), lambda b, j: (b, 0, 0)))
    return pl.pallas_call(
        functools.partial(_premix_body, lt=lt),
        grid=grid,
        in_specs=[tok(d), full(tails)] + [full(w) for w in wts],
        out_specs=out_specs,
        out_shape=out_shape,
        scratch_shapes=[pltpu.VMEM((HIST + lt, ext_w), _F32)],
        compiler_params=_cparams("arbitrary", "arbitrary"),
        name="premix",
    )(x, tails, *wts)


def _cumsum_rows(x):
    row = lax.broadcasted_iota(_I32, x.shape, 0)
    s = 1
    while s < x.shape[0]:
        x = x + jnp.where(row >= s, pltpu.roll(x, s, 0), 0.0)
        s *= 2
    return x


def _cumsum_lanes_seg(x):
    lane = lax.broadcasted_iota(_I32, x.shape, 1) & (CHUNK - 1)
    s = 1
    while s < CHUNK:
        x = x + jnp.where(lane >= s, pltpu.roll(x, s, 1), 0.0)
        s *= 2
    return x


def _stack_heads(a):
    return jnp.concatenate([a[:, h * GDN_HEAD_DIM:(h + 1) * GDN_HEAD_DIM] for h in range(GDN_HEADS)], axis=0)


def _gdn_body(q_ref, k_ref, v_ref, z_ref, bgc_ref, grow_ref, s0_ref, gnw_ref,
              y_ref, sout_ref, s_ref, *, nc, nbb):
    @pl.when(pl.program_id(1) == 0)
    def _():
        for r in range(nbb):
            s_ref[r] = s0_ref[...]

    ri = lax.broadcasted_iota(_I32, (STACK, STACK), 0)
    ci = lax.broadcasted_iota(_I32, (STACK, STACK), 1)
    same64 = (ri >> 6) == (ci >> 6)
    same32 = (ri >> 5) == (ci >> 5)
    same16 = (ri >> 4) == (ci >> 4)
    low_incl = same64 & (ri >= ci)
    low_strict = same64 & (ri > ci)
    gnw = gnw_ref[...]

    def chunk_row(r, c):
        off = pl.multiple_of(c * CHUNK, CHUNK)
        q_all = _stack_heads(q_ref[r, pl.ds(off, CHUNK), :].astype(_F32))
        k_all = _stack_heads(k_ref[r, pl.ds(off, CHUNK), :].astype(_F32))
        v_all = _stack_heads(v_ref[r, pl.ds(off, CHUNK), :].astype(_F32))
        bgc = bgc_ref[r, pl.ds(off, CHUNK), :]
        gcs = _cumsum_rows(bgc)
        hd = (CHUNK, GDN_HEAD_DIM)
        beta_b = jnp.concatenate(
            [jnp.broadcast_to(bgc[:, h:h + 1], hd) for h in range(GDN_HEADS)], axis=0)
        gc_b = jnp.concatenate(
            [jnp.broadcast_to(gcs[:, GDN_HEADS + h:GDN_HEADS + h + 1], hd) for h in range(GDN_HEADS)], axis=0)
        gl = [gcs[CHUNK - 1:CHUNK, GDN_HEADS + h:GDN_HEADS + h + 1] for h in range(GDN_HEADS)]
        gl_b = jnp.concatenate([jnp.broadcast_to(g1, hd) for g1 in gl], axis=0)
        gcr = _cumsum_lanes_seg(jnp.broadcast_to(grow_ref[r, c], (8, STACK)))[0:1, :]

        diff = jnp.concatenate([gc_b, gc_b], axis=1) - gcr
        decay = jnp.exp(jnp.where(low_incl, diff, -1e30))
        kb = k_all * beta_b
        a1 = _mm_nt(jnp.concatenate([kb, q_all], axis=0), k_all)
        yield
        m = jnp.where(low_strict, a1[:STACK] * decay, 0.0)
        attn = a1[STACK:] * decay

        l16 = jnp.where(same16, m, 0.0)
        c1 = jnp.where(same32 & jnp.logical_not(same16), m, 0.0)
        c2 = jnp.where(same32, 0.0, m)
        p2 = _mm(l16, l16)
        yield
        p4 = _mm(p2, p2)
        t = _mm(l16, p2)
        yield
        na = p2 - l16 - t
        p8 = _mm(p4, p4)
        t = _mm(na, p4)
        yield
        nb = na + p4 + t
        t = _mm(nb, p8)
        yield
        ncm = nb + p8 + t
        t = _mm(c1, ncm)
        yield
        y1 = c1 + t
        t = _mm(ncm, y1)
        yield
        n1 = ncm - y1 - t
        t = _mm(c2, n1)
        yield
        y2 = c2 + t
        t = _mm(n1, y2)
        yield
        nt = n1 - y2 - t

        egc = jnp.exp(gc_b)
        rhs = jnp.concatenate([v_all * beta_b, kb * egc], axis=1)
        t = _mm(nt, rhs)
        yield
        uw = rhs + t
        u_all = uw[:, :GDN_HEAD_DIM]
        w_all = uw[:, GDN_HEAD_DIM:]
        qd = q_all * egc
        kd = k_all * jnp.exp(gl_b - gc_b)

        bs = []
        for h in range(GDN_HEADS):
            r0, r1 = h * CHUNK, (h + 1) * CHUNK
            bs.append(_mm(jnp.concatenate([w_all[r0:r1], qd[r0:r1]], axis=0), s_ref[r, h]))
        yield
        vn = [u_all[h * CHUNK:(h + 1) * CHUNK] - bs[h][:CHUNK] for h in range(GDN_HEADS)]
        vn_all = jnp.concatenate(vn, axis=0)
        t = _mm(attn, vn_all)
        ds = [_mm_tn(kd[h * CHUNK:(h + 1) * CHUNK], vn[h]) for h in range(GDN_HEADS)]
        yield
        o_all = jnp.concatenate([b[CHUNK:] for b in bs], axis=0) + t
        for h in range(GDN_HEADS):
            r0, r1 = h * CHUNK, (h + 1) * CHUNK
            s_ref[r, h] = s_ref[r, h] * jnp.exp(gl[h]) + ds[h]
            o = o_all[r0:r1]
            zz = z_ref[r, pl.ds(off, CHUNK), h * GDN_HEAD_DIM:(h + 1) * GDN_HEAD_DIM].astype(_F32)
            on = o * lax.rsqrt(jnp.mean(o * o, axis=-1, keepdims=True) + NORM_EPS) * gnw
            y_ref[r, pl.ds(off, CHUNK), h * GDN_HEAD_DIM:(h + 1) * GDN_HEAD_DIM] = (on * _silu(zz)).astype(_BF16)

    def chunk(c, carry):
        live = [chunk_row(r, c) for r in range(nbb)]
        while live:
            live = [g for g in live if next(g, live) is not live]
        return carry

    lax.fori_loop(0, nc, chunk, 0)
    sout_ref[...] = s_ref[...]


def _gdn(q, k, v, z, bgc, grow, s0, gnw, *, lg, nbb):
    bsz, seq, _ = q.shape
    assert seq % lg == 0 and lg % CHUNK == 0 and bsz % nbb == 0
    nc = lg // CHUNK
    tok = lambda w: pl.BlockSpec((nbb, lg, w), lambda b, j: (b, j, 0))
    full = lambda a: pl.BlockSpec(a.shape, lambda b, j: (0,) * a.ndim)
    st = (nbb, GDN_HEADS, GDN_HEAD_DIM, GDN_HEAD_DIM)
    return pl.pallas_call(
        functools.partial(_gdn_body, nc=nc, nbb=nbb),
        grid=(bsz // nbb, seq // lg),
        in_specs=[tok(GDN_WIDTH)] * 4 + [tok(128), pl.BlockSpec((nbb, nc, 1, STACK), lambda b, j: (b, j, 0, 0)),
                                           full(s0), full(gnw)],
        out_specs=(tok(GDN_WIDTH), pl.BlockSpec(st, lambda b, j: (b, 0, 0, 0))),
        out_shape=(jax.ShapeDtypeStruct((bsz, seq, GDN_WIDTH), _BF16),
                   jax.ShapeDtypeStruct((bsz, GDN_HEADS, GDN_HEAD_DIM, GDN_HEAD_DIM), _F32)),
        scratch_shapes=[pltpu.VMEM(st, _F32)],
        compiler_params=_cparams("arbitrary", "arbitrary"),
        name="gdn",
    )(q, k, v, z, bgc, grow, s0, gnw)


def _outproj_body(yc_ref, yg_ref, x_ref, wo_ref, g_ref, b_ref, h1_ref, h1p_ref):
    mix = (jnp.dot(yc_ref[...], wo_ref[0:CONV_WIDTH, :], preferred_element_type=_F32)
           + jnp.dot(yg_ref[...], wo_ref[CONV_WIDTH:, :], preferred_element_type=_F32))
    h1 = _layer_norm(DN_ALPHA * x_ref[...] + mix, g_ref[...], b_ref[...])
    h1_ref[...] = h1
    _store_rows(h1p_ref, _pack_halves(h1))


def _outproj(yc, yg, x2d, wo, g, b, *, tm):
    t = x2d.shape[0]
    assert t % tm == 0
    row = lambda w: pl.BlockSpec((tm, w), lambda i: (i, 0))
    full = lambda a: pl.BlockSpec(a.shape, lambda i: (0,) * a.ndim)
    return pl.pallas_call(
        _outproj_body,
        grid=(t // tm,),
        in_specs=[row(CONV_WIDTH), row(GDN_WIDTH), row(D_MODEL), full(wo), full(g), full(b)],
        out_specs=(row(D_MODEL), pl.BlockSpec((tm * QUAD, 128), lambda i: (i, 0))),
        out_shape=(jax.ShapeDtypeStruct((t, D_MODEL), _F32), jax.ShapeDtypeStruct((t * QUAD, 128), jnp.uint32)),
        compiler_params=_cparams("arbitrary"),
        name="outproj",
    )(yc, yg, x2d, wo, g, b)


def _router_body(h1_ref, wh_ref, wl_ref, br_ref, idx_ref, gate_ref, rank_ref, cnt_ref, carry_ref, *, tt):
    @pl.when(pl.program_id(0) == 0)
    def _():
        carry_ref[...] = jnp.zeros_like(carry_ref)

    x = h1_ref[...]
    xh = x.astype(_BF16)
    xl = (x - xh.astype(_F32)).astype(_BF16)
    wh = wh_ref[...]
    logits = _mm_nt(wh, xh) + _mm_nt(wh, xl) + _mm_nt(wl_ref[...], xh)
    scores = _sigmoid(logits)
    sel = scores + br_ref[...]
    ninf = -jnp.inf

    r32 = lax.broadcasted_iota(_I32, (E_PER_GROUP, tt), 0)
    gsc = []
    for g in range(N_GROUPS):
        xg = sel[g * E_PER_GROUP:(g + 1) * E_PER_GROUP]
        m1 = jnp.max(xg, axis=0, keepdims=True)
        i1 = jnp.min(jnp.where(xg == m1, r32, E_PER_GROUP), axis=0, keepdims=True)
        m2 = jnp.max(jnp.where(r32 == i1, ninf, xg), axis=0, keepdims=True)
        gsc.append(m1 + m2)
    work = jnp.concatenate(gsc, axis=0)
    r8 = lax.broadcasted_iota(_I32, (N_GROUPS, tt), 0)
    gkeep = jnp.zeros((N_GROUPS, tt), _F32)
    for _ in range(TOPK_GROUPS):
        m = jnp.max(work, axis=0, keepdims=True)
        gi = jnp.min(jnp.where(work == m, r8, N_GROUPS), axis=0, keepdims=True)
        pick = r8 == gi
        gkeep = jnp.where(pick, 1.0, gkeep)
        work = jnp.where(pick, ninf, work)
    selm = jnp.concatenate(
        [jnp.where(gkeep[g:g + 1] > 0.5, sel[g * E_PER_GROUP:(g + 1) * E_PER_GROUP], ninf)
         for g in range(N_GROUPS)], axis=0)

    re = lax.broadcasted_iota(_I32, (N_EXPERTS, tt), 0)
    msel = jnp.zeros((N_EXPERTS, tt), _F32)
    idxs, gates = [], []
    for _ in range(TOP_K):
        m = jnp.max(selm, axis=0, keepdims=True)
        ii = jnp.min(jnp.where(selm == m, re, N_EXPERTS), axis=0, keepdims=True)
        hit = re == ii
        idxs.append(ii)
        gates.append(jnp.sum(jnp.where(hit, scores, 0.0), axis=0, keepdims=True))
        selm = jnp.where(hit, ninf, selm)
        msel = jnp.where(hit, 1.0, msel)
    gate = jnp.concatenate(gates, axis=0)
    gate_ref[...] = gate / jnp.sum(gate, axis=0, keepdims=True) * ROUTED_SCALE
    idx_ref[...] = jnp.concatenate(idxs, axis=0)

    ta = lax.broadcasted_iota(_I32, (tt, tt), 0)
    tb = lax.broadcasted_iota(_I32, (tt, tt), 1)
    earlier = jnp.where(ta < tb, 1.0, 0.0)
    carry = carry_ref[...]
    rank_all = _mm(msel, earlier) + carry[:, 0:1]
    rank_ref[...] = jnp.concatenate(
        [jnp.sum(jnp.where(re == ii, rank_all, 0.0), axis=0, keepdims=True) for ii in idxs],
        axis=0).astype(_I32)
    carry = carry + jnp.sum(msel, axis=1, keepdims=True)
    carry_ref[...] = carry
    cnt_ref[...] = carry


def _router(h1, wh, wl, br, *, tt):
    t = h1.shape[0]
    assert t % tt == 0
    full = lambda a: pl.BlockSpec(a.shape, lambda i: (0,) * a.ndim)
    kt = pl.BlockSpec((TOP_K, tt), lambda i: (0, i))
    return pl.pallas_call(
        functools.partial(_router_body, tt=tt),
        grid=(t // tt,),
        in_specs=[pl.BlockSpec((tt, D_MODEL), lambda i: (i, 0)), full(wh), full(wl), full(br)],
        out_specs=(kt, kt, kt, pl.BlockSpec((N_EXPERTS, 128), lambda i: (0, 0))),
        out_shape=(jax.ShapeDtypeStruct((TOP_K, t), _I32), jax.ShapeDtypeStruct((TOP_K, t), _F32),
                   jax.ShapeDtypeStruct((TOP_K, t), _I32), jax.ShapeDtypeStruct((N_EXPERTS, 128), _F32)),
        scratch_shapes=[pltpu.VMEM((N_EXPERTS, 128), _F32)],
        compiler_params=_cparams("arbitrary"),
        name="router",
    )(h1, wh, wl, br)


def _position_body(idx_ref, rank_ref, pstart_ref, pos_ref, *, tt):
    re = lax.broadcasted_iota(_I32, (N_EXPERTS, tt), 0)
    ps = pstart_ref[...]
    idx = idx_ref[...]
    rows = [jnp.sum(jnp.where(re == idx[k:k + 1], ps, 0), axis=0, keepdims=True) for k in range(TOP_K)]
    pos_ref[0] = jnp.concatenate(rows, axis=0) + rank_ref[...]


def _position(idx, rank, pstart, *, tt):
    t = idx.shape[1]
    kt = pl.BlockSpec((TOP_K, tt), lambda i: (0, i))
    return pl.pallas_call(
        functools.partial(_position_body, tt=tt),
        grid=(t // tt,),
        in_specs=[kt, kt, pl.BlockSpec(pstart.shape, lambda i: (0, 0))],
        out_specs=pl.BlockSpec((1, TOP_K, tt), lambda i: (i, 0, 0)),
        out_shape=jax.ShapeDtypeStruct((t // tt, TOP_K, tt), _I32),
        compiler_params=_cparams("arbitrary"),
        name="position",
    )(idx, rank, pstart)


def _dispatch_body(cnt_ref, pst_ref, pcn_ref, pos_hbm, h1p_ref, xs_out, pos_smem, psem, sem, zbuf, zsem, *, tt):
    i = pl.program_id(0)
    cp = pltpu.make_async_copy(pos_hbm.at[i], pos_smem, psem)
    cp.start()

    @pl.when(i == 0)
    def _():
        zbuf[...] = jnp.zeros_like(zbuf)

        def pad_runs(e, act):
            pad = pcn_ref[e] - cnt_ref[e]
            base = pst_ref[e] + cnt_ref[e]
            for b in range(ROW_BLOCK.bit_length() - 1):
                n = 1 << b

                @pl.when(((pad >> b) & 1) == 1)
                def _():
                    off = base + (pad & (n - 1))
                    act(pltpu.make_async_copy(zbuf.at[pl.ds(0, QUAD * n)],
                                              xs_out.at[pl.ds(QUAD * off, QUAD * n)], zsem))

        def start_all(e, c):
            pad_runs(e, lambda d: d.start())
            return c

        def wait_all(e, c):
            pad_runs(e, lambda d: d.wait())
            return c

        lax.fori_loop(0, N_EXPERTS, start_all, 0)
        lax.fori_loop(0, N_EXPERTS, wait_all, 0)

    cp.wait()

    def row_copy(t, k):
        return pltpu.make_async_copy(h1p_ref.at[pl.ds(QUAD * t, QUAD)],
                                     xs_out.at[pl.ds(QUAD * pos_smem[k * tt + t], QUAD)], sem)

    def issue(t, c):
        for k in range(TOP_K):
            row_copy(t, k).start(priority=k % 2)
        return c

    lax.fori_loop(0, tt, issue, 0, unroll=ISSUE_UNROLL)

    for k in range(TOP_K):
        pltpu.make_async_copy(h1p_ref, xs_out.at[pl.ds(0, QUAD * tt)], sem).wait()


def _dispatch(counts, pstarts, pcounts, pos_tiles, h1p, n_rows, *, tt):
    t = h1p.shape[0] // QUAD
    grid_spec = pltpu.PrefetchScalarGridSpec(
        num_scalar_prefetch=3,
        grid=(t // tt,),
        in_specs=[pl.BlockSpec(memory_space=pl.ANY), pl.BlockSpec((tt * QUAD, 128), lambda i, *_: (i, 0))],
        out_specs=pl.BlockSpec(memory_space=pl.ANY),
        scratch_shapes=[pltpu.SMEM((TOP_K * tt,), _I32), pltpu.SemaphoreType.DMA, pltpu.SemaphoreType.DMA,
                        pltpu.VMEM((QUAD * ROW_BLOCK // 2, 128), jnp.uint32), pltpu.SemaphoreType.DMA],
    )
    return pl.pallas_call(
        functools.partial(_dispatch_body, tt=tt),
        grid_spec=grid_spec,
        out_shape=jax.ShapeDtypeStruct((n_rows * QUAD, 128), jnp.uint32),
        compiler_params=_cparams("arbitrary"),
        name="dispatch",
    )(counts, pstarts, pcounts, pos_tiles, h1p)


def _ffn_body(blk0_ref, nblk_ref, ntot_ref, xs_hbm, wg_ref, wu_ref, wd_ref, ys_hbm,
              xbuf, ybuf, sem_in, sem_out, wgu_bf, wd_bf):
    e = pl.program_id(0)
    nblk = nblk_ref[e]
    blk0 = blk0_ref[e]
    ntot = ntot_ref[0]

    blk_rows = ROW_BLOCK * QUAD

    def rows(g):
        return pl.ds(pl.multiple_of(g * blk_rows, blk_rows), blk_rows)

    def in_start(g, slot):
        pltpu.make_async_copy(xs_hbm.at[rows(g)], xbuf.at[slot], sem_in.at[slot]).start()

    def in_wait(slot):
        pltpu.make_async_copy(xs_hbm.at[rows(0)], xbuf.at[slot], sem_in.at[slot]).wait()

    def out_start(g, slot):
        pltpu.make_async_copy(ybuf.at[slot], ys_hbm.at[rows(g)], sem_out.at[slot]).start()

    def out_wait(slot):
        pltpu.make_async_copy(ybuf.at[slot], ys_hbm.at[rows(0)], sem_out.at[slot]).wait()

    @pl.when(e == 0)
    def _():
        for i in range(IN_AHEAD):
            @pl.when(i < ntot)
            def _():
                in_start(i, i)

    @pl.when(nblk > 0)
    def _():
        wgu_bf[:, 0:EXPERT_FF] = wg_ref[0].astype(_BF16)
        wgu_bf[:, EXPERT_FF:] = wu_ref[0].astype(_BF16)
        wd_bf[...] = wd_ref[0].astype(_BF16)

        def block(j, carry):
            g = blk0 + j
            slot = g & (RING - 1)
            in_wait(slot)

            @pl.when(g + IN_AHEAD < ntot)
            def _():
                in_start(g + IN_AHEAD, (g + IN_AHEAD) & (RING - 1))

            @pl.when(g >= RING)
            def _():
                out_wait(slot)

            lo, hi = _unpack_halves(_load_rows(xbuf.at[slot], ROW_BLOCK))
            gu = (jnp.dot(lo.astype(_BF16), wgu_bf[0:HALF, :], preferred_element_type=_F32)
                  + jnp.dot(hi.astype(_BF16), wgu_bf[HALF:, :], preferred_element_type=_F32))
            h = (_silu(gu[:, :EXPERT_FF]) * gu[:, EXPERT_FF:]).astype(_BF16)
            _store_rows(ybuf.at[slot], _pack_halves(jnp.dot(h, wd_bf[...], preferred_element_type=_F32)))
            out_start(g, slot)
            return carry

        lax.fori_loop(0, nblk, block, 0)

    @pl.when(e == N_EXPERTS - 1)
    def _():
        for i in range(RING):
            @pl.when(i < ntot)
            def _():
                out_wait((ntot - 1 - i) & (RING - 1))


def _ffn(blk0, nblk, ntot, xs, wg, wu, wd):
    grid_spec = pltpu.PrefetchScalarGridSpec(
        num_scalar_prefetch=3,
        grid=(N_EXPERTS,),
        in_specs=[pl.BlockSpec(memory_space=pl.ANY),
                  pl.BlockSpec((1, D_MODEL, EXPERT_FF), lambda e, *_: (e, 0, 0)),
                  pl.BlockSpec((1, D_MODEL, EXPERT_FF), lambda e, *_: (e, 0, 0)),
                  pl.BlockSpec((1, EXPERT_FF, D_MODEL), lambda e, *_: (e, 0, 0))],
        out_specs=pl.BlockSpec(memory_space=pl.ANY),
        scratch_shapes=[pltpu.VMEM((RING, ROW_BLOCK * QUAD, 128), jnp.uint32),
                        pltpu.VMEM((RING, ROW_BLOCK * QUAD, 128), jnp.uint32),
                        pltpu.SemaphoreType.DMA((RING,)), pltpu.SemaphoreType.DMA((RING,)),
                        pltpu.VMEM((D_MODEL, 2 * EXPERT_FF), _BF16), pltpu.VMEM((EXPERT_FF, D_MODEL), _BF16)],
    )
    return pl.pallas_call(
        _ffn_body,
        grid_spec=grid_spec,
        out_shape=jax.ShapeDtypeStruct(xs.shape, jnp.uint32),
        compiler_params=_cparams("arbitrary"),
        name="ffn",
    )(blk0, nblk, ntot, xs, wg, wu, wd)


def _combine_body(pos_hbm, gate_ref, h1_ref, ys_hbm, wsg_ref, wsu_ref, wsd_ref, g_ref, b_ref,
                  out_ref, pos_smem, psem, ybuf, sem, *, tt):
    i = pl.program_id(0)
    cp = pltpu.make_async_copy(pos_hbm.at[i], pos_smem, psem)
    cp.start()
    cp.wait()

    def row_copy(t, k):
        return pltpu.make_async_copy(ys_hbm.at[pl.ds(QUAD * pos_smem[k * tt + t], QUAD)],
                                     ybuf.at[k, pl.ds(QUAD * t, QUAD)], sem)

    def issue(t, c):
        for k in range(TOP_K):
            row_copy(t, k).start(priority=k % 2)
        return c

    lax.fori_loop(0, tt, issue, 0, unroll=ISSUE_UNROLL)

    x = h1_ref[...]
    xb = x.astype(_BF16)
    shared = _mm(_silu(_mm(xb, wsg_ref[...])) * _mm(xb, wsu_ref[...]), wsd_ref[...])

    for k in range(TOP_K):
        pltpu.make_async_copy(ys_hbm.at[pl.ds(0, QUAD * tt)], ybuf.at[k], sem).wait()

    gcol = gate_ref[...].T
    acc_lo = jnp.zeros((tt, HALF), _F32)
    acc_hi = jnp.zeros((tt, HALF), _F32)
    for k in range(TOP_K):
        lo, hi = _unpack_halves(_load_rows(ybuf.at[k], tt))
        acc_lo = acc_lo + gcol[:, k:k + 1] * lo
        acc_hi = acc_hi + gcol[:, k:k + 1] * hi
    routed = jnp.concatenate([acc_lo, acc_hi], axis=1)
    out_ref[...] = _layer_norm(DN_ALPHA * x + (routed + shared), g_ref[...], b_ref[...])


def _combine(pos_tiles, gate, h1, ys, wsg, wsu, wsd, g, b, *, tt):
    t = h1.shape[0]
    full = lambda a: pl.BlockSpec(a.shape, lambda i: (0,) * a.ndim)
    return pl.pallas_call(
        functools.partial(_combine_body, tt=tt),
        grid=(t // tt,),
        in_specs=[pl.BlockSpec(memory_space=pl.ANY), pl.BlockSpec((TOP_K, tt), lambda i: (0, i)),
                  pl.BlockSpec((tt, D_MODEL), lambda i: (i, 0)), pl.BlockSpec(memory_space=pl.ANY),
                  full(wsg), full(wsu), full(wsd), full(g), full(b)],
        out_specs=pl.BlockSpec((tt, D_MODEL), lambda i: (i, 0)),
        out_shape=jax.ShapeDtypeStruct((t, D_MODEL), _F32),
        scratch_shapes=[pltpu.SMEM((TOP_K * tt,), _I32), pltpu.SemaphoreType.DMA,
                        pltpu.VMEM((TOP_K, tt * QUAD, 128), jnp.uint32), pltpu.SemaphoreType.DMA],
        compiler_params=_cparams("arbitrary"),
        name="combine",
    )(pos_tiles, gate, h1, ys, wsg, wsu, wsd, g, b)


def _sc_gather(table, idx):
    b = idx.shape[0]
    per_w = b // SC_WORKERS
    assert per_w * SC_WORKERS == b and per_w % SC_CHUNK == 0
    row = table.shape[1:]
    mesh = plsc.VectorSubcoreMesh(core_axis_name="c", subcore_axis_name="s",
                                  num_cores=SC_CORES, num_subcores=SC_SUBCORES)

    @functools.partial(
        pl.kernel, mesh=mesh,
        out_type=jax.ShapeDtypeStruct((b,) + row, table.dtype),
        scratch_types=[pltpu.VMEM((SC_CHUNK,), _I32), pltpu.VMEM((SC_CHUNK,) + row, table.dtype),
                       pltpu.SemaphoreType.DMA],
        name="sc_gather",
    )
    def gather(table_hbm, idx_hbm, out_hbm, idx_v, rows_v, sem):
        wid = lax.axis_index("s") * SC_CORES + lax.axis_index("c")
        base = wid * per_w

        @pl.loop(0, per_w // SC_CHUNK)
        def _(i):
            off = pl.multiple_of(base + i * SC_CHUNK, SC_CHUNK)
            pltpu.sync_copy(idx_hbm.at[pl.ds(off, SC_CHUNK)], idx_v)
            pltpu.async_copy(table_hbm.at[idx_v], rows_v, sem).wait()
            pltpu.sync_copy(rows_v, out_hbm.at[pl.ds(off, SC_CHUNK)])

    return gather(table, idx)


def _sc_scatter(rows, pos3, n_out):
    nchunk, nk, w = pos3.shape
    per_w = nchunk // SC_WORKERS
    assert per_w * SC_WORKERS == nchunk and w <= 128 and rows.shape[0] == nchunk * w
    row = rows.shape[1:]
    mesh = plsc.VectorSubcoreMesh(core_axis_name="c", subcore_axis_name="s",
                                  num_cores=SC_CORES, num_subcores=SC_SUBCORES)

    @functools.partial(
        pl.kernel, mesh=mesh,
        out_type=jax.ShapeDtypeStruct((n_out,) + row, rows.dtype),
        scratch_types=[pltpu.VMEM((nk, w), _I32), pltpu.VMEM((w,) + row, rows.dtype), pltpu.SemaphoreType.DMA],
        name="sc_scatter",
    )
    def scatter(rows_hbm, pos_hbm, out_hbm, idx_v, rows_v, sem):
        wid = lax.axis_index("s") * SC_CORES + lax.axis_index("c")

        @pl.loop(0, per_w)
        def _(i):
            c = wid * per_w + i
            pltpu.sync_copy(pos_hbm.at[c], idx_v)
            pltpu.sync_copy(rows_hbm.at[pl.ds(pl.multiple_of(c * w, w), w)], rows_v)
            copies = [pltpu.async_copy(rows_v, out_hbm.at[idx_v.at[k]], sem) for k in range(nk)]
            for cp in copies:
                cp.wait()

    return scatter(rows, pos3)


def _padfill_body(cnt_ref, pst_ref, pcn_ref, xs_in, xs_out, zbuf, zsem):
    del xs_in
    zbuf[...] = jnp.zeros_like(zbuf)

    def pad_runs(e, act):
        pad = pcn_ref[e] - cnt_ref[e]
        base = pst_ref[e] + cnt_ref[e]
        for b in range(ROW_BLOCK.bit_length() - 1):
            n = 1 << b

            @pl.when(((pad >> b) & 1) == 1)
            def _():
                off = base + (pad & (n - 1))
                act(pltpu.make_async_copy(zbuf.at[pl.ds(0, QUAD * n)],
                                          xs_out.at[pl.ds(QUAD * off, QUAD * n)], zsem))

    def start_all(e, c):
        pad_runs(e, lambda d: d.start())
        return c

    def wait_all(e, c):
        pad_runs(e, lambda d: d.wait())
        return c

    lax.fori_loop(0, N_EXPERTS, start_all, 0)
    lax.fori_loop(0, N_EXPERTS, wait_all, 0)


def _padfill(counts, pstarts, pcounts, xs):
    grid_spec = pltpu.PrefetchScalarGridSpec(
        num_scalar_prefetch=3,
        grid=(1,),
        in_specs=[pl.BlockSpec(memory_space=pl.ANY)],
        out_specs=pl.BlockSpec(memory_space=pl.ANY),
        scratch_shapes=[pltpu.VMEM((QUAD * ROW_BLOCK // 2, 128), jnp.uint32), pltpu.SemaphoreType.DMA],
    )
    return pl.pallas_call(
        _padfill_body,
        grid_spec=grid_spec,
        out_shape=jax.ShapeDtypeStruct(xs.shape, xs.dtype),
        input_output_aliases={3: 0},
        compiler_params=_cparams("arbitrary"),
        name="padfill",
    )(counts, pstarts, pcounts, xs)


def _combine_stream_body(gate_ref, h1_ref, yg_ref, wsg_ref, wsu_ref, wsd_ref, g_ref, b_ref, out_ref, *, tt):
    x = h1_ref[...]
    xb = x.astype(_BF16)
    shared = _mm(_silu(_mm(xb, wsg_ref[...])) * _mm(xb, wsu_ref[...]), wsd_ref[...])
    gcol = gate_ref[...].T
    acc_lo = jnp.zeros((tt, HALF), _F32)
    acc_hi = jnp.zeros((tt, HALF), _F32)
    for k in range(TOP_K):
        lo, hi = _unpack_halves(_load_rows(yg_ref.at[0, k], tt))
        acc_lo = acc_lo + gcol[:, k:k + 1] * lo
        acc_hi = acc_hi + gcol[:, k:k + 1] * hi
    routed = jnp.concatenate([acc_lo, acc_hi], axis=1)
    out_ref[...] = _layer_norm(DN_ALPHA * x + (routed + shared), g_ref[...], b_ref[...])


def _combine_stream(gate, h1, yg, wsg, wsu, wsd, g, b, *, tt):
    t = h1.shape[0]
    full = lambda a: pl.BlockSpec(a.shape, lambda i: (0,) * a.ndim)
    return pl.pallas_call(
        functools.partial(_combine_stream_body, tt=tt),
        grid=(t // tt,),
        in_specs=[pl.BlockSpec((TOP_K, tt), lambda i: (0, i)), pl.BlockSpec((tt, D_MODEL), lambda i: (i, 0)),
                  pl.BlockSpec((1, TOP_K, tt * QUAD, 128), lambda i: (i, 0, 0, 0)),
                  full(wsg), full(wsu), full(wsd), full(g), full(b)],
        out_specs=pl.BlockSpec((tt, D_MODEL), lambda i: (i, 0)),
        out_shape=jax.ShapeDtypeStruct((t, D_MODEL), _F32),
        compiler_params=_cparams("arbitrary"),
        name="combine",
    )(gate, h1, yg, wsg, wsu, wsd, g, b)


def _pick(n, pref):
    t = min(n, pref)
    while n % t:
        t -= CHUNK
    return t


def _mixer(x, tails, s0, wts, gnw, *, lt, lg, nbb):
    yc, q, k, v, z, bgc, bgr, tails_out = _premix(x, tails, wts, lt=lt)
    bsz, seq, _ = x.shape
    nch = seq // CHUNK
    grow = bgr[:, GDN_HEADS:2 * GDN_HEADS, :].reshape(bsz, GDN_HEADS, nch, CHUNK)
    grow = grow.transpose(0, 2, 1, 3).reshape(bsz, nch, 1, STACK)
    yg, s_out = _gdn(q, k, v, z, bgc, grow, s0, gnw, lg=lg, nbb=nbb)
    return yc, yg, tails_out, s_out


def kernel(x, meta_tokens, w_in, conv_w, conv_norm_w, gdn_conv_w, a_log, dt_bias, gdn_norm_w, w_out,
           ln1_g, ln1_b, w_router, b_router, w_gate, w_up, w_down, ws_gate, ws_up, ws_down, ln2_g, ln2_b):
    assert w_in.shape[0] == 1, "single-layer stack"
    bsz, seq, d = x.shape
    assert d == D_MODEL and seq % CHUNK == 0
    c, gw = CONV_WIDTH, GDN_WIDTH
    win = w_in[0].astype(_BF16)
    wbd = win[:, 3 * c + 4 * gw:]
    zpad = jnp.zeros((128 - 2 * GDN_HEADS,), _F32)
    zpad4 = jnp.zeros((GDN_HEADS,), _F32)
    prow = jnp.zeros((8, 128), _F32)
    prow = prow.at[0].set(jnp.concatenate([zpad4, a_log[0], zpad]))
    prow = prow.at[1].set(jnp.concatenate([zpad4, dt_bias[0], zpad]))
    wts = (win[:, :3 * c], win[:, 3 * c:3 * c + 3 * gw], win[:, 3 * c + 3 * gw:3 * c + 4 * gw],
           jnp.pad(wbd, ((0, 0), (0, 128 - 2 * GDN_HEADS))), wbd.T,
           conv_w[0], conv_norm_w, gdn_conv_w[0], prow, prow.T[:8])
    gnw = gdn_norm_w

    meta = jnp.concatenate([jnp.zeros((CHUNK - N_META, d), x.dtype), meta_tokens.astype(x.dtype)])[None]
    tails0 = jnp.zeros((HIST, c + 3 * gw), _F32)
    s00 = jnp.zeros((GDN_HEADS, GDN_HEAD_DIM, GDN_HEAD_DIM), _F32)
    _, _, tails_m, s_m = _mixer(meta, tails0, s00, wts, gnw, lt=CHUNK, lg=CHUNK, nbb=1)

    yc, yg, _, _ = _mixer(x, tails_m[0], s_m[0], wts, gnw, lt=_pick(seq, 512), lg=_pick(seq, 512),
                          nbb=GDN_ROWS if bsz % GDN_ROWS == 0 else 1)

    t = bsz * seq
    tm = _pick(t, 512)
    h1, h1p = _outproj(yc.reshape(t, c), yg.reshape(t, gw), x.reshape(t, d), w_out[0].astype(_BF16),
                       ln1_g, ln1_b, tm=tm)

    tt = _pick(t, 256)
    wr_t = w_router[0].T
    wr_hi = wr_t.astype(_BF16)
    wr_lo = (wr_t - wr_hi.astype(_F32)).astype(_BF16)
    idx, gate, rank, cnt = _router(h1, wr_hi, wr_lo, b_router[0][:, None], tt=tt)

    counts = cnt[:, 0].astype(_I32)
    pcounts = (counts + ROW_BLOCK - 1) // ROW_BLOCK * ROW_BLOCK
    pends = jnp.cumsum(pcounts)
    pstarts = pends - pcounts
    nb = t * TOP_K // ROW_BLOCK + N_EXPERTS

    ts = SC_WINDOW
    pos3 = _position(idx, rank, pstarts[:, None].astype(_I32), tt=ts)
    xs = _sc_scatter(h1p.reshape(t, QUAD, 128), pos3, nb * ROW_BLOCK)
    xs = _padfill(counts, pstarts.astype(_I32), pcounts.astype(_I32), xs.reshape(nb * ROW_BLOCK * QUAD, 128))
    ys = _ffn((pstarts // ROW_BLOCK).astype(_I32), (pcounts // ROW_BLOCK).astype(_I32),
              (pends[-1:] // ROW_BLOCK).astype(_I32), xs, w_gate[0], w_up[0], w_down[0])
    yg = _sc_gather(ys.reshape(nb * ROW_BLOCK, QUAD, 128), pos3.reshape(t * TOP_K))
    yg = yg.reshape(t // ts, TOP_K, ts * QUAD, 128)
    out = _combine_stream(gate, h1, yg, ws_gate[0].astype(_BF16), ws_up[0].astype(_BF16),
                          ws_down[0].astype(_BF16), ln2_g, ln2_b, tt=ts)
    return out.reshape(bsz, seq, d)
```

```python
import functools

import jax
import jax.numpy as jnp
from jax import lax
from jax.experimental import pallas as pl
from jax.experimental.pallas import tpu as pltpu
from jax.experimental.pallas import tpu_sc as plsc

_F32 = jnp.float32
_BF16 = jnp.bfloat16
_I32 = jnp.int32

D_MODEL = 1024
N_META = 16
CONV_WIDTH = 512
CONV_K = 3
GDN_HEADS = 4
GDN_HEAD_DIM = 128
GDN_WIDTH = GDN_HEADS * GDN_HEAD_DIM
GDN_CONV_K = 4
CHUNK = 64
N_EXPERTS = 256
TOP_K = 8
N_GROUPS = 8
TOPK_GROUPS = 4
E_PER_GROUP = N_EXPERTS // N_GROUPS
EXPERT_FF = 256
ROUTED_SCALE = 2.5
ROW_BLOCK = 256
DN_ALPHA = 2.0 ** 0.25
NORM_EPS = 1e-5
HALF = D_MODEL // 2
QUAD = HALF // 128
STACK = GDN_HEADS * CHUNK
HIST = 8
GDN_ROWS = 4
ISSUE_UNROLL = 8
SC_CORES = 2
SC_SUBCORES = 16
SC_WORKERS = SC_CORES * SC_SUBCORES
SC_CHUNK = 64
SC_RING = 2
SC_WINDOW = 128
RING = 4
IN_AHEAD = RING - 1

V7X_VMEM_BYTES = 64 * 1024 * 1024
VMEM_LIMIT = V7X_VMEM_BYTES - 8 * 1024 * 1024


def _cparams(*sem):
    return pltpu.CompilerParams(dimension_semantics=sem, vmem_limit_bytes=VMEM_LIMIT)


def _mm(a, b):
    return jnp.dot(a.astype(_BF16), b.astype(_BF16), preferred_element_type=_F32)


def _mm_nt(a, b):
    return lax.dot_general(a.astype(_BF16), b.astype(_BF16), (((1,), (1,)), ((), ())),
                           preferred_element_type=_F32)


def _mm_tn(a, b):
    return lax.dot_general(a.astype(_BF16), b.astype(_BF16), (((0,), (0,)), ((), ())),
                           preferred_element_type=_F32)


def _sigmoid(x):
    return 1.0 / (1.0 + jnp.exp(-x))


def _silu(x):
    return x * _sigmoid(x)


def _softplus(x):
    return jnp.maximum(x, 0.0) + jnp.log1p(jnp.exp(-jnp.abs(x)))


def _pack_halves(y):
    return pltpu.pack_elementwise([y[:, :HALF], y[:, HALF:]], packed_dtype=_BF16)


def _store_rows(ref, packed):
    r = packed.shape[0]
    for c in range(QUAD):
        ref[pl.ds(c, r, stride=QUAD), :] = packed[:, c * 128:(c + 1) * 128]


def _load_rows(ref, r):
    return jnp.concatenate([ref[pl.ds(c, r, stride=QUAD), :] for c in range(QUAD)], axis=1)


def _unpack_halves(p):
    lo = pltpu.unpack_elementwise(p, index=0, packed_dtype=_BF16, unpacked_dtype=_F32)
    hi = pltpu.unpack_elementwise(p, index=1, packed_dtype=_BF16, unpacked_dtype=_F32)
    return lo, hi


def _layer_norm(h, g, b):
    mu = jnp.mean(h, axis=-1, keepdims=True)
    d = h - mu
    var = jnp.mean(d * d, axis=-1, keepdims=True)
    return d * lax.rsqrt(var + NORM_EPS) * g + b


def _premix_body(x_ref, tails_ref, wa_ref, wq_ref, wz_ref, wbd_ref, wbdt_ref, cw_ref, cnw_ref,
                 gcw_ref, prow_ref, pcol_ref,
                 yc_ref, q_ref, k_ref, v_ref, z_ref, bgc_ref, bgr_ref, tout_ref, ext_ref, *, lt):
    cw_ = CONV_WIDTH

    @pl.when(pl.program_id(1) == 0)
    def _():
        ext_ref[0:HIST, :] = tails_ref[...]

    xb = x_ref[0].astype(_BF16)
    pa = jnp.dot(xb, wa_ref[...], preferred_element_type=_F32)
    gate_b = pa[:, 0:cw_]
    u = pa[:, cw_:2 * cw_] * pa[:, 2 * cw_:3 * cw_]
    ext_ref[HIST:HIST + lt, 0:cw_] = u
    pq = jnp.dot(xb, wq_ref[...], preferred_element_type=_F32)
    ext_ref[HIST:HIST + lt, cw_:] = pq

    cw = cw_ref[...]
    ca = u * cw[CONV_K - 1:CONV_K, :]
    for j in range(CONV_K - 1):
        ca = ca + ext_ref[pl.ds(HIST - (CONV_K - 1) + j, lt), 0:cw_] * cw[j:j + 1, :]
    yc = gate_b * ca
    ms = jnp.mean(yc * yc, axis=-1, keepdims=True)
    yc_ref[0] = (yc * lax.rsqrt(ms + NORM_EPS) * cnw_ref[...]).astype(_BF16)

    gcw = gcw_ref[...]
    cq = pq * gcw[GDN_CONV_K - 1:GDN_CONV_K, :]
    for j in range(GDN_CONV_K - 1):
        cq = cq + ext_ref[pl.ds(HIST - (GDN_CONV_K - 1) + j, lt), cw_:] * gcw[j:j + 1, :]
    s = _silu(cq)
    for h in range(GDN_HEADS):
        lo, hi = h * GDN_HEAD_DIM, (h + 1) * GDN_HEAD_DIM
        qh = s[:, lo:hi]
        kh = s[:, GDN_WIDTH + lo:GDN_WIDTH + hi]
        qn = qh * lax.rsqrt(jnp.sum(qh * qh, axis=-1, keepdims=True) + 1e-6)
        kn = kh * lax.rsqrt(jnp.sum(kh * kh, axis=-1, keepdims=True) + 1e-6)
        q_ref[0, :, lo:hi] = (qn * (GDN_HEAD_DIM ** -0.5)).astype(_BF16)
        k_ref[0, :, lo:hi] = kn.astype(_BF16)
    v_ref[0] = s[:, 2 * GDN_WIDTH:].astype(_BF16)
    z_ref[0] = jnp.dot(xb, wz_ref[...], preferred_element_type=_F32).astype(_BF16)

    bdc = jnp.dot(xb, wbd_ref[...], preferred_element_type=_F32)
    prow = prow_ref[...]
    g_c = -jnp.exp(prow[0:1, :]) * _softplus(bdc + prow[1:2, :])
    lane = lax.broadcasted_iota(_I32, bdc.shape, 1)
    bgc_ref[0] = jnp.where(lane < GDN_HEADS, _sigmoid(bdc), g_c)
    bdr = _mm_nt(wbdt_ref[...], xb)
    pcol = pcol_ref[...]
    g_r = -jnp.exp(pcol[:, 0:1]) * _softplus(bdr + pcol[:, 1:2])
    row = lax.broadcasted_iota(_I32, bdr.shape, 0)
    bgr_ref[0] = jnp.where(row < GDN_HEADS, _sigmoid(bdr), g_r)

    tail = ext_ref[lt:lt + HIST, :]
    ext_ref[0:HIST, :] = tail
    tout_ref[0] = tail


def _premix(x, tails, wts, *, lt):
    bsz, seq, d = x.shape
    assert seq % lt == 0
    grid = (bsz, seq // lt)
    full = lambda a: pl.BlockSpec(a.shape, lambda b, j: (0,) * a.ndim)
    tok = lambda w: pl.BlockSpec((1, lt, w), lambda b, j: (b, j, 0))
    (wa, wq, wz, wbd, wbdt, cw, cnw, gcw, prow, pcol) = wts
    ext_w = CONV_WIDTH + 3 * GDN_WIDTH
    out_shape = (
        jax.ShapeDtypeStruct((bsz, seq, CONV_WIDTH), _BF16),
        jax.ShapeDtypeStruct((bsz, seq, GDN_WIDTH), _BF16),
        jax.ShapeDtypeStruct((bsz, seq, GDN_WIDTH), _BF16),
        jax.ShapeDtypeStruct((bsz, seq, GDN_WIDTH), _BF16),
        jax.ShapeDtypeStruct((bsz, seq, GDN_WIDTH), _BF16),
        jax.ShapeDtypeStruct((bsz, seq, 128), _F32),
        jax.ShapeDtypeStruct((bsz, 8, seq), _F32),
        jax.ShapeDtypeStruct((bsz, HIST, ext_w), _F32),
    )
    out_specs = (tok(CONV_WIDTH), tok(GDN_WIDTH), tok(GDN_WIDTH), tok(GDN_WIDTH), tok(GDN_WIDTH),
                 tok(128), pl.BlockSpec((1, 8, lt), lambda b, j: (b, 0, j)),
                 pl.BlockSpec((1, HIST, ext_w), lambda b, j: (b, 0, 0)))
    return pl.pallas_call(
        functools.partial(_premix_body, lt=lt),
        grid=grid,
        in_specs=[tok(d), full(tails)] + [full(w) for w in wts],
        out_specs=out_specs,
        out_shape=out_shape,
        scratch_shapes=[pltpu.VMEM((HIST + lt, ext_w), _F32)],
        compiler_params=_cparams("arbitrary", "arbitrary"),
        name="premix",
    )(x, tails, *wts)


def _cumsum_rows(x):
    row = lax.broadcasted_iota(_I32, x.shape, 0)
    s = 1
    while s < x.shape[0]:
        x = x + jnp.where(row >= s, pltpu.roll(x, s, 0), 0.0)
        s *= 2
    return x


def _cumsum_lanes_seg(x):
    lane = lax.broadcasted_iota(_I32, x.shape, 1) & (CHUNK - 1)
    s = 1
    while s < CHUNK:
        x = x + jnp.where(lane >= s, pltpu.roll(x, s, 1), 0.0)
        s *= 2
    return x


def _stack_heads(a):
    return jnp.concatenate([a[:, h * GDN_HEAD_DIM:(h + 1) * GDN_HEAD_DIM] for h in range(GDN_HEADS)], axis=0)


def _gdn_body(q_ref, k_ref, v_ref, z_ref, bgc_ref, grow_ref, s0_ref, gnw_ref,
              y_ref, sout_ref, s_ref, *, nc, nbb):
    @pl.when(pl.program_id(1) == 0)
    def _():
        for r in range(nbb):
            s_ref[r] = s0_ref[...]

    ri = lax.broadcasted_iota(_I32, (STACK, STACK), 0)
    ci = lax.broadcasted_iota(_I32, (STACK, STACK), 1)
    same64 = (ri >> 6) == (ci >> 6)
    same32 = (ri >> 5) == (ci >> 5)
    same16 = (ri >> 4) == (ci >> 4)
    low_incl = same64 & (ri >= ci)
    low_strict = same64 & (ri > ci)
    gnw = gnw_ref[...]

    def chunk_row(r, c):
        off = pl.multiple_of(c * CHUNK, CHUNK)
        q_all = _stack_heads(q_ref[r, pl.ds(off, CHUNK), :].astype(_F32))
        k_all = _stack_heads(k_ref[r, pl.ds(off, CHUNK), :].astype(_F32))
        v_all = _stack_heads(v_ref[r, pl.ds(off, CHUNK), :].astype(_F32))
        bgc = bgc_ref[r, pl.ds(off, CHUNK), :]
        gcs = _cumsum_rows(bgc)
        hd = (CHUNK, GDN_HEAD_DIM)
        beta_b = jnp.concatenate(
            [jnp.broadcast_to(bgc[:, h:h + 1], hd) for h in range(GDN_HEADS)], axis=0)
        gc_b = jnp.concatenate(
            [jnp.broadcast_to(gcs[:, GDN_HEADS + h:GDN_HEADS + h + 1], hd) for h in range(GDN_HEADS)], axis=0)
        gl = [gcs[CHUNK - 1:CHUNK, GDN_HEADS + h:GDN_HEADS + h + 1] for h in range(GDN_HEADS)]
        gl_b = jnp.concatenate([jnp.broadcast_to(g1, hd) for g1 in gl], axis=0)
        gcr = _cumsum_lanes_seg(jnp.broadcast_to(grow_ref[r, c], (8, STACK)))[0:1, :]

        diff = jnp.concatenate([gc_b, gc_b], axis=1) - gcr
        decay = jnp.exp(jnp.where(low_incl, diff, -1e30))
        kb = k_all * beta_b
        a1 = _mm_nt(jnp.concatenate([kb, q_all], axis=0), k_all)
        yield
        m = jnp.where(low_strict, a1[:STACK] * decay, 0.0)
        attn = a1[STACK:] * decay

        l16 = jnp.where(same16, m, 0.0)
        c1 = jnp.where(same32 & jnp.logical_not(same16), m, 0.0)
        c2 = jnp.where(same32, 0.0, m)
        p2 = _mm(l16, l16)
        yield
        p4 = _mm(p2, p2)
        t = _mm(l16, p2)
        yield
        na = p2 - l16 - t
        p8 = _mm(p4, p4)
        t = _mm(na, p4)
        yield
        nb = na + p4 + t
        t = _mm(nb, p8)
        yield
        ncm = nb + p8 + t
        t = _mm(c1, ncm)
        yield
        y1 = c1 + t
        t = _mm(ncm, y1)
        yield
        n1 = ncm - y1 - t
        t = _mm(c2, n1)
        yield
        y2 = c2 + t
        t = _mm(n1, y2)
        yield
        nt = n1 - y2 - t

        egc = jnp.exp(gc_b)
        rhs = jnp.concatenate([v_all * beta_b, kb * egc], axis=1)
        t = _mm(nt, rhs)
        yield
        uw = rhs + t
        u_all = uw[:, :GDN_HEAD_DIM]
        w_all = uw[:, GDN_HEAD_DIM:]
        qd = q_all * egc
        kd = k_all * jnp.exp(gl_b - gc_b)

        bs = []
        for h in range(GDN_HEADS):
            r0, r1 = h * CHUNK, (h + 1) * CHUNK
            bs.append(_mm(jnp.concatenate([w_all[r0:r1], qd[r0:r1]], axis=0), s_ref[r, h]))
        yield
        vn = [u_all[h * CHUNK:(h + 1) * CHUNK] - bs[h][:CHUNK] for h in range(GDN_HEADS)]
        vn_all = jnp.concatenate(vn, axis=0)
        t = _mm(attn, vn_all)
        ds = [_mm_tn(kd[h * CHUNK:(h + 1) * CHUNK], vn[h]) for h in range(GDN_HEADS)]
        yield
        o_all = jnp.concatenate([b[CHUNK:] for b in bs], axis=0) + t
        for h in range(GDN_HEADS):
            r0, r1 = h * CHUNK, (h + 1) * CHUNK
            s_ref[r, h] = s_ref[r, h] * jnp.exp(gl[h]) + ds[h]
            o = o_all[r0:r1]
            zz = z_ref[r, pl.ds(off, CHUNK), h * GDN_HEAD_DIM:(h + 1) * GDN_HEAD_DIM].astype(_F32)
            on = o * lax.rsqrt(jnp.mean(o * o, axis=-1, keepdims=True) + NORM_EPS) * gnw
            y_ref[r, pl.ds(off, CHUNK), h * GDN_HEAD_DIM:(h + 1) * GDN_HEAD_DIM] = (on * _silu(zz)).astype(_BF16)

    def chunk(c, carry):
        live = [chunk_row(r, c) for r in range(nbb)]
        while live:
            live = [g for g in live if next(g, live) is not live]
        return carry

    lax.fori_loop(0, nc, chunk, 0)
    sout_ref[...] = s_ref[...]


def _gdn(q, k, v, z, bgc, grow, s0, gnw, *, lg, nbb):
    bsz, seq, _ = q.shape
    assert seq % lg == 0 and lg % CHUNK == 0 and bsz % nbb == 0
    nc = lg // CHUNK
    tok = lambda w: pl.BlockSpec((nbb, lg, w), lambda b, j: (b, j, 0))
    full = lambda a: pl.BlockSpec(a.shape, lambda b, j: (0,) * a.ndim)
    st = (nbb, GDN_HEADS, GDN_HEAD_DIM, GDN_HEAD_DIM)
    return pl.pallas_call(
        functools.partial(_gdn_body, nc=nc, nbb=nbb),
        grid=(bsz // nbb, seq // lg),
        in_specs=[tok(GDN_WIDTH)] * 4 + [tok(128), pl.BlockSpec((nbb, nc, 1, STACK), lambda b, j: (b, j, 0, 0)),
                                           full(s0), full(gnw)],
        out_specs=(tok(GDN_WIDTH), pl.BlockSpec(st, lambda b, j: (b, 0, 0, 0))),
        out_shape=(jax.ShapeDtypeStruct((bsz, seq, GDN_WIDTH), _BF16),
                   jax.ShapeDtypeStruct((bsz, GDN_HEADS, GDN_HEAD_DIM, GDN_HEAD_DIM), _F32)),
        scratch_shapes=[pltpu.VMEM(st, _F32)],
        compiler_params=_cparams("arbitrary", "arbitrary"),
        name="gdn",
    )(q, k, v, z, bgc, grow, s0, gnw)


def _outproj_body(yc_ref, yg_ref, x_ref, wo_ref, g_ref, b_ref, h1_ref, h1p_ref):
    mix = (jnp.dot(yc_ref[...], wo_ref[0:CONV_WIDTH, :], preferred_element_type=_F32)
           + jnp.dot(yg_ref[...], wo_ref[CONV_WIDTH:, :], preferred_element_type=_F32))
    h1 = _layer_norm(DN_ALPHA * x_ref[...] + mix, g_ref[...], b_ref[...])
    h1_ref[...] = h1
    _store_rows(h1p_ref, _pack_halves(h1))


def _outproj(yc, yg, x2d, wo, g, b, *, tm):
    t = x2d.shape[0]
    assert t % tm == 0
    row = lambda w: pl.BlockSpec((tm, w), lambda i: (i, 0))
    full = lambda a: pl.BlockSpec(a.shape, lambda i: (0,) * a.ndim)
    return pl.pallas_call(
        _outproj_body,
        grid=(t // tm,),
        in_specs=[row(CONV_WIDTH), row(GDN_WIDTH), row(D_MODEL), full(wo), full(g), full(b)],
        out_specs=(row(D_MODEL), pl.BlockSpec((tm * QUAD, 128), lambda i: (i, 0))),
        out_shape=(jax.ShapeDtypeStruct((t, D_MODEL), _F32), jax.ShapeDtypeStruct((t * QUAD, 128), jnp.uint32)),
        compiler_params=_cparams("arbitrary"),
        name="outproj",
    )(yc, yg, x2d, wo, g, b)


def _router_body(h1_ref, wh_ref, wl_ref, br_ref, idx_ref, gate_ref, rank_ref, cnt_ref, carry_ref, *, tt):
    @pl.when(pl.program_id(0) == 0)
    def _():
        carry_ref[...] = jnp.zeros_like(carry_ref)

    x = h1_ref[...]
    xh = x.astype(_BF16)
    xl = (x - xh.astype(_F32)).astype(_BF16)
    wh = wh_ref[...]
    logits = _mm_nt(wh, xh) + _mm_nt(wh, xl) + _mm_nt(wl_ref[...], xh)
    scores = _sigmoid(logits)
    sel = scores + br_ref[...]
    ninf = -jnp.inf

    r32 = lax.broadcasted_iota(_I32, (E_PER_GROUP, tt), 0)
    gsc = []
    for g in range(N_GROUPS):
        xg = sel[g * E_PER_GROUP:(g + 1) * E_PER_GROUP]
        m1 = jnp.max(xg, axis=0, keepdims=True)
        i1 = jnp.min(jnp.where(xg == m1, r32, E_PER_GROUP), axis=0, keepdims=True)
        m2 = jnp.max(jnp.where(r32 == i1, ninf, xg), axis=0, keepdims=True)
        gsc.append(m1 + m2)
    work = jnp.concatenate(gsc, axis=0)
    r8 = lax.broadcasted_iota(_I32, (N_GROUPS, tt), 0)
    gkeep = jnp.zeros((N_GROUPS, tt), _F32)
    for _ in range(TOPK_GROUPS):
        m = jnp.max(work, axis=0, keepdims=True)
        gi = jnp.min(jnp.where(work == m, r8, N_GROUPS), axis=0, keepdims=True)
        pick = r8 == gi
        gkeep = jnp.where(pick, 1.0, gkeep)
        work = jnp.where(pick, ninf, work)
    selm = jnp.concatenate(
        [jnp.where(gkeep[g:g + 1] > 0.5, sel[g * E_PER_GROUP:(g + 1) * E_PER_GROUP], ninf)
         for g in range(N_GROUPS)], axis=0)

    re = lax.broadcasted_iota(_I32, (N_EXPERTS, tt), 0)
    msel = jnp.zeros((N_EXPERTS, tt), _F32)
    idxs, gates = [], []
    for _ in range(TOP_K):
        m = jnp.max(selm, axis=0, keepdims=True)
        ii = jnp.min(jnp.where(selm == m, re, N_EXPERTS), axis=0, keepdims=True)
        hit = re == ii
        idxs.append(ii)
        gates.append(jnp.sum(jnp.where(hit, scores, 0.0), axis=0, keepdims=True))
        selm = jnp.where(hit, ninf, selm)
        msel = jnp.where(hit, 1.0, msel)
    gate = jnp.concatenate(gates, axis=0)
    gate_ref[...] = gate / jnp.sum(gate, axis=0, keepdims=True) * ROUTED_SCALE
    idx_ref[...] = jnp.concatenate(idxs, axis=0)

    ta = lax.broadcasted_iota(_I32, (tt, tt), 0)
    tb = lax.broadcasted_iota(_I32, (tt, tt), 1)
    earlier = jnp.where(ta < tb, 1.0, 0.0)
    carry = carry_ref[...]
    rank_all = _mm(msel, earlier) + carry[:, 0:1]
    rank_ref[...] = jnp.concatenate(
        [jnp.sum(jnp.where(re == ii, rank_all, 0.0), axis=0, keepdims=True) for ii in idxs],
        axis=0).astype(_I32)
    carry = carry + jnp.sum(msel, axis=1, keepdims=True)
    carry_ref[...] = carry
    cnt_ref[...] = carry


def _router(h1, wh, wl, br, *, tt):
    t = h1.shape[0]
    assert t % tt == 0
    full = lambda a: pl.BlockSpec(a.shape, lambda i: (0,) * a.ndim)
    kt = pl.BlockSpec((TOP_K, tt), lambda i: (0, i))
    return pl.pallas_call(
        functools.partial(_router_body, tt=tt),
        grid=(t // tt,),
        in_specs=[pl.BlockSpec((tt, D_MODEL), lambda i: (i, 0)), full(wh), full(wl), full(br)],
        out_specs=(kt, kt, kt, pl.BlockSpec((N_EXPERTS, 128), lambda i: (0, 0))),
        out_shape=(jax.ShapeDtypeStruct((TOP_K, t), _I32), jax.ShapeDtypeStruct((TOP_K, t), _F32),
                   jax.ShapeDtypeStruct((TOP_K, t), _I32), jax.ShapeDtypeStruct((N_EXPERTS, 128), _F32)),
        scratch_shapes=[pltpu.VMEM((N_EXPERTS, 128), _F32)],
        compiler_params=_cparams("arbitrary"),
        name="router",
    )(h1, wh, wl, br)


def _position_body(idx_ref, rank_ref, pstart_ref, pos_ref, *, tt):
    re = lax.broadcasted_iota(_I32, (N_EXPERTS, tt), 0)
    ps = pstart_ref[...]
    idx = idx_ref[...]
    rows = [jnp.sum(jnp.where(re == idx[k:k + 1], ps, 0), axis=0, keepdims=True) for k in range(TOP_K)]
    pos_ref[0] = jnp.concatenate(rows, axis=0) + rank_ref[...]


def _position(idx, rank, pstart, *, tt):
    t = idx.shape[1]
    kt = pl.BlockSpec((TOP_K, tt), lambda i: (0, i))
    return pl.pallas_call(
        functools.partial(_position_body, tt=tt),
        grid=(t // tt,),
        in_specs=[kt, kt, pl.BlockSpec(pstart.shape, lambda i: (0, 0))],
        out_specs=pl.BlockSpec((1, TOP_K, tt), lambda i: (i, 0, 0)),
        out_shape=jax.ShapeDtypeStruct((t // tt, TOP_K, tt), _I32),
        compiler_params=_cparams("arbitrary"),
        name="position",
    )(idx, rank, pstart)


def _dispatch_body(cnt_ref, pst_ref, pcn_ref, pos_hbm, h1p_ref, xs_out, pos_smem, psem, sem, zbuf, zsem, *, tt):
    i = pl.program_id(0)
    cp = pltpu.make_async_copy(pos_hbm.at[i], pos_smem, psem)
    cp.start()

    @pl.when(i == 0)
    def _():
        zbuf[...] = jnp.zeros_like(zbuf)

        def pad_runs(e, act):
            pad = pcn_ref[e] - cnt_ref[e]
            base = pst_ref[e] + cnt_ref[e]
            for b in range(ROW_BLOCK.bit_length() - 1):
                n = 1 << b

                @pl.when(((pad >> b) & 1) == 1)
                def _():
                    off = base + (pad & (n - 1))
                    act(pltpu.make_async_copy(zbuf.at[pl.ds(0, QUAD * n)],
                                              xs_out.at[pl.ds(QUAD * off, QUAD * n)], zsem))

        def start_all(e, c):
            pad_runs(e, lambda d: d.start())
            return c

        def wait_all(e, c):
            pad_runs(e, lambda d: d.wait())
            return c

        lax.fori_loop(0, N_EXPERTS, start_all, 0)
        lax.fori_loop(0, N_EXPERTS, wait_all, 0)

    cp.wait()

    def row_copy(t, k):
        return pltpu.make_async_copy(h1p_ref.at[pl.ds(QUAD * t, QUAD)],
                                     xs_out.at[pl.ds(QUAD * pos_smem[k * tt + t], QUAD)], sem)

    def issue(t, c):
        for k in range(TOP_K):
            row_copy(t, k).start(priority=k % 2)
        return c

    lax.fori_loop(0, tt, issue, 0, unroll=ISSUE_UNROLL)

    for k in range(TOP_K):
        pltpu.make_async_copy(h1p_ref, xs_out.at[pl.ds(0, QUAD * tt)], sem).wait()


def _dispatch(counts, pstarts, pcounts, pos_tiles, h1p, n_rows, *, tt):
    t = h1p.shape[0] // QUAD
    grid_spec = pltpu.PrefetchScalarGridSpec(
        num_scalar_prefetch=3,
        grid=(t // tt,),
        in_specs=[pl.BlockSpec(memory_space=pl.ANY), pl.BlockSpec((tt * QUAD, 128), lambda i, *_: (i, 0))],
        out_specs=pl.BlockSpec(memory_space=pl.ANY),
        scratch_shapes=[pltpu.SMEM((TOP_K * tt,), _I32), pltpu.SemaphoreType.DMA, pltpu.SemaphoreType.DMA,
                        pltpu.VMEM((QUAD * ROW_BLOCK // 2, 128), jnp.uint32), pltpu.SemaphoreType.DMA],
    )
    return pl.pallas_call(
        functools.partial(_dispatch_body, tt=tt),
        grid_spec=grid_spec,
        out_shape=jax.ShapeDtypeStruct((n_rows * QUAD, 128), jnp.uint32),
        compiler_params=_cparams("arbitrary"),
        name="dispatch",
    )(counts, pstarts, pcounts, pos_tiles, h1p)


def _ffn_body(blk0_ref, nblk_ref, ntot_ref, xs_hbm, wg_ref, wu_ref, wd_ref, ys_hbm,
              xbuf, ybuf, sem_in, sem_out, wgu_bf, wd_bf):
    e = pl.program_id(0)
    nblk = nblk_ref[e]
    blk0 = blk0_ref[e]
    ntot = ntot_ref[0]

    blk_rows = ROW_BLOCK * QUAD

    def rows(g):
        return pl.ds(pl.multiple_of(g * blk_rows, blk_rows), blk_rows)

    def in_start(g, slot):
        pltpu.make_async_copy(xs_hbm.at[rows(g)], xbuf.at[slot], sem_in.at[slot]).start()

    def in_wait(slot):
        pltpu.make_async_copy(xs_hbm.at[rows(0)], xbuf.at[slot], sem_in.at[slot]).wait()

    def out_start(g, slot):
        pltpu.make_async_copy(ybuf.at[slot], ys_hbm.at[rows(g)], sem_out.at[slot]).start()

    def out_wait(slot):
        pltpu.make_async_copy(ybuf.at[slot], ys_hbm.at[rows(0)], sem_out.at[slot]).wait()

    @pl.when(e == 0)
    def _():
        for i in range(IN_AHEAD):
            @pl.when(i < ntot)
            def _():
                in_start(i, i)

    @pl.when(nblk > 0)
    def _():
        wgu_bf[:, 0:EXPERT_FF] = wg_ref[0].astype(_BF16)
        wgu_bf[:, EXPERT_FF:] = wu_ref[0].astype(_BF16)
        wd_bf[...] = wd_ref[0].astype(_BF16)

        def block(j, carry):
            g = blk0 + j
            slot = g & (RING - 1)
            in_wait(slot)

            @pl.when(g + IN_AHEAD < ntot)
            def _():
                in_start(g + IN_AHEAD, (g + IN_AHEAD) & (RING - 1))

            @pl.when(g >= RING)
            def _():
                out_wait(slot)

            lo, hi = _unpack_halves(_load_rows(xbuf.at[slot], ROW_BLOCK))
            gu = (jnp.dot(lo.astype(_BF16), wgu_bf[0:HALF, :], preferred_element_type=_F32)
                  + jnp.dot(hi.astype(_BF16), wgu_bf[HALF:, :], preferred_element_type=_F32))
            h = (_silu(gu[:, :EXPERT_FF]) * gu[:, EXPERT_FF:]).astype(_BF16)
            _store_rows(ybuf.at[slot], _pack_halves(jnp.dot(h, wd_bf[...], preferred_element_type=_F32)))
            out_start(g, slot)
            return carry

        lax.fori_loop(0, nblk, block, 0)

    @pl.when(e == N_EXPERTS - 1)
    def _():
        for i in range(RING):
            @pl.when(i < ntot)
            def _():
                out_wait((ntot - 1 - i) & (RING - 1))


def _ffn(blk0, nblk, ntot, xs, wg, wu, wd):
    grid_spec = pltpu.PrefetchScalarGridSpec(
        num_scalar_prefetch=3,
        grid=(N_EXPERTS,),
        in_specs=[pl.BlockSpec(memory_space=pl.ANY),
                  pl.BlockSpec((1, D_MODEL, EXPERT_FF), lambda e, *_: (e, 0, 0)),
                  pl.BlockSpec((1, D_MODEL, EXPERT_FF), lambda e, *_: (e, 0, 0)),
                  pl.BlockSpec((1, EXPERT_FF, D_MODEL), lambda e, *_: (e, 0, 0))],
        out_specs=pl.BlockSpec(memory_space=pl.ANY),
        scratch_shapes=[pltpu.VMEM((RING, ROW_BLOCK * QUAD, 128), jnp.uint32),
                        pltpu.VMEM((RING, ROW_BLOCK * QUAD, 128), jnp.uint32),
                        pltpu.SemaphoreType.DMA((RING,)), pltpu.SemaphoreType.DMA((RING,)),
                        pltpu.VMEM((D_MODEL, 2 * EXPERT_FF), _BF16), pltpu.VMEM((EXPERT_FF, D_MODEL), _BF16)],
    )
    return pl.pallas_call(
        _ffn_body,
        grid_spec=grid_spec,
        out_shape=jax.ShapeDtypeStruct(xs.shape, jnp.uint32),
        compiler_params=_cparams("arbitrary"),
        name="ffn",
    )(blk0, nblk, ntot, xs, wg, wu, wd)


def _combine_body(pos_hbm, gate_ref, h1_ref, ys_hbm, wsg_ref, wsu_ref, wsd_ref, g_ref, b_ref,
                  out_ref, pos_smem, psem, ybuf, sem, *, tt):
    i = pl.program_id(0)
    cp = pltpu.make_async_copy(pos_hbm.at[i], pos_smem, psem)
    cp.start()
    cp.wait()

    def row_copy(t, k):
        return pltpu.make_async_copy(ys_hbm.at[pl.ds(QUAD * pos_smem[k * tt + t], QUAD)],
                                     ybuf.at[k, pl.ds(QUAD * t, QUAD)], sem)

    def issue(t, c):
        for k in range(TOP_K):
            row_copy(t, k).start(priority=k % 2)
        return c

    lax.fori_loop(0, tt, issue, 0, unroll=ISSUE_UNROLL)

    x = h1_ref[...]
    xb = x.astype(_BF16)
    shared = _mm(_silu(_mm(xb, wsg_ref[...])) * _mm(xb, wsu_ref[...]), wsd_ref[...])

    for k in range(TOP_K):
        pltpu.make_async_copy(ys_hbm.at[pl.ds(0, QUAD * tt)], ybuf.at[k], sem).wait()

    gcol = gate_ref[...].T
    acc_lo = jnp.zeros((tt, HALF), _F32)
    acc_hi = jnp.zeros((tt, HALF), _F32)
    for k in range(TOP_K):
        lo, hi = _unpack_halves(_load_rows(ybuf.at[k], tt))
        acc_lo = acc_lo + gcol[:, k:k + 1] * lo
        acc_hi = acc_hi + gcol[:, k:k + 1] * hi
    routed = jnp.concatenate([acc_lo, acc_hi], axis=1)
    out_ref[...] = _layer_norm(DN_ALPHA * x + (routed + shared), g_ref[...], b_ref[...])


def _combine(pos_tiles, gate, h1, ys, wsg, wsu, wsd, g, b, *, tt):
    t = h1.shape[0]
    full = lambda a: pl.BlockSpec(a.shape, lambda i: (0,) * a.ndim)
    return pl.pallas_call(
        functools.partial(_combine_body, tt=tt),
        grid=(t // tt,),
        in_specs=[pl.BlockSpec(memory_space=pl.ANY), pl.BlockSpec((TOP_K, tt), lambda i: (0, i)),
                  pl.BlockSpec((tt, D_MODEL), lambda i: (i, 0)), pl.BlockSpec(memory_space=pl.ANY),
                  full(wsg), full(wsu), full(wsd), full(g), full(b)],
        out_specs=pl.BlockSpec((tt, D_MODEL), lambda i: (i, 0)),
        out_shape=jax.ShapeDtypeStruct((t, D_MODEL), _F32),
        scratch_shapes=[pltpu.SMEM((TOP_K * tt,), _I32), pltpu.SemaphoreType.DMA,
                        pltpu.VMEM((TOP_K, tt * QUAD, 128), jnp.uint32), pltpu.SemaphoreType.DMA],
        compiler_params=_cparams("arbitrary"),
        name="combine",
    )(pos_tiles, gate, h1, ys, wsg, wsu, wsd, g, b)


def _sc_gather(table, idx):
    b = idx.shape[0]
    nchunk = b // (SC_WORKERS * SC_CHUNK)
    assert nchunk * SC_WORKERS * SC_CHUNK == b and nchunk % SC_RING == 0
    idx2 = idx.reshape(SC_WORKERS * nchunk, SC_CHUNK)
    row = table.shape[1:]
    mesh = plsc.VectorSubcoreMesh(core_axis_name="c", subcore_axis_name="s",
                                  num_cores=SC_CORES, num_subcores=SC_SUBCORES)

    @functools.partial(
        pl.kernel, mesh=mesh,
        out_type=jax.ShapeDtypeStruct((b,) + row, table.dtype),
        scratch_types=[pltpu.VMEM((nchunk, SC_CHUNK), _I32), pltpu.VMEM((SC_RING, SC_CHUNK) + row, table.dtype),
                       pltpu.SemaphoreType.DMA((SC_RING,)), pltpu.SemaphoreType.DMA((SC_RING,))],
        name="sc_gather",
    )
    def gather(table_hbm, idx_hbm, out_hbm, idx_v, rows_v, sem_g, sem_w):
        wid = lax.axis_index("s") * SC_CORES + lax.axis_index("c")
        c0 = wid * nchunk
        pltpu.sync_copy(idx_hbm.at[pl.ds(pl.multiple_of(c0, nchunk), nchunk)], idx_v)

        def fetch(i, s):
            return pltpu.make_async_copy(table_hbm.at[idx_v.at[i]], rows_v.at[s], sem_g.at[s])

        def flush(i, s):
            rows = pl.ds(pl.multiple_of((c0 + i) * SC_CHUNK, SC_CHUNK), SC_CHUNK)
            return pltpu.make_async_copy(rows_v.at[s], out_hbm.at[rows], sem_w.at[s])

        for s in range(SC_RING):
            fetch(s, s).start()

        @pl.loop(0, nchunk, step=SC_RING)
        def _(g):
            for s in range(SC_RING):
                i = g + s
                fetch(i, s).wait()
                flush(i, s).start()
                flush(i, s).wait()

                @pl.when(i + SC_RING < nchunk)
                def _():
                    fetch(i + SC_RING, s).start()

    return gather(table, idx2)


def _sc_scatter(rows, pos3, n_out):
    nchunk, nk, w = pos3.shape
    per_w = nchunk // SC_WORKERS
    assert per_w * SC_WORKERS == nchunk and w <= 128 and rows.shape[0] == nchunk * w
    row = rows.shape[1:]
    mesh = plsc.VectorSubcoreMesh(core_axis_name="c", subcore_axis_name="s",
                                  num_cores=SC_CORES, num_subcores=SC_SUBCORES)

    @functools.partial(
        pl.kernel, mesh=mesh,
        out_type=jax.ShapeDtypeStruct((n_out,) + row, rows.dtype),
        scratch_types=[pltpu.VMEM((nk, w), _I32), pltpu.VMEM((w,) + row, rows.dtype), pltpu.SemaphoreType.DMA],
        name="sc_scatter",
    )
    def scatter(rows_hbm, pos_hbm, out_hbm, idx_v, rows_v, sem):
        wid = lax.axis_index("s") * SC_CORES + lax.axis_index("c")

        @pl.loop(0, per_w)
        def _(i):
            c = wid * per_w + i
            pltpu.sync_copy(pos_hbm.at[c], idx_v)
            pltpu.sync_copy(rows_hbm.at[pl.ds(pl.multiple_of(c * w, w), w)], rows_v)
            copies = [pltpu.async_copy(rows_v, out_hbm.at[idx_v.at[k]], sem) for k in range(nk)]
            for cp in copies:
                cp.wait()

    return scatter(rows, pos3)


def _padfill_body(cnt_ref, pst_ref, pcn_ref, xs_in, xs_out, zbuf, zsem):
    del xs_in
    zbuf[...] = jnp.zeros_like(zbuf)

    def pad_runs(e, act):
        pad = pcn_ref[e] - cnt_ref[e]
        base = pst_ref[e] + cnt_ref[e]
        for b in range(ROW_BLOCK.bit_length() - 1):
            n = 1 << b

            @pl.when(((pad >> b) & 1) == 1)
            def _():
                off = base + (pad & (n - 1))
                act(pltpu.make_async_copy(zbuf.at[pl.ds(0, QUAD * n)],
                                          xs_out.at[pl.ds(QUAD * off, QUAD * n)], zsem))

    def start_all(e, c):
        pad_runs(e, lambda d: d.start())
        return c

    def wait_all(e, c):
        pad_runs(e, lambda d: d.wait())
        return c

    lax.fori_loop(0, N_EXPERTS, start_all, 0)
    lax.fori_loop(0, N_EXPERTS, wait_all, 0)


def _padfill(counts, pstarts, pcounts, xs):
    grid_spec = pltpu.PrefetchScalarGridSpec(
        num_scalar_prefetch=3,
        grid=(1,),
        in_specs=[pl.BlockSpec(memory_space=pl.ANY)],
        out_specs=pl.BlockSpec(memory_space=pl.ANY),
        scratch_shapes=[pltpu.VMEM((QUAD * ROW_BLOCK // 2, 128), jnp.uint32), pltpu.SemaphoreType.DMA],
    )
    return pl.pallas_call(
        _padfill_body,
        grid_spec=grid_spec,
        out_shape=jax.ShapeDtypeStruct(xs.shape, xs.dtype),
        input_output_aliases={3: 0},
        compiler_params=_cparams("arbitrary"),
        name="padfill",
    )(counts, pstarts, pcounts, xs)


def _combine_stream_body(gate_ref, h1_ref, yg_ref, wsg_ref, wsu_ref, wsd_ref, g_ref, b_ref, out_ref, *, tt):
    x = h1_ref[...]
    xb = x.astype(_BF16)
    shared = _mm(_silu(_mm(xb, wsg_ref[...])) * _mm(xb, wsu_ref[...]), wsd_ref[...])
    gcol = gate_ref[...].T
    acc_lo = jnp.zeros((tt, HALF), _F32)
    acc_hi = jnp.zeros((tt, HALF), _F32)
    for k in range(TOP_K):
        lo, hi = _unpack_halves(_load_rows(yg_ref.at[0, k], tt))
        acc_lo = acc_lo + gcol[:, k:k + 1] * lo
        acc_hi = acc_hi + gcol[:, k:k + 1] * hi
    routed = jnp.concatenate([acc_lo, acc_hi], axis=1)
    out_ref[...] = _layer_norm(DN_ALPHA * x + (routed + shared), g_ref[...], b_ref[...])


def _combine_stream(gate, h1, yg, wsg, wsu, wsd, g, b, *, tt):
    t = h1.shape[0]
    full = lambda a: pl.BlockSpec(a.shape, lambda i: (0,) * a.ndim)
    return pl.pallas_call(
        functools.partial(_combine_stream_body, tt=tt),
        grid=(t // tt,),
        in_specs=[pl.BlockSpec((TOP_K, tt), lambda i: (0, i)), pl.BlockSpec((tt, D_MODEL), lambda i: (i, 0)),
                  pl.BlockSpec((1, TOP_K, tt * QUAD, 128), lambda i: (i, 0, 0, 0)),
                  full(wsg), full(wsu), full(wsd), full(g), full(b)],
        out_specs=pl.BlockSpec((tt, D_MODEL), lambda i: (i, 0)),
        out_shape=jax.ShapeDtypeStruct((t, D_MODEL), _F32),
        compiler_params=_cparams("arbitrary"),
        name="combine",
    )(gate, h1, yg, wsg, wsu, wsd, g, b)


def _pick(n, pref):
    t = min(n, pref)
    while n % t:
        t -= CHUNK
    return t


def _mixer(x, tails, s0, wts, gnw, *, lt, lg, nbb):
    yc, q, k, v, z, bgc, bgr, tails_out = _premix(x, tails, wts, lt=lt)
    bsz, seq, _ = x.shape
    nch = seq // CHUNK
    grow = bgr[:, GDN_HEADS:2 * GDN_HEADS, :].reshape(bsz, GDN_HEADS, nch, CHUNK)
    grow = grow.transpose(0, 2, 1, 3).reshape(bsz, nch, 1, STACK)
    yg, s_out = _gdn(q, k, v, z, bgc, grow, s0, gnw, lg=lg, nbb=nbb)
    return yc, yg, tails_out, s_out


def kernel(x, meta_tokens, w_in, conv_w, conv_norm_w, gdn_conv_w, a_log, dt_bias, gdn_norm_w, w_out,
           ln1_g, ln1_b, w_router, b_router, w_gate, w_up, w_down, ws_gate, ws_up, ws_down, ln2_g, ln2_b):
    assert w_in.shape[0] == 1, "single-layer stack"
    bsz, seq, d = x.shape
    assert d == D_MODEL and seq % CHUNK == 0
    c, gw = CONV_WIDTH, GDN_WIDTH
    win = w_in[0].astype(_BF16)
    wbd = win[:, 3 * c + 4 * gw:]
    zpad = jnp.zeros((128 - 2 * GDN_HEADS,), _F32)
    zpad4 = jnp.zeros((GDN_HEADS,), _F32)
    prow = jnp.zeros((8, 128), _F32)
    prow = prow.at[0].set(jnp.concatenate([zpad4, a_log[0], zpad]))
    prow = prow.at[1].set(jnp.concatenate([zpad4, dt_bias[0], zpad]))
    wts = (win[:, :3 * c], win[:, 3 * c:3 * c + 3 * gw], win[:, 3 * c + 3 * gw:3 * c + 4 * gw],
           jnp.pad(wbd, ((0, 0), (0, 128 - 2 * GDN_HEADS))), wbd.T,
           conv_w[0], conv_norm_w, gdn_conv_w[0], prow, prow.T[:8])
    gnw = gdn_norm_w

    meta = jnp.concatenate([jnp.zeros((CHUNK - N_META, d), x.dtype), meta_tokens.astype(x.dtype)])[None]
    tails0 = jnp.zeros((HIST, c + 3 * gw), _F32)
    s00 = jnp.zeros((GDN_HEADS, GDN_HEAD_DIM, GDN_HEAD_DIM), _F32)
    _, _, tails_m, s_m = _mixer(meta, tails0, s00, wts, gnw, lt=CHUNK, lg=CHUNK, nbb=1)

    yc, yg, _, _ = _mixer(x, tails_m[0], s_m[0], wts, gnw, lt=_pick(seq, 512), lg=_pick(seq, 512),
                          nbb=GDN_ROWS if bsz % GDN_ROWS == 0 else 1)

    t = bsz * seq
    tm = _pick(t, 512)
    h1, h1p = _outproj(yc.reshape(t, c), yg.reshape(t, gw), x.reshape(t, d), w_out[0].astype(_BF16),
                       ln1_g, ln1_b, tm=tm)

    tt = _pick(t, 256)
    wr_t = w_router[0].T
    wr_hi = wr_t.astype(_BF16)
    wr_lo = (wr_t - wr_hi.astype(_F32)).astype(_BF16)
    idx, gate, rank, cnt = _router(h1, wr_hi, wr_lo, b_router[0][:, None], tt=tt)

    counts = cnt[:, 0].astype(_I32)
    pcounts = (counts + ROW_BLOCK - 1) // ROW_BLOCK * ROW_BLOCK
    pends = jnp.cumsum(pcounts)
    pstarts = pends - pcounts
    nb = t * TOP_K // ROW_BLOCK + N_EXPERTS

    ts = tt
    pos = _position(idx, rank, pstarts[:, None].astype(_I32), tt=tt)
    nwin = tt // SC_WINDOW
    pos3 = pos.reshape(t // tt, TOP_K, nwin, SC_WINDOW).transpose(0, 2, 1, 3).reshape(t // SC_WINDOW, TOP_K, SC_WINDOW)
    xs = _sc_scatter(h1p.reshape(t, QUAD, 128), pos3, nb * ROW_BLOCK)
    xs = _padfill(counts, pstarts.astype(_I32), pcounts.astype(_I32), xs.reshape(nb * ROW_BLOCK * QUAD, 128))
    ys = _ffn((pstarts // ROW_BLOCK).astype(_I32), (pcounts // ROW_BLOCK).astype(_I32),
              (pends[-1:] // ROW_BLOCK).astype(_I32), xs, w_gate[0], w_up[0], w_down[0])
    yg = _sc_gather(ys.reshape(nb * ROW_BLOCK, QUAD, 128), pos.reshape(t * TOP_K))
    yg = yg.reshape(t // ts, TOP_K, ts * QUAD, 128)
    out = _combine_stream(gate, h1, yg, ws_gate[0].astype(_BF16), ws_up[0].astype(_BF16),
                          ws_down[0].astype(_BF16), ln2_g, ln2_b, tt=ts)
    return out.reshape(bsz, seq, d)
```

```python
import functools

import jax
import jax.numpy as jnp
from jax import lax
from jax.experimental import pallas as pl
from jax.experimental.pallas import tpu as pltpu
from jax.experimental.pallas import tpu_sc as plsc

_F32 = jnp.float32
_BF16 = jnp.bfloat16
_I32 = jnp.int32

D_MODEL = 1024
N_META = 16
CONV_WIDTH = 512
CONV_K = 3
GDN_HEADS = 4
GDN_HEAD_DIM = 128
GDN_WIDTH = GDN_HEADS * GDN_HEAD_DIM
GDN_CONV_K = 4
CHUNK = 64
N_EXPERTS = 256
TOP_K = 8
N_GROUPS = 8
TOPK_GROUPS = 4
E_PER_GROUP = N_EXPERTS // N_GROUPS
EXPERT_FF = 256
ROUTED_SCALE = 2.5
ROW_BLOCK = 256
DN_ALPHA = 2.0 ** 0.25
NORM_EPS = 1e-5
HALF = D_MODEL // 2
QUAD = HALF // 128
STACK = GDN_HEADS * CHUNK
HIST = 8
GDN_ROWS = 4
ISSUE_UNROLL = 8
SC_CORES = 2
SC_SUBCORES = 16
SC_WORKERS = SC_CORES * SC_SUBCORES
SC_CHUNK = 64
SC_RING = 2
SC_WINDOW = 128
MOE_PARTS = 2
RING = 4
IN_AHEAD = RING - 1

V7X_VMEM_BYTES = 64 * 1024 * 1024
VMEM_LIMIT = V7X_VMEM_BYTES - 8 * 1024 * 1024


def _cparams(*sem):
    return pltpu.CompilerParams(dimension_semantics=sem, vmem_limit_bytes=VMEM_LIMIT)


def _mm(a, b):
    return jnp.dot(a.astype(_BF16), b.astype(_BF16), preferred_element_type=_F32)


def _mm_nt(a, b):
    return lax.dot_general(a.astype(_BF16), b.astype(_BF16), (((1,), (1,)), ((), ())),
                           preferred_element_type=_F32)


def _mm_tn(a, b):
    return lax.dot_general(a.astype(_BF16), b.astype(_BF16), (((0,), (0,)), ((), ())),
                           preferred_element_type=_F32)


def _sigmoid(x):
    return 1.0 / (1.0 + jnp.exp(-x))


def _silu(x):
    return x * _sigmoid(x)


def _softplus(x):
    return jnp.maximum(x, 0.0) + jnp.log1p(jnp.exp(-jnp.abs(x)))


def _pack_halves(y):
    return pltpu.pack_elementwise([y[:, :HALF], y[:, HALF:]], packed_dtype=_BF16)


def _store_rows(ref, packed):
    r = packed.shape[0]
    for c in range(QUAD):
        ref[pl.ds(c, r, stride=QUAD), :] = packed[:, c * 128:(c + 1) * 128]


def _load_rows(ref, r):
    return jnp.concatenate([ref[pl.ds(c, r, stride=QUAD), :] for c in range(QUAD)], axis=1)


def _unpack_halves(p):
    lo = pltpu.unpack_elementwise(p, index=0, packed_dtype=_BF16, unpacked_dtype=_F32)
    hi = pltpu.unpack_elementwise(p, index=1, packed_dtype=_BF16, unpacked_dtype=_F32)
    return lo, hi


def _layer_norm(h, g, b):
    mu = jnp.mean(h, axis=-1, keepdims=True)
    d = h - mu
    var = jnp.mean(d * d, axis=-1, keepdims=True)
    return d * lax.rsqrt(var + NORM_EPS) * g + b


def _premix_body(x_ref, tails_ref, wa_ref, wq_ref, wz_ref, wbd_ref, wbdt_ref, cw_ref, cnw_ref,
                 gcw_ref, prow_ref, pcol_ref,
                 yc_ref, q_ref, k_ref, v_ref, z_ref, bgc_ref, bgr_ref, tout_ref, ext_ref, *, lt):
    cw_ = CONV_WIDTH

    @pl.when(pl.program_id(1) == 0)
    def _():
        ext_ref[0:HIST, :] = tails_ref[...]

    xb = x_ref[0].astype(_BF16)
    pa = jnp.dot(xb, wa_ref[...], preferred_element_type=_F32)
    gate_b = pa[:, 0:cw_]
    u = pa[:, cw_:2 * cw_] * pa[:, 2 * cw_:3 * cw_]
    ext_ref[HIST:HIST + lt, 0:cw_] = u
    pq = jnp.dot(xb, wq_ref[...], preferred_element_type=_F32)
    ext_ref[HIST:HIST + lt, cw_:] = pq

    cw = cw_ref[...]
    ca = u * cw[CONV_K - 1:CONV_K, :]
    for j in range(CONV_K - 1):
        ca = ca + ext_ref[pl.ds(HIST - (CONV_K - 1) + j, lt), 0:cw_] * cw[j:j + 1, :]
    yc = gate_b * ca
    ms = jnp.mean(yc * yc, axis=-1, keepdims=True)
    yc_ref[0] = (yc * lax.rsqrt(ms + NORM_EPS) * cnw_ref[...]).astype(_BF16)

    gcw = gcw_ref[...]
    cq = pq * gcw[GDN_CONV_K - 1:GDN_CONV_K, :]
    for j in range(GDN_CONV_K - 1):
        cq = cq + ext_ref[pl.ds(HIST - (GDN_CONV_K - 1) + j, lt), cw_:] * gcw[j:j + 1, :]
    s = _silu(cq)
    for h in range(GDN_HEADS):
        lo, hi = h * GDN_HEAD_DIM, (h + 1) * GDN_HEAD_DIM
        qh = s[:, lo:hi]
        kh = s[:, GDN_WIDTH + lo:GDN_WIDTH + hi]
        qn = qh * lax.rsqrt(jnp.sum(qh * qh, axis=-1, keepdims=True) + 1e-6)
        kn = kh * lax.rsqrt(jnp.sum(kh * kh, axis=-1, keepdims=True) + 1e-6)
        q_ref[0, :, lo:hi] = (qn * (GDN_HEAD_DIM ** -0.5)).astype(_BF16)
        k_ref[0, :, lo:hi] = kn.astype(_BF16)
    v_ref[0] = s[:, 2 * GDN_WIDTH:].astype(_BF16)
    z_ref[0] = jnp.dot(xb, wz_ref[...], preferred_element_type=_F32).astype(_BF16)

    bdc = jnp.dot(xb, wbd_ref[...], preferred_element_type=_F32)
    prow = prow_ref[...]
    g_c = -jnp.exp(prow[0:1, :]) * _softplus(bdc + prow[1:2, :])
    lane = lax.broadcasted_iota(_I32, bdc.shape, 1)
    bgc_ref[0] = jnp.where(lane < GDN_HEADS, _sigmoid(bdc), g_c)
    bdr = _mm_nt(wbdt_ref[...], xb)
    pcol = pcol_ref[...]
    g_r = -jnp.exp(pcol[:, 0:1]) * _softplus(bdr + pcol[:, 1:2])
    row = lax.broadcasted_iota(_I32, bdr.shape, 0)
    bgr_ref[0] = jnp.where(row < GDN_HEADS, _sigmoid(bdr), g_r)

    tail = ext_ref[lt:lt + HIST, :]
    ext_ref[0:HIST, :] = tail
    tout_ref[0] = tail


def _premix(x, tails, wts, *, lt):
    bsz, seq, d = x.shape
    assert seq % lt == 0
    grid = (bsz, seq // lt)
    full = lambda a: pl.BlockSpec(a.shape, lambda b, j: (0,) * a.ndim)
    tok = lambda w: pl.BlockSpec((1, lt, w), lambda b, j: (b, j, 0))
    (wa, wq, wz, wbd, wbdt, cw, cnw, gcw, prow, pcol) = wts
    ext_w = CONV_WIDTH + 3 * GDN_WIDTH
    out_shape = (
        jax.ShapeDtypeStruct((bsz, seq, CONV_WIDTH), _BF16),
        jax.ShapeDtypeStruct((bsz, seq, GDN_WIDTH), _BF16),
        jax.ShapeDtypeStruct((bsz, seq, GDN_WIDTH), _BF16),
        jax.ShapeDtypeStruct((bsz, seq, GDN_WIDTH), _BF16),
        jax.ShapeDtypeStruct((bsz, seq, GDN_WIDTH), _BF16),
        jax.ShapeDtypeStruct((bsz, seq, 128), _F32),
        jax.ShapeDtypeStruct((bsz, 8, seq), _F32),
        jax.ShapeDtypeStruct((bsz, HIST, ext_w), _F32),
    )
    out_specs = (tok(CONV_WIDTH), tok(GDN_WIDTH), tok(GDN_WIDTH), tok(GDN_WIDTH), tok(GDN_WIDTH),
                 tok(128), pl.BlockSpec((1, 8, lt), lambda b, j: (b, 0, j)),
                 pl.BlockSpec((1, HIST, ext_w), lambda b, j: (b, 0, 0)))
    return pl.pallas_call(
        functools.partial(_premix_body, lt=lt),
        grid=grid,
        in_specs=[tok(d), full(tails)] + [full(w) for w in wts],
        out_specs=out_specs,
        out_shape=out_shape,
        scratch_shapes=[pltpu.VMEM((HIST + lt, ext_w), _F32)],
        compiler_params=_cparams("arbitrary", "arbitrary"),
        name="premix",
    )(x, tails, *wts)


def _cumsum_rows(x):
    row = lax.broadcasted_iota(_I32, x.shape, 0)
    s = 1
    while s < x.shape[0]:
        x = x + jnp.where(row >= s, pltpu.roll(x, s, 0), 0.0)
        s *= 2
    return x


def _cumsum_lanes_seg(x):
    lane = lax.broadcasted_iota(_I32, x.shape, 1) & (CHUNK - 1)
    s = 1
    while s < CHUNK:
        x = x + jnp.where(lane >= s, pltpu.roll(x, s, 1), 0.0)
        s *= 2
    return x


def _stack_heads(a):
    return jnp.concatenate([a[:, h * GDN_HEAD_DIM:(h + 1) * GDN_HEAD_DIM] for h in range(GDN_HEADS)], axis=0)


def _gdn_body(q_ref, k_ref, v_ref, z_ref, bgc_ref, grow_ref, s0_ref, gnw_ref,
              y_ref, sout_ref, s_ref, *, nc, nbb):
    @pl.when(pl.program_id(1) == 0)
    def _():
        for r in range(nbb):
            s_ref[r] = s0_ref[...]

    ri = lax.broadcasted_iota(_I32, (STACK, STACK), 0)
    ci = lax.broadcasted_iota(_I32, (STACK, STACK), 1)
    same64 = (ri >> 6) == (ci >> 6)
    same32 = (ri >> 5) == (ci >> 5)
    same16 = (ri >> 4) == (ci >> 4)
    low_incl = same64 & (ri >= ci)
    low_strict = same64 & (ri > ci)
    gnw = gnw_ref[...]

    def chunk_row(r, c):
        off = pl.multiple_of(c * CHUNK, CHUNK)
        q_all = _stack_heads(q_ref[r, pl.ds(off, CHUNK), :].astype(_F32))
        k_all = _stack_heads(k_ref[r, pl.ds(off, CHUNK), :].astype(_F32))
        v_all = _stack_heads(v_ref[r, pl.ds(off, CHUNK), :].astype(_F32))
        bgc = bgc_ref[r, pl.ds(off, CHUNK), :]
        gcs = _cumsum_rows(bgc)
        hd = (CHUNK, GDN_HEAD_DIM)
        beta_b = jnp.concatenate(
            [jnp.broadcast_to(bgc[:, h:h + 1], hd) for h in range(GDN_HEADS)], axis=0)
        gc_b = jnp.concatenate(
            [jnp.broadcast_to(gcs[:, GDN_HEADS + h:GDN_HEADS + h + 1], hd) for h in range(GDN_HEADS)], axis=0)
        gl = [gcs[CHUNK - 1:CHUNK, GDN_HEADS + h:GDN_HEADS + h + 1] for h in range(GDN_HEADS)]
        gl_b = jnp.concatenate([jnp.broadcast_to(g1, hd) for g1 in gl], axis=0)
        gcr = _cumsum_lanes_seg(jnp.broadcast_to(grow_ref[r, c], (8, STACK)))[0:1, :]

        diff = jnp.concatenate([gc_b, gc_b], axis=1) - gcr
        decay = jnp.exp(jnp.where(low_incl, diff, -1e30))
        kb = k_all * beta_b
        a1 = _mm_nt(jnp.concatenate([kb, q_all], axis=0), k_all)
        yield
        m = jnp.where(low_strict, a1[:STACK] * decay, 0.0)
        attn = a1[STACK:] * decay

        l16 = jnp.where(same16, m, 0.0)
        c1 = jnp.where(same32 & jnp.logical_not(same16), m, 0.0)
        c2 = jnp.where(same32, 0.0, m)
        p2 = _mm(l16, l16)
        yield
        p4 = _mm(p2, p2)
        t = _mm(l16, p2)
        yield
        na = p2 - l16 - t
        p8 = _mm(p4, p4)
        t = _mm(na, p4)
        yield
        nb = na + p4 + t
        t = _mm(nb, p8)
        yield
        ncm = nb + p8 + t
        t = _mm(c1, ncm)
        yield
        y1 = c1 + t
        t = _mm(ncm, y1)
        yield
        n1 = ncm - y1 - t
        t = _mm(c2, n1)
        yield
        y2 = c2 + t
        t = _mm(n1, y2)
        yield
        nt = n1 - y2 - t

        egc = jnp.exp(gc_b)
        rhs = jnp.concatenate([v_all * beta_b, kb * egc], axis=1)
        t = _mm(nt, rhs)
        yield
        uw = rhs + t
        u_all = uw[:, :GDN_HEAD_DIM]
        w_all = uw[:, GDN_HEAD_DIM:]
        qd = q_all * egc
        kd = k_all * jnp.exp(gl_b - gc_b)

        bs = []
        for h in range(GDN_HEADS):
            r0, r1 = h * CHUNK, (h + 1) * CHUNK
            bs.append(_mm(jnp.concatenate([w_all[r0:r1], qd[r0:r1]], axis=0), s_ref[r, h]))
        yield
        vn = [u_all[h * CHUNK:(h + 1) * CHUNK] - bs[h][:CHUNK] for h in range(GDN_HEADS)]
        vn_all = jnp.concatenate(vn, axis=0)
        t = _mm(attn, vn_all)
        ds = [_mm_tn(kd[h * CHUNK:(h + 1) * CHUNK], vn[h]) for h in range(GDN_HEADS)]
        yield
        o_all = jnp.concatenate([b[CHUNK:] for b in bs], axis=0) + t
        for h in range(GDN_HEADS):
            r0, r1 = h * CHUNK, (h + 1) * CHUNK
            s_ref[r, h] = s_ref[r, h] * jnp.exp(gl[h]) + ds[h]
            o = o_all[r0:r1]
            zz = z_ref[r, pl.ds(off, CHUNK), h * GDN_HEAD_DIM:(h + 1) * GDN_HEAD_DIM].astype(_F32)
            on = o * lax.rsqrt(jnp.mean(o * o, axis=-1, keepdims=True) + NORM_EPS) * gnw
            y_ref[r, pl.ds(off, CHUNK), h * GDN_HEAD_DIM:(h + 1) * GDN_HEAD_DIM] = (on * _silu(zz)).astype(_BF16)

    def chunk(c, carry):
        live = [chunk_row(r, c) for r in range(nbb)]
        while live:
            live = [g for g in live if next(g, live) is not live]
        return carry

    lax.fori_loop(0, nc, chunk, 0)
    sout_ref[...] = s_ref[...]


def _gdn(q, k, v, z, bgc, grow, s0, gnw, *, lg, nbb):
    bsz, seq, _ = q.shape
    assert seq % lg == 0 and lg % CHUNK == 0 and bsz % nbb == 0
    nc = lg // CHUNK
    tok = lambda w: pl.BlockSpec((nbb, lg, w), lambda b, j: (b, j, 0))
    full = lambda a: pl.BlockSpec(a.shape, lambda b, j: (0,) * a.ndim)
    st = (nbb, GDN_HEADS, GDN_HEAD_DIM, GDN_HEAD_DIM)
    return pl.pallas_call(
        functools.partial(_gdn_body, nc=nc, nbb=nbb),
        grid=(bsz // nbb, seq // lg),
        in_specs=[tok(GDN_WIDTH)] * 4 + [tok(128), pl.BlockSpec((nbb, nc, 1, STACK), lambda b, j: (b, j, 0, 0)),
                                           full(s0), full(gnw)],
        out_specs=(tok(GDN_WIDTH), pl.BlockSpec(st, lambda b, j: (b, 0, 0, 0))),
        out_shape=(jax.ShapeDtypeStruct((bsz, seq, GDN_WIDTH), _BF16),
                   jax.ShapeDtypeStruct((bsz, GDN_HEADS, GDN_HEAD_DIM, GDN_HEAD_DIM), _F32)),
        scratch_shapes=[pltpu.VMEM(st, _F32)],
        compiler_params=_cparams("arbitrary", "arbitrary"),
        name="gdn",
    )(q, k, v, z, bgc, grow, s0, gnw)


def _outproj_body(yc_ref, yg_ref, x_ref, wo_ref, g_ref, b_ref, h1_ref, h1p_ref):
    mix = (jnp.dot(yc_ref[...], wo_ref[0:CONV_WIDTH, :], preferred_element_type=_F32)
           + jnp.dot(yg_ref[...], wo_ref[CONV_WIDTH:, :], preferred_element_type=_F32))
    h1 = _layer_norm(DN_ALPHA * x_ref[...] + mix, g_ref[...], b_ref[...])
    h1_ref[...] = h1
    _store_rows(h1p_ref, _pack_halves(h1))


def _outproj(yc, yg, x2d, wo, g, b, *, tm):
    t = x2d.shape[0]
    assert t % tm == 0
    row = lambda w: pl.BlockSpec((tm, w), lambda i: (i, 0))
    full = lambda a: pl.BlockSpec(a.shape, lambda i: (0,) * a.ndim)
    return pl.pallas_call(
        _outproj_body,
        grid=(t // tm,),
        in_specs=[row(CONV_WIDTH), row(GDN_WIDTH), row(D_MODEL), full(wo), full(g), full(b)],
        out_specs=(row(D_MODEL), pl.BlockSpec((tm * QUAD, 128), lambda i: (i, 0))),
        out_shape=(jax.ShapeDtypeStruct((t, D_MODEL), _F32), jax.ShapeDtypeStruct((t * QUAD, 128), jnp.uint32)),
        compiler_params=_cparams("arbitrary"),
        name="outproj",
    )(yc, yg, x2d, wo, g, b)


def _router_body(h1_ref, wh_ref, wl_ref, br_ref, idx_ref, gate_ref, rank_ref, cnt_ref, carry_ref, *, tt):
    @pl.when(pl.program_id(0) == 0)
    def _():
        carry_ref[...] = jnp.zeros_like(carry_ref)

    x = h1_ref[...]
    xh = x.astype(_BF16)
    xl = (x - xh.astype(_F32)).astype(_BF16)
    wh = wh_ref[...]
    logits = _mm_nt(wh, xh) + _mm_nt(wh, xl) + _mm_nt(wl_ref[...], xh)
    scores = _sigmoid(logits)
    sel = scores + br_ref[...]
    ninf = -jnp.inf

    r32 = lax.broadcasted_iota(_I32, (E_PER_GROUP, tt), 0)
    gsc = []
    for g in range(N_GROUPS):
        xg = sel[g * E_PER_GROUP:(g + 1) * E_PER_GROUP]
        m1 = jnp.max(xg, axis=0, keepdims=True)
        i1 = jnp.min(jnp.where(xg == m1, r32, E_PER_GROUP), axis=0, keepdims=True)
        m2 = jnp.max(jnp.where(r32 == i1, ninf, xg), axis=0, keepdims=True)
        gsc.append(m1 + m2)
    work = jnp.concatenate(gsc, axis=0)
    r8 = lax.broadcasted_iota(_I32, (N_GROUPS, tt), 0)
    gkeep = jnp.zeros((N_GROUPS, tt), _F32)
    for _ in range(TOPK_GROUPS):
        m = jnp.max(work, axis=0, keepdims=True)
        gi = jnp.min(jnp.where(work == m, r8, N_GROUPS), axis=0, keepdims=True)
        pick = r8 == gi
        gkeep = jnp.where(pick, 1.0, gkeep)
        work = jnp.where(pick, ninf, work)
    selm = jnp.concatenate(
        [jnp.where(gkeep[g:g + 1] > 0.5, sel[g * E_PER_GROUP:(g + 1) * E_PER_GROUP], ninf)
         for g in range(N_GROUPS)], axis=0)

    re = lax.broadcasted_iota(_I32, (N_EXPERTS, tt), 0)
    msel = jnp.zeros((N_EXPERTS, tt), _F32)
    idxs, gates = [], []
    for _ in range(TOP_K):
        m = jnp.max(selm, axis=0, keepdims=True)
        ii = jnp.min(jnp.where(selm == m, re, N_EXPERTS), axis=0, keepdims=True)
        hit = re == ii
        idxs.append(ii)
        gates.append(jnp.sum(jnp.where(hit, scores, 0.0), axis=0, keepdims=True))
        selm = jnp.where(hit, ninf, selm)
        msel = jnp.where(hit, 1.0, msel)
    gate = jnp.concatenate(gates, axis=0)
    gate_ref[...] = gate / jnp.sum(gate, axis=0, keepdims=True) * ROUTED_SCALE
    idx_ref[...] = jnp.concatenate(idxs, axis=0)

    ta = lax.broadcasted_iota(_I32, (tt, tt), 0)
    tb = lax.broadcasted_iota(_I32, (tt, tt), 1)
    earlier = jnp.where(ta < tb, 1.0, 0.0)
    carry = carry_ref[...]
    rank_all = _mm(msel, earlier) + carry[:, 0:1]
    rank_ref[...] = jnp.concatenate(
        [jnp.sum(jnp.where(re == ii, rank_all, 0.0), axis=0, keepdims=True) for ii in idxs],
        axis=0).astype(_I32)
    carry = carry + jnp.sum(msel, axis=1, keepdims=True)
    carry_ref[...] = carry
    cnt_ref[...] = carry


def _router(h1, wh, wl, br, *, tt, tile0, t):
    assert t % tt == 0
    full = lambda a: pl.BlockSpec(a.shape, lambda i: (0,) * a.ndim)
    kt = pl.BlockSpec((TOP_K, tt), lambda i: (0, i))
    return pl.pallas_call(
        functools.partial(_router_body, tt=tt),
        grid=(t // tt,),
        in_specs=[pl.BlockSpec((tt, D_MODEL), lambda i: (i + tile0, 0)), full(wh), full(wl), full(br)],
        out_specs=(kt, kt, kt, pl.BlockSpec((N_EXPERTS, 128), lambda i: (0, 0))),
        out_shape=(jax.ShapeDtypeStruct((TOP_K, t), _I32), jax.ShapeDtypeStruct((TOP_K, t), _F32),
                   jax.ShapeDtypeStruct((TOP_K, t), _I32), jax.ShapeDtypeStruct((N_EXPERTS, 128), _F32)),
        scratch_shapes=[pltpu.VMEM((N_EXPERTS, 128), _F32)],
        compiler_params=_cparams("arbitrary"),
        name="router",
    )(h1, wh, wl, br)


def _position_body(idx_ref, rank_ref, pstart_ref, pos_ref, *, tt):
    re = lax.broadcasted_iota(_I32, (N_EXPERTS, tt), 0)
    ps = pstart_ref[...]
    idx = idx_ref[...]
    rows = [jnp.sum(jnp.where(re == idx[k:k + 1], ps, 0), axis=0, keepdims=True) for k in range(TOP_K)]
    pos_ref[0] = jnp.concatenate(rows, axis=0) + rank_ref[...]


def _position(idx, rank, pstart, *, tt):
    t = idx.shape[1]
    kt = pl.BlockSpec((TOP_K, tt), lambda i: (0, i))
    return pl.pallas_call(
        functools.partial(_position_body, tt=tt),
        grid=(t // tt,),
        in_specs=[kt, kt, pl.BlockSpec(pstart.shape, lambda i: (0, 0))],
        out_specs=pl.BlockSpec((1, TOP_K, tt), lambda i: (i, 0, 0)),
        out_shape=jax.ShapeDtypeStruct((t // tt, TOP_K, tt), _I32),
        compiler_params=_cparams("arbitrary"),
        name="position",
    )(idx, rank, pstart)


def _dispatch_body(cnt_ref, pst_ref, pcn_ref, pos_hbm, h1p_ref, xs_out, pos_smem, psem, sem, zbuf, zsem, *, tt):
    i = pl.program_id(0)
    cp = pltpu.make_async_copy(pos_hbm.at[i], pos_smem, psem)
    cp.start()

    @pl.when(i == 0)
    def _():
        zbuf[...] = jnp.zeros_like(zbuf)

        def pad_runs(e, act):
            pad = pcn_ref[e] - cnt_ref[e]
            base = pst_ref[e] + cnt_ref[e]
            for b in range(ROW_BLOCK.bit_length() - 1):
                n = 1 << b

                @pl.when(((pad >> b) & 1) == 1)
                def _():
                    off = base + (pad & (n - 1))
                    act(pltpu.make_async_copy(zbuf.at[pl.ds(0, QUAD * n)],
                                              xs_out.at[pl.ds(QUAD * off, QUAD * n)], zsem))

        def start_all(e, c):
            pad_runs(e, lambda d: d.start())
            return c

        def wait_all(e, c):
            pad_runs(e, lambda d: d.wait())
            return c

        lax.fori_loop(0, N_EXPERTS, start_all, 0)
        lax.fori_loop(0, N_EXPERTS, wait_all, 0)

    cp.wait()

    def row_copy(t, k):
        return pltpu.make_async_copy(h1p_ref.at[pl.ds(QUAD * t, QUAD)],
                                     xs_out.at[pl.ds(QUAD * pos_smem[k * tt + t], QUAD)], sem)

    def issue(t, c):
        for k in range(TOP_K):
            row_copy(t, k).start(priority=k % 2)
        return c

    lax.fori_loop(0, tt, issue, 0, unroll=ISSUE_UNROLL)

    for k in range(TOP_K):
        pltpu.make_async_copy(h1p_ref, xs_out.at[pl.ds(0, QUAD * tt)], sem).wait()


def _dispatch(counts, pstarts, pcounts, pos_tiles, h1p, n_rows, *, tt):
    t = h1p.shape[0] // QUAD
    grid_spec = pltpu.PrefetchScalarGridSpec(
        num_scalar_prefetch=3,
        grid=(t // tt,),
        in_specs=[pl.BlockSpec(memory_space=pl.ANY), pl.BlockSpec((tt * QUAD, 128), lambda i, *_: (i, 0))],
        out_specs=pl.BlockSpec(memory_space=pl.ANY),
        scratch_shapes=[pltpu.SMEM((TOP_K * tt,), _I32), pltpu.SemaphoreType.DMA, pltpu.SemaphoreType.DMA,
                        pltpu.VMEM((QUAD * ROW_BLOCK // 2, 128), jnp.uint32), pltpu.SemaphoreType.DMA],
    )
    return pl.pallas_call(
        functools.partial(_dispatch_body, tt=tt),
        grid_spec=grid_spec,
        out_shape=jax.ShapeDtypeStruct((n_rows * QUAD, 128), jnp.uint32),
        compiler_params=_cparams("arbitrary"),
        name="dispatch",
    )(counts, pstarts, pcounts, pos_tiles, h1p)


def _ffn_body(blk0_ref, nblk_ref, ntot_ref, xs_hbm, wg_ref, wu_ref, wd_ref, ys_hbm,
              xbuf, ybuf, sem_in, sem_out, wgu_bf, wd_bf):
    e = pl.program_id(0)
    nblk = nblk_ref[e]
    blk0 = blk0_ref[e]
    ntot = ntot_ref[0]

    blk_rows = ROW_BLOCK * QUAD

    def rows(g):
        return pl.ds(pl.multiple_of(g * blk_rows, blk_rows), blk_rows)

    def in_start(g, slot):
        pltpu.make_async_copy(xs_hbm.at[rows(g)], xbuf.at[slot], sem_in.at[slot]).start()

    def in_wait(slot):
        pltpu.make_async_copy(xs_hbm.at[rows(0)], xbuf.at[slot], sem_in.at[slot]).wait()

    def out_start(g, slot):
        pltpu.make_async_copy(ybuf.at[slot], ys_hbm.at[rows(g)], sem_out.at[slot]).start()

    def out_wait(slot):
        pltpu.make_async_copy(ybuf.at[slot], ys_hbm.at[rows(0)], sem_out.at[slot]).wait()

    @pl.when(e == 0)
    def _():
        for i in range(IN_AHEAD):
            @pl.when(i < ntot)
            def _():
                in_start(i, i)

    @pl.when(nblk > 0)
    def _():
        wgu_bf[:, 0:EXPERT_FF] = wg_ref[0].astype(_BF16)
        wgu_bf[:, EXPERT_FF:] = wu_ref[0].astype(_BF16)
        wd_bf[...] = wd_ref[0].astype(_BF16)

        def block(j, carry):
            g = blk0 + j
            slot = g & (RING - 1)
            in_wait(slot)

            @pl.when(g + IN_AHEAD < ntot)
            def _():
                in_start(g + IN_AHEAD, (g + IN_AHEAD) & (RING - 1))

            @pl.when(g >= RING)
            def _():
                out_wait(slot)

            lo, hi = _unpack_halves(_load_rows(xbuf.at[slot], ROW_BLOCK))
            gu = (jnp.dot(lo.astype(_BF16), wgu_bf[0:HALF, :], preferred_element_type=_F32)
                  + jnp.dot(hi.astype(_BF16), wgu_bf[HALF:, :], preferred_element_type=_F32))
            h = (_silu(gu[:, :EXPERT_FF]) * gu[:, EXPERT_FF:]).astype(_BF16)
            _store_rows(ybuf.at[slot], _pack_halves(jnp.dot(h, wd_bf[...], preferred_element_type=_F32)))
            out_start(g, slot)
            return carry

        lax.fori_loop(0, nblk, block, 0)

    @pl.when(e == N_EXPERTS - 1)
    def _():
        for i in range(RING):
            @pl.when(i < ntot)
            def _():
                out_wait((ntot - 1 - i) & (RING - 1))


def _ffn(blk0, nblk, ntot, xs, wg, wu, wd):
    grid_spec = pltpu.PrefetchScalarGridSpec(
        num_scalar_prefetch=3,
        grid=(N_EXPERTS,),
        in_specs=[pl.BlockSpec(memory_space=pl.ANY),
                  pl.BlockSpec((1, D_MODEL, EXPERT_FF), lambda e, *_: (e, 0, 0)),
                  pl.BlockSpec((1, D_MODEL, EXPERT_FF), lambda e, *_: (e, 0, 0)),
                  pl.BlockSpec((1, EXPERT_FF, D_MODEL), lambda e, *_: (e, 0, 0))],
        out_specs=pl.BlockSpec(memory_space=pl.ANY),
        scratch_shapes=[pltpu.VMEM((RING, ROW_BLOCK * QUAD, 128), jnp.uint32),
                        pltpu.VMEM((RING, ROW_BLOCK * QUAD, 128), jnp.uint32),
                        pltpu.SemaphoreType.DMA((RING,)), pltpu.SemaphoreType.DMA((RING,)),
                        pltpu.VMEM((D_MODEL, 2 * EXPERT_FF), _BF16), pltpu.VMEM((EXPERT_FF, D_MODEL), _BF16)],
    )
    return pl.pallas_call(
        _ffn_body,
        grid_spec=grid_spec,
        out_shape=jax.ShapeDtypeStruct(xs.shape, jnp.uint32),
        compiler_params=_cparams("arbitrary"),
        name="ffn",
    )(blk0, nblk, ntot, xs, wg, wu, wd)


def _combine_body(pos_hbm, gate_ref, h1_ref, ys_hbm, wsg_ref, wsu_ref, wsd_ref, g_ref, b_ref,
                  out_ref, pos_smem, psem, ybuf, sem, *, tt):
    i = pl.program_id(0)
    cp = pltpu.make_async_copy(pos_hbm.at[i], pos_smem, psem)
    cp.start()
    cp.wait()

    def row_copy(t, k):
        return pltpu.make_async_copy(ys_hbm.at[pl.ds(QUAD * pos_smem[k * tt + t], QUAD)],
                                     ybuf.at[k, pl.ds(QUAD * t, QUAD)], sem)

    def issue(t, c):
        for k in range(TOP_K):
            row_copy(t, k).start(priority=k % 2)
        return c

    lax.fori_loop(0, tt, issue, 0, unroll=ISSUE_UNROLL)

    x = h1_ref[...]
    xb = x.astype(_BF16)
    shared = _mm(_silu(_mm(xb, wsg_ref[...])) * _mm(xb, wsu_ref[...]), wsd_ref[...])

    for k in range(TOP_K):
        pltpu.make_async_copy(ys_hbm.at[pl.ds(0, QUAD * tt)], ybuf.at[k], sem).wait()

    gcol = gate_ref[...].T
    acc_lo = jnp.zeros((tt, HALF), _F32)
    acc_hi = jnp.zeros((tt, HALF), _F32)
    for k in range(TOP_K):
        lo, hi = _unpack_halves(_load_rows(ybuf.at[k], tt))
        acc_lo = acc_lo + gcol[:, k:k + 1] * lo
        acc_hi = acc_hi + gcol[:, k:k + 1] * hi
    routed = jnp.concatenate([acc_lo, acc_hi], axis=1)
    out_ref[...] = _layer_norm(DN_ALPHA * x + (routed + shared), g_ref[...], b_ref[...])


def _combine(pos_tiles, gate, h1, ys, wsg, wsu, wsd, g, b, *, tt):
    t = h1.shape[0]
    full = lambda a: pl.BlockSpec(a.shape, lambda i: (0,) * a.ndim)
    return pl.pallas_call(
        functools.partial(_combine_body, tt=tt),
        grid=(t // tt,),
        in_specs=[pl.BlockSpec(memory_space=pl.ANY), pl.BlockSpec((TOP_K, tt), lambda i: (0, i)),
                  pl.BlockSpec((tt, D_MODEL), lambda i: (i, 0)), pl.BlockSpec(memory_space=pl.ANY),
                  full(wsg), full(wsu), full(wsd), full(g), full(b)],
        out_specs=pl.BlockSpec((tt, D_MODEL), lambda i: (i, 0)),
        out_shape=jax.ShapeDtypeStruct((t, D_MODEL), _F32),
        scratch_shapes=[pltpu.SMEM((TOP_K * tt,), _I32), pltpu.SemaphoreType.DMA,
                        pltpu.VMEM((TOP_K, tt * QUAD, 128), jnp.uint32), pltpu.SemaphoreType.DMA],
        compiler_params=_cparams("arbitrary"),
        name="combine",
    )(pos_tiles, gate, h1, ys, wsg, wsu, wsd, g, b)


def _sc_gather(table, idx):
    b = idx.shape[0]
    nchunk = b // (SC_WORKERS * SC_CHUNK)
    assert nchunk * SC_WORKERS * SC_CHUNK == b and nchunk % SC_RING == 0
    idx2 = idx.reshape(SC_WORKERS * nchunk, SC_CHUNK)
    row = table.shape[1:]
    mesh = plsc.VectorSubcoreMesh(core_axis_name="c", subcore_axis_name="s",
                                  num_cores=SC_CORES, num_subcores=SC_SUBCORES)

    @functools.partial(
        pl.kernel, mesh=mesh,
        out_type=jax.ShapeDtypeStruct((b,) + row, table.dtype),
        scratch_types=[pltpu.VMEM((nchunk, SC_CHUNK), _I32), pltpu.VMEM((SC_RING, SC_CHUNK) + row, table.dtype),
                       pltpu.SemaphoreType.DMA((SC_RING,)), pltpu.SemaphoreType.DMA((SC_RING,))],
        name="sc_gather",
    )
    def gather(table_hbm, idx_hbm, out_hbm, idx_v, rows_v, sem_g, sem_w):
        wid = lax.axis_index("s") * SC_CORES + lax.axis_index("c")
        c0 = wid * nchunk
        pltpu.sync_copy(idx_hbm.at[pl.ds(pl.multiple_of(c0, nchunk), nchunk)], idx_v)

        def fetch(i, s):
            return pltpu.make_async_copy(table_hbm.at[idx_v.at[i]], rows_v.at[s], sem_g.at[s])

        def flush(i, s):
            rows = pl.ds(pl.multiple_of((c0 + i) * SC_CHUNK, SC_CHUNK), SC_CHUNK)
            return pltpu.make_async_copy(rows_v.at[s], out_hbm.at[rows], sem_w.at[s])

        for s in range(SC_RING):
            fetch(s, s).start()

        @pl.loop(0, nchunk, step=SC_RING)
        def _(g):
            for s in range(SC_RING):
                i = g + s
                fetch(i, s).wait()
                flush(i, s).start()
                flush(i, s).wait()

                @pl.when(i + SC_RING < nchunk)
                def _():
                    fetch(i + SC_RING, s).start()

    return gather(table, idx2)


def _sc_scatter(rows, pos3, n_out, row0):
    nchunk, nk, w = pos3.shape
    per_w = nchunk // SC_WORKERS
    assert per_w * SC_WORKERS == nchunk and w <= 128 and row0 % w == 0 and rows.shape[0] >= row0 + nchunk * w
    row = rows.shape[1:]
    mesh = plsc.VectorSubcoreMesh(core_axis_name="c", subcore_axis_name="s",
                                  num_cores=SC_CORES, num_subcores=SC_SUBCORES)

    @functools.partial(
        pl.kernel, mesh=mesh,
        out_type=jax.ShapeDtypeStruct((n_out,) + row, rows.dtype),
        scratch_types=[pltpu.VMEM((nk, w), _I32), pltpu.VMEM((w,) + row, rows.dtype), pltpu.SemaphoreType.DMA],
        name="sc_scatter",
    )
    def scatter(rows_hbm, pos_hbm, out_hbm, idx_v, rows_v, sem):
        wid = lax.axis_index("s") * SC_CORES + lax.axis_index("c")

        @pl.loop(0, per_w)
        def _(i):
            c = wid * per_w + i
            pltpu.sync_copy(pos_hbm.at[c], idx_v)
            pltpu.sync_copy(rows_hbm.at[pl.ds(pl.multiple_of(row0 + c * w, w), w)], rows_v)
            copies = [pltpu.async_copy(rows_v, out_hbm.at[idx_v.at[k]], sem) for k in range(nk)]
            for cp in copies:
                cp.wait()

    return scatter(rows, pos3)


def _padfill_body(cnt_ref, pst_ref, pcn_ref, xs_in, xs_out, zbuf, zsem):
    del xs_in
    zbuf[...] = jnp.zeros_like(zbuf)

    def pad_runs(e, act):
        pad = pcn_ref[e] - cnt_ref[e]
        base = pst_ref[e] + cnt_ref[e]
        for b in range(ROW_BLOCK.bit_length() - 1):
            n = 1 << b

            @pl.when(((pad >> b) & 1) == 1)
            def _():
                off = base + (pad & (n - 1))
                act(pltpu.make_async_copy(zbuf.at[pl.ds(0, QUAD * n)],
                                          xs_out.at[pl.ds(QUAD * off, QUAD * n)], zsem))

    def start_all(e, c):
        pad_runs(e, lambda d: d.start())
        return c

    def wait_all(e, c):
        pad_runs(e, lambda d: d.wait())
        return c

    lax.fori_loop(0, N_EXPERTS, start_all, 0)
    lax.fori_loop(0, N_EXPERTS, wait_all, 0)


def _padfill(counts, pstarts, pcounts, xs):
    grid_spec = pltpu.PrefetchScalarGridSpec(
        num_scalar_prefetch=3,
        grid=(1,),
        in_specs=[pl.BlockSpec(memory_space=pl.ANY)],
        out_specs=pl.BlockSpec(memory_space=pl.ANY),
        scratch_shapes=[pltpu.VMEM((QUAD * ROW_BLOCK // 2, 128), jnp.uint32), pltpu.SemaphoreType.DMA],
    )
    return pl.pallas_call(
        _padfill_body,
        grid_spec=grid_spec,
        out_shape=jax.ShapeDtypeStruct(xs.shape, xs.dtype),
        input_output_aliases={3: 0},
        compiler_params=_cparams("arbitrary"),
        name="padfill",
    )(counts, pstarts, pcounts, xs)


def _combine_stream_body(gate_ref, h1_ref, yg_ref, wsg_ref, wsu_ref, wsd_ref, g_ref, b_ref, out_ref, *, tt):
    x = h1_ref[...]
    xb = x.astype(_BF16)
    shared = _mm(_silu(_mm(xb, wsg_ref[...])) * _mm(xb, wsu_ref[...]), wsd_ref[...])
    gcol = gate_ref[...].T
    acc_lo = jnp.zeros((tt, HALF), _F32)
    acc_hi = jnp.zeros((tt, HALF), _F32)
    for k in range(TOP_K):
        lo, hi = _unpack_halves(_load_rows(yg_ref.at[0, k], tt))
        acc_lo = acc_lo + gcol[:, k:k + 1] * lo
        acc_hi = acc_hi + gcol[:, k:k + 1] * hi
    routed = jnp.concatenate([acc_lo, acc_hi], axis=1)
    out_ref[...] = _layer_norm(DN_ALPHA * x + (routed + shared), g_ref[...], b_ref[...])


def _combine_stream(gate, h1, yg, wsg, wsu, wsd, g, b, out_prev, *, tt, tile0):
    t_all = h1.shape[0]
    t = gate.shape[1]
    full = lambda a: pl.BlockSpec(a.shape, lambda i: (0,) * a.ndim)
    in_specs = [pl.BlockSpec((TOP_K, tt), lambda i: (0, i)), pl.BlockSpec((tt, D_MODEL), lambda i: (i + tile0, 0)),
                pl.BlockSpec((1, TOP_K, tt * QUAD, 128), lambda i: (i, 0, 0, 0)),
                full(wsg), full(wsu), full(wsd), full(g), full(b)]
    args = [gate, h1, yg, wsg, wsu, wsd, g, b]
    aliases = {}
    body = functools.partial(_combine_stream_body, tt=tt)
    if out_prev is not None:
        in_specs.append(pl.BlockSpec(memory_space=pl.ANY))
        args.append(out_prev)
        aliases = {len(args) - 1: 0}
        body = lambda *refs: _combine_stream_body(*refs[:8], refs[9], tt=tt)
    return pl.pallas_call(
        body,
        grid=(t // tt,),
        in_specs=in_specs,
        out_specs=pl.BlockSpec((tt, D_MODEL), lambda i: (i + tile0, 0)),
        out_shape=jax.ShapeDtypeStruct((t_all, D_MODEL), _F32),
        input_output_aliases=aliases,
        compiler_params=_cparams("arbitrary"),
        name="combine",
    )(*args)


def _pick(n, pref):
    t = min(n, pref)
    while n % t:
        t -= CHUNK
    return t


def _mixer(x, tails, s0, wts, gnw, *, lt, lg, nbb):
    yc, q, k, v, z, bgc, bgr, tails_out = _premix(x, tails, wts, lt=lt)
    bsz, seq, _ = x.shape
    nch = seq // CHUNK
    grow = bgr[:, GDN_HEADS:2 * GDN_HEADS, :].reshape(bsz, GDN_HEADS, nch, CHUNK)
    grow = grow.transpose(0, 2, 1, 3).reshape(bsz, nch, 1, STACK)
    yg, s_out = _gdn(q, k, v, z, bgc, grow, s0, gnw, lg=lg, nbb=nbb)
    return yc, yg, tails_out, s_out


def kernel(x, meta_tokens, w_in, conv_w, conv_norm_w, gdn_conv_w, a_log, dt_bias, gdn_norm_w, w_out,
           ln1_g, ln1_b, w_router, b_router, w_gate, w_up, w_down, ws_gate, ws_up, ws_down, ln2_g, ln2_b):
    assert w_in.shape[0] == 1, "single-layer stack"
    bsz, seq, d = x.shape
    assert d == D_MODEL and seq % CHUNK == 0
    c, gw = CONV_WIDTH, GDN_WIDTH
    win = w_in[0].astype(_BF16)
    wbd = win[:, 3 * c + 4 * gw:]
    zpad = jnp.zeros((128 - 2 * GDN_HEADS,), _F32)
    zpad4 = jnp.zeros((GDN_HEADS,), _F32)
    prow = jnp.zeros((8, 128), _F32)
    prow = prow.at[0].set(jnp.concatenate([zpad4, a_log[0], zpad]))
    prow = prow.at[1].set(jnp.concatenate([zpad4, dt_bias[0], zpad]))
    wts = (win[:, :3 * c], win[:, 3 * c:3 * c + 3 * gw], win[:, 3 * c + 3 * gw:3 * c + 4 * gw],
           jnp.pad(wbd, ((0, 0), (0, 128 - 2 * GDN_HEADS))), wbd.T,
           conv_w[0], conv_norm_w, gdn_conv_w[0], prow, prow.T[:8])
    gnw = gdn_norm_w

    meta = jnp.concatenate([jnp.zeros((CHUNK - N_META, d), x.dtype), meta_tokens.astype(x.dtype)])[None]
    tails0 = jnp.zeros((HIST, c + 3 * gw), _F32)
    s00 = jnp.zeros((GDN_HEADS, GDN_HEAD_DIM, GDN_HEAD_DIM), _F32)
    _, _, tails_m, s_m = _mixer(meta, tails0, s00, wts, gnw, lt=CHUNK, lg=CHUNK, nbb=1)

    yc, yg, _, _ = _mixer(x, tails_m[0], s_m[0], wts, gnw, lt=_pick(seq, 512), lg=_pick(seq, 512),
                          nbb=GDN_ROWS if bsz % GDN_ROWS == 0 else 1)

    t = bsz * seq
    tm = _pick(t, 512)
    h1, h1p = _outproj(yc.reshape(t, c), yg.reshape(t, gw), x.reshape(t, d), w_out[0].astype(_BF16),
                       ln1_g, ln1_b, tm=tm)

    tt = _pick(t, 256)
    wr_t = w_router[0].T
    wr_hi = wr_t.astype(_BF16)
    wr_lo = (wr_t - wr_hi.astype(_F32)).astype(_BF16)
    shared_w = (ws_gate[0].astype(_BF16), ws_up[0].astype(_BF16), ws_down[0].astype(_BF16))
    h1p3 = h1p.reshape(t, QUAD, 128)

    parts = MOE_PARTS if t % (MOE_PARTS * tt * SC_WORKERS) == 0 else 1
    tp = t // parts
    nb = tp * TOP_K // ROW_BLOCK + N_EXPERTS
    out = None
    for part in range(parts):
        tile0 = part * (tp // tt)
        idx, gate, rank, cnt = _router(h1, wr_hi, wr_lo, b_router[0][:, None], tt=tt, tile0=tile0, t=tp)
        counts = cnt[:, 0].astype(_I32)
        pcounts = (counts + ROW_BLOCK - 1) // ROW_BLOCK * ROW_BLOCK
        pends = jnp.cumsum(pcounts)
        pstarts = pends - pcounts
        pos = _position(idx, rank, pstarts[:, None].astype(_I32), tt=tt)
        nwin = tt // SC_WINDOW
        pos3 = pos.reshape(tp // tt, TOP_K, nwin, SC_WINDOW).transpose(0, 2, 1, 3)
        pos3 = pos3.reshape(tp // SC_WINDOW, TOP_K, SC_WINDOW)
        xs = _sc_scatter(h1p3, pos3, nb * ROW_BLOCK, part * tp)
        xs = _padfill(counts, pstarts.astype(_I32), pcounts.astype(_I32), xs.reshape(nb * ROW_BLOCK * QUAD, 128))
        ys = _ffn((pstarts // ROW_BLOCK).astype(_I32), (pcounts // ROW_BLOCK).astype(_I32),
                  (pends[-1:] // ROW_BLOCK).astype(_I32), xs, w_gate[0], w_up[0], w_down[0])
        yg = _sc_gather(ys.reshape(nb * ROW_BLOCK, QUAD, 128), pos.reshape(tp * TOP_K))
        yg = yg.reshape(tp // tt, TOP_K, tt * QUAD, 128)
        out = _combine_stream(gate, h1, yg, *shared_w, ln2_g, ln2_b, out, tt=tt, tile0=tile0)
    return out.reshape(bsz, seq, d)
```

```python
import functools

import jax
import jax.numpy as jnp
from jax import lax
from jax.experimental import pallas as pl
from jax.experimental.pallas import tpu as pltpu
from jax.experimental.pallas import tpu_sc as plsc

_F32 = jnp.float32
_BF16 = jnp.bfloat16
_I32 = jnp.int32

D_MODEL = 1024
N_META = 16
CONV_WIDTH = 512
CONV_K = 3
GDN_HEADS = 4
GDN_HEAD_DIM = 128
GDN_WIDTH = GDN_HEADS * GDN_HEAD_DIM
GDN_CONV_K = 4
CHUNK = 64
N_EXPERTS = 256
TOP_K = 8
N_GROUPS = 8
TOPK_GROUPS = 4
E_PER_GROUP = N_EXPERTS // N_GROUPS
EXPERT_FF = 256
ROUTED_SCALE = 2.5
ROW_BLOCK = 256
DN_ALPHA = 2.0 ** 0.25
NORM_EPS = 1e-5
HALF = D_MODEL // 2
QUAD = HALF // 128
STACK = GDN_HEADS * CHUNK
HIST = 8
GDN_ROWS = 4
ISSUE_UNROLL = 8
SC_CORES = 2
SC_SUBCORES = 16
SC_WORKERS = SC_CORES * SC_SUBCORES
SC_CHUNK = 64
SC_RING = 2
SC_WINDOW = 128
MOE_PARTS = 2
RING = 8
IN_AHEAD = RING - 2

V7X_VMEM_BYTES = 64 * 1024 * 1024
VMEM_LIMIT = V7X_VMEM_BYTES - 8 * 1024 * 1024


def _cparams(*sem):
    return pltpu.CompilerParams(dimension_semantics=sem, vmem_limit_bytes=VMEM_LIMIT)


def _mm(a, b):
    return jnp.dot(a.astype(_BF16), b.astype(_BF16), preferred_element_type=_F32)


def _mm_nt(a, b):
    return lax.dot_general(a.astype(_BF16), b.astype(_BF16), (((1,), (1,)), ((), ())),
                           preferred_element_type=_F32)


def _mm_tn(a, b):
    return lax.dot_general(a.astype(_BF16), b.astype(_BF16), (((0,), (0,)), ((), ())),
                           preferred_element_type=_F32)


def _sigmoid(x):
    return 1.0 / (1.0 + jnp.exp(-x))


def _silu(x):
    return x * _sigmoid(x)


def _softplus(x):
    return jnp.maximum(x, 0.0) + jnp.log1p(jnp.exp(-jnp.abs(x)))


def _pack_halves(y):
    return pltpu.pack_elementwise([y[:, :HALF], y[:, HALF:]], packed_dtype=_BF16)


def _store_rows(ref, packed):
    r = packed.shape[0]
    for c in range(QUAD):
        ref[pl.ds(c, r, stride=QUAD), :] = packed[:, c * 128:(c + 1) * 128]


def _load_rows(ref, r):
    return jnp.concatenate([ref[pl.ds(c, r, stride=QUAD), :] for c in range(QUAD)], axis=1)


def _unpack_halves(p):
    lo = pltpu.unpack_elementwise(p, index=0, packed_dtype=_BF16, unpacked_dtype=_F32)
    hi = pltpu.unpack_elementwise(p, index=1, packed_dtype=_BF16, unpacked_dtype=_F32)
    return lo, hi


def _layer_norm(h, g, b):
    mu = jnp.mean(h, axis=-1, keepdims=True)
    d = h - mu
    var = jnp.mean(d * d, axis=-1, keepdims=True)
    return d * lax.rsqrt(var + NORM_EPS) * g + b


def _premix_body(x_ref, tails_ref, wa_ref, wq_ref, wz_ref, wbd_ref, wbdt_ref, cw_ref, cnw_ref,
                 gcw_ref, prow_ref, pcol_ref,
                 yc_ref, q_ref, k_ref, v_ref, z_ref, bgc_ref, bgr_ref, tout_ref, ext_ref, *, lt):
    cw_ = CONV_WIDTH

    @pl.when(pl.program_id(1) == 0)
    def _():
        ext_ref[0:HIST, :] = tails_ref[...]

    xb = x_ref[0].astype(_BF16)
    pa = jnp.dot(xb, wa_ref[...], preferred_element_type=_F32)
    gate_b = pa[:, 0:cw_]
    u = pa[:, cw_:2 * cw_] * pa[:, 2 * cw_:3 * cw_]
    ext_ref[HIST:HIST + lt, 0:cw_] = u
    pq = jnp.dot(xb, wq_ref[...], preferred_element_type=_F32)
    ext_ref[HIST:HIST + lt, cw_:] = pq

    cw = cw_ref[...]
    ca = u * cw[CONV_K - 1:CONV_K, :]
    for j in range(CONV_K - 1):
        ca = ca + ext_ref[pl.ds(HIST - (CONV_K - 1) + j, lt), 0:cw_] * cw[j:j + 1, :]
    yc = gate_b * ca
    ms = jnp.mean(yc * yc, axis=-1, keepdims=True)
    yc_ref[0] = (yc * lax.rsqrt(ms + NORM_EPS) * cnw_ref[...]).astype(_BF16)

    gcw = gcw_ref[...]
    cq = pq * gcw[GDN_CONV_K - 1:GDN_CONV_K, :]
    for j in range(GDN_CONV_K - 1):
        cq = cq + ext_ref[pl.ds(HIST - (GDN_CONV_K - 1) + j, lt), cw_:] * gcw[j:j + 1, :]
    s = _silu(cq)
    for h in range(GDN_HEADS):
        lo, hi = h * GDN_HEAD_DIM, (h + 1) * GDN_HEAD_DIM
        qh = s[:, lo:hi]
        kh = s[:, GDN_WIDTH + lo:GDN_WIDTH + hi]
        qn = qh * lax.rsqrt(jnp.sum(qh * qh, axis=-1, keepdims=True) + 1e-6)
        kn = kh * lax.rsqrt(jnp.sum(kh * kh, axis=-1, keepdims=True) + 1e-6)
        q_ref[0, :, lo:hi] = (qn * (GDN_HEAD_DIM ** -0.5)).astype(_BF16)
        k_ref[0, :, lo:hi] = kn.astype(_BF16)
    v_ref[0] = s[:, 2 * GDN_WIDTH:].astype(_BF16)
    z_ref[0] = jnp.dot(xb, wz_ref[...], preferred_element_type=_F32).astype(_BF16)

    bdc = jnp.dot(xb, wbd_ref[...], preferred_element_type=_F32)
    prow = prow_ref[...]
    g_c = -jnp.exp(prow[0:1, :]) * _softplus(bdc + prow[1:2, :])
    lane = lax.broadcasted_iota(_I32, bdc.shape, 1)
    bgc_ref[0] = jnp.where(lane < GDN_HEADS, _sigmoid(bdc), g_c)
    bdr = _mm_nt(wbdt_ref[...], xb)
    pcol = pcol_ref[...]
    g_r = -jnp.exp(pcol[:, 0:1]) * _softplus(bdr + pcol[:, 1:2])
    row = lax.broadcasted_iota(_I32, bdr.shape, 0)
    bgr_ref[0] = jnp.where(row < GDN_HEADS, _sigmoid(bdr), g_r)

    tail = ext_ref[lt:lt + HIST, :]
    ext_ref[0:HIST, :] = tail
    tout_ref[0] = tail


def _premix(x, tails, wts, *, lt):
    bsz, seq, d = x.shape
    assert seq % lt == 0
    grid = (bsz, seq // lt)
    full = lambda a: pl.BlockSpec(a.shape, lambda b, j: (0,) * a.ndim)
    tok = lambda w: pl.BlockSpec((1, lt, w), lambda b, j: (b, j, 0))
    (wa, wq, wz, wbd, wbdt, cw, cnw, gcw, prow, pcol) = wts
    ext_w = CONV_WIDTH + 3 * GDN_WIDTH
    out_shape = (
        jax.ShapeDtypeStruct((bsz, seq, CONV_WIDTH), _BF16),
        jax.ShapeDtypeStruct((bsz, seq, GDN_WIDTH), _BF16),
        jax.ShapeDtypeStruct((bsz, seq, GDN_WIDTH), _BF16),
        jax.ShapeDtypeStruct((bsz, seq, GDN_WIDTH), _BF16),
        jax.ShapeDtypeStruct((bsz, seq, GDN_WIDTH), _BF16),
        jax.ShapeDtypeStruct((bsz, seq, 128), _F32),
        jax.ShapeDtypeStruct((bsz, 8, seq), _F32),
        jax.ShapeDtypeStruct((bsz, HIST, ext_w), _F32),
    )
    out_specs = (tok(CONV_WIDTH), tok(GDN_WIDTH), tok(GDN_WIDTH), tok(GDN_WIDTH), tok(GDN_WIDTH),
                 tok(128), pl.BlockSpec((1, 8, lt), lambda b, j: (b, 0, j)),
                 pl.BlockSpec((1, HIST, ext_w), lambda b, j: (b, 0, 0)))
    return pl.pallas_call(
        functools.partial(_premix_body, lt=lt),
        grid=grid,
        in_specs=[tok(d), full(tails)] + [full(w) for w in wts],
        out_specs=out_specs,
        out_shape=out_shape,
        scratch_shapes=[pltpu.VMEM((HIST + lt, ext_w), _F32)],
        compiler_params=_cparams("arbitrary", "arbitrary"),
        name="premix",
    )(x, tails, *wts)


def _cumsum_rows(x):
    row = lax.broadcasted_iota(_I32, x.shape, 0)
    s = 1
    while s < x.shape[0]:
        x = x + jnp.where(row >= s, pltpu.roll(x, s, 0), 0.0)
        s *= 2
    return x


def _cumsum_lanes_seg(x):
    lane = lax.broadcasted_iota(_I32, x.shape, 1) & (CHUNK - 1)
    s = 1
    while s < CHUNK:
        x = x + jnp.where(lane >= s, pltpu.roll(x, s, 1), 0.0)
        s *= 2
    return x


def _stack_heads(a):
    return jnp.concatenate([a[:, h * GDN_HEAD_DIM:(h + 1) * GDN_HEAD_DIM] for h in range(GDN_HEADS)], axis=0)


def _gdn_body(q_ref, k_ref, v_ref, z_ref, bgc_ref, grow_ref, s0_ref, gnw_ref,
              y_ref, sout_ref, s_ref, *, nc, nbb):
    @pl.when(pl.program_id(1) == 0)
    def _():
        for r in range(nbb):
            s_ref[r] = s0_ref[...]

    ri = lax.broadcasted_iota(_I32, (STACK, STACK), 0)
    ci = lax.broadcasted_iota(_I32, (STACK, STACK), 1)
    same64 = (ri >> 6) == (ci >> 6)
    same32 = (ri >> 5) == (ci >> 5)
    same16 = (ri >> 4) == (ci >> 4)
    low_incl = same64 & (ri >= ci)
    low_strict = same64 & (ri > ci)
    gnw = gnw_ref[...]

    def chunk_row(r, c):
        off = pl.multiple_of(c * CHUNK, CHUNK)
        q_all = _stack_heads(q_ref[r, pl.ds(off, CHUNK), :].astype(_F32))
        k_all = _stack_heads(k_ref[r, pl.ds(off, CHUNK), :].astype(_F32))
        v_all = _stack_heads(v_ref[r, pl.ds(off, CHUNK), :].astype(_F32))
        bgc = bgc_ref[r, pl.ds(off, CHUNK), :]
        gcs = _cumsum_rows(bgc)
        hd = (CHUNK, GDN_HEAD_DIM)
        beta_b = jnp.concatenate(
            [jnp.broadcast_to(bgc[:, h:h + 1], hd) for h in range(GDN_HEADS)], axis=0)
        gc_b = jnp.concatenate(
            [jnp.broadcast_to(gcs[:, GDN_HEADS + h:GDN_HEADS + h + 1], hd) for h in range(GDN_HEADS)], axis=0)
        gl = [gcs[CHUNK - 1:CHUNK, GDN_HEADS + h:GDN_HEADS + h + 1] for h in range(GDN_HEADS)]
        gl_b = jnp.concatenate([jnp.broadcast_to(g1, hd) for g1 in gl], axis=0)
        gcr = _cumsum_lanes_seg(jnp.broadcast_to(grow_ref[r, c], (8, STACK)))[0:1, :]

        diff = jnp.concatenate([gc_b, gc_b], axis=1) - gcr
        decay = jnp.exp(jnp.where(low_incl, diff, -1e30))
        kb = k_all * beta_b
        a1 = _mm_nt(jnp.concatenate([kb, q_all], axis=0), k_all)
        yield
        m = jnp.where(low_strict, a1[:STACK] * decay, 0.0)
        attn = a1[STACK:] * decay

        l16 = jnp.where(same16, m, 0.0)
        c1 = jnp.where(same32 & jnp.logical_not(same16), m, 0.0)
        c2 = jnp.where(same32, 0.0, m)
        p2 = _mm(l16, l16)
        yield
        p4 = _mm(p2, p2)
        t = _mm(l16, p2)
        yield
        na = p2 - l16 - t
        p8 = _mm(p4, p4)
        t = _mm(na, p4)
        yield
        nb = na + p4 + t
        t = _mm(nb, p8)
        yield
        ncm = nb + p8 + t
        t = _mm(c1, ncm)
        yield
        y1 = c1 + t
        t = _mm(ncm, y1)
        yield
        n1 = ncm - y1 - t
        t = _mm(c2, n1)
        yield
        y2 = c2 + t
        t = _mm(n1, y2)
        yield
        nt = n1 - y2 - t

        egc = jnp.exp(gc_b)
        rhs = jnp.concatenate([v_all * beta_b, kb * egc], axis=1)
        t = _mm(nt, rhs)
        yield
        uw = rhs + t
        u_all = uw[:, :GDN_HEAD_DIM]
        w_all = uw[:, GDN_HEAD_DIM:]
        qd = q_all * egc
        kd = k_all * jnp.exp(gl_b - gc_b)

        bs = []
        for h in range(GDN_HEADS):
            r0, r1 = h * CHUNK, (h + 1) * CHUNK
            bs.append(_mm(jnp.concatenate([w_all[r0:r1], qd[r0:r1]], axis=0), s_ref[r, h]))
        yield
        vn = [u_all[h * CHUNK:(h + 1) * CHUNK] - bs[h][:CHUNK] for h in range(GDN_HEADS)]
        vn_all = jnp.concatenate(vn, axis=0)
        t = _mm(attn, vn_all)
        ds = [_mm_tn(kd[h * CHUNK:(h + 1) * CHUNK], vn[h]) for h in range(GDN_HEADS)]
        yield
        o_all = jnp.concatenate([b[CHUNK:] for b in bs], axis=0) + t
        for h in range(GDN_HEADS):
            r0, r1 = h * CHUNK, (h + 1) * CHUNK
            s_ref[r, h] = s_ref[r, h] * jnp.exp(gl[h]) + ds[h]
            o = o_all[r0:r1]
            zz = z_ref[r, pl.ds(off, CHUNK), h * GDN_HEAD_DIM:(h + 1) * GDN_HEAD_DIM].astype(_F32)
            on = o * lax.rsqrt(jnp.mean(o * o, axis=-1, keepdims=True) + NORM_EPS) * gnw
            y_ref[r, pl.ds(off, CHUNK), h * GDN_HEAD_DIM:(h + 1) * GDN_HEAD_DIM] = (on * _silu(zz)).astype(_BF16)

    def chunk(c, carry):
        live = [chunk_row(r, c) for r in range(nbb)]
        while live:
            live = [g for g in live if next(g, live) is not live]
        return carry

    lax.fori_loop(0, nc, chunk, 0)
    sout_ref[...] = s_ref[...]


def _gdn(q, k, v, z, bgc, grow, s0, gnw, *, lg, nbb):
    bsz, seq, _ = q.shape
    assert seq % lg == 0 and lg % CHUNK == 0 and bsz % nbb == 0
    nc = lg // CHUNK
    tok = lambda w: pl.BlockSpec((nbb, lg, w), lambda b, j: (b, j, 0))
    full = lambda a: pl.BlockSpec(a.shape, lambda b, j: (0,) * a.ndim)
    st = (nbb, GDN_HEADS, GDN_HEAD_DIM, GDN_HEAD_DIM)
    return pl.pallas_call(
        functools.partial(_gdn_body, nc=nc, nbb=nbb),
        grid=(bsz // nbb, seq // lg),
        in_specs=[tok(GDN_WIDTH)] * 4 + [tok(128), pl.BlockSpec((nbb, nc, 1, STACK), lambda b, j: (b, j, 0, 0)),
                                           full(s0), full(gnw)],
        out_specs=(tok(GDN_WIDTH), pl.BlockSpec(st, lambda b, j: (b, 0, 0, 0))),
        out_shape=(jax.ShapeDtypeStruct((bsz, seq, GDN_WIDTH), _BF16),
                   jax.ShapeDtypeStruct((bsz, GDN_HEADS, GDN_HEAD_DIM, GDN_HEAD_DIM), _F32)),
        scratch_shapes=[pltpu.VMEM(st, _F32)],
        compiler_params=_cparams("arbitrary", "arbitrary"),
        name="gdn",
    )(q, k, v, z, bgc, grow, s0, gnw)


def _outproj_body(yc_ref, yg_ref, x_ref, wo_ref, g_ref, b_ref, h1_ref, h1p_ref):
    mix = (jnp.dot(yc_ref[...], wo_ref[0:CONV_WIDTH, :], preferred_element_type=_F32)
           + jnp.dot(yg_ref[...], wo_ref[CONV_WIDTH:, :], preferred_element_type=_F32))
    h1 = _layer_norm(DN_ALPHA * x_ref[...] + mix, g_ref[...], b_ref[...])
    h1_ref[...] = h1
    _store_rows(h1p_ref, _pack_halves(h1))


def _outproj(yc, yg, x2d, wo, g, b, *, tm):
    t = x2d.shape[0]
    assert t % tm == 0
    row = lambda w: pl.BlockSpec((tm, w), lambda i: (i, 0))
    full = lambda a: pl.BlockSpec(a.shape, lambda i: (0,) * a.ndim)
    return pl.pallas_call(
        _outproj_body,
        grid=(t // tm,),
        in_specs=[row(CONV_WIDTH), row(GDN_WIDTH), row(D_MODEL), full(wo), full(g), full(b)],
        out_specs=(row(D_MODEL), pl.BlockSpec((tm * QUAD, 128), lambda i: (i, 0))),
        out_shape=(jax.ShapeDtypeStruct((t, D_MODEL), _F32), jax.ShapeDtypeStruct((t * QUAD, 128), jnp.uint32)),
        compiler_params=_cparams("arbitrary"),
        name="outproj",
    )(yc, yg, x2d, wo, g, b)


def _router_body(h1_ref, wh_ref, wl_ref, br_ref, idx_ref, gate_ref, rank_ref, cnt_ref, carry_ref, *, tt):
    @pl.when(pl.program_id(0) == 0)
    def _():
        carry_ref[...] = jnp.zeros_like(carry_ref)

    x = h1_ref[...]
    xh = x.astype(_BF16)
    xl = (x - xh.astype(_F32)).astype(_BF16)
    wh = wh_ref[...]
    logits = _mm_nt(wh, xh) + _mm_nt(wh, xl) + _mm_nt(wl_ref[...], xh)
    scores = _sigmoid(logits)
    sel = scores + br_ref[...]
    ninf = -jnp.inf

    r32 = lax.broadcasted_iota(_I32, (E_PER_GROUP, tt), 0)
    gsc = []
    for g in range(N_GROUPS):
        xg = sel[g * E_PER_GROUP:(g + 1) * E_PER_GROUP]
        m1 = jnp.max(xg, axis=0, keepdims=True)
        i1 = jnp.min(jnp.where(xg == m1, r32, E_PER_GROUP), axis=0, keepdims=True)
        m2 = jnp.max(jnp.where(r32 == i1, ninf, xg), axis=0, keepdims=True)
        gsc.append(m1 + m2)
    work = jnp.concatenate(gsc, axis=0)
    r8 = lax.broadcasted_iota(_I32, (N_GROUPS, tt), 0)
    gkeep = jnp.zeros((N_GROUPS, tt), _F32)
    for _ in range(TOPK_GROUPS):
        m = jnp.max(work, axis=0, keepdims=True)
        gi = jnp.min(jnp.where(work == m, r8, N_GROUPS), axis=0, keepdims=True)
        pick = r8 == gi
        gkeep = jnp.where(pick, 1.0, gkeep)
        work = jnp.where(pick, ninf, work)
    selm = jnp.concatenate(
        [jnp.where(gkeep[g:g + 1] > 0.5, sel[g * E_PER_GROUP:(g + 1) * E_PER_GROUP], ninf)
         for g in range(N_GROUPS)], axis=0)

    re = lax.broadcasted_iota(_I32, (N_EXPERTS, tt), 0)
    msel = jnp.zeros((N_EXPERTS, tt), _F32)
    idxs, gates = [], []
    for _ in range(TOP_K):
        m = jnp.max(selm, axis=0, keepdims=True)
        ii = jnp.min(jnp.where(selm == m, re, N_EXPERTS), axis=0, keepdims=True)
        hit = re == ii
        idxs.append(ii)
        gates.append(jnp.sum(jnp.where(hit, scores, 0.0), axis=0, keepdims=True))
        selm = jnp.where(hit, ninf, selm)
        msel = jnp.where(hit, 1.0, msel)
    gate = jnp.concatenate(gates, axis=0)
    gate_ref[...] = gate / jnp.sum(gate, axis=0, keepdims=True) * ROUTED_SCALE
    idx_ref[...] = jnp.concatenate(idxs, axis=0)

    ta = lax.broadcasted_iota(_I32, (tt, tt), 0)
    tb = lax.broadcasted_iota(_I32, (tt, tt), 1)
    earlier = jnp.where(ta < tb, 1.0, 0.0)
    carry = carry_ref[...]
    rank_all = _mm(msel, earlier) + carry[:, 0:1]
    rank_ref[...] = jnp.concatenate(
        [jnp.sum(jnp.where(re == ii, rank_all, 0.0), axis=0, keepdims=True) for ii in idxs],
        axis=0).astype(_I32)
    carry = carry + jnp.sum(msel, axis=1, keepdims=True)
    carry_ref[...] = carry
    cnt_ref[...] = carry


def _router(h1, wh, wl, br, *, tt, tile0, t):
    assert t % tt == 0
    full = lambda a: pl.BlockSpec(a.shape, lambda i: (0,) * a.ndim)
    kt = pl.BlockSpec((TOP_K, tt), lambda i: (0, i))
    return pl.pallas_call(
        functools.partial(_router_body, tt=tt),
        grid=(t // tt,),
        in_specs=[pl.BlockSpec((tt, D_MODEL), lambda i: (i + tile0, 0)), full(wh), full(wl), full(br)],
        out_specs=(kt, kt, kt, pl.BlockSpec((N_EXPERTS, 128), lambda i: (0, 0))),
        out_shape=(jax.ShapeDtypeStruct((TOP_K, t), _I32), jax.ShapeDtypeStruct((TOP_K, t), _F32),
                   jax.ShapeDtypeStruct((TOP_K, t), _I32), jax.ShapeDtypeStruct((N_EXPERTS, 128), _F32)),
        scratch_shapes=[pltpu.VMEM((N_EXPERTS, 128), _F32)],
        compiler_params=_cparams("arbitrary"),
        name="router",
    )(h1, wh, wl, br)


def _position_body(idx_ref, rank_ref, pstart_ref, pos_ref, *, tt):
    re = lax.broadcasted_iota(_I32, (N_EXPERTS, tt), 0)
    ps = pstart_ref[...]
    idx = idx_ref[...]
    rows = [jnp.sum(jnp.where(re == idx[k:k + 1], ps, 0), axis=0, keepdims=True) for k in range(TOP_K)]
    pos_ref[0] = jnp.concatenate(rows, axis=0) + rank_ref[...]


def _position(idx, rank, pstart, *, tt):
    t = idx.shape[1]
    kt = pl.BlockSpec((TOP_K, tt), lambda i: (0, i))
    return pl.pallas_call(
        functools.partial(_position_body, tt=tt),
        grid=(t // tt,),
        in_specs=[kt, kt, pl.BlockSpec(pstart.shape, lambda i: (0, 0))],
        out_specs=pl.BlockSpec((1, TOP_K, tt), lambda i: (i, 0, 0)),
        out_shape=jax.ShapeDtypeStruct((t // tt, TOP_K, tt), _I32),
        compiler_params=_cparams("arbitrary"),
        name="position",
    )(idx, rank, pstart)


def _dispatch_body(cnt_ref, pst_ref, pcn_ref, pos_hbm, h1p_ref, xs_out, pos_smem, psem, sem, zbuf, zsem, *, tt):
    i = pl.program_id(0)
    cp = pltpu.make_async_copy(pos_hbm.at[i], pos_smem, psem)
    cp.start()

    @pl.when(i == 0)
    def _():
        zbuf[...] = jnp.zeros_like(zbuf)

        def pad_runs(e, act):
            pad = pcn_ref[e] - cnt_ref[e]
            base = pst_ref[e] + cnt_ref[e]
            for b in range(ROW_BLOCK.bit_length() - 1):
                n = 1 << b

                @pl.when(((pad >> b) & 1) == 1)
                def _():
                    off = base + (pad & (n - 1))
                    act(pltpu.make_async_copy(zbuf.at[pl.ds(0, QUAD * n)],
                                              xs_out.at[pl.ds(QUAD * off, QUAD * n)], zsem))

        def start_all(e, c):
            pad_runs(e, lambda d: d.start())
            return c

        def wait_all(e, c):
            pad_runs(e, lambda d: d.wait())
            return c

        lax.fori_loop(0, N_EXPERTS, start_all, 0)
        lax.fori_loop(0, N_EXPERTS, wait_all, 0)

    cp.wait()

    def row_copy(t, k):
        return pltpu.make_async_copy(h1p_ref.at[pl.ds(QUAD * t, QUAD)],
                                     xs_out.at[pl.ds(QUAD * pos_smem[k * tt + t], QUAD)], sem)

    def issue(t, c):
        for k in range(TOP_K):
            row_copy(t, k).start(priority=k % 2)
        return c

    lax.fori_loop(0, tt, issue, 0, unroll=ISSUE_UNROLL)

    for k in range(TOP_K):
        pltpu.make_async_copy(h1p_ref, xs_out.at[pl.ds(0, QUAD * tt)], sem).wait()


def _dispatch(counts, pstarts, pcounts, pos_tiles, h1p, n_rows, *, tt):
    t = h1p.shape[0] // QUAD
    grid_spec = pltpu.PrefetchScalarGridSpec(
        num_scalar_prefetch=3,
        grid=(t // tt,),
        in_specs=[pl.BlockSpec(memory_space=pl.ANY), pl.BlockSpec((tt * QUAD, 128), lambda i, *_: (i, 0))],
        out_specs=pl.BlockSpec(memory_space=pl.ANY),
        scratch_shapes=[pltpu.SMEM((TOP_K * tt,), _I32), pltpu.SemaphoreType.DMA, pltpu.SemaphoreType.DMA,
                        pltpu.VMEM((QUAD * ROW_BLOCK // 2, 128), jnp.uint32), pltpu.SemaphoreType.DMA],
    )
    return pl.pallas_call(
        functools.partial(_dispatch_body, tt=tt),
        grid_spec=grid_spec,
        out_shape=jax.ShapeDtypeStruct((n_rows * QUAD, 128), jnp.uint32),
        compiler_params=_cparams("arbitrary"),
        name="dispatch",
    )(counts, pstarts, pcounts, pos_tiles, h1p)


def _ffn_body(blk0_ref, nblk_ref, ntot_ref, xs_hbm, wg_ref, wu_ref, wd_ref, ys_hbm,
              xbuf, ybuf, sem_in, sem_out, wgu_bf, wd_bf):
    e = pl.program_id(0)
    nblk = nblk_ref[e]
    blk0 = blk0_ref[e]
    ntot = ntot_ref[0]

    blk_rows = ROW_BLOCK * QUAD

    def rows(g):
        return pl.ds(pl.multiple_of(g * blk_rows, blk_rows), blk_rows)

    def in_start(g, slot):
        pltpu.make_async_copy(xs_hbm.at[rows(g)], xbuf.at[slot], sem_in.at[slot]).start()

    def in_wait(slot):
        pltpu.make_async_copy(xs_hbm.at[rows(0)], xbuf.at[slot], sem_in.at[slot]).wait()

    def out_start(g, slot):
        pltpu.make_async_copy(ybuf.at[slot], ys_hbm.at[rows(g)], sem_out.at[slot]).start()

    def out_wait(slot):
        pltpu.make_async_copy(ybuf.at[slot], ys_hbm.at[rows(0)], sem_out.at[slot]).wait()

    @pl.when(e == 0)
    def _():
        for i in range(IN_AHEAD):
            @pl.when(i < ntot)
            def _():
                in_start(i, i)

    @pl.when(nblk > 0)
    def _():
        wgu_bf[:, 0:EXPERT_FF] = wg_ref[0].astype(_BF16)
        wgu_bf[:, EXPERT_FF:] = wu_ref[0].astype(_BF16)
        wd_bf[...] = wd_ref[0].astype(_BF16)

        def acquire(g):
            slot = g & (RING - 1)
            in_wait(slot)

            @pl.when(g + IN_AHEAD < ntot)
            def _():
                in_start(g + IN_AHEAD, (g + IN_AHEAD) & (RING - 1))

            @pl.when(g >= RING)
            def _():
                out_wait(slot)

            return slot

        def compute(slot):
            lo, hi = _unpack_halves(_load_rows(xbuf.at[slot], ROW_BLOCK))
            a = jnp.dot(lo.astype(_BF16), wgu_bf[0:HALF, :], preferred_element_type=_F32)
            yield
            gu = a + jnp.dot(hi.astype(_BF16), wgu_bf[HALF:, :], preferred_element_type=_F32)
            yield
            h = (_silu(gu[:, :EXPERT_FF]) * gu[:, EXPERT_FF:]).astype(_BF16)
            y = jnp.dot(h, wd_bf[...], preferred_element_type=_F32)
            yield
            _store_rows(ybuf.at[slot], _pack_halves(y))

        def run(gs):
            slots = [acquire(g) for g in gs]
            live = [compute(s) for s in slots]
            while live:
                live = [c for c in live if next(c, live) is not live]
            for g, s in zip(gs, slots):
                out_start(g, s)

        def pair(j, carry):
            run([blk0 + 2 * j, blk0 + 2 * j + 1])
            return carry

        lax.fori_loop(0, nblk // 2, pair, 0)

        @pl.when((nblk & 1) == 1)
        def _():
            run([blk0 + nblk - 1])

    @pl.when(e == N_EXPERTS - 1)
    def _():
        for i in range(RING):
            @pl.when(i < ntot)
            def _():
                out_wait((ntot - 1 - i) & (RING - 1))


def _ffn(blk0, nblk, ntot, xs, wg, wu, wd):
    grid_spec = pltpu.PrefetchScalarGridSpec(
        num_scalar_prefetch=3,
        grid=(N_EXPERTS,),
        in_specs=[pl.BlockSpec(memory_space=pl.ANY),
                  pl.BlockSpec((1, D_MODEL, EXPERT_FF), lambda e, *_: (e, 0, 0)),
                  pl.BlockSpec((1, D_MODEL, EXPERT_FF), lambda e, *_: (e, 0, 0)),
                  pl.BlockSpec((1, EXPERT_FF, D_MODEL), lambda e, *_: (e, 0, 0))],
        out_specs=pl.BlockSpec(memory_space=pl.ANY),
        scratch_shapes=[pltpu.VMEM((RING, ROW_BLOCK * QUAD, 128), jnp.uint32),
                        pltpu.VMEM((RING, ROW_BLOCK * QUAD, 128), jnp.uint32),
                        pltpu.SemaphoreType.DMA((RING,)), pltpu.SemaphoreType.DMA((RING,)),
                        pltpu.VMEM((D_MODEL, 2 * EXPERT_FF), _BF16), pltpu.VMEM((EXPERT_FF, D_MODEL), _BF16)],
    )
    return pl.pallas_call(
        _ffn_body,
        grid_spec=grid_spec,
        out_shape=jax.ShapeDtypeStruct(xs.shape, jnp.uint32),
        compiler_params=_cparams("arbitrary"),
        name="ffn",
    )(blk0, nblk, ntot, xs, wg, wu, wd)


def _combine_body(pos_hbm, gate_ref, h1_ref, ys_hbm, wsg_ref, wsu_ref, wsd_ref, g_ref, b_ref,
                  out_ref, pos_smem, psem, ybuf, sem, *, tt):
    i = pl.program_id(0)
    cp = pltpu.make_async_copy(pos_hbm.at[i], pos_smem, psem)
    cp.start()
    cp.wait()

    def row_copy(t, k):
        return pltpu.make_async_copy(ys_hbm.at[pl.ds(QUAD * pos_smem[k * tt + t], QUAD)],
                                     ybuf.at[k, pl.ds(QUAD * t, QUAD)], sem)

    def issue(t, c):
        for k in range(TOP_K):
            row_copy(t, k).start(priority=k % 2)
        return c

    lax.fori_loop(0, tt, issue, 0, unroll=ISSUE_UNROLL)

    x = h1_ref[...]
    xb = x.astype(_BF16)
    shared = _mm(_silu(_mm(xb, wsg_ref[...])) * _mm(xb, wsu_ref[...]), wsd_ref[...])

    for k in range(TOP_K):
        pltpu.make_async_copy(ys_hbm.at[pl.ds(0, QUAD * tt)], ybuf.at[k], sem).wait()

    gcol = gate_ref[...].T
    acc_lo = jnp.zeros((tt, HALF), _F32)
    acc_hi = jnp.zeros((tt, HALF), _F32)
    for k in range(TOP_K):
        lo, hi = _unpack_halves(_load_rows(ybuf.at[k], tt))
        acc_lo = acc_lo + gcol[:, k:k + 1] * lo
        acc_hi = acc_hi + gcol[:, k:k + 1] * hi
    routed = jnp.concatenate([acc_lo, acc_hi], axis=1)
    out_ref[...] = _layer_norm(DN_ALPHA * x + (routed + shared), g_ref[...], b_ref[...])


def _combine(pos_tiles, gate, h1, ys, wsg, wsu, wsd, g, b, *, tt):
    t = h1.shape[0]
    full = lambda a: pl.BlockSpec(a.shape, lambda i: (0,) * a.ndim)
    return pl.pallas_call(
        functools.partial(_combine_body, tt=tt),
        grid=(t // tt,),
        in_specs=[pl.BlockSpec(memory_space=pl.ANY), pl.BlockSpec((TOP_K, tt), lambda i: (0, i)),
                  pl.BlockSpec((tt, D_MODEL), lambda i: (i, 0)), pl.BlockSpec(memory_space=pl.ANY),
                  full(wsg), full(wsu), full(wsd), full(g), full(b)],
        out_specs=pl.BlockSpec((tt, D_MODEL), lambda i: (i, 0)),
        out_shape=jax.ShapeDtypeStruct((t, D_MODEL), _F32),
        scratch_shapes=[pltpu.SMEM((TOP_K * tt,), _I32), pltpu.SemaphoreType.DMA,
                        pltpu.VMEM((TOP_K, tt * QUAD, 128), jnp.uint32), pltpu.SemaphoreType.DMA],
        compiler_params=_cparams("arbitrary"),
        name="combine",
    )(pos_tiles, gate, h1, ys, wsg, wsu, wsd, g, b)


def _sc_gather(table, idx):
    b = idx.shape[0]
    nchunk = b // (SC_WORKERS * SC_CHUNK)
    assert nchunk * SC_WORKERS * SC_CHUNK == b and nchunk % SC_RING == 0
    idx2 = idx.reshape(SC_WORKERS * nchunk, SC_CHUNK)
    row = table.shape[1:]
    mesh = plsc.VectorSubcoreMesh(core_axis_name="c", subcore_axis_name="s",
                                  num_cores=SC_CORES, num_subcores=SC_SUBCORES)

    @functools.partial(
        pl.kernel, mesh=mesh,
        out_type=jax.ShapeDtypeStruct((b,) + row, table.dtype),
        scratch_types=[pltpu.VMEM((nchunk, SC_CHUNK), _I32), pltpu.VMEM((SC_RING, SC_CHUNK) + row, table.dtype),
                       pltpu.SemaphoreType.DMA((SC_RING,)), pltpu.SemaphoreType.DMA((SC_RING,))],
        name="sc_gather",
    )
    def gather(table_hbm, idx_hbm, out_hbm, idx_v, rows_v, sem_g, sem_w):
        wid = lax.axis_index("s") * SC_CORES + lax.axis_index("c")
        c0 = wid * nchunk
        pltpu.sync_copy(idx_hbm.at[pl.ds(pl.multiple_of(c0, nchunk), nchunk)], idx_v)

        def fetch(i, s):
            return pltpu.make_async_copy(table_hbm.at[idx_v.at[i]], rows_v.at[s], sem_g.at[s])

        def flush(i, s):
            rows = pl.ds(pl.multiple_of((c0 + i) * SC_CHUNK, SC_CHUNK), SC_CHUNK)
            return pltpu.make_async_copy(rows_v.at[s], out_hbm.at[rows], sem_w.at[s])

        for s in range(SC_RING):
            fetch(s, s).start()

        @pl.loop(0, nchunk, step=SC_RING)
        def _(g):
            for s in range(SC_RING):
                i = g + s
                fetch(i, s).wait()
                flush(i, s).start()
                flush(i, s).wait()

                @pl.when(i + SC_RING < nchunk)
                def _():
                    fetch(i + SC_RING, s).start()

    return gather(table, idx2)


def _sc_scatter(rows, pos3, n_out, row0):
    nchunk, nk, w = pos3.shape
    per_w = nchunk // SC_WORKERS
    assert per_w * SC_WORKERS == nchunk and w <= 128 and row0 % w == 0 and rows.shape[0] >= row0 + nchunk * w
    row = rows.shape[1:]
    mesh = plsc.VectorSubcoreMesh(core_axis_name="c", subcore_axis_name="s",
                                  num_cores=SC_CORES, num_subcores=SC_SUBCORES)

    @functools.partial(
        pl.kernel, mesh=mesh,
        out_type=jax.ShapeDtypeStruct((n_out,) + row, rows.dtype),
        scratch_types=[pltpu.VMEM((nk, w), _I32), pltpu.VMEM((w,) + row, rows.dtype), pltpu.SemaphoreType.DMA],
        name="sc_scatter",
    )
    def scatter(rows_hbm, pos_hbm, out_hbm, idx_v, rows_v, sem):
        wid = lax.axis_index("s") * SC_CORES + lax.axis_index("c")

        @pl.loop(0, per_w)
        def _(i):
            c = wid * per_w + i
            pltpu.sync_copy(pos_hbm.at[c], idx_v)
            pltpu.sync_copy(rows_hbm.at[pl.ds(pl.multiple_of(row0 + c * w, w), w)], rows_v)
            copies = [pltpu.async_copy(rows_v, out_hbm.at[idx_v.at[k]], sem) for k in range(nk)]
            for cp in copies:
                cp.wait()

    return scatter(rows, pos3)


def _padfill_body(cnt_ref, pst_ref, pcn_ref, xs_in, xs_out, zbuf, zsem):
    del xs_in
    zbuf[...] = jnp.zeros_like(zbuf)

    def pad_runs(e, act):
        pad = pcn_ref[e] - cnt_ref[e]
        base = pst_ref[e] + cnt_ref[e]
        for b in range(ROW_BLOCK.bit_length() - 1):
            n = 1 << b

            @pl.when(((pad >> b) & 1) == 1)
            def _():
                off = base + (pad & (n - 1))
                act(pltpu.make_async_copy(zbuf.at[pl.ds(0, QUAD * n)],
                                          xs_out.at[pl.ds(QUAD * off, QUAD * n)], zsem))

    def start_all(e, c):
        pad_runs(e, lambda d: d.start())
        return c

    def wait_all(e, c):
        pad_runs(e, lambda d: d.wait())
        return c

    lax.fori_loop(0, N_EXPERTS, start_all, 0)
    lax.fori_loop(0, N_EXPERTS, wait_all, 0)


def _padfill(counts, pstarts, pcounts, xs):
    grid_spec = pltpu.PrefetchScalarGridSpec(
        num_scalar_prefetch=3,
        grid=(1,),
        in_specs=[pl.BlockSpec(memory_space=pl.ANY)],
        out_specs=pl.BlockSpec(memory_space=pl.ANY),
        scratch_shapes=[pltpu.VMEM((QUAD * ROW_BLOCK // 2, 128), jnp.uint32), pltpu.SemaphoreType.DMA],
    )
    return pl.pallas_call(
        _padfill_body,
        grid_spec=grid_spec,
        out_shape=jax.ShapeDtypeStruct(xs.shape, xs.dtype),
        input_output_aliases={3: 0},
        compiler_params=_cparams("arbitrary"),
        name="padfill",
    )(counts, pstarts, pcounts, xs)


def _combine_stream_body(gate_ref, h1_ref, yg_ref, wsg_ref, wsu_ref, wsd_ref, g_ref, b_ref, out_ref, *, tt):
    x = h1_ref[...]
    xb = x.astype(_BF16)
    shared = _mm(_silu(_mm(xb, wsg_ref[...])) * _mm(xb, wsu_ref[...]), wsd_ref[...])
    gcol = gate_ref[...].T
    acc_lo = jnp.zeros((tt, HALF), _F32)
    acc_hi = jnp.zeros((tt, HALF), _F32)
    for k in range(TOP_K):
        lo, hi = _unpack_halves(_load_rows(yg_ref.at[0, k], tt))
        acc_lo = acc_lo + gcol[:, k:k + 1] * lo
        acc_hi = acc_hi + gcol[:, k:k + 1] * hi
    routed = jnp.concatenate([acc_lo, acc_hi], axis=1)
    out_ref[...] = _layer_norm(DN_ALPHA * x + (routed + shared), g_ref[...], b_ref[...])


def _combine_stream(gate, h1, yg, wsg, wsu, wsd, g, b, out_prev, *, tt, tile0):
    t_all = h1.shape[0]
    t = gate.shape[1]
    full = lambda a: pl.BlockSpec(a.shape, lambda i: (0,) * a.ndim)
    in_specs = [pl.BlockSpec((TOP_K, tt), lambda i: (0, i)), pl.BlockSpec((tt, D_MODEL), lambda i: (i + tile0, 0)),
                pl.BlockSpec((1, TOP_K, tt * QUAD, 128), lambda i: (i, 0, 0, 0)),
                full(wsg), full(wsu), full(wsd), full(g), full(b)]
    args = [gate, h1, yg, wsg, wsu, wsd, g, b]
    aliases = {}
    body = functools.partial(_combine_stream_body, tt=tt)
    if out_prev is not None:
        in_specs.append(pl.BlockSpec(memory_space=pl.ANY))
        args.append(out_prev)
        aliases = {len(args) - 1: 0}
        body = lambda *refs: _combine_stream_body(*refs[:8], refs[9], tt=tt)
    return pl.pallas_call(
        body,
        grid=(t // tt,),
        in_specs=in_specs,
        out_specs=pl.BlockSpec((tt, D_MODEL), lambda i: (i + tile0, 0)),
        out_shape=jax.ShapeDtypeStruct((t_all, D_MODEL), _F32),
        input_output_aliases=aliases,
        compiler_params=_cparams("arbitrary"),
        name="combine",
    )(*args)


def _pick(n, pref):
    t = min(n, pref)
    while n % t:
        t -= CHUNK
    return t


def _mixer(x, tails, s0, wts, gnw, *, lt, lg, nbb):
    yc, q, k, v, z, bgc, bgr, tails_out = _premix(x, tails, wts, lt=lt)
    bsz, seq, _ = x.shape
    nch = seq // CHUNK
    grow = bgr[:, GDN_HEADS:2 * GDN_HEADS, :].reshape(bsz, GDN_HEADS, nch, CHUNK)
    grow = grow.transpose(0, 2, 1, 3).reshape(bsz, nch, 1, STACK)
    yg, s_out = _gdn(q, k, v, z, bgc, grow, s0, gnw, lg=lg, nbb=nbb)
    return yc, yg, tails_out, s_out


def kernel(x, meta_tokens, w_in, conv_w, conv_norm_w, gdn_conv_w, a_log, dt_bias, gdn_norm_w, w_out,
           ln1_g, ln1_b, w_router, b_router, w_gate, w_up, w_down, ws_gate, ws_up, ws_down, ln2_g, ln2_b):
    assert w_in.shape[0] == 1, "single-layer stack"
    bsz, seq, d = x.shape
    assert d == D_MODEL and seq % CHUNK == 0
    c, gw = CONV_WIDTH, GDN_WIDTH
    win = w_in[0].astype(_BF16)
    wbd = win[:, 3 * c + 4 * gw:]
    zpad = jnp.zeros((128 - 2 * GDN_HEADS,), _F32)
    zpad4 = jnp.zeros((GDN_HEADS,), _F32)
    prow = jnp.zeros((8, 128), _F32)
    prow = prow.at[0].set(jnp.concatenate([zpad4, a_log[0], zpad]))
    prow = prow.at[1].set(jnp.concatenate([zpad4, dt_bias[0], zpad]))
    wts = (win[:, :3 * c], win[:, 3 * c:3 * c + 3 * gw], win[:, 3 * c + 3 * gw:3 * c + 4 * gw],
           jnp.pad(wbd, ((0, 0), (0, 128 - 2 * GDN_HEADS))), wbd.T,
           conv_w[0], conv_norm_w, gdn_conv_w[0], prow, prow.T[:8])
    gnw = gdn_norm_w

    meta = jnp.concatenate([jnp.zeros((CHUNK - N_META, d), x.dtype), meta_tokens.astype(x.dtype)])[None]
    tails0 = jnp.zeros((HIST, c + 3 * gw), _F32)
    s00 = jnp.zeros((GDN_HEADS, GDN_HEAD_DIM, GDN_HEAD_DIM), _F32)
    _, _, tails_m, s_m = _mixer(meta, tails0, s00, wts, gnw, lt=CHUNK, lg=CHUNK, nbb=1)

    yc, yg, _, _ = _mixer(x, tails_m[0], s_m[0], wts, gnw, lt=_pick(seq, 512), lg=_pick(seq, 512),
                          nbb=GDN_ROWS if bsz % GDN_ROWS == 0 else 1)

    t = bsz * seq
    tm = _pick(t, 512)
    h1, h1p = _outproj(yc.reshape(t, c), yg.reshape(t, gw), x.reshape(t, d), w_out[0].astype(_BF16),
                       ln1_g, ln1_b, tm=tm)

    tt = _pick(t, 256)
    wr_t = w_router[0].T
    wr_hi = wr_t.astype(_BF16)
    wr_lo = (wr_t - wr_hi.astype(_F32)).astype(_BF16)
    shared_w = (ws_gate[0].astype(_BF16), ws_up[0].astype(_BF16), ws_down[0].astype(_BF16))
    h1p3 = h1p.reshape(t, QUAD, 128)

    parts = MOE_PARTS if t % (MOE_PARTS * tt * SC_WORKERS) == 0 else 1
    tp = t // parts
    nb = tp * TOP_K // ROW_BLOCK + N_EXPERTS
    out = None
    for part in range(parts):
        tile0 = part * (tp // tt)
        idx, gate, rank, cnt = _router(h1, wr_hi, wr_lo, b_router[0][:, None], tt=tt, tile0=tile0, t=tp)
        counts = cnt[:, 0].astype(_I32)
        pcounts = (counts + ROW_BLOCK - 1) // ROW_BLOCK * ROW_BLOCK
        pends = jnp.cumsum(pcounts)
        pstarts = pends - pcounts
        pos = _position(idx, rank, pstarts[:, None].astype(_I32), tt=tt)
        nwin = tt // SC_WINDOW
        pos3 = pos.reshape(tp // tt, TOP_K, nwin, SC_WINDOW).transpose(0, 2, 1, 3)
        pos3 = pos3.reshape(tp // SC_WINDOW, TOP_K, SC_WINDOW)
        xs = _sc_scatter(h1p3, pos3, nb * ROW_BLOCK, part * tp)
        xs = _padfill(counts, pstarts.astype(_I32), pcounts.astype(_I32), xs.reshape(nb * ROW_BLOCK * QUAD, 128))
        ys = _ffn((pstarts // ROW_BLOCK).astype(_I32), (pcounts // ROW_BLOCK).astype(_I32),
                  (pends[-1:] // ROW_BLOCK).astype(_I32), xs, w_gate[0], w_up[0], w_down[0])
        yg = _sc_gather(ys.reshape(nb * ROW_BLOCK, QUAD, 128), pos.reshape(tp * TOP_K))
        yg = yg.reshape(tp // tt, TOP_K, tt * QUAD, 128)
        out = _combine_stream(gate, h1, yg, *shared_w, ln2_g, ln2_b, out, tt=tt, tile0=tile0)
    return out.reshape(bsz, seq, d)
```

```python
import functools

import jax
import jax.numpy as jnp
from jax import lax
from jax.experimental import pallas as pl
from jax.experimental.pallas import tpu as pltpu
from jax.experimental.pallas import tpu_sc as plsc

_F32 = jnp.float32
_BF16 = jnp.bfloat16
_I32 = jnp.int32

D_MODEL = 1024
N_META = 16
CONV_WIDTH = 512
CONV_K = 3
GDN_HEADS = 4
GDN_HEAD_DIM = 128
GDN_WIDTH = GDN_HEADS * GDN_HEAD_DIM
GDN_CONV_K = 4
CHUNK = 64
N_EXPERTS = 256
TOP_K = 8
N_GROUPS = 8
TOPK_GROUPS = 4
E_PER_GROUP = N_EXPERTS // N_GROUPS
EXPERT_FF = 256
ROUTED_SCALE = 2.5
ROW_BLOCK = 256
DN_ALPHA = 2.0 ** 0.25
NORM_EPS = 1e-5
HALF = D_MODEL // 2
QUAD = HALF // 128
STACK = GDN_HEADS * CHUNK
HIST = 8
GDN_ROWS = 4
PREMIX_SUB = 2
ISSUE_UNROLL = 8
SC_CORES = 2
SC_SUBCORES = 16
SC_WORKERS = SC_CORES * SC_SUBCORES
SC_CHUNK = 64
SC_RING = 2
SC_WINDOW = 128
MOE_PARTS = 2
RING = 8
IN_AHEAD = RING - 2

V7X_VMEM_BYTES = 64 * 1024 * 1024
VMEM_LIMIT = V7X_VMEM_BYTES - 8 * 1024 * 1024


def _cparams(*sem):
    return pltpu.CompilerParams(dimension_semantics=sem, vmem_limit_bytes=VMEM_LIMIT)


def _mm(a, b):
    return jnp.dot(a.astype(_BF16), b.astype(_BF16), preferred_element_type=_F32)


def _mm_nt(a, b):
    return lax.dot_general(a.astype(_BF16), b.astype(_BF16), (((1,), (1,)), ((), ())),
                           preferred_element_type=_F32)


def _mm_tn(a, b):
    return lax.dot_general(a.astype(_BF16), b.astype(_BF16), (((0,), (0,)), ((), ())),
                           preferred_element_type=_F32)


def _sigmoid(x):
    return 1.0 / (1.0 + jnp.exp(-x))


def _silu(x):
    return x * _sigmoid(x)


def _softplus(x):
    return jnp.maximum(x, 0.0) + jnp.log1p(jnp.exp(-jnp.abs(x)))


def _pack_halves(y):
    return pltpu.pack_elementwise([y[:, :HALF], y[:, HALF:]], packed_dtype=_BF16)


def _store_rows(ref, packed):
    r = packed.shape[0]
    for c in range(QUAD):
        ref[pl.ds(c, r, stride=QUAD), :] = packed[:, c * 128:(c + 1) * 128]


def _load_rows(ref, r):
    return jnp.concatenate([ref[pl.ds(c, r, stride=QUAD), :] for c in range(QUAD)], axis=1)


def _unpack_halves(p):
    lo = pltpu.unpack_elementwise(p, index=0, packed_dtype=_BF16, unpacked_dtype=_F32)
    hi = pltpu.unpack_elementwise(p, index=1, packed_dtype=_BF16, unpacked_dtype=_F32)
    return lo, hi


def _layer_norm(h, g, b):
    mu = jnp.mean(h, axis=-1, keepdims=True)
    d = h - mu
    var = jnp.mean(d * d, axis=-1, keepdims=True)
    return d * lax.rsqrt(var + NORM_EPS) * g + b


def _premix_body(x_ref, tails_ref, wa_ref, wq_ref, wz_ref, wbd_ref, wbdt_ref, cw_ref, cnw_ref,
                 gcw_ref, prow_ref, pcol_ref,
                 yc_ref, q_ref, k_ref, v_ref, z_ref, bgc_ref, bgr_ref, tout_ref, ext_ref, *, lt):
    cw_ = CONV_WIDTH

    @pl.when(pl.program_id(1) == 0)
    def _():
        ext_ref[0:HIST, :] = tails_ref[...]

    cw = cw_ref[...]
    gcw = gcw_ref[...]
    prow = prow_ref[...]
    pcol = pcol_ref[...]

    def sub_tile(r0, n):
        rows = slice(r0, r0 + n)
        erows = slice(HIST + r0, HIST + r0 + n)
        xb = x_ref[0, rows, :].astype(_BF16)
        pa = jnp.dot(xb, wa_ref[...], preferred_element_type=_F32)
        yield
        gate_b = pa[:, 0:cw_]
        u = pa[:, cw_:2 * cw_] * pa[:, 2 * cw_:3 * cw_]
        ext_ref[erows, 0:cw_] = u
        pq = jnp.dot(xb, wq_ref[...], preferred_element_type=_F32)
        yield
        ext_ref[erows, cw_:] = pq
        zz = jnp.dot(xb, wz_ref[...], preferred_element_type=_F32)
        bdc = jnp.dot(xb, wbd_ref[...], preferred_element_type=_F32)
        bdr = _mm_nt(wbdt_ref[...], xb)
        yield

        ca = u * cw[CONV_K - 1:CONV_K, :]
        for j in range(CONV_K - 1):
            ca = ca + ext_ref[pl.ds(HIST + r0 - (CONV_K - 1) + j, n), 0:cw_] * cw[j:j + 1, :]
        yc = gate_b * ca
        ms = jnp.mean(yc * yc, axis=-1, keepdims=True)
        yc_ref[0, rows, :] = (yc * lax.rsqrt(ms + NORM_EPS) * cnw_ref[...]).astype(_BF16)

        cq = pq * gcw[GDN_CONV_K - 1:GDN_CONV_K, :]
        for j in range(GDN_CONV_K - 1):
            cq = cq + ext_ref[pl.ds(HIST + r0 - (GDN_CONV_K - 1) + j, n), cw_:] * gcw[j:j + 1, :]
        s = _silu(cq)
        for h in range(GDN_HEADS):
            lo, hi = h * GDN_HEAD_DIM, (h + 1) * GDN_HEAD_DIM
            qh = s[:, lo:hi]
            kh = s[:, GDN_WIDTH + lo:GDN_WIDTH + hi]
            qn = qh * lax.rsqrt(jnp.sum(qh * qh, axis=-1, keepdims=True) + 1e-6)
            kn = kh * lax.rsqrt(jnp.sum(kh * kh, axis=-1, keepdims=True) + 1e-6)
            q_ref[0, rows, lo:hi] = (qn * (GDN_HEAD_DIM ** -0.5)).astype(_BF16)
            k_ref[0, rows, lo:hi] = kn.astype(_BF16)
        v_ref[0, rows, :] = s[:, 2 * GDN_WIDTH:].astype(_BF16)
        z_ref[0, rows, :] = zz.astype(_BF16)

        g_c = -jnp.exp(prow[0:1, :]) * _softplus(bdc + prow[1:2, :])
        lane = lax.broadcasted_iota(_I32, bdc.shape, 1)
        bgc_ref[0, rows, :] = jnp.where(lane < GDN_HEADS, _sigmoid(bdc), g_c)
        g_r = -jnp.exp(pcol[:, 0:1]) * _softplus(bdr + pcol[:, 1:2])
        row = lax.broadcasted_iota(_I32, bdr.shape, 0)
        bgr_ref[0, :, rows] = jnp.where(row < GDN_HEADS, _sigmoid(bdr), g_r)

    n_sub = PREMIX_SUB if lt % (PREMIX_SUB * 128) == 0 else 1
    live = [sub_tile(i * (lt // n_sub), lt // n_sub) for i in range(n_sub)]
    while live:
        live = [g for g in live if next(g, live) is not live]

    tail = ext_ref[lt:lt + HIST, :]
    ext_ref[0:HIST, :] = tail
    tout_ref[0] = tail


def _premix(x, tails, wts, *, lt):
    bsz, seq, d = x.shape
    assert seq % lt == 0
    grid = (bsz, seq // lt)
    full = lambda a: pl.BlockSpec(a.shape, lambda b, j: (0,) * a.ndim)
    tok = lambda w: pl.BlockSpec((1, lt, w), lambda b, j: (b, j, 0))
    (wa, wq, wz, wbd, wbdt, cw, cnw, gcw, prow, pcol) = wts
    ext_w = CONV_WIDTH + 3 * GDN_WIDTH
    out_shape = (
        jax.ShapeDtypeStruct((bsz, seq, CONV_WIDTH), _BF16),
        jax.ShapeDtypeStruct((bsz, seq, GDN_WIDTH), _BF16),
        jax.ShapeDtypeStruct((bsz, seq, GDN_WIDTH), _BF16),
        jax.ShapeDtypeStruct((bsz, seq, GDN_WIDTH), _BF16),
        jax.ShapeDtypeStruct((bsz, seq, GDN_WIDTH), _BF16),
        jax.ShapeDtypeStruct((bsz, seq, 128), _F32),
        jax.ShapeDtypeStruct((bsz, 8, seq), _F32),
        jax.ShapeDtypeStruct((bsz, HIST, ext_w), _F32),
    )
    out_specs = (tok(CONV_WIDTH), tok(GDN_WIDTH), tok(GDN_WIDTH), tok(GDN_WIDTH), tok(GDN_WIDTH),
                 tok(128), pl.BlockSpec((1, 8, lt), lambda b, j: (b, 0, j)),
                 pl.BlockSpec((1, HIST, ext_w), lambda b, j: (b, 0, 0)))
    return pl.pallas_call(
        functools.partial(_premix_body, lt=lt),
        grid=grid,
        in_specs=[tok(d), full(tails)] + [full(w) for w in wts],
        out_specs=out_specs,
        out_shape=out_shape,
        scratch_shapes=[pltpu.VMEM((HIST + lt, ext_w), _F32)],
        compiler_params=_cparams("arbitrary", "arbitrary"),
        name="premix",
    )(x, tails, *wts)


def _cumsum_rows(x):
    row = lax.broadcasted_iota(_I32, x.shape, 0)
    s = 1
    while s < x.shape[0]:
        x = x + jnp.where(row >= s, pltpu.roll(x, s, 0), 0.0)
        s *= 2
    return x


def _cumsum_lanes_seg(x):
    lane = lax.broadcasted_iota(_I32, x.shape, 1) & (CHUNK - 1)
    s = 1
    while s < CHUNK:
        x = x + jnp.where(lane >= s, pltpu.roll(x, s, 1), 0.0)
        s *= 2
    return x


def _stack_heads(a):
    return jnp.concatenate([a[:, h * GDN_HEAD_DIM:(h + 1) * GDN_HEAD_DIM] for h in range(GDN_HEADS)], axis=0)


def _gdn_body(q_ref, k_ref, v_ref, z_ref, bgc_ref, grow_ref, s0_ref, gnw_ref,
              y_ref, sout_ref, s_ref, *, nc, nbb):
    @pl.when(pl.program_id(1) == 0)
    def _():
        for r in range(nbb):
            s_ref[r] = s0_ref[...]

    ri = lax.broadcasted_iota(_I32, (STACK, STACK), 0)
    ci = lax.broadcasted_iota(_I32, (STACK, STACK), 1)
    same64 = (ri >> 6) == (ci >> 6)
    same32 = (ri >> 5) == (ci >> 5)
    same16 = (ri >> 4) == (ci >> 4)
    low_incl = same64 & (ri >= ci)
    low_strict = same64 & (ri > ci)
    gnw = gnw_ref[...]

    def chunk_row(r, c):
        off = pl.multiple_of(c * CHUNK, CHUNK)
        q_all = _stack_heads(q_ref[r, pl.ds(off, CHUNK), :].astype(_F32))
        k_all = _stack_heads(k_ref[r, pl.ds(off, CHUNK), :].astype(_F32))
        v_all = _stack_heads(v_ref[r, pl.ds(off, CHUNK), :].astype(_F32))
        bgc = bgc_ref[r, pl.ds(off, CHUNK), :]
        gcs = _cumsum_rows(bgc)
        hd = (CHUNK, GDN_HEAD_DIM)
        beta_b = jnp.concatenate(
            [jnp.broadcast_to(bgc[:, h:h + 1], hd) for h in range(GDN_HEADS)], axis=0)
        gc_b = jnp.concatenate(
            [jnp.broadcast_to(gcs[:, GDN_HEADS + h:GDN_HEADS + h + 1], hd) for h in range(GDN_HEADS)], axis=0)
        gl = [gcs[CHUNK - 1:CHUNK, GDN_HEADS + h:GDN_HEADS + h + 1] for h in range(GDN_HEADS)]
        gl_b = jnp.concatenate([jnp.broadcast_to(g1, hd) for g1 in gl], axis=0)
        gcr = _cumsum_lanes_seg(jnp.broadcast_to(grow_ref[r, c], (8, STACK)))[0:1, :]

        diff = jnp.concatenate([gc_b, gc_b], axis=1) - gcr
        decay = jnp.exp(jnp.where(low_incl, diff, -1e30))
        kb = k_all * beta_b
        a1 = _mm_nt(jnp.concatenate([kb, q_all], axis=0), k_all)
        yield
        m = jnp.where(low_strict, a1[:STACK] * decay, 0.0)
        attn = a1[STACK:] * decay

        l16 = jnp.where(same16, m, 0.0)
        c1 = jnp.where(same32 & jnp.logical_not(same16), m, 0.0)
        c2 = jnp.where(same32, 0.0, m)
        p2 = _mm(l16, l16)
        yield
        p4 = _mm(p2, p2)
        t = _mm(l16, p2)
        yield
        na = p2 - l16 - t
        p8 = _mm(p4, p4)
        t = _mm(na, p4)
        yield
        nb = na + p4 + t
        t = _mm(nb, p8)
        yield
        ncm = nb + p8 + t
        t = _mm(c1, ncm)
        yield
        y1 = c1 + t
        t = _mm(ncm, y1)
        yield
        n1 = ncm - y1 - t
        t = _mm(c2, n1)
        yield
        y2 = c2 + t
        t = _mm(n1, y2)
        yield
        nt = n1 - y2 - t

        egc = jnp.exp(gc_b)
        rhs = jnp.concatenate([v_all * beta_b, kb * egc], axis=1)
        t = _mm(nt, rhs)
        yield
        uw = rhs + t
        u_all = uw[:, :GDN_HEAD_DIM]
        w_all = uw[:, GDN_HEAD_DIM:]
        qd = q_all * egc
        kd = k_all * jnp.exp(gl_b - gc_b)

        bs = []
        for h in range(GDN_HEADS):
            r0, r1 = h * CHUNK, (h + 1) * CHUNK
            bs.append(_mm(jnp.concatenate([w_all[r0:r1], qd[r0:r1]], axis=0), s_ref[r, h]))
        yield
        vn = [u_all[h * CHUNK:(h + 1) * CHUNK] - bs[h][:CHUNK] for h in range(GDN_HEADS)]
        vn_all = jnp.concatenate(vn, axis=0)
        t = _mm(attn, vn_all)
        ds = [_mm_tn(kd[h * CHUNK:(h + 1) * CHUNK], vn[h]) for h in range(GDN_HEADS)]
        yield
        o_all = jnp.concatenate([b[CHUNK:] for b in bs], axis=0) + t
        for h in range(GDN_HEADS):
            r0, r1 = h * CHUNK, (h + 1) * CHUNK
            s_ref[r, h] = s_ref[r, h] * jnp.exp(gl[h]) + ds[h]
            o = o_all[r0:r1]
            zz = z_ref[r, pl.ds(off, CHUNK), h * GDN_HEAD_DIM:(h + 1) * GDN_HEAD_DIM].astype(_F32)
            on = o * lax.rsqrt(jnp.mean(o * o, axis=-1, keepdims=True) + NORM_EPS) * gnw
            y_ref[r, pl.ds(off, CHUNK), h * GDN_HEAD_DIM:(h + 1) * GDN_HEAD_DIM] = (on * _silu(zz)).astype(_BF16)

    def chunk(c, carry):
        live = [chunk_row(r, c) for r in range(nbb)]
        while live:
            live = [g for g in live if next(g, live) is not live]
        return carry

    lax.fori_loop(0, nc, chunk, 0)
    sout_ref[...] = s_ref[...]


def _gdn(q, k, v, z, bgc, grow, s0, gnw, *, lg, nbb):
    bsz, seq, _ = q.shape
    assert seq % lg == 0 and lg % CHUNK == 0 and bsz % nbb == 0
    nc = lg // CHUNK
    tok = lambda w: pl.BlockSpec((nbb, lg, w), lambda b, j: (b, j, 0))
    full = lambda a: pl.BlockSpec(a.shape, lambda b, j: (0,) * a.ndim)
    st = (nbb, GDN_HEADS, GDN_HEAD_DIM, GDN_HEAD_DIM)
    return pl.pallas_call(
        functools.partial(_gdn_body, nc=nc, nbb=nbb),
        grid=(bsz // nbb, seq // lg),
        in_specs=[tok(GDN_WIDTH)] * 4 + [tok(128), pl.BlockSpec((nbb, nc, 1, STACK), lambda b, j: (b, j, 0, 0)),
                                           full(s0), full(gnw)],
        out_specs=(tok(GDN_WIDTH), pl.BlockSpec(st, lambda b, j: (b, 0, 0, 0))),
        out_shape=(jax.ShapeDtypeStruct((bsz, seq, GDN_WIDTH), _BF16),
                   jax.ShapeDtypeStruct((bsz, GDN_HEADS, GDN_HEAD_DIM, GDN_HEAD_DIM), _F32)),
        scratch_shapes=[pltpu.VMEM(st, _F32)],
        compiler_params=_cparams("arbitrary", "arbitrary"),
        name="gdn",
    )(q, k, v, z, bgc, grow, s0, gnw)


def _outproj_body(yc_ref, yg_ref, x_ref, wo_ref, g_ref, b_ref, h1_ref, h1p_ref):
    mix = (jnp.dot(yc_ref[...], wo_ref[0:CONV_WIDTH, :], preferred_element_type=_F32)
           + jnp.dot(yg_ref[...], wo_ref[CONV_WIDTH:, :], preferred_element_type=_F32))
    h1 = _layer_norm(DN_ALPHA * x_ref[...] + mix, g_ref[...], b_ref[...])
    h1_ref[...] = h1
    _store_rows(h1p_ref, _pack_halves(h1))


def _outproj(yc, yg, x2d, wo, g, b, *, tm):
    t = x2d.shape[0]
    assert t % tm == 0
    row = lambda w: pl.BlockSpec((tm, w), lambda i: (i, 0))
    full = lambda a: pl.BlockSpec(a.shape, lambda i: (0,) * a.ndim)
    return pl.pallas_call(
        _outproj_body,
        grid=(t // tm,),
        in_specs=[row(CONV_WIDTH), row(GDN_WIDTH), row(D_MODEL), full(wo), full(g), full(b)],
        out_specs=(row(D_MODEL), pl.BlockSpec((tm * QUAD, 128), lambda i: (i, 0))),
        out_shape=(jax.ShapeDtypeStruct((t, D_MODEL), _F32), jax.ShapeDtypeStruct((t * QUAD, 128), jnp.uint32)),
        compiler_params=_cparams("arbitrary"),
        name="outproj",
    )(yc, yg, x2d, wo, g, b)


def _router_body(h1_ref, wh_ref, wl_ref, br_ref, idx_ref, gate_ref, rank_ref, cnt_ref, carry_ref, *, tt):
    @pl.when(pl.program_id(0) == 0)
    def _():
        carry_ref[...] = jnp.zeros_like(carry_ref)

    x = h1_ref[...]
    xh = x.astype(_BF16)
    xl = (x - xh.astype(_F32)).astype(_BF16)
    wh = wh_ref[...]
    logits = _mm_nt(wh, xh) + _mm_nt(wh, xl) + _mm_nt(wl_ref[...], xh)
    scores = _sigmoid(logits)
    sel = scores + br_ref[...]
    ninf = -jnp.inf

    r32 = lax.broadcasted_iota(_I32, (E_PER_GROUP, tt), 0)
    gsc = []
    for g in range(N_GROUPS):
        xg = sel[g * E_PER_GROUP:(g + 1) * E_PER_GROUP]
        m1 = jnp.max(xg, axis=0, keepdims=True)
        i1 = jnp.min(jnp.where(xg == m1, r32, E_PER_GROUP), axis=0, keepdims=True)
        m2 = jnp.max(jnp.where(r32 == i1, ninf, xg), axis=0, keepdims=True)
        gsc.append(m1 + m2)
    work = jnp.concatenate(gsc, axis=0)
    r8 = lax.broadcasted_iota(_I32, (N_GROUPS, tt), 0)
    gkeep = jnp.zeros((N_GROUPS, tt), _F32)
    for _ in range(TOPK_GROUPS):
        m = jnp.max(work, axis=0, keepdims=True)
        gi = jnp.min(jnp.where(work == m, r8, N_GROUPS), axis=0, keepdims=True)
        pick = r8 == gi
        gkeep = jnp.where(pick, 1.0, gkeep)
        work = jnp.where(pick, ninf, work)
    selm = jnp.concatenate(
        [jnp.where(gkeep[g:g + 1] > 0.5, sel[g * E_PER_GROUP:(g + 1) * E_PER_GROUP], ninf)
         for g in range(N_GROUPS)], axis=0)

    re = lax.broadcasted_iota(_I32, (N_EXPERTS, tt), 0)
    msel = jnp.zeros((N_EXPERTS, tt), _F32)
    idxs, gates = [], []
    for _ in range(TOP_K):
        m = jnp.max(selm, axis=0, keepdims=True)
        ii = jnp.min(jnp.where(selm == m, re, N_EXPERTS), axis=0, keepdims=True)
        hit = re == ii
        idxs.append(ii)
        gates.append(jnp.sum(jnp.where(hit, scores, 0.0), axis=0, keepdims=True))
        selm = jnp.where(hit, ninf, selm)
        msel = jnp.where(hit, 1.0, msel)
    gate = jnp.concatenate(gates, axis=0)
    gate_ref[...] = gate / jnp.sum(gate, axis=0, keepdims=True) * ROUTED_SCALE
    idx_ref[...] = jnp.concatenate(idxs, axis=0)

    ta = lax.broadcasted_iota(_I32, (tt, tt), 0)
    tb = lax.broadcasted_iota(_I32, (tt, tt), 1)
    earlier = jnp.where(ta < tb, 1.0, 0.0)
    carry = carry_ref[...]
    rank_all = _mm(msel, earlier) + carry[:, 0:1]
    rank_ref[...] = jnp.concatenate(
        [jnp.sum(jnp.where(re == ii, rank_all, 0.0), axis=0, keepdims=True) for ii in idxs],
        axis=0).astype(_I32)
    carry = carry + jnp.sum(msel, axis=1, keepdims=True)
    carry_ref[...] = carry
    cnt_ref[...] = carry


def _router(h1, wh, wl, br, *, tt, tile0, t):
    assert t % tt == 0
    full = lambda a: pl.BlockSpec(a.shape, lambda i: (0,) * a.ndim)
    kt = pl.BlockSpec((TOP_K, tt), lambda i: (0, i))
    return pl.pallas_call(
        functools.partial(_router_body, tt=tt),
        grid=(t // tt,),
        in_specs=[pl.BlockSpec((tt, D_MODEL), lambda i: (i + tile0, 0)), full(wh), full(wl), full(br)],
        out_specs=(kt, kt, kt, pl.BlockSpec((N_EXPERTS, 128), lambda i: (0, 0))),
        out_shape=(jax.ShapeDtypeStruct((TOP_K, t), _I32), jax.ShapeDtypeStruct((TOP_K, t), _F32),
                   jax.ShapeDtypeStruct((TOP_K, t), _I32), jax.ShapeDtypeStruct((N_EXPERTS, 128), _F32)),
        scratch_shapes=[pltpu.VMEM((N_EXPERTS, 128), _F32)],
        compiler_params=_cparams("arbitrary"),
        name="router",
    )(h1, wh, wl, br)


def _position_body(idx_ref, rank_ref, pstart_ref, pos_ref, *, tt):
    re = lax.broadcasted_iota(_I32, (N_EXPERTS, tt), 0)
    ps = pstart_ref[...]
    idx = idx_ref[...]
    rows = [jnp.sum(jnp.where(re == idx[k:k + 1], ps, 0), axis=0, keepdims=True) for k in range(TOP_K)]
    pos_ref[0] = jnp.concatenate(rows, axis=0) + rank_ref[...]


def _position(idx, rank, pstart, *, tt):
    t = idx.shape[1]
    kt = pl.BlockSpec((TOP_K, tt), lambda i: (0, i))
    return pl.pallas_call(
        functools.partial(_position_body, tt=tt),
        grid=(t // tt,),
        in_specs=[kt, kt, pl.BlockSpec(pstart.shape, lambda i: (0, 0))],
        out_specs=pl.BlockSpec((1, TOP_K, tt), lambda i: (i, 0, 0)),
        out_shape=jax.ShapeDtypeStruct((t // tt, TOP_K, tt), _I32),
        compiler_params=_cparams("arbitrary"),
        name="position",
    )(idx, rank, pstart)


def _dispatch_body(cnt_ref, pst_ref, pcn_ref, pos_hbm, h1p_ref, xs_out, pos_smem, psem, sem, zbuf, zsem, *, tt):
    i = pl.program_id(0)
    cp = pltpu.make_async_copy(pos_hbm.at[i], pos_smem, psem)
    cp.start()

    @pl.when(i == 0)
    def _():
        zbuf[...] = jnp.zeros_like(zbuf)

        def pad_runs(e, act):
            pad = pcn_ref[e] - cnt_ref[e]
            base = pst_ref[e] + cnt_ref[e]
            for b in range(ROW_BLOCK.bit_length() - 1):
                n = 1 << b

                @pl.when(((pad >> b) & 1) == 1)
                def _():
                    off = base + (pad & (n - 1))
                    act(pltpu.make_async_copy(zbuf.at[pl.ds(0, QUAD * n)],
                                              xs_out.at[pl.ds(QUAD * off, QUAD * n)], zsem))

        def start_all(e, c):
            pad_runs(e, lambda d: d.start())
            return c

        def wait_all(e, c):
            pad_runs(e, lambda d: d.wait())
            return c

        lax.fori_loop(0, N_EXPERTS, start_all, 0)
        lax.fori_loop(0, N_EXPERTS, wait_all, 0)

    cp.wait()

    def row_copy(t, k):
        return pltpu.make_async_copy(h1p_ref.at[pl.ds(QUAD * t, QUAD)],
                                     xs_out.at[pl.ds(QUAD * pos_smem[k * tt + t], QUAD)], sem)

    def issue(t, c):
        for k in range(TOP_K):
            row_copy(t, k).start(priority=k % 2)
        return c

    lax.fori_loop(0, tt, issue, 0, unroll=ISSUE_UNROLL)

    for k in range(TOP_K):
        pltpu.make_async_copy(h1p_ref, xs_out.at[pl.ds(0, QUAD * tt)], sem).wait()


def _dispatch(counts, pstarts, pcounts, pos_tiles, h1p, n_rows, *, tt):
    t = h1p.shape[0] // QUAD
    grid_spec = pltpu.PrefetchScalarGridSpec(
        num_scalar_prefetch=3,
        grid=(t // tt,),
        in_specs=[pl.BlockSpec(memory_space=pl.ANY), pl.BlockSpec((tt * QUAD, 128), lambda i, *_: (i, 0))],
        out_specs=pl.BlockSpec(memory_space=pl.ANY),
        scratch_shapes=[pltpu.SMEM((TOP_K * tt,), _I32), pltpu.SemaphoreType.DMA, pltpu.SemaphoreType.DMA,
                        pltpu.VMEM((QUAD * ROW_BLOCK // 2, 128), jnp.uint32), pltpu.SemaphoreType.DMA],
    )
    return pl.pallas_call(
        functools.partial(_dispatch_body, tt=tt),
        grid_spec=grid_spec,
        out_shape=jax.ShapeDtypeStruct((n_rows * QUAD, 128), jnp.uint32),
        compiler_params=_cparams("arbitrary"),
        name="dispatch",
    )(counts, pstarts, pcounts, pos_tiles, h1p)


def _ffn_body(blk0_ref, nblk_ref, ntot_ref, xs_hbm, wg_ref, wu_ref, wd_ref, ys_hbm,
              xbuf, ybuf, sem_in, sem_out, wgu_bf, wd_bf):
    e = pl.program_id(0)
    nblk = nblk_ref[e]
    blk0 = blk0_ref[e]
    ntot = ntot_ref[0]

    blk_rows = ROW_BLOCK * QUAD

    def rows(g):
        return pl.ds(pl.multiple_of(g * blk_rows, blk_rows), blk_rows)

    def in_start(g, slot):
        pltpu.make_async_copy(xs_hbm.at[rows(g)], xbuf.at[slot], sem_in.at[slot]).start()

    def in_wait(slot):
        pltpu.make_async_copy(xs_hbm.at[rows(0)], xbuf.at[slot], sem_in.at[slot]).wait()

    def out_start(g, slot):
        pltpu.make_async_copy(ybuf.at[slot], ys_hbm.at[rows(g)], sem_out.at[slot]).start()

    def out_wait(slot):
        pltpu.make_async_copy(ybuf.at[slot], ys_hbm.at[rows(0)], sem_out.at[slot]).wait()

    @pl.when(e == 0)
    def _():
        for i in range(IN_AHEAD):
            @pl.when(i < ntot)
            def _():
                in_start(i, i)

    @pl.when(nblk > 0)
    def _():
        wgu_bf[:, 0:EXPERT_FF] = wg_ref[0].astype(_BF16)
        wgu_bf[:, EXPERT_FF:] = wu_ref[0].astype(_BF16)
        wd_bf[...] = wd_ref[0].astype(_BF16)

        def acquire(g):
            slot = g & (RING - 1)
            in_wait(slot)

            @pl.when(g + IN_AHEAD < ntot)
            def _():
                in_start(g + IN_AHEAD, (g + IN_AHEAD) & (RING - 1))

            @pl.when(g >= RING)
            def _():
                out_wait(slot)

            return slot

        def compute(slot):
            lo, hi = _unpack_halves(_load_rows(xbuf.at[slot], ROW_BLOCK))
            a = jnp.dot(lo.astype(_BF16), wgu_bf[0:HALF, :], preferred_element_type=_F32)
            yield
            gu = a + jnp.dot(hi.astype(_BF16), wgu_bf[HALF:, :], preferred_element_type=_F32)
            yield
            h = (_silu(gu[:, :EXPERT_FF]) * gu[:, EXPERT_FF:]).astype(_BF16)
            y = jnp.dot(h, wd_bf[...], preferred_element_type=_F32)
            yield
            _store_rows(ybuf.at[slot], _pack_halves(y))

        def run(gs):
            slots = [acquire(g) for g in gs]
            live = [compute(s) for s in slots]
            while live:
                live = [c for c in live if next(c, live) is not live]
            for g, s in zip(gs, slots):
                out_start(g, s)

        def pair(j, carry):
            run([blk0 + 2 * j, blk0 + 2 * j + 1])
            return carry

        lax.fori_loop(0, nblk // 2, pair, 0)

        @pl.when((nblk & 1) == 1)
        def _():
            run([blk0 + nblk - 1])

    @pl.when(e == N_EXPERTS - 1)
    def _():
        for i in range(RING):
            @pl.when(i < ntot)
            def _():
                out_wait((ntot - 1 - i) & (RING - 1))


def _ffn(blk0, nblk, ntot, xs, wg, wu, wd):
    grid_spec = pltpu.PrefetchScalarGridSpec(
        num_scalar_prefetch=3,
        grid=(N_EXPERTS,),
        in_specs=[pl.BlockSpec(memory_space=pl.ANY),
                  pl.BlockSpec((1, D_MODEL, EXPERT_FF), lambda e, *_: (e, 0, 0)),
                  pl.BlockSpec((1, D_MODEL, EXPERT_FF), lambda e, *_: (e, 0, 0)),
                  pl.BlockSpec((1, EXPERT_FF, D_MODEL), lambda e, *_: (e, 0, 0))],
        out_specs=pl.BlockSpec(memory_space=pl.ANY),
        scratch_shapes=[pltpu.VMEM((RING, ROW_BLOCK * QUAD, 128), jnp.uint32),
                        pltpu.VMEM((RING, ROW_BLOCK * QUAD, 128), jnp.uint32),
                        pltpu.SemaphoreType.DMA((RING,)), pltpu.SemaphoreType.DMA((RING,)),
                        pltpu.VMEM((D_MODEL, 2 * EXPERT_FF), _BF16), pltpu.VMEM((EXPERT_FF, D_MODEL), _BF16)],
    )
    return pl.pallas_call(
        _ffn_body,
        grid_spec=grid_spec,
        out_shape=jax.ShapeDtypeStruct(xs.shape, jnp.uint32),
        compiler_params=_cparams("arbitrary"),
        name="ffn",
    )(blk0, nblk, ntot, xs, wg, wu, wd)


def _combine_body(pos_hbm, gate_ref, h1_ref, ys_hbm, wsg_ref, wsu_ref, wsd_ref, g_ref, b_ref,
                  out_ref, pos_smem, psem, ybuf, sem, *, tt):
    i = pl.program_id(0)
    cp = pltpu.make_async_copy(pos_hbm.at[i], pos_smem, psem)
    cp.start()
    cp.wait()

    def row_copy(t, k):
        return pltpu.make_async_copy(ys_hbm.at[pl.ds(QUAD * pos_smem[k * tt + t], QUAD)],
                                     ybuf.at[k, pl.ds(QUAD * t, QUAD)], sem)

    def issue(t, c):
        for k in range(TOP_K):
            row_copy(t, k).start(priority=k % 2)
        return c

    lax.fori_loop(0, tt, issue, 0, unroll=ISSUE_UNROLL)

    x = h1_ref[...]
    xb = x.astype(_BF16)
    shared = _mm(_silu(_mm(xb, wsg_ref[...])) * _mm(xb, wsu_ref[...]), wsd_ref[...])

    for k in range(TOP_K):
        pltpu.make_async_copy(ys_hbm.at[pl.ds(0, QUAD * tt)], ybuf.at[k], sem).wait()

    gcol = gate_ref[...].T
    acc_lo = jnp.zeros((tt, HALF), _F32)
    acc_hi = jnp.zeros((tt, HALF), _F32)
    for k in range(TOP_K):
        lo, hi = _unpack_halves(_load_rows(ybuf.at[k], tt))
        acc_lo = acc_lo + gcol[:, k:k + 1] * lo
        acc_hi = acc_hi + gcol[:, k:k + 1] * hi
    routed = jnp.concatenate([acc_lo, acc_hi], axis=1)
    out_ref[...] = _layer_norm(DN_ALPHA * x + (routed + shared), g_ref[...], b_ref[...])


def _combine(pos_tiles, gate, h1, ys, wsg, wsu, wsd, g, b, *, tt):
    t = h1.shape[0]
    full = lambda a: pl.BlockSpec(a.shape, lambda i: (0,) * a.ndim)
    return pl.pallas_call(
        functools.partial(_combine_body, tt=tt),
        grid=(t // tt,),
        in_specs=[pl.BlockSpec(memory_space=pl.ANY), pl.BlockSpec((TOP_K, tt), lambda i: (0, i)),
                  pl.BlockSpec((tt, D_MODEL), lambda i: (i, 0)), pl.BlockSpec(memory_space=pl.ANY),
                  full(wsg), full(wsu), full(wsd), full(g), full(b)],
        out_specs=pl.BlockSpec((tt, D_MODEL), lambda i: (i, 0)),
        out_shape=jax.ShapeDtypeStruct((t, D_MODEL), _F32),
        scratch_shapes=[pltpu.SMEM((TOP_K * tt,), _I32), pltpu.SemaphoreType.DMA,
                        pltpu.VMEM((TOP_K, tt * QUAD, 128), jnp.uint32), pltpu.SemaphoreType.DMA],
        compiler_params=_cparams("arbitrary"),
        name="combine",
    )(pos_tiles, gate, h1, ys, wsg, wsu, wsd, g, b)


def _sc_gather(table, idx):
    b = idx.shape[0]
    nchunk = b // (SC_WORKERS * SC_CHUNK)
    assert nchunk * SC_WORKERS * SC_CHUNK == b and nchunk % SC_RING == 0
    idx2 = idx.reshape(SC_WORKERS * nchunk, SC_CHUNK)
    row = table.shape[1:]
    mesh = plsc.VectorSubcoreMesh(core_axis_name="c", subcore_axis_name="s",
                                  num_cores=SC_CORES, num_subcores=SC_SUBCORES)

    @functools.partial(
        pl.kernel, mesh=mesh,
        out_type=jax.ShapeDtypeStruct((b,) + row, table.dtype),
        scratch_types=[pltpu.VMEM((nchunk, SC_CHUNK), _I32), pltpu.VMEM((SC_RING, SC_CHUNK) + row, table.dtype),
                       pltpu.SemaphoreType.DMA((SC_RING,)), pltpu.SemaphoreType.DMA((SC_RING,))],
        name="sc_gather",
    )
    def gather(table_hbm, idx_hbm, out_hbm, idx_v, rows_v, sem_g, sem_w):
        wid = lax.axis_index("s") * SC_CORES + lax.axis_index("c")
        c0 = wid * nchunk
        pltpu.sync_copy(idx_hbm.at[pl.ds(pl.multiple_of(c0, nchunk), nchunk)], idx_v)

        def fetch(i, s):
            return pltpu.make_async_copy(table_hbm.at[idx_v.at[i]], rows_v.at[s], sem_g.at[s])

        def flush(i, s):
            rows = pl.ds(pl.multiple_of((c0 + i) * SC_CHUNK, SC_CHUNK), SC_CHUNK)
            return pltpu.make_async_copy(rows_v.at[s], out_hbm.at[rows], sem_w.at[s])

        for s in range(SC_RING):
            fetch(s, s).start()

        @pl.loop(0, nchunk, step=SC_RING)
        def _(g):
            for s in range(SC_RING):
                i = g + s
                fetch(i, s).wait()
                flush(i, s).start()
                flush(i, s).wait()

                @pl.when(i + SC_RING < nchunk)
                def _():
                    fetch(i + SC_RING, s).start()

    return gather(table, idx2)


def _sc_scatter(rows, pos3, n_out, row0):
    nchunk, nk, w = pos3.shape
    per_w = nchunk // SC_WORKERS
    assert per_w * SC_WORKERS == nchunk and w <= 128 and row0 % w == 0 and rows.shape[0] >= row0 + nchunk * w
    row = rows.shape[1:]
    mesh = plsc.VectorSubcoreMesh(core_axis_name="c", subcore_axis_name="s",
                                  num_cores=SC_CORES, num_subcores=SC_SUBCORES)

    @functools.partial(
        pl.kernel, mesh=mesh,
        out_type=jax.ShapeDtypeStruct((n_out,) + row, rows.dtype),
        scratch_types=[pltpu.VMEM((nk, w), _I32), pltpu.VMEM((w,) + row, rows.dtype), pltpu.SemaphoreType.DMA],
        name="sc_scatter",
    )
    def scatter(rows_hbm, pos_hbm, out_hbm, idx_v, rows_v, sem):
        wid = lax.axis_index("s") * SC_CORES + lax.axis_index("c")

        @pl.loop(0, per_w)
        def _(i):
            c = wid * per_w + i
            pltpu.sync_copy(pos_hbm.at[c], idx_v)
            pltpu.sync_copy(rows_hbm.at[pl.ds(pl.multiple_of(row0 + c * w, w), w)], rows_v)
            copies = [pltpu.async_copy(rows_v, out_hbm.at[idx_v.at[k]], sem) for k in range(nk)]
            for cp in copies:
                cp.wait()

    return scatter(rows, pos3)


def _padfill_body(cnt_ref, pst_ref, pcn_ref, xs_in, xs_out, zbuf, zsem):
    del xs_in
    zbuf[...] = jnp.zeros_like(zbuf)

    def pad_runs(e, act):
        pad = pcn_ref[e] - cnt_ref[e]
        base = pst_ref[e] + cnt_ref[e]
        for b in range(ROW_BLOCK.bit_length() - 1):
            n = 1 << b

            @pl.when(((pad >> b) & 1) == 1)
            def _():
                off = base + (pad & (n - 1))
                act(pltpu.make_async_copy(zbuf.at[pl.ds(0, QUAD * n)],
                                          xs_out.at[pl.ds(QUAD * off, QUAD * n)], zsem))

    def start_all(e, c):
        pad_runs(e, lambda d: d.start())
        return c

    def wait_all(e, c):
        pad_runs(e, lambda d: d.wait())
        return c

    lax.fori_loop(0, N_EXPERTS, start_all, 0)
    lax.fori_loop(0, N_EXPERTS, wait_all, 0)


def _padfill(counts, pstarts, pcounts, xs):
    grid_spec = pltpu.PrefetchScalarGridSpec(
        num_scalar_prefetch=3,
        grid=(1,),
        in_specs=[pl.BlockSpec(memory_space=pl.ANY)],
        out_specs=pl.BlockSpec(memory_space=pl.ANY),
        scratch_shapes=[pltpu.VMEM((QUAD * ROW_BLOCK // 2, 128), jnp.uint32), pltpu.SemaphoreType.DMA],
    )
    return pl.pallas_call(
        _padfill_body,
        grid_spec=grid_spec,
        out_shape=jax.ShapeDtypeStruct(xs.shape, xs.dtype),
        input_output_aliases={3: 0},
        compiler_params=_cparams("arbitrary"),
        name="padfill",
    )(counts, pstarts, pcounts, xs)


def _combine_stream_body(gate_ref, h1_ref, yg_ref, wsg_ref, wsu_ref, wsd_ref, g_ref, b_ref, out_ref, *, tt):
    x = h1_ref[...]
    xb = x.astype(_BF16)
    shared = _mm(_silu(_mm(xb, wsg_ref[...])) * _mm(xb, wsu_ref[...]), wsd_ref[...])
    gcol = gate_ref[...].T
    acc_lo = jnp.zeros((tt, HALF), _F32)
    acc_hi = jnp.zeros((tt, HALF), _F32)
    for k in range(TOP_K):
        lo, hi = _unpack_halves(_load_rows(yg_ref.at[0, k], tt))
        acc_lo = acc_lo + gcol[:, k:k + 1] * lo
        acc_hi = acc_hi + gcol[:, k:k + 1] * hi
    routed = jnp.concatenate([acc_lo, acc_hi], axis=1)
    out_ref[...] = _layer_norm(DN_ALPHA * x + (routed + shared), g_ref[...], b_ref[...])


def _combine_stream(gate, h1, yg, wsg, wsu, wsd, g, b, out_prev, *, tt, tile0):
    t_all = h1.shape[0]
    t = gate.shape[1]
    full = lambda a: pl.BlockSpec(a.shape, lambda i: (0,) * a.ndim)
    in_specs = [pl.BlockSpec((TOP_K, tt), lambda i: (0, i)), pl.BlockSpec((tt, D_MODEL), lambda i: (i + tile0, 0)),
                pl.BlockSpec((1, TOP_K, tt * QUAD, 128), lambda i: (i, 0, 0, 0)),
                full(wsg), full(wsu), full(wsd), full(g), full(b)]
    args = [gate, h1, yg, wsg, wsu, wsd, g, b]
    aliases = {}
    body = functools.partial(_combine_stream_body, tt=tt)
    if out_prev is not None:
        in_specs.append(pl.BlockSpec(memory_space=pl.ANY))
        args.append(out_prev)
        aliases = {len(args) - 1: 0}
        body = lambda *refs: _combine_stream_body(*refs[:8], refs[9], tt=tt)
    return pl.pallas_call(
        body,
        grid=(t // tt,),
        in_specs=in_specs,
        out_specs=pl.BlockSpec((tt, D_MODEL), lambda i: (i + tile0, 0)),
        out_shape=jax.ShapeDtypeStruct((t_all, D_MODEL), _F32),
        input_output_aliases=aliases,
        compiler_params=_cparams("arbitrary"),
        name="combine",
    )(*args)


def _pick(n, pref):
    t = min(n, pref)
    while n % t:
        t -= CHUNK
    return t


def _mixer(x, tails, s0, wts, gnw, *, lt, lg, nbb):
    yc, q, k, v, z, bgc, bgr, tails_out = _premix(x, tails, wts, lt=lt)
    bsz, seq, _ = x.shape
    nch = seq // CHUNK
    grow = bgr[:, GDN_HEADS:2 * GDN_HEADS, :].reshape(bsz, GDN_HEADS, nch, CHUNK)
    grow = grow.transpose(0, 2, 1, 3).reshape(bsz, nch, 1, STACK)
    yg, s_out = _gdn(q, k, v, z, bgc, grow, s0, gnw, lg=lg, nbb=nbb)
    return yc, yg, tails_out, s_out


def kernel(x, meta_tokens, w_in, conv_w, conv_norm_w, gdn_conv_w, a_log, dt_bias, gdn_norm_w, w_out,
           ln1_g, ln1_b, w_router, b_router, w_gate, w_up, w_down, ws_gate, ws_up, ws_down, ln2_g, ln2_b):
    assert w_in.shape[0] == 1, "single-layer stack"
    bsz, seq, d = x.shape
    assert d == D_MODEL and seq % CHUNK == 0
    c, gw = CONV_WIDTH, GDN_WIDTH
    win = w_in[0].astype(_BF16)
    wbd = win[:, 3 * c + 4 * gw:]
    zpad = jnp.zeros((128 - 2 * GDN_HEADS,), _F32)
    zpad4 = jnp.zeros((GDN_HEADS,), _F32)
    prow = jnp.zeros((8, 128), _F32)
    prow = prow.at[0].set(jnp.concatenate([zpad4, a_log[0], zpad]))
    prow = prow.at[1].set(jnp.concatenate([zpad4, dt_bias[0], zpad]))
    wts = (win[:, :3 * c], win[:, 3 * c:3 * c + 3 * gw], win[:, 3 * c + 3 * gw:3 * c + 4 * gw],
           jnp.pad(wbd, ((0, 0), (0, 128 - 2 * GDN_HEADS))), wbd.T,
           conv_w[0], conv_norm_w, gdn_conv_w[0], prow, prow.T[:8])
    gnw = gdn_norm_w

    meta = jnp.concatenate([jnp.zeros((CHUNK - N_META, d), x.dtype), meta_tokens.astype(x.dtype)])[None]
    tails0 = jnp.zeros((HIST, c + 3 * gw), _F32)
    s00 = jnp.zeros((GDN_HEADS, GDN_HEAD_DIM, GDN_HEAD_DIM), _F32)
    _, _, tails_m, s_m = _mixer(meta, tails0, s00, wts, gnw, lt=CHUNK, lg=CHUNK, nbb=1)

    yc, yg, _, _ = _mixer(x, tails_m[0], s_m[0], wts, gnw, lt=_pick(seq, 512), lg=_pick(seq, 512),
                          nbb=GDN_ROWS if bsz % GDN_ROWS == 0 else 1)

    t = bsz * seq
    tm = _pick(t, 512)
    h1, h1p = _outproj(yc.reshape(t, c), yg.reshape(t, gw), x.reshape(t, d), w_out[0].astype(_BF16),
                       ln1_g, ln1_b, tm=tm)

    tt = _pick(t, 256)
    wr_t = w_router[0].T
    wr_hi = wr_t.astype(_BF16)
    wr_lo = (wr_t - wr_hi.astype(_F32)).astype(_BF16)
    shared_w = (ws_gate[0].astype(_BF16), ws_up[0].astype(_BF16), ws_down[0].astype(_BF16))
    h1p3 = h1p.reshape(t, QUAD, 128)

    parts = MOE_PARTS if t % (MOE_PARTS * tt * SC_WORKERS) == 0 else 1
    tp = t // parts
    nb = tp * TOP_K // ROW_BLOCK + N_EXPERTS
    out = None
    for part in range(parts):
        tile0 = part * (tp // tt)
        idx, gate, rank, cnt = _router(h1, wr_hi, wr_lo, b_router[0][:, None], tt=tt, tile0=tile0, t=tp)
        counts = cnt[:, 0].astype(_I32)
        pcounts = (counts + ROW_BLOCK - 1) // ROW_BLOCK * ROW_BLOCK
        pends = jnp.cumsum(pcounts)
        pstarts = pends - pcounts
        pos = _position(idx, rank, pstarts[:, None].astype(_I32), tt=tt)
        nwin = tt // SC_WINDOW
        pos3 = pos.reshape(tp // tt, TOP_K, nwin, SC_WINDOW).transpose(0, 2, 1, 3)
        pos3 = pos3.reshape(tp // SC_WINDOW, TOP_K, SC_WINDOW)
        xs = _sc_scatter(h1p3, pos3, nb * ROW_BLOCK, part * tp)
        xs = _padfill(counts, pstarts.astype(_I32), pcounts.astype(_I32), xs.reshape(nb * ROW_BLOCK * QUAD, 128))
        ys = _ffn((pstarts // ROW_BLOCK).astype(_I32), (pcounts // ROW_BLOCK).astype(_I32),
                  (pends[-1:] // ROW_BLOCK).astype(_I32), xs, w_gate[0], w_up[0], w_down[0])
        yg = _sc_gather(ys.reshape(nb * ROW_BLOCK, QUAD, 128), pos.reshape(tp * TOP_K))
        yg = yg.reshape(tp // tt, TOP_K, tt * QUAD, 128)
        out = _combine_stream(gate, h1, yg, *shared_w, ln2_g, ln2_b, out, tt=tt, tile0=tile0)
    return out.reshape(bsz, seq, d)
```

```python
import functools

import jax
import jax.numpy as jnp
from jax import lax
from jax.experimental import pallas as pl
from jax.experimental.pallas import tpu as pltpu
from jax.experimental.pallas import tpu_sc as plsc

_F32 = jnp.float32
_BF16 = jnp.bfloat16
_I32 = jnp.int32

D_MODEL = 1024
N_META = 16
CONV_WIDTH = 512
CONV_K = 3
GDN_HEADS = 4
GDN_HEAD_DIM = 128
GDN_WIDTH = GDN_HEADS * GDN_HEAD_DIM
GDN_CONV_K = 4
CHUNK = 64
N_EXPERTS = 256
TOP_K = 8
N_GROUPS = 8
TOPK_GROUPS = 4
E_PER_GROUP = N_EXPERTS // N_GROUPS
EXPERT_FF = 256
ROUTED_SCALE = 2.5
ROW_BLOCK = 256
DN_ALPHA = 2.0 ** 0.25
NORM_EPS = 1e-5
HALF = D_MODEL // 2
QUAD = HALF // 128
STACK = GDN_HEADS * CHUNK
HIST = 8
GDN_ROWS = 4
PREMIX_SUB = 2
SC_CORES = 2
SC_SUBCORES = 16
SC_WORKERS = SC_CORES * SC_SUBCORES
SC_CHUNK = 64
SC_RING = 2
SC_WINDOW = 128
MOE_PARTS = 2
RING = 8
IN_AHEAD = RING - 2

V7X_VMEM_BYTES = 64 * 1024 * 1024
VMEM_LIMIT = V7X_VMEM_BYTES - 8 * 1024 * 1024


def _cparams(*sem):
    return pltpu.CompilerParams(dimension_semantics=sem, vmem_limit_bytes=VMEM_LIMIT)


def _mm(a, b):
    return jnp.dot(a.astype(_BF16), b.astype(_BF16), preferred_element_type=_F32)


def _mm_nt(a, b):
    return lax.dot_general(a.astype(_BF16), b.astype(_BF16), (((1,), (1,)), ((), ())),
                           preferred_element_type=_F32)


def _mm_tn(a, b):
    return lax.dot_general(a.astype(_BF16), b.astype(_BF16), (((0,), (0,)), ((), ())),
                           preferred_element_type=_F32)


def _sigmoid(x):
    return 1.0 / (1.0 + jnp.exp(-x))


def _silu(x):
    return x * _sigmoid(x)


def _softplus(x):
    return jnp.maximum(x, 0.0) + jnp.log1p(jnp.exp(-jnp.abs(x)))


def _pack_halves(y):
    return pltpu.pack_elementwise([y[:, :HALF], y[:, HALF:]], packed_dtype=_BF16)


def _store_rows(ref, packed):
    r = packed.shape[0]
    for c in range(QUAD):
        ref[pl.ds(c, r, stride=QUAD), :] = packed[:, c * 128:(c + 1) * 128]


def _load_rows(ref, r):
    return jnp.concatenate([ref[pl.ds(c, r, stride=QUAD), :] for c in range(QUAD)], axis=1)


def _unpack_halves(p):
    lo = pltpu.unpack_elementwise(p, index=0, packed_dtype=_BF16, unpacked_dtype=_F32)
    hi = pltpu.unpack_elementwise(p, index=1, packed_dtype=_BF16, unpacked_dtype=_F32)
    return lo, hi


def _layer_norm(h, g, b):
    mu = jnp.mean(h, axis=-1, keepdims=True)
    d = h - mu
    var = jnp.mean(d * d, axis=-1, keepdims=True)
    return d * lax.rsqrt(var + NORM_EPS) * g + b


def _premix_body(x_ref, tails_ref, wa_ref, wq_ref, wz_ref, wbd_ref, wbdt_ref, cw_ref, cnw_ref,
                 gcw_ref, prow_ref, pcol_ref,
                 yc_ref, q_ref, k_ref, v_ref, z_ref, bgc_ref, bgr_ref, tout_ref, ext_ref, *, lt):
    cw_ = CONV_WIDTH

    @pl.when(pl.program_id(1) == 0)
    def _():
        ext_ref[0:HIST, :] = tails_ref[...]

    cw = cw_ref[...]
    gcw = gcw_ref[...]
    prow = prow_ref[...]
    pcol = pcol_ref[...]

    def sub_tile(r0, n):
        rows = slice(r0, r0 + n)
        erows = slice(HIST + r0, HIST + r0 + n)
        xb = x_ref[0, rows, :].astype(_BF16)
        pa = jnp.dot(xb, wa_ref[...], preferred_element_type=_F32)
        yield
        gate_b = pa[:, 0:cw_]
        u = pa[:, cw_:2 * cw_] * pa[:, 2 * cw_:3 * cw_]
        ext_ref[erows, 0:cw_] = u
        pq = jnp.dot(xb, wq_ref[...], preferred_element_type=_F32)
        yield
        ext_ref[erows, cw_:] = pq
        zz = jnp.dot(xb, wz_ref[...], preferred_element_type=_F32)
        bdc = jnp.dot(xb, wbd_ref[...], preferred_element_type=_F32)
        bdr = _mm_nt(wbdt_ref[...], xb)
        yield

        ca = u * cw[CONV_K - 1:CONV_K, :]
        for j in range(CONV_K - 1):
            ca = ca + ext_ref[pl.ds(HIST + r0 - (CONV_K - 1) + j, n), 0:cw_] * cw[j:j + 1, :]
        yc = gate_b * ca
        ms = jnp.mean(yc * yc, axis=-1, keepdims=True)
        yc_ref[0, rows, :] = (yc * lax.rsqrt(ms + NORM_EPS) * cnw_ref[...]).astype(_BF16)

        cq = pq * gcw[GDN_CONV_K - 1:GDN_CONV_K, :]
        for j in range(GDN_CONV_K - 1):
            cq = cq + ext_ref[pl.ds(HIST + r0 - (GDN_CONV_K - 1) + j, n), cw_:] * gcw[j:j + 1, :]
        s = _silu(cq)
        for h in range(GDN_HEADS):
            lo, hi = h * GDN_HEAD_DIM, (h + 1) * GDN_HEAD_DIM
            qh = s[:, lo:hi]
            kh = s[:, GDN_WIDTH + lo:GDN_WIDTH + hi]
            qn = qh * lax.rsqrt(jnp.sum(qh * qh, axis=-1, keepdims=True) + 1e-6)
            kn = kh * lax.rsqrt(jnp.sum(kh * kh, axis=-1, keepdims=True) + 1e-6)
            q_ref[0, rows, lo:hi] = (qn * (GDN_HEAD_DIM ** -0.5)).astype(_BF16)
            k_ref[0, rows, lo:hi] = kn.astype(_BF16)
        v_ref[0, rows, :] = s[:, 2 * GDN_WIDTH:].astype(_BF16)
        z_ref[0, rows, :] = zz.astype(_BF16)

        g_c = -jnp.exp(prow[0:1, :]) * _softplus(bdc + prow[1:2, :])
        lane = lax.broadcasted_iota(_I32, bdc.shape, 1)
        bgc_ref[0, rows, :] = jnp.where(lane < GDN_HEADS, _sigmoid(bdc), g_c)
        g_r = -jnp.exp(pcol[:, 0:1]) * _softplus(bdr + pcol[:, 1:2])
        row = lax.broadcasted_iota(_I32, bdr.shape, 0)
        bgr_ref[0, :, rows] = jnp.where(row < GDN_HEADS, _sigmoid(bdr), g_r)

    n_sub = PREMIX_SUB if lt % (PREMIX_SUB * 128) == 0 else 1
    live = [sub_tile(i * (lt // n_sub), lt // n_sub) for i in range(n_sub)]
    while live:
        live = [g for g in live if next(g, live) is not live]

    tail = ext_ref[lt:lt + HIST, :]
    ext_ref[0:HIST, :] = tail
    tout_ref[0] = tail


def _premix(x, tails, wts, *, lt):
    bsz, seq, d = x.shape
    assert seq % lt == 0
    grid = (bsz, seq // lt)
    full = lambda a: pl.BlockSpec(a.shape, lambda b, j: (0,) * a.ndim)
    tok = lambda w: pl.BlockSpec((1, lt, w), lambda b, j: (b, j, 0))
    (wa, wq, wz, wbd, wbdt, cw, cnw, gcw, prow, pcol) = wts
    ext_w = CONV_WIDTH + 3 * GDN_WIDTH
    out_shape = (
        jax.ShapeDtypeStruct((bsz, seq, CONV_WIDTH), _BF16),
        jax.ShapeDtypeStruct((bsz, seq, GDN_WIDTH), _BF16),
        jax.ShapeDtypeStruct((bsz, seq, GDN_WIDTH), _BF16),
        jax.ShapeDtypeStruct((bsz, seq, GDN_WIDTH), _BF16),
        jax.ShapeDtypeStruct((bsz, seq, GDN_WIDTH), _BF16),
        jax.ShapeDtypeStruct((bsz, seq, 128), _F32),
        jax.ShapeDtypeStruct((bsz, 8, seq), _F32),
        jax.ShapeDtypeStruct((bsz, HIST, ext_w), _F32),
    )
    out_specs = (tok(CONV_WIDTH), tok(GDN_WIDTH), tok(GDN_WIDTH), tok(GDN_WIDTH), tok(GDN_WIDTH),
                 tok(128), pl.BlockSpec((1, 8, lt), lambda b, j: (b, 0, j)),
                 pl.BlockSpec((1, HIST, ext_w), lambda b, j: (b, 0, 0)))
    return pl.pallas_call(
        functools.partial(_premix_body, lt=lt),
        grid=grid,
        in_specs=[tok(d), full(tails)] + [full(w) for w in wts],
        out_specs=out_specs,
        out_shape=out_shape,
        scratch_shapes=[pltpu.VMEM((HIST + lt, ext_w), _F32)],
        compiler_params=_cparams("arbitrary", "arbitrary"),
        name="premix",
    )(x, tails, *wts)


def _cumsum_rows(x):
    row = lax.broadcasted_iota(_I32, x.shape, 0)
    s = 1
    while s < x.shape[0]:
        x = x + jnp.where(row >= s, pltpu.roll(x, s, 0), 0.0)
        s *= 2
    return x


def _cumsum_lanes_seg(x):
    lane = lax.broadcasted_iota(_I32, x.shape, 1) & (CHUNK - 1)
    s = 1
    while s < CHUNK:
        x = x + jnp.where(lane >= s, pltpu.roll(x, s, 1), 0.0)
        s *= 2
    return x


def _stack_heads(a):
    return jnp.concatenate([a[:, h * GDN_HEAD_DIM:(h + 1) * GDN_HEAD_DIM] for h in range(GDN_HEADS)], axis=0)


def _gdn_body(q_ref, k_ref, v_ref, z_ref, bgc_ref, grow_ref, s0_ref, gnw_ref,
              y_ref, sout_ref, s_ref, *, nc, nbb):
    @pl.when(pl.program_id(1) == 0)
    def _():
        for r in range(nbb):
            s_ref[r] = s0_ref[...]

    ri = lax.broadcasted_iota(_I32, (STACK, STACK), 0)
    ci = lax.broadcasted_iota(_I32, (STACK, STACK), 1)
    same64 = (ri >> 6) == (ci >> 6)
    same32 = (ri >> 5) == (ci >> 5)
    same16 = (ri >> 4) == (ci >> 4)
    low_incl = same64 & (ri >= ci)
    low_strict = same64 & (ri > ci)
    gnw = gnw_ref[...]

    def chunk_row(r, c):
        off = pl.multiple_of(c * CHUNK, CHUNK)
        q_all = _stack_heads(q_ref[r, pl.ds(off, CHUNK), :].astype(_F32))
        k_all = _stack_heads(k_ref[r, pl.ds(off, CHUNK), :].astype(_F32))
        v_all = _stack_heads(v_ref[r, pl.ds(off, CHUNK), :].astype(_F32))
        bgc = bgc_ref[r, pl.ds(off, CHUNK), :]
        gcs = _cumsum_rows(bgc)
        hd = (CHUNK, GDN_HEAD_DIM)
        beta_b = jnp.concatenate(
            [jnp.broadcast_to(bgc[:, h:h + 1], hd) for h in range(GDN_HEADS)], axis=0)
        gc_b = jnp.concatenate(
            [jnp.broadcast_to(gcs[:, GDN_HEADS + h:GDN_HEADS + h + 1], hd) for h in range(GDN_HEADS)], axis=0)
        gl = [gcs[CHUNK - 1:CHUNK, GDN_HEADS + h:GDN_HEADS + h + 1] for h in range(GDN_HEADS)]
        gl_b = jnp.concatenate([jnp.broadcast_to(g1, hd) for g1 in gl], axis=0)
        gcr = _cumsum_lanes_seg(jnp.broadcast_to(grow_ref[r, c], (8, STACK)))[0:1, :]

        diff = jnp.concatenate([gc_b, gc_b], axis=1) - gcr
        decay = jnp.exp(jnp.where(low_incl, diff, -1e30))
        kb = k_all * beta_b
        a1 = _mm_nt(jnp.concatenate([kb, q_all], axis=0), k_all)
        yield
        m = jnp.where(low_strict, a1[:STACK] * decay, 0.0)
        attn = a1[STACK:] * decay

        l16 = jnp.where(same16, m, 0.0)
        c1 = jnp.where(same32 & jnp.logical_not(same16), m, 0.0)
        c2 = jnp.where(same32, 0.0, m)
        p2 = _mm(l16, l16)
        yield
        p4 = _mm(p2, p2)
        t = _mm(l16, p2)
        yield
        na = p2 - l16 - t
        p8 = _mm(p4, p4)
        t = _mm(na, p4)
        yield
        nb = na + p4 + t
        t = _mm(nb, p8)
        yield
        ncm = nb + p8 + t
        t = _mm(c1, ncm)
        yield
        y1 = c1 + t
        t = _mm(ncm, y1)
        yield
        n1 = ncm - y1 - t
        t = _mm(c2, n1)
        yield
        y2 = c2 + t
        t = _mm(n1, y2)
        yield
        nt = n1 - y2 - t

        egc = jnp.exp(gc_b)
        rhs = jnp.concatenate([v_all * beta_b, kb * egc], axis=1)
        t = _mm(nt, rhs)
        yield
        uw = rhs + t
        u_all = uw[:, :GDN_HEAD_DIM]
        w_all = uw[:, GDN_HEAD_DIM:]
        qd = q_all * egc
        kd = k_all * jnp.exp(gl_b - gc_b)

        bs = []
        for h in range(GDN_HEADS):
            r0, r1 = h * CHUNK, (h + 1) * CHUNK
            bs.append(_mm(jnp.concatenate([w_all[r0:r1], qd[r0:r1]], axis=0), s_ref[r, h]))
        yield
        vn = [u_all[h * CHUNK:(h + 1) * CHUNK] - bs[h][:CHUNK] for h in range(GDN_HEADS)]
        vn_all = jnp.concatenate(vn, axis=0)
        t = _mm(attn, vn_all)
        ds = [_mm_tn(kd[h * CHUNK:(h + 1) * CHUNK], vn[h]) for h in range(GDN_HEADS)]
        yield
        o_all = jnp.concatenate([b[CHUNK:] for b in bs], axis=0) + t
        for h in range(GDN_HEADS):
            r0, r1 = h * CHUNK, (h + 1) * CHUNK
            s_ref[r, h] = s_ref[r, h] * jnp.exp(gl[h]) + ds[h]
            o = o_all[r0:r1]
            zz = z_ref[r, pl.ds(off, CHUNK), h * GDN_HEAD_DIM:(h + 1) * GDN_HEAD_DIM].astype(_F32)
            on = o * lax.rsqrt(jnp.mean(o * o, axis=-1, keepdims=True) + NORM_EPS) * gnw
            y_ref[r, pl.ds(off, CHUNK), h * GDN_HEAD_DIM:(h + 1) * GDN_HEAD_DIM] = (on * _silu(zz)).astype(_BF16)

    def chunk(c, carry):
        live = [chunk_row(r, c) for r in range(nbb)]
        while live:
            live = [g for g in live if next(g, live) is not live]
        return carry

    lax.fori_loop(0, nc, chunk, 0)
    sout_ref[...] = s_ref[...]


def _gdn(q, k, v, z, bgc, grow, s0, gnw, *, lg, nbb):
    bsz, seq, _ = q.shape
    assert seq % lg == 0 and lg % CHUNK == 0 and bsz % nbb == 0
    nc = lg // CHUNK
    tok = lambda w: pl.BlockSpec((nbb, lg, w), lambda b, j: (b, j, 0))
    full = lambda a: pl.BlockSpec(a.shape, lambda b, j: (0,) * a.ndim)
    st = (nbb, GDN_HEADS, GDN_HEAD_DIM, GDN_HEAD_DIM)
    return pl.pallas_call(
        functools.partial(_gdn_body, nc=nc, nbb=nbb),
        grid=(bsz // nbb, seq // lg),
        in_specs=[tok(GDN_WIDTH)] * 4 + [tok(128), pl.BlockSpec((nbb, nc, 1, STACK), lambda b, j: (b, j, 0, 0)),
                                           full(s0), full(gnw)],
        out_specs=(tok(GDN_WIDTH), pl.BlockSpec(st, lambda b, j: (b, 0, 0, 0))),
        out_shape=(jax.ShapeDtypeStruct((bsz, seq, GDN_WIDTH), _BF16),
                   jax.ShapeDtypeStruct((bsz, GDN_HEADS, GDN_HEAD_DIM, GDN_HEAD_DIM), _F32)),
        scratch_shapes=[pltpu.VMEM(st, _F32)],
        compiler_params=_cparams("arbitrary", "arbitrary"),
        name="gdn",
    )(q, k, v, z, bgc, grow, s0, gnw)


def _outproj_body(yc_ref, yg_ref, x_ref, wo_ref, g_ref, b_ref, h1_ref, h1p_ref):
    mix = (jnp.dot(yc_ref[...], wo_ref[0:CONV_WIDTH, :], preferred_element_type=_F32)
           + jnp.dot(yg_ref[...], wo_ref[CONV_WIDTH:, :], preferred_element_type=_F32))
    h1 = _layer_norm(DN_ALPHA * x_ref[...] + mix, g_ref[...], b_ref[...])
    h1_ref[...] = h1
    _store_rows(h1p_ref, _pack_halves(h1))


def _outproj(yc, yg, x2d, wo, g, b, *, tm):
    t = x2d.shape[0]
    assert t % tm == 0
    row = lambda w: pl.BlockSpec((tm, w), lambda i: (i, 0))
    full = lambda a: pl.BlockSpec(a.shape, lambda i: (0,) * a.ndim)
    return pl.pallas_call(
        _outproj_body,
        grid=(t // tm,),
        in_specs=[row(CONV_WIDTH), row(GDN_WIDTH), row(D_MODEL), full(wo), full(g), full(b)],
        out_specs=(row(D_MODEL), pl.BlockSpec((tm * QUAD, 128), lambda i: (i, 0))),
        out_shape=(jax.ShapeDtypeStruct((t, D_MODEL), _F32), jax.ShapeDtypeStruct((t * QUAD, 128), jnp.uint32)),
        compiler_params=_cparams("arbitrary"),
        name="outproj",
    )(yc, yg, x2d, wo, g, b)


def _router_body(h1_ref, wh_ref, wl_ref, br_ref, idx_ref, gate_ref, rank_ref, cnt_ref, carry_ref, *, tt):
    @pl.when(pl.program_id(0) == 0)
    def _():
        carry_ref[...] = jnp.zeros_like(carry_ref)

    x = h1_ref[...]
    xh = x.astype(_BF16)
    xl = (x - xh.astype(_F32)).astype(_BF16)
    wh = wh_ref[...]
    logits = _mm_nt(wh, xh) + _mm_nt(wh, xl) + _mm_nt(wl_ref[...], xh)
    scores = _sigmoid(logits)
    sel = scores + br_ref[...]
    ninf = -jnp.inf

    r32 = lax.broadcasted_iota(_I32, (E_PER_GROUP, tt), 0)
    gsc = []
    for g in range(N_GROUPS):
        xg = sel[g * E_PER_GROUP:(g + 1) * E_PER_GROUP]
        m1 = jnp.max(xg, axis=0, keepdims=True)
        i1 = jnp.min(jnp.where(xg == m1, r32, E_PER_GROUP), axis=0, keepdims=True)
        m2 = jnp.max(jnp.where(r32 == i1, ninf, xg), axis=0, keepdims=True)
        gsc.append(m1 + m2)
    work = jnp.concatenate(gsc, axis=0)
    r8 = lax.broadcasted_iota(_I32, (N_GROUPS, tt), 0)
    gkeep = jnp.zeros((N_GROUPS, tt), _F32)
    for _ in range(TOPK_GROUPS):
        m = jnp.max(work, axis=0, keepdims=True)
        gi = jnp.min(jnp.where(work == m, r8, N_GROUPS), axis=0, keepdims=True)
        pick = r8 == gi
        gkeep = jnp.where(pick, 1.0, gkeep)
        work = jnp.where(pick, ninf, work)
    selm = jnp.concatenate(
        [jnp.where(gkeep[g:g + 1] > 0.5, sel[g * E_PER_GROUP:(g + 1) * E_PER_GROUP], ninf)
         for g in range(N_GROUPS)], axis=0)

    re = lax.broadcasted_iota(_I32, (N_EXPERTS, tt), 0)
    msel = jnp.zeros((N_EXPERTS, tt), _F32)
    idxs, gates = [], []
    for _ in range(TOP_K):
        m = jnp.max(selm, axis=0, keepdims=True)
        ii = jnp.min(jnp.where(selm == m, re, N_EXPERTS), axis=0, keepdims=True)
        hit = re == ii
        idxs.append(ii)
        gates.append(jnp.sum(jnp.where(hit, scores, 0.0), axis=0, keepdims=True))
        selm = jnp.where(hit, ninf, selm)
        msel = jnp.where(hit, 1.0, msel)
    gate = jnp.concatenate(gates, axis=0)
    gate_ref[...] = gate / jnp.sum(gate, axis=0, keepdims=True) * ROUTED_SCALE
    idx_ref[...] = jnp.concatenate(idxs, axis=0)

    ta = lax.broadcasted_iota(_I32, (tt, tt), 0)
    tb = lax.broadcasted_iota(_I32, (tt, tt), 1)
    earlier = jnp.where(ta < tb, 1.0, 0.0)
    carry = carry_ref[...]
    rank_all = _mm(msel, earlier) + carry[:, 0:1]
    rank_ref[...] = jnp.concatenate(
        [jnp.sum(jnp.where(re == ii, rank_all, 0.0), axis=0, keepdims=True) for ii in idxs],
        axis=0).astype(_I32)
    carry = carry + jnp.sum(msel, axis=1, keepdims=True)
    carry_ref[...] = carry
    cnt_ref[...] = carry


def _router(h1, wh, wl, br, *, tt, tile0, t):
    assert t % tt == 0
    full = lambda a: pl.BlockSpec(a.shape, lambda i: (0,) * a.ndim)
    kt = pl.BlockSpec((TOP_K, tt), lambda i: (0, i))
    return pl.pallas_call(
        functools.partial(_router_body, tt=tt),
        grid=(t // tt,),
        in_specs=[pl.BlockSpec((tt, D_MODEL), lambda i: (i + tile0, 0)), full(wh), full(wl), full(br)],
        out_specs=(kt, kt, kt, pl.BlockSpec((N_EXPERTS, 128), lambda i: (0, 0))),
        out_shape=(jax.ShapeDtypeStruct((TOP_K, t), _I32), jax.ShapeDtypeStruct((TOP_K, t), _F32),
                   jax.ShapeDtypeStruct((TOP_K, t), _I32), jax.ShapeDtypeStruct((N_EXPERTS, 128), _F32)),
        scratch_shapes=[pltpu.VMEM((N_EXPERTS, 128), _F32)],
        compiler_params=_cparams("arbitrary"),
        name="router",
    )(h1, wh, wl, br)


def _position_body(idx_ref, rank_ref, pstart_ref, pos_ref, *, tt):
    re = lax.broadcasted_iota(_I32, (N_EXPERTS, tt), 0)
    ps = pstart_ref[...]
    idx = idx_ref[...]
    rows = [jnp.sum(jnp.where(re == idx[k:k + 1], ps, 0), axis=0, keepdims=True) for k in range(TOP_K)]
    pos_ref[0] = jnp.concatenate(rows, axis=0) + rank_ref[...]


def _position(idx, rank, pstart, *, tt):
    t = idx.shape[1]
    kt = pl.BlockSpec((TOP_K, tt), lambda i: (0, i))
    return pl.pallas_call(
        functools.partial(_position_body, tt=tt),
        grid=(t // tt,),
        in_specs=[kt, kt, pl.BlockSpec(pstart.shape, lambda i: (0, 0))],
        out_specs=pl.BlockSpec((1, TOP_K, tt), lambda i: (i, 0, 0)),
        out_shape=jax.ShapeDtypeStruct((t // tt, TOP_K, tt), _I32),
        compiler_params=_cparams("arbitrary"),
        name="position",
    )(idx, rank, pstart)


def _ffn_body(blk0_ref, nblk_ref, ntot_ref, xs_hbm, wg_ref, wu_ref, wd_ref, ys_hbm,
              xbuf, ybuf, sem_in, sem_out, wgu_bf, wd_bf):
    e = pl.program_id(0)
    nblk = nblk_ref[e]
    blk0 = blk0_ref[e]
    ntot = ntot_ref[0]

    blk_rows = ROW_BLOCK * QUAD

    def rows(g):
        return pl.ds(pl.multiple_of(g * blk_rows, blk_rows), blk_rows)

    def in_start(g, slot):
        pltpu.make_async_copy(xs_hbm.at[rows(g)], xbuf.at[slot], sem_in.at[slot]).start()

    def in_wait(slot):
        pltpu.make_async_copy(xs_hbm.at[rows(0)], xbuf.at[slot], sem_in.at[slot]).wait()

    def out_start(g, slot):
        pltpu.make_async_copy(ybuf.at[slot], ys_hbm.at[rows(g)], sem_out.at[slot]).start()

    def out_wait(slot):
        pltpu.make_async_copy(ybuf.at[slot], ys_hbm.at[rows(0)], sem_out.at[slot]).wait()

    @pl.when(e == 0)
    def _():
        for i in range(IN_AHEAD):
            @pl.when(i < ntot)
            def _():
                in_start(i, i)

    @pl.when(nblk > 0)
    def _():
        wgu_bf[:, 0:EXPERT_FF] = wg_ref[0].astype(_BF16)
        wgu_bf[:, EXPERT_FF:] = wu_ref[0].astype(_BF16)
        wd_bf[...] = wd_ref[0].astype(_BF16)

        def acquire(g):
            slot = g & (RING - 1)
            in_wait(slot)

            @pl.when(g + IN_AHEAD < ntot)
            def _():
                in_start(g + IN_AHEAD, (g + IN_AHEAD) & (RING - 1))

            @pl.when(g >= RING)
            def _():
                out_wait(slot)

            return slot

        def compute(slot):
            lo, hi = _unpack_halves(_load_rows(xbuf.at[slot], ROW_BLOCK))
            a = jnp.dot(lo.astype(_BF16), wgu_bf[0:HALF, :], preferred_element_type=_F32)
            yield
            gu = a + jnp.dot(hi.astype(_BF16), wgu_bf[HALF:, :], preferred_element_type=_F32)
            yield
            h = (_silu(gu[:, :EXPERT_FF]) * gu[:, EXPERT_FF:]).astype(_BF16)
            y = jnp.dot(h, wd_bf[...], preferred_element_type=_F32)
            yield
            _store_rows(ybuf.at[slot], _pack_halves(y))

        def run(gs):
            slots = [acquire(g) for g in gs]
            live = [compute(s) for s in slots]
            while live:
                live = [c for c in live if next(c, live) is not live]
            for g, s in zip(gs, slots):
                out_start(g, s)

        def pair(j, carry):
            run([blk0 + 2 * j, blk0 + 2 * j + 1])
            return carry

        lax.fori_loop(0, nblk // 2, pair, 0)

        @pl.when((nblk & 1) == 1)
        def _():
            run([blk0 + nblk - 1])

    @pl.when(e == N_EXPERTS - 1)
    def _():
        for i in range(RING):
            @pl.when(i < ntot)
            def _():
                out_wait((ntot - 1 - i) & (RING - 1))


def _ffn(blk0, nblk, ntot, xs, wg, wu, wd):
    grid_spec = pltpu.PrefetchScalarGridSpec(
        num_scalar_prefetch=3,
        grid=(N_EXPERTS,),
        in_specs=[pl.BlockSpec(memory_space=pl.ANY),
                  pl.BlockSpec((1, D_MODEL, EXPERT_FF), lambda e, *_: (e, 0, 0)),
                  pl.BlockSpec((1, D_MODEL, EXPERT_FF), lambda e, *_: (e, 0, 0)),
                  pl.BlockSpec((1, EXPERT_FF, D_MODEL), lambda e, *_: (e, 0, 0))],
        out_specs=pl.BlockSpec(memory_space=pl.ANY),
        scratch_shapes=[pltpu.VMEM((RING, ROW_BLOCK * QUAD, 128), jnp.uint32),
                        pltpu.VMEM((RING, ROW_BLOCK * QUAD, 128), jnp.uint32),
                        pltpu.SemaphoreType.DMA((RING,)), pltpu.SemaphoreType.DMA((RING,)),
                        pltpu.VMEM((D_MODEL, 2 * EXPERT_FF), _BF16), pltpu.VMEM((EXPERT_FF, D_MODEL), _BF16)],
    )
    return pl.pallas_call(
        _ffn_body,
        grid_spec=grid_spec,
        out_shape=jax.ShapeDtypeStruct(xs.shape, jnp.uint32),
        compiler_params=_cparams("arbitrary"),
        name="ffn",
    )(blk0, nblk, ntot, xs, wg, wu, wd)


def _sc_gather(table, idx):
    b = idx.shape[0]
    nchunk = b // (SC_WORKERS * SC_CHUNK)
    assert nchunk * SC_WORKERS * SC_CHUNK == b and nchunk % SC_RING == 0
    idx2 = idx.reshape(SC_WORKERS * nchunk, SC_CHUNK)
    row = table.shape[1:]
    mesh = plsc.VectorSubcoreMesh(core_axis_name="c", subcore_axis_name="s",
                                  num_cores=SC_CORES, num_subcores=SC_SUBCORES)

    @functools.partial(
        pl.kernel, mesh=mesh,
        out_type=jax.ShapeDtypeStruct((b,) + row, table.dtype),
        scratch_types=[pltpu.VMEM((nchunk, SC_CHUNK), _I32), pltpu.VMEM((SC_RING, SC_CHUNK) + row, table.dtype),
                       pltpu.SemaphoreType.DMA((SC_RING,)), pltpu.SemaphoreType.DMA((SC_RING,))],
        name="sc_gather",
    )
    def gather(table_hbm, idx_hbm, out_hbm, idx_v, rows_v, sem_g, sem_w):
        wid = lax.axis_index("s") * SC_CORES + lax.axis_index("c")
        c0 = wid * nchunk
        pltpu.sync_copy(idx_hbm.at[pl.ds(pl.multiple_of(c0, nchunk), nchunk)], idx_v)

        def fetch(i, s):
            return pltpu.make_async_copy(table_hbm.at[idx_v.at[i]], rows_v.at[s], sem_g.at[s])

        def flush(i, s):
            rows = pl.ds(pl.multiple_of((c0 + i) * SC_CHUNK, SC_CHUNK), SC_CHUNK)
            return pltpu.make_async_copy(rows_v.at[s], out_hbm.at[rows], sem_w.at[s])

        for s in range(SC_RING):
            fetch(s, s).start()

        @pl.loop(0, nchunk, step=SC_RING)
        def _(g):
            for s in range(SC_RING):
                i = g + s
                fetch(i, s).wait()
                flush(i, s).start()
                flush(i, s).wait()

                @pl.when(i + SC_RING < nchunk)
                def _():
                    fetch(i + SC_RING, s).start()

    return gather(table, idx2)


def _sc_scatter(rows, pos3, n_out, row0):
    nchunk, nk, w = pos3.shape
    per_w = nchunk // SC_WORKERS
    assert per_w * SC_WORKERS == nchunk and w <= 128 and row0 % w == 0 and rows.shape[0] >= row0 + nchunk * w
    row = rows.shape[1:]
    mesh = plsc.VectorSubcoreMesh(core_axis_name="c", subcore_axis_name="s",
                                  num_cores=SC_CORES, num_subcores=SC_SUBCORES)

    @functools.partial(
        pl.kernel, mesh=mesh,
        out_type=jax.ShapeDtypeStruct((n_out,) + row, rows.dtype),
        scratch_types=[pltpu.VMEM((nk, w), _I32), pltpu.VMEM((w,) + row, rows.dtype), pltpu.SemaphoreType.DMA],
        name="sc_scatter",
    )
    def scatter(rows_hbm, pos_hbm, out_hbm, idx_v, rows_v, sem):
        wid = lax.axis_index("s") * SC_CORES + lax.axis_index("c")

        @pl.loop(0, per_w)
        def _(i):
            c = wid * per_w + i
            pltpu.sync_copy(pos_hbm.at[c], idx_v)
            pltpu.sync_copy(rows_hbm.at[pl.ds(pl.multiple_of(row0 + c * w, w), w)], rows_v)
            copies = [pltpu.async_copy(rows_v, out_hbm.at[idx_v.at[k]], sem) for k in range(nk)]
            for cp in copies:
                cp.wait()

    return scatter(rows, pos3)


def _padfill_body(cnt_ref, pst_ref, pcn_ref, xs_in, xs_out, zbuf, zsem):
    del xs_in
    zbuf[...] = jnp.zeros_like(zbuf)

    def pad_runs(e, act):
        pad = pcn_ref[e] - cnt_ref[e]
        base = pst_ref[e] + cnt_ref[e]
        for b in range(ROW_BLOCK.bit_length() - 1):
            n = 1 << b

            @pl.when(((pad >> b) & 1) == 1)
            def _():
                off = base + (pad & (n - 1))
                act(pltpu.make_async_copy(zbuf.at[pl.ds(0, QUAD * n)],
                                          xs_out.at[pl.ds(QUAD * off, QUAD * n)], zsem))

    def start_all(e, c):
        pad_runs(e, lambda d: d.start())
        return c

    def wait_all(e, c):
        pad_runs(e, lambda d: d.wait())
        return c

    lax.fori_loop(0, N_EXPERTS, start_all, 0)
    lax.fori_loop(0, N_EXPERTS, wait_all, 0)


def _padfill(counts, pstarts, pcounts, xs):
    grid_spec = pltpu.PrefetchScalarGridSpec(
        num_scalar_prefetch=3,
        grid=(1,),
        in_specs=[pl.BlockSpec(memory_space=pl.ANY)],
        out_specs=pl.BlockSpec(memory_space=pl.ANY),
        scratch_shapes=[pltpu.VMEM((QUAD * ROW_BLOCK // 2, 128), jnp.uint32), pltpu.SemaphoreType.DMA],
    )
    return pl.pallas_call(
        _padfill_body,
        grid_spec=grid_spec,
        out_shape=jax.ShapeDtypeStruct(xs.shape, xs.dtype),
        input_output_aliases={3: 0},
        compiler_params=_cparams("arbitrary"),
        name="padfill",
    )(counts, pstarts, pcounts, xs)


def _combine_stream_body(gate_ref, h1_ref, yg_ref, wsg_ref, wsu_ref, wsd_ref, g_ref, b_ref, out_ref, *, tt):
    x = h1_ref[...]
    xb = x.astype(_BF16)
    shared = _mm(_silu(_mm(xb, wsg_ref[...])) * _mm(xb, wsu_ref[...]), wsd_ref[...])
    gcol = gate_ref[...].T
    acc_lo = jnp.zeros((tt, HALF), _F32)
    acc_hi = jnp.zeros((tt, HALF), _F32)
    for k in range(TOP_K):
        lo, hi = _unpack_halves(_load_rows(yg_ref.at[0, k], tt))
        acc_lo = acc_lo + gcol[:, k:k + 1] * lo
        acc_hi = acc_hi + gcol[:, k:k + 1] * hi
    routed = jnp.concatenate([acc_lo, acc_hi], axis=1)
    out_ref[...] = _layer_norm(DN_ALPHA * x + (routed + shared), g_ref[...], b_ref[...])


def _combine_stream(gate, h1, yg, wsg, wsu, wsd, g, b, out_prev, *, tt, tile0):
    t_all = h1.shape[0]
    t = gate.shape[1]
    full = lambda a: pl.BlockSpec(a.shape, lambda i: (0,) * a.ndim)
    in_specs = [pl.BlockSpec((TOP_K, tt), lambda i: (0, i)), pl.BlockSpec((tt, D_MODEL), lambda i: (i + tile0, 0)),
                pl.BlockSpec((1, TOP_K, tt * QUAD, 128), lambda i: (i, 0, 0, 0)),
                full(wsg), full(wsu), full(wsd), full(g), full(b)]
    args = [gate, h1, yg, wsg, wsu, wsd, g, b]
    aliases = {}
    body = functools.partial(_combine_stream_body, tt=tt)
    if out_prev is not None:
        in_specs.append(pl.BlockSpec(memory_space=pl.ANY))
        args.append(out_prev)
        aliases = {len(args) - 1: 0}
        body = lambda *refs: _combine_stream_body(*refs[:8], refs[9], tt=tt)
    return pl.pallas_call(
        body,
        grid=(t // tt,),
        in_specs=in_specs,
        out_specs=pl.BlockSpec((tt, D_MODEL), lambda i: (i + tile0, 0)),
        out_shape=jax.ShapeDtypeStruct((t_all, D_MODEL), _F32),
        input_output_aliases=aliases,
        compiler_params=_cparams("arbitrary"),
        name="combine",
    )(*args)


def _pick(n, pref):
    t = min(n, pref)
    while n % t:
        t -= CHUNK
    return t


def _mixer(x, tails, s0, wts, gnw, *, lt, lg, nbb):
    yc, q, k, v, z, bgc, bgr, tails_out = _premix(x, tails, wts, lt=lt)
    bsz, seq, _ = x.shape
    nch = seq // CHUNK
    grow = bgr[:, GDN_HEADS:2 * GDN_HEADS, :].reshape(bsz, GDN_HEADS, nch, CHUNK)
    grow = grow.transpose(0, 2, 1, 3).reshape(bsz, nch, 1, STACK)
    yg, s_out = _gdn(q, k, v, z, bgc, grow, s0, gnw, lg=lg, nbb=nbb)
    return yc, yg, tails_out, s_out


def kernel(x, meta_tokens, w_in, conv_w, conv_norm_w, gdn_conv_w, a_log, dt_bias, gdn_norm_w, w_out,
           ln1_g, ln1_b, w_router, b_router, w_gate, w_up, w_down, ws_gate, ws_up, ws_down, ln2_g, ln2_b):
    assert w_in.shape[0] == 1, "single-layer stack"
    bsz, seq, d = x.shape
    assert d == D_MODEL and seq % CHUNK == 0
    c, gw = CONV_WIDTH, GDN_WIDTH
    win = w_in[0].astype(_BF16)
    wbd = win[:, 3 * c + 4 * gw:]
    zpad = jnp.zeros((128 - 2 * GDN_HEADS,), _F32)
    zpad4 = jnp.zeros((GDN_HEADS,), _F32)
    prow = jnp.zeros((8, 128), _F32)
    prow = prow.at[0].set(jnp.concatenate([zpad4, a_log[0], zpad]))
    prow = prow.at[1].set(jnp.concatenate([zpad4, dt_bias[0], zpad]))
    wts = (win[:, :3 * c], win[:, 3 * c:3 * c + 3 * gw], win[:, 3 * c + 3 * gw:3 * c + 4 * gw],
           jnp.pad(wbd, ((0, 0), (0, 128 - 2 * GDN_HEADS))), wbd.T,
           conv_w[0], conv_norm_w, gdn_conv_w[0], prow, prow.T[:8])
    gnw = gdn_norm_w

    meta = jnp.concatenate([jnp.zeros((CHUNK - N_META, d), x.dtype), meta_tokens.astype(x.dtype)])[None]
    tails0 = jnp.zeros((HIST, c + 3 * gw), _F32)
    s00 = jnp.zeros((GDN_HEADS, GDN_HEAD_DIM, GDN_HEAD_DIM), _F32)
    _, _, tails_m, s_m = _mixer(meta, tails0, s00, wts, gnw, lt=CHUNK, lg=CHUNK, nbb=1)

    yc, yg, _, _ = _mixer(x, tails_m[0], s_m[0], wts, gnw, lt=_pick(seq, 512), lg=_pick(seq, 512),
                          nbb=GDN_ROWS if bsz % GDN_ROWS == 0 else 1)

    t = bsz * seq
    tm = _pick(t, 512)
    h1, h1p = _outproj(yc.reshape(t, c), yg.reshape(t, gw), x.reshape(t, d), w_out[0].astype(_BF16),
                       ln1_g, ln1_b, tm=tm)

    tt = _pick(t, 256)
    wr_t = w_router[0].T
    wr_hi = wr_t.astype(_BF16)
    wr_lo = (wr_t - wr_hi.astype(_F32)).astype(_BF16)
    shared_w = (ws_gate[0].astype(_BF16), ws_up[0].astype(_BF16), ws_down[0].astype(_BF16))
    h1p3 = h1p.reshape(t, QUAD, 128)

    parts = MOE_PARTS if t % (MOE_PARTS * tt * SC_WORKERS) == 0 else 1
    tp = t // parts
    nb = tp * TOP_K // ROW_BLOCK + N_EXPERTS
    out = None
    for part in range(parts):
        tile0 = part * (tp // tt)
        idx, gate, rank, cnt = _router(h1, wr_hi, wr_lo, b_router[0][:, None], tt=tt, tile0=tile0, t=tp)
        counts = cnt[:, 0].astype(_I32)
        pcounts = (counts + ROW_BLOCK - 1) // ROW_BLOCK * ROW_BLOCK
        pends = jnp.cumsum(pcounts)
        pstarts = pends - pcounts
        pos = _position(idx, rank, pstarts[:, None].astype(_I32), tt=tt)
        nwin = tt // SC_WINDOW
        pos3 = pos.reshape(tp // tt, TOP_K, nwin, SC_WINDOW).transpose(0, 2, 1, 3)
        pos3 = pos3.reshape(tp // SC_WINDOW, TOP_K, SC_WINDOW)
        xs = _sc_scatter(h1p3, pos3, nb * ROW_BLOCK, part * tp)
        xs = _padfill(counts, pstarts.astype(_I32), pcounts.astype(_I32), xs.reshape(nb * ROW_BLOCK * QUAD, 128))
        ys = _ffn((pstarts // ROW_BLOCK).astype(_I32), (pcounts // ROW_BLOCK).astype(_I32),
                  (pends[-1:] // ROW_BLOCK).astype(_I32), xs, w_gate[0], w_up[0], w_down[0])
        yg = _sc_gather(ys.reshape(nb * ROW_BLOCK, QUAD, 128), pos.reshape(tp * TOP_K))
        yg = yg.reshape(tp // tt, TOP_K, tt * QUAD, 128)
        out = _combine_stream(gate, h1, yg, *shared_w, ln2_g, ln2_b, out, tt=tt, tile0=tile0)
    return out.reshape(bsz, seq, d)
```

```python
import functools

import jax
import jax.numpy as jnp
from jax import lax
from jax.experimental import pallas as pl
from jax.experimental.pallas import tpu as pltpu
from jax.experimental.pallas import tpu_sc as plsc

_F32 = jnp.float32
_BF16 = jnp.bfloat16
_I32 = jnp.int32

D_MODEL = 1024
N_META = 16
CONV_WIDTH = 512
CONV_K = 3
GDN_HEADS = 4
GDN_HEAD_DIM = 128
GDN_WIDTH = GDN_HEADS * GDN_HEAD_DIM
GDN_CONV_K = 4
CHUNK = 64
N_EXPERTS = 256
TOP_K = 8
N_GROUPS = 8
TOPK_GROUPS = 4
E_PER_GROUP = N_EXPERTS // N_GROUPS
EXPERT_FF = 256
ROUTED_SCALE = 2.5
ROW_BLOCK = 256
DN_ALPHA = 2.0 ** 0.25
NORM_EPS = 1e-5
HALF = D_MODEL // 2
QUAD = HALF // 128
STACK = GDN_HEADS * CHUNK
HIST = 8
GDN_ROWS = 4
PREMIX_SUB = 2
SC_CORES = 2
SC_SUBCORES = 16
SC_WORKERS = SC_CORES * SC_SUBCORES
SC_LANES = 16
SC_CHUNK = 64
SC_RING = 2
SC_WINDOW = 128
MOE_PARTS = 2
RING = 8
IN_AHEAD = RING - 2

V7X_VMEM_BYTES = 64 * 1024 * 1024
VMEM_LIMIT = V7X_VMEM_BYTES - 8 * 1024 * 1024


def _cparams(*sem):
    return pltpu.CompilerParams(dimension_semantics=sem, vmem_limit_bytes=VMEM_LIMIT)


def _mm(a, b):
    return jnp.dot(a.astype(_BF16), b.astype(_BF16), preferred_element_type=_F32)


def _mm_nt(a, b):
    return lax.dot_general(a.astype(_BF16), b.astype(_BF16), (((1,), (1,)), ((), ())),
                           preferred_element_type=_F32)


def _mm_tn(a, b):
    return lax.dot_general(a.astype(_BF16), b.astype(_BF16), (((0,), (0,)), ((), ())),
                           preferred_element_type=_F32)


def _sigmoid(x):
    return 1.0 / (1.0 + jnp.exp(-x))


def _silu(x):
    return x * _sigmoid(x)


def _softplus(x):
    return jnp.maximum(x, 0.0) + jnp.log1p(jnp.exp(-jnp.abs(x)))


def _pack_halves(y):
    return pltpu.pack_elementwise([y[:, :HALF], y[:, HALF:]], packed_dtype=_BF16)


def _store_rows(ref, packed):
    r = packed.shape[0]
    for c in range(QUAD):
        ref[pl.ds(c, r, stride=QUAD), :] = packed[:, c * 128:(c + 1) * 128]


def _load_rows(ref, r):
    return jnp.concatenate([ref[pl.ds(c, r, stride=QUAD), :] for c in range(QUAD)], axis=1)


def _unpack_halves(p):
    lo = pltpu.unpack_elementwise(p, index=0, packed_dtype=_BF16, unpacked_dtype=_F32)
    hi = pltpu.unpack_elementwise(p, index=1, packed_dtype=_BF16, unpacked_dtype=_F32)
    return lo, hi


def _layer_norm(h, g, b):
    mu = jnp.mean(h, axis=-1, keepdims=True)
    d = h - mu
    var = jnp.mean(d * d, axis=-1, keepdims=True)
    return d * lax.rsqrt(var + NORM_EPS) * g + b


def _premix_body(x_ref, tails_ref, wa_ref, wq_ref, wz_ref, wbd_ref, wbdt_ref, cw_ref, cnw_ref,
                 gcw_ref, prow_ref, pcol_ref,
                 yc_ref, q_ref, k_ref, v_ref, z_ref, bgc_ref, bgr_ref, tout_ref, ext_ref, *, lt):
    cw_ = CONV_WIDTH

    @pl.when(pl.program_id(1) == 0)
    def _():
        ext_ref[0:HIST, :] = tails_ref[...]

    cw = cw_ref[...]
    gcw = gcw_ref[...]
    prow = prow_ref[...]
    pcol = pcol_ref[...]

    def sub_tile(r0, n):
        rows = slice(r0, r0 + n)
        erows = slice(HIST + r0, HIST + r0 + n)
        xb = x_ref[0, rows, :].astype(_BF16)
        pa = jnp.dot(xb, wa_ref[...], preferred_element_type=_F32)
        yield
        gate_b = pa[:, 0:cw_]
        u = pa[:, cw_:2 * cw_] * pa[:, 2 * cw_:3 * cw_]
        ext_ref[erows, 0:cw_] = u
        pq = jnp.dot(xb, wq_ref[...], preferred_element_type=_F32)
        yield
        ext_ref[erows, cw_:] = pq
        zz = jnp.dot(xb, wz_ref[...], preferred_element_type=_F32)
        bdc = jnp.dot(xb, wbd_ref[...], preferred_element_type=_F32)
        bdr = _mm_nt(wbdt_ref[...], xb)
        yield

        ca = u * cw[CONV_K - 1:CONV_K, :]
        for j in range(CONV_K - 1):
            ca = ca + ext_ref[pl.ds(HIST + r0 - (CONV_K - 1) + j, n), 0:cw_] * cw[j:j + 1, :]
        yc = gate_b * ca
        ms = jnp.mean(yc * yc, axis=-1, keepdims=True)
        yc_ref[0, rows, :] = (yc * lax.rsqrt(ms + NORM_EPS) * cnw_ref[...]).astype(_BF16)

        cq = pq * gcw[GDN_CONV_K - 1:GDN_CONV_K, :]
        for j in range(GDN_CONV_K - 1):
            cq = cq + ext_ref[pl.ds(HIST + r0 - (GDN_CONV_K - 1) + j, n), cw_:] * gcw[j:j + 1, :]
        s = _silu(cq)
        for h in range(GDN_HEADS):
            lo, hi = h * GDN_HEAD_DIM, (h + 1) * GDN_HEAD_DIM
            qh = s[:, lo:hi]
            kh = s[:, GDN_WIDTH + lo:GDN_WIDTH + hi]
            qn = qh * lax.rsqrt(jnp.sum(qh * qh, axis=-1, keepdims=True) + 1e-6)
            kn = kh * lax.rsqrt(jnp.sum(kh * kh, axis=-1, keepdims=True) + 1e-6)
            q_ref[0, rows, lo:hi] = (qn * (GDN_HEAD_DIM ** -0.5)).astype(_BF16)
            k_ref[0, rows, lo:hi] = kn.astype(_BF16)
        v_ref[0, rows, :] = s[:, 2 * GDN_WIDTH:].astype(_BF16)
        z_ref[0, rows, :] = zz.astype(_BF16)

        g_c = -jnp.exp(prow[0:1, :]) * _softplus(bdc + prow[1:2, :])
        lane = lax.broadcasted_iota(_I32, bdc.shape, 1)
        bgc_ref[0, rows, :] = jnp.where(lane < GDN_HEADS, _sigmoid(bdc), g_c)
        g_r = -jnp.exp(pcol[:, 0:1]) * _softplus(bdr + pcol[:, 1:2])
        row = lax.broadcasted_iota(_I32, bdr.shape, 0)
        bgr_ref[0, :, rows] = jnp.where(row < GDN_HEADS, _sigmoid(bdr), g_r)

    n_sub = PREMIX_SUB if lt % (PREMIX_SUB * 128) == 0 else 1
    live = [sub_tile(i * (lt // n_sub), lt // n_sub) for i in range(n_sub)]
    while live:
        live = [g for g in live if next(g, live) is not live]

    tail = ext_ref[lt:lt + HIST, :]
    ext_ref[0:HIST, :] = tail
    tout_ref[0] = tail


def _premix(x, tails, wts, *, lt):
    bsz, seq, d = x.shape
    assert seq % lt == 0
    grid = (bsz, seq // lt)
    full = lambda a: pl.BlockSpec(a.shape, lambda b, j: (0,) * a.ndim)
    tok = lambda w: pl.BlockSpec((1, lt, w), lambda b, j: (b, j, 0))
    (wa, wq, wz, wbd, wbdt, cw, cnw, gcw, prow, pcol) = wts
    ext_w = CONV_WIDTH + 3 * GDN_WIDTH
    out_shape = (
        jax.ShapeDtypeStruct((bsz, seq, CONV_WIDTH), _BF16),
        jax.ShapeDtypeStruct((bsz, seq, GDN_WIDTH), _BF16),
        jax.ShapeDtypeStruct((bsz, seq, GDN_WIDTH), _BF16),
        jax.ShapeDtypeStruct((bsz, seq, GDN_WIDTH), _BF16),
        jax.ShapeDtypeStruct((bsz, seq, GDN_WIDTH), _BF16),
        jax.ShapeDtypeStruct((bsz, seq, 128), _F32),
        jax.ShapeDtypeStruct((bsz, 8, seq), _F32),
        jax.ShapeDtypeStruct((bsz, HIST, ext_w), _F32),
    )
    out_specs = (tok(CONV_WIDTH), tok(GDN_WIDTH), tok(GDN_WIDTH), tok(GDN_WIDTH), tok(GDN_WIDTH),
                 tok(128), pl.BlockSpec((1, 8, lt), lambda b, j: (b, 0, j)),
                 pl.BlockSpec((1, HIST, ext_w), lambda b, j: (b, 0, 0)))
    return pl.pallas_call(
        functools.partial(_premix_body, lt=lt),
        grid=grid,
        in_specs=[tok(d), full(tails)] + [full(w) for w in wts],
        out_specs=out_specs,
        out_shape=out_shape,
        scratch_shapes=[pltpu.VMEM((HIST + lt, ext_w), _F32)],
        compiler_params=_cparams("arbitrary", "arbitrary"),
        name="premix",
    )(x, tails, *wts)


def _cumsum_rows(x):
    row = lax.broadcasted_iota(_I32, x.shape, 0)
    s = 1
    while s < x.shape[0]:
        x = x + jnp.where(row >= s, pltpu.roll(x, s, 0), 0.0)
        s *= 2
    return x


def _cumsum_lanes_seg(x):
    lane = lax.broadcasted_iota(_I32, x.shape, 1) & (CHUNK - 1)
    s = 1
    while s < CHUNK:
        x = x + jnp.where(lane >= s, pltpu.roll(x, s, 1), 0.0)
        s *= 2
    return x


def _stack_heads(a):
    return jnp.concatenate([a[:, h * GDN_HEAD_DIM:(h + 1) * GDN_HEAD_DIM] for h in range(GDN_HEADS)], axis=0)


def _gdn_body(q_ref, k_ref, v_ref, z_ref, bgc_ref, grow_ref, s0_ref, gnw_ref,
              y_ref, sout_ref, s_ref, *, nc, nbb):
    @pl.when(pl.program_id(1) == 0)
    def _():
        for r in range(nbb):
            s_ref[r] = s0_ref[...]

    ri = lax.broadcasted_iota(_I32, (STACK, STACK), 0)
    ci = lax.broadcasted_iota(_I32, (STACK, STACK), 1)
    same64 = (ri >> 6) == (ci >> 6)
    same32 = (ri >> 5) == (ci >> 5)
    same16 = (ri >> 4) == (ci >> 4)
    low_incl = same64 & (ri >= ci)
    low_strict = same64 & (ri > ci)
    gnw = gnw_ref[...]

    def chunk_row(r, c):
        off = pl.multiple_of(c * CHUNK, CHUNK)
        q_all = _stack_heads(q_ref[r, pl.ds(off, CHUNK), :].astype(_F32))
        k_all = _stack_heads(k_ref[r, pl.ds(off, CHUNK), :].astype(_F32))
        v_all = _stack_heads(v_ref[r, pl.ds(off, CHUNK), :].astype(_F32))
        bgc = bgc_ref[r, pl.ds(off, CHUNK), :]
        gcs = _cumsum_rows(bgc)
        hd = (CHUNK, GDN_HEAD_DIM)
        beta_b = jnp.concatenate(
            [jnp.broadcast_to(bgc[:, h:h + 1], hd) for h in range(GDN_HEADS)], axis=0)
        gc_b = jnp.concatenate(
            [jnp.broadcast_to(gcs[:, GDN_HEADS + h:GDN_HEADS + h + 1], hd) for h in range(GDN_HEADS)], axis=0)
        gl = [gcs[CHUNK - 1:CHUNK, GDN_HEADS + h:GDN_HEADS + h + 1] for h in range(GDN_HEADS)]
        gl_b = jnp.concatenate([jnp.broadcast_to(g1, hd) for g1 in gl], axis=0)
        gcr = _cumsum_lanes_seg(jnp.broadcast_to(grow_ref[r, c], (8, STACK)))[0:1, :]

        diff = jnp.concatenate([gc_b, gc_b], axis=1) - gcr
        decay = jnp.exp(jnp.where(low_incl, diff, -1e30))
        kb = k_all * beta_b
        a1 = _mm_nt(jnp.concatenate([kb, q_all], axis=0), k_all)
        yield
        m = jnp.where(low_strict, a1[:STACK] * decay, 0.0)
        attn = a1[STACK:] * decay

        l16 = jnp.where(same16, m, 0.0)
        c1 = jnp.where(same32 & jnp.logical_not(same16), m, 0.0)
        c2 = jnp.where(same32, 0.0, m)
        p2 = _mm(l16, l16)
        yield
        p4 = _mm(p2, p2)
        t = _mm(l16, p2)
        yield
        na = p2 - l16 - t
        p8 = _mm(p4, p4)
        t = _mm(na, p4)
        yield
        nb = na + p4 + t
        t = _mm(nb, p8)
        yield
        ncm = nb + p8 + t
        t = _mm(c1, ncm)
        yield
        y1 = c1 + t
        t = _mm(ncm, y1)
        yield
        n1 = ncm - y1 - t
        t = _mm(c2, n1)
        yield
        y2 = c2 + t
        t = _mm(n1, y2)
        yield
        nt = n1 - y2 - t

        egc = jnp.exp(gc_b)
        rhs = jnp.concatenate([v_all * beta_b, kb * egc], axis=1)
        t = _mm(nt, rhs)
        yield
        uw = rhs + t
        u_all = uw[:, :GDN_HEAD_DIM]
        w_all = uw[:, GDN_HEAD_DIM:]
        qd = q_all * egc
        kd = k_all * jnp.exp(gl_b - gc_b)

        bs = []
        for h in range(GDN_HEADS):
            r0, r1 = h * CHUNK, (h + 1) * CHUNK
            bs.append(_mm(jnp.concatenate([w_all[r0:r1], qd[r0:r1]], axis=0), s_ref[r, h]))
        yield
        vn = [u_all[h * CHUNK:(h + 1) * CHUNK] - bs[h][:CHUNK] for h in range(GDN_HEADS)]
        vn_all = jnp.concatenate(vn, axis=0)
        t = _mm(attn, vn_all)
        ds = [_mm_tn(kd[h * CHUNK:(h + 1) * CHUNK], vn[h]) for h in range(GDN_HEADS)]
        yield
        o_all = jnp.concatenate([b[CHUNK:] for b in bs], axis=0) + t
        for h in range(GDN_HEADS):
            r0, r1 = h * CHUNK, (h + 1) * CHUNK
            s_ref[r, h] = s_ref[r, h] * jnp.exp(gl[h]) + ds[h]
            o = o_all[r0:r1]
            zz = z_ref[r, pl.ds(off, CHUNK), h * GDN_HEAD_DIM:(h + 1) * GDN_HEAD_DIM].astype(_F32)
            on = o * lax.rsqrt(jnp.mean(o * o, axis=-1, keepdims=True) + NORM_EPS) * gnw
            y_ref[r, pl.ds(off, CHUNK), h * GDN_HEAD_DIM:(h + 1) * GDN_HEAD_DIM] = (on * _silu(zz)).astype(_BF16)

    def chunk(c, carry):
        live = [chunk_row(r, c) for r in range(nbb)]
        while live:
            live = [g for g in live if next(g, live) is not live]
        return carry

    lax.fori_loop(0, nc, chunk, 0)
    sout_ref[...] = s_ref[...]


def _gdn(q, k, v, z, bgc, grow, s0, gnw, *, lg, nbb):
    bsz, seq, _ = q.shape
    assert seq % lg == 0 and lg % CHUNK == 0 and bsz % nbb == 0
    nc = lg // CHUNK
    tok = lambda w: pl.BlockSpec((nbb, lg, w), lambda b, j: (b, j, 0))
    full = lambda a: pl.BlockSpec(a.shape, lambda b, j: (0,) * a.ndim)
    st = (nbb, GDN_HEADS, GDN_HEAD_DIM, GDN_HEAD_DIM)
    return pl.pallas_call(
        functools.partial(_gdn_body, nc=nc, nbb=nbb),
        grid=(bsz // nbb, seq // lg),
        in_specs=[tok(GDN_WIDTH)] * 4 + [tok(128), pl.BlockSpec((nbb, nc, 1, STACK), lambda b, j: (b, j, 0, 0)),
                                           full(s0), full(gnw)],
        out_specs=(tok(GDN_WIDTH), pl.BlockSpec(st, lambda b, j: (b, 0, 0, 0))),
        out_shape=(jax.ShapeDtypeStruct((bsz, seq, GDN_WIDTH), _BF16),
                   jax.ShapeDtypeStruct((bsz, GDN_HEADS, GDN_HEAD_DIM, GDN_HEAD_DIM), _F32)),
        scratch_shapes=[pltpu.VMEM(st, _F32)],
        compiler_params=_cparams("arbitrary", "arbitrary"),
        name="gdn",
    )(q, k, v, z, bgc, grow, s0, gnw)


def _outproj_body(yc_ref, yg_ref, x_ref, wo_ref, g_ref, b_ref, h1_ref, h1p_ref):
    mix = (jnp.dot(yc_ref[...], wo_ref[0:CONV_WIDTH, :], preferred_element_type=_F32)
           + jnp.dot(yg_ref[...], wo_ref[CONV_WIDTH:, :], preferred_element_type=_F32))
    h1 = _layer_norm(DN_ALPHA * x_ref[...] + mix, g_ref[...], b_ref[...])
    h1_ref[...] = h1
    _store_rows(h1p_ref, _pack_halves(h1))


def _outproj(yc, yg, x2d, wo, g, b, *, tm):
    t = x2d.shape[0]
    assert t % tm == 0
    row = lambda w: pl.BlockSpec((tm, w), lambda i: (i, 0))
    full = lambda a: pl.BlockSpec(a.shape, lambda i: (0,) * a.ndim)
    return pl.pallas_call(
        _outproj_body,
        grid=(t // tm,),
        in_specs=[row(CONV_WIDTH), row(GDN_WIDTH), row(D_MODEL), full(wo), full(g), full(b)],
        out_specs=(row(D_MODEL), pl.BlockSpec((tm * QUAD, 128), lambda i: (i, 0))),
        out_shape=(jax.ShapeDtypeStruct((t, D_MODEL), _F32), jax.ShapeDtypeStruct((t * QUAD, 128), jnp.uint32)),
        compiler_params=_cparams("arbitrary"),
        name="outproj",
    )(yc, yg, x2d, wo, g, b)


def _router_body(h1_ref, wh_ref, wl_ref, br_ref, idx_ref, gate_ref, rank_ref, cnt_ref, carry_ref, *, tt):
    @pl.when(pl.program_id(0) == 0)
    def _():
        carry_ref[...] = jnp.zeros_like(carry_ref)

    x = h1_ref[...]
    xh = x.astype(_BF16)
    xl = (x - xh.astype(_F32)).astype(_BF16)
    wh = wh_ref[...]
    logits = _mm_nt(wh, xh) + _mm_nt(wh, xl) + _mm_nt(wl_ref[...], xh)
    scores = _sigmoid(logits)
    sel = scores + br_ref[...]
    ninf = -jnp.inf

    r32 = lax.broadcasted_iota(_I32, (E_PER_GROUP, tt), 0)
    gsc = []
    for g in range(N_GROUPS):
        xg = sel[g * E_PER_GROUP:(g + 1) * E_PER_GROUP]
        m1 = jnp.max(xg, axis=0, keepdims=True)
        i1 = jnp.min(jnp.where(xg == m1, r32, E_PER_GROUP), axis=0, keepdims=True)
        m2 = jnp.max(jnp.where(r32 == i1, ninf, xg), axis=0, keepdims=True)
        gsc.append(m1 + m2)
    work = jnp.concatenate(gsc, axis=0)
    r8 = lax.broadcasted_iota(_I32, (N_GROUPS, tt), 0)
    gkeep = jnp.zeros((N_GROUPS, tt), _F32)
    for _ in range(TOPK_GROUPS):
        m = jnp.max(work, axis=0, keepdims=True)
        gi = jnp.min(jnp.where(work == m, r8, N_GROUPS), axis=0, keepdims=True)
        pick = r8 == gi
        gkeep = jnp.where(pick, 1.0, gkeep)
        work = jnp.where(pick, ninf, work)
    selm = jnp.concatenate(
        [jnp.where(gkeep[g:g + 1] > 0.5, sel[g * E_PER_GROUP:(g + 1) * E_PER_GROUP], ninf)
         for g in range(N_GROUPS)], axis=0)

    re = lax.broadcasted_iota(_I32, (N_EXPERTS, tt), 0)
    msel = jnp.zeros((N_EXPERTS, tt), _F32)
    idxs, gates = [], []
    for _ in range(TOP_K):
        m = jnp.max(selm, axis=0, keepdims=True)
        ii = jnp.min(jnp.where(selm == m, re, N_EXPERTS), axis=0, keepdims=True)
        hit = re == ii
        idxs.append(ii)
        gates.append(jnp.sum(jnp.where(hit, scores, 0.0), axis=0, keepdims=True))
        selm = jnp.where(hit, ninf, selm)
        msel = jnp.where(hit, 1.0, msel)
    gate = jnp.concatenate(gates, axis=0)
    gate_ref[...] = gate / jnp.sum(gate, axis=0, keepdims=True) * ROUTED_SCALE
    idx_ref[...] = jnp.concatenate(idxs, axis=0)

    ta = lax.broadcasted_iota(_I32, (tt, tt), 0)
    tb = lax.broadcasted_iota(_I32, (tt, tt), 1)
    earlier = jnp.where(ta < tb, 1.0, 0.0)
    carry = carry_ref[...]
    rank_all = _mm(msel, earlier) + carry[:, 0:1]
    rank_ref[...] = jnp.concatenate(
        [jnp.sum(jnp.where(re == ii, rank_all, 0.0), axis=0, keepdims=True) for ii in idxs],
        axis=0).astype(_I32)
    carry = carry + jnp.sum(msel, axis=1, keepdims=True)
    carry_ref[...] = carry
    cnt_ref[...] = carry


def _router(h1, wh, wl, br, *, tt, tile0, t):
    assert t % tt == 0
    full = lambda a: pl.BlockSpec(a.shape, lambda i: (0,) * a.ndim)
    kt = pl.BlockSpec((TOP_K, tt), lambda i: (0, i))
    return pl.pallas_call(
        functools.partial(_router_body, tt=tt),
        grid=(t // tt,),
        in_specs=[pl.BlockSpec((tt, D_MODEL), lambda i: (i + tile0, 0)), full(wh), full(wl), full(br)],
        out_specs=(kt, kt, kt, pl.BlockSpec((N_EXPERTS, 128), lambda i: (0, 0))),
        out_shape=(jax.ShapeDtypeStruct((TOP_K, t), _I32), jax.ShapeDtypeStruct((TOP_K, t), _F32),
                   jax.ShapeDtypeStruct((TOP_K, t), _I32), jax.ShapeDtypeStruct((N_EXPERTS, 128), _F32)),
        scratch_shapes=[pltpu.VMEM((N_EXPERTS, 128), _F32)],
        compiler_params=_cparams("arbitrary"),
        name="router",
    )(h1, wh, wl, br)


def _position_body(idx_ref, rank_ref, pstart_ref, pos_ref, *, tt):
    re = lax.broadcasted_iota(_I32, (N_EXPERTS, tt), 0)
    ps = pstart_ref[...]
    idx = idx_ref[...]
    rows = [jnp.sum(jnp.where(re == idx[k:k + 1], ps, 0), axis=0, keepdims=True) for k in range(TOP_K)]
    pos_ref[0] = jnp.concatenate(rows, axis=0) + rank_ref[...]


def _position(idx, rank, pstart, *, tt):
    t = idx.shape[1]
    kt = pl.BlockSpec((TOP_K, tt), lambda i: (0, i))
    return pl.pallas_call(
        functools.partial(_position_body, tt=tt),
        grid=(t // tt,),
        in_specs=[kt, kt, pl.BlockSpec(pstart.shape, lambda i: (0, 0))],
        out_specs=pl.BlockSpec((1, TOP_K, tt), lambda i: (i, 0, 0)),
        out_shape=jax.ShapeDtypeStruct((t // tt, TOP_K, tt), _I32),
        compiler_params=_cparams("arbitrary"),
        name="position",
    )(idx, rank, pstart)


def _ffn_body(blk0_ref, nblk_ref, ntot_ref, xs_hbm, wg_ref, wu_ref, wd_ref, ys_hbm,
              xbuf, ybuf, sem_in, sem_out, wgu_bf, wd_bf):
    e = pl.program_id(0)
    nblk = nblk_ref[e]
    blk0 = blk0_ref[e]
    ntot = ntot_ref[0]

    blk_rows = ROW_BLOCK * QUAD

    def rows(g):
        return pl.ds(pl.multiple_of(g * blk_rows, blk_rows), blk_rows)

    def in_start(g, slot):
        pltpu.make_async_copy(xs_hbm.at[rows(g)], xbuf.at[slot], sem_in.at[slot]).start()

    def in_wait(slot):
        pltpu.make_async_copy(xs_hbm.at[rows(0)], xbuf.at[slot], sem_in.at[slot]).wait()

    def out_start(g, slot):
        pltpu.make_async_copy(ybuf.at[slot], ys_hbm.at[rows(g)], sem_out.at[slot]).start()

    def out_wait(slot):
        pltpu.make_async_copy(ybuf.at[slot], ys_hbm.at[rows(0)], sem_out.at[slot]).wait()

    @pl.when(e == 0)
    def _():
        for i in range(IN_AHEAD):
            @pl.when(i < ntot)
            def _():
                in_start(i, i)

    @pl.when(nblk > 0)
    def _():
        wgu_bf[:, 0:EXPERT_FF] = wg_ref[0].astype(_BF16)
        wgu_bf[:, EXPERT_FF:] = wu_ref[0].astype(_BF16)
        wd_bf[...] = wd_ref[0].astype(_BF16)

        def acquire(g):
            slot = g & (RING - 1)
            in_wait(slot)

            @pl.when(g + IN_AHEAD < ntot)
            def _():
                in_start(g + IN_AHEAD, (g + IN_AHEAD) & (RING - 1))

            @pl.when(g >= RING)
            def _():
                out_wait(slot)

            return slot

        def compute(slot):
            lo, hi = _unpack_halves(_load_rows(xbuf.at[slot], ROW_BLOCK))
            a = jnp.dot(lo.astype(_BF16), wgu_bf[0:HALF, :], preferred_element_type=_F32)
            yield
            gu = a + jnp.dot(hi.astype(_BF16), wgu_bf[HALF:, :], preferred_element_type=_F32)
            yield
            h = (_silu(gu[:, :EXPERT_FF]) * gu[:, EXPERT_FF:]).astype(_BF16)
            y = jnp.dot(h, wd_bf[...], preferred_element_type=_F32)
            yield
            _store_rows(ybuf.at[slot], _pack_halves(y))

        def run(gs):
            slots = [acquire(g) for g in gs]
            live = [compute(s) for s in slots]
            while live:
                live = [c for c in live if next(c, live) is not live]
            for g, s in zip(gs, slots):
                out_start(g, s)

        def pair(j, carry):
            run([blk0 + 2 * j, blk0 + 2 * j + 1])
            return carry

        lax.fori_loop(0, nblk // 2, pair, 0)

        @pl.when((nblk & 1) == 1)
        def _():
            run([blk0 + nblk - 1])

    @pl.when(e == N_EXPERTS - 1)
    def _():
        for i in range(RING):
            @pl.when(i < ntot)
            def _():
                out_wait((ntot - 1 - i) & (RING - 1))


def _ffn(blk0, nblk, ntot, xs, wg, wu, wd):
    grid_spec = pltpu.PrefetchScalarGridSpec(
        num_scalar_prefetch=3,
        grid=(N_EXPERTS,),
        in_specs=[pl.BlockSpec(memory_space=pl.ANY),
                  pl.BlockSpec((1, D_MODEL, EXPERT_FF), lambda e, *_: (e, 0, 0)),
                  pl.BlockSpec((1, D_MODEL, EXPERT_FF), lambda e, *_: (e, 0, 0)),
                  pl.BlockSpec((1, EXPERT_FF, D_MODEL), lambda e, *_: (e, 0, 0))],
        out_specs=pl.BlockSpec(memory_space=pl.ANY),
        scratch_shapes=[pltpu.VMEM((RING, ROW_BLOCK * QUAD, 128), jnp.uint32),
                        pltpu.VMEM((RING, ROW_BLOCK * QUAD, 128), jnp.uint32),
                        pltpu.SemaphoreType.DMA((RING,)), pltpu.SemaphoreType.DMA((RING,)),
                        pltpu.VMEM((D_MODEL, 2 * EXPERT_FF), _BF16), pltpu.VMEM((EXPERT_FF, D_MODEL), _BF16)],
    )
    return pl.pallas_call(
        _ffn_body,
        grid_spec=grid_spec,
        out_shape=jax.ShapeDtypeStruct(xs.shape, jnp.uint32),
        compiler_params=_cparams("arbitrary"),
        name="ffn",
    )(blk0, nblk, ntot, xs, wg, wu, wd)


def _sc_position(idx, rank, pstart):
    n = idx.shape[0]
    per_w = n // SC_WORKERS
    assert per_w * SC_WORKERS == n and per_w % SC_LANES == 0
    mesh = plsc.VectorSubcoreMesh(core_axis_name="c", subcore_axis_name="s",
                                  num_cores=SC_CORES, num_subcores=SC_SUBCORES)

    @functools.partial(
        pl.kernel, mesh=mesh,
        out_type=jax.ShapeDtypeStruct((n,), _I32),
        scratch_types=[pltpu.VMEM((per_w,), _I32), pltpu.VMEM((per_w,), _I32), pltpu.VMEM((per_w,), _I32),
                       pltpu.VMEM((N_EXPERTS,), _I32)],
        compiler_params=pltpu.CompilerParams(needs_layout_passes=False),
        name="sc_position",
    )
    def position(idx_hbm, rank_hbm, ps_hbm, out_hbm, idx_v, rank_v, pos_v, ps_v):
        wid = lax.axis_index("s") * SC_CORES + lax.axis_index("c")
        mine = pl.ds(pl.multiple_of(wid * per_w, per_w), per_w)
        pltpu.sync_copy(ps_hbm, ps_v)
        pltpu.sync_copy(idx_hbm.at[mine], idx_v)
        pltpu.sync_copy(rank_hbm.at[mine], rank_v)

        @pl.loop(0, per_w, step=SC_LANES)
        def _(j):
            lanes = pl.ds(j, SC_LANES)
            pos_v[lanes] = plsc.load_gather(ps_v, [idx_v[lanes]]) + rank_v[lanes]

        pltpu.sync_copy(pos_v, out_hbm.at[mine])

    return position(idx, rank, pstart)


def _sc_gather(table, idx):
    b = idx.shape[0]
    nchunk = b // (SC_WORKERS * SC_CHUNK)
    assert nchunk * SC_WORKERS * SC_CHUNK == b and nchunk % SC_RING == 0
    idx2 = idx.reshape(SC_WORKERS * nchunk, SC_CHUNK)
    row = table.shape[1:]
    mesh = plsc.VectorSubcoreMesh(core_axis_name="c", subcore_axis_name="s",
                                  num_cores=SC_CORES, num_subcores=SC_SUBCORES)

    @functools.partial(
        pl.kernel, mesh=mesh,
        out_type=jax.ShapeDtypeStruct((b,) + row, table.dtype),
        scratch_types=[pltpu.VMEM((nchunk, SC_CHUNK), _I32), pltpu.VMEM((SC_RING, SC_CHUNK) + row, table.dtype),
                       pltpu.SemaphoreType.DMA((SC_RING,)), pltpu.SemaphoreType.DMA((SC_RING,))],
        name="sc_gather",
    )
    def gather(table_hbm, idx_hbm, out_hbm, idx_v, rows_v, sem_g, sem_w):
        wid = lax.axis_index("s") * SC_CORES + lax.axis_index("c")
        c0 = wid * nchunk
        pltpu.sync_copy(idx_hbm.at[pl.ds(pl.multiple_of(c0, nchunk), nchunk)], idx_v)

        def fetch(i, s):
            return pltpu.make_async_copy(table_hbm.at[idx_v.at[i]], rows_v.at[s], sem_g.at[s])

        def flush(i, s):
            rows = pl.ds(pl.multiple_of((c0 + i) * SC_CHUNK, SC_CHUNK), SC_CHUNK)
            return pltpu.make_async_copy(rows_v.at[s], out_hbm.at[rows], sem_w.at[s])

        for s in range(SC_RING):
            fetch(s, s).start()

        @pl.loop(0, nchunk, step=SC_RING)
        def _(g):
            for s in range(SC_RING):
                i = g + s
                fetch(i, s).wait()
                flush(i, s).start()
                flush(i, s).wait()

                @pl.when(i + SC_RING < nchunk)
                def _():
                    fetch(i + SC_RING, s).start()

    return gather(table, idx2)


def _sc_scatter(rows, pos3, n_out, row0):
    nchunk, nk, w = pos3.shape
    per_w = nchunk // SC_WORKERS
    assert per_w * SC_WORKERS == nchunk and w <= 128 and row0 % w == 0 and rows.shape[0] >= row0 + nchunk * w
    row = rows.shape[1:]
    mesh = plsc.VectorSubcoreMesh(core_axis_name="c", subcore_axis_name="s",
                                  num_cores=SC_CORES, num_subcores=SC_SUBCORES)

    @functools.partial(
        pl.kernel, mesh=mesh,
        out_type=jax.ShapeDtypeStruct((n_out,) + row, rows.dtype),
        scratch_types=[pltpu.VMEM((nk, w), _I32), pltpu.VMEM((w,) + row, rows.dtype), pltpu.SemaphoreType.DMA],
        name="sc_scatter",
    )
    def scatter(rows_hbm, pos_hbm, out_hbm, idx_v, rows_v, sem):
        wid = lax.axis_index("s") * SC_CORES + lax.axis_index("c")

        @pl.loop(0, per_w)
        def _(i):
            c = wid * per_w + i
            pltpu.sync_copy(pos_hbm.at[c], idx_v)
            pltpu.sync_copy(rows_hbm.at[pl.ds(pl.multiple_of(row0 + c * w, w), w)], rows_v)
            copies = [pltpu.async_copy(rows_v, out_hbm.at[idx_v.at[k]], sem) for k in range(nk)]
            for cp in copies:
                cp.wait()

    return scatter(rows, pos3)


def _padfill_body(cnt_ref, pst_ref, pcn_ref, xs_in, xs_out, zbuf, zsem):
    del xs_in
    zbuf[...] = jnp.zeros_like(zbuf)

    def pad_runs(e, act):
        pad = pcn_ref[e] - cnt_ref[e]
        base = pst_ref[e] + cnt_ref[e]
        for b in range(ROW_BLOCK.bit_length() - 1):
            n = 1 << b

            @pl.when(((pad >> b) & 1) == 1)
            def _():
                off = base + (pad & (n - 1))
                act(pltpu.make_async_copy(zbuf.at[pl.ds(0, QUAD * n)],
                                          xs_out.at[pl.ds(QUAD * off, QUAD * n)], zsem))

    def start_all(e, c):
        pad_runs(e, lambda d: d.start())
        return c

    def wait_all(e, c):
        pad_runs(e, lambda d: d.wait())
        return c

    lax.fori_loop(0, N_EXPERTS, start_all, 0)
    lax.fori_loop(0, N_EXPERTS, wait_all, 0)


def _padfill(counts, pstarts, pcounts, xs):
    grid_spec = pltpu.PrefetchScalarGridSpec(
        num_scalar_prefetch=3,
        grid=(1,),
        in_specs=[pl.BlockSpec(memory_space=pl.ANY)],
        out_specs=pl.BlockSpec(memory_space=pl.ANY),
        scratch_shapes=[pltpu.VMEM((QUAD * ROW_BLOCK // 2, 128), jnp.uint32), pltpu.SemaphoreType.DMA],
    )
    return pl.pallas_call(
        _padfill_body,
        grid_spec=grid_spec,
        out_shape=jax.ShapeDtypeStruct(xs.shape, xs.dtype),
        input_output_aliases={3: 0},
        compiler_params=_cparams("arbitrary"),
        name="padfill",
    )(counts, pstarts, pcounts, xs)


def _combine_stream_body(gate_ref, h1_ref, yg_ref, wsg_ref, wsu_ref, wsd_ref, g_ref, b_ref, out_ref, *, tt):
    x = h1_ref[...]
    xb = x.astype(_BF16)
    shared = _mm(_silu(_mm(xb, wsg_ref[...])) * _mm(xb, wsu_ref[...]), wsd_ref[...])
    gcol = gate_ref[...].T
    acc_lo = jnp.zeros((tt, HALF), _F32)
    acc_hi = jnp.zeros((tt, HALF), _F32)
    for k in range(TOP_K):
        lo, hi = _unpack_halves(_load_rows(yg_ref.at[0, k], tt))
        acc_lo = acc_lo + gcol[:, k:k + 1] * lo
        acc_hi = acc_hi + gcol[:, k:k + 1] * hi
    routed = jnp.concatenate([acc_lo, acc_hi], axis=1)
    out_ref[...] = _layer_norm(DN_ALPHA * x + (routed + shared), g_ref[...], b_ref[...])


def _combine_stream(gate, h1, yg, wsg, wsu, wsd, g, b, out_prev, *, tt, tile0):
    t_all = h1.shape[0]
    t = gate.shape[1]
    full = lambda a: pl.BlockSpec(a.shape, lambda i: (0,) * a.ndim)
    in_specs = [pl.BlockSpec((TOP_K, tt), lambda i: (0, i)), pl.BlockSpec((tt, D_MODEL), lambda i: (i + tile0, 0)),
                pl.BlockSpec((1, TOP_K, tt * QUAD, 128), lambda i: (i, 0, 0, 0)),
                full(wsg), full(wsu), full(wsd), full(g), full(b)]
    args = [gate, h1, yg, wsg, wsu, wsd, g, b]
    aliases = {}
    body = functools.partial(_combine_stream_body, tt=tt)
    if out_prev is not None:
        in_specs.append(pl.BlockSpec(memory_space=pl.ANY))
        args.append(out_prev)
        aliases = {len(args) - 1: 0}
        body = lambda *refs: _combine_stream_body(*refs[:8], refs[9], tt=tt)
    return pl.pallas_call(
        body,
        grid=(t // tt,),
        in_specs=in_specs,
        out_specs=pl.BlockSpec((tt, D_MODEL), lambda i: (i + tile0, 0)),
        out_shape=jax.ShapeDtypeStruct((t_all, D_MODEL), _F32),
        input_output_aliases=aliases,
        compiler_params=_cparams("arbitrary"),
        name="combine",
    )(*args)


def _pick(n, pref):
    t = min(n, pref)
    while n % t:
        t -= CHUNK
    return t


def _mixer(x, tails, s0, wts, gnw, *, lt, lg, nbb):
    yc, q, k, v, z, bgc, bgr, tails_out = _premix(x, tails, wts, lt=lt)
    bsz, seq, _ = x.shape
    nch = seq // CHUNK
    grow = bgr[:, GDN_HEADS:2 * GDN_HEADS, :].reshape(bsz, GDN_HEADS, nch, CHUNK)
    grow = grow.transpose(0, 2, 1, 3).reshape(bsz, nch, 1, STACK)
    yg, s_out = _gdn(q, k, v, z, bgc, grow, s0, gnw, lg=lg, nbb=nbb)
    return yc, yg, tails_out, s_out


def kernel(x, meta_tokens, w_in, conv_w, conv_norm_w, gdn_conv_w, a_log, dt_bias, gdn_norm_w, w_out,
           ln1_g, ln1_b, w_router, b_router, w_gate, w_up, w_down, ws_gate, ws_up, ws_down, ln2_g, ln2_b):
    assert w_in.shape[0] == 1, "single-layer stack"
    bsz, seq, d = x.shape
    assert d == D_MODEL and seq % CHUNK == 0
    c, gw = CONV_WIDTH, GDN_WIDTH
    win = w_in[0].astype(_BF16)
    wbd = win[:, 3 * c + 4 * gw:]
    zpad = jnp.zeros((128 - 2 * GDN_HEADS,), _F32)
    zpad4 = jnp.zeros((GDN_HEADS,), _F32)
    prow = jnp.zeros((8, 128), _F32)
    prow = prow.at[0].set(jnp.concatenate([zpad4, a_log[0], zpad]))
    prow = prow.at[1].set(jnp.concatenate([zpad4, dt_bias[0], zpad]))
    wts = (win[:, :3 * c], win[:, 3 * c:3 * c + 3 * gw], win[:, 3 * c + 3 * gw:3 * c + 4 * gw],
           jnp.pad(wbd, ((0, 0), (0, 128 - 2 * GDN_HEADS))), wbd.T,
           conv_w[0], conv_norm_w, gdn_conv_w[0], prow, prow.T[:8])
    gnw = gdn_norm_w

    meta = jnp.concatenate([jnp.zeros((CHUNK - N_META, d), x.dtype), meta_tokens.astype(x.dtype)])[None]
    tails0 = jnp.zeros((HIST, c + 3 * gw), _F32)
    s00 = jnp.zeros((GDN_HEADS, GDN_HEAD_DIM, GDN_HEAD_DIM), _F32)
    _, _, tails_m, s_m = _mixer(meta, tails0, s00, wts, gnw, lt=CHUNK, lg=CHUNK, nbb=1)

    yc, yg, _, _ = _mixer(x, tails_m[0], s_m[0], wts, gnw, lt=_pick(seq, 512), lg=_pick(seq, 512),
                          nbb=GDN_ROWS if bsz % GDN_ROWS == 0 else 1)

    t = bsz * seq
    tm = _pick(t, 512)
    h1, h1p = _outproj(yc.reshape(t, c), yg.reshape(t, gw), x.reshape(t, d), w_out[0].astype(_BF16),
                       ln1_g, ln1_b, tm=tm)

    tt = _pick(t, 256)
    wr_t = w_router[0].T
    wr_hi = wr_t.astype(_BF16)
    wr_lo = (wr_t - wr_hi.astype(_F32)).astype(_BF16)
    shared_w = (ws_gate[0].astype(_BF16), ws_up[0].astype(_BF16), ws_down[0].astype(_BF16))
    h1p3 = h1p.reshape(t, QUAD, 128)

    parts = MOE_PARTS if t % (MOE_PARTS * tt * SC_WORKERS) == 0 else 1
    tp = t // parts
    nb = tp * TOP_K // ROW_BLOCK + N_EXPERTS
    out = None
    for part in range(parts):
        tile0 = part * (tp // tt)
        idx, gate, rank, cnt = _router(h1, wr_hi, wr_lo, b_router[0][:, None], tt=tt, tile0=tile0, t=tp)
        counts = cnt[:, 0].astype(_I32)
        pcounts = (counts + ROW_BLOCK - 1) // ROW_BLOCK * ROW_BLOCK
        pends = jnp.cumsum(pcounts)
        pstarts = pends - pcounts
        tiled = lambda a: a.reshape(TOP_K, tp // tt, tt).transpose(1, 0, 2).reshape(tp * TOP_K)
        pos = _sc_position(tiled(idx), tiled(rank), pstarts.astype(_I32))
        pos = pos.reshape(tp // tt, TOP_K, tt)
        nwin = tt // SC_WINDOW
        pos3 = pos.reshape(tp // tt, TOP_K, nwin, SC_WINDOW).transpose(0, 2, 1, 3)
        pos3 = pos3.reshape(tp // SC_WINDOW, TOP_K, SC_WINDOW)
        xs = _sc_scatter(h1p3, pos3, nb * ROW_BLOCK, part * tp)
        xs = _padfill(counts, pstarts.astype(_I32), pcounts.astype(_I32), xs.reshape(nb * ROW_BLOCK * QUAD, 128))
        ys = _ffn((pstarts // ROW_BLOCK).astype(_I32), (pcounts // ROW_BLOCK).astype(_I32),
                  (pends[-1:] // ROW_BLOCK).astype(_I32), xs, w_gate[0], w_up[0], w_down[0])
        yg = _sc_gather(ys.reshape(nb * ROW_BLOCK, QUAD, 128), pos.reshape(tp * TOP_K))
        yg = yg.reshape(tp // tt, TOP_K, tt * QUAD, 128)
        out = _combine_stream(gate, h1, yg, *shared_w, ln2_g, ln2_b, out, tt=tt, tile0=tile0)
    return out.reshape(bsz, seq, d)
```

```python
import functools

import jax
import jax.numpy as jnp
from jax import lax
from jax.experimental import pallas as pl
from jax.experimental.pallas import tpu as pltpu
from jax.experimental.pallas import tpu_sc as plsc

_F32 = jnp.float32
_BF16 = jnp.bfloat16
_I32 = jnp.int32

D_MODEL = 1024
N_META = 16
CONV_WIDTH = 512
CONV_K = 3
GDN_HEADS = 4
GDN_HEAD_DIM = 128
GDN_WIDTH = GDN_HEADS * GDN_HEAD_DIM
GDN_CONV_K = 4
CHUNK = 64
N_EXPERTS = 256
TOP_K = 8
N_GROUPS = 8
TOPK_GROUPS = 4
E_PER_GROUP = N_EXPERTS // N_GROUPS
EXPERT_FF = 256
ROUTED_SCALE = 2.5
ROW_BLOCK = 256
DN_ALPHA = 2.0 ** 0.25
NORM_EPS = 1e-5
HALF = D_MODEL // 2
QUAD = HALF // 128
STACK = GDN_HEADS * CHUNK
HIST = 8
GDN_ROWS = 4
PREMIX_SUB = 2
SC_CORES = 2
SC_SUBCORES = 16
SC_WORKERS = SC_CORES * SC_SUBCORES
SC_LANES = 16
SC_CHUNK = 64
SC_RING = 2
SC_WINDOW = 128
MOE_PARTS = 2
RING = 8
IN_AHEAD = RING - 2

V7X_VMEM_BYTES = 64 * 1024 * 1024
VMEM_LIMIT = V7X_VMEM_BYTES - 8 * 1024 * 1024


def _cparams(*sem):
    return pltpu.CompilerParams(dimension_semantics=sem, vmem_limit_bytes=VMEM_LIMIT)


def _mm(a, b):
    return jnp.dot(a.astype(_BF16), b.astype(_BF16), preferred_element_type=_F32)


def _mm_nt(a, b):
    return lax.dot_general(a.astype(_BF16), b.astype(_BF16), (((1,), (1,)), ((), ())),
                           preferred_element_type=_F32)


def _mm_tn(a, b):
    return lax.dot_general(a.astype(_BF16), b.astype(_BF16), (((0,), (0,)), ((), ())),
                           preferred_element_type=_F32)


def _sigmoid(x):
    return 1.0 / (1.0 + jnp.exp(-x))


def _silu(x):
    return x * _sigmoid(x)


def _softplus(x):
    return jnp.maximum(x, 0.0) + jnp.log1p(jnp.exp(-jnp.abs(x)))


def _pack_halves(y):
    return pltpu.pack_elementwise([y[:, :HALF], y[:, HALF:]], packed_dtype=_BF16)


def _store_rows(ref, packed):
    r = packed.shape[0]
    for c in range(QUAD):
        ref[pl.ds(c, r, stride=QUAD), :] = packed[:, c * 128:(c + 1) * 128]


def _load_rows(ref, r):
    return jnp.concatenate([ref[pl.ds(c, r, stride=QUAD), :] for c in range(QUAD)], axis=1)


def _unpack_halves(p):
    lo = pltpu.unpack_elementwise(p, index=0, packed_dtype=_BF16, unpacked_dtype=_F32)
    hi = pltpu.unpack_elementwise(p, index=1, packed_dtype=_BF16, unpacked_dtype=_F32)
    return lo, hi


def _layer_norm(h, g, b):
    mu = jnp.mean(h, axis=-1, keepdims=True)
    d = h - mu
    var = jnp.mean(d * d, axis=-1, keepdims=True)
    return d * lax.rsqrt(var + NORM_EPS) * g + b


def _premix_body(x_ref, tails_ref, wa_ref, wq_ref, wz_ref, wbd_ref, wbdt_ref, cw_ref, cnw_ref,
                 gcw_ref, prow_ref, pcol_ref,
                 yc_ref, q_ref, k_ref, v_ref, z_ref, bgc_ref, bgr_ref, tout_ref, ext_ref, *, lt):
    cw_ = CONV_WIDTH

    @pl.when(pl.program_id(1) == 0)
    def _():
        ext_ref[0:HIST, :] = tails_ref[...]

    cw = cw_ref[...]
    gcw = gcw_ref[...]
    prow = prow_ref[...]
    pcol = pcol_ref[...]

    def sub_tile(r0, n):
        rows = slice(r0, r0 + n)
        erows = slice(HIST + r0, HIST + r0 + n)
        xb = x_ref[0, rows, :].astype(_BF16)
        pa = jnp.dot(xb, wa_ref[...], preferred_element_type=_F32)
        yield
        gate_b = pa[:, 0:cw_]
        u = pa[:, cw_:2 * cw_] * pa[:, 2 * cw_:3 * cw_]
        ext_ref[erows, 0:cw_] = u
        pq = jnp.dot(xb, wq_ref[...], preferred_element_type=_F32)
        yield
        ext_ref[erows, cw_:] = pq
        zz = jnp.dot(xb, wz_ref[...], preferred_element_type=_F32)
        bdc = jnp.dot(xb, wbd_ref[...], preferred_element_type=_F32)
        bdr = _mm_nt(wbdt_ref[...], xb)
        yield

        ca = u * cw[CONV_K - 1:CONV_K, :]
        for j in range(CONV_K - 1):
            ca = ca + ext_ref[pl.ds(HIST + r0 - (CONV_K - 1) + j, n), 0:cw_] * cw[j:j + 1, :]
        yc = gate_b * ca
        ms = jnp.mean(yc * yc, axis=-1, keepdims=True)
        yc_ref[0, rows, :] = (yc * lax.rsqrt(ms + NORM_EPS) * cnw_ref[...]).astype(_BF16)

        cq = pq * gcw[GDN_CONV_K - 1:GDN_CONV_K, :]
        for j in range(GDN_CONV_K - 1):
            cq = cq + ext_ref[pl.ds(HIST + r0 - (GDN_CONV_K - 1) + j, n), cw_:] * gcw[j:j + 1, :]
        s = _silu(cq)
        for h in range(GDN_HEADS):
            lo, hi = h * GDN_HEAD_DIM, (h + 1) * GDN_HEAD_DIM
            qh = s[:, lo:hi]
            kh = s[:, GDN_WIDTH + lo:GDN_WIDTH + hi]
            qn = qh * lax.rsqrt(jnp.sum(qh * qh, axis=-1, keepdims=True) + 1e-6)
            kn = kh * lax.rsqrt(jnp.sum(kh * kh, axis=-1, keepdims=True) + 1e-6)
            q_ref[0, rows, lo:hi] = (qn * (GDN_HEAD_DIM ** -0.5)).astype(_BF16)
            k_ref[0, rows, lo:hi] = kn.astype(_BF16)
        v_ref[0, rows, :] = s[:, 2 * GDN_WIDTH:].astype(_BF16)
        z_ref[0, rows, :] = zz.astype(_BF16)

        g_c = -jnp.exp(prow[0:1, :]) * _softplus(bdc + prow[1:2, :])
        lane = lax.broadcasted_iota(_I32, bdc.shape, 1)
        bgc_ref[0, rows, :] = jnp.where(lane < GDN_HEADS, _sigmoid(bdc), g_c)
        g_r = -jnp.exp(pcol[:, 0:1]) * _softplus(bdr + pcol[:, 1:2])
        row = lax.broadcasted_iota(_I32, bdr.shape, 0)
        bgr_ref[0, :, rows] = jnp.where(row < GDN_HEADS, _sigmoid(bdr), g_r)

    n_sub = PREMIX_SUB if lt % (PREMIX_SUB * 128) == 0 else 1
    live = [sub_tile(i * (lt // n_sub), lt // n_sub) for i in range(n_sub)]
    while live:
        live = [g for g in live if next(g, live) is not live]

    tail = ext_ref[lt:lt + HIST, :]
    ext_ref[0:HIST, :] = tail
    tout_ref[0] = tail


def _premix(x, tails, wts, *, lt):
    bsz, seq, d = x.shape
    assert seq % lt == 0
    grid = (bsz, seq // lt)
    full = lambda a: pl.BlockSpec(a.shape, lambda b, j: (0,) * a.ndim)
    tok = lambda w: pl.BlockSpec((1, lt, w), lambda b, j: (b, j, 0))
    (wa, wq, wz, wbd, wbdt, cw, cnw, gcw, prow, pcol) = wts
    ext_w = CONV_WIDTH + 3 * GDN_WIDTH
    out_shape = (
        jax.ShapeDtypeStruct((bsz, seq, CONV_WIDTH), _BF16),
        jax.ShapeDtypeStruct((bsz, seq, GDN_WIDTH), _BF16),
        jax.ShapeDtypeStruct((bsz, seq, GDN_WIDTH), _BF16),
        jax.ShapeDtypeStruct((bsz, seq, GDN_WIDTH), _BF16),
        jax.ShapeDtypeStruct((bsz, seq, GDN_WIDTH), _BF16),
        jax.ShapeDtypeStruct((bsz, seq, 128), _F32),
        jax.ShapeDtypeStruct((bsz, 8, seq), _F32),
        jax.ShapeDtypeStruct((bsz, HIST, ext_w), _F32),
    )
    out_specs = (tok(CONV_WIDTH), tok(GDN_WIDTH), tok(GDN_WIDTH), tok(GDN_WIDTH), tok(GDN_WIDTH),
                 tok(128), pl.BlockSpec((1, 8, lt), lambda b, j: (b, 0, j)),
                 pl.BlockSpec((1, HIST, ext_w), lambda b, j: (b, 0, 0)))
    return pl.pallas_call(
        functools.partial(_premix_body, lt=lt),
        grid=grid,
        in_specs=[tok(d), full(tails)] + [full(w) for w in wts],
        out_specs=out_specs,
        out_shape=out_shape,
        scratch_shapes=[pltpu.VMEM((HIST + lt, ext_w), _F32)],
        compiler_params=_cparams("arbitrary", "arbitrary"),
        name="premix",
    )(x, tails, *wts)


def _cumsum_rows(x):
    row = lax.broadcasted_iota(_I32, x.shape, 0)
    s = 1
    while s < x.shape[0]:
        x = x + jnp.where(row >= s, pltpu.roll(x, s, 0), 0.0)
        s *= 2
    return x


def _cumsum_lanes_seg(x):
    lane = lax.broadcasted_iota(_I32, x.shape, 1) & (CHUNK - 1)
    s = 1
    while s < CHUNK:
        x = x + jnp.where(lane >= s, pltpu.roll(x, s, 1), 0.0)
        s *= 2
    return x


def _stack_heads(a):
    return jnp.concatenate([a[:, h * GDN_HEAD_DIM:(h + 1) * GDN_HEAD_DIM] for h in range(GDN_HEADS)], axis=0)


def _gdn_body(q_ref, k_ref, v_ref, z_ref, bgc_ref, grow_ref, s0_ref, gnw_ref,
              y_ref, sout_ref, s_ref, *, nc, nbb):
    @pl.when(pl.program_id(1) == 0)
    def _():
        for r in range(nbb):
            s_ref[r] = s0_ref[...]

    ri = lax.broadcasted_iota(_I32, (STACK, STACK), 0)
    ci = lax.broadcasted_iota(_I32, (STACK, STACK), 1)
    same64 = (ri >> 6) == (ci >> 6)
    same32 = (ri >> 5) == (ci >> 5)
    same16 = (ri >> 4) == (ci >> 4)
    low_incl = same64 & (ri >= ci)
    low_strict = same64 & (ri > ci)
    gnw = gnw_ref[...]

    def chunk_row(r, c):
        off = pl.multiple_of(c * CHUNK, CHUNK)
        q_all = _stack_heads(q_ref[r, pl.ds(off, CHUNK), :].astype(_F32))
        k_all = _stack_heads(k_ref[r, pl.ds(off, CHUNK), :].astype(_F32))
        v_all = _stack_heads(v_ref[r, pl.ds(off, CHUNK), :].astype(_F32))
        bgc = bgc_ref[r, pl.ds(off, CHUNK), :]
        gcs = _cumsum_rows(bgc)
        hd = (CHUNK, GDN_HEAD_DIM)
        beta_b = jnp.concatenate(
            [jnp.broadcast_to(bgc[:, h:h + 1], hd) for h in range(GDN_HEADS)], axis=0)
        gc_b = jnp.concatenate(
            [jnp.broadcast_to(gcs[:, GDN_HEADS + h:GDN_HEADS + h + 1], hd) for h in range(GDN_HEADS)], axis=0)
        gl = [gcs[CHUNK - 1:CHUNK, GDN_HEADS + h:GDN_HEADS + h + 1] for h in range(GDN_HEADS)]
        gl_b = jnp.concatenate([jnp.broadcast_to(g1, hd) for g1 in gl], axis=0)
        gcr = _cumsum_lanes_seg(jnp.broadcast_to(grow_ref[r, c], (8, STACK)))[0:1, :]

        diff = jnp.concatenate([gc_b, gc_b], axis=1) - gcr
        decay = jnp.exp(jnp.where(low_incl, diff, -1e30))
        kb = k_all * beta_b
        a1 = _mm_nt(jnp.concatenate([kb, q_all], axis=0), k_all)
        yield
        m = jnp.where(low_strict, a1[:STACK] * decay, 0.0)
        attn = a1[STACK:] * decay

        l16 = jnp.where(same16, m, 0.0)
        c1 = jnp.where(same32 & jnp.logical_not(same16), m, 0.0)
        c2 = jnp.where(same32, 0.0, m)
        p2 = _mm(l16, l16)
        yield
        p4 = _mm(p2, p2)
        t = _mm(l16, p2)
        yield
        na = p2 - l16 - t
        p8 = _mm(p4, p4)
        t = _mm(na, p4)
        yield
        nb = na + p4 + t
        t = _mm(nb, p8)
        yield
        ncm = nb + p8 + t
        t = _mm(c1, ncm)
        yield
        y1 = c1 + t
        t = _mm(ncm, y1)
        yield
        n1 = ncm - y1 - t
        t = _mm(c2, n1)
        yield
        y2 = c2 + t
        t = _mm(n1, y2)
        yield
        nt = n1 - y2 - t

        egc = jnp.exp(gc_b)
        rhs = jnp.concatenate([v_all * beta_b, kb * egc], axis=1)
        t = _mm(nt, rhs)
        yield
        uw = rhs + t
        u_all = uw[:, :GDN_HEAD_DIM]
        w_all = uw[:, GDN_HEAD_DIM:]
        qd = q_all * egc
        kd = k_all * jnp.exp(gl_b - gc_b)

        bs = []
        for h in range(GDN_HEADS):
            r0, r1 = h * CHUNK, (h + 1) * CHUNK
            bs.append(_mm(jnp.concatenate([w_all[r0:r1], qd[r0:r1]], axis=0), s_ref[r, h]))
        yield
        vn = [u_all[h * CHUNK:(h + 1) * CHUNK] - bs[h][:CHUNK] for h in range(GDN_HEADS)]
        vn_all = jnp.concatenate(vn, axis=0)
        t = _mm(attn, vn_all)
        ds = [_mm_tn(kd[h * CHUNK:(h + 1) * CHUNK], vn[h]) for h in range(GDN_HEADS)]
        yield
        o_all = jnp.concatenate([b[CHUNK:] for b in bs], axis=0) + t
        for h in range(GDN_HEADS):
            r0, r1 = h * CHUNK, (h + 1) * CHUNK
            s_ref[r, h] = s_ref[r, h] * jnp.exp(gl[h]) + ds[h]
            o = o_all[r0:r1]
            zz = z_ref[r, pl.ds(off, CHUNK), h * GDN_HEAD_DIM:(h + 1) * GDN_HEAD_DIM].astype(_F32)
            on = o * lax.rsqrt(jnp.mean(o * o, axis=-1, keepdims=True) + NORM_EPS) * gnw
            y_ref[r, pl.ds(off, CHUNK), h * GDN_HEAD_DIM:(h + 1) * GDN_HEAD_DIM] = (on * _silu(zz)).astype(_BF16)

    def chunk(c, carry):
        live = [chunk_row(r, c) for r in range(nbb)]
        while live:
            live = [g for g in live if next(g, live) is not live]
        return carry

    lax.fori_loop(0, nc, chunk, 0)
    sout_ref[...] = s_ref[...]


def _gdn(q, k, v, z, bgc, grow, s0, gnw, *, lg, nbb):
    bsz, seq, _ = q.shape
    assert seq % lg == 0 and lg % CHUNK == 0 and bsz % nbb == 0
    nc = lg // CHUNK
    tok = lambda w: pl.BlockSpec((nbb, lg, w), lambda b, j: (b, j, 0))
    full = lambda a: pl.BlockSpec(a.shape, lambda b, j: (0,) * a.ndim)
    st = (nbb, GDN_HEADS, GDN_HEAD_DIM, GDN_HEAD_DIM)
    return pl.pallas_call(
        functools.partial(_gdn_body, nc=nc, nbb=nbb),
        grid=(bsz // nbb, seq // lg),
        in_specs=[tok(GDN_WIDTH)] * 4 + [tok(128), pl.BlockSpec((nbb, nc, 1, STACK), lambda b, j: (b, j, 0, 0)),
                                           full(s0), full(gnw)],
        out_specs=(tok(GDN_WIDTH), pl.BlockSpec(st, lambda b, j: (b, 0, 0, 0))),
        out_shape=(jax.ShapeDtypeStruct((bsz, seq, GDN_WIDTH), _BF16),
                   jax.ShapeDtypeStruct((bsz, GDN_HEADS, GDN_HEAD_DIM, GDN_HEAD_DIM), _F32)),
        scratch_shapes=[pltpu.VMEM(st, _F32)],
        compiler_params=_cparams("arbitrary", "arbitrary"),
        name="gdn",
    )(q, k, v, z, bgc, grow, s0, gnw)


def _outproj_body(yc_ref, yg_ref, x_ref, wo_ref, g_ref, b_ref, h1_ref, h1p_ref):
    mix = (jnp.dot(yc_ref[...], wo_ref[0:CONV_WIDTH, :], preferred_element_type=_F32)
           + jnp.dot(yg_ref[...], wo_ref[CONV_WIDTH:, :], preferred_element_type=_F32))
    h1 = _layer_norm(DN_ALPHA * x_ref[...] + mix, g_ref[...], b_ref[...])
    h1_ref[...] = h1
    _store_rows(h1p_ref, _pack_halves(h1))


def _outproj(yc, yg, x2d, wo, g, b, *, tm):
    t = x2d.shape[0]
    assert t % tm == 0
    row = lambda w: pl.BlockSpec((tm, w), lambda i: (i, 0))
    full = lambda a: pl.BlockSpec(a.shape, lambda i: (0,) * a.ndim)
    return pl.pallas_call(
        _outproj_body,
        grid=(t // tm,),
        in_specs=[row(CONV_WIDTH), row(GDN_WIDTH), row(D_MODEL), full(wo), full(g), full(b)],
        out_specs=(row(D_MODEL), pl.BlockSpec((tm * QUAD, 128), lambda i: (i, 0))),
        out_shape=(jax.ShapeDtypeStruct((t, D_MODEL), _F32), jax.ShapeDtypeStruct((t * QUAD, 128), jnp.uint32)),
        compiler_params=_cparams("arbitrary"),
        name="outproj",
    )(yc, yg, x2d, wo, g, b)


def _router_body(h1_ref, wh_ref, wl_ref, br_ref, idx_ref, gate_ref, rank_ref, cnt_ref, carry_ref, *, tt):
    @pl.when(pl.program_id(0) == 0)
    def _():
        carry_ref[...] = jnp.zeros_like(carry_ref)

    x = h1_ref[...]
    xh = x.astype(_BF16)
    xl = (x - xh.astype(_F32)).astype(_BF16)
    wh = wh_ref[...]
    logits = _mm_nt(wh, xh) + _mm_nt(wh, xl) + _mm_nt(wl_ref[...], xh)
    scores = _sigmoid(logits)
    sel = scores + br_ref[...]
    ninf = -jnp.inf

    r32 = lax.broadcasted_iota(_I32, (E_PER_GROUP, tt), 0)
    gsc = []
    for g in range(N_GROUPS):
        xg = sel[g * E_PER_GROUP:(g + 1) * E_PER_GROUP]
        m1 = jnp.max(xg, axis=0, keepdims=True)
        i1 = jnp.min(jnp.where(xg == m1, r32, E_PER_GROUP), axis=0, keepdims=True)
        m2 = jnp.max(jnp.where(r32 == i1, ninf, xg), axis=0, keepdims=True)
        gsc.append(m1 + m2)
    work = jnp.concatenate(gsc, axis=0)
    r8 = lax.broadcasted_iota(_I32, (N_GROUPS, tt), 0)
    gkeep = jnp.zeros((N_GROUPS, tt), _F32)
    for _ in range(TOPK_GROUPS):
        m = jnp.max(work, axis=0, keepdims=True)
        gi = jnp.min(jnp.where(work == m, r8, N_GROUPS), axis=0, keepdims=True)
        pick = r8 == gi
        gkeep = jnp.where(pick, 1.0, gkeep)
        work = jnp.where(pick, ninf, work)
    selm = jnp.concatenate(
        [jnp.where(gkeep[g:g + 1] > 0.5, sel[g * E_PER_GROUP:(g + 1) * E_PER_GROUP], ninf)
         for g in range(N_GROUPS)], axis=0)

    re = lax.broadcasted_iota(_I32, (N_EXPERTS, tt), 0)
    msel = jnp.zeros((N_EXPERTS, tt), _F32)
    idxs, gates = [], []
    for _ in range(TOP_K):
        m = jnp.max(selm, axis=0, keepdims=True)
        ii = jnp.min(jnp.where(selm == m, re, N_EXPERTS), axis=0, keepdims=True)
        hit = re == ii
        idxs.append(ii)
        gates.append(jnp.sum(jnp.where(hit, scores, 0.0), axis=0, keepdims=True))
        selm = jnp.where(hit, ninf, selm)
        msel = jnp.where(hit, 1.0, msel)
    gate = jnp.concatenate(gates, axis=0)
    gate_ref[...] = gate / jnp.sum(gate, axis=0, keepdims=True) * ROUTED_SCALE
    idx_ref[...] = jnp.concatenate(idxs, axis=0)

    ta = lax.broadcasted_iota(_I32, (tt, tt), 0)
    tb = lax.broadcasted_iota(_I32, (tt, tt), 1)
    earlier = jnp.where(ta < tb, 1.0, 0.0)
    carry = carry_ref[...]
    rank_all = _mm(msel, earlier) + carry[:, 0:1]
    rank_ref[...] = jnp.concatenate(
        [jnp.sum(jnp.where(re == ii, rank_all, 0.0), axis=0, keepdims=True) for ii in idxs],
        axis=0).astype(_I32)
    carry = carry + jnp.sum(msel, axis=1, keepdims=True)
    carry_ref[...] = carry
    cnt_ref[...] = carry


def _router(h1, wh, wl, br, *, tt, tile0, t):
    assert t % tt == 0
    full = lambda a: pl.BlockSpec(a.shape, lambda i: (0,) * a.ndim)
    kt = pl.BlockSpec((TOP_K, tt), lambda i: (0, i))
    return pl.pallas_call(
        functools.partial(_router_body, tt=tt),
        grid=(t // tt,),
        in_specs=[pl.BlockSpec((tt, D_MODEL), lambda i: (i + tile0, 0)), full(wh), full(wl), full(br)],
        out_specs=(kt, kt, kt, pl.BlockSpec((N_EXPERTS, 128), lambda i: (0, 0))),
        out_shape=(jax.ShapeDtypeStruct((TOP_K, t), _I32), jax.ShapeDtypeStruct((TOP_K, t), _F32),
                   jax.ShapeDtypeStruct((TOP_K, t), _I32), jax.ShapeDtypeStruct((N_EXPERTS, 128), _F32)),
        scratch_shapes=[pltpu.VMEM((N_EXPERTS, 128), _F32)],
        compiler_params=_cparams("arbitrary"),
        name="router",
    )(h1, wh, wl, br)


def _position_body(idx_ref, rank_ref, pstart_ref, pos_ref, *, tt):
    re = lax.broadcasted_iota(_I32, (N_EXPERTS, tt), 0)
    ps = pstart_ref[...]
    idx = idx_ref[...]
    rows = [jnp.sum(jnp.where(re == idx[k:k + 1], ps, 0), axis=0, keepdims=True) for k in range(TOP_K)]
    pos_ref[0] = jnp.concatenate(rows, axis=0) + rank_ref[...]


def _position(idx, rank, pstart, *, tt):
    t = idx.shape[1]
    kt = pl.BlockSpec((TOP_K, tt), lambda i: (0, i))
    return pl.pallas_call(
        functools.partial(_position_body, tt=tt),
        grid=(t // tt,),
        in_specs=[kt, kt, pl.BlockSpec(pstart.shape, lambda i: (0, 0))],
        out_specs=pl.BlockSpec((1, TOP_K, tt), lambda i: (i, 0, 0)),
        out_shape=jax.ShapeDtypeStruct((t // tt, TOP_K, tt), _I32),
        compiler_params=_cparams("arbitrary"),
        name="position",
    )(idx, rank, pstart)


def _ffn_body(blk0_ref, nblk_ref, ntot_ref, xs_hbm, wg_ref, wu_ref, wd_ref, ys_hbm,
              xbuf, ybuf, sem_in, sem_out, wgu_bf, wd_bf):
    e = pl.program_id(0)
    nblk = nblk_ref[e]
    blk0 = blk0_ref[e]
    ntot = ntot_ref[0]

    blk_rows = ROW_BLOCK * QUAD

    def rows(g):
        return pl.ds(pl.multiple_of(g * blk_rows, blk_rows), blk_rows)

    def in_start(g, slot):
        pltpu.make_async_copy(xs_hbm.at[rows(g)], xbuf.at[slot], sem_in.at[slot]).start()

    def in_wait(slot):
        pltpu.make_async_copy(xs_hbm.at[rows(0)], xbuf.at[slot], sem_in.at[slot]).wait()

    def out_start(g, slot):
        pltpu.make_async_copy(ybuf.at[slot], ys_hbm.at[rows(g)], sem_out.at[slot]).start()

    def out_wait(slot):
        pltpu.make_async_copy(ybuf.at[slot], ys_hbm.at[rows(0)], sem_out.at[slot]).wait()

    @pl.when(e == 0)
    def _():
        for i in range(IN_AHEAD):
            @pl.when(i < ntot)
            def _():
                in_start(i, i)

    @pl.when(nblk > 0)
    def _():
        wgu_bf[:, 0:EXPERT_FF] = wg_ref[0].astype(_BF16)
        wgu_bf[:, EXPERT_FF:] = wu_ref[0].astype(_BF16)
        wd_bf[...] = wd_ref[0].astype(_BF16)

        def acquire(g):
            slot = g & (RING - 1)
            in_wait(slot)

            @pl.when(g + IN_AHEAD < ntot)
            def _():
                in_start(g + IN_AHEAD, (g + IN_AHEAD) & (RING - 1))

            @pl.when(g >= RING)
            def _():
                out_wait(slot)

            return slot

        def compute(slot):
            lo, hi = _unpack_halves(_load_rows(xbuf.at[slot], ROW_BLOCK))
            a = jnp.dot(lo.astype(_BF16), wgu_bf[0:HALF, :], preferred_element_type=_F32)
            yield
            gu = a + jnp.dot(hi.astype(_BF16), wgu_bf[HALF:, :], preferred_element_type=_F32)
            yield
            h = (_silu(gu[:, :EXPERT_FF]) * gu[:, EXPERT_FF:]).astype(_BF16)
            y = jnp.dot(h, wd_bf[...], preferred_element_type=_F32)
            yield
            _store_rows(ybuf.at[slot], _pack_halves(y))

        def run(gs):
            slots = [acquire(g) for g in gs]
            live = [compute(s) for s in slots]
            while live:
                live = [c for c in live if next(c, live) is not live]
            for g, s in zip(gs, slots):
                out_start(g, s)

        def pair(j, carry):
            run([blk0 + 2 * j, blk0 + 2 * j + 1])
            return carry

        lax.fori_loop(0, nblk // 2, pair, 0)

        @pl.when((nblk & 1) == 1)
        def _():
            run([blk0 + nblk - 1])

    @pl.when(e == N_EXPERTS - 1)
    def _():
        for i in range(RING):
            @pl.when(i < ntot)
            def _():
                out_wait((ntot - 1 - i) & (RING - 1))


def _ffn(blk0, nblk, ntot, xs, wg, wu, wd):
    grid_spec = pltpu.PrefetchScalarGridSpec(
        num_scalar_prefetch=3,
        grid=(N_EXPERTS,),
        in_specs=[pl.BlockSpec(memory_space=pl.ANY),
                  pl.BlockSpec((1, D_MODEL, EXPERT_FF), lambda e, *_: (e, 0, 0)),
                  pl.BlockSpec((1, D_MODEL, EXPERT_FF), lambda e, *_: (e, 0, 0)),
                  pl.BlockSpec((1, EXPERT_FF, D_MODEL), lambda e, *_: (e, 0, 0))],
        out_specs=pl.BlockSpec(memory_space=pl.ANY),
        scratch_shapes=[pltpu.VMEM((RING, ROW_BLOCK * QUAD, 128), jnp.uint32),
                        pltpu.VMEM((RING, ROW_BLOCK * QUAD, 128), jnp.uint32),
                        pltpu.SemaphoreType.DMA((RING,)), pltpu.SemaphoreType.DMA((RING,)),
                        pltpu.VMEM((D_MODEL, 2 * EXPERT_FF), _BF16), pltpu.VMEM((EXPERT_FF, D_MODEL), _BF16)],
    )
    return pl.pallas_call(
        _ffn_body,
        grid_spec=grid_spec,
        out_shape=jax.ShapeDtypeStruct(xs.shape, jnp.uint32),
        compiler_params=_cparams("arbitrary"),
        name="ffn",
    )(blk0, nblk, ntot, xs, wg, wu, wd)


def _sc_position(idx, rank, pstart, after):
    n = idx.shape[0]
    per_w = n // SC_WORKERS
    assert per_w * SC_WORKERS == n and per_w % SC_LANES == 0
    mesh = plsc.VectorSubcoreMesh(core_axis_name="c", subcore_axis_name="s",
                                  num_cores=SC_CORES, num_subcores=SC_SUBCORES)

    @functools.partial(
        pl.kernel, mesh=mesh,
        out_type=jax.ShapeDtypeStruct((n,), _I32),
        scratch_types=[pltpu.VMEM((per_w,), _I32), pltpu.VMEM((per_w,), _I32), pltpu.VMEM((per_w,), _I32),
                       pltpu.VMEM((N_EXPERTS,), _I32)],
        compiler_params=pltpu.CompilerParams(needs_layout_passes=False),
        name="sc_position",
    )
    def position(idx_hbm, rank_hbm, ps_hbm, after_hbm, out_hbm, idx_v, rank_v, pos_v, ps_v):
        del after_hbm
        wid = lax.axis_index("s") * SC_CORES + lax.axis_index("c")
        mine = pl.ds(pl.multiple_of(wid * per_w, per_w), per_w)
        pltpu.sync_copy(ps_hbm, ps_v)
        pltpu.sync_copy(idx_hbm.at[mine], idx_v)
        pltpu.sync_copy(rank_hbm.at[mine], rank_v)

        @pl.loop(0, per_w, step=SC_LANES)
        def _(j):
            lanes = pl.ds(j, SC_LANES)
            pos_v[lanes] = plsc.load_gather(ps_v, [idx_v[lanes]]) + rank_v[lanes]

        pltpu.sync_copy(pos_v, out_hbm.at[mine])

    return position(idx, rank, pstart, after)


def _sc_gather(table, idx):
    b = idx.shape[0]
    nchunk = b // (SC_WORKERS * SC_CHUNK)
    assert nchunk * SC_WORKERS * SC_CHUNK == b and nchunk % SC_RING == 0
    idx2 = idx.reshape(SC_WORKERS * nchunk, SC_CHUNK)
    row = table.shape[1:]
    mesh = plsc.VectorSubcoreMesh(core_axis_name="c", subcore_axis_name="s",
                                  num_cores=SC_CORES, num_subcores=SC_SUBCORES)

    @functools.partial(
        pl.kernel, mesh=mesh,
        out_type=jax.ShapeDtypeStruct((b,) + row, table.dtype),
        scratch_types=[pltpu.VMEM((nchunk, SC_CHUNK), _I32), pltpu.VMEM((SC_RING, SC_CHUNK) + row, table.dtype),
                       pltpu.SemaphoreType.DMA((SC_RING,)), pltpu.SemaphoreType.DMA((SC_RING,))],
        name="sc_gather",
    )
    def gather(table_hbm, idx_hbm, out_hbm, idx_v, rows_v, sem_g, sem_w):
        wid = lax.axis_index("s") * SC_CORES + lax.axis_index("c")
        c0 = wid * nchunk
        pltpu.sync_copy(idx_hbm.at[pl.ds(pl.multiple_of(c0, nchunk), nchunk)], idx_v)

        def fetch(i, s):
            return pltpu.make_async_copy(table_hbm.at[idx_v.at[i]], rows_v.at[s], sem_g.at[s])

        def flush(i, s):
            rows = pl.ds(pl.multiple_of((c0 + i) * SC_CHUNK, SC_CHUNK), SC_CHUNK)
            return pltpu.make_async_copy(rows_v.at[s], out_hbm.at[rows], sem_w.at[s])

        for s in range(SC_RING):
            fetch(s, s).start()

        @pl.loop(0, nchunk, step=SC_RING)
        def _(g):
            for s in range(SC_RING):
                i = g + s
                fetch(i, s).wait()
                flush(i, s).start()
                flush(i, s).wait()

                @pl.when(i + SC_RING < nchunk)
                def _():
                    fetch(i + SC_RING, s).start()

    return gather(table, idx2)


def _sc_scatter(rows, pos3, n_out, row0):
    nchunk, nk, w = pos3.shape
    per_w = nchunk // SC_WORKERS
    assert per_w * SC_WORKERS == nchunk and w <= 128 and row0 % w == 0 and rows.shape[0] >= row0 + nchunk * w
    row = rows.shape[1:]
    mesh = plsc.VectorSubcoreMesh(core_axis_name="c", subcore_axis_name="s",
                                  num_cores=SC_CORES, num_subcores=SC_SUBCORES)

    @functools.partial(
        pl.kernel, mesh=mesh,
        out_type=(jax.ShapeDtypeStruct((n_out,) + row, rows.dtype),
                  jax.ShapeDtypeStruct((SC_WORKERS, nk, w), _I32)),
        scratch_types=[pltpu.VMEM((nk, w), _I32), pltpu.VMEM((w,) + row, rows.dtype), pltpu.SemaphoreType.DMA],
        name="sc_scatter",
    )
    def scatter(rows_hbm, pos_hbm, out_hbm, done_hbm, idx_v, rows_v, sem):
        wid = lax.axis_index("s") * SC_CORES + lax.axis_index("c")

        @pl.loop(0, per_w)
        def _(i):
            c = wid * per_w + i
            pltpu.sync_copy(pos_hbm.at[c], idx_v)
            pltpu.sync_copy(rows_hbm.at[pl.ds(pl.multiple_of(row0 + c * w, w), w)], rows_v)
            copies = [pltpu.async_copy(rows_v, out_hbm.at[idx_v.at[k]], sem) for k in range(nk)]
            for cp in copies:
                cp.wait()

        pltpu.sync_copy(idx_v, done_hbm.at[wid])

    return scatter(rows, pos3)


def _padfill_body(cnt_ref, pst_ref, pcn_ref, xs_in, xs_out, zbuf, zsem):
    del xs_in
    zbuf[...] = jnp.zeros_like(zbuf)

    def pad_runs(e, act):
        pad = pcn_ref[e] - cnt_ref[e]
        base = pst_ref[e] + cnt_ref[e]
        for b in range(ROW_BLOCK.bit_length() - 1):
            n = 1 << b

            @pl.when(((pad >> b) & 1) == 1)
            def _():
                off = base + (pad & (n - 1))
                act(pltpu.make_async_copy(zbuf.at[pl.ds(0, QUAD * n)],
                                          xs_out.at[pl.ds(QUAD * off, QUAD * n)], zsem))

    def start_all(e, c):
        pad_runs(e, lambda d: d.start())
        return c

    def wait_all(e, c):
        pad_runs(e, lambda d: d.wait())
        return c

    lax.fori_loop(0, N_EXPERTS, start_all, 0)
    lax.fori_loop(0, N_EXPERTS, wait_all, 0)


def _padfill(counts, pstarts, pcounts, xs):
    grid_spec = pltpu.PrefetchScalarGridSpec(
        num_scalar_prefetch=3,
        grid=(1,),
        in_specs=[pl.BlockSpec(memory_space=pl.ANY)],
        out_specs=pl.BlockSpec(memory_space=pl.ANY),
        scratch_shapes=[pltpu.VMEM((QUAD * ROW_BLOCK // 2, 128), jnp.uint32), pltpu.SemaphoreType.DMA],
    )
    return pl.pallas_call(
        _padfill_body,
        grid_spec=grid_spec,
        out_shape=jax.ShapeDtypeStruct(xs.shape, xs.dtype),
        input_output_aliases={3: 0},
        compiler_params=_cparams("arbitrary"),
        name="padfill",
    )(counts, pstarts, pcounts, xs)


def _combine_stream_body(gate_ref, h1_ref, yg_ref, wsg_ref, wsu_ref, wsd_ref, g_ref, b_ref, out_ref, *, tt):
    x = h1_ref[...]
    xb = x.astype(_BF16)
    shared = _mm(_silu(_mm(xb, wsg_ref[...])) * _mm(xb, wsu_ref[...]), wsd_ref[...])
    gcol = gate_ref[...].T
    acc_lo = jnp.zeros((tt, HALF), _F32)
    acc_hi = jnp.zeros((tt, HALF), _F32)
    for k in range(TOP_K):
        lo, hi = _unpack_halves(_load_rows(yg_ref.at[0, k], tt))
        acc_lo = acc_lo + gcol[:, k:k + 1] * lo
        acc_hi = acc_hi + gcol[:, k:k + 1] * hi
    routed = jnp.concatenate([acc_lo, acc_hi], axis=1)
    out_ref[...] = _layer_norm(DN_ALPHA * x + (routed + shared), g_ref[...], b_ref[...])


def _combine_stream(gate, h1, yg, wsg, wsu, wsd, g, b, out_prev, *, tt, tile0):
    t_all = h1.shape[0]
    t = gate.shape[1]
    full = lambda a: pl.BlockSpec(a.shape, lambda i: (0,) * a.ndim)
    in_specs = [pl.BlockSpec((TOP_K, tt), lambda i: (0, i)), pl.BlockSpec((tt, D_MODEL), lambda i: (i + tile0, 0)),
                pl.BlockSpec((1, TOP_K, tt * QUAD, 128), lambda i: (i, 0, 0, 0)),
                full(wsg), full(wsu), full(wsd), full(g), full(b)]
    args = [gate, h1, yg, wsg, wsu, wsd, g, b]
    aliases = {}
    body = functools.partial(_combine_stream_body, tt=tt)
    if out_prev is not None:
        in_specs.append(pl.BlockSpec(memory_space=pl.ANY))
        args.append(out_prev)
        aliases = {len(args) - 1: 0}
        body = lambda *refs: _combine_stream_body(*refs[:8], refs[9], tt=tt)
    return pl.pallas_call(
        body,
        grid=(t // tt,),
        in_specs=in_specs,
        out_specs=pl.BlockSpec((tt, D_MODEL), lambda i: (i + tile0, 0)),
        out_shape=jax.ShapeDtypeStruct((t_all, D_MODEL), _F32),
        input_output_aliases=aliases,
        compiler_params=_cparams("arbitrary"),
        name="combine",
    )(*args)


def _pick(n, pref):
    t = min(n, pref)
    while n % t:
        t -= CHUNK
    return t


def _mixer(x, tails, s0, wts, gnw, *, lt, lg, nbb):
    yc, q, k, v, z, bgc, bgr, tails_out = _premix(x, tails, wts, lt=lt)
    bsz, seq, _ = x.shape
    nch = seq // CHUNK
    grow = bgr[:, GDN_HEADS:2 * GDN_HEADS, :].reshape(bsz, GDN_HEADS, nch, CHUNK)
    grow = grow.transpose(0, 2, 1, 3).reshape(bsz, nch, 1, STACK)
    yg, s_out = _gdn(q, k, v, z, bgc, grow, s0, gnw, lg=lg, nbb=nbb)
    return yc, yg, tails_out, s_out


def kernel(x, meta_tokens, w_in, conv_w, conv_norm_w, gdn_conv_w, a_log, dt_bias, gdn_norm_w, w_out,
           ln1_g, ln1_b, w_router, b_router, w_gate, w_up, w_down, ws_gate, ws_up, ws_down, ln2_g, ln2_b):
    assert w_in.shape[0] == 1, "single-layer stack"
    bsz, seq, d = x.shape
    assert d == D_MODEL and seq % CHUNK == 0
    c, gw = CONV_WIDTH, GDN_WIDTH
    win = w_in[0].astype(_BF16)
    wbd = win[:, 3 * c + 4 * gw:]
    zpad = jnp.zeros((128 - 2 * GDN_HEADS,), _F32)
    zpad4 = jnp.zeros((GDN_HEADS,), _F32)
    prow = jnp.zeros((8, 128), _F32)
    prow = prow.at[0].set(jnp.concatenate([zpad4, a_log[0], zpad]))
    prow = prow.at[1].set(jnp.concatenate([zpad4, dt_bias[0], zpad]))
    wts = (win[:, :3 * c], win[:, 3 * c:3 * c + 3 * gw], win[:, 3 * c + 3 * gw:3 * c + 4 * gw],
           jnp.pad(wbd, ((0, 0), (0, 128 - 2 * GDN_HEADS))), wbd.T,
           conv_w[0], conv_norm_w, gdn_conv_w[0], prow, prow.T[:8])
    gnw = gdn_norm_w

    meta = jnp.concatenate([jnp.zeros((CHUNK - N_META, d), x.dtype), meta_tokens.astype(x.dtype)])[None]
    tails0 = jnp.zeros((HIST, c + 3 * gw), _F32)
    s00 = jnp.zeros((GDN_HEADS, GDN_HEAD_DIM, GDN_HEAD_DIM), _F32)
    _, _, tails_m, s_m = _mixer(meta, tails0, s00, wts, gnw, lt=CHUNK, lg=CHUNK, nbb=1)

    yc, yg, _, _ = _mixer(x, tails_m[0], s_m[0], wts, gnw, lt=_pick(seq, 512), lg=_pick(seq, 512),
                          nbb=GDN_ROWS if bsz % GDN_ROWS == 0 else 1)

    t = bsz * seq
    tm = _pick(t, 512)
    h1, h1p = _outproj(yc.reshape(t, c), yg.reshape(t, gw), x.reshape(t, d), w_out[0].astype(_BF16),
                       ln1_g, ln1_b, tm=tm)

    tt = _pick(t, 256)
    wr_t = w_router[0].T
    wr_hi = wr_t.astype(_BF16)
    wr_lo = (wr_t - wr_hi.astype(_F32)).astype(_BF16)
    shared_w = (ws_gate[0].astype(_BF16), ws_up[0].astype(_BF16), ws_down[0].astype(_BF16))
    h1p3 = h1p.reshape(t, QUAD, 128)

    parts = MOE_PARTS if t % (MOE_PARTS * tt * SC_WORKERS) == 0 else 1
    tp = t // parts
    nb = tp * TOP_K // ROW_BLOCK + N_EXPERTS
    out = None
    sc_prev = jnp.zeros((SC_WORKERS, TOP_K, SC_WINDOW), _I32)
    for part in range(parts):
        tile0 = part * (tp // tt)
        idx, gate, rank, cnt = _router(h1, wr_hi, wr_lo, b_router[0][:, None], tt=tt, tile0=tile0, t=tp)
        counts = cnt[:, 0].astype(_I32)
        pcounts = (counts + ROW_BLOCK - 1) // ROW_BLOCK * ROW_BLOCK
        pends = jnp.cumsum(pcounts)
        pstarts = pends - pcounts
        tiled = lambda a: a.reshape(TOP_K, tp // tt, tt).transpose(1, 0, 2).reshape(tp * TOP_K)
        pos = _sc_position(tiled(idx), tiled(rank), pstarts.astype(_I32), sc_prev)
        pos = pos.reshape(tp // tt, TOP_K, tt)
        nwin = tt // SC_WINDOW
        pos3 = pos.reshape(tp // tt, TOP_K, nwin, SC_WINDOW).transpose(0, 2, 1, 3)
        pos3 = pos3.reshape(tp // SC_WINDOW, TOP_K, SC_WINDOW)
        xs, sc_prev = _sc_scatter(h1p3, pos3, nb * ROW_BLOCK, part * tp)
        xs = _padfill(counts, pstarts.astype(_I32), pcounts.astype(_I32), xs.reshape(nb * ROW_BLOCK * QUAD, 128))
        ys = _ffn((pstarts // ROW_BLOCK).astype(_I32), (pcounts // ROW_BLOCK).astype(_I32),
                  (pends[-1:] // ROW_BLOCK).astype(_I32), xs, w_gate[0], w_up[0], w_down[0])
        yg = _sc_gather(ys.reshape(nb * ROW_BLOCK, QUAD, 128), pos.reshape(tp * TOP_K))
        yg = yg.reshape(tp // tt, TOP_K, tt * QUAD, 128)
        out = _combine_stream(gate, h1, yg, *shared_w, ln2_g, ln2_b, out, tt=tt, tile0=tile0)
    return out.reshape(bsz, seq, d)
```

```python
import functools

import jax
import jax.numpy as jnp
from jax import lax
from jax.experimental import pallas as pl
from jax.experimental.pallas import tpu as pltpu
from jax.experimental.pallas import tpu_sc as plsc

_F32 = jnp.float32
_BF16 = jnp.bfloat16
_I32 = jnp.int32

D_MODEL = 1024
N_META = 16
CONV_WIDTH = 512
CONV_K = 3
GDN_HEADS = 4
GDN_HEAD_DIM = 128
GDN_WIDTH = GDN_HEADS * GDN_HEAD_DIM
GDN_CONV_K = 4
CHUNK = 64
N_EXPERTS = 256
TOP_K = 8
N_GROUPS = 8
TOPK_GROUPS = 4
E_PER_GROUP = N_EXPERTS // N_GROUPS
EXPERT_FF = 256
ROUTED_SCALE = 2.5
ROW_BLOCK = 256
DN_ALPHA = 2.0 ** 0.25
NORM_EPS = 1e-5
HALF = D_MODEL // 2
QUAD = HALF // 128
STACK = GDN_HEADS * CHUNK
HIST = 8
GDN_ROWS = 4
PREMIX_SUB = 2
SC_CORES = 2
SC_SUBCORES = 16
SC_WORKERS = SC_CORES * SC_SUBCORES
SC_LANES = 16
SC_CHUNK = 64
SC_RING = 2
SC_WINDOW = 128
LAYER_PARTS = 2
RING = 8
IN_AHEAD = RING - 2

V7X_VMEM_BYTES = 64 * 1024 * 1024
VMEM_LIMIT = V7X_VMEM_BYTES - 8 * 1024 * 1024


def _cparams(*sem):
    return pltpu.CompilerParams(dimension_semantics=sem, vmem_limit_bytes=VMEM_LIMIT)


def _mm(a, b):
    return jnp.dot(a.astype(_BF16), b.astype(_BF16), preferred_element_type=_F32)


def _mm_nt(a, b):
    return lax.dot_general(a.astype(_BF16), b.astype(_BF16), (((1,), (1,)), ((), ())),
                           preferred_element_type=_F32)


def _mm_tn(a, b):
    return lax.dot_general(a.astype(_BF16), b.astype(_BF16), (((0,), (0,)), ((), ())),
                           preferred_element_type=_F32)


def _sigmoid(x):
    return 1.0 / (1.0 + jnp.exp(-x))


def _silu(x):
    return x * _sigmoid(x)


def _softplus(x):
    return jnp.maximum(x, 0.0) + jnp.log1p(jnp.exp(-jnp.abs(x)))


def _pack_halves(y):
    return pltpu.pack_elementwise([y[:, :HALF], y[:, HALF:]], packed_dtype=_BF16)


def _store_rows(ref, packed):
    r = packed.shape[0]
    for c in range(QUAD):
        ref[pl.ds(c, r, stride=QUAD), :] = packed[:, c * 128:(c + 1) * 128]


def _load_rows(ref, r):
    return jnp.concatenate([ref[pl.ds(c, r, stride=QUAD), :] for c in range(QUAD)], axis=1)


def _unpack_halves(p):
    lo = pltpu.unpack_elementwise(p, index=0, packed_dtype=_BF16, unpacked_dtype=_F32)
    hi = pltpu.unpack_elementwise(p, index=1, packed_dtype=_BF16, unpacked_dtype=_F32)
    return lo, hi


def _layer_norm(h, g, b):
    mu = jnp.mean(h, axis=-1, keepdims=True)
    d = h - mu
    var = jnp.mean(d * d, axis=-1, keepdims=True)
    return d * lax.rsqrt(var + NORM_EPS) * g + b


def _premix_body(x_ref, tails_ref, wa_ref, wq_ref, wz_ref, wbd_ref, wbdt_ref, cw_ref, cnw_ref,
                 gcw_ref, prow_ref, pcol_ref,
                 yc_ref, q_ref, k_ref, v_ref, z_ref, bgc_ref, bgr_ref, tout_ref, ext_ref, *, lt):
    cw_ = CONV_WIDTH

    @pl.when(pl.program_id(1) == 0)
    def _():
        ext_ref[0:HIST, :] = tails_ref[...]

    cw = cw_ref[...]
    gcw = gcw_ref[...]
    prow = prow_ref[...]
    pcol = pcol_ref[...]

    def sub_tile(r0, n):
        rows = slice(r0, r0 + n)
        erows = slice(HIST + r0, HIST + r0 + n)
        xb = x_ref[0, rows, :].astype(_BF16)
        pa = jnp.dot(xb, wa_ref[...], preferred_element_type=_F32)
        yield
        gate_b = pa[:, 0:cw_]
        u = pa[:, cw_:2 * cw_] * pa[:, 2 * cw_:3 * cw_]
        ext_ref[erows, 0:cw_] = u
        pq = jnp.dot(xb, wq_ref[...], preferred_element_type=_F32)
        yield
        ext_ref[erows, cw_:] = pq
        zz = jnp.dot(xb, wz_ref[...], preferred_element_type=_F32)
        bdc = jnp.dot(xb, wbd_ref[...], preferred_element_type=_F32)
        bdr = _mm_nt(wbdt_ref[...], xb)
        yield

        ca = u * cw[CONV_K - 1:CONV_K, :]
        for j in range(CONV_K - 1):
            ca = ca + ext_ref[pl.ds(HIST + r0 - (CONV_K - 1) + j, n), 0:cw_] * cw[j:j + 1, :]
        yc = gate_b * ca
        ms = jnp.mean(yc * yc, axis=-1, keepdims=True)
        yc_ref[0, rows, :] = (yc * lax.rsqrt(ms + NORM_EPS) * cnw_ref[...]).astype(_BF16)

        cq = pq * gcw[GDN_CONV_K - 1:GDN_CONV_K, :]
        for j in range(GDN_CONV_K - 1):
            cq = cq + ext_ref[pl.ds(HIST + r0 - (GDN_CONV_K - 1) + j, n), cw_:] * gcw[j:j + 1, :]
        s = _silu(cq)
        for h in range(GDN_HEADS):
            lo, hi = h * GDN_HEAD_DIM, (h + 1) * GDN_HEAD_DIM
            qh = s[:, lo:hi]
            kh = s[:, GDN_WIDTH + lo:GDN_WIDTH + hi]
            qn = qh * lax.rsqrt(jnp.sum(qh * qh, axis=-1, keepdims=True) + 1e-6)
            kn = kh * lax.rsqrt(jnp.sum(kh * kh, axis=-1, keepdims=True) + 1e-6)
            q_ref[0, rows, lo:hi] = (qn * (GDN_HEAD_DIM ** -0.5)).astype(_BF16)
            k_ref[0, rows, lo:hi] = kn.astype(_BF16)
        v_ref[0, rows, :] = s[:, 2 * GDN_WIDTH:].astype(_BF16)
        z_ref[0, rows, :] = zz.astype(_BF16)

        g_c = -jnp.exp(prow[0:1, :]) * _softplus(bdc + prow[1:2, :])
        lane = lax.broadcasted_iota(_I32, bdc.shape, 1)
        bgc_ref[0, rows, :] = jnp.where(lane < GDN_HEADS, _sigmoid(bdc), g_c)
        g_r = -jnp.exp(pcol[:, 0:1]) * _softplus(bdr + pcol[:, 1:2])
        row = lax.broadcasted_iota(_I32, bdr.shape, 0)
        bgr_ref[0, :, rows] = jnp.where(row < GDN_HEADS, _sigmoid(bdr), g_r)

    n_sub = PREMIX_SUB if lt % (PREMIX_SUB * 128) == 0 else 1
    live = [sub_tile(i * (lt // n_sub), lt // n_sub) for i in range(n_sub)]
    while live:
        live = [g for g in live if next(g, live) is not live]

    tail = ext_ref[lt:lt + HIST, :]
    ext_ref[0:HIST, :] = tail
    tout_ref[0] = tail


def _premix(x, tails, wts, *, lt, b0, bsz):
    _, seq, d = x.shape
    assert seq % lt == 0
    grid = (bsz, seq // lt)
    full = lambda a: pl.BlockSpec(a.shape, lambda b, j: (0,) * a.ndim)
    tok = lambda w: pl.BlockSpec((1, lt, w), lambda b, j: (b, j, 0))
    x_spec = pl.BlockSpec((1, lt, d), lambda b, j: (b + b0, j, 0))
    (wa, wq, wz, wbd, wbdt, cw, cnw, gcw, prow, pcol) = wts
    ext_w = CONV_WIDTH + 3 * GDN_WIDTH
    out_shape = (
        jax.ShapeDtypeStruct((bsz, seq, CONV_WIDTH), _BF16),
        jax.ShapeDtypeStruct((bsz, seq, GDN_WIDTH), _BF16),
        jax.ShapeDtypeStruct((bsz, seq, GDN_WIDTH), _BF16),
        jax.ShapeDtypeStruct((bsz, seq, GDN_WIDTH), _BF16),
        jax.ShapeDtypeStruct((bsz, seq, GDN_WIDTH), _BF16),
        jax.ShapeDtypeStruct((bsz, seq, 128), _F32),
        jax.ShapeDtypeStruct((bsz, 8, seq), _F32),
        jax.ShapeDtypeStruct((bsz, HIST, ext_w), _F32),
    )
    out_specs = (tok(CONV_WIDTH), tok(GDN_WIDTH), tok(GDN_WIDTH), tok(GDN_WIDTH), tok(GDN_WIDTH),
                 tok(128), pl.BlockSpec((1, 8, lt), lambda b, j: (b, 0, j)),
                 pl.BlockSpec((1, HIST, ext_w), lambda b, j: (b, 0, 0)))
    return pl.pallas_call(
        functools.partial(_premix_body, lt=lt),
        grid=grid,
        in_specs=[x_spec, full(tails)] + [full(w) for w in wts],
        out_specs=out_specs,
        out_shape=out_shape,
        scratch_shapes=[pltpu.VMEM((HIST + lt, ext_w), _F32)],
        compiler_params=_cparams("arbitrary", "arbitrary"),
        name="premix",
    )(x, tails, *wts)


def _cumsum_rows(x):
    row = lax.broadcasted_iota(_I32, x.shape, 0)
    s = 1
    while s < x.shape[0]:
        x = x + jnp.where(row >= s, pltpu.roll(x, s, 0), 0.0)
        s *= 2
    return x


def _cumsum_lanes_seg(x):
    lane = lax.broadcasted_iota(_I32, x.shape, 1) & (CHUNK - 1)
    s = 1
    while s < CHUNK:
        x = x + jnp.where(lane >= s, pltpu.roll(x, s, 1), 0.0)
        s *= 2
    return x


def _stack_heads(a):
    return jnp.concatenate([a[:, h * GDN_HEAD_DIM:(h + 1) * GDN_HEAD_DIM] for h in range(GDN_HEADS)], axis=0)


def _gdn_body(q_ref, k_ref, v_ref, z_ref, bgc_ref, grow_ref, s0_ref, gnw_ref,
              y_ref, sout_ref, s_ref, *, nc, nbb):
    @pl.when(pl.program_id(1) == 0)
    def _():
        for r in range(nbb):
            s_ref[r] = s0_ref[...]

    ri = lax.broadcasted_iota(_I32, (STACK, STACK), 0)
    ci = lax.broadcasted_iota(_I32, (STACK, STACK), 1)
    same64 = (ri >> 6) == (ci >> 6)
    same32 = (ri >> 5) == (ci >> 5)
    same16 = (ri >> 4) == (ci >> 4)
    low_incl = same64 & (ri >= ci)
    low_strict = same64 & (ri > ci)
    gnw = gnw_ref[...]

    def chunk_row(r, c):
        off = pl.multiple_of(c * CHUNK, CHUNK)
        q_all = _stack_heads(q_ref[r, pl.ds(off, CHUNK), :].astype(_F32))
        k_all = _stack_heads(k_ref[r, pl.ds(off, CHUNK), :].astype(_F32))
        v_all = _stack_heads(v_ref[r, pl.ds(off, CHUNK), :].astype(_F32))
        bgc = bgc_ref[r, pl.ds(off, CHUNK), :]
        gcs = _cumsum_rows(bgc)
        hd = (CHUNK, GDN_HEAD_DIM)
        beta_b = jnp.concatenate(
            [jnp.broadcast_to(bgc[:, h:h + 1], hd) for h in range(GDN_HEADS)], axis=0)
        gc_b = jnp.concatenate(
            [jnp.broadcast_to(gcs[:, GDN_HEADS + h:GDN_HEADS + h + 1], hd) for h in range(GDN_HEADS)], axis=0)
        gl = [gcs[CHUNK - 1:CHUNK, GDN_HEADS + h:GDN_HEADS + h + 1] for h in range(GDN_HEADS)]
        gl_b = jnp.concatenate([jnp.broadcast_to(g1, hd) for g1 in gl], axis=0)
        gcr = _cumsum_lanes_seg(jnp.broadcast_to(grow_ref[r, c], (8, STACK)))[0:1, :]

        diff = jnp.concatenate([gc_b, gc_b], axis=1) - gcr
        decay = jnp.exp(jnp.where(low_incl, diff, -1e30))
        kb = k_all * beta_b
        a1 = _mm_nt(jnp.concatenate([kb, q_all], axis=0), k_all)
        yield
        m = jnp.where(low_strict, a1[:STACK] * decay, 0.0)
        attn = a1[STACK:] * decay

        l16 = jnp.where(same16, m, 0.0)
        c1 = jnp.where(same32 & jnp.logical_not(same16), m, 0.0)
        c2 = jnp.where(same32, 0.0, m)
        p2 = _mm(l16, l16)
        yield
        p4 = _mm(p2, p2)
        t = _mm(l16, p2)
        yield
        na = p2 - l16 - t
        p8 = _mm(p4, p4)
        t = _mm(na, p4)
        yield
        nb = na + p4 + t
        t = _mm(nb, p8)
        yield
        ncm = nb + p8 + t
        t = _mm(c1, ncm)
        yield
        y1 = c1 + t
        t = _mm(ncm, y1)
        yield
        n1 = ncm - y1 - t
        t = _mm(c2, n1)
        yield
        y2 = c2 + t
        t = _mm(n1, y2)
        yield
        nt = n1 - y2 - t

        egc = jnp.exp(gc_b)
        rhs = jnp.concatenate([v_all * beta_b, kb * egc], axis=1)
        t = _mm(nt, rhs)
        yield
        uw = rhs + t
        u_all = uw[:, :GDN_HEAD_DIM]
        w_all = uw[:, GDN_HEAD_DIM:]
        qd = q_all * egc
        kd = k_all * jnp.exp(gl_b - gc_b)

        bs = []
        for h in range(GDN_HEADS):
            r0, r1 = h * CHUNK, (h + 1) * CHUNK
            bs.append(_mm(jnp.concatenate([w_all[r0:r1], qd[r0:r1]], axis=0), s_ref[r, h]))
        yield
        vn = [u_all[h * CHUNK:(h + 1) * CHUNK] - bs[h][:CHUNK] for h in range(GDN_HEADS)]
        vn_all = jnp.concatenate(vn, axis=0)
        t = _mm(attn, vn_all)
        ds = [_mm_tn(kd[h * CHUNK:(h + 1) * CHUNK], vn[h]) for h in range(GDN_HEADS)]
        yield
        o_all = jnp.concatenate([b[CHUNK:] for b in bs], axis=0) + t
        for h in range(GDN_HEADS):
            r0, r1 = h * CHUNK, (h + 1) * CHUNK
            s_ref[r, h] = s_ref[r, h] * jnp.exp(gl[h]) + ds[h]
            o = o_all[r0:r1]
            zz = z_ref[r, pl.ds(off, CHUNK), h * GDN_HEAD_DIM:(h + 1) * GDN_HEAD_DIM].astype(_F32)
            on = o * lax.rsqrt(jnp.mean(o * o, axis=-1, keepdims=True) + NORM_EPS) * gnw
            y_ref[r, pl.ds(off, CHUNK), h * GDN_HEAD_DIM:(h + 1) * GDN_HEAD_DIM] = (on * _silu(zz)).astype(_BF16)

    def chunk(c, carry):
        live = [chunk_row(r, c) for r in range(nbb)]
        while live:
            live = [g for g in live if next(g, live) is not live]
        return carry

    lax.fori_loop(0, nc, chunk, 0)
    sout_ref[...] = s_ref[...]


def _gdn(q, k, v, z, bgc, grow, s0, gnw, *, lg, nbb):
    bsz, seq, _ = q.shape
    assert seq % lg == 0 and lg % CHUNK == 0 and bsz % nbb == 0
    nc = lg // CHUNK
    tok = lambda w: pl.BlockSpec((nbb, lg, w), lambda b, j: (b, j, 0))
    full = lambda a: pl.BlockSpec(a.shape, lambda b, j: (0,) * a.ndim)
    st = (nbb, GDN_HEADS, GDN_HEAD_DIM, GDN_HEAD_DIM)
    return pl.pallas_call(
        functools.partial(_gdn_body, nc=nc, nbb=nbb),
        grid=(bsz // nbb, seq // lg),
        in_specs=[tok(GDN_WIDTH)] * 4 + [tok(128), pl.BlockSpec((nbb, nc, 1, STACK), lambda b, j: (b, j, 0, 0)),
                                           full(s0), full(gnw)],
        out_specs=(tok(GDN_WIDTH), pl.BlockSpec(st, lambda b, j: (b, 0, 0, 0))),
        out_shape=(jax.ShapeDtypeStruct((bsz, seq, GDN_WIDTH), _BF16),
                   jax.ShapeDtypeStruct((bsz, GDN_HEADS, GDN_HEAD_DIM, GDN_HEAD_DIM), _F32)),
        scratch_shapes=[pltpu.VMEM(st, _F32)],
        compiler_params=_cparams("arbitrary", "arbitrary"),
        name="gdn",
    )(q, k, v, z, bgc, grow, s0, gnw)


def _outproj_body(yc_ref, yg_ref, x_ref, wo_ref, g_ref, b_ref, h1_ref, h1p_ref):
    mix = (jnp.dot(yc_ref[...], wo_ref[0:CONV_WIDTH, :], preferred_element_type=_F32)
           + jnp.dot(yg_ref[...], wo_ref[CONV_WIDTH:, :], preferred_element_type=_F32))
    h1 = _layer_norm(DN_ALPHA * x_ref[...] + mix, g_ref[...], b_ref[...])
    h1_ref[...] = h1
    _store_rows(h1p_ref, _pack_halves(h1))


def _outproj(yc, yg, x2d, wo, g, b, *, tm, tile0):
    t = yc.shape[0]
    assert t % tm == 0
    row = lambda w: pl.BlockSpec((tm, w), lambda i: (i, 0))
    full = lambda a: pl.BlockSpec(a.shape, lambda i: (0,) * a.ndim)
    return pl.pallas_call(
        _outproj_body,
        grid=(t // tm,),
        in_specs=[row(CONV_WIDTH), row(GDN_WIDTH), pl.BlockSpec((tm, D_MODEL), lambda i: (i + tile0, 0)),
                  full(wo), full(g), full(b)],
        out_specs=(row(D_MODEL), pl.BlockSpec((tm * QUAD, 128), lambda i: (i, 0))),
        out_shape=(jax.ShapeDtypeStruct((t, D_MODEL), _F32), jax.ShapeDtypeStruct((t * QUAD, 128), jnp.uint32)),
        compiler_params=_cparams("arbitrary"),
        name="outproj",
    )(yc, yg, x2d, wo, g, b)


def _router_body(h1_ref, wh_ref, wl_ref, br_ref, idx_ref, gate_ref, rank_ref, cnt_ref, carry_ref, *, tt):
    @pl.when(pl.program_id(0) == 0)
    def _():
        carry_ref[...] = jnp.zeros_like(carry_ref)

    x = h1_ref[...]
    xh = x.astype(_BF16)
    xl = (x - xh.astype(_F32)).astype(_BF16)
    wh = wh_ref[...]
    logits = _mm_nt(wh, xh) + _mm_nt(wh, xl) + _mm_nt(wl_ref[...], xh)
    scores = _sigmoid(logits)
    sel = scores + br_ref[...]
    ninf = -jnp.inf

    r32 = lax.broadcasted_iota(_I32, (E_PER_GROUP, tt), 0)
    gsc = []
    for g in range(N_GROUPS):
        xg = sel[g * E_PER_GROUP:(g + 1) * E_PER_GROUP]
        m1 = jnp.max(xg, axis=0, keepdims=True)
        i1 = jnp.min(jnp.where(xg == m1, r32, E_PER_GROUP), axis=0, keepdims=True)
        m2 = jnp.max(jnp.where(r32 == i1, ninf, xg), axis=0, keepdims=True)
        gsc.append(m1 + m2)
    work = jnp.concatenate(gsc, axis=0)
    r8 = lax.broadcasted_iota(_I32, (N_GROUPS, tt), 0)
    gkeep = jnp.zeros((N_GROUPS, tt), _F32)
    for _ in range(TOPK_GROUPS):
        m = jnp.max(work, axis=0, keepdims=True)
        gi = jnp.min(jnp.where(work == m, r8, N_GROUPS), axis=0, keepdims=True)
        pick = r8 == gi
        gkeep = jnp.where(pick, 1.0, gkeep)
        work = jnp.where(pick, ninf, work)
    selm = jnp.concatenate(
        [jnp.where(gkeep[g:g + 1] > 0.5, sel[g * E_PER_GROUP:(g + 1) * E_PER_GROUP], ninf)
         for g in range(N_GROUPS)], axis=0)

    re = lax.broadcasted_iota(_I32, (N_EXPERTS, tt), 0)
    msel = jnp.zeros((N_EXPERTS, tt), _F32)
    idxs, gates = [], []
    for _ in range(TOP_K):
        m = jnp.max(selm, axis=0, keepdims=True)
        ii = jnp.min(jnp.where(selm == m, re, N_EXPERTS), axis=0, keepdims=True)
        hit = re == ii
        idxs.append(ii)
        gates.append(jnp.sum(jnp.where(hit, scores, 0.0), axis=0, keepdims=True))
        selm = jnp.where(hit, ninf, selm)
        msel = jnp.where(hit, 1.0, msel)
    gate = jnp.concatenate(gates, axis=0)
    gate_ref[...] = gate / jnp.sum(gate, axis=0, keepdims=True) * ROUTED_SCALE
    idx_ref[...] = jnp.concatenate(idxs, axis=0)

    ta = lax.broadcasted_iota(_I32, (tt, tt), 0)
    tb = lax.broadcasted_iota(_I32, (tt, tt), 1)
    earlier = jnp.where(ta < tb, 1.0, 0.0)
    carry = carry_ref[...]
    rank_all = _mm(msel, earlier) + carry[:, 0:1]
    rank_ref[...] = jnp.concatenate(
        [jnp.sum(jnp.where(re == ii, rank_all, 0.0), axis=0, keepdims=True) for ii in idxs],
        axis=0).astype(_I32)
    carry = carry + jnp.sum(msel, axis=1, keepdims=True)
    carry_ref[...] = carry
    cnt_ref[...] = carry


def _router(h1, wh, wl, br, *, tt, tile0, t):
    assert t % tt == 0
    full = lambda a: pl.BlockSpec(a.shape, lambda i: (0,) * a.ndim)
    kt = pl.BlockSpec((TOP_K, tt), lambda i: (0, i))
    return pl.pallas_call(
        functools.partial(_router_body, tt=tt),
        grid=(t // tt,),
        in_specs=[pl.BlockSpec((tt, D_MODEL), lambda i: (i + tile0, 0)), full(wh), full(wl), full(br)],
        out_specs=(kt, kt, kt, pl.BlockSpec((N_EXPERTS, 128), lambda i: (0, 0))),
        out_shape=(jax.ShapeDtypeStruct((TOP_K, t), _I32), jax.ShapeDtypeStruct((TOP_K, t), _F32),
                   jax.ShapeDtypeStruct((TOP_K, t), _I32), jax.ShapeDtypeStruct((N_EXPERTS, 128), _F32)),
        scratch_shapes=[pltpu.VMEM((N_EXPERTS, 128), _F32)],
        compiler_params=_cparams("arbitrary"),
        name="router",
    )(h1, wh, wl, br)


def _position_body(idx_ref, rank_ref, pstart_ref, pos_ref, *, tt):
    re = lax.broadcasted_iota(_I32, (N_EXPERTS, tt), 0)
    ps = pstart_ref[...]
    idx = idx_ref[...]
    rows = [jnp.sum(jnp.where(re == idx[k:k + 1], ps, 0), axis=0, keepdims=True) for k in range(TOP_K)]
    pos_ref[0] = jnp.concatenate(rows, axis=0) + rank_ref[...]


def _position(idx, rank, pstart, *, tt):
    t = idx.shape[1]
    kt = pl.BlockSpec((TOP_K, tt), lambda i: (0, i))
    return pl.pallas_call(
        functools.partial(_position_body, tt=tt),
        grid=(t // tt,),
        in_specs=[kt, kt, pl.BlockSpec(pstart.shape, lambda i: (0, 0))],
        out_specs=pl.BlockSpec((1, TOP_K, tt), lambda i: (i, 0, 0)),
        out_shape=jax.ShapeDtypeStruct((t // tt, TOP_K, tt), _I32),
        compiler_params=_cparams("arbitrary"),
        name="position",
    )(idx, rank, pstart)


def _ffn_body(blk0_ref, nblk_ref, ntot_ref, xs_hbm, wg_ref, wu_ref, wd_ref, ys_hbm,
              xbuf, ybuf, sem_in, sem_out, wgu_bf, wd_bf):
    e = pl.program_id(0)
    nblk = nblk_ref[e]
    blk0 = blk0_ref[e]
    ntot = ntot_ref[0]

    blk_rows = ROW_BLOCK * QUAD

    def rows(g):
        return pl.ds(pl.multiple_of(g * blk_rows, blk_rows), blk_rows)

    def in_start(g, slot):
        pltpu.make_async_copy(xs_hbm.at[rows(g)], xbuf.at[slot], sem_in.at[slot]).start()

    def in_wait(slot):
        pltpu.make_async_copy(xs_hbm.at[rows(0)], xbuf.at[slot], sem_in.at[slot]).wait()

    def out_start(g, slot):
        pltpu.make_async_copy(ybuf.at[slot], ys_hbm.at[rows(g)], sem_out.at[slot]).start()

    def out_wait(slot):
        pltpu.make_async_copy(ybuf.at[slot], ys_hbm.at[rows(0)], sem_out.at[slot]).wait()

    @pl.when(e == 0)
    def _():
        for i in range(IN_AHEAD):
            @pl.when(i < ntot)
            def _():
                in_start(i, i)

    @pl.when(nblk > 0)
    def _():
        wgu_bf[:, 0:EXPERT_FF] = wg_ref[0].astype(_BF16)
        wgu_bf[:, EXPERT_FF:] = wu_ref[0].astype(_BF16)
        wd_bf[...] = wd_ref[0].astype(_BF16)

        def acquire(g):
            slot = g & (RING - 1)
            in_wait(slot)

            @pl.when(g + IN_AHEAD < ntot)
            def _():
                in_start(g + IN_AHEAD, (g + IN_AHEAD) & (RING - 1))

            @pl.when(g >= RING)
            def _():
                out_wait(slot)

            return slot

        def compute(slot):
            lo, hi = _unpack_halves(_load_rows(xbuf.at[slot], ROW_BLOCK))
            a = jnp.dot(lo.astype(_BF16), wgu_bf[0:HALF, :], preferred_element_type=_F32)
            yield
            gu = a + jnp.dot(hi.astype(_BF16), wgu_bf[HALF:, :], preferred_element_type=_F32)
            yield
            h = (_silu(gu[:, :EXPERT_FF]) * gu[:, EXPERT_FF:]).astype(_BF16)
            y = jnp.dot(h, wd_bf[...], preferred_element_type=_F32)
            yield
            _store_rows(ybuf.at[slot], _pack_halves(y))

        def run(gs):
            slots = [acquire(g) for g in gs]
            live = [compute(s) for s in slots]
            while live:
                live = [c for c in live if next(c, live) is not live]
            for g, s in zip(gs, slots):
                out_start(g, s)

        def pair(j, carry):
            run([blk0 + 2 * j, blk0 + 2 * j + 1])
            return carry

        lax.fori_loop(0, nblk // 2, pair, 0)

        @pl.when((nblk & 1) == 1)
        def _():
            run([blk0 + nblk - 1])

    @pl.when(e == N_EXPERTS - 1)
    def _():
        for i in range(RING):
            @pl.when(i < ntot)
            def _():
                out_wait((ntot - 1 - i) & (RING - 1))


def _ffn(blk0, nblk, ntot, xs, wg, wu, wd):
    grid_spec = pltpu.PrefetchScalarGridSpec(
        num_scalar_prefetch=3,
        grid=(N_EXPERTS,),
        in_specs=[pl.BlockSpec(memory_space=pl.ANY),
                  pl.BlockSpec((1, D_MODEL, EXPERT_FF), lambda e, *_: (e, 0, 0)),
                  pl.BlockSpec((1, D_MODEL, EXPERT_FF), lambda e, *_: (e, 0, 0)),
                  pl.BlockSpec((1, EXPERT_FF, D_MODEL), lambda e, *_: (e, 0, 0))],
        out_specs=pl.BlockSpec(memory_space=pl.ANY),
        scratch_shapes=[pltpu.VMEM((RING, ROW_BLOCK * QUAD, 128), jnp.uint32),
                        pltpu.VMEM((RING, ROW_BLOCK * QUAD, 128), jnp.uint32),
                        pltpu.SemaphoreType.DMA((RING,)), pltpu.SemaphoreType.DMA((RING,)),
                        pltpu.VMEM((D_MODEL, 2 * EXPERT_FF), _BF16), pltpu.VMEM((EXPERT_FF, D_MODEL), _BF16)],
    )
    return pl.pallas_call(
        _ffn_body,
        grid_spec=grid_spec,
        out_shape=jax.ShapeDtypeStruct(xs.shape, jnp.uint32),
        compiler_params=_cparams("arbitrary"),
        name="ffn",
    )(blk0, nblk, ntot, xs, wg, wu, wd)


def _sc_position(idx, rank, pstart, after):
    n = idx.shape[0]
    per_w = n // SC_WORKERS
    assert per_w * SC_WORKERS == n and per_w % SC_LANES == 0
    mesh = plsc.VectorSubcoreMesh(core_axis_name="c", subcore_axis_name="s",
                                  num_cores=SC_CORES, num_subcores=SC_SUBCORES)

    @functools.partial(
        pl.kernel, mesh=mesh,
        out_type=jax.ShapeDtypeStruct((n,), _I32),
        scratch_types=[pltpu.VMEM((per_w,), _I32), pltpu.VMEM((per_w,), _I32), pltpu.VMEM((per_w,), _I32),
                       pltpu.VMEM((N_EXPERTS,), _I32)],
        compiler_params=pltpu.CompilerParams(needs_layout_passes=False),
        name="sc_position",
    )
    def position(idx_hbm, rank_hbm, ps_hbm, after_hbm, out_hbm, idx_v, rank_v, pos_v, ps_v):
        del after_hbm
        wid = lax.axis_index("s") * SC_CORES + lax.axis_index("c")
        mine = pl.ds(pl.multiple_of(wid * per_w, per_w), per_w)
        pltpu.sync_copy(ps_hbm, ps_v)
        pltpu.sync_copy(idx_hbm.at[mine], idx_v)
        pltpu.sync_copy(rank_hbm.at[mine], rank_v)

        @pl.loop(0, per_w, step=SC_LANES)
        def _(j):
            lanes = pl.ds(j, SC_LANES)
            pos_v[lanes] = plsc.load_gather(ps_v, [idx_v[lanes]]) + rank_v[lanes]

        pltpu.sync_copy(pos_v, out_hbm.at[mine])

    return position(idx, rank, pstart, after)


def _sc_gather(table, idx):
    b = idx.shape[0]
    nchunk = b // (SC_WORKERS * SC_CHUNK)
    assert nchunk * SC_WORKERS * SC_CHUNK == b and nchunk % SC_RING == 0
    idx2 = idx.reshape(SC_WORKERS * nchunk, SC_CHUNK)
    row = table.shape[1:]
    mesh = plsc.VectorSubcoreMesh(core_axis_name="c", subcore_axis_name="s",
                                  num_cores=SC_CORES, num_subcores=SC_SUBCORES)

    @functools.partial(
        pl.kernel, mesh=mesh,
        out_type=jax.ShapeDtypeStruct((b,) + row, table.dtype),
        scratch_types=[pltpu.VMEM((nchunk, SC_CHUNK), _I32), pltpu.VMEM((SC_RING, SC_CHUNK) + row, table.dtype),
                       pltpu.SemaphoreType.DMA((SC_RING,)), pltpu.SemaphoreType.DMA((SC_RING,))],
        name="sc_gather",
    )
    def gather(table_hbm, idx_hbm, out_hbm, idx_v, rows_v, sem_g, sem_w):
        wid = lax.axis_index("s") * SC_CORES + lax.axis_index("c")
        c0 = wid * nchunk
        pltpu.sync_copy(idx_hbm.at[pl.ds(pl.multiple_of(c0, nchunk), nchunk)], idx_v)

        def fetch(i, s):
            return pltpu.make_async_copy(table_hbm.at[idx_v.at[i]], rows_v.at[s], sem_g.at[s])

        def flush(i, s):
            rows = pl.ds(pl.multiple_of((c0 + i) * SC_CHUNK, SC_CHUNK), SC_CHUNK)
            return pltpu.make_async_copy(rows_v.at[s], out_hbm.at[rows], sem_w.at[s])

        for s in range(SC_RING):
            fetch(s, s).start()

        @pl.loop(0, nchunk, step=SC_RING)
        def _(g):
            for s in range(SC_RING):
                i = g + s
                fetch(i, s).wait()
                flush(i, s).start()
                flush(i, s).wait()

                @pl.when(i + SC_RING < nchunk)
                def _():
                    fetch(i + SC_RING, s).start()

    return gather(table, idx2)


def _sc_scatter(rows, pos3, n_out, row0):
    nchunk, nk, w = pos3.shape
    per_w = nchunk // SC_WORKERS
    assert per_w * SC_WORKERS == nchunk and w <= 128 and row0 % w == 0 and rows.shape[0] >= row0 + nchunk * w
    row = rows.shape[1:]
    mesh = plsc.VectorSubcoreMesh(core_axis_name="c", subcore_axis_name="s",
                                  num_cores=SC_CORES, num_subcores=SC_SUBCORES)

    @functools.partial(
        pl.kernel, mesh=mesh,
        out_type=(jax.ShapeDtypeStruct((n_out,) + row, rows.dtype),
                  jax.ShapeDtypeStruct((SC_WORKERS, nk, w), _I32)),
        scratch_types=[pltpu.VMEM((nk, w), _I32), pltpu.VMEM((w,) + row, rows.dtype), pltpu.SemaphoreType.DMA],
        name="sc_scatter",
    )
    def scatter(rows_hbm, pos_hbm, out_hbm, done_hbm, idx_v, rows_v, sem):
        wid = lax.axis_index("s") * SC_CORES + lax.axis_index("c")

        @pl.loop(0, per_w)
        def _(i):
            c = wid * per_w + i
            pltpu.sync_copy(pos_hbm.at[c], idx_v)
            pltpu.sync_copy(rows_hbm.at[pl.ds(pl.multiple_of(row0 + c * w, w), w)], rows_v)
            copies = [pltpu.async_copy(rows_v, out_hbm.at[idx_v.at[k]], sem) for k in range(nk)]
            for cp in copies:
                cp.wait()

        pltpu.sync_copy(idx_v, done_hbm.at[wid])

    return scatter(rows, pos3)


def _padfill_body(cnt_ref, pst_ref, pcn_ref, xs_in, xs_out, zbuf, zsem):
    del xs_in
    zbuf[...] = jnp.zeros_like(zbuf)

    def pad_runs(e, act):
        pad = pcn_ref[e] - cnt_ref[e]
        base = pst_ref[e] + cnt_ref[e]
        for b in range(ROW_BLOCK.bit_length() - 1):
            n = 1 << b

            @pl.when(((pad >> b) & 1) == 1)
            def _():
                off = base + (pad & (n - 1))
                act(pltpu.make_async_copy(zbuf.at[pl.ds(0, QUAD * n)],
                                          xs_out.at[pl.ds(QUAD * off, QUAD * n)], zsem))

    def start_all(e, c):
        pad_runs(e, lambda d: d.start())
        return c

    def wait_all(e, c):
        pad_runs(e, lambda d: d.wait())
        return c

    lax.fori_loop(0, N_EXPERTS, start_all, 0)
    lax.fori_loop(0, N_EXPERTS, wait_all, 0)


def _padfill(counts, pstarts, pcounts, xs):
    grid_spec = pltpu.PrefetchScalarGridSpec(
        num_scalar_prefetch=3,
        grid=(1,),
        in_specs=[pl.BlockSpec(memory_space=pl.ANY)],
        out_specs=pl.BlockSpec(memory_space=pl.ANY),
        scratch_shapes=[pltpu.VMEM((QUAD * ROW_BLOCK // 2, 128), jnp.uint32), pltpu.SemaphoreType.DMA],
    )
    return pl.pallas_call(
        _padfill_body,
        grid_spec=grid_spec,
        out_shape=jax.ShapeDtypeStruct(xs.shape, xs.dtype),
        input_output_aliases={3: 0},
        compiler_params=_cparams("arbitrary"),
        name="padfill",
    )(counts, pstarts, pcounts, xs)


def _combine_stream_body(gate_ref, h1_ref, yg_ref, wsg_ref, wsu_ref, wsd_ref, g_ref, b_ref, out_ref, *, tt):
    x = h1_ref[...]
    xb = x.astype(_BF16)
    shared = _mm(_silu(_mm(xb, wsg_ref[...])) * _mm(xb, wsu_ref[...]), wsd_ref[...])
    gcol = gate_ref[...].T
    acc_lo = jnp.zeros((tt, HALF), _F32)
    acc_hi = jnp.zeros((tt, HALF), _F32)
    for k in range(TOP_K):
        lo, hi = _unpack_halves(_load_rows(yg_ref.at[0, k], tt))
        acc_lo = acc_lo + gcol[:, k:k + 1] * lo
        acc_hi = acc_hi + gcol[:, k:k + 1] * hi
    routed = jnp.concatenate([acc_lo, acc_hi], axis=1)
    out_ref[...] = _layer_norm(DN_ALPHA * x + (routed + shared), g_ref[...], b_ref[...])


def _combine_stream(gate, h1, yg, wsg, wsu, wsd, g, b, out_prev, *, tt, tile0, t_all):
    t = gate.shape[1]
    full = lambda a: pl.BlockSpec(a.shape, lambda i: (0,) * a.ndim)
    in_specs = [pl.BlockSpec((TOP_K, tt), lambda i: (0, i)), pl.BlockSpec((tt, D_MODEL), lambda i: (i, 0)),
                pl.BlockSpec((1, TOP_K, tt * QUAD, 128), lambda i: (i, 0, 0, 0)),
                full(wsg), full(wsu), full(wsd), full(g), full(b)]
    args = [gate, h1, yg, wsg, wsu, wsd, g, b]
    aliases = {}
    body = functools.partial(_combine_stream_body, tt=tt)
    if out_prev is not None:
        in_specs.append(pl.BlockSpec(memory_space=pl.ANY))
        args.append(out_prev)
        aliases = {len(args) - 1: 0}
        body = lambda *refs: _combine_stream_body(*refs[:8], refs[9], tt=tt)
    return pl.pallas_call(
        body,
        grid=(t // tt,),
        in_specs=in_specs,
        out_specs=pl.BlockSpec((tt, D_MODEL), lambda i: (i + tile0, 0)),
        out_shape=jax.ShapeDtypeStruct((t_all, D_MODEL), _F32),
        input_output_aliases=aliases,
        compiler_params=_cparams("arbitrary"),
        name="combine",
    )(*args)


def _pick(n, pref):
    t = min(n, pref)
    while n % t:
        t -= CHUNK
    return t


def _delta(pre, s0, gnw, *, lg, nbb):
    yc, q, k, v, z, bgc, bgr, tails_out = pre
    bsz, seq, _ = q.shape
    nch = seq // CHUNK
    grow = bgr[:, GDN_HEADS:2 * GDN_HEADS, :].reshape(bsz, GDN_HEADS, nch, CHUNK)
    grow = grow.transpose(0, 2, 1, 3).reshape(bsz, nch, 1, STACK)
    yg, s_out = _gdn(q, k, v, z, bgc, grow, s0, gnw, lg=lg, nbb=nbb)
    return yc, yg, tails_out, s_out


def kernel(x, meta_tokens, w_in, conv_w, conv_norm_w, gdn_conv_w, a_log, dt_bias, gdn_norm_w, w_out,
           ln1_g, ln1_b, w_router, b_router, w_gate, w_up, w_down, ws_gate, ws_up, ws_down, ln2_g, ln2_b):
    assert w_in.shape[0] == 1, "single-layer stack"
    bsz, seq, d = x.shape
    assert d == D_MODEL and seq % CHUNK == 0
    c, gw = CONV_WIDTH, GDN_WIDTH
    win = w_in[0].astype(_BF16)
    wbd = win[:, 3 * c + 4 * gw:]
    zpad = jnp.zeros((128 - 2 * GDN_HEADS,), _F32)
    zpad4 = jnp.zeros((GDN_HEADS,), _F32)
    prow = jnp.zeros((8, 128), _F32)
    prow = prow.at[0].set(jnp.concatenate([zpad4, a_log[0], zpad]))
    prow = prow.at[1].set(jnp.concatenate([zpad4, dt_bias[0], zpad]))
    wts = (win[:, :3 * c], win[:, 3 * c:3 * c + 3 * gw], win[:, 3 * c + 3 * gw:3 * c + 4 * gw],
           jnp.pad(wbd, ((0, 0), (0, 128 - 2 * GDN_HEADS))), wbd.T,
           conv_w[0], conv_norm_w, gdn_conv_w[0], prow, prow.T[:8])
    gnw = gdn_norm_w

    meta = jnp.concatenate([jnp.zeros((CHUNK - N_META, d), x.dtype), meta_tokens.astype(x.dtype)])[None]
    tails0 = jnp.zeros((HIST, c + 3 * gw), _F32)
    s00 = jnp.zeros((GDN_HEADS, GDN_HEAD_DIM, GDN_HEAD_DIM), _F32)
    _, _, tails_m, s_m = _delta(_premix(meta, tails0, wts, lt=CHUNK, b0=0, bsz=1), s00, gnw, lg=CHUNK, nbb=1)

    t = bsz * seq
    tm = _pick(seq, 512)
    tt = _pick(seq, 256)
    wo = w_out[0].astype(_BF16)
    wr_t = w_router[0].T
    wr_hi = wr_t.astype(_BF16)
    wr_lo = (wr_t - wr_hi.astype(_F32)).astype(_BF16)
    shared_w = (ws_gate[0].astype(_BF16), ws_up[0].astype(_BF16), ws_down[0].astype(_BF16))
    x2d = x.reshape(t, d)

    parts = LAYER_PARTS if bsz % LAYER_PARTS == 0 and (bsz // LAYER_PARTS * seq) % (tt * SC_WORKERS) == 0 else 1
    bp = bsz // parts
    tp = bp * seq
    nb = tp * TOP_K // ROW_BLOCK + N_EXPERTS
    nbb = GDN_ROWS if bp % GDN_ROWS == 0 else 1

    def premix(part):
        return _premix(x, tails_m[0], wts, lt=_pick(seq, 512), b0=part * bp, bsz=bp)

    def delta(part, pre):
        yc, yg, _, _ = _delta(pre, s_m[0], gnw, lg=_pick(seq, 512), nbb=nbb)
        return _outproj(yc.reshape(tp, c), yg.reshape(tp, gw), x2d, wo, ln1_g, ln1_b, tm=tm, tile0=part * (tp // tm))

    def route(h1, h1p, after):
        idx, gate, rank, cnt = _router(h1, wr_hi, wr_lo, b_router[0][:, None], tt=tt, tile0=0, t=tp)
        counts = cnt[:, 0].astype(_I32)
        pcounts = (counts + ROW_BLOCK - 1) // ROW_BLOCK * ROW_BLOCK
        pends = jnp.cumsum(pcounts)
        pstarts = (pends - pcounts).astype(_I32)
        tiled = lambda a: a.reshape(TOP_K, tp // tt, tt).transpose(1, 0, 2).reshape(tp * TOP_K)
        pos = _sc_position(tiled(idx), tiled(rank), pstarts, after)
        pos = pos.reshape(tp // tt, TOP_K, tt)
        nwin = tt // SC_WINDOW
        pos3 = pos.reshape(tp // tt, TOP_K, nwin, SC_WINDOW).transpose(0, 2, 1, 3)
        pos3 = pos3.reshape(tp // SC_WINDOW, TOP_K, SC_WINDOW)
        xs, done = _sc_scatter(h1p.reshape(tp, QUAD, 128), pos3, nb * ROW_BLOCK, 0)
        xs = _padfill(counts, pstarts, pcounts.astype(_I32), xs.reshape(nb * ROW_BLOCK * QUAD, 128))
        blocks = ((pstarts // ROW_BLOCK).astype(_I32), (pcounts // ROW_BLOCK).astype(_I32),
                  (pends[-1:] // ROW_BLOCK).astype(_I32))
        return xs, blocks, pos, gate, done

    def experts(xs, blocks, pos):
        ys = _ffn(*blocks, xs, w_gate[0], w_up[0], w_down[0])
        yg = _sc_gather(ys.reshape(nb * ROW_BLOCK, QUAD, 128), pos.reshape(tp * TOP_K))
        return yg.reshape(tp // tt, TOP_K, tt * QUAD, 128)

    def combine(part, gate, h1, yg, out):
        return _combine_stream(gate, h1, yg, *shared_w, ln2_g, ln2_b, out, tt=tt, tile0=part * (tp // tt), t_all=t)

    out = None
    done = jnp.zeros((SC_WORKERS, TOP_K, SC_WINDOW), _I32)
    h1, h1p = delta(0, premix(0))
    routed = route(h1, h1p, done)
    for part in range(parts):
        xs, blocks, pos, gate, done = routed
        h1_cur = h1
        last = part + 1 == parts
        pre = None if last else premix(part + 1)
        yg = experts(xs, blocks, pos)
        if not last:
            h1, h1p = delta(part + 1, pre)
            routed = route(h1, h1p, done)
        out = combine(part, gate, h1_cur, yg, out)
    return out.reshape(bsz, seq, d)
```

```python
import functools

import jax
import jax.numpy as jnp
from jax import lax
from jax.experimental import pallas as pl
from jax.experimental.pallas import tpu as pltpu
from jax.experimental.pallas import tpu_sc as plsc

_F32 = jnp.float32
_BF16 = jnp.bfloat16
_I32 = jnp.int32

D_MODEL = 1024
N_META = 16
CONV_WIDTH = 512
CONV_K = 3
GDN_HEADS = 4
GDN_HEAD_DIM = 128
GDN_WIDTH = GDN_HEADS * GDN_HEAD_DIM
GDN_CONV_K = 4
CHUNK = 64
N_EXPERTS = 256
TOP_K = 8
N_GROUPS = 8
TOPK_GROUPS = 4
E_PER_GROUP = N_EXPERTS // N_GROUPS
EXPERT_FF = 256
ROUTED_SCALE = 2.5
ROW_BLOCK = 256
DN_ALPHA = 2.0 ** 0.25
NORM_EPS = 1e-5
HALF = D_MODEL // 2
QUAD = HALF // 128
STACK = GDN_HEADS * CHUNK
HIST = 8
GDN_ROWS = 4
PREMIX_SUB = 2
SC_CORES = 2
SC_SUBCORES = 16
SC_WORKERS = SC_CORES * SC_SUBCORES
SC_LANES = 16
SC_CHUNK = 64
SC_RING = 2
SC_WINDOW = 128
LAYER_PARTS = 2
RING = 8
IN_AHEAD = RING - 2

V7X_VMEM_BYTES = 64 * 1024 * 1024
VMEM_LIMIT = V7X_VMEM_BYTES - 8 * 1024 * 1024


def _cparams(*sem):
    return pltpu.CompilerParams(dimension_semantics=sem, vmem_limit_bytes=VMEM_LIMIT)


def _mm(a, b):
    return jnp.dot(a.astype(_BF16), b.astype(_BF16), preferred_element_type=_F32)


def _mm_nt(a, b):
    return lax.dot_general(a.astype(_BF16), b.astype(_BF16), (((1,), (1,)), ((), ())),
                           preferred_element_type=_F32)


def _mm_tn(a, b):
    return lax.dot_general(a.astype(_BF16), b.astype(_BF16), (((0,), (0,)), ((), ())),
                           preferred_element_type=_F32)


def _sigmoid(x):
    return 1.0 / (1.0 + jnp.exp(-x))


def _silu(x):
    return x * _sigmoid(x)


def _softplus(x):
    return jnp.maximum(x, 0.0) + jnp.log1p(jnp.exp(-jnp.abs(x)))


def _pack_halves(y):
    return pltpu.pack_elementwise([y[:, :HALF], y[:, HALF:]], packed_dtype=_BF16)


def _store_rows(ref, packed):
    r = packed.shape[0]
    for c in range(QUAD):
        ref[pl.ds(c, r, stride=QUAD), :] = packed[:, c * 128:(c + 1) * 128]


def _load_rows(ref, r):
    return jnp.concatenate([ref[pl.ds(c, r, stride=QUAD), :] for c in range(QUAD)], axis=1)


def _unpack_halves(p):
    lo = pltpu.unpack_elementwise(p, index=0, packed_dtype=_BF16, unpacked_dtype=_F32)
    hi = pltpu.unpack_elementwise(p, index=1, packed_dtype=_BF16, unpacked_dtype=_F32)
    return lo, hi


def _layer_norm(h, g, b):
    mu = jnp.mean(h, axis=-1, keepdims=True)
    d = h - mu
    var = jnp.mean(d * d, axis=-1, keepdims=True)
    return d * lax.rsqrt(var + NORM_EPS) * g + b


def _premix_body(x_ref, tails_ref, wa_ref, wq_ref, wz_ref, wbd_ref, wbdt_ref, cw_ref, cnw_ref,
                 gcw_ref, prow_ref, pcol_ref,
                 yc_ref, q_ref, k_ref, v_ref, z_ref, bgc_ref, bgr_ref, tout_ref, ext_ref, *, lt):
    cw_ = CONV_WIDTH

    @pl.when(pl.program_id(1) == 0)
    def _():
        ext_ref[0:HIST, :] = tails_ref[...]

    cw = cw_ref[...]
    gcw = gcw_ref[...]
    prow = prow_ref[...]
    pcol = pcol_ref[...]

    def sub_tile(r0, n):
        rows = slice(r0, r0 + n)
        erows = slice(HIST + r0, HIST + r0 + n)
        xb = x_ref[0, rows, :].astype(_BF16)
        pa = jnp.dot(xb, wa_ref[...], preferred_element_type=_F32)
        yield
        gate_b = pa[:, 0:cw_]
        u = pa[:, cw_:2 * cw_] * pa[:, 2 * cw_:3 * cw_]
        ext_ref[erows, 0:cw_] = u
        pq = jnp.dot(xb, wq_ref[...], preferred_element_type=_F32)
        yield
        ext_ref[erows, cw_:] = pq
        zz = jnp.dot(xb, wz_ref[...], preferred_element_type=_F32)
        bdc = jnp.dot(xb, wbd_ref[...], preferred_element_type=_F32)
        bdr = _mm_nt(wbdt_ref[...], xb)
        yield

        ca = u * cw[CONV_K - 1:CONV_K, :]
        for j in range(CONV_K - 1):
            ca = ca + ext_ref[pl.ds(HIST + r0 - (CONV_K - 1) + j, n), 0:cw_] * cw[j:j + 1, :]
        yc = gate_b * ca
        ms = jnp.mean(yc * yc, axis=-1, keepdims=True)
        yc_ref[0, rows, :] = (yc * lax.rsqrt(ms + NORM_EPS) * cnw_ref[...]).astype(_BF16)

        cq = pq * gcw[GDN_CONV_K - 1:GDN_CONV_K, :]
        for j in range(GDN_CONV_K - 1):
            cq = cq + ext_ref[pl.ds(HIST + r0 - (GDN_CONV_K - 1) + j, n), cw_:] * gcw[j:j + 1, :]
        s = _silu(cq)
        for h in range(GDN_HEADS):
            lo, hi = h * GDN_HEAD_DIM, (h + 1) * GDN_HEAD_DIM
            qh = s[:, lo:hi]
            kh = s[:, GDN_WIDTH + lo:GDN_WIDTH + hi]
            qn = qh * lax.rsqrt(jnp.sum(qh * qh, axis=-1, keepdims=True) + 1e-6)
            kn = kh * lax.rsqrt(jnp.sum(kh * kh, axis=-1, keepdims=True) + 1e-6)
            q_ref[0, rows, lo:hi] = (qn * (GDN_HEAD_DIM ** -0.5)).astype(_BF16)
            k_ref[0, rows, lo:hi] = kn.astype(_BF16)
        v_ref[0, rows, :] = s[:, 2 * GDN_WIDTH:].astype(_BF16)
        z_ref[0, rows, :] = zz.astype(_BF16)

        g_c = -jnp.exp(prow[0:1, :]) * _softplus(bdc + prow[1:2, :])
        lane = lax.broadcasted_iota(_I32, bdc.shape, 1)
        bgc_ref[0, rows, :] = jnp.where(lane < GDN_HEADS, _sigmoid(bdc), g_c)
        g_r = -jnp.exp(pcol[:, 0:1]) * _softplus(bdr + pcol[:, 1:2])
        row = lax.broadcasted_iota(_I32, bdr.shape, 0)
        bgr_ref[0, :, rows] = jnp.where(row < GDN_HEADS, _sigmoid(bdr), g_r)

    n_sub = PREMIX_SUB if lt % (PREMIX_SUB * 128) == 0 else 1
    live = [sub_tile(i * (lt // n_sub), lt // n_sub) for i in range(n_sub)]
    while live:
        live = [g for g in live if next(g, live) is not live]

    tail = ext_ref[lt:lt + HIST, :]
    ext_ref[0:HIST, :] = tail
    tout_ref[0] = tail


def _premix(x, tails, wts, *, lt, b0, bsz):
    _, seq, d = x.shape
    assert seq % lt == 0
    grid = (bsz, seq // lt)
    full = lambda a: pl.BlockSpec(a.shape, lambda b, j: (0,) * a.ndim)
    tok = lambda w: pl.BlockSpec((1, lt, w), lambda b, j: (b, j, 0))
    x_spec = pl.BlockSpec((1, lt, d), lambda b, j: (b + b0, j, 0))
    (wa, wq, wz, wbd, wbdt, cw, cnw, gcw, prow, pcol) = wts
    ext_w = CONV_WIDTH + 3 * GDN_WIDTH
    out_shape = (
        jax.ShapeDtypeStruct((bsz, seq, CONV_WIDTH), _BF16),
        jax.ShapeDtypeStruct((bsz, seq, GDN_WIDTH), _BF16),
        jax.ShapeDtypeStruct((bsz, seq, GDN_WIDTH), _BF16),
        jax.ShapeDtypeStruct((bsz, seq, GDN_WIDTH), _BF16),
        jax.ShapeDtypeStruct((bsz, seq, GDN_WIDTH), _BF16),
        jax.ShapeDtypeStruct((bsz, seq, 128), _F32),
        jax.ShapeDtypeStruct((bsz, 8, seq), _F32),
        jax.ShapeDtypeStruct((bsz, HIST, ext_w), _F32),
    )
    out_specs = (tok(CONV_WIDTH), tok(GDN_WIDTH), tok(GDN_WIDTH), tok(GDN_WIDTH), tok(GDN_WIDTH),
                 tok(128), pl.BlockSpec((1, 8, lt), lambda b, j: (b, 0, j)),
                 pl.BlockSpec((1, HIST, ext_w), lambda b, j: (b, 0, 0)))
    return pl.pallas_call(
        functools.partial(_premix_body, lt=lt),
        grid=grid,
        in_specs=[x_spec, full(tails)] + [full(w) for w in wts],
        out_specs=out_specs,
        out_shape=out_shape,
        scratch_shapes=[pltpu.VMEM((HIST + lt, ext_w), _F32)],
        compiler_params=_cparams("arbitrary", "arbitrary"),
        name="premix",
    )(x, tails, *wts)


def _cumsum_rows(x):
    row = lax.broadcasted_iota(_I32, x.shape, 0)
    s = 1
    while s < x.shape[0]:
        x = x + jnp.where(row >= s, pltpu.roll(x, s, 0), 0.0)
        s *= 2
    return x


def _cumsum_lanes_seg(x):
    lane = lax.broadcasted_iota(_I32, x.shape, 1) & (CHUNK - 1)
    s = 1
    while s < CHUNK:
        x = x + jnp.where(lane >= s, pltpu.roll(x, s, 1), 0.0)
        s *= 2
    return x


def _stack_heads(a):
    return jnp.concatenate([a[:, h * GDN_HEAD_DIM:(h + 1) * GDN_HEAD_DIM] for h in range(GDN_HEADS)], axis=0)


def _gdn_body(q_ref, k_ref, v_ref, z_ref, bgc_ref, grow_ref, s0_ref, gnw_ref, after_ref,
              y_ref, sout_ref, s_ref, *, nc, nbb):
    del after_ref

    @pl.when(pl.program_id(1) == 0)
    def _():
        for r in range(nbb):
            s_ref[r] = s0_ref[...]

    ri = lax.broadcasted_iota(_I32, (STACK, STACK), 0)
    ci = lax.broadcasted_iota(_I32, (STACK, STACK), 1)
    same64 = (ri >> 6) == (ci >> 6)
    same32 = (ri >> 5) == (ci >> 5)
    same16 = (ri >> 4) == (ci >> 4)
    low_incl = same64 & (ri >= ci)
    low_strict = same64 & (ri > ci)
    gnw = gnw_ref[...]

    def chunk_row(r, c):
        off = pl.multiple_of(c * CHUNK, CHUNK)
        q_all = _stack_heads(q_ref[r, pl.ds(off, CHUNK), :].astype(_F32))
        k_all = _stack_heads(k_ref[r, pl.ds(off, CHUNK), :].astype(_F32))
        v_all = _stack_heads(v_ref[r, pl.ds(off, CHUNK), :].astype(_F32))
        bgc = bgc_ref[r, pl.ds(off, CHUNK), :]
        gcs = _cumsum_rows(bgc)
        hd = (CHUNK, GDN_HEAD_DIM)
        beta_b = jnp.concatenate(
            [jnp.broadcast_to(bgc[:, h:h + 1], hd) for h in range(GDN_HEADS)], axis=0)
        gc_b = jnp.concatenate(
            [jnp.broadcast_to(gcs[:, GDN_HEADS + h:GDN_HEADS + h + 1], hd) for h in range(GDN_HEADS)], axis=0)
        gl = [gcs[CHUNK - 1:CHUNK, GDN_HEADS + h:GDN_HEADS + h + 1] for h in range(GDN_HEADS)]
        gl_b = jnp.concatenate([jnp.broadcast_to(g1, hd) for g1 in gl], axis=0)
        gcr = _cumsum_lanes_seg(jnp.broadcast_to(grow_ref[r, c], (8, STACK)))[0:1, :]

        diff = jnp.concatenate([gc_b, gc_b], axis=1) - gcr
        decay = jnp.exp(jnp.where(low_incl, diff, -1e30))
        kb = k_all * beta_b
        a1 = _mm_nt(jnp.concatenate([kb, q_all], axis=0), k_all)
        yield
        m = jnp.where(low_strict, a1[:STACK] * decay, 0.0)
        attn = a1[STACK:] * decay

        l16 = jnp.where(same16, m, 0.0)
        c1 = jnp.where(same32 & jnp.logical_not(same16), m, 0.0)
        c2 = jnp.where(same32, 0.0, m)
        p2 = _mm(l16, l16)
        yield
        p4 = _mm(p2, p2)
        t = _mm(l16, p2)
        yield
        na = p2 - l16 - t
        p8 = _mm(p4, p4)
        t = _mm(na, p4)
        yield
        nb = na + p4 + t
        t = _mm(nb, p8)
        yield
        ncm = nb + p8 + t
        t = _mm(c1, ncm)
        yield
        y1 = c1 + t
        t = _mm(ncm, y1)
        yield
        n1 = ncm - y1 - t
        t = _mm(c2, n1)
        yield
        y2 = c2 + t
        t = _mm(n1, y2)
        yield
        nt = n1 - y2 - t

        egc = jnp.exp(gc_b)
        rhs = jnp.concatenate([v_all * beta_b, kb * egc], axis=1)
        t = _mm(nt, rhs)
        yield
        uw = rhs + t
        u_all = uw[:, :GDN_HEAD_DIM]
        w_all = uw[:, GDN_HEAD_DIM:]
        qd = q_all * egc
        kd = k_all * jnp.exp(gl_b - gc_b)

        bs = []
        for h in range(GDN_HEADS):
            r0, r1 = h * CHUNK, (h + 1) * CHUNK
            bs.append(_mm(jnp.concatenate([w_all[r0:r1], qd[r0:r1]], axis=0), s_ref[r, h]))
        yield
        vn = [u_all[h * CHUNK:(h + 1) * CHUNK] - bs[h][:CHUNK] for h in range(GDN_HEADS)]
        vn_all = jnp.concatenate(vn, axis=0)
        t = _mm(attn, vn_all)
        ds = [_mm_tn(kd[h * CHUNK:(h + 1) * CHUNK], vn[h]) for h in range(GDN_HEADS)]
        yield
        o_all = jnp.concatenate([b[CHUNK:] for b in bs], axis=0) + t
        for h in range(GDN_HEADS):
            r0, r1 = h * CHUNK, (h + 1) * CHUNK
            s_ref[r, h] = s_ref[r, h] * jnp.exp(gl[h]) + ds[h]
            o = o_all[r0:r1]
            zz = z_ref[r, pl.ds(off, CHUNK), h * GDN_HEAD_DIM:(h + 1) * GDN_HEAD_DIM].astype(_F32)
            on = o * lax.rsqrt(jnp.mean(o * o, axis=-1, keepdims=True) + NORM_EPS) * gnw
            y_ref[r, pl.ds(off, CHUNK), h * GDN_HEAD_DIM:(h + 1) * GDN_HEAD_DIM] = (on * _silu(zz)).astype(_BF16)

    def chunk(c, carry):
        live = [chunk_row(r, c) for r in range(nbb)]
        while live:
            live = [g for g in live if next(g, live) is not live]
        return carry

    lax.fori_loop(0, nc, chunk, 0)
    sout_ref[...] = s_ref[...]


def _gdn(q, k, v, z, bgc, grow, s0, gnw, after, *, lg, nbb):
    bsz, seq, _ = q.shape
    assert seq % lg == 0 and lg % CHUNK == 0 and bsz % nbb == 0
    nc = lg // CHUNK
    tok = lambda w: pl.BlockSpec((nbb, lg, w), lambda b, j: (b, j, 0))
    full = lambda a: pl.BlockSpec(a.shape, lambda b, j: (0,) * a.ndim)
    st = (nbb, GDN_HEADS, GDN_HEAD_DIM, GDN_HEAD_DIM)
    return pl.pallas_call(
        functools.partial(_gdn_body, nc=nc, nbb=nbb),
        grid=(bsz // nbb, seq // lg),
        in_specs=[tok(GDN_WIDTH)] * 4 + [tok(128), pl.BlockSpec((nbb, nc, 1, STACK), lambda b, j: (b, j, 0, 0)),
                                           full(s0), full(gnw), pl.BlockSpec(memory_space=pl.ANY)],
        out_specs=(tok(GDN_WIDTH), pl.BlockSpec(st, lambda b, j: (b, 0, 0, 0))),
        out_shape=(jax.ShapeDtypeStruct((bsz, seq, GDN_WIDTH), _BF16),
                   jax.ShapeDtypeStruct((bsz, GDN_HEADS, GDN_HEAD_DIM, GDN_HEAD_DIM), _F32)),
        scratch_shapes=[pltpu.VMEM(st, _F32)],
        compiler_params=_cparams("arbitrary", "arbitrary"),
        name="gdn",
    )(q, k, v, z, bgc, grow, s0, gnw, after)


def _outproj_body(yc_ref, yg_ref, x_ref, wo_ref, g_ref, b_ref, h1_ref, h1p_ref):
    mix = (jnp.dot(yc_ref[...], wo_ref[0:CONV_WIDTH, :], preferred_element_type=_F32)
           + jnp.dot(yg_ref[...], wo_ref[CONV_WIDTH:, :], preferred_element_type=_F32))
    h1 = _layer_norm(DN_ALPHA * x_ref[...] + mix, g_ref[...], b_ref[...])
    h1_ref[...] = h1
    _store_rows(h1p_ref, _pack_halves(h1))


def _outproj(yc, yg, x2d, wo, g, b, *, tm, tile0):
    t = yc.shape[0]
    assert t % tm == 0
    row = lambda w: pl.BlockSpec((tm, w), lambda i: (i, 0))
    full = lambda a: pl.BlockSpec(a.shape, lambda i: (0,) * a.ndim)
    return pl.pallas_call(
        _outproj_body,
        grid=(t // tm,),
        in_specs=[row(CONV_WIDTH), row(GDN_WIDTH), pl.BlockSpec((tm, D_MODEL), lambda i: (i + tile0, 0)),
                  full(wo), full(g), full(b)],
        out_specs=(row(D_MODEL), pl.BlockSpec((tm * QUAD, 128), lambda i: (i, 0))),
        out_shape=(jax.ShapeDtypeStruct((t, D_MODEL), _F32), jax.ShapeDtypeStruct((t * QUAD, 128), jnp.uint32)),
        compiler_params=_cparams("arbitrary"),
        name="outproj",
    )(yc, yg, x2d, wo, g, b)


def _router_body(h1_ref, wh_ref, wl_ref, br_ref, idx_ref, gate_ref, rank_ref, cnt_ref, carry_ref, *, tt):
    @pl.when(pl.program_id(0) == 0)
    def _():
        carry_ref[...] = jnp.zeros_like(carry_ref)

    x = h1_ref[...]
    xh = x.astype(_BF16)
    xl = (x - xh.astype(_F32)).astype(_BF16)
    wh = wh_ref[...]
    logits = _mm_nt(wh, xh) + _mm_nt(wh, xl) + _mm_nt(wl_ref[...], xh)
    scores = _sigmoid(logits)
    sel = scores + br_ref[...]
    ninf = -jnp.inf

    r32 = lax.broadcasted_iota(_I32, (E_PER_GROUP, tt), 0)
    gsc = []
    for g in range(N_GROUPS):
        xg = sel[g * E_PER_GROUP:(g + 1) * E_PER_GROUP]
        m1 = jnp.max(xg, axis=0, keepdims=True)
        i1 = jnp.min(jnp.where(xg == m1, r32, E_PER_GROUP), axis=0, keepdims=True)
        m2 = jnp.max(jnp.where(r32 == i1, ninf, xg), axis=0, keepdims=True)
        gsc.append(m1 + m2)
    work = jnp.concatenate(gsc, axis=0)
    r8 = lax.broadcasted_iota(_I32, (N_GROUPS, tt), 0)
    gkeep = jnp.zeros((N_GROUPS, tt), _F32)
    for _ in range(TOPK_GROUPS):
        m = jnp.max(work, axis=0, keepdims=True)
        gi = jnp.min(jnp.where(work == m, r8, N_GROUPS), axis=0, keepdims=True)
        pick = r8 == gi
        gkeep = jnp.where(pick, 1.0, gkeep)
        work = jnp.where(pick, ninf, work)
    selm = jnp.concatenate(
        [jnp.where(gkeep[g:g + 1] > 0.5, sel[g * E_PER_GROUP:(g + 1) * E_PER_GROUP], ninf)
         for g in range(N_GROUPS)], axis=0)

    re = lax.broadcasted_iota(_I32, (N_EXPERTS, tt), 0)
    msel = jnp.zeros((N_EXPERTS, tt), _F32)
    idxs, gates = [], []
    for _ in range(TOP_K):
        m = jnp.max(selm, axis=0, keepdims=True)
        ii = jnp.min(jnp.where(selm == m, re, N_EXPERTS), axis=0, keepdims=True)
        hit = re == ii
        idxs.append(ii)
        gates.append(jnp.sum(jnp.where(hit, scores, 0.0), axis=0, keepdims=True))
        selm = jnp.where(hit, ninf, selm)
        msel = jnp.where(hit, 1.0, msel)
    gate = jnp.concatenate(gates, axis=0)
    gate_ref[...] = gate / jnp.sum(gate, axis=0, keepdims=True) * ROUTED_SCALE
    idx_ref[...] = jnp.concatenate(idxs, axis=0)

    ta = lax.broadcasted_iota(_I32, (tt, tt), 0)
    tb = lax.broadcasted_iota(_I32, (tt, tt), 1)
    earlier = jnp.where(ta < tb, 1.0, 0.0)
    carry = carry_ref[...]
    rank_all = _mm(msel, earlier) + carry[:, 0:1]
    rank_ref[...] = jnp.concatenate(
        [jnp.sum(jnp.where(re == ii, rank_all, 0.0), axis=0, keepdims=True) for ii in idxs],
        axis=0).astype(_I32)
    carry = carry + jnp.sum(msel, axis=1, keepdims=True)
    carry_ref[...] = carry
    cnt_ref[...] = carry


def _router(h1, wh, wl, br, *, tt, tile0, t):
    assert t % tt == 0
    full = lambda a: pl.BlockSpec(a.shape, lambda i: (0,) * a.ndim)
    kt = pl.BlockSpec((TOP_K, tt), lambda i: (0, i))
    return pl.pallas_call(
        functools.partial(_router_body, tt=tt),
        grid=(t // tt,),
        in_specs=[pl.BlockSpec((tt, D_MODEL), lambda i: (i + tile0, 0)), full(wh), full(wl), full(br)],
        out_specs=(kt, kt, kt, pl.BlockSpec((N_EXPERTS, 128), lambda i: (0, 0))),
        out_shape=(jax.ShapeDtypeStruct((TOP_K, t), _I32), jax.ShapeDtypeStruct((TOP_K, t), _F32),
                   jax.ShapeDtypeStruct((TOP_K, t), _I32), jax.ShapeDtypeStruct((N_EXPERTS, 128), _F32)),
        scratch_shapes=[pltpu.VMEM((N_EXPERTS, 128), _F32)],
        compiler_params=_cparams("arbitrary"),
        name="router",
    )(h1, wh, wl, br)


def _position_body(idx_ref, rank_ref, pstart_ref, pos_ref, *, tt):
    re = lax.broadcasted_iota(_I32, (N_EXPERTS, tt), 0)
    ps = pstart_ref[...]
    idx = idx_ref[...]
    rows = [jnp.sum(jnp.where(re == idx[k:k + 1], ps, 0), axis=0, keepdims=True) for k in range(TOP_K)]
    pos_ref[0] = jnp.concatenate(rows, axis=0) + rank_ref[...]


def _position(idx, rank, pstart, *, tt):
    t = idx.shape[1]
    kt = pl.BlockSpec((TOP_K, tt), lambda i: (0, i))
    return pl.pallas_call(
        functools.partial(_position_body, tt=tt),
        grid=(t // tt,),
        in_specs=[kt, kt, pl.BlockSpec(pstart.shape, lambda i: (0, 0))],
        out_specs=pl.BlockSpec((1, TOP_K, tt), lambda i: (i, 0, 0)),
        out_shape=jax.ShapeDtypeStruct((t // tt, TOP_K, tt), _I32),
        compiler_params=_cparams("arbitrary"),
        name="position",
    )(idx, rank, pstart)


def _ffn_body(blk0_ref, nblk_ref, ntot_ref, xs_hbm, wg_ref, wu_ref, wd_ref, ys_hbm,
              xbuf, ybuf, sem_in, sem_out, wgu_bf, wd_bf):
    e = pl.program_id(0)
    nblk = nblk_ref[e]
    blk0 = blk0_ref[e]
    ntot = ntot_ref[0]

    blk_rows = ROW_BLOCK * QUAD

    def rows(g):
        return pl.ds(pl.multiple_of(g * blk_rows, blk_rows), blk_rows)

    def in_start(g, slot):
        pltpu.make_async_copy(xs_hbm.at[rows(g)], xbuf.at[slot], sem_in.at[slot]).start()

    def in_wait(slot):
        pltpu.make_async_copy(xs_hbm.at[rows(0)], xbuf.at[slot], sem_in.at[slot]).wait()

    def out_start(g, slot):
        pltpu.make_async_copy(ybuf.at[slot], ys_hbm.at[rows(g)], sem_out.at[slot]).start()

    def out_wait(slot):
        pltpu.make_async_copy(ybuf.at[slot], ys_hbm.at[rows(0)], sem_out.at[slot]).wait()

    @pl.when(e == 0)
    def _():
        for i in range(IN_AHEAD):
            @pl.when(i < ntot)
            def _():
                in_start(i, i)

    @pl.when(nblk > 0)
    def _():
        wgu_bf[:, 0:EXPERT_FF] = wg_ref[0].astype(_BF16)
        wgu_bf[:, EXPERT_FF:] = wu_ref[0].astype(_BF16)
        wd_bf[...] = wd_ref[0].astype(_BF16)

        def acquire(g):
            slot = g & (RING - 1)
            in_wait(slot)

            @pl.when(g + IN_AHEAD < ntot)
            def _():
                in_start(g + IN_AHEAD, (g + IN_AHEAD) & (RING - 1))

            @pl.when(g >= RING)
            def _():
                out_wait(slot)

            return slot

        def compute(slot):
            lo, hi = _unpack_halves(_load_rows(xbuf.at[slot], ROW_BLOCK))
            a = jnp.dot(lo.astype(_BF16), wgu_bf[0:HALF, :], preferred_element_type=_F32)
            yield
            gu = a + jnp.dot(hi.astype(_BF16), wgu_bf[HALF:, :], preferred_element_type=_F32)
            yield
            h = (_silu(gu[:, :EXPERT_FF]) * gu[:, EXPERT_FF:]).astype(_BF16)
            y = jnp.dot(h, wd_bf[...], preferred_element_type=_F32)
            yield
            _store_rows(ybuf.at[slot], _pack_halves(y))

        def run(gs):
            slots = [acquire(g) for g in gs]
            live = [compute(s) for s in slots]
            while live:
                live = [c for c in live if next(c, live) is not live]
            for g, s in zip(gs, slots):
                out_start(g, s)

        def pair(j, carry):
            run([blk0 + 2 * j, blk0 + 2 * j + 1])
            return carry

        lax.fori_loop(0, nblk // 2, pair, 0)

        @pl.when((nblk & 1) == 1)
        def _():
            run([blk0 + nblk - 1])

    @pl.when(e == N_EXPERTS - 1)
    def _():
        for i in range(RING):
            @pl.when(i < ntot)
            def _():
                out_wait((ntot - 1 - i) & (RING - 1))


def _ffn(blk0, nblk, ntot, xs, wg, wu, wd):
    grid_spec = pltpu.PrefetchScalarGridSpec(
        num_scalar_prefetch=3,
        grid=(N_EXPERTS,),
        in_specs=[pl.BlockSpec(memory_space=pl.ANY),
                  pl.BlockSpec((1, D_MODEL, EXPERT_FF), lambda e, *_: (e, 0, 0)),
                  pl.BlockSpec((1, D_MODEL, EXPERT_FF), lambda e, *_: (e, 0, 0)),
                  pl.BlockSpec((1, EXPERT_FF, D_MODEL), lambda e, *_: (e, 0, 0))],
        out_specs=pl.BlockSpec(memory_space=pl.ANY),
        scratch_shapes=[pltpu.VMEM((RING, ROW_BLOCK * QUAD, 128), jnp.uint32),
                        pltpu.VMEM((RING, ROW_BLOCK * QUAD, 128), jnp.uint32),
                        pltpu.SemaphoreType.DMA((RING,)), pltpu.SemaphoreType.DMA((RING,)),
                        pltpu.VMEM((D_MODEL, 2 * EXPERT_FF), _BF16), pltpu.VMEM((EXPERT_FF, D_MODEL), _BF16)],
    )
    return pl.pallas_call(
        _ffn_body,
        grid_spec=grid_spec,
        out_shape=jax.ShapeDtypeStruct(xs.shape, jnp.uint32),
        compiler_params=_cparams("arbitrary"),
        name="ffn",
    )(blk0, nblk, ntot, xs, wg, wu, wd)


def _sc_position(idx, rank, pstart, after):
    n = idx.shape[0]
    per_w = n // SC_WORKERS
    assert per_w * SC_WORKERS == n and per_w % SC_LANES == 0
    mesh = plsc.VectorSubcoreMesh(core_axis_name="c", subcore_axis_name="s",
                                  num_cores=SC_CORES, num_subcores=SC_SUBCORES)

    @functools.partial(
        pl.kernel, mesh=mesh,
        out_type=jax.ShapeDtypeStruct((n,), _I32),
        scratch_types=[pltpu.VMEM((per_w,), _I32), pltpu.VMEM((per_w,), _I32), pltpu.VMEM((per_w,), _I32),
                       pltpu.VMEM((N_EXPERTS,), _I32)],
        compiler_params=pltpu.CompilerParams(needs_layout_passes=False),
        name="sc_position",
    )
    def position(idx_hbm, rank_hbm, ps_hbm, after_hbm, out_hbm, idx_v, rank_v, pos_v, ps_v):
        del after_hbm
        wid = lax.axis_index("s") * SC_CORES + lax.axis_index("c")
        mine = pl.ds(pl.multiple_of(wid * per_w, per_w), per_w)
        pltpu.sync_copy(ps_hbm, ps_v)
        pltpu.sync_copy(idx_hbm.at[mine], idx_v)
        pltpu.sync_copy(rank_hbm.at[mine], rank_v)

        @pl.loop(0, per_w, step=SC_LANES)
        def _(j):
            lanes = pl.ds(j, SC_LANES)
            pos_v[lanes] = plsc.load_gather(ps_v, [idx_v[lanes]]) + rank_v[lanes]

        pltpu.sync_copy(pos_v, out_hbm.at[mine])

    return position(idx, rank, pstart, after)


def _sc_gather(table, idx):
    b = idx.shape[0]
    nchunk = b // (SC_WORKERS * SC_CHUNK)
    assert nchunk * SC_WORKERS * SC_CHUNK == b and nchunk % SC_RING == 0
    idx2 = idx.reshape(SC_WORKERS * nchunk, SC_CHUNK)
    row = table.shape[1:]
    mesh = plsc.VectorSubcoreMesh(core_axis_name="c", subcore_axis_name="s",
                                  num_cores=SC_CORES, num_subcores=SC_SUBCORES)

    @functools.partial(
        pl.kernel, mesh=mesh,
        out_type=jax.ShapeDtypeStruct((b,) + row, table.dtype),
        scratch_types=[pltpu.VMEM((nchunk, SC_CHUNK), _I32), pltpu.VMEM((SC_RING, SC_CHUNK) + row, table.dtype),
                       pltpu.SemaphoreType.DMA((SC_RING,)), pltpu.SemaphoreType.DMA((SC_RING,))],
        name="sc_gather",
    )
    def gather(table_hbm, idx_hbm, out_hbm, idx_v, rows_v, sem_g, sem_w):
        wid = lax.axis_index("s") * SC_CORES + lax.axis_index("c")
        c0 = wid * nchunk
        pltpu.sync_copy(idx_hbm.at[pl.ds(pl.multiple_of(c0, nchunk), nchunk)], idx_v)

        def fetch(i, s):
            return pltpu.make_async_copy(table_hbm.at[idx_v.at[i]], rows_v.at[s], sem_g.at[s])

        def flush(i, s):
            rows = pl.ds(pl.multiple_of((c0 + i) * SC_CHUNK, SC_CHUNK), SC_CHUNK)
            return pltpu.make_async_copy(rows_v.at[s], out_hbm.at[rows], sem_w.at[s])

        for s in range(SC_RING):
            fetch(s, s).start()

        @pl.loop(0, nchunk, step=SC_RING)
        def _(g):
            for s in range(SC_RING):
                i = g + s
                fetch(i, s).wait()
                flush(i, s).start()
                flush(i, s).wait()

                @pl.when(i + SC_RING < nchunk)
                def _():
                    fetch(i + SC_RING, s).start()

    return gather(table, idx2)


def _sc_scatter(rows, pos3, n_out, row0):
    nchunk, nk, w = pos3.shape
    per_w = nchunk // SC_WORKERS
    assert per_w * SC_WORKERS == nchunk and w <= 128 and row0 % w == 0 and rows.shape[0] >= row0 + nchunk * w
    row = rows.shape[1:]
    mesh = plsc.VectorSubcoreMesh(core_axis_name="c", subcore_axis_name="s",
                                  num_cores=SC_CORES, num_subcores=SC_SUBCORES)

    @functools.partial(
        pl.kernel, mesh=mesh,
        out_type=(jax.ShapeDtypeStruct((n_out,) + row, rows.dtype),
                  jax.ShapeDtypeStruct((SC_WORKERS, nk, w), _I32)),
        scratch_types=[pltpu.VMEM((nk, w), _I32), pltpu.VMEM((w,) + row, rows.dtype), pltpu.SemaphoreType.DMA],
        name="sc_scatter",
    )
    def scatter(rows_hbm, pos_hbm, out_hbm, done_hbm, idx_v, rows_v, sem):
        wid = lax.axis_index("s") * SC_CORES + lax.axis_index("c")

        @pl.loop(0, per_w)
        def _(i):
            c = wid * per_w + i
            pltpu.sync_copy(pos_hbm.at[c], idx_v)
            pltpu.sync_copy(rows_hbm.at[pl.ds(pl.multiple_of(row0 + c * w, w), w)], rows_v)
            copies = [pltpu.async_copy(rows_v, out_hbm.at[idx_v.at[k]], sem) for k in range(nk)]
            for cp in copies:
                cp.wait()

        pltpu.sync_copy(idx_v, done_hbm.at[wid])

    return scatter(rows, pos3)


def _padfill_body(cnt_ref, pst_ref, pcn_ref, xs_in, xs_out, zbuf, zsem):
    del xs_in
    zbuf[...] = jnp.zeros_like(zbuf)

    def pad_runs(e, act):
        pad = pcn_ref[e] - cnt_ref[e]
        base = pst_ref[e] + cnt_ref[e]
        for b in range(ROW_BLOCK.bit_length() - 1):
            n = 1 << b

            @pl.when(((pad >> b) & 1) == 1)
            def _():
                off = base + (pad & (n - 1))
                act(pltpu.make_async_copy(zbuf.at[pl.ds(0, QUAD * n)],
                                          xs_out.at[pl.ds(QUAD * off, QUAD * n)], zsem))

    def start_all(e, c):
        pad_runs(e, lambda d: d.start())
        return c

    def wait_all(e, c):
        pad_runs(e, lambda d: d.wait())
        return c

    lax.fori_loop(0, N_EXPERTS, start_all, 0)
    lax.fori_loop(0, N_EXPERTS, wait_all, 0)


def _padfill(counts, pstarts, pcounts, xs):
    grid_spec = pltpu.PrefetchScalarGridSpec(
        num_scalar_prefetch=3,
        grid=(1,),
        in_specs=[pl.BlockSpec(memory_space=pl.ANY)],
        out_specs=pl.BlockSpec(memory_space=pl.ANY),
        scratch_shapes=[pltpu.VMEM((QUAD * ROW_BLOCK // 2, 128), jnp.uint32), pltpu.SemaphoreType.DMA],
    )
    return pl.pallas_call(
        _padfill_body,
        grid_spec=grid_spec,
        out_shape=jax.ShapeDtypeStruct(xs.shape, xs.dtype),
        input_output_aliases={3: 0},
        compiler_params=_cparams("arbitrary"),
        name="padfill",
    )(counts, pstarts, pcounts, xs)


def _combine_stream_body(gate_ref, h1_ref, yg_ref, wsg_ref, wsu_ref, wsd_ref, g_ref, b_ref, out_ref, *, tt):
    x = h1_ref[...]
    xb = x.astype(_BF16)
    shared = _mm(_silu(_mm(xb, wsg_ref[...])) * _mm(xb, wsu_ref[...]), wsd_ref[...])
    gcol = gate_ref[...].T
    acc_lo = jnp.zeros((tt, HALF), _F32)
    acc_hi = jnp.zeros((tt, HALF), _F32)
    for k in range(TOP_K):
        lo, hi = _unpack_halves(_load_rows(yg_ref.at[0, k], tt))
        acc_lo = acc_lo + gcol[:, k:k + 1] * lo
        acc_hi = acc_hi + gcol[:, k:k + 1] * hi
    routed = jnp.concatenate([acc_lo, acc_hi], axis=1)
    out_ref[...] = _layer_norm(DN_ALPHA * x + (routed + shared), g_ref[...], b_ref[...])


def _combine_stream(gate, h1, yg, wsg, wsu, wsd, g, b, out_prev, *, tt, tile0, t_all):
    t = gate.shape[1]
    full = lambda a: pl.BlockSpec(a.shape, lambda i: (0,) * a.ndim)
    in_specs = [pl.BlockSpec((TOP_K, tt), lambda i: (0, i)), pl.BlockSpec((tt, D_MODEL), lambda i: (i, 0)),
                pl.BlockSpec((1, TOP_K, tt * QUAD, 128), lambda i: (i, 0, 0, 0)),
                full(wsg), full(wsu), full(wsd), full(g), full(b)]
    args = [gate, h1, yg, wsg, wsu, wsd, g, b]
    aliases = {}
    body = functools.partial(_combine_stream_body, tt=tt)
    if out_prev is not None:
        in_specs.append(pl.BlockSpec(memory_space=pl.ANY))
        args.append(out_prev)
        aliases = {len(args) - 1: 0}
        body = lambda *refs: _combine_stream_body(*refs[:8], refs[9], tt=tt)
    return pl.pallas_call(
        body,
        grid=(t // tt,),
        in_specs=in_specs,
        out_specs=pl.BlockSpec((tt, D_MODEL), lambda i: (i + tile0, 0)),
        out_shape=jax.ShapeDtypeStruct((t_all, D_MODEL), _F32),
        input_output_aliases=aliases,
        compiler_params=_cparams("arbitrary"),
        name="combine",
    )(*args)


def _pick(n, pref):
    t = min(n, pref)
    while n % t:
        t -= CHUNK
    return t


def _delta(pre, s0, gnw, after, *, lg, nbb):
    yc, q, k, v, z, bgc, bgr, tails_out = pre
    bsz, seq, _ = q.shape
    nch = seq // CHUNK
    grow = bgr[:, GDN_HEADS:2 * GDN_HEADS, :].reshape(bsz, GDN_HEADS, nch, CHUNK)
    grow = grow.transpose(0, 2, 1, 3).reshape(bsz, nch, 1, STACK)
    yg, s_out = _gdn(q, k, v, z, bgc, grow, s0, gnw, after, lg=lg, nbb=nbb)
    return yc, yg, tails_out, s_out


def kernel(x, meta_tokens, w_in, conv_w, conv_norm_w, gdn_conv_w, a_log, dt_bias, gdn_norm_w, w_out,
           ln1_g, ln1_b, w_router, b_router, w_gate, w_up, w_down, ws_gate, ws_up, ws_down, ln2_g, ln2_b):
    assert w_in.shape[0] == 1, "single-layer stack"
    bsz, seq, d = x.shape
    assert d == D_MODEL and seq % CHUNK == 0
    c, gw = CONV_WIDTH, GDN_WIDTH
    win = w_in[0].astype(_BF16)
    wbd = win[:, 3 * c + 4 * gw:]
    zpad = jnp.zeros((128 - 2 * GDN_HEADS,), _F32)
    zpad4 = jnp.zeros((GDN_HEADS,), _F32)
    prow = jnp.zeros((8, 128), _F32)
    prow = prow.at[0].set(jnp.concatenate([zpad4, a_log[0], zpad]))
    prow = prow.at[1].set(jnp.concatenate([zpad4, dt_bias[0], zpad]))
    wts = (win[:, :3 * c], win[:, 3 * c:3 * c + 3 * gw], win[:, 3 * c + 3 * gw:3 * c + 4 * gw],
           jnp.pad(wbd, ((0, 0), (0, 128 - 2 * GDN_HEADS))), wbd.T,
           conv_w[0], conv_norm_w, gdn_conv_w[0], prow, prow.T[:8])
    gnw = gdn_norm_w

    meta = jnp.concatenate([jnp.zeros((CHUNK - N_META, d), x.dtype), meta_tokens.astype(x.dtype)])[None]
    tails0 = jnp.zeros((HIST, c + 3 * gw), _F32)
    s00 = jnp.zeros((GDN_HEADS, GDN_HEAD_DIM, GDN_HEAD_DIM), _F32)
    _, _, tails_m, s_m = _delta(_premix(meta, tails0, wts, lt=CHUNK, b0=0, bsz=1), s00, gnw, s00, lg=CHUNK, nbb=1)

    t = bsz * seq
    tm = _pick(seq, 512)
    tt = _pick(seq, 256)
    wo = w_out[0].astype(_BF16)
    wr_t = w_router[0].T
    wr_hi = wr_t.astype(_BF16)
    wr_lo = (wr_t - wr_hi.astype(_F32)).astype(_BF16)
    shared_w = (ws_gate[0].astype(_BF16), ws_up[0].astype(_BF16), ws_down[0].astype(_BF16))
    x2d = x.reshape(t, d)

    parts = LAYER_PARTS if bsz % LAYER_PARTS == 0 and (bsz // LAYER_PARTS * seq) % (tt * SC_WORKERS) == 0 else 1
    bp = bsz // parts
    tp = bp * seq
    nb = tp * TOP_K // ROW_BLOCK + N_EXPERTS
    nbb = GDN_ROWS if bp % GDN_ROWS == 0 else 1

    def premix(part):
        return _premix(x, tails_m[0], wts, lt=_pick(seq, 512), b0=part * bp, bsz=bp)

    def delta(part, pre, after):
        yc, yg, _, _ = _delta(pre, s_m[0], gnw, after, lg=_pick(seq, 512), nbb=nbb)
        return _outproj(yc.reshape(tp, c), yg.reshape(tp, gw), x2d, wo, ln1_g, ln1_b, tm=tm, tile0=part * (tp // tm))

    def route(h1, h1p, after):
        idx, gate, rank, cnt = _router(h1, wr_hi, wr_lo, b_router[0][:, None], tt=tt, tile0=0, t=tp)
        counts = cnt[:, 0].astype(_I32)
        pcounts = (counts + ROW_BLOCK - 1) // ROW_BLOCK * ROW_BLOCK
        pends = jnp.cumsum(pcounts)
        pstarts = (pends - pcounts).astype(_I32)
        tiled = lambda a: a.reshape(TOP_K, tp // tt, tt).transpose(1, 0, 2).reshape(tp * TOP_K)
        pos = _sc_position(tiled(idx), tiled(rank), pstarts, after)
        pos = pos.reshape(tp // tt, TOP_K, tt)
        nwin = tt // SC_WINDOW
        pos3 = pos.reshape(tp // tt, TOP_K, nwin, SC_WINDOW).transpose(0, 2, 1, 3)
        pos3 = pos3.reshape(tp // SC_WINDOW, TOP_K, SC_WINDOW)
        xs, done = _sc_scatter(h1p.reshape(tp, QUAD, 128), pos3, nb * ROW_BLOCK, 0)
        xs = _padfill(counts, pstarts, pcounts.astype(_I32), xs.reshape(nb * ROW_BLOCK * QUAD, 128))
        blocks = ((pstarts // ROW_BLOCK).astype(_I32), (pcounts // ROW_BLOCK).astype(_I32),
                  (pends[-1:] // ROW_BLOCK).astype(_I32))
        return xs, blocks, pos, gate, done

    def experts(xs, blocks, pos):
        ys = _ffn(*blocks, xs, w_gate[0], w_up[0], w_down[0])
        yg = _sc_gather(ys.reshape(nb * ROW_BLOCK, QUAD, 128), pos.reshape(tp * TOP_K))
        return yg.reshape(tp // tt, TOP_K, tt * QUAD, 128), ys

    def combine(part, gate, h1, yg, out):
        return _combine_stream(gate, h1, yg, *shared_w, ln2_g, ln2_b, out, tt=tt, tile0=part * (tp // tt), t_all=t)

    out = None
    done = jnp.zeros((SC_WORKERS, TOP_K, SC_WINDOW), _I32)
    h1, h1p = delta(0, premix(0), s00)
    routed = route(h1, h1p, done)
    for part in range(parts):
        xs, blocks, pos, gate, done = routed
        h1_cur = h1
        last = part + 1 == parts
        pre = None if last else premix(part + 1)
        yg, ys = experts(xs, blocks, pos)
        if not last:
            h1, h1p = delta(part + 1, pre, ys)
            routed = route(h1, h1p, done)
        out = combine(part, gate, h1_cur, yg, out)
    return out.reshape(bsz, seq, d)
```

```python
import functools

import jax
import jax.numpy as jnp
from jax import lax
from jax.experimental import pallas as pl
from jax.experimental.pallas import tpu as pltpu
from jax.experimental.pallas import tpu_sc as plsc

_F32 = jnp.float32
_BF16 = jnp.bfloat16
_I32 = jnp.int32

D_MODEL = 1024
N_META = 16
CONV_WIDTH = 512
CONV_K = 3
GDN_HEADS = 4
GDN_HEAD_DIM = 128
GDN_WIDTH = GDN_HEADS * GDN_HEAD_DIM
GDN_CONV_K = 4
CHUNK = 64
N_EXPERTS = 256
TOP_K = 8
N_GROUPS = 8
TOPK_GROUPS = 4
E_PER_GROUP = N_EXPERTS // N_GROUPS
EXPERT_FF = 256
ROUTED_SCALE = 2.5
ROW_BLOCK = 256
DN_ALPHA = 2.0 ** 0.25
NORM_EPS = 1e-5
HALF = D_MODEL // 2
QUAD = HALF // 128
STACK = GDN_HEADS * CHUNK
HIST = 8
GDN_ROWS = 4
PREMIX_SUB = 2
SC_CORES = 2
SC_SUBCORES = 16
SC_WORKERS = SC_CORES * SC_SUBCORES
SC_LANES = 16
SC_CHUNK = 64
SC_RING = 2
SC_WINDOW = 128
LAYER_PARTS = 2
RING = 8
IN_AHEAD = RING - 2

V7X_VMEM_BYTES = 64 * 1024 * 1024
VMEM_LIMIT = V7X_VMEM_BYTES - 8 * 1024 * 1024


def _cparams(*sem):
    return pltpu.CompilerParams(dimension_semantics=sem, vmem_limit_bytes=VMEM_LIMIT)


def _mm(a, b):
    return jnp.dot(a.astype(_BF16), b.astype(_BF16), preferred_element_type=_F32)


def _mm_nt(a, b):
    return lax.dot_general(a.astype(_BF16), b.astype(_BF16), (((1,), (1,)), ((), ())),
                           preferred_element_type=_F32)


def _mm_tn(a, b):
    return lax.dot_general(a.astype(_BF16), b.astype(_BF16), (((0,), (0,)), ((), ())),
                           preferred_element_type=_F32)


def _sigmoid(x):
    return 1.0 / (1.0 + jnp.exp(-x))


def _silu(x):
    return x * _sigmoid(x)


def _softplus(x):
    return jnp.maximum(x, 0.0) + jnp.log1p(jnp.exp(-jnp.abs(x)))


def _pack_halves(y):
    return pltpu.pack_elementwise([y[:, :HALF], y[:, HALF:]], packed_dtype=_BF16)


def _store_rows(ref, packed):
    r = packed.shape[0]
    for c in range(QUAD):
        ref[pl.ds(c, r, stride=QUAD), :] = packed[:, c * 128:(c + 1) * 128]


def _load_rows(ref, r):
    return jnp.concatenate([ref[pl.ds(c, r, stride=QUAD), :] for c in range(QUAD)], axis=1)


def _unpack_halves(p):
    lo = pltpu.unpack_elementwise(p, index=0, packed_dtype=_BF16, unpacked_dtype=_F32)
    hi = pltpu.unpack_elementwise(p, index=1, packed_dtype=_BF16, unpacked_dtype=_F32)
    return lo, hi


def _layer_norm(h, g, b):
    mu = jnp.mean(h, axis=-1, keepdims=True)
    d = h - mu
    var = jnp.mean(d * d, axis=-1, keepdims=True)
    return d * lax.rsqrt(var + NORM_EPS) * g + b


def _premix_body(x_ref, tails_ref, wa_ref, wq_ref, wz_ref, wbd_ref, wbdt_ref, cw_ref, cnw_ref,
                 gcw_ref, prow_ref, pcol_ref,
                 yc_ref, q_ref, k_ref, v_ref, z_ref, bgc_ref, bgr_ref, tout_ref, ext_ref, *, lt):
    cw_ = CONV_WIDTH

    @pl.when(pl.program_id(1) == 0)
    def _():
        ext_ref[0:HIST, :] = tails_ref[...]

    cw = cw_ref[...]
    gcw = gcw_ref[...]
    prow = prow_ref[...]
    pcol = pcol_ref[...]

    def sub_tile(r0, n):
        rows = slice(r0, r0 + n)
        erows = slice(HIST + r0, HIST + r0 + n)
        xb = x_ref[0, rows, :].astype(_BF16)
        pa = jnp.dot(xb, wa_ref[...], preferred_element_type=_F32)
        yield
        gate_b = pa[:, 0:cw_]
        u = pa[:, cw_:2 * cw_] * pa[:, 2 * cw_:3 * cw_]
        ext_ref[erows, 0:cw_] = u
        pq = jnp.dot(xb, wq_ref[...], preferred_element_type=_F32)
        yield
        ext_ref[erows, cw_:] = pq
        zz = jnp.dot(xb, wz_ref[...], preferred_element_type=_F32)
        bdc = jnp.dot(xb, wbd_ref[...], preferred_element_type=_F32)
        bdr = _mm_nt(wbdt_ref[...], xb)
        yield

        ca = u * cw[CONV_K - 1:CONV_K, :]
        for j in range(CONV_K - 1):
            ca = ca + ext_ref[pl.ds(HIST + r0 - (CONV_K - 1) + j, n), 0:cw_] * cw[j:j + 1, :]
        yc = gate_b * ca
        ms = jnp.mean(yc * yc, axis=-1, keepdims=True)
        yc_ref[0, rows, :] = (yc * lax.rsqrt(ms + NORM_EPS) * cnw_ref[...]).astype(_BF16)

        cq = pq * gcw[GDN_CONV_K - 1:GDN_CONV_K, :]
        for j in range(GDN_CONV_K - 1):
            cq = cq + ext_ref[pl.ds(HIST + r0 - (GDN_CONV_K - 1) + j, n), cw_:] * gcw[j:j + 1, :]
        s = _silu(cq)
        for h in range(GDN_HEADS):
            lo, hi = h * GDN_HEAD_DIM, (h + 1) * GDN_HEAD_DIM
            qh = s[:, lo:hi]
            kh = s[:, GDN_WIDTH + lo:GDN_WIDTH + hi]
            qn = qh * lax.rsqrt(jnp.sum(qh * qh, axis=-1, keepdims=True) + 1e-6)
            kn = kh * lax.rsqrt(jnp.sum(kh * kh, axis=-1, keepdims=True) + 1e-6)
            q_ref[0, rows, lo:hi] = (qn * (GDN_HEAD_DIM ** -0.5)).astype(_BF16)
            k_ref[0, rows, lo:hi] = kn.astype(_BF16)
        v_ref[0, rows, :] = s[:, 2 * GDN_WIDTH:].astype(_BF16)
        z_ref[0, rows, :] = zz.astype(_BF16)

        g_c = -jnp.exp(prow[0:1, :]) * _softplus(bdc + prow[1:2, :])
        lane = lax.broadcasted_iota(_I32, bdc.shape, 1)
        bgc_ref[0, rows, :] = jnp.where(lane < GDN_HEADS, _sigmoid(bdc), g_c)
        g_r = -jnp.exp(pcol[:, 0:1]) * _softplus(bdr + pcol[:, 1:2])
        row = lax.broadcasted_iota(_I32, bdr.shape, 0)
        bgr_ref[0, :, rows] = jnp.where(row < GDN_HEADS, _sigmoid(bdr), g_r)

    n_sub = PREMIX_SUB if lt % (PREMIX_SUB * 128) == 0 else 1
    live = [sub_tile(i * (lt // n_sub), lt // n_sub) for i in range(n_sub)]
    while live:
        live = [g for g in live if next(g, live) is not live]

    tail = ext_ref[lt:lt + HIST, :]
    ext_ref[0:HIST, :] = tail
    tout_ref[0] = tail


def _premix(x, tails, wts, *, lt, b0, bsz):
    _, seq, d = x.shape
    assert seq % lt == 0
    grid = (bsz, seq // lt)
    full = lambda a: pl.BlockSpec(a.shape, lambda b, j: (0,) * a.ndim)
    tok = lambda w: pl.BlockSpec((1, lt, w), lambda b, j: (b, j, 0))
    x_spec = pl.BlockSpec((1, lt, d), lambda b, j: (b + b0, j, 0))
    (wa, wq, wz, wbd, wbdt, cw, cnw, gcw, prow, pcol) = wts
    ext_w = CONV_WIDTH + 3 * GDN_WIDTH
    out_shape = (
        jax.ShapeDtypeStruct((bsz, seq, CONV_WIDTH), _BF16),
        jax.ShapeDtypeStruct((bsz, seq, GDN_WIDTH), _BF16),
        jax.ShapeDtypeStruct((bsz, seq, GDN_WIDTH), _BF16),
        jax.ShapeDtypeStruct((bsz, seq, GDN_WIDTH), _BF16),
        jax.ShapeDtypeStruct((bsz, seq, GDN_WIDTH), _BF16),
        jax.ShapeDtypeStruct((bsz, seq, 128), _F32),
        jax.ShapeDtypeStruct((bsz, 8, seq), _F32),
        jax.ShapeDtypeStruct((bsz, HIST, ext_w), _F32),
    )
    out_specs = (tok(CONV_WIDTH), tok(GDN_WIDTH), tok(GDN_WIDTH), tok(GDN_WIDTH), tok(GDN_WIDTH),
                 tok(128), pl.BlockSpec((1, 8, lt), lambda b, j: (b, 0, j)),
                 pl.BlockSpec((1, HIST, ext_w), lambda b, j: (b, 0, 0)))
    return pl.pallas_call(
        functools.partial(_premix_body, lt=lt),
        grid=grid,
        in_specs=[x_spec, full(tails)] + [full(w) for w in wts],
        out_specs=out_specs,
        out_shape=out_shape,
        scratch_shapes=[pltpu.VMEM((HIST + lt, ext_w), _F32)],
        compiler_params=_cparams("arbitrary", "arbitrary"),
        name="premix",
    )(x, tails, *wts)


def _cumsum_rows(x):
    row = lax.broadcasted_iota(_I32, x.shape, 0)
    s = 1
    while s < x.shape[0]:
        x = x + jnp.where(row >= s, pltpu.roll(x, s, 0), 0.0)
        s *= 2
    return x


def _cumsum_lanes_seg(x):
    lane = lax.broadcasted_iota(_I32, x.shape, 1) & (CHUNK - 1)
    s = 1
    while s < CHUNK:
        x = x + jnp.where(lane >= s, pltpu.roll(x, s, 1), 0.0)
        s *= 2
    return x


def _stack_heads(a):
    return jnp.concatenate([a[:, h * GDN_HEAD_DIM:(h + 1) * GDN_HEAD_DIM] for h in range(GDN_HEADS)], axis=0)


def _gdn_body(q_ref, k_ref, v_ref, z_ref, bgc_ref, grow_ref, s0_ref, gnw_ref, after_ref,
              y_ref, sout_ref, s_ref, *, nc, nbb):
    del after_ref

    @pl.when(pl.program_id(1) == 0)
    def _():
        for r in range(nbb):
            s_ref[r] = s0_ref[...]

    ri = lax.broadcasted_iota(_I32, (STACK, STACK), 0)
    ci = lax.broadcasted_iota(_I32, (STACK, STACK), 1)
    same64 = (ri >> 6) == (ci >> 6)
    same32 = (ri >> 5) == (ci >> 5)
    same16 = (ri >> 4) == (ci >> 4)
    low_incl = same64 & (ri >= ci)
    low_strict = same64 & (ri > ci)
    gnw = gnw_ref[...]

    def chunk_row(r, c):
        off = pl.multiple_of(c * CHUNK, CHUNK)
        q_all = _stack_heads(q_ref[r, pl.ds(off, CHUNK), :].astype(_F32))
        k_all = _stack_heads(k_ref[r, pl.ds(off, CHUNK), :].astype(_F32))
        v_all = _stack_heads(v_ref[r, pl.ds(off, CHUNK), :].astype(_F32))
        bgc = bgc_ref[r, pl.ds(off, CHUNK), :]
        gcs = _cumsum_rows(bgc)
        hd = (CHUNK, GDN_HEAD_DIM)
        beta_b = jnp.concatenate(
            [jnp.broadcast_to(bgc[:, h:h + 1], hd) for h in range(GDN_HEADS)], axis=0)
        gc_b = jnp.concatenate(
            [jnp.broadcast_to(gcs[:, GDN_HEADS + h:GDN_HEADS + h + 1], hd) for h in range(GDN_HEADS)], axis=0)
        gl = [gcs[CHUNK - 1:CHUNK, GDN_HEADS + h:GDN_HEADS + h + 1] for h in range(GDN_HEADS)]
        gl_b = jnp.concatenate([jnp.broadcast_to(g1, hd) for g1 in gl], axis=0)
        gcr = _cumsum_lanes_seg(jnp.broadcast_to(grow_ref[r, c], (8, STACK)))[0:1, :]

        diff = jnp.concatenate([gc_b, gc_b], axis=1) - gcr
        decay = jnp.exp(jnp.where(low_incl, diff, -1e30))
        kb = k_all * beta_b
        a1 = _mm_nt(jnp.concatenate([kb, q_all], axis=0), k_all)
        yield
        m = jnp.where(low_strict, a1[:STACK] * decay, 0.0)
        attn = a1[STACK:] * decay

        l16 = jnp.where(same16, m, 0.0)
        c1 = jnp.where(same32 & jnp.logical_not(same16), m, 0.0)
        c2 = jnp.where(same32, 0.0, m)
        p2 = _mm(l16, l16)
        yield
        p4 = _mm(p2, p2)
        t = _mm(l16, p2)
        yield
        na = p2 - l16 - t
        p8 = _mm(p4, p4)
        t = _mm(na, p4)
        yield
        nb = na + p4 + t
        t = _mm(nb, p8)
        yield
        ncm = nb + p8 + t
        t = _mm(c1, ncm)
        yield
        y1 = c1 + t
        t = _mm(ncm, y1)
        yield
        n1 = ncm - y1 - t
        t = _mm(c2, n1)
        yield
        y2 = c2 + t
        t = _mm(n1, y2)
        yield
        nt = n1 - y2 - t

        egc = jnp.exp(gc_b)
        rhs = jnp.concatenate([v_all * beta_b, kb * egc], axis=1)
        t = _mm(nt, rhs)
        yield
        uw = rhs + t
        u_all = uw[:, :GDN_HEAD_DIM]
        w_all = uw[:, GDN_HEAD_DIM:]
        qd = q_all * egc
        kd = k_all * jnp.exp(gl_b - gc_b)

        bs = []
        for h in range(GDN_HEADS):
            r0, r1 = h * CHUNK, (h + 1) * CHUNK
            bs.append(_mm(jnp.concatenate([w_all[r0:r1], qd[r0:r1]], axis=0), s_ref[r, h]))
        yield
        vn = [u_all[h * CHUNK:(h + 1) * CHUNK] - bs[h][:CHUNK] for h in range(GDN_HEADS)]
        vn_all = jnp.concatenate(vn, axis=0)
        t = _mm(attn, vn_all)
        ds = [_mm_tn(kd[h * CHUNK:(h + 1) * CHUNK], vn[h]) for h in range(GDN_HEADS)]
        yield
        o_all = jnp.concatenate([b[CHUNK:] for b in bs], axis=0) + t
        for h in range(GDN_HEADS):
            r0, r1 = h * CHUNK, (h + 1) * CHUNK
            s_ref[r, h] = s_ref[r, h] * jnp.exp(gl[h]) + ds[h]
            o = o_all[r0:r1]
            zz = z_ref[r, pl.ds(off, CHUNK), h * GDN_HEAD_DIM:(h + 1) * GDN_HEAD_DIM].astype(_F32)
            on = o * lax.rsqrt(jnp.mean(o * o, axis=-1, keepdims=True) + NORM_EPS) * gnw
            y_ref[r, pl.ds(off, CHUNK), h * GDN_HEAD_DIM:(h + 1) * GDN_HEAD_DIM] = (on * _silu(zz)).astype(_BF16)

    def chunk(c, carry):
        live = [chunk_row(r, c) for r in range(nbb)]
        while live:
            live = [g for g in live if next(g, live) is not live]
        return carry

    lax.fori_loop(0, nc, chunk, 0)
    sout_ref[...] = s_ref[...]


def _gdn(q, k, v, z, bgc, grow, s0, gnw, after, *, lg, nbb):
    bsz, seq, _ = q.shape
    assert seq % lg == 0 and lg % CHUNK == 0 and bsz % nbb == 0
    nc = lg // CHUNK
    tok = lambda w: pl.BlockSpec((nbb, lg, w), lambda b, j: (b, j, 0))
    full = lambda a: pl.BlockSpec(a.shape, lambda b, j: (0,) * a.ndim)
    st = (nbb, GDN_HEADS, GDN_HEAD_DIM, GDN_HEAD_DIM)
    return pl.pallas_call(
        functools.partial(_gdn_body, nc=nc, nbb=nbb),
        grid=(bsz // nbb, seq // lg),
        in_specs=[tok(GDN_WIDTH)] * 4 + [tok(128), pl.BlockSpec((nbb, nc, 1, STACK), lambda b, j: (b, j, 0, 0)),
                                           full(s0), full(gnw), pl.BlockSpec(memory_space=pl.ANY)],
        out_specs=(tok(GDN_WIDTH), pl.BlockSpec(st, lambda b, j: (b, 0, 0, 0))),
        out_shape=(jax.ShapeDtypeStruct((bsz, seq, GDN_WIDTH), _BF16),
                   jax.ShapeDtypeStruct((bsz, GDN_HEADS, GDN_HEAD_DIM, GDN_HEAD_DIM), _F32)),
        scratch_shapes=[pltpu.VMEM(st, _F32)],
        compiler_params=_cparams("arbitrary", "arbitrary"),
        name="gdn",
    )(q, k, v, z, bgc, grow, s0, gnw, after)


def _outproj_body(yc_ref, yg_ref, x_ref, wo_ref, g_ref, b_ref, h1_ref, h1p_ref):
    mix = (jnp.dot(yc_ref[...], wo_ref[0:CONV_WIDTH, :], preferred_element_type=_F32)
           + jnp.dot(yg_ref[...], wo_ref[CONV_WIDTH:, :], preferred_element_type=_F32))
    h1 = _layer_norm(DN_ALPHA * x_ref[...] + mix, g_ref[...], b_ref[...])
    h1_ref[...] = h1
    _store_rows(h1p_ref, _pack_halves(h1))


def _outproj(yc, yg, x2d, wo, g, b, *, tm, tile0):
    t = yc.shape[0]
    assert t % tm == 0
    row = lambda w: pl.BlockSpec((tm, w), lambda i: (i, 0))
    full = lambda a: pl.BlockSpec(a.shape, lambda i: (0,) * a.ndim)
    return pl.pallas_call(
        _outproj_body,
        grid=(t // tm,),
        in_specs=[row(CONV_WIDTH), row(GDN_WIDTH), pl.BlockSpec((tm, D_MODEL), lambda i: (i + tile0, 0)),
                  full(wo), full(g), full(b)],
        out_specs=(row(D_MODEL), pl.BlockSpec((tm * QUAD, 128), lambda i: (i, 0))),
        out_shape=(jax.ShapeDtypeStruct((t, D_MODEL), _F32), jax.ShapeDtypeStruct((t * QUAD, 128), jnp.uint32)),
        compiler_params=_cparams("arbitrary"),
        name="outproj",
    )(yc, yg, x2d, wo, g, b)


def _router_body(h1_ref, wh_ref, wl_ref, br_ref, idx_ref, gate_ref, rank_ref, cnt_ref, carry_ref, *, tt):
    @pl.when(pl.program_id(0) == 0)
    def _():
        carry_ref[...] = jnp.zeros_like(carry_ref)

    x = h1_ref[...]
    xh = x.astype(_BF16)
    xl = (x - xh.astype(_F32)).astype(_BF16)
    wh = wh_ref[...]
    logits = _mm_nt(wh, xh) + _mm_nt(wh, xl) + _mm_nt(wl_ref[...], xh)
    scores = _sigmoid(logits)
    sel = scores + br_ref[...]
    ninf = -jnp.inf

    r32 = lax.broadcasted_iota(_I32, (E_PER_GROUP, tt), 0)
    gsc = []
    for g in range(N_GROUPS):
        xg = sel[g * E_PER_GROUP:(g + 1) * E_PER_GROUP]
        m1 = jnp.max(xg, axis=0, keepdims=True)
        i1 = jnp.min(jnp.where(xg == m1, r32, E_PER_GROUP), axis=0, keepdims=True)
        m2 = jnp.max(jnp.where(r32 == i1, ninf, xg), axis=0, keepdims=True)
        gsc.append(m1 + m2)
    work = jnp.concatenate(gsc, axis=0)
    r8 = lax.broadcasted_iota(_I32, (N_GROUPS, tt), 0)
    gkeep = jnp.zeros((N_GROUPS, tt), _F32)
    for _ in range(TOPK_GROUPS):
        m = jnp.max(work, axis=0, keepdims=True)
        gi = jnp.min(jnp.where(work == m, r8, N_GROUPS), axis=0, keepdims=True)
        pick = r8 == gi
        gkeep = jnp.where(pick, 1.0, gkeep)
        work = jnp.where(pick, ninf, work)
    selm = jnp.concatenate(
        [jnp.where(gkeep[g:g + 1] > 0.5, sel[g * E_PER_GROUP:(g + 1) * E_PER_GROUP], ninf)
         for g in range(N_GROUPS)], axis=0)

    re = lax.broadcasted_iota(_I32, (N_EXPERTS, tt), 0)
    msel = jnp.zeros((N_EXPERTS, tt), _F32)
    idxs, gates = [], []
    for _ in range(TOP_K):
        m = jnp.max(selm, axis=0, keepdims=True)
        ii = jnp.min(jnp.where(selm == m, re, N_EXPERTS), axis=0, keepdims=True)
        hit = re == ii
        idxs.append(ii)
        gates.append(jnp.sum(jnp.where(hit, scores, 0.0), axis=0, keepdims=True))
        selm = jnp.where(hit, ninf, selm)
        msel = jnp.where(hit, 1.0, msel)
    gate = jnp.concatenate(gates, axis=0)
    gate_ref[...] = gate / jnp.sum(gate, axis=0, keepdims=True) * ROUTED_SCALE
    idx_ref[...] = jnp.concatenate(idxs, axis=0)

    ta = lax.broadcasted_iota(_I32, (tt, tt), 0)
    tb = lax.broadcasted_iota(_I32, (tt, tt), 1)
    earlier = jnp.where(ta < tb, 1.0, 0.0)
    carry = carry_ref[...]
    rank_all = _mm(msel, earlier) + carry[:, 0:1]
    rank_ref[...] = jnp.concatenate(
        [jnp.sum(jnp.where(re == ii, rank_all, 0.0), axis=0, keepdims=True) for ii in idxs],
        axis=0).astype(_I32)
    carry = carry + jnp.sum(msel, axis=1, keepdims=True)
    carry_ref[...] = carry
    cnt_ref[...] = carry


def _router(h1, wh, wl, br, *, tt, tile0, t):
    assert t % tt == 0
    full = lambda a: pl.BlockSpec(a.shape, lambda i: (0,) * a.ndim)
    kt = pl.BlockSpec((TOP_K, tt), lambda i: (0, i))
    return pl.pallas_call(
        functools.partial(_router_body, tt=tt),
        grid=(t // tt,),
        in_specs=[pl.BlockSpec((tt, D_MODEL), lambda i: (i + tile0, 0)), full(wh), full(wl), full(br)],
        out_specs=(kt, kt, kt, pl.BlockSpec((N_EXPERTS, 128), lambda i: (0, 0))),
        out_shape=(jax.ShapeDtypeStruct((TOP_K, t), _I32), jax.ShapeDtypeStruct((TOP_K, t), _F32),
                   jax.ShapeDtypeStruct((TOP_K, t), _I32), jax.ShapeDtypeStruct((N_EXPERTS, 128), _F32)),
        scratch_shapes=[pltpu.VMEM((N_EXPERTS, 128), _F32)],
        compiler_params=_cparams("arbitrary"),
        name="router",
    )(h1, wh, wl, br)


def _position_body(idx_ref, rank_ref, pstart_ref, pos_ref, *, tt):
    re = lax.broadcasted_iota(_I32, (N_EXPERTS, tt), 0)
    ps = pstart_ref[...]
    idx = idx_ref[...]
    rows = [jnp.sum(jnp.where(re == idx[k:k + 1], ps, 0), axis=0, keepdims=True) for k in range(TOP_K)]
    pos_ref[0] = jnp.concatenate(rows, axis=0) + rank_ref[...]


def _position(idx, rank, pstart, *, tt):
    t = idx.shape[1]
    kt = pl.BlockSpec((TOP_K, tt), lambda i: (0, i))
    return pl.pallas_call(
        functools.partial(_position_body, tt=tt),
        grid=(t // tt,),
        in_specs=[kt, kt, pl.BlockSpec(pstart.shape, lambda i: (0, 0))],
        out_specs=pl.BlockSpec((1, TOP_K, tt), lambda i: (i, 0, 0)),
        out_shape=jax.ShapeDtypeStruct((t // tt, TOP_K, tt), _I32),
        compiler_params=_cparams("arbitrary"),
        name="position",
    )(idx, rank, pstart)


def _ffn_body(blk0_ref, nblk_ref, ntot_ref, xs_hbm, wg_ref, wu_ref, wd_ref, ys_hbm,
              xbuf, ybuf, sem_in, sem_out, wgu_bf, wd_bf):
    e = pl.program_id(0)
    nblk = nblk_ref[e]
    blk0 = blk0_ref[e]
    ntot = ntot_ref[0]

    blk_rows = ROW_BLOCK * QUAD

    def rows(g):
        return pl.ds(pl.multiple_of(g * blk_rows, blk_rows), blk_rows)

    def in_start(g, slot):
        pltpu.make_async_copy(xs_hbm.at[rows(g)], xbuf.at[slot], sem_in.at[slot]).start()

    def in_wait(slot):
        pltpu.make_async_copy(xs_hbm.at[rows(0)], xbuf.at[slot], sem_in.at[slot]).wait()

    def out_start(g, slot):
        pltpu.make_async_copy(ybuf.at[slot], ys_hbm.at[rows(g)], sem_out.at[slot]).start()

    def out_wait(slot):
        pltpu.make_async_copy(ybuf.at[slot], ys_hbm.at[rows(0)], sem_out.at[slot]).wait()

    @pl.when(e == 0)
    def _():
        for i in range(IN_AHEAD):
            @pl.when(i < ntot)
            def _():
                in_start(i, i)

    @pl.when(nblk > 0)
    def _():
        wgu_bf[:, 0:EXPERT_FF] = wg_ref[0].astype(_BF16)
        wgu_bf[:, EXPERT_FF:] = wu_ref[0].astype(_BF16)
        wd_bf[...] = wd_ref[0].astype(_BF16)

        def acquire(g):
            slot = g & (RING - 1)
            in_wait(slot)

            @pl.when(g + IN_AHEAD < ntot)
            def _():
                in_start(g + IN_AHEAD, (g + IN_AHEAD) & (RING - 1))

            @pl.when(g >= RING)
            def _():
                out_wait(slot)

            return slot

        def compute(slot):
            lo, hi = _unpack_halves(_load_rows(xbuf.at[slot], ROW_BLOCK))
            a = jnp.dot(lo.astype(_BF16), wgu_bf[0:HALF, :], preferred_element_type=_F32)
            yield
            gu = a + jnp.dot(hi.astype(_BF16), wgu_bf[HALF:, :], preferred_element_type=_F32)
            yield
            h = (_silu(gu[:, :EXPERT_FF]) * gu[:, EXPERT_FF:]).astype(_BF16)
            y = jnp.dot(h, wd_bf[...], preferred_element_type=_F32)
            yield
            _store_rows(ybuf.at[slot], _pack_halves(y))

        def run(gs):
            slots = [acquire(g) for g in gs]
            live = [compute(s) for s in slots]
            while live:
                live = [c for c in live if next(c, live) is not live]
            for g, s in zip(gs, slots):
                out_start(g, s)

        def pair(j, carry):
            run([blk0 + 2 * j, blk0 + 2 * j + 1])
            return carry

        lax.fori_loop(0, nblk // 2, pair, 0)

        @pl.when((nblk & 1) == 1)
        def _():
            run([blk0 + nblk - 1])

    @pl.when(e == N_EXPERTS - 1)
    def _():
        for i in range(RING):
            @pl.when(i < ntot)
            def _():
                out_wait((ntot - 1 - i) & (RING - 1))


def _ffn(blk0, nblk, ntot, xs, wg, wu, wd):
    grid_spec = pltpu.PrefetchScalarGridSpec(
        num_scalar_prefetch=3,
        grid=(N_EXPERTS,),
        in_specs=[pl.BlockSpec(memory_space=pl.ANY),
                  pl.BlockSpec((1, D_MODEL, EXPERT_FF), lambda e, *_: (e, 0, 0)),
                  pl.BlockSpec((1, D_MODEL, EXPERT_FF), lambda e, *_: (e, 0, 0)),
                  pl.BlockSpec((1, EXPERT_FF, D_MODEL), lambda e, *_: (e, 0, 0))],
        out_specs=pl.BlockSpec(memory_space=pl.ANY),
        scratch_shapes=[pltpu.VMEM((RING, ROW_BLOCK * QUAD, 128), jnp.uint32),
                        pltpu.VMEM((RING, ROW_BLOCK * QUAD, 128), jnp.uint32),
                        pltpu.SemaphoreType.DMA((RING,)), pltpu.SemaphoreType.DMA((RING,)),
                        pltpu.VMEM((D_MODEL, 2 * EXPERT_FF), _BF16), pltpu.VMEM((EXPERT_FF, D_MODEL), _BF16)],
    )
    return pl.pallas_call(
        _ffn_body,
        grid_spec=grid_spec,
        out_shape=jax.ShapeDtypeStruct(xs.shape, jnp.uint32),
        compiler_params=_cparams("arbitrary"),
        name="ffn",
    )(blk0, nblk, ntot, xs, wg, wu, wd)


def _sc_position(idx, rank, pstart, after):
    n = idx.shape[0]
    per_w = n // SC_WORKERS
    assert per_w * SC_WORKERS == n and per_w % SC_LANES == 0
    mesh = plsc.VectorSubcoreMesh(core_axis_name="c", subcore_axis_name="s",
                                  num_cores=SC_CORES, num_subcores=SC_SUBCORES)

    @functools.partial(
        pl.kernel, mesh=mesh,
        out_type=jax.ShapeDtypeStruct((n,), _I32),
        scratch_types=[pltpu.VMEM((per_w,), _I32), pltpu.VMEM((per_w,), _I32), pltpu.VMEM((per_w,), _I32),
                       pltpu.VMEM((N_EXPERTS,), _I32)],
        compiler_params=pltpu.CompilerParams(needs_layout_passes=False),
        name="sc_position",
    )
    def position(idx_hbm, rank_hbm, ps_hbm, after_hbm, out_hbm, idx_v, rank_v, pos_v, ps_v):
        del after_hbm
        wid = lax.axis_index("s") * SC_CORES + lax.axis_index("c")
        mine = pl.ds(pl.multiple_of(wid * per_w, per_w), per_w)
        pltpu.sync_copy(ps_hbm, ps_v)
        pltpu.sync_copy(idx_hbm.at[mine], idx_v)
        pltpu.sync_copy(rank_hbm.at[mine], rank_v)

        @pl.loop(0, per_w, step=SC_LANES)
        def _(j):
            lanes = pl.ds(j, SC_LANES)
            pos_v[lanes] = plsc.load_gather(ps_v, [idx_v[lanes]]) + rank_v[lanes]

        pltpu.sync_copy(pos_v, out_hbm.at[mine])

    return position(idx, rank, pstart, after)


def _sc_gather(table, idx):
    b = idx.shape[0]
    nchunk = b // (SC_WORKERS * SC_CHUNK)
    assert nchunk * SC_WORKERS * SC_CHUNK == b and nchunk % SC_RING == 0
    idx2 = idx.reshape(SC_WORKERS * nchunk, SC_CHUNK)
    row = table.shape[1:]
    mesh = plsc.VectorSubcoreMesh(core_axis_name="c", subcore_axis_name="s",
                                  num_cores=SC_CORES, num_subcores=SC_SUBCORES)

    @functools.partial(
        pl.kernel, mesh=mesh,
        out_type=jax.ShapeDtypeStruct((b,) + row, table.dtype),
        scratch_types=[pltpu.VMEM((nchunk, SC_CHUNK), _I32), pltpu.VMEM((SC_RING, SC_CHUNK) + row, table.dtype),
                       pltpu.SemaphoreType.DMA((SC_RING,)), pltpu.SemaphoreType.DMA((SC_RING,))],
        name="sc_gather",
    )
    def gather(table_hbm, idx_hbm, out_hbm, idx_v, rows_v, sem_g, sem_w):
        wid = lax.axis_index("s") * SC_CORES + lax.axis_index("c")
        c0 = wid * nchunk
        pltpu.sync_copy(idx_hbm.at[pl.ds(pl.multiple_of(c0, nchunk), nchunk)], idx_v)

        def fetch(i, s):
            return pltpu.make_async_copy(table_hbm.at[idx_v.at[i]], rows_v.at[s], sem_g.at[s])

        def flush(i, s):
            rows = pl.ds(pl.multiple_of((c0 + i) * SC_CHUNK, SC_CHUNK), SC_CHUNK)
            return pltpu.make_async_copy(rows_v.at[s], out_hbm.at[rows], sem_w.at[s])

        for s in range(SC_RING):
            fetch(s, s).start()

        @pl.loop(0, nchunk, step=SC_RING)
        def _(g):
            for s in range(SC_RING):
                i = g + s
                fetch(i, s).wait()
                flush(i, s).start()
                flush(i, s).wait()

                @pl.when(i + SC_RING < nchunk)
                def _():
                    fetch(i + SC_RING, s).start()

    return gather(table, idx2)


def _sc_scatter(rows, pos3, n_out, row0):
    nchunk, nk, w = pos3.shape
    per_w = nchunk // SC_WORKERS
    assert per_w * SC_WORKERS == nchunk and w <= 128 and row0 % w == 0 and rows.shape[0] >= row0 + nchunk * w
    row = rows.shape[1:]
    mesh = plsc.VectorSubcoreMesh(core_axis_name="c", subcore_axis_name="s",
                                  num_cores=SC_CORES, num_subcores=SC_SUBCORES)

    @functools.partial(
        pl.kernel, mesh=mesh,
        out_type=(jax.ShapeDtypeStruct((n_out,) + row, rows.dtype),
                  jax.ShapeDtypeStruct((SC_WORKERS, nk, w), _I32)),
        scratch_types=[pltpu.VMEM((nk, w), _I32), pltpu.VMEM((w,) + row, rows.dtype), pltpu.SemaphoreType.DMA],
        name="sc_scatter",
    )
    def scatter(rows_hbm, pos_hbm, out_hbm, done_hbm, idx_v, rows_v, sem):
        wid = lax.axis_index("s") * SC_CORES + lax.axis_index("c")

        @pl.loop(0, per_w)
        def _(i):
            c = wid * per_w + i
            pltpu.sync_copy(pos_hbm.at[c], idx_v)
            pltpu.sync_copy(rows_hbm.at[pl.ds(pl.multiple_of(row0 + c * w, w), w)], rows_v)
            copies = [pltpu.async_copy(rows_v, out_hbm.at[idx_v.at[k]], sem) for k in range(nk)]
            for cp in copies:
                cp.wait()

        pltpu.sync_copy(idx_v, done_hbm.at[wid])

    return scatter(rows, pos3)


def _padfill_body(cnt_ref, pst_ref, pcn_ref, xs_in, xs_out, zbuf, zsem):
    del xs_in
    zbuf[...] = jnp.zeros_like(zbuf)

    def pad_runs(e, act):
        pad = pcn_ref[e] - cnt_ref[e]
        base = pst_ref[e] + cnt_ref[e]
        for b in range(ROW_BLOCK.bit_length() - 1):
            n = 1 << b

            @pl.when(((pad >> b) & 1) == 1)
            def _():
                off = base + (pad & (n - 1))
                act(pltpu.make_async_copy(zbuf.at[pl.ds(0, QUAD * n)],
                                          xs_out.at[pl.ds(QUAD * off, QUAD * n)], zsem))

    def start_all(e, c):
        pad_runs(e, lambda d: d.start())
        return c

    def wait_all(e, c):
        pad_runs(e, lambda d: d.wait())
        return c

    lax.fori_loop(0, N_EXPERTS, start_all, 0)
    lax.fori_loop(0, N_EXPERTS, wait_all, 0)


def _padfill(counts, pstarts, pcounts, xs):
    grid_spec = pltpu.PrefetchScalarGridSpec(
        num_scalar_prefetch=3,
        grid=(1,),
        in_specs=[pl.BlockSpec(memory_space=pl.ANY)],
        out_specs=pl.BlockSpec(memory_space=pl.ANY),
        scratch_shapes=[pltpu.VMEM((QUAD * ROW_BLOCK // 2, 128), jnp.uint32), pltpu.SemaphoreType.DMA],
    )
    return pl.pallas_call(
        _padfill_body,
        grid_spec=grid_spec,
        out_shape=jax.ShapeDtypeStruct(xs.shape, xs.dtype),
        input_output_aliases={3: 0},
        compiler_params=_cparams("arbitrary"),
        name="padfill",
    )(counts, pstarts, pcounts, xs)


def _combine_stream_body(gate_ref, h1_ref, yg_ref, wsg_ref, wsu_ref, wsd_ref, g_ref, b_ref, out_ref, *, tt):
    x = h1_ref[...]
    xb = x.astype(_BF16)
    shared = _mm(_silu(_mm(xb, wsg_ref[...])) * _mm(xb, wsu_ref[...]), wsd_ref[...])
    gcol = gate_ref[...].T
    acc_lo = jnp.zeros((tt, HALF), _F32)
    acc_hi = jnp.zeros((tt, HALF), _F32)
    for k in range(TOP_K):
        lo, hi = _unpack_halves(_load_rows(yg_ref.at[0, k], tt))
        acc_lo = acc_lo + gcol[:, k:k + 1] * lo
        acc_hi = acc_hi + gcol[:, k:k + 1] * hi
    routed = jnp.concatenate([acc_lo, acc_hi], axis=1)
    out_ref[...] = _layer_norm(DN_ALPHA * x + (routed + shared), g_ref[...], b_ref[...])


def _combine_stream(gate, h1, yg, wsg, wsu, wsd, g, b, out_prev, *, tt, tile0, t_all):
    t = gate.shape[1]
    full = lambda a: pl.BlockSpec(a.shape, lambda i: (0,) * a.ndim)
    in_specs = [pl.BlockSpec((TOP_K, tt), lambda i: (0, i)), pl.BlockSpec((tt, D_MODEL), lambda i: (i, 0)),
                pl.BlockSpec((1, TOP_K, tt * QUAD, 128), lambda i: (i, 0, 0, 0)),
                full(wsg), full(wsu), full(wsd), full(g), full(b)]
    args = [gate, h1, yg, wsg, wsu, wsd, g, b]
    aliases = {}
    body = functools.partial(_combine_stream_body, tt=tt)
    if out_prev is not None:
        in_specs.append(pl.BlockSpec(memory_space=pl.ANY))
        args.append(out_prev)
        aliases = {len(args) - 1: 0}
        body = lambda *refs: _combine_stream_body(*refs[:8], refs[9], tt=tt)
    return pl.pallas_call(
        body,
        grid=(t // tt,),
        in_specs=in_specs,
        out_specs=pl.BlockSpec((tt, D_MODEL), lambda i: (i + tile0, 0)),
        out_shape=jax.ShapeDtypeStruct((t_all, D_MODEL), _F32),
        input_output_aliases=aliases,
        compiler_params=_cparams("arbitrary"),
        name="combine",
    )(*args)


def _pick(n, pref):
    t = min(n, pref)
    while n % t:
        t -= CHUNK
    return t


def _delta(pre, s0, gnw, after, *, lg, nbb):
    yc, q, k, v, z, bgc, bgr, tails_out = pre
    bsz, seq, _ = q.shape
    nch = seq // CHUNK
    grow = bgr[:, GDN_HEADS:2 * GDN_HEADS, :].reshape(bsz, GDN_HEADS, nch, CHUNK)
    grow = grow.transpose(0, 2, 1, 3).reshape(bsz, nch, 1, STACK)
    yg, s_out = _gdn(q, k, v, z, bgc, grow, s0, gnw, after, lg=lg, nbb=nbb)
    return yc, yg, tails_out, s_out


def kernel(x, meta_tokens, w_in, conv_w, conv_norm_w, gdn_conv_w, a_log, dt_bias, gdn_norm_w, w_out,
           ln1_g, ln1_b, w_router, b_router, w_gate, w_up, w_down, ws_gate, ws_up, ws_down, ln2_g, ln2_b):
    assert w_in.shape[0] == 1, "single-layer stack"
    bsz, seq, d = x.shape
    assert d == D_MODEL and seq % CHUNK == 0
    c, gw = CONV_WIDTH, GDN_WIDTH
    win = w_in[0].astype(_BF16)
    wbd = win[:, 3 * c + 4 * gw:]
    zpad = jnp.zeros((128 - 2 * GDN_HEADS,), _F32)
    zpad4 = jnp.zeros((GDN_HEADS,), _F32)
    prow = jnp.zeros((8, 128), _F32)
    prow = prow.at[0].set(jnp.concatenate([zpad4, a_log[0], zpad]))
    prow = prow.at[1].set(jnp.concatenate([zpad4, dt_bias[0], zpad]))
    wts = (win[:, :3 * c], win[:, 3 * c:3 * c + 3 * gw], win[:, 3 * c + 3 * gw:3 * c + 4 * gw],
           jnp.pad(wbd, ((0, 0), (0, 128 - 2 * GDN_HEADS))), wbd.T,
           conv_w[0], conv_norm_w, gdn_conv_w[0], prow, prow.T[:8])
    gnw = gdn_norm_w

    meta = jnp.concatenate([jnp.zeros((CHUNK - N_META, d), x.dtype), meta_tokens.astype(x.dtype)])[None]
    tails0 = jnp.zeros((HIST, c + 3 * gw), _F32)
    s00 = jnp.zeros((GDN_HEADS, GDN_HEAD_DIM, GDN_HEAD_DIM), _F32)
    _, _, tails_m, s_m = _delta(_premix(meta, tails0, wts, lt=CHUNK, b0=0, bsz=1), s00, gnw, s00, lg=CHUNK, nbb=1)

    t = bsz * seq
    tm = _pick(seq, 512)
    tt = _pick(seq, 256)
    wo = w_out[0].astype(_BF16)
    wr_t = w_router[0].T
    wr_hi = wr_t.astype(_BF16)
    wr_lo = (wr_t - wr_hi.astype(_F32)).astype(_BF16)
    shared_w = (ws_gate[0].astype(_BF16), ws_up[0].astype(_BF16), ws_down[0].astype(_BF16))
    x2d = x.reshape(t, d)

    parts = LAYER_PARTS if bsz % LAYER_PARTS == 0 and (bsz // LAYER_PARTS * seq) % (tt * SC_WORKERS) == 0 else 1
    bp = bsz // parts
    tp = bp * seq
    nb = tp * TOP_K // ROW_BLOCK + N_EXPERTS
    nbb = GDN_ROWS if bp % GDN_ROWS == 0 else 1

    def premix(part):
        return _premix(x, tails_m[0], wts, lt=_pick(seq, 512), b0=part * bp, bsz=bp)

    def delta(part, pre, after):
        yc, yg, _, _ = _delta(pre, s_m[0], gnw, after, lg=_pick(seq, 512), nbb=nbb)
        return _outproj(yc.reshape(tp, c), yg.reshape(tp, gw), x2d, wo, ln1_g, ln1_b, tm=tm, tile0=part * (tp // tm))

    def route(h1, h1p, after):
        idx, gate, rank, cnt = _router(h1, wr_hi, wr_lo, b_router[0][:, None], tt=tt, tile0=0, t=tp)
        counts = cnt[:, 0].astype(_I32)
        pcounts = (counts + ROW_BLOCK - 1) // ROW_BLOCK * ROW_BLOCK
        pends = jnp.cumsum(pcounts)
        pstarts = (pends - pcounts).astype(_I32)
        tiled = lambda a: a.reshape(TOP_K, tp // tt, tt).transpose(1, 0, 2).reshape(tp * TOP_K)
        pos = _sc_position(tiled(idx), tiled(rank), pstarts, after)
        pos = pos.reshape(tp // tt, TOP_K, tt)
        nwin = tt // SC_WINDOW
        pos3 = pos.reshape(tp // tt, TOP_K, nwin, SC_WINDOW).transpose(0, 2, 1, 3)
        pos3 = pos3.reshape(tp // SC_WINDOW, TOP_K, SC_WINDOW)
        xs, done = _sc_scatter(h1p.reshape(tp, QUAD, 128), pos3, nb * ROW_BLOCK, 0)
        xs = _padfill(counts, pstarts, pcounts.astype(_I32), xs.reshape(nb * ROW_BLOCK * QUAD, 128))
        blocks = ((pstarts // ROW_BLOCK).astype(_I32), (pcounts // ROW_BLOCK).astype(_I32),
                  (pends[-1:] // ROW_BLOCK).astype(_I32))
        return xs, blocks, pos, gate, done

    def experts(xs, blocks, pos):
        ys = _ffn(*blocks, xs, w_gate[0], w_up[0], w_down[0])
        yg = _sc_gather(ys.reshape(nb * ROW_BLOCK, QUAD, 128), pos.reshape(tp * TOP_K))
        return yg.reshape(tp // tt, TOP_K, tt * QUAD, 128), ys

    def combine(part, gate, h1, yg, out):
        return _combine_stream(gate, h1, yg, *shared_w, ln2_g, ln2_b, out, tt=tt, tile0=part * (tp // tt), t_all=t)

    out = None
    done = jnp.zeros((SC_WORKERS, TOP_K, SC_WINDOW), _I32)
    h1, h1p = delta(0, premix(0), s00)
    routed = route(h1, h1p, done)
    for part in range(parts):
        xs, blocks, pos, gate, done = routed
        h1_cur = h1
        last = part + 1 == parts
        pre = None if last else premix(part + 1)
        yg, ys = experts(xs, blocks, pos)
        if not last:
            h1, h1p = delta(part + 1, pre, ys)
            routed = route(h1, h1p, yg)
        out = combine(part, gate, h1_cur, yg, out)
    return out.reshape(bsz, seq, d)
```

```python
import functools

import jax
import jax.numpy as jnp
from jax import lax
from jax.experimental import pallas as pl
from jax.experimental.pallas import tpu as pltpu
from jax.experimental.pallas import tpu_sc as plsc

_F32 = jnp.float32
_BF16 = jnp.bfloat16
_I32 = jnp.int32

D_MODEL = 1024
N_META = 16
CONV_WIDTH = 512
CONV_K = 3
GDN_HEADS = 4
GDN_HEAD_DIM = 128
GDN_WIDTH = GDN_HEADS * GDN_HEAD_DIM
GDN_CONV_K = 4
CHUNK = 64
N_EXPERTS = 256
TOP_K = 8
N_GROUPS = 8
TOPK_GROUPS = 4
E_PER_GROUP = N_EXPERTS // N_GROUPS
EXPERT_FF = 256
ROUTED_SCALE = 2.5
ROW_BLOCK = 256
DN_ALPHA = 2.0 ** 0.25
NORM_EPS = 1e-5
HALF = D_MODEL // 2
QUAD = HALF // 128
STACK = GDN_HEADS * CHUNK
HIST = 8
GDN_ROWS = 4
PREMIX_SUB = 2
SC_CORES = 2
SC_SUBCORES = 16
SC_WORKERS = SC_CORES * SC_SUBCORES
SC_LANES = 16
SC_CHUNK = 64
SC_RING = 2
SC_WINDOW = 128
LAYER_PARTS = 2
RING = 8
IN_AHEAD = RING - 2

V7X_VMEM_BYTES = 64 * 1024 * 1024
VMEM_LIMIT = V7X_VMEM_BYTES - 8 * 1024 * 1024


def _cparams(*sem):
    return pltpu.CompilerParams(dimension_semantics=sem, vmem_limit_bytes=VMEM_LIMIT)


def _mm(a, b):
    return jnp.dot(a.astype(_BF16), b.astype(_BF16), preferred_element_type=_F32)


def _mm_nt(a, b):
    return lax.dot_general(a.astype(_BF16), b.astype(_BF16), (((1,), (1,)), ((), ())),
                           preferred_element_type=_F32)


def _mm_tn(a, b):
    return lax.dot_general(a.astype(_BF16), b.astype(_BF16), (((0,), (0,)), ((), ())),
                           preferred_element_type=_F32)


def _sigmoid(x):
    return 1.0 / (1.0 + jnp.exp(-x))


def _silu(x):
    return x * _sigmoid(x)


def _softplus(x):
    return jnp.maximum(x, 0.0) + jnp.log1p(jnp.exp(-jnp.abs(x)))


def _pack_halves(y):
    return pltpu.pack_elementwise([y[:, :HALF], y[:, HALF:]], packed_dtype=_BF16)


def _store_rows(ref, packed):
    r = packed.shape[0]
    for c in range(QUAD):
        ref[pl.ds(c, r, stride=QUAD), :] = packed[:, c * 128:(c + 1) * 128]


def _load_rows(ref, r):
    return jnp.concatenate([ref[pl.ds(c, r, stride=QUAD), :] for c in range(QUAD)], axis=1)


def _unpack_halves(p):
    lo = pltpu.unpack_elementwise(p, index=0, packed_dtype=_BF16, unpacked_dtype=_F32)
    hi = pltpu.unpack_elementwise(p, index=1, packed_dtype=_BF16, unpacked_dtype=_F32)
    return lo, hi


def _layer_norm(h, g, b):
    mu = jnp.mean(h, axis=-1, keepdims=True)
    d = h - mu
    var = jnp.mean(d * d, axis=-1, keepdims=True)
    return d * lax.rsqrt(var + NORM_EPS) * g + b


def _premix_body(x_ref, tails_ref, wa_ref, wq_ref, wz_ref, wbd_ref, wbdt_ref, cw_ref, cnw_ref,
                 gcw_ref, prow_ref, pcol_ref,
                 yc_ref, q_ref, k_ref, v_ref, z_ref, bgc_ref, bgr_ref, tout_ref, ext_ref, *, lt):
    cw_ = CONV_WIDTH

    @pl.when(pl.program_id(1) == 0)
    def _():
        ext_ref[0:HIST, :] = tails_ref[...]

    cw = cw_ref[...]
    gcw = gcw_ref[...]
    prow = prow_ref[...]
    pcol = pcol_ref[...]

    def sub_tile(r0, n):
        rows = slice(r0, r0 + n)
        erows = slice(HIST + r0, HIST + r0 + n)
        xb = x_ref[0, rows, :].astype(_BF16)
        pa = jnp.dot(xb, wa_ref[...], preferred_element_type=_F32)
        yield
        gate_b = pa[:, 0:cw_]
        u = pa[:, cw_:2 * cw_] * pa[:, 2 * cw_:3 * cw_]
        ext_ref[erows, 0:cw_] = u
        pq = jnp.dot(xb, wq_ref[...], preferred_element_type=_F32)
        yield
        ext_ref[erows, cw_:] = pq
        zz = jnp.dot(xb, wz_ref[...], preferred_element_type=_F32)
        bdc = jnp.dot(xb, wbd_ref[...], preferred_element_type=_F32)
        bdr = _mm_nt(wbdt_ref[...], xb)
        yield

        ca = u * cw[CONV_K - 1:CONV_K, :]
        for j in range(CONV_K - 1):
            ca = ca + ext_ref[pl.ds(HIST + r0 - (CONV_K - 1) + j, n), 0:cw_] * cw[j:j + 1, :]
        yc = gate_b * ca
        ms = jnp.mean(yc * yc, axis=-1, keepdims=True)
        yc_ref[0, rows, :] = (yc * lax.rsqrt(ms + NORM_EPS) * cnw_ref[...]).astype(_BF16)

        cq = pq * gcw[GDN_CONV_K - 1:GDN_CONV_K, :]
        for j in range(GDN_CONV_K - 1):
            cq = cq + ext_ref[pl.ds(HIST + r0 - (GDN_CONV_K - 1) + j, n), cw_:] * gcw[j:j + 1, :]
        s = _silu(cq)
        for h in range(GDN_HEADS):
            lo, hi = h * GDN_HEAD_DIM, (h + 1) * GDN_HEAD_DIM
            qh = s[:, lo:hi]
            kh = s[:, GDN_WIDTH + lo:GDN_WIDTH + hi]
            qn = qh * lax.rsqrt(jnp.sum(qh * qh, axis=-1, keepdims=True) + 1e-6)
            kn = kh * lax.rsqrt(jnp.sum(kh * kh, axis=-1, keepdims=True) + 1e-6)
            q_ref[0, rows, lo:hi] = (qn * (GDN_HEAD_DIM ** -0.5)).astype(_BF16)
            k_ref[0, rows, lo:hi] = kn.astype(_BF16)
        v_ref[0, rows, :] = s[:, 2 * GDN_WIDTH:].astype(_BF16)
        z_ref[0, rows, :] = zz.astype(_BF16)

        g_c = -jnp.exp(prow[0:1, :]) * _softplus(bdc + prow[1:2, :])
        lane = lax.broadcasted_iota(_I32, bdc.shape, 1)
        bgc_ref[0, rows, :] = jnp.where(lane < GDN_HEADS, _sigmoid(bdc), g_c)
        g_r = -jnp.exp(pcol[:, 0:1]) * _softplus(bdr + pcol[:, 1:2])
        row = lax.broadcasted_iota(_I32, bdr.shape, 0)
        bgr_ref[0, :, rows] = jnp.where(row < GDN_HEADS, _sigmoid(bdr), g_r)

    n_sub = PREMIX_SUB if lt % (PREMIX_SUB * 128) == 0 else 1
    live = [sub_tile(i * (lt // n_sub), lt // n_sub) for i in range(n_sub)]
    while live:
        live = [g for g in live if next(g, live) is not live]

    tail = ext_ref[lt:lt + HIST, :]
    ext_ref[0:HIST, :] = tail
    tout_ref[0] = tail


def _premix(x, tails, wts, *, lt, b0, bsz):
    _, seq, d = x.shape
    assert seq % lt == 0
    grid = (bsz, seq // lt)
    full = lambda a: pl.BlockSpec(a.shape, lambda b, j: (0,) * a.ndim)
    tok = lambda w: pl.BlockSpec((1, lt, w), lambda b, j: (b, j, 0))
    x_spec = pl.BlockSpec((1, lt, d), lambda b, j: (b + b0, j, 0))
    (wa, wq, wz, wbd, wbdt, cw, cnw, gcw, prow, pcol) = wts
    ext_w = CONV_WIDTH + 3 * GDN_WIDTH
    out_shape = (
        jax.ShapeDtypeStruct((bsz, seq, CONV_WIDTH), _BF16),
        jax.ShapeDtypeStruct((bsz, seq, GDN_WIDTH), _BF16),
        jax.ShapeDtypeStruct((bsz, seq, GDN_WIDTH), _BF16),
        jax.ShapeDtypeStruct((bsz, seq, GDN_WIDTH), _BF16),
        jax.ShapeDtypeStruct((bsz, seq, GDN_WIDTH), _BF16),
        jax.ShapeDtypeStruct((bsz, seq, 128), _F32),
        jax.ShapeDtypeStruct((bsz, 8, seq), _F32),
        jax.ShapeDtypeStruct((bsz, HIST, ext_w), _F32),
    )
    out_specs = (tok(CONV_WIDTH), tok(GDN_WIDTH), tok(GDN_WIDTH), tok(GDN_WIDTH), tok(GDN_WIDTH),
                 tok(128), pl.BlockSpec((1, 8, lt), lambda b, j: (b, 0, j)),
                 pl.BlockSpec((1, HIST, ext_w), lambda b, j: (b, 0, 0)))
    return pl.pallas_call(
        functools.partial(_premix_body, lt=lt),
        grid=grid,
        in_specs=[x_spec, full(tails)] + [full(w) for w in wts],
        out_specs=out_specs,
        out_shape=out_shape,
        scratch_shapes=[pltpu.VMEM((HIST + lt, ext_w), _F32)],
        compiler_params=_cparams("arbitrary", "arbitrary"),
        name="premix",
    )(x, tails, *wts)


def _cumsum_rows(x):
    row = lax.broadcasted_iota(_I32, x.shape, 0)
    s = 1
    while s < x.shape[0]:
        x = x + jnp.where(row >= s, pltpu.roll(x, s, 0), 0.0)
        s *= 2
    return x


def _cumsum_lanes_seg(x):
    lane = lax.broadcasted_iota(_I32, x.shape, 1) & (CHUNK - 1)
    s = 1
    while s < CHUNK:
        x = x + jnp.where(lane >= s, pltpu.roll(x, s, 1), 0.0)
        s *= 2
    return x


def _stack_heads(a):
    return jnp.concatenate([a[:, h * GDN_HEAD_DIM:(h + 1) * GDN_HEAD_DIM] for h in range(GDN_HEADS)], axis=0)


def _gdn_body(q_ref, k_ref, v_ref, z_ref, bgc_ref, grow_ref, s0_ref, gnw_ref, after_ref,
              y_ref, sout_ref, s_ref, *, nc, nbb):
    del after_ref

    @pl.when(pl.program_id(1) == 0)
    def _():
        for r in range(nbb):
            s_ref[r] = s0_ref[...]

    ri = lax.broadcasted_iota(_I32, (STACK, STACK), 0)
    ci = lax.broadcasted_iota(_I32, (STACK, STACK), 1)
    same64 = (ri >> 6) == (ci >> 6)
    same32 = (ri >> 5) == (ci >> 5)
    same16 = (ri >> 4) == (ci >> 4)
    low_incl = same64 & (ri >= ci)
    low_strict = same64 & (ri > ci)
    gnw = gnw_ref[...]

    def chunk_row(r, c):
        off = pl.multiple_of(c * CHUNK, CHUNK)
        q_all = _stack_heads(q_ref[r, pl.ds(off, CHUNK), :].astype(_F32))
        k_all = _stack_heads(k_ref[r, pl.ds(off, CHUNK), :].astype(_F32))
        v_all = _stack_heads(v_ref[r, pl.ds(off, CHUNK), :].astype(_F32))
        bgc = bgc_ref[r, pl.ds(off, CHUNK), :]
        gcs = _cumsum_rows(bgc)
        hd = (CHUNK, GDN_HEAD_DIM)
        beta_b = jnp.concatenate(
            [jnp.broadcast_to(bgc[:, h:h + 1], hd) for h in range(GDN_HEADS)], axis=0)
        gc_b = jnp.concatenate(
            [jnp.broadcast_to(gcs[:, GDN_HEADS + h:GDN_HEADS + h + 1], hd) for h in range(GDN_HEADS)], axis=0)
        gl = [gcs[CHUNK - 1:CHUNK, GDN_HEADS + h:GDN_HEADS + h + 1] for h in range(GDN_HEADS)]
        gl_b = jnp.concatenate([jnp.broadcast_to(g1, hd) for g1 in gl], axis=0)
        gcr = _cumsum_lanes_seg(jnp.broadcast_to(grow_ref[r, c], (8, STACK)))[0:1, :]

        diff = jnp.concatenate([gc_b, gc_b], axis=1) - gcr
        decay = jnp.exp(jnp.where(low_incl, diff, -1e30))
        kb = k_all * beta_b
        a1 = _mm_nt(jnp.concatenate([kb, q_all], axis=0), k_all)
        yield
        m = jnp.where(low_strict, a1[:STACK] * decay, 0.0)
        attn = a1[STACK:] * decay

        l16 = jnp.where(same16, m, 0.0)
        c1 = jnp.where(same32 & jnp.logical_not(same16), m, 0.0)
        c2 = jnp.where(same32, 0.0, m)
        p2 = _mm(l16, l16)
        yield
        p4 = _mm(p2, p2)
        t = _mm(l16, p2)
        yield
        na = p2 - l16 - t
        p8 = _mm(p4, p4)
        t = _mm(na, p4)
        yield
        nb = na + p4 + t
        t = _mm(nb, p8)
        yield
        ncm = nb + p8 + t
        t = _mm(c1, ncm)
        yield
        y1 = c1 + t
        t = _mm(ncm, y1)
        yield
        n1 = ncm - y1 - t
        t = _mm(c2, n1)
        yield
        y2 = c2 + t
        t = _mm(n1, y2)
        yield
        nt = n1 - y2 - t

        egc = jnp.exp(gc_b)
        rhs = jnp.concatenate([v_all * beta_b, kb * egc], axis=1)
        t = _mm(nt, rhs)
        yield
        uw = rhs + t
        u_all = uw[:, :GDN_HEAD_DIM]
        w_all = uw[:, GDN_HEAD_DIM:]
        qd = q_all * egc
        kd = k_all * jnp.exp(gl_b - gc_b)

        bs = []
        for h in range(GDN_HEADS):
            r0, r1 = h * CHUNK, (h + 1) * CHUNK
            bs.append(_mm(jnp.concatenate([w_all[r0:r1], qd[r0:r1]], axis=0), s_ref[r, h]))
        yield
        vn = [u_all[h * CHUNK:(h + 1) * CHUNK] - bs[h][:CHUNK] for h in range(GDN_HEADS)]
        vn_all = jnp.concatenate(vn, axis=0)
        t = _mm(attn, vn_all)
        ds = [_mm_tn(kd[h * CHUNK:(h + 1) * CHUNK], vn[h]) for h in range(GDN_HEADS)]
        yield
        o_all = jnp.concatenate([b[CHUNK:] for b in bs], axis=0) + t
        for h in range(GDN_HEADS):
            r0, r1 = h * CHUNK, (h + 1) * CHUNK
            s_ref[r, h] = s_ref[r, h] * jnp.exp(gl[h]) + ds[h]
            o = o_all[r0:r1]
            zz = z_ref[r, pl.ds(off, CHUNK), h * GDN_HEAD_DIM:(h + 1) * GDN_HEAD_DIM].astype(_F32)
            on = o * lax.rsqrt(jnp.mean(o * o, axis=-1, keepdims=True) + NORM_EPS) * gnw
            y_ref[r, pl.ds(off, CHUNK), h * GDN_HEAD_DIM:(h + 1) * GDN_HEAD_DIM] = (on * _silu(zz)).astype(_BF16)

    def chunk(c, carry):
        live = [chunk_row(r, c) for r in range(nbb)]
        while live:
            live = [g for g in live if next(g, live) is not live]
        return carry

    lax.fori_loop(0, nc, chunk, 0)
    sout_ref[...] = s_ref[...]


def _gdn(q, k, v, z, bgc, grow, s0, gnw, after, *, lg, nbb):
    bsz, seq, _ = q.shape
    assert seq % lg == 0 and lg % CHUNK == 0 and bsz % nbb == 0
    nc = lg // CHUNK
    tok = lambda w: pl.BlockSpec((nbb, lg, w), lambda b, j: (b, j, 0))
    full = lambda a: pl.BlockSpec(a.shape, lambda b, j: (0,) * a.ndim)
    st = (nbb, GDN_HEADS, GDN_HEAD_DIM, GDN_HEAD_DIM)
    return pl.pallas_call(
        functools.partial(_gdn_body, nc=nc, nbb=nbb),
        grid=(bsz // nbb, seq // lg),
        in_specs=[tok(GDN_WIDTH)] * 4 + [tok(128), pl.BlockSpec((nbb, nc, 1, STACK), lambda b, j: (b, j, 0, 0)),
                                           full(s0), full(gnw), pl.BlockSpec(memory_space=pl.ANY)],
        out_specs=(tok(GDN_WIDTH), pl.BlockSpec(st, lambda b, j: (b, 0, 0, 0))),
        out_shape=(jax.ShapeDtypeStruct((bsz, seq, GDN_WIDTH), _BF16),
                   jax.ShapeDtypeStruct((bsz, GDN_HEADS, GDN_HEAD_DIM, GDN_HEAD_DIM), _F32)),
        scratch_shapes=[pltpu.VMEM(st, _F32)],
        compiler_params=_cparams("arbitrary", "arbitrary"),
        name="gdn",
    )(q, k, v, z, bgc, grow, s0, gnw, after)


def _outproj_body(yc_ref, yg_ref, x_ref, wo_ref, g_ref, b_ref, h1_ref, h1p_ref):
    mix = (jnp.dot(yc_ref[...], wo_ref[0:CONV_WIDTH, :], preferred_element_type=_F32)
           + jnp.dot(yg_ref[...], wo_ref[CONV_WIDTH:, :], preferred_element_type=_F32))
    h1 = _layer_norm(DN_ALPHA * x_ref[...] + mix, g_ref[...], b_ref[...])
    h1_ref[...] = h1
    _store_rows(h1p_ref, _pack_halves(h1))


def _outproj(yc, yg, x2d, wo, g, b, *, tm, tile0):
    t = yc.shape[0]
    assert t % tm == 0
    row = lambda w: pl.BlockSpec((tm, w), lambda i: (i, 0))
    full = lambda a: pl.BlockSpec(a.shape, lambda i: (0,) * a.ndim)
    return pl.pallas_call(
        _outproj_body,
        grid=(t // tm,),
        in_specs=[row(CONV_WIDTH), row(GDN_WIDTH), pl.BlockSpec((tm, D_MODEL), lambda i: (i + tile0, 0)),
                  full(wo), full(g), full(b)],
        out_specs=(row(D_MODEL), pl.BlockSpec((tm * QUAD, 128), lambda i: (i, 0))),
        out_shape=(jax.ShapeDtypeStruct((t, D_MODEL), _F32), jax.ShapeDtypeStruct((t * QUAD, 128), jnp.uint32)),
        compiler_params=_cparams("arbitrary"),
        name="outproj",
    )(yc, yg, x2d, wo, g, b)


def _router_body(h1_ref, wh_ref, wl_ref, br_ref, idx_ref, gate_ref, rank_ref, cnt_ref, carry_ref, *, tt):
    @pl.when(pl.program_id(0) == 0)
    def _():
        carry_ref[...] = jnp.zeros_like(carry_ref)

    x = h1_ref[...]
    xh = x.astype(_BF16)
    xl = (x - xh.astype(_F32)).astype(_BF16)
    wh = wh_ref[...]
    logits = _mm_nt(wh, xh) + _mm_nt(wh, xl) + _mm_nt(wl_ref[...], xh)
    scores = _sigmoid(logits)
    sel = scores + br_ref[...]
    ninf = -jnp.inf

    r32 = lax.broadcasted_iota(_I32, (E_PER_GROUP, tt), 0)
    gsc = []
    for g in range(N_GROUPS):
        xg = sel[g * E_PER_GROUP:(g + 1) * E_PER_GROUP]
        m1 = jnp.max(xg, axis=0, keepdims=True)
        i1 = jnp.min(jnp.where(xg == m1, r32, E_PER_GROUP), axis=0, keepdims=True)
        m2 = jnp.max(jnp.where(r32 == i1, ninf, xg), axis=0, keepdims=True)
        gsc.append(m1 + m2)
    work = jnp.concatenate(gsc, axis=0)
    r8 = lax.broadcasted_iota(_I32, (N_GROUPS, tt), 0)
    gkeep = jnp.zeros((N_GROUPS, tt), _F32)
    for _ in range(TOPK_GROUPS):
        m = jnp.max(work, axis=0, keepdims=True)
        gi = jnp.min(jnp.where(work == m, r8, N_GROUPS), axis=0, keepdims=True)
        pick = r8 == gi
        gkeep = jnp.where(pick, 1.0, gkeep)
        work = jnp.where(pick, ninf, work)
    selm = jnp.concatenate(
        [jnp.where(gkeep[g:g + 1] > 0.5, sel[g * E_PER_GROUP:(g + 1) * E_PER_GROUP], ninf)
         for g in range(N_GROUPS)], axis=0)

    re = lax.broadcasted_iota(_I32, (N_EXPERTS, tt), 0)
    msel = jnp.zeros((N_EXPERTS, tt), _F32)
    idxs, gates = [], []
    for _ in range(TOP_K):
        m = jnp.max(selm, axis=0, keepdims=True)
        ii = jnp.min(jnp.where(selm == m, re, N_EXPERTS), axis=0, keepdims=True)
        hit = re == ii
        idxs.append(ii)
        gates.append(jnp.sum(jnp.where(hit, scores, 0.0), axis=0, keepdims=True))
        selm = jnp.where(hit, ninf, selm)
        msel = jnp.where(hit, 1.0, msel)
    gate = jnp.concatenate(gates, axis=0)
    gate_ref[...] = gate / jnp.sum(gate, axis=0, keepdims=True) * ROUTED_SCALE
    idx_ref[...] = jnp.concatenate(idxs, axis=0)

    ta = lax.broadcasted_iota(_I32, (tt, tt), 0)
    tb = lax.broadcasted_iota(_I32, (tt, tt), 1)
    earlier = jnp.where(ta < tb, 1.0, 0.0)
    carry = carry_ref[...]
    rank_all = _mm(msel, earlier) + carry[:, 0:1]
    rank_ref[...] = jnp.concatenate(
        [jnp.sum(jnp.where(re == ii, rank_all, 0.0), axis=0, keepdims=True) for ii in idxs],
        axis=0).astype(_I32)
    carry = carry + jnp.sum(msel, axis=1, keepdims=True)
    carry_ref[...] = carry
    cnt_ref[...] = carry


def _router(h1, wh, wl, br, *, tt, tile0, t):
    assert t % tt == 0
    full = lambda a: pl.BlockSpec(a.shape, lambda i: (0,) * a.ndim)
    kt = pl.BlockSpec((TOP_K, tt), lambda i: (0, i))
    return pl.pallas_call(
        functools.partial(_router_body, tt=tt),
        grid=(t // tt,),
        in_specs=[pl.BlockSpec((tt, D_MODEL), lambda i: (i + tile0, 0)), full(wh), full(wl), full(br)],
        out_specs=(kt, kt, kt, pl.BlockSpec((N_EXPERTS, 128), lambda i: (0, 0))),
        out_shape=(jax.ShapeDtypeStruct((TOP_K, t), _I32), jax.ShapeDtypeStruct((TOP_K, t), _F32),
                   jax.ShapeDtypeStruct((TOP_K, t), _I32), jax.ShapeDtypeStruct((N_EXPERTS, 128), _F32)),
        scratch_shapes=[pltpu.VMEM((N_EXPERTS, 128), _F32)],
        compiler_params=_cparams("arbitrary"),
        name="router",
    )(h1, wh, wl, br)


def _position_body(idx_ref, rank_ref, pstart_ref, pos_ref, *, tt):
    re = lax.broadcasted_iota(_I32, (N_EXPERTS, tt), 0)
    ps = pstart_ref[...]
    idx = idx_ref[...]
    rows = [jnp.sum(jnp.where(re == idx[k:k + 1], ps, 0), axis=0, keepdims=True) for k in range(TOP_K)]
    pos_ref[0] = jnp.concatenate(rows, axis=0) + rank_ref[...]


def _position(idx, rank, pstart, *, tt):
    t = idx.shape[1]
    kt = pl.BlockSpec((TOP_K, tt), lambda i: (0, i))
    return pl.pallas_call(
        functools.partial(_position_body, tt=tt),
        grid=(t // tt,),
        in_specs=[kt, kt, pl.BlockSpec(pstart.shape, lambda i: (0, 0))],
        out_specs=pl.BlockSpec((1, TOP_K, tt), lambda i: (i, 0, 0)),
        out_shape=jax.ShapeDtypeStruct((t // tt, TOP_K, tt), _I32),
        compiler_params=_cparams("arbitrary"),
        name="position",
    )(idx, rank, pstart)


def _ffn_body(blk0_ref, nblk_ref, ntot_ref, xs_hbm, wg_ref, wu_ref, wd_ref, after_a, after_b, ys_hbm,
              xbuf, ybuf, sem_in, sem_out, wgu_bf, wd_bf):
    del after_a, after_b
    e = pl.program_id(0)
    nblk = nblk_ref[e]
    blk0 = blk0_ref[e]
    ntot = ntot_ref[0]

    blk_rows = ROW_BLOCK * QUAD

    def rows(g):
        return pl.ds(pl.multiple_of(g * blk_rows, blk_rows), blk_rows)

    def in_start(g, slot):
        pltpu.make_async_copy(xs_hbm.at[rows(g)], xbuf.at[slot], sem_in.at[slot]).start()

    def in_wait(slot):
        pltpu.make_async_copy(xs_hbm.at[rows(0)], xbuf.at[slot], sem_in.at[slot]).wait()

    def out_start(g, slot):
        pltpu.make_async_copy(ybuf.at[slot], ys_hbm.at[rows(g)], sem_out.at[slot]).start()

    def out_wait(slot):
        pltpu.make_async_copy(ybuf.at[slot], ys_hbm.at[rows(0)], sem_out.at[slot]).wait()

    @pl.when(e == 0)
    def _():
        for i in range(IN_AHEAD):
            @pl.when(i < ntot)
            def _():
                in_start(i, i)

    @pl.when(nblk > 0)
    def _():
        wgu_bf[:, 0:EXPERT_FF] = wg_ref[0].astype(_BF16)
        wgu_bf[:, EXPERT_FF:] = wu_ref[0].astype(_BF16)
        wd_bf[...] = wd_ref[0].astype(_BF16)

        def acquire(g):
            slot = g & (RING - 1)
            in_wait(slot)

            @pl.when(g + IN_AHEAD < ntot)
            def _():
                in_start(g + IN_AHEAD, (g + IN_AHEAD) & (RING - 1))

            @pl.when(g >= RING)
            def _():
                out_wait(slot)

            return slot

        def compute(slot):
            lo, hi = _unpack_halves(_load_rows(xbuf.at[slot], ROW_BLOCK))
            a = jnp.dot(lo.astype(_BF16), wgu_bf[0:HALF, :], preferred_element_type=_F32)
            yield
            gu = a + jnp.dot(hi.astype(_BF16), wgu_bf[HALF:, :], preferred_element_type=_F32)
            yield
            h = (_silu(gu[:, :EXPERT_FF]) * gu[:, EXPERT_FF:]).astype(_BF16)
            y = jnp.dot(h, wd_bf[...], preferred_element_type=_F32)
            yield
            _store_rows(ybuf.at[slot], _pack_halves(y))

        def run(gs):
            slots = [acquire(g) for g in gs]
            live = [compute(s) for s in slots]
            while live:
                live = [c for c in live if next(c, live) is not live]
            for g, s in zip(gs, slots):
                out_start(g, s)

        def pair(j, carry):
            run([blk0 + 2 * j, blk0 + 2 * j + 1])
            return carry

        lax.fori_loop(0, nblk // 2, pair, 0)

        @pl.when((nblk & 1) == 1)
        def _():
            run([blk0 + nblk - 1])

    @pl.when(e == N_EXPERTS - 1)
    def _():
        for i in range(RING):
            @pl.when(i < ntot)
            def _():
                out_wait((ntot - 1 - i) & (RING - 1))


def _ffn(blk0, nblk, ntot, xs, wg, wu, wd, after_a, after_b):
    grid_spec = pltpu.PrefetchScalarGridSpec(
        num_scalar_prefetch=3,
        grid=(N_EXPERTS,),
        in_specs=[pl.BlockSpec(memory_space=pl.ANY),
                  pl.BlockSpec((1, D_MODEL, EXPERT_FF), lambda e, *_: (e, 0, 0)),
                  pl.BlockSpec((1, D_MODEL, EXPERT_FF), lambda e, *_: (e, 0, 0)),
                  pl.BlockSpec((1, EXPERT_FF, D_MODEL), lambda e, *_: (e, 0, 0)),
                  pl.BlockSpec(memory_space=pl.ANY), pl.BlockSpec(memory_space=pl.ANY)],
        out_specs=pl.BlockSpec(memory_space=pl.ANY),
        scratch_shapes=[pltpu.VMEM((RING, ROW_BLOCK * QUAD, 128), jnp.uint32),
                        pltpu.VMEM((RING, ROW_BLOCK * QUAD, 128), jnp.uint32),
                        pltpu.SemaphoreType.DMA((RING,)), pltpu.SemaphoreType.DMA((RING,)),
                        pltpu.VMEM((D_MODEL, 2 * EXPERT_FF), _BF16), pltpu.VMEM((EXPERT_FF, D_MODEL), _BF16)],
    )
    return pl.pallas_call(
        _ffn_body,
        grid_spec=grid_spec,
        out_shape=jax.ShapeDtypeStruct(xs.shape, jnp.uint32),
        compiler_params=_cparams("arbitrary"),
        name="ffn",
    )(blk0, nblk, ntot, xs, wg, wu, wd, after_a, after_b)


def _sc_position(idx, rank, pstart, after):
    n = idx.shape[0]
    per_w = n // SC_WORKERS
    assert per_w * SC_WORKERS == n and per_w % SC_LANES == 0
    mesh = plsc.VectorSubcoreMesh(core_axis_name="c", subcore_axis_name="s",
                                  num_cores=SC_CORES, num_subcores=SC_SUBCORES)

    @functools.partial(
        pl.kernel, mesh=mesh,
        out_type=jax.ShapeDtypeStruct((n,), _I32),
        scratch_types=[pltpu.VMEM((per_w,), _I32), pltpu.VMEM((per_w,), _I32), pltpu.VMEM((per_w,), _I32),
                       pltpu.VMEM((N_EXPERTS,), _I32)],
        compiler_params=pltpu.CompilerParams(needs_layout_passes=False),
        name="sc_position",
    )
    def position(idx_hbm, rank_hbm, ps_hbm, after_hbm, out_hbm, idx_v, rank_v, pos_v, ps_v):
        del after_hbm
        wid = lax.axis_index("s") * SC_CORES + lax.axis_index("c")
        mine = pl.ds(pl.multiple_of(wid * per_w, per_w), per_w)
        pltpu.sync_copy(ps_hbm, ps_v)
        pltpu.sync_copy(idx_hbm.at[mine], idx_v)
        pltpu.sync_copy(rank_hbm.at[mine], rank_v)

        @pl.loop(0, per_w, step=SC_LANES)
        def _(j):
            lanes = pl.ds(j, SC_LANES)
            pos_v[lanes] = plsc.load_gather(ps_v, [idx_v[lanes]]) + rank_v[lanes]

        pltpu.sync_copy(pos_v, out_hbm.at[mine])

    return position(idx, rank, pstart, after)


def _sc_gather(table, idx):
    b = idx.shape[0]
    nchunk = b // (SC_WORKERS * SC_CHUNK)
    assert nchunk * SC_WORKERS * SC_CHUNK == b and nchunk % SC_RING == 0
    idx2 = idx.reshape(SC_WORKERS * nchunk, SC_CHUNK)
    row = table.shape[1:]
    mesh = plsc.VectorSubcoreMesh(core_axis_name="c", subcore_axis_name="s",
                                  num_cores=SC_CORES, num_subcores=SC_SUBCORES)

    @functools.partial(
        pl.kernel, mesh=mesh,
        out_type=jax.ShapeDtypeStruct((b,) + row, table.dtype),
        scratch_types=[pltpu.VMEM((nchunk, SC_CHUNK), _I32), pltpu.VMEM((SC_RING, SC_CHUNK) + row, table.dtype),
                       pltpu.SemaphoreType.DMA((SC_RING,)), pltpu.SemaphoreType.DMA((SC_RING,))],
        name="sc_gather",
    )
    def gather(table_hbm, idx_hbm, out_hbm, idx_v, rows_v, sem_g, sem_w):
        wid = lax.axis_index("s") * SC_CORES + lax.axis_index("c")
        c0 = wid * nchunk
        pltpu.sync_copy(idx_hbm.at[pl.ds(pl.multiple_of(c0, nchunk), nchunk)], idx_v)

        def fetch(i, s):
            return pltpu.make_async_copy(table_hbm.at[idx_v.at[i]], rows_v.at[s], sem_g.at[s])

        def flush(i, s):
            rows = pl.ds(pl.multiple_of((c0 + i) * SC_CHUNK, SC_CHUNK), SC_CHUNK)
            return pltpu.make_async_copy(rows_v.at[s], out_hbm.at[rows], sem_w.at[s])

        for s in range(SC_RING):
            fetch(s, s).start()

        @pl.loop(0, nchunk, step=SC_RING)
        def _(g):
            for s in range(SC_RING):
                i = g + s
                fetch(i, s).wait()
                flush(i, s).start()
                flush(i, s).wait()

                @pl.when(i + SC_RING < nchunk)
                def _():
                    fetch(i + SC_RING, s).start()

    return gather(table, idx2)


def _sc_scatter(rows, pos3, n_out, row0):
    nchunk, nk, w = pos3.shape
    per_w = nchunk // SC_WORKERS
    assert per_w * SC_WORKERS == nchunk and w <= 128 and row0 % w == 0 and rows.shape[0] >= row0 + nchunk * w
    row = rows.shape[1:]
    mesh = plsc.VectorSubcoreMesh(core_axis_name="c", subcore_axis_name="s",
                                  num_cores=SC_CORES, num_subcores=SC_SUBCORES)

    @functools.partial(
        pl.kernel, mesh=mesh,
        out_type=(jax.ShapeDtypeStruct((n_out,) + row, rows.dtype),
                  jax.ShapeDtypeStruct((SC_WORKERS, nk, w), _I32)),
        scratch_types=[pltpu.VMEM((nk, w), _I32), pltpu.VMEM((w,) + row, rows.dtype), pltpu.SemaphoreType.DMA],
        name="sc_scatter",
    )
    def scatter(rows_hbm, pos_hbm, out_hbm, done_hbm, idx_v, rows_v, sem):
        wid = lax.axis_index("s") * SC_CORES + lax.axis_index("c")

        @pl.loop(0, per_w)
        def _(i):
            c = wid * per_w + i
            pltpu.sync_copy(pos_hbm.at[c], idx_v)
            pltpu.sync_copy(rows_hbm.at[pl.ds(pl.multiple_of(row0 + c * w, w), w)], rows_v)
            copies = [pltpu.async_copy(rows_v, out_hbm.at[idx_v.at[k]], sem) for k in range(nk)]
            for cp in copies:
                cp.wait()

        pltpu.sync_copy(idx_v, done_hbm.at[wid])

    return scatter(rows, pos3)


def _padfill_body(cnt_ref, pst_ref, pcn_ref, xs_in, xs_out, zbuf, zsem):
    del xs_in
    zbuf[...] = jnp.zeros_like(zbuf)

    def pad_runs(e, act):
        pad = pcn_ref[e] - cnt_ref[e]
        base = pst_ref[e] + cnt_ref[e]
        for b in range(ROW_BLOCK.bit_length() - 1):
            n = 1 << b

            @pl.when(((pad >> b) & 1) == 1)
            def _():
                off = base + (pad & (n - 1))
                act(pltpu.make_async_copy(zbuf.at[pl.ds(0, QUAD * n)],
                                          xs_out.at[pl.ds(QUAD * off, QUAD * n)], zsem))

    def start_all(e, c):
        pad_runs(e, lambda d: d.start())
        return c

    def wait_all(e, c):
        pad_runs(e, lambda d: d.wait())
        return c

    lax.fori_loop(0, N_EXPERTS, start_all, 0)
    lax.fori_loop(0, N_EXPERTS, wait_all, 0)


def _padfill(counts, pstarts, pcounts, xs):
    grid_spec = pltpu.PrefetchScalarGridSpec(
        num_scalar_prefetch=3,
        grid=(1,),
        in_specs=[pl.BlockSpec(memory_space=pl.ANY)],
        out_specs=pl.BlockSpec(memory_space=pl.ANY),
        scratch_shapes=[pltpu.VMEM((QUAD * ROW_BLOCK // 2, 128), jnp.uint32), pltpu.SemaphoreType.DMA],
    )
    return pl.pallas_call(
        _padfill_body,
        grid_spec=grid_spec,
        out_shape=jax.ShapeDtypeStruct(xs.shape, xs.dtype),
        input_output_aliases={3: 0},
        compiler_params=_cparams("arbitrary"),
        name="padfill",
    )(counts, pstarts, pcounts, xs)


def _combine_stream_body(gate_ref, h1_ref, yg_ref, wsg_ref, wsu_ref, wsd_ref, g_ref, b_ref, out_ref, *, tt):
    x = h1_ref[...]
    xb = x.astype(_BF16)
    shared = _mm(_silu(_mm(xb, wsg_ref[...])) * _mm(xb, wsu_ref[...]), wsd_ref[...])
    gcol = gate_ref[...].T
    acc_lo = jnp.zeros((tt, HALF), _F32)
    acc_hi = jnp.zeros((tt, HALF), _F32)
    for k in range(TOP_K):
        lo, hi = _unpack_halves(_load_rows(yg_ref.at[0, k], tt))
        acc_lo = acc_lo + gcol[:, k:k + 1] * lo
        acc_hi = acc_hi + gcol[:, k:k + 1] * hi
    routed = jnp.concatenate([acc_lo, acc_hi], axis=1)
    out_ref[...] = _layer_norm(DN_ALPHA * x + (routed + shared), g_ref[...], b_ref[...])


def _combine_stream(gate, h1, yg, wsg, wsu, wsd, g, b, out_prev, *, tt, tile0, t_all):
    t = gate.shape[1]
    full = lambda a: pl.BlockSpec(a.shape, lambda i: (0,) * a.ndim)
    in_specs = [pl.BlockSpec((TOP_K, tt), lambda i: (0, i)), pl.BlockSpec((tt, D_MODEL), lambda i: (i, 0)),
                pl.BlockSpec((1, TOP_K, tt * QUAD, 128), lambda i: (i, 0, 0, 0)),
                full(wsg), full(wsu), full(wsd), full(g), full(b)]
    args = [gate, h1, yg, wsg, wsu, wsd, g, b]
    aliases = {}
    body = functools.partial(_combine_stream_body, tt=tt)
    if out_prev is not None:
        in_specs.append(pl.BlockSpec(memory_space=pl.ANY))
        args.append(out_prev)
        aliases = {len(args) - 1: 0}
        body = lambda *refs: _combine_stream_body(*refs[:8], refs[9], tt=tt)
    return pl.pallas_call(
        body,
        grid=(t // tt,),
        in_specs=in_specs,
        out_specs=pl.BlockSpec((tt, D_MODEL), lambda i: (i + tile0, 0)),
        out_shape=jax.ShapeDtypeStruct((t_all, D_MODEL), _F32),
        input_output_aliases=aliases,
        compiler_params=_cparams("arbitrary"),
        name="combine",
    )(*args)


def _pick(n, pref):
    t = min(n, pref)
    while n % t:
        t -= CHUNK
    return t


def _delta(pre, s0, gnw, after, *, lg, nbb):
    yc, q, k, v, z, bgc, bgr, tails_out = pre
    bsz, seq, _ = q.shape
    nch = seq // CHUNK
    grow = bgr[:, GDN_HEADS:2 * GDN_HEADS, :].reshape(bsz, GDN_HEADS, nch, CHUNK)
    grow = grow.transpose(0, 2, 1, 3).reshape(bsz, nch, 1, STACK)
    yg, s_out = _gdn(q, k, v, z, bgc, grow, s0, gnw, after, lg=lg, nbb=nbb)
    return yc, yg, tails_out, s_out


def kernel(x, meta_tokens, w_in, conv_w, conv_norm_w, gdn_conv_w, a_log, dt_bias, gdn_norm_w, w_out,
           ln1_g, ln1_b, w_router, b_router, w_gate, w_up, w_down, ws_gate, ws_up, ws_down, ln2_g, ln2_b):
    assert w_in.shape[0] == 1, "single-layer stack"
    bsz, seq, d = x.shape
    assert d == D_MODEL and seq % CHUNK == 0
    c, gw = CONV_WIDTH, GDN_WIDTH
    win = w_in[0].astype(_BF16)
    wbd = win[:, 3 * c + 4 * gw:]
    zpad = jnp.zeros((128 - 2 * GDN_HEADS,), _F32)
    zpad4 = jnp.zeros((GDN_HEADS,), _F32)
    prow = jnp.zeros((8, 128), _F32)
    prow = prow.at[0].set(jnp.concatenate([zpad4, a_log[0], zpad]))
    prow = prow.at[1].set(jnp.concatenate([zpad4, dt_bias[0], zpad]))
    wts = (win[:, :3 * c], win[:, 3 * c:3 * c + 3 * gw], win[:, 3 * c + 3 * gw:3 * c + 4 * gw],
           jnp.pad(wbd, ((0, 0), (0, 128 - 2 * GDN_HEADS))), wbd.T,
           conv_w[0], conv_norm_w, gdn_conv_w[0], prow, prow.T[:8])
    gnw = gdn_norm_w

    meta = jnp.concatenate([jnp.zeros((CHUNK - N_META, d), x.dtype), meta_tokens.astype(x.dtype)])[None]
    tails0 = jnp.zeros((HIST, c + 3 * gw), _F32)
    s00 = jnp.zeros((GDN_HEADS, GDN_HEAD_DIM, GDN_HEAD_DIM), _F32)
    _, _, tails_m, s_m = _delta(_premix(meta, tails0, wts, lt=CHUNK, b0=0, bsz=1), s00, gnw, s00, lg=CHUNK, nbb=1)

    t = bsz * seq
    tm = _pick(seq, 512)
    tt = _pick(seq, 256)
    wo = w_out[0].astype(_BF16)
    wr_t = w_router[0].T
    wr_hi = wr_t.astype(_BF16)
    wr_lo = (wr_t - wr_hi.astype(_F32)).astype(_BF16)
    shared_w = (ws_gate[0].astype(_BF16), ws_up[0].astype(_BF16), ws_down[0].astype(_BF16))
    x2d = x.reshape(t, d)

    parts = LAYER_PARTS if bsz % LAYER_PARTS == 0 and (bsz // LAYER_PARTS * seq) % (tt * SC_WORKERS) == 0 else 1
    bp = bsz // parts
    tp = bp * seq
    nb = tp * TOP_K // ROW_BLOCK + N_EXPERTS
    nbb = GDN_ROWS if bp % GDN_ROWS == 0 else 1

    def premix(part):
        return _premix(x, tails_m[0], wts, lt=_pick(seq, 512), b0=part * bp, bsz=bp)

    def delta(part, pre, after):
        yc, yg, _, _ = _delta(pre, s_m[0], gnw, after, lg=_pick(seq, 512), nbb=nbb)
        return _outproj(yc.reshape(tp, c), yg.reshape(tp, gw), x2d, wo, ln1_g, ln1_b, tm=tm, tile0=part * (tp // tm))

    def route(h1, h1p, after):
        idx, gate, rank, cnt = _router(h1, wr_hi, wr_lo, b_router[0][:, None], tt=tt, tile0=0, t=tp)
        counts = cnt[:, 0].astype(_I32)
        pcounts = (counts + ROW_BLOCK - 1) // ROW_BLOCK * ROW_BLOCK
        pends = jnp.cumsum(pcounts)
        pstarts = (pends - pcounts).astype(_I32)
        tiled = lambda a: a.reshape(TOP_K, tp // tt, tt).transpose(1, 0, 2).reshape(tp * TOP_K)
        pos = _sc_position(tiled(idx), tiled(rank), pstarts, after)
        pos = pos.reshape(tp // tt, TOP_K, tt)
        nwin = tt // SC_WINDOW
        pos3 = pos.reshape(tp // tt, TOP_K, nwin, SC_WINDOW).transpose(0, 2, 1, 3)
        pos3 = pos3.reshape(tp // SC_WINDOW, TOP_K, SC_WINDOW)
        xs, done = _sc_scatter(h1p.reshape(tp, QUAD, 128), pos3, nb * ROW_BLOCK, 0)
        xs = _padfill(counts, pstarts, pcounts.astype(_I32), xs.reshape(nb * ROW_BLOCK * QUAD, 128))
        blocks = ((pstarts // ROW_BLOCK).astype(_I32), (pcounts // ROW_BLOCK).astype(_I32),
                  (pends[-1:] // ROW_BLOCK).astype(_I32))
        return xs, blocks, pos, gate, done

    def experts(xs, blocks, pos, after_a, after_b):
        ys = _ffn(*blocks, xs, w_gate[0], w_up[0], w_down[0], after_a, after_b)
        yg = _sc_gather(ys.reshape(nb * ROW_BLOCK, QUAD, 128), pos.reshape(tp * TOP_K))
        return yg.reshape(tp // tt, TOP_K, tt * QUAD, 128), ys

    def combine(part, gate, h1, yg, out):
        return _combine_stream(gate, h1, yg, *shared_w, ln2_g, ln2_b, out, tt=tt, tile0=part * (tp // tt), t_all=t)

    out = None
    done = jnp.zeros((SC_WORKERS, TOP_K, SC_WINDOW), _I32)
    h1, h1p = delta(0, premix(0), s00)
    routed = route(h1, h1p, done)
    for part in range(parts):
        xs, blocks, pos, gate, done = routed
        h1_cur = h1
        last = part + 1 == parts
        pre = None if last else premix(part + 1)
        yg, ys = experts(xs, blocks, pos, done if last else pre[0], done if out is None else out)
        if not last:
            h1, h1p = delta(part + 1, pre, ys)
            routed = route(h1, h1p, yg)
        out = combine(part, gate, h1_cur, yg, out)
    return out.reshape(bsz, seq, d)
```

```python
import functools

import jax
import jax.numpy as jnp
from jax import lax
from jax.experimental import pallas as pl
from jax.experimental.pallas import tpu as pltpu
from jax.experimental.pallas import tpu_sc as plsc

_F32 = jnp.float32
_BF16 = jnp.bfloat16
_I32 = jnp.int32

D_MODEL = 1024
N_META = 16
CONV_WIDTH = 512
CONV_K = 3
GDN_HEADS = 4
GDN_HEAD_DIM = 128
GDN_WIDTH = GDN_HEADS * GDN_HEAD_DIM
GDN_CONV_K = 4
CHUNK = 64
N_EXPERTS = 256
TOP_K = 8
N_GROUPS = 8
TOPK_GROUPS = 4
E_PER_GROUP = N_EXPERTS // N_GROUPS
EXPERT_FF = 256
ROUTED_SCALE = 2.5
ROW_BLOCK = 256
DN_ALPHA = 2.0 ** 0.25
NORM_EPS = 1e-5
HALF = D_MODEL // 2
QUAD = HALF // 128
STACK = GDN_HEADS * CHUNK
HIST = 8
GDN_ROWS = 4
PREMIX_SUB = 2
SC_CORES = 2
SC_SUBCORES = 16
SC_WORKERS = SC_CORES * SC_SUBCORES
SC_LANES = 16
SC_CHUNK = 64
SC_RING = 2
SC_WINDOW = 128
LAYER_PARTS = 2
TAIL_SPLIT = 2
RING = 8
IN_AHEAD = RING - 2

V7X_VMEM_BYTES = 64 * 1024 * 1024
VMEM_LIMIT = V7X_VMEM_BYTES - 8 * 1024 * 1024


def _cparams(*sem):
    return pltpu.CompilerParams(dimension_semantics=sem, vmem_limit_bytes=VMEM_LIMIT)


def _mm(a, b):
    return jnp.dot(a.astype(_BF16), b.astype(_BF16), preferred_element_type=_F32)


def _mm_nt(a, b):
    return lax.dot_general(a.astype(_BF16), b.astype(_BF16), (((1,), (1,)), ((), ())),
                           preferred_element_type=_F32)


def _mm_tn(a, b):
    return lax.dot_general(a.astype(_BF16), b.astype(_BF16), (((0,), (0,)), ((), ())),
                           preferred_element_type=_F32)


def _sigmoid(x):
    return 1.0 / (1.0 + jnp.exp(-x))


def _silu(x):
    return x * _sigmoid(x)


def _softplus(x):
    return jnp.maximum(x, 0.0) + jnp.log1p(jnp.exp(-jnp.abs(x)))


def _pack_halves(y):
    return pltpu.pack_elementwise([y[:, :HALF], y[:, HALF:]], packed_dtype=_BF16)


def _store_rows(ref, packed):
    r = packed.shape[0]
    for c in range(QUAD):
        ref[pl.ds(c, r, stride=QUAD), :] = packed[:, c * 128:(c + 1) * 128]


def _load_rows(ref, r):
    return jnp.concatenate([ref[pl.ds(c, r, stride=QUAD), :] for c in range(QUAD)], axis=1)


def _unpack_halves(p):
    lo = pltpu.unpack_elementwise(p, index=0, packed_dtype=_BF16, unpacked_dtype=_F32)
    hi = pltpu.unpack_elementwise(p, index=1, packed_dtype=_BF16, unpacked_dtype=_F32)
    return lo, hi


def _layer_norm(h, g, b):
    mu = jnp.mean(h, axis=-1, keepdims=True)
    d = h - mu
    var = jnp.mean(d * d, axis=-1, keepdims=True)
    return d * lax.rsqrt(var + NORM_EPS) * g + b


def _premix_body(x_ref, tails_ref, wa_ref, wq_ref, wz_ref, wbd_ref, wbdt_ref, cw_ref, cnw_ref,
                 gcw_ref, prow_ref, pcol_ref,
                 yc_ref, q_ref, k_ref, v_ref, z_ref, bgc_ref, bgr_ref, tout_ref, ext_ref, *, lt):
    cw_ = CONV_WIDTH

    @pl.when(pl.program_id(1) == 0)
    def _():
        ext_ref[0:HIST, :] = tails_ref[...]

    cw = cw_ref[...]
    gcw = gcw_ref[...]
    prow = prow_ref[...]
    pcol = pcol_ref[...]

    def sub_tile(r0, n):
        rows = slice(r0, r0 + n)
        erows = slice(HIST + r0, HIST + r0 + n)
        xb = x_ref[0, rows, :].astype(_BF16)
        pa = jnp.dot(xb, wa_ref[...], preferred_element_type=_F32)
        yield
        gate_b = pa[:, 0:cw_]
        u = pa[:, cw_:2 * cw_] * pa[:, 2 * cw_:3 * cw_]
        ext_ref[erows, 0:cw_] = u
        pq = jnp.dot(xb, wq_ref[...], preferred_element_type=_F32)
        yield
        ext_ref[erows, cw_:] = pq
        zz = jnp.dot(xb, wz_ref[...], preferred_element_type=_F32)
        bdc = jnp.dot(xb, wbd_ref[...], preferred_element_type=_F32)
        bdr = _mm_nt(wbdt_ref[...], xb)
        yield

        ca = u * cw[CONV_K - 1:CONV_K, :]
        for j in range(CONV_K - 1):
            ca = ca + ext_ref[pl.ds(HIST + r0 - (CONV_K - 1) + j, n), 0:cw_] * cw[j:j + 1, :]
        yc = gate_b * ca
        ms = jnp.mean(yc * yc, axis=-1, keepdims=True)
        yc_ref[0, rows, :] = (yc * lax.rsqrt(ms + NORM_EPS) * cnw_ref[...]).astype(_BF16)

        cq = pq * gcw[GDN_CONV_K - 1:GDN_CONV_K, :]
        for j in range(GDN_CONV_K - 1):
            cq = cq + ext_ref[pl.ds(HIST + r0 - (GDN_CONV_K - 1) + j, n), cw_:] * gcw[j:j + 1, :]
        s = _silu(cq)
        for h in range(GDN_HEADS):
            lo, hi = h * GDN_HEAD_DIM, (h + 1) * GDN_HEAD_DIM
            qh = s[:, lo:hi]
            kh = s[:, GDN_WIDTH + lo:GDN_WIDTH + hi]
            qn = qh * lax.rsqrt(jnp.sum(qh * qh, axis=-1, keepdims=True) + 1e-6)
            kn = kh * lax.rsqrt(jnp.sum(kh * kh, axis=-1, keepdims=True) + 1e-6)
            q_ref[0, rows, lo:hi] = (qn * (GDN_HEAD_DIM ** -0.5)).astype(_BF16)
            k_ref[0, rows, lo:hi] = kn.astype(_BF16)
        v_ref[0, rows, :] = s[:, 2 * GDN_WIDTH:].astype(_BF16)
        z_ref[0, rows, :] = zz.astype(_BF16)

        g_c = -jnp.exp(prow[0:1, :]) * _softplus(bdc + prow[1:2, :])
        lane = lax.broadcasted_iota(_I32, bdc.shape, 1)
        bgc_ref[0, rows, :] = jnp.where(lane < GDN_HEADS, _sigmoid(bdc), g_c)
        g_r = -jnp.exp(pcol[:, 0:1]) * _softplus(bdr + pcol[:, 1:2])
        row = lax.broadcasted_iota(_I32, bdr.shape, 0)
        bgr_ref[0, :, rows] = jnp.where(row < GDN_HEADS, _sigmoid(bdr), g_r)

    n_sub = PREMIX_SUB if lt % (PREMIX_SUB * 128) == 0 else 1
    live = [sub_tile(i * (lt // n_sub), lt // n_sub) for i in range(n_sub)]
    while live:
        live = [g for g in live if next(g, live) is not live]

    tail = ext_ref[lt:lt + HIST, :]
    ext_ref[0:HIST, :] = tail
    tout_ref[0] = tail


def _premix(x, tails, wts, *, lt, b0, bsz):
    _, seq, d = x.shape
    assert seq % lt == 0
    grid = (bsz, seq // lt)
    full = lambda a: pl.BlockSpec(a.shape, lambda b, j: (0,) * a.ndim)
    tok = lambda w: pl.BlockSpec((1, lt, w), lambda b, j: (b, j, 0))
    x_spec = pl.BlockSpec((1, lt, d), lambda b, j: (b + b0, j, 0))
    (wa, wq, wz, wbd, wbdt, cw, cnw, gcw, prow, pcol) = wts
    ext_w = CONV_WIDTH + 3 * GDN_WIDTH
    out_shape = (
        jax.ShapeDtypeStruct((bsz, seq, CONV_WIDTH), _BF16),
        jax.ShapeDtypeStruct((bsz, seq, GDN_WIDTH), _BF16),
        jax.ShapeDtypeStruct((bsz, seq, GDN_WIDTH), _BF16),
        jax.ShapeDtypeStruct((bsz, seq, GDN_WIDTH), _BF16),
        jax.ShapeDtypeStruct((bsz, seq, GDN_WIDTH), _BF16),
        jax.ShapeDtypeStruct((bsz, seq, 128), _F32),
        jax.ShapeDtypeStruct((bsz, 8, seq), _F32),
        jax.ShapeDtypeStruct((bsz, HIST, ext_w), _F32),
    )
    out_specs = (tok(CONV_WIDTH), tok(GDN_WIDTH), tok(GDN_WIDTH), tok(GDN_WIDTH), tok(GDN_WIDTH),
                 tok(128), pl.BlockSpec((1, 8, lt), lambda b, j: (b, 0, j)),
                 pl.BlockSpec((1, HIST, ext_w), lambda b, j: (b, 0, 0)))
    return pl.pallas_call(
        functools.partial(_premix_body, lt=lt),
        grid=grid,
        in_specs=[x_spec, full(tails)] + [full(w) for w in wts],
        out_specs=out_specs,
        out_shape=out_shape,
        scratch_shapes=[pltpu.VMEM((HIST + lt, ext_w), _F32)],
        compiler_params=_cparams("arbitrary", "arbitrary"),
        name="premix",
    )(x, tails, *wts)


def _cumsum_rows(x):
    row = lax.broadcasted_iota(_I32, x.shape, 0)
    s = 1
    while s < x.shape[0]:
        x = x + jnp.where(row >= s, pltpu.roll(x, s, 0), 0.0)
        s *= 2
    return x


def _cumsum_lanes_seg(x):
    lane = lax.broadcasted_iota(_I32, x.shape, 1) & (CHUNK - 1)
    s = 1
    while s < CHUNK:
        x = x + jnp.where(lane >= s, pltpu.roll(x, s, 1), 0.0)
        s *= 2
    return x


def _stack_heads(a):
    return jnp.concatenate([a[:, h * GDN_HEAD_DIM:(h + 1) * GDN_HEAD_DIM] for h in range(GDN_HEADS)], axis=0)


def _gdn_body(q_ref, k_ref, v_ref, z_ref, bgc_ref, grow_ref, s0_ref, gnw_ref, after_ref,
              y_ref, sout_ref, s_ref, *, nc, nbb):
    del after_ref

    @pl.when(pl.program_id(1) == 0)
    def _():
        for r in range(nbb):
            s_ref[r] = s0_ref[...]

    ri = lax.broadcasted_iota(_I32, (STACK, STACK), 0)
    ci = lax.broadcasted_iota(_I32, (STACK, STACK), 1)
    same64 = (ri >> 6) == (ci >> 6)
    same32 = (ri >> 5) == (ci >> 5)
    same16 = (ri >> 4) == (ci >> 4)
    low_incl = same64 & (ri >= ci)
    low_strict = same64 & (ri > ci)
    gnw = gnw_ref[...]

    def chunk_row(r, c):
        off = pl.multiple_of(c * CHUNK, CHUNK)
        q_all = _stack_heads(q_ref[r, pl.ds(off, CHUNK), :].astype(_F32))
        k_all = _stack_heads(k_ref[r, pl.ds(off, CHUNK), :].astype(_F32))
        v_all = _stack_heads(v_ref[r, pl.ds(off, CHUNK), :].astype(_F32))
        bgc = bgc_ref[r, pl.ds(off, CHUNK), :]
        gcs = _cumsum_rows(bgc)
        hd = (CHUNK, GDN_HEAD_DIM)
        beta_b = jnp.concatenate(
            [jnp.broadcast_to(bgc[:, h:h + 1], hd) for h in range(GDN_HEADS)], axis=0)
        gc_b = jnp.concatenate(
            [jnp.broadcast_to(gcs[:, GDN_HEADS + h:GDN_HEADS + h + 1], hd) for h in range(GDN_HEADS)], axis=0)
        gl = [gcs[CHUNK - 1:CHUNK, GDN_HEADS + h:GDN_HEADS + h + 1] for h in range(GDN_HEADS)]
        gl_b = jnp.concatenate([jnp.broadcast_to(g1, hd) for g1 in gl], axis=0)
        gcr = _cumsum_lanes_seg(jnp.broadcast_to(grow_ref[r, c], (8, STACK)))[0:1, :]

        diff = jnp.concatenate([gc_b, gc_b], axis=1) - gcr
        decay = jnp.exp(jnp.where(low_incl, diff, -1e30))
        kb = k_all * beta_b
        a1 = _mm_nt(jnp.concatenate([kb, q_all], axis=0), k_all)
        yield
        m = jnp.where(low_strict, a1[:STACK] * decay, 0.0)
        attn = a1[STACK:] * decay

        l16 = jnp.where(same16, m, 0.0)
        c1 = jnp.where(same32 & jnp.logical_not(same16), m, 0.0)
        c2 = jnp.where(same32, 0.0, m)
        p2 = _mm(l16, l16)
        yield
        p4 = _mm(p2, p2)
        t = _mm(l16, p2)
        yield
        na = p2 - l16 - t
        p8 = _mm(p4, p4)
        t = _mm(na, p4)
        yield
        nb = na + p4 + t
        t = _mm(nb, p8)
        yield
        ncm = nb + p8 + t
        t = _mm(c1, ncm)
        yield
        y1 = c1 + t
        t = _mm(ncm, y1)
        yield
        n1 = ncm - y1 - t
        t = _mm(c2, n1)
        yield
        y2 = c2 + t
        t = _mm(n1, y2)
        yield
        nt = n1 - y2 - t

        egc = jnp.exp(gc_b)
        rhs = jnp.concatenate([v_all * beta_b, kb * egc], axis=1)
        t = _mm(nt, rhs)
        yield
        uw = rhs + t
        u_all = uw[:, :GDN_HEAD_DIM]
        w_all = uw[:, GDN_HEAD_DIM:]
        qd = q_all * egc
        kd = k_all * jnp.exp(gl_b - gc_b)

        bs = []
        for h in range(GDN_HEADS):
            r0, r1 = h * CHUNK, (h + 1) * CHUNK
            bs.append(_mm(jnp.concatenate([w_all[r0:r1], qd[r0:r1]], axis=0), s_ref[r, h]))
        yield
        vn = [u_all[h * CHUNK:(h + 1) * CHUNK] - bs[h][:CHUNK] for h in range(GDN_HEADS)]
        vn_all = jnp.concatenate(vn, axis=0)
        t = _mm(attn, vn_all)
        ds = [_mm_tn(kd[h * CHUNK:(h + 1) * CHUNK], vn[h]) for h in range(GDN_HEADS)]
        yield
        o_all = jnp.concatenate([b[CHUNK:] for b in bs], axis=0) + t
        for h in range(GDN_HEADS):
            r0, r1 = h * CHUNK, (h + 1) * CHUNK
            s_ref[r, h] = s_ref[r, h] * jnp.exp(gl[h]) + ds[h]
            o = o_all[r0:r1]
            zz = z_ref[r, pl.ds(off, CHUNK), h * GDN_HEAD_DIM:(h + 1) * GDN_HEAD_DIM].astype(_F32)
            on = o * lax.rsqrt(jnp.mean(o * o, axis=-1, keepdims=True) + NORM_EPS) * gnw
            y_ref[r, pl.ds(off, CHUNK), h * GDN_HEAD_DIM:(h + 1) * GDN_HEAD_DIM] = (on * _silu(zz)).astype(_BF16)

    def chunk(c, carry):
        live = [chunk_row(r, c) for r in range(nbb)]
        while live:
            live = [g for g in live if next(g, live) is not live]
        return carry

    lax.fori_loop(0, nc, chunk, 0)
    sout_ref[...] = s_ref[...]


def _gdn(q, k, v, z, bgc, grow, s0, gnw, after, *, lg, nbb):
    bsz, seq, _ = q.shape
    assert seq % lg == 0 and lg % CHUNK == 0 and bsz % nbb == 0
    nc = lg // CHUNK
    tok = lambda w: pl.BlockSpec((nbb, lg, w), lambda b, j: (b, j, 0))
    full = lambda a: pl.BlockSpec(a.shape, lambda b, j: (0,) * a.ndim)
    st = (nbb, GDN_HEADS, GDN_HEAD_DIM, GDN_HEAD_DIM)
    return pl.pallas_call(
        functools.partial(_gdn_body, nc=nc, nbb=nbb),
        grid=(bsz // nbb, seq // lg),
        in_specs=[tok(GDN_WIDTH)] * 4 + [tok(128), pl.BlockSpec((nbb, nc, 1, STACK), lambda b, j: (b, j, 0, 0)),
                                           full(s0), full(gnw), pl.BlockSpec(memory_space=pl.ANY)],
        out_specs=(tok(GDN_WIDTH), pl.BlockSpec(st, lambda b, j: (b, 0, 0, 0))),
        out_shape=(jax.ShapeDtypeStruct((bsz, seq, GDN_WIDTH), _BF16),
                   jax.ShapeDtypeStruct((bsz, GDN_HEADS, GDN_HEAD_DIM, GDN_HEAD_DIM), _F32)),
        scratch_shapes=[pltpu.VMEM(st, _F32)],
        compiler_params=_cparams("arbitrary", "arbitrary"),
        name="gdn",
    )(q, k, v, z, bgc, grow, s0, gnw, after)


def _outproj_body(yc_ref, yg_ref, x_ref, wo_ref, g_ref, b_ref, h1_ref, h1p_ref):
    mix = (jnp.dot(yc_ref[...], wo_ref[0:CONV_WIDTH, :], preferred_element_type=_F32)
           + jnp.dot(yg_ref[...], wo_ref[CONV_WIDTH:, :], preferred_element_type=_F32))
    h1 = _layer_norm(DN_ALPHA * x_ref[...] + mix, g_ref[...], b_ref[...])
    h1_ref[...] = h1
    _store_rows(h1p_ref, _pack_halves(h1))


def _outproj(yc, yg, x2d, wo, g, b, *, tm, tile0):
    t = yc.shape[0]
    assert t % tm == 0
    row = lambda w: pl.BlockSpec((tm, w), lambda i: (i, 0))
    full = lambda a: pl.BlockSpec(a.shape, lambda i: (0,) * a.ndim)
    return pl.pallas_call(
        _outproj_body,
        grid=(t // tm,),
        in_specs=[row(CONV_WIDTH), row(GDN_WIDTH), pl.BlockSpec((tm, D_MODEL), lambda i: (i + tile0, 0)),
                  full(wo), full(g), full(b)],
        out_specs=(row(D_MODEL), pl.BlockSpec((tm * QUAD, 128), lambda i: (i, 0))),
        out_shape=(jax.ShapeDtypeStruct((t, D_MODEL), _F32), jax.ShapeDtypeStruct((t * QUAD, 128), jnp.uint32)),
        compiler_params=_cparams("arbitrary"),
        name="outproj",
    )(yc, yg, x2d, wo, g, b)


def _router_body(h1_ref, wh_ref, wl_ref, br_ref, idx_ref, gate_ref, rank_ref, cnt_ref, carry_ref, *, tt):
    @pl.when(pl.program_id(0) == 0)
    def _():
        carry_ref[...] = jnp.zeros_like(carry_ref)

    x = h1_ref[...]
    xh = x.astype(_BF16)
    xl = (x - xh.astype(_F32)).astype(_BF16)
    wh = wh_ref[...]
    logits = _mm_nt(wh, xh) + _mm_nt(wh, xl) + _mm_nt(wl_ref[...], xh)
    scores = _sigmoid(logits)
    sel = scores + br_ref[...]
    ninf = -jnp.inf

    r32 = lax.broadcasted_iota(_I32, (E_PER_GROUP, tt), 0)
    gsc = []
    for g in range(N_GROUPS):
        xg = sel[g * E_PER_GROUP:(g + 1) * E_PER_GROUP]
        m1 = jnp.max(xg, axis=0, keepdims=True)
        i1 = jnp.min(jnp.where(xg == m1, r32, E_PER_GROUP), axis=0, keepdims=True)
        m2 = jnp.max(jnp.where(r32 == i1, ninf, xg), axis=0, keepdims=True)
        gsc.append(m1 + m2)
    work = jnp.concatenate(gsc, axis=0)
    r8 = lax.broadcasted_iota(_I32, (N_GROUPS, tt), 0)
    gkeep = jnp.zeros((N_GROUPS, tt), _F32)
    for _ in range(TOPK_GROUPS):
        m = jnp.max(work, axis=0, keepdims=True)
        gi = jnp.min(jnp.where(work == m, r8, N_GROUPS), axis=0, keepdims=True)
        pick = r8 == gi
        gkeep = jnp.where(pick, 1.0, gkeep)
        work = jnp.where(pick, ninf, work)
    selm = jnp.concatenate(
        [jnp.where(gkeep[g:g + 1] > 0.5, sel[g * E_PER_GROUP:(g + 1) * E_PER_GROUP], ninf)
         for g in range(N_GROUPS)], axis=0)

    re = lax.broadcasted_iota(_I32, (N_EXPERTS, tt), 0)
    msel = jnp.zeros((N_EXPERTS, tt), _F32)
    idxs, gates = [], []
    for _ in range(TOP_K):
        m = jnp.max(selm, axis=0, keepdims=True)
        ii = jnp.min(jnp.where(selm == m, re, N_EXPERTS), axis=0, keepdims=True)
        hit = re == ii
        idxs.append(ii)
        gates.append(jnp.sum(jnp.where(hit, scores, 0.0), axis=0, keepdims=True))
        selm = jnp.where(hit, ninf, selm)
        msel = jnp.where(hit, 1.0, msel)
    gate = jnp.concatenate(gates, axis=0)
    gate_ref[...] = gate / jnp.sum(gate, axis=0, keepdims=True) * ROUTED_SCALE
    idx_ref[...] = jnp.concatenate(idxs, axis=0)

    ta = lax.broadcasted_iota(_I32, (tt, tt), 0)
    tb = lax.broadcasted_iota(_I32, (tt, tt), 1)
    earlier = jnp.where(ta < tb, 1.0, 0.0)
    carry = carry_ref[...]
    rank_all = _mm(msel, earlier) + carry[:, 0:1]
    rank_ref[...] = jnp.concatenate(
        [jnp.sum(jnp.where(re == ii, rank_all, 0.0), axis=0, keepdims=True) for ii in idxs],
        axis=0).astype(_I32)
    carry = carry + jnp.sum(msel, axis=1, keepdims=True)
    carry_ref[...] = carry
    cnt_ref[...] = carry


def _router(h1, wh, wl, br, *, tt, tile0, t):
    assert t % tt == 0
    full = lambda a: pl.BlockSpec(a.shape, lambda i: (0,) * a.ndim)
    kt = pl.BlockSpec((TOP_K, tt), lambda i: (0, i))
    return pl.pallas_call(
        functools.partial(_router_body, tt=tt),
        grid=(t // tt,),
        in_specs=[pl.BlockSpec((tt, D_MODEL), lambda i: (i + tile0, 0)), full(wh), full(wl), full(br)],
        out_specs=(kt, kt, kt, pl.BlockSpec((N_EXPERTS, 128), lambda i: (0, 0))),
        out_shape=(jax.ShapeDtypeStruct((TOP_K, t), _I32), jax.ShapeDtypeStruct((TOP_K, t), _F32),
                   jax.ShapeDtypeStruct((TOP_K, t), _I32), jax.ShapeDtypeStruct((N_EXPERTS, 128), _F32)),
        scratch_shapes=[pltpu.VMEM((N_EXPERTS, 128), _F32)],
        compiler_params=_cparams("arbitrary"),
        name="router",
    )(h1, wh, wl, br)


def _position_body(idx_ref, rank_ref, pstart_ref, pos_ref, *, tt):
    re = lax.broadcasted_iota(_I32, (N_EXPERTS, tt), 0)
    ps = pstart_ref[...]
    idx = idx_ref[...]
    rows = [jnp.sum(jnp.where(re == idx[k:k + 1], ps, 0), axis=0, keepdims=True) for k in range(TOP_K)]
    pos_ref[0] = jnp.concatenate(rows, axis=0) + rank_ref[...]


def _position(idx, rank, pstart, *, tt):
    t = idx.shape[1]
    kt = pl.BlockSpec((TOP_K, tt), lambda i: (0, i))
    return pl.pallas_call(
        functools.partial(_position_body, tt=tt),
        grid=(t // tt,),
        in_specs=[kt, kt, pl.BlockSpec(pstart.shape, lambda i: (0, 0))],
        out_specs=pl.BlockSpec((1, TOP_K, tt), lambda i: (i, 0, 0)),
        out_shape=jax.ShapeDtypeStruct((t // tt, TOP_K, tt), _I32),
        compiler_params=_cparams("arbitrary"),
        name="position",
    )(idx, rank, pstart)


def _ffn_body(blk0_ref, nblk_ref, ntot_ref, xs_hbm, wg_ref, wu_ref, wd_ref, after_a, after_b, ys_hbm,
              xbuf, ybuf, sem_in, sem_out, wgu_bf, wd_bf):
    del after_a, after_b
    e = pl.program_id(0)
    nblk = nblk_ref[e]
    blk0 = blk0_ref[e]
    ntot = ntot_ref[0]

    blk_rows = ROW_BLOCK * QUAD

    def rows(g):
        return pl.ds(pl.multiple_of(g * blk_rows, blk_rows), blk_rows)

    def in_start(g, slot):
        pltpu.make_async_copy(xs_hbm.at[rows(g)], xbuf.at[slot], sem_in.at[slot]).start()

    def in_wait(slot):
        pltpu.make_async_copy(xs_hbm.at[rows(0)], xbuf.at[slot], sem_in.at[slot]).wait()

    def out_start(g, slot):
        pltpu.make_async_copy(ybuf.at[slot], ys_hbm.at[rows(g)], sem_out.at[slot]).start()

    def out_wait(slot):
        pltpu.make_async_copy(ybuf.at[slot], ys_hbm.at[rows(0)], sem_out.at[slot]).wait()

    @pl.when(e == 0)
    def _():
        for i in range(IN_AHEAD):
            @pl.when(i < ntot)
            def _():
                in_start(i, i)

    @pl.when(nblk > 0)
    def _():
        wgu_bf[:, 0:EXPERT_FF] = wg_ref[0].astype(_BF16)
        wgu_bf[:, EXPERT_FF:] = wu_ref[0].astype(_BF16)
        wd_bf[...] = wd_ref[0].astype(_BF16)

        def acquire(g):
            slot = g & (RING - 1)
            in_wait(slot)

            @pl.when(g + IN_AHEAD < ntot)
            def _():
                in_start(g + IN_AHEAD, (g + IN_AHEAD) & (RING - 1))

            @pl.when(g >= RING)
            def _():
                out_wait(slot)

            return slot

        def compute(slot):
            lo, hi = _unpack_halves(_load_rows(xbuf.at[slot], ROW_BLOCK))
            a = jnp.dot(lo.astype(_BF16), wgu_bf[0:HALF, :], preferred_element_type=_F32)
            yield
            gu = a + jnp.dot(hi.astype(_BF16), wgu_bf[HALF:, :], preferred_element_type=_F32)
            yield
            h = (_silu(gu[:, :EXPERT_FF]) * gu[:, EXPERT_FF:]).astype(_BF16)
            y = jnp.dot(h, wd_bf[...], preferred_element_type=_F32)
            yield
            _store_rows(ybuf.at[slot], _pack_halves(y))

        def run(gs):
            slots = [acquire(g) for g in gs]
            live = [compute(s) for s in slots]
            while live:
                live = [c for c in live if next(c, live) is not live]
            for g, s in zip(gs, slots):
                out_start(g, s)

        def pair(j, carry):
            run([blk0 + 2 * j, blk0 + 2 * j + 1])
            return carry

        lax.fori_loop(0, nblk // 2, pair, 0)

        @pl.when((nblk & 1) == 1)
        def _():
            run([blk0 + nblk - 1])

    @pl.when(e == N_EXPERTS - 1)
    def _():
        for i in range(RING):
            @pl.when(i < ntot)
            def _():
                out_wait((ntot - 1 - i) & (RING - 1))


def _ffn(blk0, nblk, ntot, xs, wg, wu, wd, after_a, after_b):
    grid_spec = pltpu.PrefetchScalarGridSpec(
        num_scalar_prefetch=3,
        grid=(N_EXPERTS,),
        in_specs=[pl.BlockSpec(memory_space=pl.ANY),
                  pl.BlockSpec((1, D_MODEL, EXPERT_FF), lambda e, *_: (e, 0, 0)),
                  pl.BlockSpec((1, D_MODEL, EXPERT_FF), lambda e, *_: (e, 0, 0)),
                  pl.BlockSpec((1, EXPERT_FF, D_MODEL), lambda e, *_: (e, 0, 0)),
                  pl.BlockSpec(memory_space=pl.ANY), pl.BlockSpec(memory_space=pl.ANY)],
        out_specs=pl.BlockSpec(memory_space=pl.ANY),
        scratch_shapes=[pltpu.VMEM((RING, ROW_BLOCK * QUAD, 128), jnp.uint32),
                        pltpu.VMEM((RING, ROW_BLOCK * QUAD, 128), jnp.uint32),
                        pltpu.SemaphoreType.DMA((RING,)), pltpu.SemaphoreType.DMA((RING,)),
                        pltpu.VMEM((D_MODEL, 2 * EXPERT_FF), _BF16), pltpu.VMEM((EXPERT_FF, D_MODEL), _BF16)],
    )
    return pl.pallas_call(
        _ffn_body,
        grid_spec=grid_spec,
        out_shape=jax.ShapeDtypeStruct(xs.shape, jnp.uint32),
        compiler_params=_cparams("arbitrary"),
        name="ffn",
    )(blk0, nblk, ntot, xs, wg, wu, wd, after_a, after_b)


def _sc_position(idx, rank, pstart, after):
    n = idx.shape[0]
    per_w = n // SC_WORKERS
    assert per_w * SC_WORKERS == n and per_w % SC_LANES == 0
    mesh = plsc.VectorSubcoreMesh(core_axis_name="c", subcore_axis_name="s",
                                  num_cores=SC_CORES, num_subcores=SC_SUBCORES)

    @functools.partial(
        pl.kernel, mesh=mesh,
        out_type=jax.ShapeDtypeStruct((n,), _I32),
        scratch_types=[pltpu.VMEM((per_w,), _I32), pltpu.VMEM((per_w,), _I32), pltpu.VMEM((per_w,), _I32),
                       pltpu.VMEM((N_EXPERTS,), _I32)],
        compiler_params=pltpu.CompilerParams(needs_layout_passes=False),
        name="sc_position",
    )
    def position(idx_hbm, rank_hbm, ps_hbm, after_hbm, out_hbm, idx_v, rank_v, pos_v, ps_v):
        del after_hbm
        wid = lax.axis_index("s") * SC_CORES + lax.axis_index("c")
        mine = pl.ds(pl.multiple_of(wid * per_w, per_w), per_w)
        pltpu.sync_copy(ps_hbm, ps_v)
        pltpu.sync_copy(idx_hbm.at[mine], idx_v)
        pltpu.sync_copy(rank_hbm.at[mine], rank_v)

        @pl.loop(0, per_w, step=SC_LANES)
        def _(j):
            lanes = pl.ds(j, SC_LANES)
            pos_v[lanes] = plsc.load_gather(ps_v, [idx_v[lanes]]) + rank_v[lanes]

        pltpu.sync_copy(pos_v, out_hbm.at[mine])

    return position(idx, rank, pstart, after)


def _sc_gather(table, idx):
    b = idx.shape[0]
    nchunk = b // (SC_WORKERS * SC_CHUNK)
    assert nchunk * SC_WORKERS * SC_CHUNK == b and nchunk % SC_RING == 0
    idx2 = idx.reshape(SC_WORKERS * nchunk, SC_CHUNK)
    row = table.shape[1:]
    mesh = plsc.VectorSubcoreMesh(core_axis_name="c", subcore_axis_name="s",
                                  num_cores=SC_CORES, num_subcores=SC_SUBCORES)

    @functools.partial(
        pl.kernel, mesh=mesh,
        out_type=jax.ShapeDtypeStruct((b,) + row, table.dtype),
        scratch_types=[pltpu.VMEM((nchunk, SC_CHUNK), _I32), pltpu.VMEM((SC_RING, SC_CHUNK) + row, table.dtype),
                       pltpu.SemaphoreType.DMA((SC_RING,)), pltpu.SemaphoreType.DMA((SC_RING,))],
        name="sc_gather",
    )
    def gather(table_hbm, idx_hbm, out_hbm, idx_v, rows_v, sem_g, sem_w):
        wid = lax.axis_index("s") * SC_CORES + lax.axis_index("c")
        c0 = wid * nchunk
        pltpu.sync_copy(idx_hbm.at[pl.ds(pl.multiple_of(c0, nchunk), nchunk)], idx_v)

        def fetch(i, s):
            return pltpu.make_async_copy(table_hbm.at[idx_v.at[i]], rows_v.at[s], sem_g.at[s])

        def flush(i, s):
            rows = pl.ds(pl.multiple_of((c0 + i) * SC_CHUNK, SC_CHUNK), SC_CHUNK)
            return pltpu.make_async_copy(rows_v.at[s], out_hbm.at[rows], sem_w.at[s])

        for s in range(SC_RING):
            fetch(s, s).start()

        @pl.loop(0, nchunk, step=SC_RING)
        def _(g):
            for s in range(SC_RING):
                i = g + s
                fetch(i, s).wait()
                flush(i, s).start()
                flush(i, s).wait()

                @pl.when(i + SC_RING < nchunk)
                def _():
                    fetch(i + SC_RING, s).start()

    return gather(table, idx2)


def _sc_scatter(rows, pos3, n_out, row0):
    nchunk, nk, w = pos3.shape
    per_w = nchunk // SC_WORKERS
    assert per_w * SC_WORKERS == nchunk and w <= 128 and row0 % w == 0 and rows.shape[0] >= row0 + nchunk * w
    row = rows.shape[1:]
    mesh = plsc.VectorSubcoreMesh(core_axis_name="c", subcore_axis_name="s",
                                  num_cores=SC_CORES, num_subcores=SC_SUBCORES)

    @functools.partial(
        pl.kernel, mesh=mesh,
        out_type=(jax.ShapeDtypeStruct((n_out,) + row, rows.dtype),
                  jax.ShapeDtypeStruct((SC_WORKERS, nk, w), _I32)),
        scratch_types=[pltpu.VMEM((nk, w), _I32), pltpu.VMEM((w,) + row, rows.dtype), pltpu.SemaphoreType.DMA],
        name="sc_scatter",
    )
    def scatter(rows_hbm, pos_hbm, out_hbm, done_hbm, idx_v, rows_v, sem):
        wid = lax.axis_index("s") * SC_CORES + lax.axis_index("c")

        @pl.loop(0, per_w)
        def _(i):
            c = wid * per_w + i
            pltpu.sync_copy(pos_hbm.at[c], idx_v)
            pltpu.sync_copy(rows_hbm.at[pl.ds(pl.multiple_of(row0 + c * w, w), w)], rows_v)
            copies = [pltpu.async_copy(rows_v, out_hbm.at[idx_v.at[k]], sem) for k in range(nk)]
            for cp in copies:
                cp.wait()

        pltpu.sync_copy(idx_v, done_hbm.at[wid])

    return scatter(rows, pos3)


def _padfill_body(cnt_ref, pst_ref, pcn_ref, xs_in, xs_out, zbuf, zsem):
    del xs_in
    zbuf[...] = jnp.zeros_like(zbuf)

    def pad_runs(e, act):
        pad = pcn_ref[e] - cnt_ref[e]
        base = pst_ref[e] + cnt_ref[e]
        for b in range(ROW_BLOCK.bit_length() - 1):
            n = 1 << b

            @pl.when(((pad >> b) & 1) == 1)
            def _():
                off = base + (pad & (n - 1))
                act(pltpu.make_async_copy(zbuf.at[pl.ds(0, QUAD * n)],
                                          xs_out.at[pl.ds(QUAD * off, QUAD * n)], zsem))

    def start_all(e, c):
        pad_runs(e, lambda d: d.start())
        return c

    def wait_all(e, c):
        pad_runs(e, lambda d: d.wait())
        return c

    lax.fori_loop(0, N_EXPERTS, start_all, 0)
    lax.fori_loop(0, N_EXPERTS, wait_all, 0)


def _padfill(counts, pstarts, pcounts, xs):
    grid_spec = pltpu.PrefetchScalarGridSpec(
        num_scalar_prefetch=3,
        grid=(1,),
        in_specs=[pl.BlockSpec(memory_space=pl.ANY)],
        out_specs=pl.BlockSpec(memory_space=pl.ANY),
        scratch_shapes=[pltpu.VMEM((QUAD * ROW_BLOCK // 2, 128), jnp.uint32), pltpu.SemaphoreType.DMA],
    )
    return pl.pallas_call(
        _padfill_body,
        grid_spec=grid_spec,
        out_shape=jax.ShapeDtypeStruct(xs.shape, xs.dtype),
        input_output_aliases={3: 0},
        compiler_params=_cparams("arbitrary"),
        name="padfill",
    )(counts, pstarts, pcounts, xs)


def _combine_stream_body(gate_ref, h1_ref, yg_ref, wsg_ref, wsu_ref, wsd_ref, g_ref, b_ref, out_ref, *, tt):
    x = h1_ref[...]
    xb = x.astype(_BF16)
    shared = _mm(_silu(_mm(xb, wsg_ref[...])) * _mm(xb, wsu_ref[...]), wsd_ref[...])
    gcol = gate_ref[...].T
    acc_lo = jnp.zeros((tt, HALF), _F32)
    acc_hi = jnp.zeros((tt, HALF), _F32)
    for k in range(TOP_K):
        lo, hi = _unpack_halves(_load_rows(yg_ref.at[0, k], tt))
        acc_lo = acc_lo + gcol[:, k:k + 1] * lo
        acc_hi = acc_hi + gcol[:, k:k + 1] * hi
    routed = jnp.concatenate([acc_lo, acc_hi], axis=1)
    out_ref[...] = _layer_norm(DN_ALPHA * x + (routed + shared), g_ref[...], b_ref[...])


def _combine_stream(gate, h1, yg, wsg, wsu, wsd, g, b, out_prev, *, tt, tile0, t_all, sub0):
    t = yg.shape[0] * tt
    full = lambda a: pl.BlockSpec(a.shape, lambda i: (0,) * a.ndim)
    in_specs = [pl.BlockSpec((TOP_K, tt), lambda i: (0, i + sub0)),
                pl.BlockSpec((tt, D_MODEL), lambda i: (i + sub0, 0)),
                pl.BlockSpec((1, TOP_K, tt * QUAD, 128), lambda i: (i, 0, 0, 0)),
                full(wsg), full(wsu), full(wsd), full(g), full(b)]
    args = [gate, h1, yg, wsg, wsu, wsd, g, b]
    aliases = {}
    body = functools.partial(_combine_stream_body, tt=tt)
    if out_prev is not None:
        in_specs.append(pl.BlockSpec(memory_space=pl.ANY))
        args.append(out_prev)
        aliases = {len(args) - 1: 0}
        body = lambda *refs: _combine_stream_body(*refs[:8], refs[9], tt=tt)
    return pl.pallas_call(
        body,
        grid=(t // tt,),
        in_specs=in_specs,
        out_specs=pl.BlockSpec((tt, D_MODEL), lambda i: (i + tile0 + sub0, 0)),
        out_shape=jax.ShapeDtypeStruct((t_all, D_MODEL), _F32),
        input_output_aliases=aliases,
        compiler_params=_cparams("arbitrary"),
        name="combine",
    )(*args)


def _pick(n, pref):
    t = min(n, pref)
    while n % t:
        t -= CHUNK
    return t


def _delta(pre, s0, gnw, after, *, lg, nbb):
    yc, q, k, v, z, bgc, bgr, tails_out = pre
    bsz, seq, _ = q.shape
    nch = seq // CHUNK
    grow = bgr[:, GDN_HEADS:2 * GDN_HEADS, :].reshape(bsz, GDN_HEADS, nch, CHUNK)
    grow = grow.transpose(0, 2, 1, 3).reshape(bsz, nch, 1, STACK)
    yg, s_out = _gdn(q, k, v, z, bgc, grow, s0, gnw, after, lg=lg, nbb=nbb)
    return yc, yg, tails_out, s_out


def kernel(x, meta_tokens, w_in, conv_w, conv_norm_w, gdn_conv_w, a_log, dt_bias, gdn_norm_w, w_out,
           ln1_g, ln1_b, w_router, b_router, w_gate, w_up, w_down, ws_gate, ws_up, ws_down, ln2_g, ln2_b):
    assert w_in.shape[0] == 1, "single-layer stack"
    bsz, seq, d = x.shape
    assert d == D_MODEL and seq % CHUNK == 0
    c, gw = CONV_WIDTH, GDN_WIDTH
    win = w_in[0].astype(_BF16)
    wbd = win[:, 3 * c + 4 * gw:]
    zpad = jnp.zeros((128 - 2 * GDN_HEADS,), _F32)
    zpad4 = jnp.zeros((GDN_HEADS,), _F32)
    prow = jnp.zeros((8, 128), _F32)
    prow = prow.at[0].set(jnp.concatenate([zpad4, a_log[0], zpad]))
    prow = prow.at[1].set(jnp.concatenate([zpad4, dt_bias[0], zpad]))
    wts = (win[:, :3 * c], win[:, 3 * c:3 * c + 3 * gw], win[:, 3 * c + 3 * gw:3 * c + 4 * gw],
           jnp.pad(wbd, ((0, 0), (0, 128 - 2 * GDN_HEADS))), wbd.T,
           conv_w[0], conv_norm_w, gdn_conv_w[0], prow, prow.T[:8])
    gnw = gdn_norm_w

    meta = jnp.concatenate([jnp.zeros((CHUNK - N_META, d), x.dtype), meta_tokens.astype(x.dtype)])[None]
    tails0 = jnp.zeros((HIST, c + 3 * gw), _F32)
    s00 = jnp.zeros((GDN_HEADS, GDN_HEAD_DIM, GDN_HEAD_DIM), _F32)
    _, _, tails_m, s_m = _delta(_premix(meta, tails0, wts, lt=CHUNK, b0=0, bsz=1), s00, gnw, s00, lg=CHUNK, nbb=1)

    t = bsz * seq
    tm = _pick(seq, 512)
    tt = _pick(seq, 256)
    wo = w_out[0].astype(_BF16)
    wr_t = w_router[0].T
    wr_hi = wr_t.astype(_BF16)
    wr_lo = (wr_t - wr_hi.astype(_F32)).astype(_BF16)
    shared_w = (ws_gate[0].astype(_BF16), ws_up[0].astype(_BF16), ws_down[0].astype(_BF16))
    x2d = x.reshape(t, d)

    parts = LAYER_PARTS if bsz % LAYER_PARTS == 0 and (bsz // LAYER_PARTS * seq) % (tt * SC_WORKERS) == 0 else 1
    bp = bsz // parts
    tp = bp * seq
    nb = tp * TOP_K // ROW_BLOCK + N_EXPERTS
    nbb = GDN_ROWS if bp % GDN_ROWS == 0 else 1

    def premix(part):
        return _premix(x, tails_m[0], wts, lt=_pick(seq, 512), b0=part * bp, bsz=bp)

    def delta(part, pre, after):
        yc, yg, _, _ = _delta(pre, s_m[0], gnw, after, lg=_pick(seq, 512), nbb=nbb)
        return _outproj(yc.reshape(tp, c), yg.reshape(tp, gw), x2d, wo, ln1_g, ln1_b, tm=tm, tile0=part * (tp // tm))

    def route(h1, h1p, after):
        idx, gate, rank, cnt = _router(h1, wr_hi, wr_lo, b_router[0][:, None], tt=tt, tile0=0, t=tp)
        counts = cnt[:, 0].astype(_I32)
        pcounts = (counts + ROW_BLOCK - 1) // ROW_BLOCK * ROW_BLOCK
        pends = jnp.cumsum(pcounts)
        pstarts = (pends - pcounts).astype(_I32)
        tiled = lambda a: a.reshape(TOP_K, tp // tt, tt).transpose(1, 0, 2).reshape(tp * TOP_K)
        pos = _sc_position(tiled(idx), tiled(rank), pstarts, after)
        pos = pos.reshape(tp // tt, TOP_K, tt)
        nwin = tt // SC_WINDOW
        pos3 = pos.reshape(tp // tt, TOP_K, nwin, SC_WINDOW).transpose(0, 2, 1, 3)
        pos3 = pos3.reshape(tp // SC_WINDOW, TOP_K, SC_WINDOW)
        xs, done = _sc_scatter(h1p.reshape(tp, QUAD, 128), pos3, nb * ROW_BLOCK, 0)
        xs = _padfill(counts, pstarts, pcounts.astype(_I32), xs.reshape(nb * ROW_BLOCK * QUAD, 128))
        blocks = ((pstarts // ROW_BLOCK).astype(_I32), (pcounts // ROW_BLOCK).astype(_I32),
                  (pends[-1:] // ROW_BLOCK).astype(_I32))
        return xs, blocks, pos, gate, done

    def experts(xs, blocks, after_a, after_b):
        return _ffn(*blocks, xs, w_gate[0], w_up[0], w_down[0], after_a, after_b)

    def gather(ys, pos, sub0, nsub):
        yg = _sc_gather(ys.reshape(nb * ROW_BLOCK, QUAD, 128), pos[sub0:sub0 + nsub].reshape(nsub * tt * TOP_K))
        return yg.reshape(nsub, TOP_K, tt * QUAD, 128)

    def combine(part, gate, h1, yg, out, sub0):
        return _combine_stream(gate, h1, yg, *shared_w, ln2_g, ln2_b, out, tt=tt, tile0=part * (tp // tt), t_all=t,
                               sub0=sub0)

    out = None
    done = jnp.zeros((SC_WORKERS, TOP_K, SC_WINDOW), _I32)
    h1, h1p = delta(0, premix(0), s00)
    routed = route(h1, h1p, done)
    for part in range(parts):
        xs, blocks, pos, gate, done = routed
        h1_cur = h1
        last = part + 1 == parts
        pre = None if last else premix(part + 1)
        ys = experts(xs, blocks, done if last else pre[0], done if out is None else out)
        ntile = tp // tt
        if not last:
            yg = gather(ys, pos, 0, ntile)
            h1, h1p = delta(part + 1, pre, ys)
            routed = route(h1, h1p, yg)
            out = combine(part, gate, h1_cur, yg, out, 0)
        else:
            nsub = ntile // TAIL_SPLIT
            if ntile % TAIL_SPLIT or (nsub * tt * TOP_K) % (SC_WORKERS * SC_CHUNK * SC_RING):
                nsub = ntile
            for sub0 in range(0, ntile, nsub):
                out = combine(part, gate, h1_cur, gather(ys, pos, sub0, nsub), out, sub0)
    return out.reshape(bsz, seq, d)
```

```python
import functools

import jax
import jax.numpy as jnp
from jax import lax
from jax.experimental import pallas as pl
from jax.experimental.pallas import tpu as pltpu
from jax.experimental.pallas import tpu_sc as plsc

_F32 = jnp.float32
_BF16 = jnp.bfloat16
_I32 = jnp.int32

D_MODEL = 1024
N_META = 16
CONV_WIDTH = 512
CONV_K = 3
GDN_HEADS = 4
GDN_HEAD_DIM = 128
GDN_WIDTH = GDN_HEADS * GDN_HEAD_DIM
GDN_CONV_K = 4
CHUNK = 64
N_EXPERTS = 256
TOP_K = 8
N_GROUPS = 8
TOPK_GROUPS = 4
E_PER_GROUP = N_EXPERTS // N_GROUPS
EXPERT_FF = 256
ROUTED_SCALE = 2.5
ROW_BLOCK = 256
DN_ALPHA = 2.0 ** 0.25
NORM_EPS = 1e-5
HALF = D_MODEL // 2
QUAD = HALF // 128
STACK = GDN_HEADS * CHUNK
HIST = 8
GDN_ROWS = 4
PREMIX_SUB = 2
SC_CORES = 2
SC_SUBCORES = 16
SC_WORKERS = SC_CORES * SC_SUBCORES
SC_LANES = 16
SC_CHUNK = 64
SC_RING = 2
SC_WINDOW = 128
LAYER_PARTS = 2
RING = 8
IN_AHEAD = RING - 2

V7X_VMEM_BYTES = 64 * 1024 * 1024
VMEM_LIMIT = V7X_VMEM_BYTES - 8 * 1024 * 1024


def _cparams(*sem):
    return pltpu.CompilerParams(dimension_semantics=sem, vmem_limit_bytes=VMEM_LIMIT)


def _mm(a, b):
    return jnp.dot(a.astype(_BF16), b.astype(_BF16), preferred_element_type=_F32)


def _mm_nt(a, b):
    return lax.dot_general(a.astype(_BF16), b.astype(_BF16), (((1,), (1,)), ((), ())),
                           preferred_element_type=_F32)


def _mm_tn(a, b):
    return lax.dot_general(a.astype(_BF16), b.astype(_BF16), (((0,), (0,)), ((), ())),
                           preferred_element_type=_F32)


def _sigmoid(x):
    return 1.0 / (1.0 + jnp.exp(-x))


def _silu(x):
    return x * _sigmoid(x)


def _softplus(x):
    return jnp.maximum(x, 0.0) + jnp.log1p(jnp.exp(-jnp.abs(x)))


def _pack_halves(y):
    return pltpu.pack_elementwise([y[:, :HALF], y[:, HALF:]], packed_dtype=_BF16)


def _store_rows(ref, packed):
    r = packed.shape[0]
    for c in range(QUAD):
        ref[pl.ds(c, r, stride=QUAD), :] = packed[:, c * 128:(c + 1) * 128]


def _load_rows(ref, r):
    return jnp.concatenate([ref[pl.ds(c, r, stride=QUAD), :] for c in range(QUAD)], axis=1)


def _unpack_halves(p):
    lo = pltpu.unpack_elementwise(p, index=0, packed_dtype=_BF16, unpacked_dtype=_F32)
    hi = pltpu.unpack_elementwise(p, index=1, packed_dtype=_BF16, unpacked_dtype=_F32)
    return lo, hi


def _layer_norm(h, g, b):
    mu = jnp.mean(h, axis=-1, keepdims=True)
    d = h - mu
    var = jnp.mean(d * d, axis=-1, keepdims=True)
    return d * lax.rsqrt(var + NORM_EPS) * g + b


def _premix_body(x_ref, tails_ref, wa_ref, wq_ref, wz_ref, wbd_ref, wbdt_ref, cw_ref, cnw_ref,
                 gcw_ref, prow_ref, pcol_ref,
                 yc_ref, q_ref, k_ref, v_ref, z_ref, bgc_ref, bgr_ref, tout_ref, ext_ref, *, lt):
    cw_ = CONV_WIDTH

    @pl.when(pl.program_id(1) == 0)
    def _():
        ext_ref[0:HIST, :] = tails_ref[...]

    cw = cw_ref[...]
    gcw = gcw_ref[...]
    prow = prow_ref[...]
    pcol = pcol_ref[...]

    def sub_tile(r0, n):
        rows = slice(r0, r0 + n)
        erows = slice(HIST + r0, HIST + r0 + n)
        xb = x_ref[0, rows, :].astype(_BF16)
        pa = jnp.dot(xb, wa_ref[...], preferred_element_type=_F32)
        yield
        gate_b = pa[:, 0:cw_]
        u = pa[:, cw_:2 * cw_] * pa[:, 2 * cw_:3 * cw_]
        ext_ref[erows, 0:cw_] = u
        pq = jnp.dot(xb, wq_ref[...], preferred_element_type=_F32)
        yield
        ext_ref[erows, cw_:] = pq
        zz = jnp.dot(xb, wz_ref[...], preferred_element_type=_F32)
        bdc = jnp.dot(xb, wbd_ref[...], preferred_element_type=_F32)
        bdr = _mm_nt(wbdt_ref[...], xb)
        yield

        ca = u * cw[CONV_K - 1:CONV_K, :]
        for j in range(CONV_K - 1):
            ca = ca + ext_ref[pl.ds(HIST + r0 - (CONV_K - 1) + j, n), 0:cw_] * cw[j:j + 1, :]
        yc = gate_b * ca
        ms = jnp.mean(yc * yc, axis=-1, keepdims=True)
        yc_ref[0, rows, :] = (yc * lax.rsqrt(ms + NORM_EPS) * cnw_ref[...]).astype(_BF16)

        cq = pq * gcw[GDN_CONV_K - 1:GDN_CONV_K, :]
        for j in range(GDN_CONV_K - 1):
            cq = cq + ext_ref[pl.ds(HIST + r0 - (GDN_CONV_K - 1) + j, n), cw_:] * gcw[j:j + 1, :]
        s = _silu(cq)
        for h in range(GDN_HEADS):
            lo, hi = h * GDN_HEAD_DIM, (h + 1) * GDN_HEAD_DIM
            qh = s[:, lo:hi]
            kh = s[:, GDN_WIDTH + lo:GDN_WIDTH + hi]
            qn = qh * lax.rsqrt(jnp.sum(qh * qh, axis=-1, keepdims=True) + 1e-6)
            kn = kh * lax.rsqrt(jnp.sum(kh * kh, axis=-1, keepdims=True) + 1e-6)
            q_ref[0, rows, lo:hi] = (qn * (GDN_HEAD_DIM ** -0.5)).astype(_BF16)
            k_ref[0, rows, lo:hi] = kn.astype(_BF16)
        v_ref[0, rows, :] = s[:, 2 * GDN_WIDTH:].astype(_BF16)
        z_ref[0, rows, :] = zz.astype(_BF16)

        g_c = -jnp.exp(prow[0:1, :]) * _softplus(bdc + prow[1:2, :])
        lane = lax.broadcasted_iota(_I32, bdc.shape, 1)
        bgc_ref[0, rows, :] = jnp.where(lane < GDN_HEADS, _sigmoid(bdc), g_c)
        g_r = -jnp.exp(pcol[:, 0:1]) * _softplus(bdr + pcol[:, 1:2])
        row = lax.broadcasted_iota(_I32, bdr.shape, 0)
        bgr_ref[0, :, rows] = jnp.where(row < GDN_HEADS, _sigmoid(bdr), g_r)

    n_sub = PREMIX_SUB if lt % (PREMIX_SUB * 128) == 0 else 1
    live = [sub_tile(i * (lt // n_sub), lt // n_sub) for i in range(n_sub)]
    while live:
        live = [g for g in live if next(g, live) is not live]

    tail = ext_ref[lt:lt + HIST, :]
    ext_ref[0:HIST, :] = tail
    tout_ref[0] = tail


def _premix(x, tails, wts, *, lt, b0, bsz):
    _, seq, d = x.shape
    assert seq % lt == 0
    grid = (bsz, seq // lt)
    full = lambda a: pl.BlockSpec(a.shape, lambda b, j: (0,) * a.ndim)
    tok = lambda w: pl.BlockSpec((1, lt, w), lambda b, j: (b, j, 0))
    x_spec = pl.BlockSpec((1, lt, d), lambda b, j: (b + b0, j, 0))
    (wa, wq, wz, wbd, wbdt, cw, cnw, gcw, prow, pcol) = wts
    ext_w = CONV_WIDTH + 3 * GDN_WIDTH
    out_shape = (
        jax.ShapeDtypeStruct((bsz, seq, CONV_WIDTH), _BF16),
        jax.ShapeDtypeStruct((bsz, seq, GDN_WIDTH), _BF16),
        jax.ShapeDtypeStruct((bsz, seq, GDN_WIDTH), _BF16),
        jax.ShapeDtypeStruct((bsz, seq, GDN_WIDTH), _BF16),
        jax.ShapeDtypeStruct((bsz, seq, GDN_WIDTH), _BF16),
        jax.ShapeDtypeStruct((bsz, seq, 128), _F32),
        jax.ShapeDtypeStruct((bsz, 8, seq), _F32),
        jax.ShapeDtypeStruct((bsz, HIST, ext_w), _F32),
    )
    out_specs = (tok(CONV_WIDTH), tok(GDN_WIDTH), tok(GDN_WIDTH), tok(GDN_WIDTH), tok(GDN_WIDTH),
                 tok(128), pl.BlockSpec((1, 8, lt), lambda b, j: (b, 0, j)),
                 pl.BlockSpec((1, HIST, ext_w), lambda b, j: (b, 0, 0)))
    return pl.pallas_call(
        functools.partial(_premix_body, lt=lt),
        grid=grid,
        in_specs=[x_spec, full(tails)] + [full(w) for w in wts],
        out_specs=out_specs,
        out_shape=out_shape,
        scratch_shapes=[pltpu.VMEM((HIST + lt, ext_w), _F32)],
        compiler_params=_cparams("arbitrary", "arbitrary"),
        name="premix",
    )(x, tails, *wts)


def _cumsum_rows(x):
    row = lax.broadcasted_iota(_I32, x.shape, 0)
    s = 1
    while s < x.shape[0]:
        x = x + jnp.where(row >= s, pltpu.roll(x, s, 0), 0.0)
        s *= 2
    return x


def _cumsum_lanes_seg(x):
    lane = lax.broadcasted_iota(_I32, x.shape, 1) & (CHUNK - 1)
    s = 1
    while s < CHUNK:
        x = x + jnp.where(lane >= s, pltpu.roll(x, s, 1), 0.0)
        s *= 2
    return x


def _stack_heads(a):
    return jnp.concatenate([a[:, h * GDN_HEAD_DIM:(h + 1) * GDN_HEAD_DIM] for h in range(GDN_HEADS)], axis=0)


def _gdn_body(q_ref, k_ref, v_ref, z_ref, bgc_ref, grow_ref, s0_ref, gnw_ref, after_ref,
              y_ref, sout_ref, s_ref, *, nc, nbb):
    del after_ref

    @pl.when(pl.program_id(1) == 0)
    def _():
        for r in range(nbb):
            s_ref[r] = s0_ref[...]

    ri = lax.broadcasted_iota(_I32, (STACK, STACK), 0)
    ci = lax.broadcasted_iota(_I32, (STACK, STACK), 1)
    same64 = (ri >> 6) == (ci >> 6)
    same32 = (ri >> 5) == (ci >> 5)
    same16 = (ri >> 4) == (ci >> 4)
    low_incl = same64 & (ri >= ci)
    low_strict = same64 & (ri > ci)
    gnw = gnw_ref[...]

    def chunk_row(r, c):
        off = pl.multiple_of(c * CHUNK, CHUNK)
        q_all = _stack_heads(q_ref[r, pl.ds(off, CHUNK), :].astype(_F32))
        k_all = _stack_heads(k_ref[r, pl.ds(off, CHUNK), :].astype(_F32))
        v_all = _stack_heads(v_ref[r, pl.ds(off, CHUNK), :].astype(_F32))
        bgc = bgc_ref[r, pl.ds(off, CHUNK), :]
        gcs = _cumsum_rows(bgc)
        hd = (CHUNK, GDN_HEAD_DIM)
        beta_b = jnp.concatenate(
            [jnp.broadcast_to(bgc[:, h:h + 1], hd) for h in range(GDN_HEADS)], axis=0)
        gc_b = jnp.concatenate(
            [jnp.broadcast_to(gcs[:, GDN_HEADS + h:GDN_HEADS + h + 1], hd) for h in range(GDN_HEADS)], axis=0)
        gl = [gcs[CHUNK - 1:CHUNK, GDN_HEADS + h:GDN_HEADS + h + 1] for h in range(GDN_HEADS)]
        gl_b = jnp.concatenate([jnp.broadcast_to(g1, hd) for g1 in gl], axis=0)
        gcr = _cumsum_lanes_seg(jnp.broadcast_to(grow_ref[r, c], (8, STACK)))[0:1, :]

        diff = jnp.concatenate([gc_b, gc_b], axis=1) - gcr
        decay = jnp.exp(jnp.where(low_incl, diff, -1e30))
        kb = k_all * beta_b
        a1 = _mm_nt(jnp.concatenate([kb, q_all], axis=0), k_all)
        yield
        m = jnp.where(low_strict, a1[:STACK] * decay, 0.0)
        attn = a1[STACK:] * decay

        l16 = jnp.where(same16, m, 0.0)
        c1 = jnp.where(same32 & jnp.logical_not(same16), m, 0.0)
        c2 = jnp.where(same32, 0.0, m)
        p2 = _mm(l16, l16)
        yield
        p4 = _mm(p2, p2)
        t = _mm(l16, p2)
        yield
        na = p2 - l16 - t
        p8 = _mm(p4, p4)
        t = _mm(na, p4)
        yield
        nb = na + p4 + t
        t = _mm(nb, p8)
        yield
        ncm = nb + p8 + t
        t = _mm(c1, ncm)
        yield
        y1 = c1 + t
        t = _mm(ncm, y1)
        yield
        n1 = ncm - y1 - t
        t = _mm(c2, n1)
        yield
        y2 = c2 + t
        t = _mm(n1, y2)
        yield
        nt = n1 - y2 - t

        egc = jnp.exp(gc_b)
        rhs = jnp.concatenate([v_all * beta_b, kb * egc], axis=1)
        t = _mm(nt, rhs)
        yield
        uw = rhs + t
        u_all = uw[:, :GDN_HEAD_DIM]
        w_all = uw[:, GDN_HEAD_DIM:]
        qd = q_all * egc
        kd = k_all * jnp.exp(gl_b - gc_b)

        bs = []
        for h in range(GDN_HEADS):
            r0, r1 = h * CHUNK, (h + 1) * CHUNK
            bs.append(_mm(jnp.concatenate([w_all[r0:r1], qd[r0:r1]], axis=0), s_ref[r, h]))
        yield
        vn = [u_all[h * CHUNK:(h + 1) * CHUNK] - bs[h][:CHUNK] for h in range(GDN_HEADS)]
        vn_all = jnp.concatenate(vn, axis=0)
        t = _mm(attn, vn_all)
        ds = [_mm_tn(kd[h * CHUNK:(h + 1) * CHUNK], vn[h]) for h in range(GDN_HEADS)]
        yield
        o_all = jnp.concatenate([b[CHUNK:] for b in bs], axis=0) + t
        for h in range(GDN_HEADS):
            r0, r1 = h * CHUNK, (h + 1) * CHUNK
            s_ref[r, h] = s_ref[r, h] * jnp.exp(gl[h]) + ds[h]
            o = o_all[r0:r1]
            zz = z_ref[r, pl.ds(off, CHUNK), h * GDN_HEAD_DIM:(h + 1) * GDN_HEAD_DIM].astype(_F32)
            on = o * lax.rsqrt(jnp.mean(o * o, axis=-1, keepdims=True) + NORM_EPS) * gnw
            y_ref[r, pl.ds(off, CHUNK), h * GDN_HEAD_DIM:(h + 1) * GDN_HEAD_DIM] = (on * _silu(zz)).astype(_BF16)

    def chunk(c, carry):
        live = [chunk_row(r, c) for r in range(nbb)]
        while live:
            live = [g for g in live if next(g, live) is not live]
        return carry

    lax.fori_loop(0, nc, chunk, 0)
    sout_ref[...] = s_ref[...]


def _gdn(q, k, v, z, bgc, grow, s0, gnw, after, *, lg, nbb):
    bsz, seq, _ = q.shape
    assert seq % lg == 0 and lg % CHUNK == 0 and bsz % nbb == 0
    nc = lg // CHUNK
    tok = lambda w: pl.BlockSpec((nbb, lg, w), lambda b, j: (b, j, 0))
    full = lambda a: pl.BlockSpec(a.shape, lambda b, j: (0,) * a.ndim)
    st = (nbb, GDN_HEADS, GDN_HEAD_DIM, GDN_HEAD_DIM)
    return pl.pallas_call(
        functools.partial(_gdn_body, nc=nc, nbb=nbb),
        grid=(bsz // nbb, seq // lg),
        in_specs=[tok(GDN_WIDTH)] * 4 + [tok(128), pl.BlockSpec((nbb, nc, 1, STACK), lambda b, j: (b, j, 0, 0)),
                                           full(s0), full(gnw), pl.BlockSpec(memory_space=pl.ANY)],
        out_specs=(tok(GDN_WIDTH), pl.BlockSpec(st, lambda b, j: (b, 0, 0, 0))),
        out_shape=(jax.ShapeDtypeStruct((bsz, seq, GDN_WIDTH), _BF16),
                   jax.ShapeDtypeStruct((bsz, GDN_HEADS, GDN_HEAD_DIM, GDN_HEAD_DIM), _F32)),
        scratch_shapes=[pltpu.VMEM(st, _F32)],
        compiler_params=_cparams("arbitrary", "arbitrary"),
        name="gdn",
    )(q, k, v, z, bgc, grow, s0, gnw, after)


def _outproj_body(yc_ref, yg_ref, x_ref, wo_ref, g_ref, b_ref, h1_ref, h1p_ref):
    mix = (jnp.dot(yc_ref[...], wo_ref[0:CONV_WIDTH, :], preferred_element_type=_F32)
           + jnp.dot(yg_ref[...], wo_ref[CONV_WIDTH:, :], preferred_element_type=_F32))
    h1 = _layer_norm(DN_ALPHA * x_ref[...] + mix, g_ref[...], b_ref[...])
    h1_ref[...] = h1
    _store_rows(h1p_ref, _pack_halves(h1))


def _outproj(yc, yg, x2d, wo, g, b, *, tm, tile0):
    t = yc.shape[0]
    assert t % tm == 0
    row = lambda w: pl.BlockSpec((tm, w), lambda i: (i, 0))
    full = lambda a: pl.BlockSpec(a.shape, lambda i: (0,) * a.ndim)
    return pl.pallas_call(
        _outproj_body,
        grid=(t // tm,),
        in_specs=[row(CONV_WIDTH), row(GDN_WIDTH), pl.BlockSpec((tm, D_MODEL), lambda i: (i + tile0, 0)),
                  full(wo), full(g), full(b)],
        out_specs=(row(D_MODEL), pl.BlockSpec((tm * QUAD, 128), lambda i: (i, 0))),
        out_shape=(jax.ShapeDtypeStruct((t, D_MODEL), _F32), jax.ShapeDtypeStruct((t * QUAD, 128), jnp.uint32)),
        compiler_params=_cparams("arbitrary"),
        name="outproj",
    )(yc, yg, x2d, wo, g, b)


def _router_body(h1_ref, wh_ref, wl_ref, br_ref, idx_ref, gate_ref, rank_ref, cnt_ref, carry_ref, *, tt):
    @pl.when(pl.program_id(0) == 0)
    def _():
        carry_ref[...] = jnp.zeros_like(carry_ref)

    x = h1_ref[...]
    xh = x.astype(_BF16)
    xl = (x - xh.astype(_F32)).astype(_BF16)
    wh = wh_ref[...]
    logits = _mm_nt(wh, xh) + _mm_nt(wh, xl) + _mm_nt(wl_ref[...], xh)
    scores = _sigmoid(logits)
    sel = scores + br_ref[...]
    ninf = -jnp.inf

    r32 = lax.broadcasted_iota(_I32, (E_PER_GROUP, tt), 0)
    gsc = []
    for g in range(N_GROUPS):
        xg = sel[g * E_PER_GROUP:(g + 1) * E_PER_GROUP]
        m1 = jnp.max(xg, axis=0, keepdims=True)
        i1 = jnp.min(jnp.where(xg == m1, r32, E_PER_GROUP), axis=0, keepdims=True)
        m2 = jnp.max(jnp.where(r32 == i1, ninf, xg), axis=0, keepdims=True)
        gsc.append(m1 + m2)
    work = jnp.concatenate(gsc, axis=0)
    r8 = lax.broadcasted_iota(_I32, (N_GROUPS, tt), 0)
    gkeep = jnp.zeros((N_GROUPS, tt), _F32)
    for _ in range(TOPK_GROUPS):
        m = jnp.max(work, axis=0, keepdims=True)
        gi = jnp.min(jnp.where(work == m, r8, N_GROUPS), axis=0, keepdims=True)
        pick = r8 == gi
        gkeep = jnp.where(pick, 1.0, gkeep)
        work = jnp.where(pick, ninf, work)
    selm = jnp.concatenate(
        [jnp.where(gkeep[g:g + 1] > 0.5, sel[g * E_PER_GROUP:(g + 1) * E_PER_GROUP], ninf)
         for g in range(N_GROUPS)], axis=0)

    re = lax.broadcasted_iota(_I32, (N_EXPERTS, tt), 0)
    msel = jnp.zeros((N_EXPERTS, tt), _F32)
    idxs, gates = [], []
    for _ in range(TOP_K):
        m = jnp.max(selm, axis=0, keepdims=True)
        ii = jnp.min(jnp.where(selm == m, re, N_EXPERTS), axis=0, keepdims=True)
        hit = re == ii
        idxs.append(ii)
        gates.append(jnp.sum(jnp.where(hit, scores, 0.0), axis=0, keepdims=True))
        selm = jnp.where(hit, ninf, selm)
        msel = jnp.where(hit, 1.0, msel)
    gate = jnp.concatenate(gates, axis=0)
    gate_ref[...] = gate / jnp.sum(gate, axis=0, keepdims=True) * ROUTED_SCALE
    idx_ref[...] = jnp.concatenate(idxs, axis=0)

    ta = lax.broadcasted_iota(_I32, (tt, tt), 0)
    tb = lax.broadcasted_iota(_I32, (tt, tt), 1)
    earlier = jnp.where(ta < tb, 1.0, 0.0)
    carry = carry_ref[...]
    rank_all = _mm(msel, earlier) + carry[:, 0:1]
    rank_ref[...] = jnp.concatenate(
        [jnp.sum(jnp.where(re == ii, rank_all, 0.0), axis=0, keepdims=True) for ii in idxs],
        axis=0).astype(_I32)
    carry = carry + jnp.sum(msel, axis=1, keepdims=True)
    carry_ref[...] = carry
    cnt_ref[...] = carry


def _router(h1, wh, wl, br, *, tt, tile0, t):
    assert t % tt == 0
    full = lambda a: pl.BlockSpec(a.shape, lambda i: (0,) * a.ndim)
    kt = pl.BlockSpec((TOP_K, tt), lambda i: (0, i))
    return pl.pallas_call(
        functools.partial(_router_body, tt=tt),
        grid=(t // tt,),
        in_specs=[pl.BlockSpec((tt, D_MODEL), lambda i: (i + tile0, 0)), full(wh), full(wl), full(br)],
        out_specs=(kt, kt, kt, pl.BlockSpec((N_EXPERTS, 128), lambda i: (0, 0))),
        out_shape=(jax.ShapeDtypeStruct((TOP_K, t), _I32), jax.ShapeDtypeStruct((TOP_K, t), _F32),
                   jax.ShapeDtypeStruct((TOP_K, t), _I32), jax.ShapeDtypeStruct((N_EXPERTS, 128), _F32)),
        scratch_shapes=[pltpu.VMEM((N_EXPERTS, 128), _F32)],
        compiler_params=_cparams("arbitrary"),
        name="router",
    )(h1, wh, wl, br)


def _ffn_body(blk0_ref, nblk_ref, ntot_ref, xs_hbm, wg_ref, wu_ref, wd_ref, after_a, after_b, ys_hbm,
              xbuf, ybuf, sem_in, sem_out, wgu_bf, wd_bf):
    del after_a, after_b
    e = pl.program_id(0)
    nblk = nblk_ref[e]
    blk0 = blk0_ref[e]
    ntot = ntot_ref[0]

    blk_rows = ROW_BLOCK * QUAD

    def rows(g):
        return pl.ds(pl.multiple_of(g * blk_rows, blk_rows), blk_rows)

    def in_start(g, slot):
        pltpu.make_async_copy(xs_hbm.at[rows(g)], xbuf.at[slot], sem_in.at[slot]).start()

    def in_wait(slot):
        pltpu.make_async_copy(xs_hbm.at[rows(0)], xbuf.at[slot], sem_in.at[slot]).wait()

    def out_start(g, slot):
        pltpu.make_async_copy(ybuf.at[slot], ys_hbm.at[rows(g)], sem_out.at[slot]).start()

    def out_wait(slot):
        pltpu.make_async_copy(ybuf.at[slot], ys_hbm.at[rows(0)], sem_out.at[slot]).wait()

    @pl.when(e == 0)
    def _():
        for i in range(IN_AHEAD):
            @pl.when(i < ntot)
            def _():
                in_start(i, i)

    @pl.when(nblk > 0)
    def _():
        wgu_bf[:, 0:EXPERT_FF] = wg_ref[0].astype(_BF16)
        wgu_bf[:, EXPERT_FF:] = wu_ref[0].astype(_BF16)
        wd_bf[...] = wd_ref[0].astype(_BF16)

        def acquire(g):
            slot = g & (RING - 1)
            in_wait(slot)

            @pl.when(g + IN_AHEAD < ntot)
            def _():
                in_start(g + IN_AHEAD, (g + IN_AHEAD) & (RING - 1))

            @pl.when(g >= RING)
            def _():
                out_wait(slot)

            return slot

        def compute(slot):
            lo, hi = _unpack_halves(_load_rows(xbuf.at[slot], ROW_BLOCK))
            a = jnp.dot(lo.astype(_BF16), wgu_bf[0:HALF, :], preferred_element_type=_F32)
            yield
            gu = a + jnp.dot(hi.astype(_BF16), wgu_bf[HALF:, :], preferred_element_type=_F32)
            yield
            h = (_silu(gu[:, :EXPERT_FF]) * gu[:, EXPERT_FF:]).astype(_BF16)
            y = jnp.dot(h, wd_bf[...], preferred_element_type=_F32)
            yield
            _store_rows(ybuf.at[slot], _pack_halves(y))

        def run(gs):
            slots = [acquire(g) for g in gs]
            live = [compute(s) for s in slots]
            while live:
                live = [c for c in live if next(c, live) is not live]
            for g, s in zip(gs, slots):
                out_start(g, s)

        def pair(j, carry):
            run([blk0 + 2 * j, blk0 + 2 * j + 1])
            return carry

        lax.fori_loop(0, nblk // 2, pair, 0)

        @pl.when((nblk & 1) == 1)
        def _():
            run([blk0 + nblk - 1])

    @pl.when(e == N_EXPERTS - 1)
    def _():
        for i in range(RING):
            @pl.when(i < ntot)
            def _():
                out_wait((ntot - 1 - i) & (RING - 1))


def _ffn(blk0, nblk, ntot, xs, wg, wu, wd, after_a, after_b):
    grid_spec = pltpu.PrefetchScalarGridSpec(
        num_scalar_prefetch=3,
        grid=(N_EXPERTS,),
        in_specs=[pl.BlockSpec(memory_space=pl.ANY),
                  pl.BlockSpec((1, D_MODEL, EXPERT_FF), lambda e, *_: (e, 0, 0)),
                  pl.BlockSpec((1, D_MODEL, EXPERT_FF), lambda e, *_: (e, 0, 0)),
                  pl.BlockSpec((1, EXPERT_FF, D_MODEL), lambda e, *_: (e, 0, 0)),
                  pl.BlockSpec(memory_space=pl.ANY), pl.BlockSpec(memory_space=pl.ANY)],
        out_specs=pl.BlockSpec(memory_space=pl.ANY),
        scratch_shapes=[pltpu.VMEM((RING, ROW_BLOCK * QUAD, 128), jnp.uint32),
                        pltpu.VMEM((RING, ROW_BLOCK * QUAD, 128), jnp.uint32),
                        pltpu.SemaphoreType.DMA((RING,)), pltpu.SemaphoreType.DMA((RING,)),
                        pltpu.VMEM((D_MODEL, 2 * EXPERT_FF), _BF16), pltpu.VMEM((EXPERT_FF, D_MODEL), _BF16)],
    )
    return pl.pallas_call(
        _ffn_body,
        grid_spec=grid_spec,
        out_shape=jax.ShapeDtypeStruct(xs.shape, jnp.uint32),
        compiler_params=_cparams("arbitrary"),
        name="ffn",
    )(blk0, nblk, ntot, xs, wg, wu, wd, after_a, after_b)


def _sc_position(idx, rank, pstart, after):
    n = idx.shape[0]
    per_w = n // SC_WORKERS
    assert per_w * SC_WORKERS == n and per_w % SC_LANES == 0
    mesh = plsc.VectorSubcoreMesh(core_axis_name="c", subcore_axis_name="s",
                                  num_cores=SC_CORES, num_subcores=SC_SUBCORES)

    @functools.partial(
        pl.kernel, mesh=mesh,
        out_type=jax.ShapeDtypeStruct((n,), _I32),
        scratch_types=[pltpu.VMEM((per_w,), _I32), pltpu.VMEM((per_w,), _I32), pltpu.VMEM((per_w,), _I32),
                       pltpu.VMEM((N_EXPERTS,), _I32)],
        compiler_params=pltpu.CompilerParams(needs_layout_passes=False),
        name="sc_position",
    )
    def position(idx_hbm, rank_hbm, ps_hbm, after_hbm, out_hbm, idx_v, rank_v, pos_v, ps_v):
        del after_hbm
        wid = lax.axis_index("s") * SC_CORES + lax.axis_index("c")
        mine = pl.ds(pl.multiple_of(wid * per_w, per_w), per_w)
        pltpu.sync_copy(ps_hbm, ps_v)
        pltpu.sync_copy(idx_hbm.at[mine], idx_v)
        pltpu.sync_copy(rank_hbm.at[mine], rank_v)

        @pl.loop(0, per_w, step=SC_LANES)
        def _(j):
            lanes = pl.ds(j, SC_LANES)
            pos_v[lanes] = plsc.load_gather(ps_v, [idx_v[lanes]]) + rank_v[lanes]

        pltpu.sync_copy(pos_v, out_hbm.at[mine])

    return position(idx, rank, pstart, after)


def _sc_gather(table, idx):
    b = idx.shape[0]
    nchunk = b // (SC_WORKERS * SC_CHUNK)
    assert nchunk * SC_WORKERS * SC_CHUNK == b and nchunk % SC_RING == 0
    idx2 = idx.reshape(SC_WORKERS * nchunk, SC_CHUNK)
    row = table.shape[1:]
    mesh = plsc.VectorSubcoreMesh(core_axis_name="c", subcore_axis_name="s",
                                  num_cores=SC_CORES, num_subcores=SC_SUBCORES)

    @functools.partial(
        pl.kernel, mesh=mesh,
        out_type=jax.ShapeDtypeStruct((b,) + row, table.dtype),
        scratch_types=[pltpu.VMEM((nchunk, SC_CHUNK), _I32), pltpu.VMEM((SC_RING, SC_CHUNK) + row, table.dtype),
                       pltpu.SemaphoreType.DMA((SC_RING,)), pltpu.SemaphoreType.DMA((SC_RING,))],
        name="sc_gather",
    )
    def gather(table_hbm, idx_hbm, out_hbm, idx_v, rows_v, sem_g, sem_w):
        wid = lax.axis_index("s") * SC_CORES + lax.axis_index("c")
        c0 = wid * nchunk
        pltpu.sync_copy(idx_hbm.at[pl.ds(pl.multiple_of(c0, nchunk), nchunk)], idx_v)

        def fetch(i, s):
            return pltpu.make_async_copy(table_hbm.at[idx_v.at[i]], rows_v.at[s], sem_g.at[s])

        def flush(i, s):
            rows = pl.ds(pl.multiple_of((c0 + i) * SC_CHUNK, SC_CHUNK), SC_CHUNK)
            return pltpu.make_async_copy(rows_v.at[s], out_hbm.at[rows], sem_w.at[s])

        for s in range(SC_RING):
            fetch(s, s).start()

        @pl.loop(0, nchunk, step=SC_RING)
        def _(g):
            for s in range(SC_RING):
                i = g + s
                fetch(i, s).wait()
                flush(i, s).start()
                flush(i, s).wait()

                @pl.when(i + SC_RING < nchunk)
                def _():
                    fetch(i + SC_RING, s).start()

    return gather(table, idx2)


def _sc_scatter(rows, pos3, n_out, row0):
    nchunk, nk, w = pos3.shape
    per_w = nchunk // SC_WORKERS
    assert per_w * SC_WORKERS == nchunk and w <= 128 and row0 % w == 0 and rows.shape[0] >= row0 + nchunk * w
    row = rows.shape[1:]
    mesh = plsc.VectorSubcoreMesh(core_axis_name="c", subcore_axis_name="s",
                                  num_cores=SC_CORES, num_subcores=SC_SUBCORES)

    @functools.partial(
        pl.kernel, mesh=mesh,
        out_type=(jax.ShapeDtypeStruct((n_out,) + row, rows.dtype),
                  jax.ShapeDtypeStruct((SC_WORKERS, nk, w), _I32)),
        scratch_types=[pltpu.VMEM((nk, w), _I32), pltpu.VMEM((w,) + row, rows.dtype), pltpu.SemaphoreType.DMA],
        name="sc_scatter",
    )
    def scatter(rows_hbm, pos_hbm, out_hbm, done_hbm, idx_v, rows_v, sem):
        wid = lax.axis_index("s") * SC_CORES + lax.axis_index("c")

        @pl.loop(0, per_w)
        def _(i):
            c = wid * per_w + i
            pltpu.sync_copy(pos_hbm.at[c], idx_v)
            pltpu.sync_copy(rows_hbm.at[pl.ds(pl.multiple_of(row0 + c * w, w), w)], rows_v)
            copies = [pltpu.async_copy(rows_v, out_hbm.at[idx_v.at[k]], sem) for k in range(nk)]
            for cp in copies:
                cp.wait()

        pltpu.sync_copy(idx_v, done_hbm.at[wid])

    return scatter(rows, pos3)


def _padfill_body(cnt_ref, pst_ref, pcn_ref, xs_in, xs_out, zbuf, zsem):
    del xs_in
    zbuf[...] = jnp.zeros_like(zbuf)

    def pad_runs(e, act):
        pad = pcn_ref[e] - cnt_ref[e]
        base = pst_ref[e] + cnt_ref[e]
        for b in range(ROW_BLOCK.bit_length() - 1):
            n = 1 << b

            @pl.when(((pad >> b) & 1) == 1)
            def _():
                off = base + (pad & (n - 1))
                act(pltpu.make_async_copy(zbuf.at[pl.ds(0, QUAD * n)],
                                          xs_out.at[pl.ds(QUAD * off, QUAD * n)], zsem))

    def start_all(e, c):
        pad_runs(e, lambda d: d.start())
        return c

    def wait_all(e, c):
        pad_runs(e, lambda d: d.wait())
        return c

    lax.fori_loop(0, N_EXPERTS, start_all, 0)
    lax.fori_loop(0, N_EXPERTS, wait_all, 0)


def _padfill(counts, pstarts, pcounts, xs):
    grid_spec = pltpu.PrefetchScalarGridSpec(
        num_scalar_prefetch=3,
        grid=(1,),
        in_specs=[pl.BlockSpec(memory_space=pl.ANY)],
        out_specs=pl.BlockSpec(memory_space=pl.ANY),
        scratch_shapes=[pltpu.VMEM((QUAD * ROW_BLOCK // 2, 128), jnp.uint32), pltpu.SemaphoreType.DMA],
    )
    return pl.pallas_call(
        _padfill_body,
        grid_spec=grid_spec,
        out_shape=jax.ShapeDtypeStruct(xs.shape, xs.dtype),
        input_output_aliases={3: 0},
        compiler_params=_cparams("arbitrary"),
        name="padfill",
    )(counts, pstarts, pcounts, xs)


def _combine_stream_body(gate_ref, h1_ref, yg_ref, wsg_ref, wsu_ref, wsd_ref, g_ref, b_ref, out_ref, *, tt):
    x = h1_ref[...]
    xb = x.astype(_BF16)
    shared = _mm(_silu(_mm(xb, wsg_ref[...])) * _mm(xb, wsu_ref[...]), wsd_ref[...])
    gcol = gate_ref[...].T
    acc_lo = jnp.zeros((tt, HALF), _F32)
    acc_hi = jnp.zeros((tt, HALF), _F32)
    for k in range(TOP_K):
        lo, hi = _unpack_halves(_load_rows(yg_ref.at[0, k], tt))
        acc_lo = acc_lo + gcol[:, k:k + 1] * lo
        acc_hi = acc_hi + gcol[:, k:k + 1] * hi
    routed = jnp.concatenate([acc_lo, acc_hi], axis=1)
    out_ref[...] = _layer_norm(DN_ALPHA * x + (routed + shared), g_ref[...], b_ref[...])


def _combine_stream(gate, h1, yg, wsg, wsu, wsd, g, b, out_prev, *, tt, tile0, t_all):
    t = gate.shape[1]
    full = lambda a: pl.BlockSpec(a.shape, lambda i: (0,) * a.ndim)
    in_specs = [pl.BlockSpec((TOP_K, tt), lambda i: (0, i)), pl.BlockSpec((tt, D_MODEL), lambda i: (i, 0)),
                pl.BlockSpec((1, TOP_K, tt * QUAD, 128), lambda i: (i, 0, 0, 0)),
                full(wsg), full(wsu), full(wsd), full(g), full(b)]
    args = [gate, h1, yg, wsg, wsu, wsd, g, b]
    aliases = {}
    body = functools.partial(_combine_stream_body, tt=tt)
    if out_prev is not None:
        in_specs.append(pl.BlockSpec(memory_space=pl.ANY))
        args.append(out_prev)
        aliases = {len(args) - 1: 0}
        body = lambda *refs: _combine_stream_body(*refs[:8], refs[9], tt=tt)
    return pl.pallas_call(
        body,
        grid=(t // tt,),
        in_specs=in_specs,
        out_specs=pl.BlockSpec((tt, D_MODEL), lambda i: (i + tile0, 0)),
        out_shape=jax.ShapeDtypeStruct((t_all, D_MODEL), _F32),
        input_output_aliases=aliases,
        compiler_params=_cparams("arbitrary"),
        name="combine",
    )(*args)


def _pick(n, pref):
    t = min(n, pref)
    while n % t:
        t -= CHUNK
    return t


def _delta(pre, s0, gnw, after, *, lg, nbb):
    yc, q, k, v, z, bgc, bgr, tails_out = pre
    bsz, seq, _ = q.shape
    nch = seq // CHUNK
    grow = bgr[:, GDN_HEADS:2 * GDN_HEADS, :].reshape(bsz, GDN_HEADS, nch, CHUNK)
    grow = grow.transpose(0, 2, 1, 3).reshape(bsz, nch, 1, STACK)
    yg, s_out = _gdn(q, k, v, z, bgc, grow, s0, gnw, after, lg=lg, nbb=nbb)
    return yc, yg, tails_out, s_out


def kernel(x, meta_tokens, w_in, conv_w, conv_norm_w, gdn_conv_w, a_log, dt_bias, gdn_norm_w, w_out,
           ln1_g, ln1_b, w_router, b_router, w_gate, w_up, w_down, ws_gate, ws_up, ws_down, ln2_g, ln2_b):
    assert w_in.shape[0] == 1, "single-layer stack"
    bsz, seq, d = x.shape
    assert d == D_MODEL and seq % CHUNK == 0
    c, gw = CONV_WIDTH, GDN_WIDTH
    win = w_in[0].astype(_BF16)
    wbd = win[:, 3 * c + 4 * gw:]
    zpad = jnp.zeros((128 - 2 * GDN_HEADS,), _F32)
    zpad4 = jnp.zeros((GDN_HEADS,), _F32)
    prow = jnp.zeros((8, 128), _F32)
    prow = prow.at[0].set(jnp.concatenate([zpad4, a_log[0], zpad]))
    prow = prow.at[1].set(jnp.concatenate([zpad4, dt_bias[0], zpad]))
    wts = (win[:, :3 * c], win[:, 3 * c:3 * c + 3 * gw], win[:, 3 * c + 3 * gw:3 * c + 4 * gw],
           jnp.pad(wbd, ((0, 0), (0, 128 - 2 * GDN_HEADS))), wbd.T,
           conv_w[0], conv_norm_w, gdn_conv_w[0], prow, prow.T[:8])
    gnw = gdn_norm_w

    meta = jnp.concatenate([jnp.zeros((CHUNK - N_META, d), x.dtype), meta_tokens.astype(x.dtype)])[None]
    tails0 = jnp.zeros((HIST, c + 3 * gw), _F32)
    s00 = jnp.zeros((GDN_HEADS, GDN_HEAD_DIM, GDN_HEAD_DIM), _F32)
    _, _, tails_m, s_m = _delta(_premix(meta, tails0, wts, lt=CHUNK, b0=0, bsz=1), s00, gnw, s00, lg=CHUNK, nbb=1)

    t = bsz * seq
    tm = _pick(seq, 512)
    tt = _pick(seq, 256)
    wo = w_out[0].astype(_BF16)
    wr_t = w_router[0].T
    wr_hi = wr_t.astype(_BF16)
    wr_lo = (wr_t - wr_hi.astype(_F32)).astype(_BF16)
    shared_w = (ws_gate[0].astype(_BF16), ws_up[0].astype(_BF16), ws_down[0].astype(_BF16))
    x2d = x.reshape(t, d)

    parts = LAYER_PARTS if bsz % LAYER_PARTS == 0 and (bsz // LAYER_PARTS * seq) % (tt * SC_WORKERS) == 0 else 1
    bp = bsz // parts
    tp = bp * seq
    nb = tp * TOP_K // ROW_BLOCK + N_EXPERTS
    nbb = GDN_ROWS if bp % GDN_ROWS == 0 else 1

    def premix(part):
        return _premix(x, tails_m[0], wts, lt=_pick(seq, 512), b0=part * bp, bsz=bp)

    def delta(part, pre, after):
        yc, yg, _, _ = _delta(pre, s_m[0], gnw, after, lg=_pick(seq, 512), nbb=nbb)
        return _outproj(yc.reshape(tp, c), yg.reshape(tp, gw), x2d, wo, ln1_g, ln1_b, tm=tm, tile0=part * (tp // tm))

    def route(h1, h1p, after):
        idx, gate, rank, cnt = _router(h1, wr_hi, wr_lo, b_router[0][:, None], tt=tt, tile0=0, t=tp)
        counts = cnt[:, 0].astype(_I32)
        pcounts = (counts + ROW_BLOCK - 1) // ROW_BLOCK * ROW_BLOCK
        pends = jnp.cumsum(pcounts)
        pstarts = (pends - pcounts).astype(_I32)
        tiled = lambda a: a.reshape(TOP_K, tp // tt, tt).transpose(1, 0, 2).reshape(tp * TOP_K)
        pos = _sc_position(tiled(idx), tiled(rank), pstarts, after)
        pos = pos.reshape(tp // tt, TOP_K, tt)
        nwin = tt // SC_WINDOW
        pos3 = pos.reshape(tp // tt, TOP_K, nwin, SC_WINDOW).transpose(0, 2, 1, 3)
        pos3 = pos3.reshape(tp // SC_WINDOW, TOP_K, SC_WINDOW)
        xs, done = _sc_scatter(h1p.reshape(tp, QUAD, 128), pos3, nb * ROW_BLOCK, 0)
        xs = _padfill(counts, pstarts, pcounts.astype(_I32), xs.reshape(nb * ROW_BLOCK * QUAD, 128))
        blocks = ((pstarts // ROW_BLOCK).astype(_I32), (pcounts // ROW_BLOCK).astype(_I32),
                  (pends[-1:] // ROW_BLOCK).astype(_I32))
        return xs, blocks, pos, gate, done

    def experts(xs, blocks, pos, after_a, after_b):
        ys = _ffn(*blocks, xs, w_gate[0], w_up[0], w_down[0], after_a, after_b)
        yg = _sc_gather(ys.reshape(nb * ROW_BLOCK, QUAD, 128), pos.reshape(tp * TOP_K))
        return yg.reshape(tp // tt, TOP_K, tt * QUAD, 128), ys

    def combine(part, gate, h1, yg, out):
        return _combine_stream(gate, h1, yg, *shared_w, ln2_g, ln2_b, out, tt=tt, tile0=part * (tp // tt), t_all=t)

    out = None
    done = jnp.zeros((SC_WORKERS, TOP_K, SC_WINDOW), _I32)
    h1, h1p = delta(0, premix(0), s00)
    routed = route(h1, h1p, done)
    for part in range(parts):
        xs, blocks, pos, gate, done = routed
        h1_cur = h1
        last = part + 1 == parts
        pre = None if last else premix(part + 1)
        yg, ys = experts(xs, blocks, pos, done if last else pre[0], done if out is None else out)
        if not last:
            h1, h1p = delta(part + 1, pre, ys)
            routed = route(h1, h1p, yg)
        out = combine(part, gate, h1_cur, yg, out)
    return out.reshape(bsz, seq, d)
```

```python
import functools

import jax
import jax.numpy as jnp
from jax import lax
from jax.experimental import pallas as pl
from jax.experimental.pallas import tpu as pltpu
from jax.experimental.pallas import tpu_sc as plsc

_F32 = jnp.float32
_BF16 = jnp.bfloat16
_I32 = jnp.int32

D_MODEL = 1024
N_META = 16
CONV_WIDTH = 512
CONV_K = 3
GDN_HEADS = 4
GDN_HEAD_DIM = 128
GDN_WIDTH = GDN_HEADS * GDN_HEAD_DIM
GDN_CONV_K = 4
CHUNK = 64
N_EXPERTS = 256
TOP_K = 8
N_GROUPS = 8
TOPK_GROUPS = 4
E_PER_GROUP = N_EXPERTS // N_GROUPS
EXPERT_FF = 256
ROUTED_SCALE = 2.5
ROW_BLOCK = 256
DN_ALPHA = 2.0 ** 0.25
NORM_EPS = 1e-5
HALF = D_MODEL // 2
QUAD = HALF // 128
STACK = GDN_HEADS * CHUNK
HIST = 8
GDN_ROWS = 4
PREMIX_SUB = 2
SC_CORES = 2
SC_SUBCORES = 16
SC_WORKERS = SC_CORES * SC_SUBCORES
SC_LANES = 16
SC_CHUNK = 64
SC_RING = 2
SC_WINDOW = 128
LAYER_SPLIT = (5, 3)
RING = 8
IN_AHEAD = RING - 2

V7X_VMEM_BYTES = 64 * 1024 * 1024
VMEM_LIMIT = V7X_VMEM_BYTES - 8 * 1024 * 1024


def _cparams(*sem):
    return pltpu.CompilerParams(dimension_semantics=sem, vmem_limit_bytes=VMEM_LIMIT)


def _mm(a, b):
    return jnp.dot(a.astype(_BF16), b.astype(_BF16), preferred_element_type=_F32)


def _mm_nt(a, b):
    return lax.dot_general(a.astype(_BF16), b.astype(_BF16), (((1,), (1,)), ((), ())),
                           preferred_element_type=_F32)


def _mm_tn(a, b):
    return lax.dot_general(a.astype(_BF16), b.astype(_BF16), (((0,), (0,)), ((), ())),
                           preferred_element_type=_F32)


def _sigmoid(x):
    return 1.0 / (1.0 + jnp.exp(-x))


def _silu(x):
    return x * _sigmoid(x)


def _softplus(x):
    return jnp.maximum(x, 0.0) + jnp.log1p(jnp.exp(-jnp.abs(x)))


def _pack_halves(y):
    return pltpu.pack_elementwise([y[:, :HALF], y[:, HALF:]], packed_dtype=_BF16)


def _store_rows(ref, packed):
    r = packed.shape[0]
    for c in range(QUAD):
        ref[pl.ds(c, r, stride=QUAD), :] = packed[:, c * 128:(c + 1) * 128]


def _load_rows(ref, r):
    return jnp.concatenate([ref[pl.ds(c, r, stride=QUAD), :] for c in range(QUAD)], axis=1)


def _unpack_halves(p):
    lo = pltpu.unpack_elementwise(p, index=0, packed_dtype=_BF16, unpacked_dtype=_F32)
    hi = pltpu.unpack_elementwise(p, index=1, packed_dtype=_BF16, unpacked_dtype=_F32)
    return lo, hi


def _layer_norm(h, g, b):
    mu = jnp.mean(h, axis=-1, keepdims=True)
    d = h - mu
    var = jnp.mean(d * d, axis=-1, keepdims=True)
    return d * lax.rsqrt(var + NORM_EPS) * g + b


def _premix_body(x_ref, tails_ref, wa_ref, wq_ref, wz_ref, wbd_ref, wbdt_ref, cw_ref, cnw_ref,
                 gcw_ref, prow_ref, pcol_ref,
                 yc_ref, q_ref, k_ref, v_ref, z_ref, bgc_ref, bgr_ref, tout_ref, ext_ref, *, lt):
    cw_ = CONV_WIDTH

    @pl.when(pl.program_id(1) == 0)
    def _():
        ext_ref[0:HIST, :] = tails_ref[...]

    cw = cw_ref[...]
    gcw = gcw_ref[...]
    prow = prow_ref[...]
    pcol = pcol_ref[...]

    def sub_tile(r0, n):
        rows = slice(r0, r0 + n)
        erows = slice(HIST + r0, HIST + r0 + n)
        xb = x_ref[0, rows, :].astype(_BF16)
        pa = jnp.dot(xb, wa_ref[...], preferred_element_type=_F32)
        yield
        gate_b = pa[:, 0:cw_]
        u = pa[:, cw_:2 * cw_] * pa[:, 2 * cw_:3 * cw_]
        ext_ref[erows, 0:cw_] = u
        pq = jnp.dot(xb, wq_ref[...], preferred_element_type=_F32)
        yield
        ext_ref[erows, cw_:] = pq
        zz = jnp.dot(xb, wz_ref[...], preferred_element_type=_F32)
        bdc = jnp.dot(xb, wbd_ref[...], preferred_element_type=_F32)
        bdr = _mm_nt(wbdt_ref[...], xb)
        yield

        ca = u * cw[CONV_K - 1:CONV_K, :]
        for j in range(CONV_K - 1):
            ca = ca + ext_ref[pl.ds(HIST + r0 - (CONV_K - 1) + j, n), 0:cw_] * cw[j:j + 1, :]
        yc = gate_b * ca
        ms = jnp.mean(yc * yc, axis=-1, keepdims=True)
        yc_ref[0, rows, :] = (yc * lax.rsqrt(ms + NORM_EPS) * cnw_ref[...]).astype(_BF16)

        cq = pq * gcw[GDN_CONV_K - 1:GDN_CONV_K, :]
        for j in range(GDN_CONV_K - 1):
            cq = cq + ext_ref[pl.ds(HIST + r0 - (GDN_CONV_K - 1) + j, n), cw_:] * gcw[j:j + 1, :]
        s = _silu(cq)
        for h in range(GDN_HEADS):
            lo, hi = h * GDN_HEAD_DIM, (h + 1) * GDN_HEAD_DIM
            qh = s[:, lo:hi]
            kh = s[:, GDN_WIDTH + lo:GDN_WIDTH + hi]
            qn = qh * lax.rsqrt(jnp.sum(qh * qh, axis=-1, keepdims=True) + 1e-6)
            kn = kh * lax.rsqrt(jnp.sum(kh * kh, axis=-1, keepdims=True) + 1e-6)
            q_ref[0, rows, lo:hi] = (qn * (GDN_HEAD_DIM ** -0.5)).astype(_BF16)
            k_ref[0, rows, lo:hi] = kn.astype(_BF16)
        v_ref[0, rows, :] = s[:, 2 * GDN_WIDTH:].astype(_BF16)
        z_ref[0, rows, :] = zz.astype(_BF16)

        g_c = -jnp.exp(prow[0:1, :]) * _softplus(bdc + prow[1:2, :])
        lane = lax.broadcasted_iota(_I32, bdc.shape, 1)
        bgc_ref[0, rows, :] = jnp.where(lane < GDN_HEADS, _sigmoid(bdc), g_c)
        g_r = -jnp.exp(pcol[:, 0:1]) * _softplus(bdr + pcol[:, 1:2])
        row = lax.broadcasted_iota(_I32, bdr.shape, 0)
        bgr_ref[0, :, rows] = jnp.where(row < GDN_HEADS, _sigmoid(bdr), g_r)

    n_sub = PREMIX_SUB if lt % (PREMIX_SUB * 128) == 0 else 1
    live = [sub_tile(i * (lt // n_sub), lt // n_sub) for i in range(n_sub)]
    while live:
        live = [g for g in live if next(g, live) is not live]

    tail = ext_ref[lt:lt + HIST, :]
    ext_ref[0:HIST, :] = tail
    tout_ref[0] = tail


def _premix(x, tails, wts, *, lt, b0, bsz):
    _, seq, d = x.shape
    assert seq % lt == 0
    grid = (bsz, seq // lt)
    full = lambda a: pl.BlockSpec(a.shape, lambda b, j: (0,) * a.ndim)
    tok = lambda w: pl.BlockSpec((1, lt, w), lambda b, j: (b, j, 0))
    x_spec = pl.BlockSpec((1, lt, d), lambda b, j: (b + b0, j, 0))
    (wa, wq, wz, wbd, wbdt, cw, cnw, gcw, prow, pcol) = wts
    ext_w = CONV_WIDTH + 3 * GDN_WIDTH
    out_shape = (
        jax.ShapeDtypeStruct((bsz, seq, CONV_WIDTH), _BF16),
        jax.ShapeDtypeStruct((bsz, seq, GDN_WIDTH), _BF16),
        jax.ShapeDtypeStruct((bsz, seq, GDN_WIDTH), _BF16),
        jax.ShapeDtypeStruct((bsz, seq, GDN_WIDTH), _BF16),
        jax.ShapeDtypeStruct((bsz, seq, GDN_WIDTH), _BF16),
        jax.ShapeDtypeStruct((bsz, seq, 128), _F32),
        jax.ShapeDtypeStruct((bsz, 8, seq), _F32),
        jax.ShapeDtypeStruct((bsz, HIST, ext_w), _F32),
    )
    out_specs = (tok(CONV_WIDTH), tok(GDN_WIDTH), tok(GDN_WIDTH), tok(GDN_WIDTH), tok(GDN_WIDTH),
                 tok(128), pl.BlockSpec((1, 8, lt), lambda b, j: (b, 0, j)),
                 pl.BlockSpec((1, HIST, ext_w), lambda b, j: (b, 0, 0)))
    return pl.pallas_call(
        functools.partial(_premix_body, lt=lt),
        grid=grid,
        in_specs=[x_spec, full(tails)] + [full(w) for w in wts],
        out_specs=out_specs,
        out_shape=out_shape,
        scratch_shapes=[pltpu.VMEM((HIST + lt, ext_w), _F32)],
        compiler_params=_cparams("arbitrary", "arbitrary"),
        name="premix",
    )(x, tails, *wts)


def _cumsum_rows(x):
    row = lax.broadcasted_iota(_I32, x.shape, 0)
    s = 1
    while s < x.shape[0]:
        x = x + jnp.where(row >= s, pltpu.roll(x, s, 0), 0.0)
        s *= 2
    return x


def _cumsum_lanes_seg(x):
    lane = lax.broadcasted_iota(_I32, x.shape, 1) & (CHUNK - 1)
    s = 1
    while s < CHUNK:
        x = x + jnp.where(lane >= s, pltpu.roll(x, s, 1), 0.0)
        s *= 2
    return x


def _stack_heads(a):
    return jnp.concatenate([a[:, h * GDN_HEAD_DIM:(h + 1) * GDN_HEAD_DIM] for h in range(GDN_HEADS)], axis=0)


def _gdn_body(q_ref, k_ref, v_ref, z_ref, bgc_ref, grow_ref, s0_ref, gnw_ref, after_ref,
              y_ref, sout_ref, s_ref, *, nc, nbb):
    del after_ref

    @pl.when(pl.program_id(1) == 0)
    def _():
        for r in range(nbb):
            s_ref[r] = s0_ref[...]

    ri = lax.broadcasted_iota(_I32, (STACK, STACK), 0)
    ci = lax.broadcasted_iota(_I32, (STACK, STACK), 1)
    same64 = (ri >> 6) == (ci >> 6)
    same32 = (ri >> 5) == (ci >> 5)
    same16 = (ri >> 4) == (ci >> 4)
    low_incl = same64 & (ri >= ci)
    low_strict = same64 & (ri > ci)
    gnw = gnw_ref[...]

    def chunk_row(r, c):
        off = pl.multiple_of(c * CHUNK, CHUNK)
        q_all = _stack_heads(q_ref[r, pl.ds(off, CHUNK), :].astype(_F32))
        k_all = _stack_heads(k_ref[r, pl.ds(off, CHUNK), :].astype(_F32))
        v_all = _stack_heads(v_ref[r, pl.ds(off, CHUNK), :].astype(_F32))
        bgc = bgc_ref[r, pl.ds(off, CHUNK), :]
        gcs = _cumsum_rows(bgc)
        hd = (CHUNK, GDN_HEAD_DIM)
        beta_b = jnp.concatenate(
            [jnp.broadcast_to(bgc[:, h:h + 1], hd) for h in range(GDN_HEADS)], axis=0)
        gc_b = jnp.concatenate(
            [jnp.broadcast_to(gcs[:, GDN_HEADS + h:GDN_HEADS + h + 1], hd) for h in range(GDN_HEADS)], axis=0)
        gl = [gcs[CHUNK - 1:CHUNK, GDN_HEADS + h:GDN_HEADS + h + 1] for h in range(GDN_HEADS)]
        gl_b = jnp.concatenate([jnp.broadcast_to(g1, hd) for g1 in gl], axis=0)
        gcr = _cumsum_lanes_seg(jnp.broadcast_to(grow_ref[r, c], (8, STACK)))[0:1, :]

        diff = jnp.concatenate([gc_b, gc_b], axis=1) - gcr
        decay = jnp.exp(jnp.where(low_incl, diff, -1e30))
        kb = k_all * beta_b
        a1 = _mm_nt(jnp.concatenate([kb, q_all], axis=0), k_all)
        yield
        m = jnp.where(low_strict, a1[:STACK] * decay, 0.0)
        attn = a1[STACK:] * decay

        l16 = jnp.where(same16, m, 0.0)
        c1 = jnp.where(same32 & jnp.logical_not(same16), m, 0.0)
        c2 = jnp.where(same32, 0.0, m)
        p2 = _mm(l16, l16)
        yield
        p4 = _mm(p2, p2)
        t = _mm(l16, p2)
        yield
        na = p2 - l16 - t
        p8 = _mm(p4, p4)
        t = _mm(na, p4)
        yield
        nb = na + p4 + t
        t = _mm(nb, p8)
        yield
        ncm = nb + p8 + t
        t = _mm(c1, ncm)
        yield
        y1 = c1 + t
        t = _mm(ncm, y1)
        yield
        n1 = ncm - y1 - t
        t = _mm(c2, n1)
        yield
        y2 = c2 + t
        t = _mm(n1, y2)
        yield
        nt = n1 - y2 - t

        egc = jnp.exp(gc_b)
        rhs = jnp.concatenate([v_all * beta_b, kb * egc], axis=1)
        t = _mm(nt, rhs)
        yield
        uw = rhs + t
        u_all = uw[:, :GDN_HEAD_DIM]
        w_all = uw[:, GDN_HEAD_DIM:]
        qd = q_all * egc
        kd = k_all * jnp.exp(gl_b - gc_b)

        bs = []
        for h in range(GDN_HEADS):
            r0, r1 = h * CHUNK, (h + 1) * CHUNK
            bs.append(_mm(jnp.concatenate([w_all[r0:r1], qd[r0:r1]], axis=0), s_ref[r, h]))
        yield
        vn = [u_all[h * CHUNK:(h + 1) * CHUNK] - bs[h][:CHUNK] for h in range(GDN_HEADS)]
        vn_all = jnp.concatenate(vn, axis=0)
        t = _mm(attn, vn_all)
        ds = [_mm_tn(kd[h * CHUNK:(h + 1) * CHUNK], vn[h]) for h in range(GDN_HEADS)]
        yield
        o_all = jnp.concatenate([b[CHUNK:] for b in bs], axis=0) + t
        for h in range(GDN_HEADS):
            r0, r1 = h * CHUNK, (h + 1) * CHUNK
            s_ref[r, h] = s_ref[r, h] * jnp.exp(gl[h]) + ds[h]
            o = o_all[r0:r1]
            zz = z_ref[r, pl.ds(off, CHUNK), h * GDN_HEAD_DIM:(h + 1) * GDN_HEAD_DIM].astype(_F32)
            on = o * lax.rsqrt(jnp.mean(o * o, axis=-1, keepdims=True) + NORM_EPS) * gnw
            y_ref[r, pl.ds(off, CHUNK), h * GDN_HEAD_DIM:(h + 1) * GDN_HEAD_DIM] = (on * _silu(zz)).astype(_BF16)

    def chunk(c, carry):
        live = [chunk_row(r, c) for r in range(nbb)]
        while live:
            live = [g for g in live if next(g, live) is not live]
        return carry

    lax.fori_loop(0, nc, chunk, 0)
    sout_ref[...] = s_ref[...]


def _gdn(q, k, v, z, bgc, grow, s0, gnw, after, *, lg, nbb):
    bsz, seq, _ = q.shape
    assert seq % lg == 0 and lg % CHUNK == 0 and bsz % nbb == 0
    nc = lg // CHUNK
    tok = lambda w: pl.BlockSpec((nbb, lg, w), lambda b, j: (b, j, 0))
    full = lambda a: pl.BlockSpec(a.shape, lambda b, j: (0,) * a.ndim)
    st = (nbb, GDN_HEADS, GDN_HEAD_DIM, GDN_HEAD_DIM)
    return pl.pallas_call(
        functools.partial(_gdn_body, nc=nc, nbb=nbb),
        grid=(bsz // nbb, seq // lg),
        in_specs=[tok(GDN_WIDTH)] * 4 + [tok(128), pl.BlockSpec((nbb, nc, 1, STACK), lambda b, j: (b, j, 0, 0)),
                                           full(s0), full(gnw), pl.BlockSpec(memory_space=pl.ANY)],
        out_specs=(tok(GDN_WIDTH), pl.BlockSpec(st, lambda b, j: (b, 0, 0, 0))),
        out_shape=(jax.ShapeDtypeStruct((bsz, seq, GDN_WIDTH), _BF16),
                   jax.ShapeDtypeStruct((bsz, GDN_HEADS, GDN_HEAD_DIM, GDN_HEAD_DIM), _F32)),
        scratch_shapes=[pltpu.VMEM(st, _F32)],
        compiler_params=_cparams("arbitrary", "arbitrary"),
        name="gdn",
    )(q, k, v, z, bgc, grow, s0, gnw, after)


def _outproj_body(yc_ref, yg_ref, x_ref, wo_ref, g_ref, b_ref, h1_ref, h1p_ref):
    mix = (jnp.dot(yc_ref[...], wo_ref[0:CONV_WIDTH, :], preferred_element_type=_F32)
           + jnp.dot(yg_ref[...], wo_ref[CONV_WIDTH:, :], preferred_element_type=_F32))
    h1 = _layer_norm(DN_ALPHA * x_ref[...] + mix, g_ref[...], b_ref[...])
    h1_ref[...] = h1
    _store_rows(h1p_ref, _pack_halves(h1))


def _outproj(yc, yg, x2d, wo, g, b, *, tm, tile0):
    t = yc.shape[0]
    assert t % tm == 0
    row = lambda w: pl.BlockSpec((tm, w), lambda i: (i, 0))
    full = lambda a: pl.BlockSpec(a.shape, lambda i: (0,) * a.ndim)
    return pl.pallas_call(
        _outproj_body,
        grid=(t // tm,),
        in_specs=[row(CONV_WIDTH), row(GDN_WIDTH), pl.BlockSpec((tm, D_MODEL), lambda i: (i + tile0, 0)),
                  full(wo), full(g), full(b)],
        out_specs=(row(D_MODEL), pl.BlockSpec((tm * QUAD, 128), lambda i: (i, 0))),
        out_shape=(jax.ShapeDtypeStruct((t, D_MODEL), _F32), jax.ShapeDtypeStruct((t * QUAD, 128), jnp.uint32)),
        compiler_params=_cparams("arbitrary"),
        name="outproj",
    )(yc, yg, x2d, wo, g, b)


def _router_body(h1_ref, wh_ref, wl_ref, br_ref, idx_ref, gate_ref, rank_ref, cnt_ref, carry_ref, *, tt):
    @pl.when(pl.program_id(0) == 0)
    def _():
        carry_ref[...] = jnp.zeros_like(carry_ref)

    x = h1_ref[...]
    xh = x.astype(_BF16)
    xl = (x - xh.astype(_F32)).astype(_BF16)
    wh = wh_ref[...]
    logits = _mm_nt(wh, xh) + _mm_nt(wh, xl) + _mm_nt(wl_ref[...], xh)
    scores = _sigmoid(logits)
    sel = scores + br_ref[...]
    ninf = -jnp.inf

    r32 = lax.broadcasted_iota(_I32, (E_PER_GROUP, tt), 0)
    gsc = []
    for g in range(N_GROUPS):
        xg = sel[g * E_PER_GROUP:(g + 1) * E_PER_GROUP]
        m1 = jnp.max(xg, axis=0, keepdims=True)
        i1 = jnp.min(jnp.where(xg == m1, r32, E_PER_GROUP), axis=0, keepdims=True)
        m2 = jnp.max(jnp.where(r32 == i1, ninf, xg), axis=0, keepdims=True)
        gsc.append(m1 + m2)
    work = jnp.concatenate(gsc, axis=0)
    r8 = lax.broadcasted_iota(_I32, (N_GROUPS, tt), 0)
    gkeep = jnp.zeros((N_GROUPS, tt), _F32)
    for _ in range(TOPK_GROUPS):
        m = jnp.max(work, axis=0, keepdims=True)
        gi = jnp.min(jnp.where(work == m, r8, N_GROUPS), axis=0, keepdims=True)
        pick = r8 == gi
        gkeep = jnp.where(pick, 1.0, gkeep)
        work = jnp.where(pick, ninf, work)
    selm = jnp.concatenate(
        [jnp.where(gkeep[g:g + 1] > 0.5, sel[g * E_PER_GROUP:(g + 1) * E_PER_GROUP], ninf)
         for g in range(N_GROUPS)], axis=0)

    re = lax.broadcasted_iota(_I32, (N_EXPERTS, tt), 0)
    msel = jnp.zeros((N_EXPERTS, tt), _F32)
    idxs, gates = [], []
    for _ in range(TOP_K):
        m = jnp.max(selm, axis=0, keepdims=True)
        ii = jnp.min(jnp.where(selm == m, re, N_EXPERTS), axis=0, keepdims=True)
        hit = re == ii
        idxs.append(ii)
        gates.append(jnp.sum(jnp.where(hit, scores, 0.0), axis=0, keepdims=True))
        selm = jnp.where(hit, ninf, selm)
        msel = jnp.where(hit, 1.0, msel)
    gate = jnp.concatenate(gates, axis=0)
    gate_ref[...] = gate / jnp.sum(gate, axis=0, keepdims=True) * ROUTED_SCALE
    idx_ref[...] = jnp.concatenate(idxs, axis=0)

    ta = lax.broadcasted_iota(_I32, (tt, tt), 0)
    tb = lax.broadcasted_iota(_I32, (tt, tt), 1)
    earlier = jnp.where(ta < tb, 1.0, 0.0)
    carry = carry_ref[...]
    rank_all = _mm(msel, earlier) + carry[:, 0:1]
    rank_ref[...] = jnp.concatenate(
        [jnp.sum(jnp.where(re == ii, rank_all, 0.0), axis=0, keepdims=True) for ii in idxs],
        axis=0).astype(_I32)
    carry = carry + jnp.sum(msel, axis=1, keepdims=True)
    carry_ref[...] = carry
    cnt_ref[...] = carry


def _router(h1, wh, wl, br, *, tt, tile0, t):
    assert t % tt == 0
    full = lambda a: pl.BlockSpec(a.shape, lambda i: (0,) * a.ndim)
    kt = pl.BlockSpec((TOP_K, tt), lambda i: (0, i))
    return pl.pallas_call(
        functools.partial(_router_body, tt=tt),
        grid=(t // tt,),
        in_specs=[pl.BlockSpec((tt, D_MODEL), lambda i: (i + tile0, 0)), full(wh), full(wl), full(br)],
        out_specs=(kt, kt, kt, pl.BlockSpec((N_EXPERTS, 128), lambda i: (0, 0))),
        out_shape=(jax.ShapeDtypeStruct((TOP_K, t), _I32), jax.ShapeDtypeStruct((TOP_K, t), _F32),
                   jax.ShapeDtypeStruct((TOP_K, t), _I32), jax.ShapeDtypeStruct((N_EXPERTS, 128), _F32)),
        scratch_shapes=[pltpu.VMEM((N_EXPERTS, 128), _F32)],
        compiler_params=_cparams("arbitrary"),
        name="router",
    )(h1, wh, wl, br)


def _ffn_body(blk0_ref, nblk_ref, ntot_ref, xs_hbm, wg_ref, wu_ref, wd_ref, after_a, after_b, ys_hbm,
              xbuf, ybuf, sem_in, sem_out, wgu_bf, wd_bf):
    del after_a, after_b
    e = pl.program_id(0)
    nblk = nblk_ref[e]
    blk0 = blk0_ref[e]
    ntot = ntot_ref[0]

    blk_rows = ROW_BLOCK * QUAD

    def rows(g):
        return pl.ds(pl.multiple_of(g * blk_rows, blk_rows), blk_rows)

    def in_start(g, slot):
        pltpu.make_async_copy(xs_hbm.at[rows(g)], xbuf.at[slot], sem_in.at[slot]).start()

    def in_wait(slot):
        pltpu.make_async_copy(xs_hbm.at[rows(0)], xbuf.at[slot], sem_in.at[slot]).wait()

    def out_start(g, slot):
        pltpu.make_async_copy(ybuf.at[slot], ys_hbm.at[rows(g)], sem_out.at[slot]).start()

    def out_wait(slot):
        pltpu.make_async_copy(ybuf.at[slot], ys_hbm.at[rows(0)], sem_out.at[slot]).wait()

    @pl.when(e == 0)
    def _():
        for i in range(IN_AHEAD):
            @pl.when(i < ntot)
            def _():
                in_start(i, i)

    @pl.when(nblk > 0)
    def _():
        wgu_bf[:, 0:EXPERT_FF] = wg_ref[0].astype(_BF16)
        wgu_bf[:, EXPERT_FF:] = wu_ref[0].astype(_BF16)
        wd_bf[...] = wd_ref[0].astype(_BF16)

        def acquire(g):
            slot = g & (RING - 1)
            in_wait(slot)

            @pl.when(g + IN_AHEAD < ntot)
            def _():
                in_start(g + IN_AHEAD, (g + IN_AHEAD) & (RING - 1))

            @pl.when(g >= RING)
            def _():
                out_wait(slot)

            return slot

        def compute(slot):
            lo, hi = _unpack_halves(_load_rows(xbuf.at[slot], ROW_BLOCK))
            a = jnp.dot(lo.astype(_BF16), wgu_bf[0:HALF, :], preferred_element_type=_F32)
            yield
            gu = a + jnp.dot(hi.astype(_BF16), wgu_bf[HALF:, :], preferred_element_type=_F32)
            yield
            h = (_silu(gu[:, :EXPERT_FF]) * gu[:, EXPERT_FF:]).astype(_BF16)
            y = jnp.dot(h, wd_bf[...], preferred_element_type=_F32)
            yield
            _store_rows(ybuf.at[slot], _pack_halves(y))

        def run(gs):
            slots = [acquire(g) for g in gs]
            live = [compute(s) for s in slots]
            while live:
                live = [c for c in live if next(c, live) is not live]
            for g, s in zip(gs, slots):
                out_start(g, s)

        def pair(j, carry):
            run([blk0 + 2 * j, blk0 + 2 * j + 1])
            return carry

        lax.fori_loop(0, nblk // 2, pair, 0)

        @pl.when((nblk & 1) == 1)
        def _():
            run([blk0 + nblk - 1])

    @pl.when(e == N_EXPERTS - 1)
    def _():
        for i in range(RING):
            @pl.when(i < ntot)
            def _():
                out_wait((ntot - 1 - i) & (RING - 1))


def _ffn(blk0, nblk, ntot, xs, wg, wu, wd, after_a, after_b):
    grid_spec = pltpu.PrefetchScalarGridSpec(
        num_scalar_prefetch=3,
        grid=(N_EXPERTS,),
        in_specs=[pl.BlockSpec(memory_space=pl.ANY),
                  pl.BlockSpec((1, D_MODEL, EXPERT_FF), lambda e, *_: (e, 0, 0)),
                  pl.BlockSpec((1, D_MODEL, EXPERT_FF), lambda e, *_: (e, 0, 0)),
                  pl.BlockSpec((1, EXPERT_FF, D_MODEL), lambda e, *_: (e, 0, 0)),
                  pl.BlockSpec(memory_space=pl.ANY), pl.BlockSpec(memory_space=pl.ANY)],
        out_specs=pl.BlockSpec(memory_space=pl.ANY),
        scratch_shapes=[pltpu.VMEM((RING, ROW_BLOCK * QUAD, 128), jnp.uint32),
                        pltpu.VMEM((RING, ROW_BLOCK * QUAD, 128), jnp.uint32),
                        pltpu.SemaphoreType.DMA((RING,)), pltpu.SemaphoreType.DMA((RING,)),
                        pltpu.VMEM((D_MODEL, 2 * EXPERT_FF), _BF16), pltpu.VMEM((EXPERT_FF, D_MODEL), _BF16)],
    )
    return pl.pallas_call(
        _ffn_body,
        grid_spec=grid_spec,
        out_shape=jax.ShapeDtypeStruct(xs.shape, jnp.uint32),
        compiler_params=_cparams("arbitrary"),
        name="ffn",
    )(blk0, nblk, ntot, xs, wg, wu, wd, after_a, after_b)


def _sc_position(idx, rank, pstart, after):
    n = idx.shape[0]
    per_w = n // SC_WORKERS
    assert per_w * SC_WORKERS == n and per_w % SC_LANES == 0
    mesh = plsc.VectorSubcoreMesh(core_axis_name="c", subcore_axis_name="s",
                                  num_cores=SC_CORES, num_subcores=SC_SUBCORES)

    @functools.partial(
        pl.kernel, mesh=mesh,
        out_type=jax.ShapeDtypeStruct((n,), _I32),
        scratch_types=[pltpu.VMEM((per_w,), _I32), pltpu.VMEM((per_w,), _I32), pltpu.VMEM((per_w,), _I32),
                       pltpu.VMEM((N_EXPERTS,), _I32)],
        compiler_params=pltpu.CompilerParams(needs_layout_passes=False),
        name="sc_position",
    )
    def position(idx_hbm, rank_hbm, ps_hbm, after_hbm, out_hbm, idx_v, rank_v, pos_v, ps_v):
        del after_hbm
        wid = lax.axis_index("s") * SC_CORES + lax.axis_index("c")
        mine = pl.ds(pl.multiple_of(wid * per_w, per_w), per_w)
        pltpu.sync_copy(ps_hbm, ps_v)
        pltpu.sync_copy(idx_hbm.at[mine], idx_v)
        pltpu.sync_copy(rank_hbm.at[mine], rank_v)

        @pl.loop(0, per_w, step=SC_LANES)
        def _(j):
            lanes = pl.ds(j, SC_LANES)
            pos_v[lanes] = plsc.load_gather(ps_v, [idx_v[lanes]]) + rank_v[lanes]

        pltpu.sync_copy(pos_v, out_hbm.at[mine])

    return position(idx, rank, pstart, after)


def _sc_gather(table, idx):
    b = idx.shape[0]
    nchunk = b // (SC_WORKERS * SC_CHUNK)
    assert nchunk * SC_WORKERS * SC_CHUNK == b and nchunk % SC_RING == 0
    idx2 = idx.reshape(SC_WORKERS * nchunk, SC_CHUNK)
    row = table.shape[1:]
    mesh = plsc.VectorSubcoreMesh(core_axis_name="c", subcore_axis_name="s",
                                  num_cores=SC_CORES, num_subcores=SC_SUBCORES)

    @functools.partial(
        pl.kernel, mesh=mesh,
        out_type=jax.ShapeDtypeStruct((b,) + row, table.dtype),
        scratch_types=[pltpu.VMEM((nchunk, SC_CHUNK), _I32), pltpu.VMEM((SC_RING, SC_CHUNK) + row, table.dtype),
                       pltpu.SemaphoreType.DMA((SC_RING,)), pltpu.SemaphoreType.DMA((SC_RING,))],
        name="sc_gather",
    )
    def gather(table_hbm, idx_hbm, out_hbm, idx_v, rows_v, sem_g, sem_w):
        wid = lax.axis_index("s") * SC_CORES + lax.axis_index("c")
        c0 = wid * nchunk
        pltpu.sync_copy(idx_hbm.at[pl.ds(pl.multiple_of(c0, nchunk), nchunk)], idx_v)

        def fetch(i, s):
            return pltpu.make_async_copy(table_hbm.at[idx_v.at[i]], rows_v.at[s], sem_g.at[s])

        def flush(i, s):
            rows = pl.ds(pl.multiple_of((c0 + i) * SC_CHUNK, SC_CHUNK), SC_CHUNK)
            return pltpu.make_async_copy(rows_v.at[s], out_hbm.at[rows], sem_w.at[s])

        for s in range(SC_RING):
            fetch(s, s).start()

        @pl.loop(0, nchunk, step=SC_RING)
        def _(g):
            for s in range(SC_RING):
                i = g + s
                fetch(i, s).wait()
                flush(i, s).start()
                flush(i, s).wait()

                @pl.when(i + SC_RING < nchunk)
                def _():
                    fetch(i + SC_RING, s).start()

    return gather(table, idx2)


def _sc_scatter(rows, pos3, n_out, row0):
    nchunk, nk, w = pos3.shape
    per_w = nchunk // SC_WORKERS
    assert per_w * SC_WORKERS == nchunk and w <= 128 and row0 % w == 0 and rows.shape[0] >= row0 + nchunk * w
    row = rows.shape[1:]
    mesh = plsc.VectorSubcoreMesh(core_axis_name="c", subcore_axis_name="s",
                                  num_cores=SC_CORES, num_subcores=SC_SUBCORES)

    @functools.partial(
        pl.kernel, mesh=mesh,
        out_type=(jax.ShapeDtypeStruct((n_out,) + row, rows.dtype),
                  jax.ShapeDtypeStruct((SC_WORKERS, nk, w), _I32)),
        scratch_types=[pltpu.VMEM((nk, w), _I32), pltpu.VMEM((w,) + row, rows.dtype), pltpu.SemaphoreType.DMA],
        name="sc_scatter",
    )
    def scatter(rows_hbm, pos_hbm, out_hbm, done_hbm, idx_v, rows_v, sem):
        wid = lax.axis_index("s") * SC_CORES + lax.axis_index("c")

        @pl.loop(0, per_w)
        def _(i):
            c = wid * per_w + i
            pltpu.sync_copy(pos_hbm.at[c], idx_v)
            pltpu.sync_copy(rows_hbm.at[pl.ds(pl.multiple_of(row0 + c * w, w), w)], rows_v)
            copies = [pltpu.async_copy(rows_v, out_hbm.at[idx_v.at[k]], sem) for k in range(nk)]
            for cp in copies:
                cp.wait()

        pltpu.sync_copy(idx_v, done_hbm.at[wid])

    return scatter(rows, pos3)


def _padfill_body(cnt_ref, pst_ref, pcn_ref, xs_in, xs_out, zbuf, zsem):
    del xs_in
    zbuf[...] = jnp.zeros_like(zbuf)

    def pad_runs(e, act):
        pad = pcn_ref[e] - cnt_ref[e]
        base = pst_ref[e] + cnt_ref[e]
        for b in range(ROW_BLOCK.bit_length() - 1):
            n = 1 << b

            @pl.when(((pad >> b) & 1) == 1)
            def _():
                off = base + (pad & (n - 1))
                act(pltpu.make_async_copy(zbuf.at[pl.ds(0, QUAD * n)],
                                          xs_out.at[pl.ds(QUAD * off, QUAD * n)], zsem))

    def start_all(e, c):
        pad_runs(e, lambda d: d.start())
        return c

    def wait_all(e, c):
        pad_runs(e, lambda d: d.wait())
        return c

    lax.fori_loop(0, N_EXPERTS, start_all, 0)
    lax.fori_loop(0, N_EXPERTS, wait_all, 0)


def _padfill(counts, pstarts, pcounts, xs):
    grid_spec = pltpu.PrefetchScalarGridSpec(
        num_scalar_prefetch=3,
        grid=(1,),
        in_specs=[pl.BlockSpec(memory_space=pl.ANY)],
        out_specs=pl.BlockSpec(memory_space=pl.ANY),
        scratch_shapes=[pltpu.VMEM((QUAD * ROW_BLOCK // 2, 128), jnp.uint32), pltpu.SemaphoreType.DMA],
    )
    return pl.pallas_call(
        _padfill_body,
        grid_spec=grid_spec,
        out_shape=jax.ShapeDtypeStruct(xs.shape, xs.dtype),
        input_output_aliases={3: 0},
        compiler_params=_cparams("arbitrary"),
        name="padfill",
    )(counts, pstarts, pcounts, xs)


def _combine_stream_body(gate_ref, h1_ref, yg_ref, wsg_ref, wsu_ref, wsd_ref, g_ref, b_ref, out_ref, *, tt):
    x = h1_ref[...]
    xb = x.astype(_BF16)
    shared = _mm(_silu(_mm(xb, wsg_ref[...])) * _mm(xb, wsu_ref[...]), wsd_ref[...])
    gcol = gate_ref[...].T
    acc_lo = jnp.zeros((tt, HALF), _F32)
    acc_hi = jnp.zeros((tt, HALF), _F32)
    for k in range(TOP_K):
        lo, hi = _unpack_halves(_load_rows(yg_ref.at[0, k], tt))
        acc_lo = acc_lo + gcol[:, k:k + 1] * lo
        acc_hi = acc_hi + gcol[:, k:k + 1] * hi
    routed = jnp.concatenate([acc_lo, acc_hi], axis=1)
    out_ref[...] = _layer_norm(DN_ALPHA * x + (routed + shared), g_ref[...], b_ref[...])


def _combine_stream(gate, h1, yg, wsg, wsu, wsd, g, b, out_prev, *, tt, tile0, t_all):
    t = gate.shape[1]
    full = lambda a: pl.BlockSpec(a.shape, lambda i: (0,) * a.ndim)
    in_specs = [pl.BlockSpec((TOP_K, tt), lambda i: (0, i)), pl.BlockSpec((tt, D_MODEL), lambda i: (i, 0)),
                pl.BlockSpec((1, TOP_K, tt * QUAD, 128), lambda i: (i, 0, 0, 0)),
                full(wsg), full(wsu), full(wsd), full(g), full(b)]
    args = [gate, h1, yg, wsg, wsu, wsd, g, b]
    aliases = {}
    body = functools.partial(_combine_stream_body, tt=tt)
    if out_prev is not None:
        in_specs.append(pl.BlockSpec(memory_space=pl.ANY))
        args.append(out_prev)
        aliases = {len(args) - 1: 0}
        body = lambda *refs: _combine_stream_body(*refs[:8], refs[9], tt=tt)
    return pl.pallas_call(
        body,
        grid=(t // tt,),
        in_specs=in_specs,
        out_specs=pl.BlockSpec((tt, D_MODEL), lambda i: (i + tile0, 0)),
        out_shape=jax.ShapeDtypeStruct((t_all, D_MODEL), _F32),
        input_output_aliases=aliases,
        compiler_params=_cparams("arbitrary"),
        name="combine",
    )(*args)


def _pick(n, pref):
    t = min(n, pref)
    while n % t:
        t -= CHUNK
    return t


def _delta(pre, s0, gnw, after, *, lg, nbb):
    yc, q, k, v, z, bgc, bgr, tails_out = pre
    bsz, seq, _ = q.shape
    nch = seq // CHUNK
    grow = bgr[:, GDN_HEADS:2 * GDN_HEADS, :].reshape(bsz, GDN_HEADS, nch, CHUNK)
    grow = grow.transpose(0, 2, 1, 3).reshape(bsz, nch, 1, STACK)
    yg, s_out = _gdn(q, k, v, z, bgc, grow, s0, gnw, after, lg=lg, nbb=nbb)
    return yc, yg, tails_out, s_out


def kernel(x, meta_tokens, w_in, conv_w, conv_norm_w, gdn_conv_w, a_log, dt_bias, gdn_norm_w, w_out,
           ln1_g, ln1_b, w_router, b_router, w_gate, w_up, w_down, ws_gate, ws_up, ws_down, ln2_g, ln2_b):
    assert w_in.shape[0] == 1, "single-layer stack"
    bsz, seq, d = x.shape
    assert d == D_MODEL and seq % CHUNK == 0
    c, gw = CONV_WIDTH, GDN_WIDTH
    win = w_in[0].astype(_BF16)
    wbd = win[:, 3 * c + 4 * gw:]
    zpad = jnp.zeros((128 - 2 * GDN_HEADS,), _F32)
    zpad4 = jnp.zeros((GDN_HEADS,), _F32)
    prow = jnp.zeros((8, 128), _F32)
    prow = prow.at[0].set(jnp.concatenate([zpad4, a_log[0], zpad]))
    prow = prow.at[1].set(jnp.concatenate([zpad4, dt_bias[0], zpad]))
    wts = (win[:, :3 * c], win[:, 3 * c:3 * c + 3 * gw], win[:, 3 * c + 3 * gw:3 * c + 4 * gw],
           jnp.pad(wbd, ((0, 0), (0, 128 - 2 * GDN_HEADS))), wbd.T,
           conv_w[0], conv_norm_w, gdn_conv_w[0], prow, prow.T[:8])
    gnw = gdn_norm_w

    meta = jnp.concatenate([jnp.zeros((CHUNK - N_META, d), x.dtype), meta_tokens.astype(x.dtype)])[None]
    tails0 = jnp.zeros((HIST, c + 3 * gw), _F32)
    s00 = jnp.zeros((GDN_HEADS, GDN_HEAD_DIM, GDN_HEAD_DIM), _F32)
    _, _, tails_m, s_m = _delta(_premix(meta, tails0, wts, lt=CHUNK, b0=0, bsz=1), s00, gnw, s00, lg=CHUNK, nbb=1)

    t = bsz * seq
    tm = _pick(seq, 512)
    tt = _pick(seq, 256)
    wo = w_out[0].astype(_BF16)
    wr_t = w_router[0].T
    wr_hi = wr_t.astype(_BF16)
    wr_lo = (wr_t - wr_hi.astype(_F32)).astype(_BF16)
    shared_w = (ws_gate[0].astype(_BF16), ws_up[0].astype(_BF16), ws_down[0].astype(_BF16))
    x2d = x.reshape(t, d)

    unit = sum(LAYER_SPLIT)
    sizes = [bsz * f // unit for f in LAYER_SPLIT]
    if bsz % unit or any((b * seq) % (SC_WINDOW * SC_WORKERS) or (b * seq) % tt for b in sizes):
        sizes = [bsz]
    parts = len(sizes)
    starts = [sum(sizes[:i]) for i in range(parts)]

    def premix(part):
        return _premix(x, tails_m[0], wts, lt=_pick(seq, 512), b0=starts[part], bsz=sizes[part])

    def delta(part, pre, after):
        bp = sizes[part]
        tp = bp * seq
        yc, yg, _, _ = _delta(pre, s_m[0], gnw, after, lg=_pick(seq, 512), nbb=GDN_ROWS if bp % GDN_ROWS == 0 else 1)
        return _outproj(yc.reshape(tp, c), yg.reshape(tp, gw), x2d, wo, ln1_g, ln1_b, tm=tm,
                        tile0=starts[part] * seq // tm)

    def route(h1, h1p, after):
        tp = h1.shape[0]
        nb = tp * TOP_K // ROW_BLOCK + N_EXPERTS
        idx, gate, rank, cnt = _router(h1, wr_hi, wr_lo, b_router[0][:, None], tt=tt, tile0=0, t=tp)
        counts = cnt[:, 0].astype(_I32)
        pcounts = (counts + ROW_BLOCK - 1) // ROW_BLOCK * ROW_BLOCK
        pends = jnp.cumsum(pcounts)
        pstarts = (pends - pcounts).astype(_I32)
        tiled = lambda a: a.reshape(TOP_K, tp // tt, tt).transpose(1, 0, 2).reshape(tp * TOP_K)
        pos = _sc_position(tiled(idx), tiled(rank), pstarts, after)
        pos = pos.reshape(tp // tt, TOP_K, tt)
        nwin = tt // SC_WINDOW
        pos3 = pos.reshape(tp // tt, TOP_K, nwin, SC_WINDOW).transpose(0, 2, 1, 3)
        pos3 = pos3.reshape(tp // SC_WINDOW, TOP_K, SC_WINDOW)
        xs, done = _sc_scatter(h1p.reshape(tp, QUAD, 128), pos3, nb * ROW_BLOCK, 0)
        xs = _padfill(counts, pstarts, pcounts.astype(_I32), xs.reshape(nb * ROW_BLOCK * QUAD, 128))
        blocks = ((pstarts // ROW_BLOCK).astype(_I32), (pcounts // ROW_BLOCK).astype(_I32),
                  (pends[-1:] // ROW_BLOCK).astype(_I32))
        return xs, blocks, pos, gate, done

    def experts(xs, blocks, pos, after_a, after_b):
        ys = _ffn(*blocks, xs, w_gate[0], w_up[0], w_down[0], after_a, after_b)
        yg = _sc_gather(ys.reshape(ys.shape[0] // QUAD, QUAD, 128), pos.reshape(pos.size))
        return yg.reshape(pos.shape[0], TOP_K, tt * QUAD, 128), ys

    def combine(part, gate, h1, yg, out):
        return _combine_stream(gate, h1, yg, *shared_w, ln2_g, ln2_b, out, tt=tt, tile0=starts[part] * seq // tt,
                               t_all=t)

    out = None
    done = jnp.zeros((SC_WORKERS, TOP_K, SC_WINDOW), _I32)
    h1, h1p = delta(0, premix(0), s00)
    routed = route(h1, h1p, done)
    for part in range(parts):
        xs, blocks, pos, gate, done = routed
        h1_cur = h1
        last = part + 1 == parts
        pre = None if last else premix(part + 1)
        yg, ys = experts(xs, blocks, pos, done if last else pre[0], done if out is None else out)
        if not last:
            h1, h1p = delta(part + 1, pre, ys)
            routed = route(h1, h1p, yg)
        out = combine(part, gate, h1_cur, yg, out)
    return out.reshape(bsz, seq, d)
```
